```python
import math
import jax, jax.numpy as jnp
from jax import lax
import numpy as np

D_MODEL = 2048
BATCH = 8
SEQ = 4096
DEPTH = 1

EPS = 1e-6
BLOCK_Q = 128
CONV_WIDTH = D_MODEL // 2
CONV_GROUPS = 8
CONV_TAPS = 3
FOX_HEAD_DIM = 128
FOX_HEADS = (D_MODEL // 2) // FOX_HEAD_DIM
FOX_WIDTH = FOX_HEADS * FOX_HEAD_DIM
MEM_TOKENS = 256
MEM_HEADS = 4
MEM_HEAD_DIM = (D_MODEL // 2) // MEM_HEADS
MEM_WIDTH = MEM_HEADS * MEM_HEAD_DIM
N_BRANCHES = 3
D_FF = 4 * D_MODEL
IN_COLS = 3 * CONV_WIDTH + 3 * FOX_WIDTH + FOX_HEADS + MEM_WIDTH + N_BRANCHES * D_MODEL

kernel_name = "hybrid_gated_conv_fox_memxattn_block"


def _rms(x, g):
    xf = x.astype(jnp.float32)
    y = xf * lax.rsqrt(jnp.mean(xf * xf, axis=-1, keepdims=True) + EPS)
    return (y * g.astype(jnp.float32)).astype(x.dtype)


def _causal_dwconv(u, w):
    s = u.shape[1]
    up = jnp.pad(u, ((0, 0), (CONV_TAPS - 1, 0), (0, 0)))
    y = w[0] * up[:, 0:s]
    for i in range(1, CONV_TAPS):
        y = y + w[i] * up[:, i:i + s]
    return y


def _forgetting_attention(q, k, v, log_f):
    b, s, h, d = q.shape
    nb = s // BLOCK_Q
    c = jnp.cumsum(log_f, axis=1).transpose(0, 2, 1)
    qh = q.transpose(0, 2, 1, 3)
    kh = k.transpose(0, 2, 1, 3)
    vh = v.transpose(0, 2, 1, 3)
    q_blocks = qh.reshape(b, h, nb, BLOCK_Q, d).transpose(2, 0, 1, 3, 4)
    c_blocks = c.reshape(b, h, nb, BLOCK_Q).transpose(2, 0, 1, 3)
    pos_blocks = jnp.arange(s).reshape(nb, BLOCK_Q)
    kpos = jnp.arange(s)
    scale = 1.0 / math.sqrt(d)

    def one_block(args):
        qb, cb, pb = args
        logits = jnp.einsum('bhqd,bhkd->bhqk', qb, kh,
                            preferred_element_type=jnp.float32) * scale
        logits = logits + (cb[..., None] - c[:, :, None, :])
        logits = jnp.where(pb[:, None] >= kpos[None, :], logits, -jnp.inf)
        p = jax.nn.softmax(logits, axis=-1)
        return jnp.einsum('bhqk,bhkd->bhqd', p.astype(vh.dtype), vh)

    out = lax.map(one_block, (q_blocks, c_blocks, pos_blocks))
    return out.transpose(1, 0, 3, 2, 4).reshape(b, s, h * d)


def _memory_attention(qm, mem, mem_norm_g, w_mem_kv, mem_q_g, mem_k_g):
    b, s, _, _ = qm.shape
    m = mem.shape[1]
    kv = _rms(mem, mem_norm_g) @ w_mem_kv
    km = kv[..., :MEM_WIDTH].reshape(b, m, MEM_HEADS, MEM_HEAD_DIM)
    vm = kv[..., MEM_WIDTH:].reshape(b, m, MEM_HEADS, MEM_HEAD_DIM)
    qm = _rms(qm, mem_q_g)
    km = _rms(km, mem_k_g)
    logits = jnp.einsum('bshd,bmhd->bhsm', qm, km,
                        preferred_element_type=jnp.float32) / math.sqrt(MEM_HEAD_DIM)
    p = jax.nn.softmax(logits, axis=-1)
    out = jnp.einsum('bhsm,bmhd->bshd', p.astype(vm.dtype), vm)
    return out.reshape(b, s, MEM_WIDTH)


def _fwd_setup_inputs(seed: int = 0) -> dict:
    key = jax.random.key(seed)
    ks = jax.random.split(key, 20)
    f32 = jnp.float32

    def nrm(k, shape, fan_in):
        return jax.random.normal(k, shape, f32) * (fan_in ** -0.5)

    def gain(k, n):
        return 1.0 + 0.02 * jax.random.normal(k, (n,), f32)

    return {
        "x": jax.random.normal(ks[0], (BATCH, SEQ, D_MODEL), f32),
        "mem": jax.random.normal(ks[1], (BATCH, MEM_TOKENS, D_MODEL), f32),
        "norm1_g": gain(ks[2], D_MODEL),
        "w_in": nrm(ks[3], (D_MODEL, IN_COLS), D_MODEL),
        "b_f": jax.random.uniform(ks[4], (FOX_HEADS,), f32, minval=1.0, maxval=4.0),
        "conv_w": nrm(ks[5], (CONV_TAPS, CONV_WIDTH), CONV_TAPS),
        "fox_q_g": gain(ks[6], FOX_HEAD_DIM),
        "fox_k_g": gain(ks[7], FOX_HEAD_DIM),
        "mem_norm_g": gain(ks[8], D_MODEL),
        "w_mem_kv": nrm(ks[9], (D_MODEL, 2 * MEM_WIDTH), D_MODEL),
        "mem_q_g": gain(ks[10], MEM_HEAD_DIM),
        "mem_k_g": gain(ks[11], MEM_HEAD_DIM),
        "w_conv_out": nrm(ks[12], (CONV_WIDTH, D_MODEL), CONV_WIDTH),
        "w_fox_out": nrm(ks[13], (FOX_WIDTH, D_MODEL), FOX_WIDTH),
        "w_mem_out": nrm(ks[14], (MEM_WIDTH, D_MODEL), MEM_WIDTH),
        "w_out": nrm(ks[15], (D_MODEL, D_MODEL), D_MODEL),
        "norm2_g": gain(ks[16], D_MODEL),
        "w_up": nrm(ks[17], (D_MODEL, D_FF), D_MODEL),
        "w_down": nrm(ks[18], (D_FF, D_MODEL), D_FF),
    }


def _fwd_reference(x, mem, norm1_g, w_in, b_f, conv_w, fox_q_g, fox_k_g, mem_norm_g,
              w_mem_kv, mem_q_g, mem_k_g, w_conv_out, w_fox_out, w_mem_out,
              w_out, norm2_g, w_up, w_down):
    b, s, _ = x.shape
    for _layer in range(DEPTH):
        h = _rms(x, norm1_g)
        proj = h @ w_in
        o = 0
        conv_bg = proj[..., o:o + CONV_WIDTH]; o += CONV_WIDTH
        conv_cg = proj[..., o:o + CONV_WIDTH]; o += CONV_WIDTH
        conv_v = proj[..., o:o + CONV_WIDTH]; o += CONV_WIDTH
        fq = proj[..., o:o + FOX_WIDTH]; o += FOX_WIDTH
        fk = proj[..., o:o + FOX_WIDTH]; o += FOX_WIDTH
        fv = proj[..., o:o + FOX_WIDTH]; o += FOX_WIDTH
        f_logit = proj[..., o:o + FOX_HEADS]; o += FOX_HEADS
        mq = proj[..., o:o + MEM_WIDTH]; o += MEM_WIDTH
        gate_logit = proj[..., o:o + N_BRANCHES * D_MODEL]

        y_conv = conv_bg * _causal_dwconv(conv_cg * conv_v, conv_w)
        o_conv = y_conv @ w_conv_out

        fq = _rms(fq.reshape(b, s, FOX_HEADS, FOX_HEAD_DIM), fox_q_g)
        fk = _rms(fk.reshape(b, s, FOX_HEADS, FOX_HEAD_DIM), fox_k_g)
        fv = fv.reshape(b, s, FOX_HEADS, FOX_HEAD_DIM)
        log_f = jax.nn.log_sigmoid(f_logit.astype(jnp.float32) + b_f.astype(jnp.float32))
        o_fox = _forgetting_attention(fq, fk, fv, log_f) @ w_fox_out

        mq = mq.reshape(b, s, MEM_HEADS, MEM_HEAD_DIM)
        o_mem = _memory_attention(mq, mem, mem_norm_g, w_mem_kv, mem_q_g, mem_k_g) @ w_mem_out

        g = jax.nn.sigmoid(gate_logit).reshape(b, s, N_BRANCHES, D_MODEL)
        merged = g[:, :, 0] * o_conv + g[:, :, 1] * o_fox + g[:, :, 2] * o_mem
        x = x + merged @ w_out

        h2 = _rms(x, norm2_g)
        x = x + jnp.square(jax.nn.relu(h2 @ w_up)) @ w_down
    return x


import jax as _jax
import jax.numpy as _jnp

TWIN_FORMAT = 'train_step'
FWD_PARAMS = ['x', 'mem', 'norm1_g', 'w_in', 'b_f', 'conv_w', 'fox_q_g', 'fox_k_g', 'mem_norm_g', 'w_mem_kv', 'mem_q_g', 'mem_k_g', 'w_conv_out', 'w_fox_out', 'w_mem_out', 'w_out', 'norm2_g', 'w_up', 'w_down']
TWIN_WEIGHTS = ['norm1_g', 'w_in', 'b_f', 'conv_w', 'fox_q_g', 'fox_k_g', 'mem_norm_g', 'w_mem_kv', 'mem_q_g', 'mem_k_g', 'w_conv_out', 'w_fox_out', 'w_mem_out', 'w_out', 'norm2_g', 'w_up', 'w_down']
TWIN_DIFF_INPUT = 'x'
TWIN_INPUTS = ['x', 'mem', 'norm1_g', 'w_in', 'b_f', 'conv_w', 'fox_q_g', 'fox_k_g', 'mem_norm_g', 'w_mem_kv', 'mem_q_g', 'mem_k_g', 'w_conv_out', 'w_fox_out', 'w_mem_out', 'w_out', 'norm2_g', 'w_up', 'w_down', 'loss_target', 'm_norm1_g', 'm_w_in', 'm_b_f', 'm_conv_w', 'm_fox_q_g', 'm_fox_k_g', 'm_mem_norm_g', 'm_w_mem_kv', 'm_mem_q_g', 'm_mem_k_g', 'm_w_conv_out', 'm_w_fox_out', 'm_w_mem_out', 'm_w_out', 'm_norm2_g', 'm_w_up', 'm_w_down', 'v_norm1_g', 'v_w_in', 'v_b_f', 'v_conv_w', 'v_fox_q_g', 'v_fox_k_g', 'v_mem_norm_g', 'v_w_mem_kv', 'v_mem_q_g', 'v_mem_k_g', 'v_w_conv_out', 'v_w_fox_out', 'v_w_mem_out', 'v_w_out', 'v_norm2_g', 'v_w_up', 'v_w_down']
TWIN_OUTPUTS = ['loss', 'grad_x', 'grad_norm1_g', 'grad_w_in', 'grad_b_f', 'grad_conv_w', 'grad_fox_q_g', 'grad_fox_k_g', 'grad_mem_norm_g', 'grad_w_mem_kv', 'grad_mem_q_g', 'grad_mem_k_g', 'grad_w_conv_out', 'grad_w_fox_out', 'grad_w_mem_out', 'grad_w_out', 'grad_norm2_g', 'grad_w_up', 'grad_w_down', 'delta_norm1_g', 'delta_w_in', 'delta_b_f', 'delta_conv_w', 'delta_fox_q_g', 'delta_fox_k_g', 'delta_mem_norm_g', 'delta_w_mem_kv', 'delta_mem_q_g', 'delta_mem_k_g', 'delta_w_conv_out', 'delta_w_fox_out', 'delta_w_mem_out', 'delta_w_out', 'delta_norm2_g', 'delta_w_up', 'delta_w_down', 'new_m_norm1_g', 'new_m_w_in', 'new_m_b_f', 'new_m_conv_w', 'new_m_fox_q_g', 'new_m_fox_k_g', 'new_m_mem_norm_g', 'new_m_w_mem_kv', 'new_m_mem_q_g', 'new_m_mem_k_g', 'new_m_w_conv_out', 'new_m_w_fox_out', 'new_m_w_mem_out', 'new_m_w_out', 'new_m_norm2_g', 'new_m_w_up', 'new_m_w_down', 'new_v_norm1_g', 'new_v_w_in', 'new_v_b_f', 'new_v_conv_w', 'new_v_fox_q_g', 'new_v_fox_k_g', 'new_v_mem_norm_g', 'new_v_w_mem_kv', 'new_v_mem_q_g', 'new_v_mem_k_g', 'new_v_w_conv_out', 'new_v_w_fox_out', 'new_v_w_mem_out', 'new_v_w_out', 'new_v_norm2_g', 'new_v_w_up', 'new_v_w_down']
TWIN_LEAF_KINDS = {'loss': 'loss', 'grad_x': 'grad_x', 'grad_norm1_g': 'grad_w', 'grad_w_in': 'grad_w', 'grad_b_f': 'grad_w', 'grad_conv_w': 'grad_w', 'grad_fox_q_g': 'grad_w', 'grad_fox_k_g': 'grad_w', 'grad_mem_norm_g': 'grad_w', 'grad_w_mem_kv': 'grad_w', 'grad_mem_q_g': 'grad_w', 'grad_mem_k_g': 'grad_w', 'grad_w_conv_out': 'grad_w', 'grad_w_fox_out': 'grad_w', 'grad_w_mem_out': 'grad_w', 'grad_w_out': 'grad_w', 'grad_norm2_g': 'grad_w', 'grad_w_up': 'grad_w', 'grad_w_down': 'grad_w', 'delta_norm1_g': 'delta_w', 'delta_w_in': 'delta_w', 'delta_b_f': 'delta_w', 'delta_conv_w': 'delta_w', 'delta_fox_q_g': 'delta_w', 'delta_fox_k_g': 'delta_w', 'delta_mem_norm_g': 'delta_w', 'delta_w_mem_kv': 'delta_w', 'delta_mem_q_g': 'delta_w', 'delta_mem_k_g': 'delta_w', 'delta_w_conv_out': 'delta_w', 'delta_w_fox_out': 'delta_w', 'delta_w_mem_out': 'delta_w', 'delta_w_out': 'delta_w', 'delta_norm2_g': 'delta_w', 'delta_w_up': 'delta_w', 'delta_w_down': 'delta_w', 'new_m_norm1_g': 'new_m', 'new_m_w_in': 'new_m', 'new_m_b_f': 'new_m', 'new_m_conv_w': 'new_m', 'new_m_fox_q_g': 'new_m', 'new_m_fox_k_g': 'new_m', 'new_m_mem_norm_g': 'new_m', 'new_m_w_mem_kv': 'new_m', 'new_m_mem_q_g': 'new_m', 'new_m_mem_k_g': 'new_m', 'new_m_w_conv_out': 'new_m', 'new_m_w_fox_out': 'new_m', 'new_m_w_mem_out': 'new_m', 'new_m_w_out': 'new_m', 'new_m_norm2_g': 'new_m', 'new_m_w_up': 'new_m', 'new_m_w_down': 'new_m', 'new_v_norm1_g': 'new_v', 'new_v_w_in': 'new_v', 'new_v_b_f': 'new_v', 'new_v_conv_w': 'new_v', 'new_v_fox_q_g': 'new_v', 'new_v_fox_k_g': 'new_v', 'new_v_mem_norm_g': 'new_v', 'new_v_w_mem_kv': 'new_v', 'new_v_mem_q_g': 'new_v', 'new_v_mem_k_g': 'new_v', 'new_v_w_conv_out': 'new_v', 'new_v_w_fox_out': 'new_v', 'new_v_w_mem_out': 'new_v', 'new_v_w_out': 'new_v', 'new_v_norm2_g': 'new_v', 'new_v_w_up': 'new_v', 'new_v_w_down': 'new_v'}


def _forward(args):
    return _fwd_reference(*[args[k] for k in FWD_PARAMS])


def _output_shape():
    def fwd():
        inp = _fwd_setup_inputs(0)
        return _fwd_reference(*[inp[k] for k in FWD_PARAMS])
    out = _jax.eval_shape(fwd)
    return out.shape, out.dtype

N_MICROBATCH = 1
ADAM_LR = 0.001
ADAM_B1 = 0.9
ADAM_B2 = 0.999
ADAM_EPS = 1e-08
ADAM_WD = 0.01
ADAM_STEP = 10
PER_EXAMPLE_BATCH_AXIS = {'x': 0, 'mem': 0, 'loss_target': 0}
SHARED_INPUTS = []
_WEIGHT_DTYPES = {'norm1_g': _jnp.float32, 'w_in': _jnp.float32, 'b_f': _jnp.float32, 'conv_w': _jnp.float32, 'fox_q_g': _jnp.float32, 'fox_k_g': _jnp.float32, 'mem_norm_g': _jnp.float32, 'w_mem_kv': _jnp.float32, 'mem_q_g': _jnp.float32, 'mem_k_g': _jnp.float32, 'w_conv_out': _jnp.float32, 'w_fox_out': _jnp.float32, 'w_mem_out': _jnp.float32, 'w_out': _jnp.float32, 'norm2_g': _jnp.float32, 'w_up': _jnp.float32, 'w_down': _jnp.float32}
MOMENT_SCALE = {'norm1_g': 1.539665e+01, 'w_in': 1.888973e-01, 'b_f': 5.741736e+01, 'conv_w': 5.379231e+00, 'fox_q_g': 4.066885e+00, 'fox_k_g': 4.084238e+00, 'mem_norm_g': 1.691959e-01, 'w_mem_kv': 1.546709e-01, 'mem_q_g': 3.678405e-01, 'mem_k_g': 3.671781e-01, 'w_conv_out': 3.231859e-01, 'w_fox_out': 9.593443e-02, 'w_mem_out': 1.532771e-01, 'w_out': 3.730365e-01, 'norm2_g': 4.807115e+01, 'w_up': 3.348763e-01, 'w_down': 3.982188e+00}


def _to_microbatches(a, axis):
    t = _jnp.moveaxis(a, axis, 0)
    t = t.reshape((N_MICROBATCH, t.shape[0] // N_MICROBATCH) + t.shape[1:])
    return _jnp.moveaxis(t, 1, axis + 1)


def setup_inputs(seed: int = 0) -> dict:
    inp = _fwd_setup_inputs(seed)
    key = _jax.random.fold_in(_jax.random.key(seed), 7919)
    shape, _ = _output_shape()
    out = dict(inp)
    out["loss_target"] = _jax.random.normal(_jax.random.fold_in(key, 0), shape, _jnp.float32)
    for i, name in enumerate(TWIN_WEIGHTS):
        w = inp[name].astype(_jnp.float32)
        if MOMENT_SCALE is None:
            s = _jnp.sqrt(_jnp.mean(_jnp.square(w)) + 1e-30)
        else:
            s = MOMENT_SCALE[name]
        km, kv = _jax.random.split(_jax.random.fold_in(key, i + 1))
        out[name] = w
        out["m_" + name] = s * _jax.random.normal(km, w.shape, _jnp.float32)
        out["v_" + name] = (s * s) * _jax.random.uniform(kv, w.shape, _jnp.float32, 0.5, 1.5)
    if N_MICROBATCH > 1:
        for name, axis in PER_EXAMPLE_BATCH_AXIS.items():
            out[name] = _to_microbatches(out[name], axis)
    return {'x': out['x'], 'mem': out['mem'], 'norm1_g': out['norm1_g'], 'w_in': out['w_in'], 'b_f': out['b_f'], 'conv_w': out['conv_w'], 'fox_q_g': out['fox_q_g'], 'fox_k_g': out['fox_k_g'], 'mem_norm_g': out['mem_norm_g'], 'w_mem_kv': out['w_mem_kv'], 'mem_q_g': out['mem_q_g'], 'mem_k_g': out['mem_k_g'], 'w_conv_out': out['w_conv_out'], 'w_fox_out': out['w_fox_out'], 'w_mem_out': out['w_mem_out'], 'w_out': out['w_out'], 'norm2_g': out['norm2_g'], 'w_up': out['w_up'], 'w_down': out['w_down'], 'loss_target': out['loss_target'], 'm_norm1_g': out['m_norm1_g'], 'm_w_in': out['m_w_in'], 'm_b_f': out['m_b_f'], 'm_conv_w': out['m_conv_w'], 'm_fox_q_g': out['m_fox_q_g'], 'm_fox_k_g': out['m_fox_k_g'], 'm_mem_norm_g': out['m_mem_norm_g'], 'm_w_mem_kv': out['m_w_mem_kv'], 'm_mem_q_g': out['m_mem_q_g'], 'm_mem_k_g': out['m_mem_k_g'], 'm_w_conv_out': out['m_w_conv_out'], 'm_w_fox_out': out['m_w_fox_out'], 'm_w_mem_out': out['m_w_mem_out'], 'm_w_out': out['m_w_out'], 'm_norm2_g': out['m_norm2_g'], 'm_w_up': out['m_w_up'], 'm_w_down': out['m_w_down'], 'v_norm1_g': out['v_norm1_g'], 'v_w_in': out['v_w_in'], 'v_b_f': out['v_b_f'], 'v_conv_w': out['v_conv_w'], 'v_fox_q_g': out['v_fox_q_g'], 'v_fox_k_g': out['v_fox_k_g'], 'v_mem_norm_g': out['v_mem_norm_g'], 'v_w_mem_kv': out['v_w_mem_kv'], 'v_mem_q_g': out['v_mem_q_g'], 'v_mem_k_g': out['v_mem_k_g'], 'v_w_conv_out': out['v_w_conv_out'], 'v_w_fox_out': out['v_w_fox_out'], 'v_w_mem_out': out['v_w_mem_out'], 'v_w_out': out['v_w_out'], 'v_norm2_g': out['v_norm2_g'], 'v_w_up': out['v_w_up'], 'v_w_down': out['v_w_down']}


def _loss(weights, diff, rest, loss_target):
    with _jax.named_scope("forward"):
        args = {**rest, TWIN_DIFF_INPUT: diff, **{k: w.astype(_WEIGHT_DTYPES[k]) for k, w in weights.items()}}
        y = _forward(args)
    with _jax.named_scope("loss_head"):
        err = _jnp.square(y.astype(_jnp.float32) - loss_target)
        return 0.5 * _jnp.sum(_jnp.mean(err, axis=-1)) if err.ndim else 0.5 * err


def _adamw(w, g, m, v):
    m = ADAM_B1 * m + (1.0 - ADAM_B1) * g
    v = ADAM_B2 * v + (1.0 - ADAM_B2) * _jnp.square(g)
    m_hat = m / (1.0 - ADAM_B1 ** ADAM_STEP)
    v_hat = v / (1.0 - ADAM_B2 ** ADAM_STEP)
    delta = -ADAM_LR * (m_hat / (_jnp.sqrt(v_hat) + ADAM_EPS) + ADAM_WD * w)
    return delta, m, v


def reference(x, mem, norm1_g, w_in, b_f, conv_w, fox_q_g, fox_k_g, mem_norm_g, w_mem_kv, mem_q_g, mem_k_g, w_conv_out, w_fox_out, w_mem_out, w_out, norm2_g, w_up, w_down, loss_target, m_norm1_g, m_w_in, m_b_f, m_conv_w, m_fox_q_g, m_fox_k_g, m_mem_norm_g, m_w_mem_kv, m_mem_q_g, m_mem_k_g, m_w_conv_out, m_w_fox_out, m_w_mem_out, m_w_out, m_norm2_g, m_w_up, m_w_down, v_norm1_g, v_w_in, v_b_f, v_conv_w, v_fox_q_g, v_fox_k_g, v_mem_norm_g, v_w_mem_kv, v_mem_q_g, v_mem_k_g, v_w_conv_out, v_w_fox_out, v_w_mem_out, v_w_out, v_norm2_g, v_w_up, v_w_down):
    given = dict(x=x, mem=mem, norm1_g=norm1_g, w_in=w_in, b_f=b_f, conv_w=conv_w, fox_q_g=fox_q_g, fox_k_g=fox_k_g, mem_norm_g=mem_norm_g, w_mem_kv=w_mem_kv, mem_q_g=mem_q_g, mem_k_g=mem_k_g, w_conv_out=w_conv_out, w_fox_out=w_fox_out, w_mem_out=w_mem_out, w_out=w_out, norm2_g=norm2_g, w_up=w_up, w_down=w_down, loss_target=loss_target, m_norm1_g=m_norm1_g, m_w_in=m_w_in, m_b_f=m_b_f, m_conv_w=m_conv_w, m_fox_q_g=m_fox_q_g, m_fox_k_g=m_fox_k_g, m_mem_norm_g=m_mem_norm_g, m_w_mem_kv=m_w_mem_kv, m_mem_q_g=m_mem_q_g, m_mem_k_g=m_mem_k_g, m_w_conv_out=m_w_conv_out, m_w_fox_out=m_w_fox_out, m_w_mem_out=m_w_mem_out, m_w_out=m_w_out, m_norm2_g=m_norm2_g, m_w_up=m_w_up, m_w_down=m_w_down, v_norm1_g=v_norm1_g, v_w_in=v_w_in, v_b_f=v_b_f, v_conv_w=v_conv_w, v_fox_q_g=v_fox_q_g, v_fox_k_g=v_fox_k_g, v_mem_norm_g=v_mem_norm_g, v_w_mem_kv=v_w_mem_kv, v_mem_q_g=v_mem_q_g, v_mem_k_g=v_mem_k_g, v_w_conv_out=v_w_conv_out, v_w_fox_out=v_w_fox_out, v_w_mem_out=v_w_mem_out, v_w_out=v_w_out, v_norm2_g=v_norm2_g, v_w_up=v_w_up, v_w_down=v_w_down)
    weights = {n: given[n] for n in TWIN_WEIGHTS}
    shared = {n: given[n] for n in SHARED_INPUTS}
    per_example = {n: given[n] for n in ['x', 'mem']}
    grad_fn = _jax.value_and_grad(_loss, argnums=(0, 1))

    def one_microbatch(ex, loss_target):
        ex = dict(ex)
        diff = ex.pop(TWIN_DIFF_INPUT)
        return grad_fn(weights, diff, {**shared, **ex}, loss_target)

    if N_MICROBATCH == 1:
        loss, (grad_w, grad_x) = one_microbatch(per_example, given["loss_target"])
    else:
        def body(carry, xs):
            loss_sum, grad_sum = carry
            l_k, (gw_k, gx_k) = one_microbatch(xs[0], xs[1])
            with _jax.named_scope("update"):
                return (loss_sum + l_k, _jax.tree.map(_jnp.add, grad_sum, gw_k)), gx_k

        init = (_jnp.zeros((), _jnp.float32), _jax.tree.map(_jnp.zeros_like, weights))
        (loss, grad_w), grad_x = _jax.lax.scan(body, init, (per_example, given["loss_target"]))
    with _jax.named_scope("update"):
        delta_w, new_m, new_v = {}, {}, {}
        for n in TWIN_WEIGHTS:
            delta_w[n], new_m[n], new_v[n] = _adamw(weights[n], grad_w[n], given["m_" + n], given["v_" + n])
    return (loss, grad_x, *[grad_w[n] for n in TWIN_WEIGHTS], *[delta_w[n] for n in TWIN_WEIGHTS],
            *[new_m[n] for n in TWIN_WEIGHTS], *[new_v[n] for n in TWIN_WEIGHTS])
```

```python
import functools
import math

import jax
import jax.numpy as jnp
from jax import lax
from jax.experimental import pallas as pl
from jax.experimental.pallas import tpu as pltpu

F32 = jnp.float32
BF16 = jnp.bfloat16

EPS = 1e-6
N_DEV = 8
FOX_HEAD_DIM = 128
MEM_HEADS = 4
CONV_TAPS = 3
N_BRANCHES = 3
F_ROWS = 16

ADAM_LR = 0.001
ADAM_B1 = 0.9
ADAM_B2 = 0.999
ADAM_EPS = 1e-08
ADAM_WD = 0.01
ADAM_STEP = 10

V7X_VMEM_BYTES = 64 * 1024 * 1024
VMEM_LIMIT = V7X_VMEM_BYTES * 3 // 4
LANES = 128
NEG = -1e30

MESH_AXES = ("x", "y", "c")
MESH = pl.DeviceIdType.MESH
ANY = pl.BlockSpec(memory_space=pl.ANY)

NN = (((1,), (0,)), ((), ()))
NT = (((1,), (1,)), ((), ()))
TN = (((0,), (0,)), ((), ()))


def _params(sem):
    return pltpu.CompilerParams(dimension_semantics=sem, vmem_limit_bytes=VMEM_LIMIT)


def _dot(a, b, dn):
    return lax.dot_general(a, b, dn, preferred_element_type=F32)


def _tile(n, t):
    if n <= t:
        return n
    for cand in range(t - t % LANES, 0, -LANES):
        if n % cand == 0:
            return cand
    raise ValueError((n, t))


def _matmul(name, kind, a, b, *, tm, tn, tk, outs, epilogue=None, extras=(), b_blocks=False, out_blocks=False):
    if kind == "nn":
        m, kdim = a.shape
        n = b.shape[0] * b.shape[2] if b_blocks else b.shape[1]
    elif kind == "nt":
        m, kdim = a.shape
        n = b.shape[1] if b_blocks else b.shape[0]
    else:
        kdim, m = a.shape
        n = b.shape[1]
    if b_blocks and kind == "nn":
        tn = min(tn, b.shape[2])
    if b_blocks and kind == "nt":
        tk = min(tk, b.shape[2])
    if out_blocks:
        tn = min(tn, n // N_DEV)
    tm, tn, tk = _tile(m, tm), _tile(n, tn), _tile(kdim, tk)
    nk = kdim // tk

    if kind == "tn":
        a_spec = pl.BlockSpec((tk, tm), lambda i, j, k: (k, i))
    else:
        a_spec = pl.BlockSpec((tm, tk), lambda i, j, k: (i, k))
    if kind == "nn":
        if b_blocks:
            r = b.shape[2] // tn
            assert b.shape[2] % tn == 0
            b_spec = pl.BlockSpec((None, tk, tn), lambda i, j, k: (j // r, k, j % r))
        else:
            b_spec = pl.BlockSpec((tk, tn), lambda i, j, k: (k, j))
    elif kind == "nt":
        if b_blocks:
            r = b.shape[2] // tk
            assert b.shape[2] % tk == 0
            b_spec = pl.BlockSpec((None, tn, tk), lambda i, j, k: (k // r, j, k % r))
        else:
            b_spec = pl.BlockSpec((tn, tk), lambda i, j, k: (j, k))
    else:
        b_spec = pl.BlockSpec((tk, tn), lambda i, j, k: (k, j))
    dn = {"nn": NN, "nt": NT, "tn": TN}[kind]

    tile_spec = pl.BlockSpec((tm, tn), lambda i, j, k: (i, j))
    if out_blocks:
        width = n // N_DEV
        r_out = width // tn
        assert width % tn == 0
        out_shape = [jax.ShapeDtypeStruct((N_DEV, m, width), dt) for dt in outs]
        out_specs = [pl.BlockSpec((None, tm, tn), lambda i, j, k: (j // r_out, i, j % r_out)) for _ in outs]
    else:
        out_shape = [jax.ShapeDtypeStruct((m, n), dt) for dt in outs]
        out_specs = [tile_spec for _ in outs]
    n_ex, n_out = len(extras), len(outs)

    def body(*refs):
        a_ref, b_ref = refs[0], refs[1]
        ex_refs = refs[2:2 + n_ex]
        out_refs = refs[2 + n_ex:2 + n_ex + n_out]
        acc_ref = refs[-1]
        k = pl.program_id(2)

        @pl.when(k == 0)
        def _():
            acc_ref[...] = jnp.zeros_like(acc_ref)

        acc_ref[...] += _dot(a_ref[...], b_ref[...], dn)

        @pl.when(k == nk - 1)
        def _():
            acc = acc_ref[...]
            vals = (acc,) if epilogue is None else epilogue(acc, *[e[...] for e in ex_refs])
            for o_ref, v in zip(out_refs, vals):
                o_ref[...] = v.astype(o_ref.dtype)

    res = pl.pallas_call(
        body,
        name=name,
        grid=(m // tm, n // tn, nk),
        in_specs=[a_spec, b_spec] + [tile_spec for _ in extras],
        out_specs=out_specs,
        out_shape=out_shape,
        scratch_shapes=[pltpu.VMEM((tm, tn), F32)],
        compiler_params=_params(("parallel", "parallel", "arbitrary")),
    )(a, b, *extras)
    return res[0] if n_out == 1 else res


def _rms_fwd(name, x, g, tm=512):
    t, d = x.shape
    tm = _tile(t, tm)

    def body(x_ref, g_ref, h_ref):
        xf = x_ref[...]
        r = lax.rsqrt(jnp.mean(xf * xf, axis=-1, keepdims=True) + EPS)
        h_ref[...] = (xf * r * g_ref[...]).astype(h_ref.dtype)

    return pl.pallas_call(
        body,
        name=name,
        grid=(t // tm,),
        in_specs=[pl.BlockSpec((tm, d), lambda i: (i, 0)), pl.BlockSpec((1, d), lambda i: (0, 0))],
        out_specs=pl.BlockSpec((tm, d), lambda i: (i, 0)),
        out_shape=jax.ShapeDtypeStruct((t, d), BF16),
        compiler_params=_params(("parallel",)),
    )(x, g.reshape(1, d))


def _rms_bwd(name, dh, x, g, res=None, tm=256):
    t, d = x.shape
    tm = _tile(t, tm)
    has_res = res is not None

    def body(*refs):
        if has_res:
            dh_ref, x_ref, g_ref, res_ref, dx_ref, dxb_ref, gg_ref = refs
        else:
            dh_ref, x_ref, g_ref, dx_ref, dxb_ref, gg_ref = refs
        i = pl.program_id(0)
        xf = x_ref[...]
        r = lax.rsqrt(jnp.mean(xf * xf, axis=-1, keepdims=True) + EPS)
        xh = xf * r
        dhf = dh_ref[...].astype(F32)
        dxh = dhf * g_ref[...]
        dx = r * (dxh - xh * jnp.mean(dxh * xh, axis=-1, keepdims=True))
        if has_res:
            dx = dx + res_ref[...]
        dx_ref[...] = dx
        dxb_ref[...] = dx.astype(BF16)

        @pl.when(i == 0)
        def _():
            gg_ref[...] = jnp.zeros_like(gg_ref)

        gg_ref[...] += jnp.sum(dhf * xh, axis=0, keepdims=True)

    row = pl.BlockSpec((tm, d), lambda i: (i, 0))
    vec = pl.BlockSpec((1, d), lambda i: (0, 0))
    ins = [dh, x, g.reshape(1, d)] + ([res] if has_res else [])
    dx, dxb, gg = pl.pallas_call(
        body,
        name=name,
        grid=(t // tm,),
        in_specs=[row, row, vec] + ([row] if has_res else []),
        out_specs=[row, row, vec],
        out_shape=[jax.ShapeDtypeStruct((t, d), F32), jax.ShapeDtypeStruct((t, d), BF16), jax.ShapeDtypeStruct((1, d), F32)],
        compiler_params=_params(("arbitrary",)),
    )(*ins)
    return dx, dxb, gg.reshape(d)


def _head_rms(xf, g):
    r = lax.rsqrt(jnp.mean(xf * xf, axis=-1, keepdims=True) + EPS)
    return xf * r, r


def _head_rms_bwd(dy, xn, r, g):
    dxh = dy * g
    dx = r * (dxh - xn * jnp.mean(dxh * xn, axis=-1, keepdims=True))
    return dx, jnp.sum(dy * xn, axis=0, keepdims=True)


def _col_to_row(col):
    n = col.shape[0]
    eye = lax.broadcasted_iota(jnp.int32, (n, n), 0) == lax.broadcasted_iota(jnp.int32, (n, n), 1)
    return jnp.sum(jnp.where(eye, col, 0.0), axis=0, keepdims=True)


def _row_to_col(row):
    n = row.shape[1]
    eye = lax.broadcasted_iota(jnp.int32, (n, n), 0) == lax.broadcasted_iota(jnp.int32, (n, n), 1)
    return jnp.sum(jnp.where(eye, row, 0.0), axis=1, keepdims=True)


def _shift_down(u, s, rows):
    return jnp.where(rows >= s, pltpu.roll(u, s, axis=0), 0.0)


def _shift_up(u, s, rows, t):
    return jnp.where(rows < t - s, pltpu.roll(u, t - s, axis=0), 0.0)


def _conv_fwd(pc, conv_w, cb):
    t = pc.shape[0]
    c = pc.shape[1] // 3

    def body(p_ref, w_ref, y_ref):
        rows = lax.broadcasted_iota(jnp.int32, (t, cb), 0)
        bg = p_ref[:, 0:cb].astype(F32)
        u = p_ref[:, cb:2 * cb].astype(F32) * p_ref[:, 2 * cb:3 * cb].astype(F32)
        w = w_ref[...]
        conv = w[2:3] * u + w[1:2] * _shift_down(u, 1, rows) + w[0:1] * _shift_down(u, 2, rows)
        y_ref[...] = (bg * conv).astype(y_ref.dtype)

    return pl.pallas_call(
        body,
        name="conv_fwd",
        grid=(c // cb,),
        in_specs=[pl.BlockSpec((t, 3 * cb), lambda j: (0, j)), pl.BlockSpec((CONV_TAPS, cb), lambda j: (0, j))],
        out_specs=pl.BlockSpec((t, cb), lambda j: (0, j)),
        out_shape=jax.ShapeDtypeStruct((t, c), BF16),
        compiler_params=_params(("parallel",)),
    )(pc, conv_w)


def _conv_bwd(pc, conv_w, dy, cb):
    t = pc.shape[0]
    c = pc.shape[1] // 3

    def body(p_ref, w_ref, dy_ref, dp_ref, gw_ref):
        rows = lax.broadcasted_iota(jnp.int32, (t, cb), 0)
        bg = p_ref[:, 0:cb].astype(F32)
        cg = p_ref[:, cb:2 * cb].astype(F32)
        v = p_ref[:, 2 * cb:3 * cb].astype(F32)
        u = cg * v
        w = w_ref[...]
        u1 = _shift_down(u, 1, rows)
        u2 = _shift_down(u, 2, rows)
        conv = w[2:3] * u + w[1:2] * u1 + w[0:1] * u2
        dyf = dy_ref[...].astype(F32)
        dconv = dyf * bg
        du = w[2:3] * dconv + w[1:2] * _shift_up(dconv, 1, rows, t) + w[0:1] * _shift_up(dconv, 2, rows, t)
        dp_ref[:, 0:cb] = (dyf * conv).astype(dp_ref.dtype)
        dp_ref[:, cb:2 * cb] = (du * v).astype(dp_ref.dtype)
        dp_ref[:, 2 * cb:3 * cb] = (du * cg).astype(dp_ref.dtype)
        gw_ref[0:1, :] = jnp.sum(dconv * u2, axis=0, keepdims=True)
        gw_ref[1:2, :] = jnp.sum(dconv * u1, axis=0, keepdims=True)
        gw_ref[2:3, :] = jnp.sum(dconv * u, axis=0, keepdims=True)

    return pl.pallas_call(
        body,
        name="conv_bwd",
        grid=(c // cb,),
        in_specs=[
            pl.BlockSpec((t, 3 * cb), lambda j: (0, j)),
            pl.BlockSpec((CONV_TAPS, cb), lambda j: (0, j)),
            pl.BlockSpec((t, cb), lambda j: (0, j)),
        ],
        out_specs=[pl.BlockSpec((t, 3 * cb), lambda j: (0, j)), pl.BlockSpec((CONV_TAPS, cb), lambda j: (0, j))],
        out_shape=[jax.ShapeDtypeStruct((t, 3 * c), BF16), jax.ShapeDtypeStruct((CONV_TAPS, c), F32)],
        compiler_params=_params(("parallel",)),
    )(pc, conv_w, dy)


def _lane_scan(x, reverse):
    lane = lax.broadcasted_iota(jnp.int32, x.shape, 1)
    s = 1
    while s < LANES:
        if reverse:
            x = x + jnp.where(lane < LANES - s, pltpu.roll(x, LANES - s, axis=1), 0.0)
        else:
            x = x + jnp.where(lane >= s, pltpu.roll(x, s, axis=1), 0.0)
        s *= 2
    return x


def _scan_rows(src_ref, dst_ref, t, reverse, fn=None):
    groups = list(range(t // LANES))
    if reverse:
        groups = groups[::-1]
    carry = None
    for gi in groups:
        sl = slice(gi * LANES, (gi + 1) * LANES)
        blk = src_ref[:, sl]
        if fn is not None:
            blk = fn(blk, sl)
        blk = _lane_scan(blk, reverse)
        if carry is not None:
            blk = blk + carry
        dst_ref[:, sl] = blk
        carry = blk[:, 0:1] if reverse else blk[:, LANES - 1:LANES]


def _forget_fwd(z_row, b_col):
    rows, t = z_row.shape

    def body(z_ref, b_ref, c_ref):
        def logf(z, sl):
            zz = z + b_ref[...]
            return jnp.minimum(zz, 0.0) - jnp.log(1.0 + jnp.exp(-jnp.abs(zz)))

        _scan_rows(z_ref, c_ref, t, False, logf)

    return pl.pallas_call(
        body,
        name="forget_fwd",
        out_shape=jax.ShapeDtypeStruct((rows, t), F32),
        compiler_params=pltpu.CompilerParams(vmem_limit_bytes=VMEM_LIMIT),
    )(z_row, b_col)


def _rows_to_colb(c_row3, tq):
    heads, _, t = c_row3.shape

    def body(r_ref, o_ref):
        o_ref[...] = jnp.broadcast_to(_row_to_col(r_ref[...]), (tq, LANES))

    return pl.pallas_call(
        body,
        name="rows_to_colb",
        grid=(heads, t // tq),
        in_specs=[pl.BlockSpec((None, 1, tq), lambda h, i: (h, 0, i))],
        out_specs=pl.BlockSpec((None, tq, LANES), lambda h, i: (h, i, 0)),
        out_shape=jax.ShapeDtypeStruct((heads, t, LANES), F32),
        compiler_params=_params(("parallel", "parallel")),
    )(c_row3)


def _forget_bwd(z_row, b_col, dc_row):
    rows, t = z_row.shape

    def body(z_ref, b_ref, dc_ref, dz_ref, db_ref, tmp_ref):
        _scan_rows(dc_ref, tmp_ref, t, True)
        zz = z_ref[...] + b_ref[...]
        dz = tmp_ref[...] * (1.0 / (1.0 + jnp.exp(zz)))
        dz_ref[...] = dz.astype(dz_ref.dtype)
        db_ref[...] = jnp.sum(dz, axis=1, keepdims=True)

    return pl.pallas_call(
        body,
        name="forget_bwd",
        out_shape=[jax.ShapeDtypeStruct((rows, t), BF16), jax.ShapeDtypeStruct((rows, 1), F32)],
        scratch_shapes=[pltpu.VMEM((rows, t), F32)],
        compiler_params=pltpu.CompilerParams(vmem_limit_bytes=VMEM_LIMIT),
    )(z_row, b_col, dc_row)


def _fox_fwd(pf, gq, gk, c_row3, c_colb, heads, tq):
    t = pf.shape[0]
    hd = FOX_HEAD_DIM
    tq = _tile(t, tq)
    scale = 1.0 / math.sqrt(hd)

    def body(q_ref, k_ref, v_ref, gq_ref, gk_ref, crow_ref, ccol_ref, o_ref, lse_ref, khat_ref):
        qi = pl.program_id(1)

        @pl.when(qi == 0)
        def _():
            kn, _ = _head_rms(k_ref[...].astype(F32), None)
            khat_ref[...] = (kn * gk_ref[...]).astype(BF16)

        qn, _ = _head_rms(q_ref[...].astype(F32), None)
        qhat = (qn * gq_ref[...]).astype(BF16)
        ccol = ccol_ref[:, 0:1]
        rows = qi * tq + lax.broadcasted_iota(jnp.int32, (tq, tq), 0)

        def step(j, carry):
            m, l, acc = carry
            ks = pl.multiple_of(j * tq, tq)
            s = _dot(qhat, khat_ref[pl.ds(ks, tq), :], NT) * scale + (ccol - crow_ref[:, pl.ds(ks, tq)])
            cols = ks + lax.broadcasted_iota(jnp.int32, (tq, tq), 1)
            s = jnp.where(rows >= cols, s, NEG)
            m_new = jnp.maximum(m, jnp.max(s, axis=-1, keepdims=True))
            alpha = jnp.exp(m - m_new)
            p = jnp.exp(s - m_new)
            l = alpha * l + jnp.sum(p, axis=-1, keepdims=True)
            acc = alpha * acc + _dot(p.astype(BF16), v_ref[pl.ds(ks, tq), :], NN)
            return m_new, l, acc

        init = (jnp.full((tq, 1), NEG, F32), jnp.zeros((tq, 1), F32), jnp.zeros((tq, hd), F32))
        m, l, acc = lax.fori_loop(0, qi + 1, step, init)
        o_ref[...] = (acc / l).astype(o_ref.dtype)
        lse_ref[...] = jnp.broadcast_to(m + jnp.log(l), (tq, LANES))

    return pl.pallas_call(
        body,
        name="fox_fwd",
        grid=(heads, t // tq),
        in_specs=[
            pl.BlockSpec((tq, hd), lambda h, i: (i, 3 * h)),
            pl.BlockSpec((t, hd), lambda h, i: (0, 3 * h + 1)),
            pl.BlockSpec((t, hd), lambda h, i: (0, 3 * h + 2)),
            pl.BlockSpec((1, hd), lambda h, i: (0, 0)),
            pl.BlockSpec((1, hd), lambda h, i: (0, 0)),
            pl.BlockSpec((None, 1, t), lambda h, i: (h, 0, 0)),
            pl.BlockSpec((None, tq, LANES), lambda h, i: (h, i, 0)),
        ],
        out_specs=[pl.BlockSpec((tq, hd), lambda h, i: (i, h)), pl.BlockSpec((None, tq, LANES), lambda h, i: (h, i, 0))],
        out_shape=[jax.ShapeDtypeStruct((t, heads * hd), BF16), jax.ShapeDtypeStruct((heads, t, LANES), F32)],
        scratch_shapes=[pltpu.VMEM((t, hd), BF16)],
        compiler_params=_params(("parallel", "arbitrary")),
    )(pf, pf, pf, gq.reshape(1, hd), gk.reshape(1, hd), c_row3, c_colb)


def _fox_bwd(pf, o, do, gq, gk, c_row3, c_colb, lse, heads, tq):
    t = pf.shape[0]
    hd = FOX_HEAD_DIM
    tq = _tile(t, tq)
    nb = t // tq
    scale = 1.0 / math.sqrt(hd)

    def body(q_ref, k_ref, v_ref, o_ref, do_ref, gq_ref, gk_ref, crow_ref, ccol_ref, lse_ref,
             dp_ref, dc_ref, ggq_ref, ggk_ref, qhat_ref, khat_ref, dq_ref, dk_ref, dcq_ref, dck_ref, delta_ref):
        h = pl.program_id(0)
        qn, rq = _head_rms(q_ref[...].astype(F32), None)
        qhat_ref[...] = (qn * gq_ref[...]).astype(BF16)
        kn, rk = _head_rms(k_ref[...].astype(F32), None)
        khat_ref[...] = (kn * gk_ref[...]).astype(BF16)
        delta_ref[...] = jnp.sum(do_ref[...].astype(F32) * o_ref[...].astype(F32), axis=-1, keepdims=True)
        dq_ref[...] = jnp.zeros_like(dq_ref)
        dcq_ref[...] = jnp.zeros_like(dcq_ref)

        def kv_block(j, _):
            ks = pl.multiple_of(j * tq, tq)
            kh = khat_ref[pl.ds(ks, tq), :]
            vv = v_ref[pl.ds(ks, tq), :]
            crow = crow_ref[:, pl.ds(ks, tq)]
            cols = ks + lax.broadcasted_iota(jnp.int32, (tq, tq), 1)

            def q_block(i, carry):
                dk, dv, dck = carry
                qs = pl.multiple_of(i * tq, tq)
                qh = qhat_ref[pl.ds(qs, tq), :]
                dob = do_ref[pl.ds(qs, tq), :]
                s = _dot(qh, kh, NT) * scale + (ccol_ref[pl.ds(qs, tq), 0:1] - crow)
                rows = qs + lax.broadcasted_iota(jnp.int32, (tq, tq), 0)
                p = jnp.where(rows >= cols, jnp.exp(s - lse_ref[pl.ds(qs, tq), 0:1]), 0.0)
                ds = p * (_dot(dob, vv, NT) - delta_ref[pl.ds(qs, tq), :])
                dsb = ds.astype(BF16)
                dv = dv + _dot(p.astype(BF16), dob, TN)
                dk = dk + _dot(dsb, qh, TN)
                dq_ref[pl.ds(qs, tq), :] += _dot(dsb, kh, NN)
                dcq_ref[pl.ds(qs, tq), :] += jnp.sum(ds, axis=-1, keepdims=True)
                dck = dck + jnp.sum(ds, axis=0, keepdims=True)
                return dk, dv, dck

            zero = jnp.zeros((tq, hd), F32)
            dk, dv, dck = lax.fori_loop(j, nb, q_block, (zero, zero, jnp.zeros((1, tq), F32)))
            dk_ref[pl.ds(ks, tq), :] = dk * scale
            dp_ref[pl.ds(ks, tq), 2 * hd:3 * hd] = dv.astype(dp_ref.dtype)
            dck_ref[:, pl.ds(ks, tq)] = dck
            return 0

        lax.fori_loop(0, nb, kv_block, 0)

        dq, ggq = _head_rms_bwd(dq_ref[...] * scale, qn, rq, gq_ref[...])
        dk, ggk = _head_rms_bwd(dk_ref[...], kn, rk, gk_ref[...])
        dp_ref[:, 0:hd] = dq.astype(dp_ref.dtype)
        dp_ref[:, hd:2 * hd] = dk.astype(dp_ref.dtype)
        for b in range(nb):
            sl = slice(b * tq, (b + 1) * tq)
            dc_ref[:, sl] = _col_to_row(dcq_ref[sl, :]) - dck_ref[:, sl]

        @pl.when(h == 0)
        def _():
            ggq_ref[...] = jnp.zeros_like(ggq_ref)
            ggk_ref[...] = jnp.zeros_like(ggk_ref)

        ggq_ref[...] += ggq
        ggk_ref[...] += ggk

    head_in = lambda off: pl.BlockSpec((t, hd), lambda h: (0, 3 * h + off))
    vec = pl.BlockSpec((1, hd), lambda h: (0, 0))
    colb = pl.BlockSpec((None, t, LANES), lambda h: (h, 0, 0))
    return pl.pallas_call(
        body,
        name="fox_bwd",
        grid=(heads,),
        in_specs=[
            head_in(0), head_in(1), head_in(2),
            pl.BlockSpec((t, hd), lambda h: (0, h)),
            pl.BlockSpec((t, hd), lambda h: (0, h)),
            vec, vec,
            pl.BlockSpec((None, 1, t), lambda h: (h, 0, 0)),
            colb, colb,
        ],
        out_specs=[
            pl.BlockSpec((t, 3 * hd), lambda h: (0, h)),
            pl.BlockSpec((None, 1, t), lambda h: (h, 0, 0)),
            vec, vec,
        ],
        out_shape=[
            jax.ShapeDtypeStruct((t, 3 * heads * hd), BF16),
            jax.ShapeDtypeStruct((heads, 1, t), F32),
            jax.ShapeDtypeStruct((1, hd), F32),
            jax.ShapeDtypeStruct((1, hd), F32),
        ],
        scratch_shapes=[
            pltpu.VMEM((t, hd), BF16), pltpu.VMEM((t, hd), BF16),
            pltpu.VMEM((t, hd), F32), pltpu.VMEM((t, hd), F32),
            pltpu.VMEM((t, 1), F32), pltpu.VMEM((1, t), F32), pltpu.VMEM((t, 1), F32),
        ],
        compiler_params=_params(("arbitrary",)),
    )(pf, pf, pf, o, do, gq.reshape(1, hd), gk.reshape(1, hd), c_row3, c_colb, lse)


def _mem_fwd(pq, kv, gq, gk, tq):
    t, width = pq.shape
    m = kv.shape[0]
    hd = width // MEM_HEADS
    tq = _tile(t, tq)
    scale = 1.0 / math.sqrt(hd)

    def body(q_ref, k_ref, v_ref, gq_ref, gk_ref, o_ref):
        qn, _ = _head_rms(q_ref[...].astype(F32), None)
        kn, _ = _head_rms(k_ref[...], None)
        s = _dot((qn * gq_ref[...]).astype(BF16), (kn * gk_ref[...]).astype(BF16), NT) * scale
        p = jnp.exp(s - jnp.max(s, axis=-1, keepdims=True))
        p = p / jnp.sum(p, axis=-1, keepdims=True)
        o_ref[...] = _dot(p.astype(BF16), v_ref[...].astype(BF16), NN).astype(o_ref.dtype)

    vec = pl.BlockSpec((1, hd), lambda h, i: (0, 0))
    return pl.pallas_call(
        body,
        name="mem_fwd",
        grid=(MEM_HEADS, t // tq),
        in_specs=[
            pl.BlockSpec((tq, hd), lambda h, i: (i, h)),
            pl.BlockSpec((m, hd), lambda h, i: (0, h)),
            pl.BlockSpec((m, hd), lambda h, i: (0, MEM_HEADS + h)),
            vec, vec,
        ],
        out_specs=pl.BlockSpec((tq, hd), lambda h, i: (i, h)),
        out_shape=jax.ShapeDtypeStruct((t, width), BF16),
        compiler_params=_params(("parallel", "parallel")),
    )(pq, kv, kv, gq.reshape(1, hd), gk.reshape(1, hd))


def _mem_bwd(pq, kv, do, gq, gk, tq):
    t, width = pq.shape
    m = kv.shape[0]
    hd = width // MEM_HEADS
    tq = _tile(t, tq)
    nq = t // tq
    scale = 1.0 / math.sqrt(hd)

    def body(q_ref, k_ref, v_ref, do_ref, gq_ref, gk_ref, dq_ref, dk_ref, dv_ref, ggq_ref, ggk_ref, dkh_ref, dvh_ref):
        h = pl.program_id(0)
        i = pl.program_id(1)
        qn, rq = _head_rms(q_ref[...].astype(F32), None)
        kn, rk = _head_rms(k_ref[...], None)
        qhat = (qn * gq_ref[...]).astype(BF16)
        khat = (kn * gk_ref[...]).astype(BF16)
        vb = v_ref[...].astype(BF16)
        dob = do_ref[...]
        s = _dot(qhat, khat, NT) * scale
        p = jnp.exp(s - jnp.max(s, axis=-1, keepdims=True))
        p = p / jnp.sum(p, axis=-1, keepdims=True)
        dp = _dot(dob, vb, NT)
        ds = p * (dp - jnp.sum(dp * p, axis=-1, keepdims=True))
        dsb = ds.astype(BF16)
        dq, ggq = _head_rms_bwd(_dot(dsb, khat, NN) * scale, qn, rq, gq_ref[...])
        dq_ref[...] = dq.astype(dq_ref.dtype)

        @pl.when(i == 0)
        def _():
            dkh_ref[...] = jnp.zeros_like(dkh_ref)
            dvh_ref[...] = jnp.zeros_like(dvh_ref)

        @pl.when(jnp.logical_and(h == 0, i == 0))
        def _():
            ggq_ref[...] = jnp.zeros_like(ggq_ref)
            ggk_ref[...] = jnp.zeros_like(ggk_ref)

        dkh_ref[...] += _dot(dsb, qhat, TN)
        dvh_ref[...] += _dot(p.astype(BF16), dob, TN)
        ggq_ref[...] += ggq

        @pl.when(i == nq - 1)
        def _():
            dk, ggk = _head_rms_bwd(dkh_ref[...] * scale, kn, rk, gk_ref[...])
            dk_ref[...] = dk.astype(dk_ref.dtype)
            dv_ref[...] = dvh_ref[...].astype(dv_ref.dtype)
            ggk_ref[...] += ggk

    vec = pl.BlockSpec((1, hd), lambda h, i: (0, 0))
    kblk = pl.BlockSpec((m, hd), lambda h, i: (0, h))
    qblk = pl.BlockSpec((tq, hd), lambda h, i: (i, h))
    dq, dk, dv, ggq, ggk = pl.pallas_call(
        body,
        name="mem_bwd",
        grid=(MEM_HEADS, nq),
        in_specs=[qblk, kblk, pl.BlockSpec((m, hd), lambda h, i: (0, MEM_HEADS + h)), qblk, vec, vec],
        out_specs=[qblk, kblk, kblk, vec, vec],
        out_shape=[
            jax.ShapeDtypeStruct((t, width), BF16),
            jax.ShapeDtypeStruct((m, width), BF16),
            jax.ShapeDtypeStruct((m, width), BF16),
            jax.ShapeDtypeStruct((1, hd), F32),
            jax.ShapeDtypeStruct((1, hd), F32),
        ],
        scratch_shapes=[pltpu.VMEM((m, hd), F32), pltpu.VMEM((m, hd), F32)],
        compiler_params=_params(("arbitrary", "arbitrary")),
    )(pq, kv, kv, do, gq.reshape(1, hd), gk.reshape(1, hd))
    return dq, jnp.concatenate([dk, dv], axis=1), ggq.reshape(hd), ggk.reshape(hd)


def _sigmoid(z):
    return 1.0 / (1.0 + jnp.exp(-z))


def _merge_fwd(gate, o3, tm, tc):
    t, d = o3[0].shape
    tm = _tile(t, tm)

    def body(g_ref, oa_ref, ob_ref, oc_ref, out_ref):
        acc = jnp.zeros((tm, tc), F32)
        for s, o_ref in enumerate((oa_ref, ob_ref, oc_ref)):
            acc = acc + _sigmoid(g_ref[:, s * tc:(s + 1) * tc].astype(F32)) * o_ref[...].astype(F32)
        out_ref[...] = acc.astype(out_ref.dtype)

    blk = pl.BlockSpec((tm, tc), lambda i, j: (i, j))
    return pl.pallas_call(
        body,
        name="merge_fwd",
        grid=(t // tm, d // tc),
        in_specs=[pl.BlockSpec((tm, 3 * tc), lambda i, j: (i, j)), blk, blk, blk],
        out_specs=blk,
        out_shape=jax.ShapeDtypeStruct((t, d), BF16),
        compiler_params=_params(("parallel", "parallel")),
    )(gate, *o3)


def _merge_bwd(gate, o3, dm, tm, tc):
    t, d = dm.shape
    tm = _tile(t, tm)

    def body(g_ref, oa_ref, ob_ref, oc_ref, dm_ref, dg_ref, da_ref, db_ref, dc_ref):
        dmf = dm_ref[...].astype(F32)
        for s, (o_ref, do_ref) in enumerate(((oa_ref, da_ref), (ob_ref, db_ref), (oc_ref, dc_ref))):
            g = _sigmoid(g_ref[:, s * tc:(s + 1) * tc].astype(F32))
            do_ref[...] = (dmf * g).astype(do_ref.dtype)
            dg_ref[:, s * tc:(s + 1) * tc] = (dmf * o_ref[...].astype(F32) * g * (1.0 - g)).astype(dg_ref.dtype)

    blk = pl.BlockSpec((tm, tc), lambda i, j: (i, j))
    wide = pl.BlockSpec((tm, 3 * tc), lambda i, j: (i, j))
    return pl.pallas_call(
        body,
        name="merge_bwd",
        grid=(t // tm, d // tc),
        in_specs=[wide, blk, blk, blk, blk],
        out_specs=[wide, blk, blk, blk],
        out_shape=[jax.ShapeDtypeStruct((t, 3 * d), BF16)] + [jax.ShapeDtypeStruct((t, d), BF16)] * 3,
        compiler_params=_params(("parallel", "parallel")),
    )(gate, *o3, dm)


def _split_w_in(w_in, d, tc):
    cw = d // 2
    heads = cw // FOX_HEAD_DIM
    k = w_in.shape[0]
    o = 0
    conv = w_in[:, o:o + 3 * cw]; o += 3 * cw
    fox = w_in[:, o:o + 3 * cw]; o += 3 * cw
    f = w_in[:, o:o + heads]; o += heads
    mq = w_in[:, o:o + cw]; o += cw
    gate = w_in[:, o:o + N_BRANCHES * d]
    conv = conv.reshape(k, 3, cw // LANES, LANES).transpose(0, 2, 1, 3).reshape(k, 3 * cw)
    fox = fox.reshape(k, 3, heads, FOX_HEAD_DIM).transpose(0, 2, 1, 3).reshape(k, 3 * cw)
    gate = gate.reshape(k, N_BRANCHES, d // tc, tc).transpose(0, 2, 1, 3).reshape(k, N_BRANCHES * d)
    f_t = jnp.pad(f.T, ((0, F_ROWS - heads), (0, 0)))
    return conv, fox, f_t, mq, gate


def _join_w_in(conv, fox, f_t, mq, gate, d, tc):
    cw = d // 2
    heads = cw // FOX_HEAD_DIM
    k = conv.shape[0]
    conv = conv.reshape(k, cw // LANES, 3, LANES).transpose(0, 2, 1, 3).reshape(k, 3 * cw)
    fox = fox.reshape(k, heads, 3, FOX_HEAD_DIM).transpose(0, 2, 1, 3).reshape(k, 3 * cw)
    gate = gate.reshape(k, d // tc, N_BRANCHES, tc).transpose(0, 2, 1, 3).reshape(k, N_BRANCHES * d)
    return jnp.concatenate([conv, fox, f_t[:heads].T, mq, gate], axis=1)


def _local_step(x, mem, target, w, small):
    t, d = x.shape
    cw = d // 2
    heads = cw // FOX_HEAD_DIM
    tc = min(512, d)
    tq = min(512, t)
    w_conv, w_fox, w_f, w_mq, w_gate = _split_w_in(w["w_in"], d, tc)
    big = dict(tm=1024, tn=512, tk=2048)

    h = _rms_fwd("rms1_fwd", x, small["norm1_g"])
    p_conv = _matmul("proj_conv", "nn", h, w_conv, outs=[BF16], **big)
    p_fox = _matmul("proj_fox", "nn", h, w_fox, outs=[BF16], **big)
    p_mq = _matmul("proj_mq", "nn", h, w_mq, outs=[BF16], **big)
    gate = _matmul("proj_gate", "nn", h, w_gate, outs=[BF16], **big)
    z_row = _matmul("proj_f", "nt", w_f, h, outs=[F32], tm=F_ROWS, tn=512, tk=2048)

    y_conv = _conv_fwd(p_conv, small["conv_w"], LANES)

    b_col = jnp.pad(small["b_f"], (0, F_ROWS - heads)).reshape(F_ROWS, 1)
    c_row3 = _forget_fwd(z_row, b_col)[:heads].reshape(heads, 1, t)
    c_colb = _rows_to_colb(c_row3, tq)
    y_fox, lse = _fox_fwd(p_fox, small["fox_q_g"], small["fox_k_g"], c_row3, c_colb, heads, tq)

    nm = _rms_fwd("mem_rms_fwd", mem, small["mem_norm_g"])
    kv = _matmul("mem_kv", "nn", nm, w["w_mem_kv"], outs=[F32], tm=256, tn=512, tk=2048)
    y_mem = _mem_fwd(p_mq, kv, small["mem_q_g"], small["mem_k_g"], tq)

    ys = (y_conv, y_fox, y_mem)
    w_outs = (w["w_conv_out"], w["w_fox_out"], w["w_mem_out"])
    o3 = [_matmul(f"branch_out{s}", "nn", ys[s], w_outs[s], outs=[BF16], b_blocks=True, tm=1024, tn=256, tk=1024)
          for s in range(3)]
    merged = _merge_fwd(gate, o3, 512, tc)
    x1 = _matmul("out_proj", "nn", merged, w["w_out"], outs=[F32], extras=[x],
                 epilogue=lambda acc, xr: (acc + xr,), **big)
    h2 = _rms_fwd("rms2_fwd", x1, small["norm2_g"])

    def up_epilogue(acc):
        return acc, jnp.square(jnp.maximum(acc, 0.0))

    up, act = _matmul("mlp_up", "nn", h2, w["w_up"], outs=[BF16, BF16], epilogue=up_epilogue, b_blocks=True, **big)

    def loss_epilogue(acc, x1r, tr):
        dy = (acc + x1r - tr) * (1.0 / d)
        return dy, dy

    dy, dyb = _matmul("mlp_down", "nn", act, w["w_down"], outs=[F32, BF16], extras=[x1, target],
                      epilogue=loss_epilogue, tm=1024, tn=512, tk=1024)

    g = {}
    g["w_down"] = _matmul("d_w_down", "tn", act, dyb, outs=[BF16], tm=512, tn=1024, tk=1024)

    def dup_epilogue(acc, upr):
        return (acc * 2.0 * jnp.maximum(upr.astype(F32), 0.0),)

    dup = _matmul("d_act", "nt", dyb, w["w_down"], outs=[BF16], extras=[up], epilogue=dup_epilogue, **big)
    g["w_up"] = _matmul("d_w_up", "tn", h2, dup, outs=[BF16], out_blocks=True, tm=512, tn=1024, tk=1024)
    dh2 = _matmul("d_h2", "nt", dup, w["w_up"], outs=[F32], b_blocks=True, tm=1024, tn=512, tk=1024)
    dx1, dx1b, g_norm2 = _rms_bwd("rms2_bwd", dh2, x1, small["norm2_g"], res=dy)
    loss = _loss(dy, d)

    g["w_out"] = _matmul("d_w_out", "tn", merged, dx1b, outs=[BF16], tm=512, tn=1024, tk=1024)
    dmerged = _matmul("d_merged", "nt", dx1b, w["w_out"], outs=[BF16], **big)
    dgate, *do3 = _merge_bwd(gate, o3, dmerged, 512, tc)
    names = ("w_conv_out", "w_fox_out", "w_mem_out")
    dys = []
    for s in range(3):
        g[names[s]] = _matmul(f"d_w_branch{s}", "tn", ys[s], do3[s], outs=[BF16], out_blocks=True, tm=512, tn=256, tk=1024)
        dys.append(_matmul(f"d_branch{s}", "nt", do3[s], w_outs[s], outs=[BF16], b_blocks=True, tm=1024, tn=512, tk=256))

    dp_conv, g_conv_w = _conv_bwd(p_conv, small["conv_w"], dys[0], LANES)

    dp_fox, dc, g_fq, g_fk = _fox_bwd(p_fox, y_fox, dys[1], small["fox_q_g"], small["fox_k_g"], c_row3, c_colb, lse, heads, tq)
    dc_row = jnp.pad(dc.reshape(heads, t), ((0, F_ROWS - heads), (0, 0)))
    dz_row, db = _forget_bwd(z_row, b_col, dc_row)

    dp_mq, dkv, g_mq, g_mk = _mem_bwd(p_mq, kv, dys[2], small["mem_q_g"], small["mem_k_g"], tq)
    g["w_mem_kv"] = _matmul("d_w_mem_kv", "tn", nm, dkv, outs=[BF16], tm=512, tn=1024, tk=256)
    dnm = _matmul("d_mem_norm", "nt", dkv, w["w_mem_kv"], outs=[F32], tm=256, tn=512, tk=2048)
    _, _, g_mem_norm = _rms_bwd("mem_rms_bwd", dnm, mem, small["mem_norm_g"])

    dsegs = (dp_conv, dp_fox, dp_mq, dgate)
    wsegs = (w_conv, w_fox, w_mq, w_gate)
    gsegs = [_matmul(f"d_w_in{s}", "tn", h, dsegs[s], outs=[BF16], tm=512, tn=1024, tk=1024) for s in range(4)]
    g_wf = _matmul("d_w_f", "nn", dz_row, h, outs=[BF16], tm=F_ROWS, tn=512, tk=1024)
    dh = _matmul("d_h_f", "tn", dz_row, w_f, outs=[F32], tm=1024, tn=512, tk=F_ROWS)
    for s in range(4):
        dh = _matmul(f"d_h{s}", "nt", dsegs[s], wsegs[s], outs=[F32], extras=[dh],
                     epilogue=lambda acc, prev: (acc + prev,), tm=1024, tn=512, tk=1024)
    g["w_in"] = _join_w_in(gsegs[0], gsegs[1], g_wf, gsegs[2], gsegs[3], d, tc)
    grad_x, _, g_norm1 = _rms_bwd("rms1_bwd", dh, x, small["norm1_g"], res=dx1)

    gs = dict(norm1_g=g_norm1, b_f=db[:heads, 0], conv_w=g_conv_w, fox_q_g=g_fq.reshape(-1), fox_k_g=g_fk.reshape(-1),
              mem_norm_g=g_mem_norm, mem_q_g=g_mq, mem_k_g=g_mk, norm2_g=g_norm2)
    return loss, grad_x, g, gs


def _loss(dy, d):
    t = dy.shape[0]
    tm = _tile(t, 512)

    def body(dy_ref, out_ref):
        i = pl.program_id(0)

        @pl.when(i == 0)
        def _():
            out_ref[...] = jnp.zeros_like(out_ref)

        e = dy_ref[...]
        out_ref[...] += jnp.sum(jnp.sum(e * e, axis=0, keepdims=True), axis=1, keepdims=True) * (0.5 * d)

    out = pl.pallas_call(
        body,
        name="loss",
        grid=(t // tm,),
        in_specs=[pl.BlockSpec((tm, dy.shape[1]), lambda i: (i, 0))],
        out_specs=pl.BlockSpec((1, 1), lambda i: (0, 0)),
        out_shape=jax.ShapeDtypeStruct((1, 1), F32),
        compiler_params=_params(("arbitrary",)),
    )(dy)
    return out[0, 0]


def _position():
    return lax.axis_index("x"), lax.axis_index("y"), lax.axis_index("c")


def _index(px, py, pc):
    return 4 * px + 2 * py + pc


def _all_gather(shards):
    n = len(shards)

    def body(*refs):
        ins, outs = refs[:n], refs[n:2 * n]
        send_sems, recv_sems, local_sems = refs[2 * n:]
        x, y, c = _position()
        me, sibling = (x, y, c), (x, y, 1 - c)
        chips = [(1 - x, y), (x, 1 - y), (1 - x, 1 - y)]

        def copy(a, k, block, to, src=None):
            rows = outs[a].at[_index(*block)]
            return pltpu.make_async_remote_copy(
                src_ref=rows if src is None else src, dst_ref=rows,
                send_sem=send_sems.at[a, k], recv_sem=recv_sems.at[a, k], device_id=to, device_id_type=MESH)

        mine = [pltpu.make_async_copy(ins[a], outs[a].at[_index(*me)], local_sems.at[a]) for a in range(n)]
        for cp in mine:
            cp.start()
        first = []
        for a in range(n):
            first.append(copy(a, 0, me, sibling, src=ins[a]))
            first += [copy(a, 1 + j, me, (*chip, c), src=ins[a]) for j, chip in enumerate(chips)]
        for cp in first:
            cp.start()
        passed = []
        for a in range(n):
            for j, chip in enumerate(chips):
                copy(a, 1 + j, (*chip, c), me).wait_recv()
                fwd = copy(a, 4 + j, (*chip, c), sibling)
                fwd.start()
                passed.append(fwd)
        for a in range(n):
            copy(a, 0, sibling, me).wait_recv()
            for j, chip in enumerate(chips):
                copy(a, 4 + j, (*chip, 1 - c), me).wait_recv()
        for cp in first + passed:
            cp.wait_send()
        for cp in mine:
            cp.wait()

    return pl.pallas_call(
        body,
        name="all_gather",
        in_specs=[ANY] * n,
        out_specs=[ANY] * n,
        out_shape=[jax.ShapeDtypeStruct((N_DEV,) + s.shape, s.dtype) for s in shards],
        scratch_shapes=[pltpu.SemaphoreType.DMA((n, 7)), pltpu.SemaphoreType.DMA((n, 7)), pltpu.SemaphoreType.DMA((n,))],
    )(*shards)


def _exchange(blocks, whole):
    nb, n = len(blocks), len(blocks) + len(whole)
    srcs = list(blocks) + list(whole)

    def body(*refs):
        ins, outs = refs[:n], refs[n:2 * n]
        send_sems, recv_sems, local_sems = refs[2 * n:]
        x, y, c = _position()
        me = _index(x, y, c)

        def peer(k):
            return (1 - x if k & 4 else x, 1 - y if k & 2 else y, 1 - c if k & 1 else c)

        def src(a, to):
            return ins[a].at[_index(*to)] if a < nb else ins[a]

        def copy(a, k):
            to = peer(k)
            return pltpu.make_async_remote_copy(
                src_ref=src(a, to), dst_ref=outs[a].at[me],
                send_sem=send_sems.at[a, k - 1], recv_sem=recv_sems.at[a, k - 1], device_id=to, device_id_type=MESH)

        def arrival(a, k):
            return pltpu.make_async_remote_copy(
                src_ref=src(a, peer(k)), dst_ref=outs[a].at[_index(*peer(k))],
                send_sem=send_sems.at[a, k - 1], recv_sem=recv_sems.at[a, k - 1], device_id=peer(k), device_id_type=MESH)

        mine = [pltpu.make_async_copy(src(a, (x, y, c)), outs[a].at[me], local_sems.at[a]) for a in range(n)]
        for cp in mine:
            cp.start()
        sends = [copy(a, k) for a in range(n) for k in range(1, N_DEV)]
        for cp in sends:
            cp.start()
        for a in range(n):
            for k in range(1, N_DEV):
                arrival(a, k).wait_recv()
        for cp in sends:
            cp.wait_send()
        for cp in mine:
            cp.wait()

    out_shape = [jax.ShapeDtypeStruct(b.shape, b.dtype) for b in blocks]
    out_shape += [jax.ShapeDtypeStruct((N_DEV,) + v.shape, v.dtype) for v in whole]
    return pl.pallas_call(
        body,
        name="exchange_grads",
        in_specs=[ANY] * n,
        out_specs=[ANY] * n,
        out_shape=out_shape,
        scratch_shapes=[pltpu.SemaphoreType.DMA((n, 7)), pltpu.SemaphoreType.DMA((n, 7)), pltpu.SemaphoreType.DMA((n,))],
    )(*srcs)


def _adamw_math(w, g, m, v):
    m = ADAM_B1 * m + (1.0 - ADAM_B1) * g
    v = ADAM_B2 * v + (1.0 - ADAM_B2) * jnp.square(g)
    m_hat = m / (1.0 - ADAM_B1 ** ADAM_STEP)
    v_hat = v / (1.0 - ADAM_B2 ** ADAM_STEP)
    delta = -ADAM_LR * (m_hat / (jnp.sqrt(v_hat) + ADAM_EPS) + ADAM_WD * w)
    return delta, m, v


def _adamw(name, parts, w, m, v, tr):
    r, c = w.shape
    tr = _tile(r, tr)
    n_parts = parts.shape[0]

    def body(p_ref, w_ref, m_ref, v_ref, g_ref, d_ref, nm_ref, nv_ref):
        g = p_ref[0].astype(F32)
        for s in range(1, n_parts):
            g = g + p_ref[s].astype(F32)
        delta, nm, nv = _adamw_math(w_ref[...], g, m_ref[...], v_ref[...])
        g_ref[...] = g
        d_ref[...] = delta
        nm_ref[...] = nm
        nv_ref[...] = nv

    blk = pl.BlockSpec((tr, c), lambda i: (i, 0))
    return pl.pallas_call(
        body,
        name=name,
        grid=(r // tr,),
        in_specs=[pl.BlockSpec((n_parts, tr, c), lambda i: (0, i, 0)), blk, blk, blk],
        out_specs=[blk] * 4,
        out_shape=[jax.ShapeDtypeStruct((r, c), F32)] * 4,
        compiler_params=_params(("parallel",)),
    )(parts, w, m, v)


def _sum_parts(name, parts):
    n_parts, r, c = parts.shape

    def body(p_ref, o_ref):
        acc = p_ref[0]
        for s in range(1, n_parts):
            acc = acc + p_ref[s]
        o_ref[...] = acc

    return pl.pallas_call(body, name=name, out_shape=jax.ShapeDtypeStruct((r, c), F32))(parts)


BIG = ("w_in", "w_mem_kv", "w_conv_out", "w_fox_out", "w_mem_out", "w_out", "w_up", "w_down")
COLUMN_SPLIT = ("w_in", "w_conv_out", "w_fox_out", "w_mem_out", "w_up")
SMALL = ("norm1_g", "b_f", "conv_w", "fox_q_g", "fox_k_g", "mem_norm_g", "mem_q_g", "mem_k_g", "norm2_g")
WEIGHTS = ("norm1_g", "w_in", "b_f", "conv_w", "fox_q_g", "fox_k_g", "mem_norm_g", "w_mem_kv", "mem_q_g", "mem_k_g",
           "w_conv_out", "w_fox_out", "w_mem_out", "w_out", "norm2_g", "w_up", "w_down")


def _pack(vectors):
    rows = []
    for vec in vectors:
        n = vec.shape[0]
        rows.append(jnp.pad(vec, (0, -n % LANES)).reshape(-1, LANES))
    out = jnp.concatenate(rows, axis=0)
    return jnp.pad(out, ((0, -out.shape[0] % 8), (0, 0)))


def _unpack(packed, sizes):
    out, row = [], 0
    for n in sizes:
        nr = -(-n // LANES)
        out.append(packed[row:row + nr].reshape(-1)[:n])
        row += nr
    return out


def kernel(x, mem, norm1_g, w_in, b_f, conv_w, fox_q_g, fox_k_g, mem_norm_g, w_mem_kv, mem_q_g, mem_k_g, w_conv_out, w_fox_out, w_mem_out, w_out, norm2_g, w_up, w_down, loss_target, m_norm1_g, m_w_in, m_b_f, m_conv_w, m_fox_q_g, m_fox_k_g, m_mem_norm_g, m_w_mem_kv, m_mem_q_g, m_mem_k_g, m_w_conv_out, m_w_fox_out, m_w_mem_out, m_w_out, m_norm2_g, m_w_up, m_w_down, v_norm1_g, v_w_in, v_b_f, v_conv_w, v_fox_q_g, v_fox_k_g, v_mem_norm_g, v_w_mem_kv, v_mem_q_g, v_mem_k_g, v_w_conv_out, v_w_fox_out, v_w_mem_out, v_w_out, v_norm2_g, v_w_up, v_w_down):
    args = dict(locals())
    wts = {n: args[n] for n in WEIGHTS}
    ms = {n: args["m_" + n] for n in WEIGHTS}
    vs = {n: args["v_" + n] for n in WEIGHTS}
    d = x.shape[-1]
    me = _index(*_position())

    gathered = _all_gather([wts[n].astype(BF16) for n in BIG] + [conv_w])
    full = dict(zip(BIG, gathered[:-1]))
    conv_w_full = gathered[-1].transpose(1, 0, 2).reshape(CONV_TAPS, -1)
    wi = full["w_in"]
    full["w_in"] = wi.transpose(1, 0, 2).reshape(wi.shape[1], -1)
    for n in ("w_mem_kv", "w_out", "w_down"):
        full[n] = full[n].reshape(-1, full[n].shape[-1])
    small = {n: wts[n] for n in SMALL}
    small["conv_w"] = conv_w_full

    loss, grad_x, g, gs = _local_step(x[0], mem[0], loss_target[0], full, small)

    gi = g["w_in"]
    g["w_in"] = gi.reshape(gi.shape[0], N_DEV, -1).transpose(1, 0, 2)
    for n in ("w_mem_kv", "w_out", "w_down"):
        g[n] = g[n].reshape(N_DEV, -1, g[n].shape[-1])
    small_sizes = [int(math.prod(gs[n].shape)) for n in SMALL]
    packed = _pack([gs[n].reshape(-1) for n in SMALL])
    received = _exchange([g[n] for n in BIG], [packed])
    parts = dict(zip(BIG, received[:-1]))

    out_g, out_d, out_m, out_v = {}, {}, {}, {}
    for n in BIG:
        out_g[n], out_d[n], out_m[n], out_v[n] = _adamw("adamw_" + n, parts[n], wts[n], ms[n], vs[n], 128)

    gsum = _sum_parts("sum_small", received[-1])
    gsmall = dict(zip(SMALL, _unpack(gsum, small_sizes)))
    cols = conv_w.shape[1]
    gsmall["conv_w"] = lax.dynamic_slice(gsmall["conv_w"].reshape(CONV_TAPS, -1), (0, me * cols), (CONV_TAPS, cols)).reshape(-1)
    pg, pw, pm, pv = (_pack([src[n].reshape(-1) for n in SMALL]) for src in (gsmall, wts, ms, vs))
    _, sd, sm, sv = _adamw("adamw_small", pg[None], pw, pm, pv, pw.shape[0])
    local_sizes = [int(math.prod(wts[n].shape)) for n in SMALL]
    for dst, src in ((out_d, sd), (out_m, sm), (out_v, sv)):
        for n, val in zip(SMALL, _unpack(src, local_sizes)):
            dst[n] = val.reshape(wts[n].shape)
    for n in SMALL:
        out_g[n] = gsmall[n].reshape(wts[n].shape)

    loss = lax.psum(loss, MESH_AXES)
    return (loss, grad_x[None], *[out_g[n] for n in WEIGHTS], *[out_d[n] for n in WEIGHTS],
            *[out_m[n] for n in WEIGHTS], *[out_v[n] for n in WEIGHTS])
```

```python
import math

import jax
import jax.numpy as jnp
from jax import lax
from jax.experimental import pallas as pl
from jax.experimental.pallas import tpu as pltpu

F32 = jnp.float32
BF16 = jnp.bfloat16

EPS = 1e-6
N_DEV = 8
N_CHIPS = 4
FOX_HEAD_DIM = 128
MEM_HEADS = 4
CONV_TAPS = 3
N_BRANCHES = 3
F_ROWS = 16

ADAM_LR = 0.001
ADAM_B1 = 0.9
ADAM_B2 = 0.999
ADAM_EPS = 1e-08
ADAM_WD = 0.01
ADAM_STEP = 10

V7X_VMEM_BYTES = 64 * 1024 * 1024
VMEM_LIMIT = V7X_VMEM_BYTES * 3 // 4
LANES = 128
NEG = -1e30

MESH_AXES = ("x", "y", "c")
MESH = pl.DeviceIdType.MESH
ANY = pl.BlockSpec(memory_space=pl.ANY)

NN = (((1,), (0,)), ((), ()))
NT = (((1,), (1,)), ((), ()))
TN = (((0,), (0,)), ((), ()))


def _params(sem):
    return pltpu.CompilerParams(dimension_semantics=sem, vmem_limit_bytes=VMEM_LIMIT)


def _dot(a, b, dn):
    return lax.dot_general(a, b, dn, preferred_element_type=F32)


def _tile(n, t):
    if n <= t:
        return n
    for cand in range(t - t % LANES, 0, -LANES):
        if n % cand == 0:
            return cand
    raise ValueError((n, t))


class _Rider:
    def __init__(self, ins, out_shapes, sem_shapes, start, finish):
        self.ins, self.out_shapes, self.sem_shapes = list(ins), list(out_shapes), list(sem_shapes)
        self.start, self.finish = start, finish


def _position():
    return lax.axis_index("x"), lax.axis_index("y"), lax.axis_index("c")


def _index(px, py, pc):
    return 4 * px + 2 * py + pc


def _dma_sems(n, per):
    return [pltpu.SemaphoreType.DMA((n, per)), pltpu.SemaphoreType.DMA((n, per)), pltpu.SemaphoreType.DMA((n,))]


def _gather_rider(shards):
    n = len(shards)

    def copies(ins, outs, sems):
        send_sems, recv_sems, local_sems = sems
        x, y, c = _position()
        me, sibling = (x, y, c), (x, y, 1 - c)
        chips = [(1 - x, y), (x, 1 - y), (1 - x, 1 - y)]

        def copy(a, k, block, to, src=None):
            rows = outs[a].at[_index(*block)]
            return pltpu.make_async_remote_copy(
                src_ref=rows if src is None else src, dst_ref=rows,
                send_sem=send_sems.at[a, k], recv_sem=recv_sems.at[a, k], device_id=to, device_id_type=MESH)

        mine = [pltpu.make_async_copy(ins[a], outs[a].at[_index(*me)], local_sems.at[a]) for a in range(n)]
        first = []
        for a in range(n):
            first.append(copy(a, 0, me, sibling, src=ins[a]))
            first += [copy(a, 1 + j, me, (*chip, c), src=ins[a]) for j, chip in enumerate(chips)]
        return copy, mine, first, me, sibling, chips, c

    def start(ins, outs, sems):
        _, mine, first, *_ = copies(ins, outs, sems)
        for cp in mine + first:
            cp.start()

    def finish(ins, outs, sems):
        copy, mine, first, me, sibling, chips, c = copies(ins, outs, sems)
        passed = []
        for a in range(n):
            for j, chip in enumerate(chips):
                copy(a, 1 + j, (*chip, c), me).wait_recv()
                fwd = copy(a, 4 + j, (*chip, c), sibling)
                fwd.start()
                passed.append(fwd)
        for a in range(n):
            copy(a, 0, sibling, me).wait_recv()
            for j, chip in enumerate(chips):
                copy(a, 4 + j, (*chip, 1 - c), me).wait_recv()
        for cp in first + passed:
            cp.wait_send()
        for cp in mine:
            cp.wait()

    out_shapes = [jax.ShapeDtypeStruct((N_DEV,) + s.shape, s.dtype) for s in shards]
    return _Rider(shards, out_shapes, _dma_sems(n, 7), start, finish)


def _pair_rider(grads):
    n = len(grads)

    def copies(ins, outs, sems):
        send_sems, recv_sems, _ = sems
        x, y, c = _position()
        return [pltpu.make_async_remote_copy(
            src_ref=ins[a].at[2 * q + (1 - c)], dst_ref=outs[a].at[q],
            send_sem=send_sems.at[a, q], recv_sem=recv_sems.at[a, q], device_id=(x, y, 1 - c), device_id_type=MESH)
            for a in range(n) for q in range(N_CHIPS)]

    def start(ins, outs, sems):
        for cp in copies(ins, outs, sems):
            cp.start()

    def finish(ins, outs, sems):
        cps = copies(ins, outs, sems)
        for cp in cps:
            cp.wait_recv()
        for cp in cps:
            cp.wait_send()

    out_shapes = [jax.ShapeDtypeStruct((N_CHIPS,) + g.shape[1:], g.dtype) for g in grads]
    return _Rider(grads, out_shapes, _dma_sems(n, N_CHIPS), start, finish)


def _chip_rider(parts):
    n = len(parts)

    def copies(ins, outs, sems):
        send_sems, recv_sems, local_sems = sems
        x, y, c = _position()
        q_me = 2 * x + y
        chips = [(1 - x, y), (x, 1 - y), (1 - x, 1 - y)]
        mine = [pltpu.make_async_copy(ins[a].at[q_me], outs[a].at[q_me], local_sems.at[a]) for a in range(n)]
        sends, arrivals = [], []
        for a in range(n):
            for j, (tx, ty) in enumerate(chips):
                q_t = 2 * tx + ty
                sends.append(pltpu.make_async_remote_copy(
                    src_ref=ins[a].at[q_t], dst_ref=outs[a].at[q_me],
                    send_sem=send_sems.at[a, j], recv_sem=recv_sems.at[a, j], device_id=(tx, ty, c), device_id_type=MESH))
                arrivals.append(pltpu.make_async_remote_copy(
                    src_ref=ins[a].at[q_t], dst_ref=outs[a].at[q_t],
                    send_sem=send_sems.at[a, j], recv_sem=recv_sems.at[a, j], device_id=(tx, ty, c), device_id_type=MESH))
        return mine, sends, arrivals

    def start(ins, outs, sems):
        mine, sends, _ = copies(ins, outs, sems)
        for cp in mine + sends:
            cp.start()

    def finish(ins, outs, sems):
        mine, sends, arrivals = copies(ins, outs, sems)
        for cp in arrivals:
            cp.wait_recv()
        for cp in sends:
            cp.wait_send()
        for cp in mine:
            cp.wait()

    out_shapes = [jax.ShapeDtypeStruct(p.shape, p.dtype) for p in parts]
    return _Rider(parts, out_shapes, _dma_sems(n, 3), start, finish)


def _broadcast_rider(values):
    n = len(values)

    def copies(ins, outs, sems):
        send_sems, recv_sems, local_sems = sems
        x, y, c = _position()
        me = _index(x, y, c)

        def peer(k):
            return (1 - x if k & 4 else x, 1 - y if k & 2 else y, 1 - c if k & 1 else c)

        mine = [pltpu.make_async_copy(ins[a], outs[a].at[me], local_sems.at[a]) for a in range(n)]
        sends, arrivals = [], []
        for a in range(n):
            for k in range(1, N_DEV):
                common = dict(send_sem=send_sems.at[a, k - 1], recv_sem=recv_sems.at[a, k - 1], device_id=peer(k), device_id_type=MESH)
                sends.append(pltpu.make_async_remote_copy(src_ref=ins[a], dst_ref=outs[a].at[me], **common))
                arrivals.append(pltpu.make_async_remote_copy(src_ref=ins[a], dst_ref=outs[a].at[_index(*peer(k))], **common))
        return mine, sends, arrivals

    def start(ins, outs, sems):
        mine, sends, _ = copies(ins, outs, sems)
        for cp in mine + sends:
            cp.start()

    def finish(ins, outs, sems):
        mine, sends, arrivals = copies(ins, outs, sems)
        for cp in arrivals:
            cp.wait_recv()
        for cp in sends:
            cp.wait_send()
        for cp in mine:
            cp.wait()

    out_shapes = [jax.ShapeDtypeStruct((N_DEV,) + v.shape, v.dtype) for v in values]
    return _Rider(values, out_shapes, _dma_sems(n, 7), start, finish)


def _run_rider(name, rider):
    n_in, n_out = len(rider.ins), len(rider.out_shapes)

    def body(*refs):
        ins, outs, sems = refs[:n_in], refs[n_in:n_in + n_out], refs[n_in + n_out:]
        rider.start(ins, outs, sems)
        rider.finish(ins, outs, sems)

    return pl.pallas_call(
        body, name=name, in_specs=[ANY] * n_in, out_specs=[ANY] * n_out, out_shape=rider.out_shapes,
        scratch_shapes=rider.sem_shapes)(*rider.ins)


class _Host:
    def __init__(self, rider):
        self.rider = rider
        self.n_in = len(rider.ins) if rider else 0
        self.n_out = len(rider.out_shapes) if rider else 0
        self.n_sem = len(rider.sem_shapes) if rider else 0
        self.ins = rider.ins if rider else []
        self.in_specs = [ANY] * self.n_in
        self.out_specs = [ANY] * self.n_out
        self.out_shapes = rider.out_shapes if rider else []
        self.scratch = rider.sem_shapes if rider else []

    def run(self, first, last, ins, outs, sems, compute):
        if self.rider is None:
            compute()
            return

        @pl.when(first)
        def _():
            self.rider.start(ins, outs, sems)

        compute()

        @pl.when(last)
        def _():
            self.rider.finish(ins, outs, sems)


def _matmul(name, kind, a, b, *, tm, tn, tk, outs, epilogue=None, extras=(), out_blocks=False, rider=None):
    if kind == "nn":
        (m, kdim), n = a.shape, b.shape[1]
    elif kind == "nt":
        (m, kdim), n = a.shape, b.shape[0]
    else:
        (kdim, m), n = a.shape, b.shape[1]
    if out_blocks:
        tn = min(tn, n // N_DEV)
    tm, tn, tk = _tile(m, tm), _tile(n, tn), _tile(kdim, tk)
    ni, nj, nk = m // tm, n // tn, kdim // tk

    a_spec = pl.BlockSpec((tk, tm), lambda i, j, k: (k, i)) if kind == "tn" else pl.BlockSpec((tm, tk), lambda i, j, k: (i, k))
    b_spec = pl.BlockSpec((tn, tk), lambda i, j, k: (j, k)) if kind == "nt" else pl.BlockSpec((tk, tn), lambda i, j, k: (k, j))
    dn = {"nn": NN, "nt": NT, "tn": TN}[kind]

    tile_spec = pl.BlockSpec((tm, tn), lambda i, j, k: (i, j))
    if out_blocks:
        width = n // N_DEV
        r_out = width // tn
        out_shape = [jax.ShapeDtypeStruct((N_DEV, m, width), dt) for dt in outs]
        out_specs = [pl.BlockSpec((None, tm, tn), lambda i, j, k: (j // r_out, i, j % r_out)) for _ in outs]
    else:
        out_shape = [jax.ShapeDtypeStruct((m, n), dt) for dt in outs]
        out_specs = [tile_spec for _ in outs]
    n_ex, n_out = len(extras), len(outs)
    host = _Host(rider)
    n_acc = 1 if nk > 1 else 0

    def body(*refs):
        a_ref, b_ref = refs[0], refs[1]
        pos = 2
        ex_refs = refs[pos:pos + n_ex]; pos += n_ex
        r_ins = refs[pos:pos + host.n_in]; pos += host.n_in
        out_refs = refs[pos:pos + n_out]; pos += n_out
        r_outs = refs[pos:pos + host.n_out]; pos += host.n_out
        acc_ref = refs[pos] if n_acc else None
        sems = refs[pos + n_acc:]
        i, j, k = pl.program_id(0), pl.program_id(1), pl.program_id(2)

        def finish_tile(acc):
            vals = (acc,) if epilogue is None else epilogue(acc, *[e[...] for e in ex_refs])
            for o_ref, v in zip(out_refs, vals):
                o_ref[...] = v.astype(o_ref.dtype)

        def compute():
            part = _dot(a_ref[...], b_ref[...], dn)
            if nk == 1:
                finish_tile(part)
                return

            @pl.when(k == 0)
            def _():
                acc_ref[...] = part

            @pl.when(jnp.logical_and(k > 0, k < nk - 1))
            def _():
                acc_ref[...] += part

            @pl.when(k == nk - 1)
            def _():
                finish_tile(acc_ref[...] + part)

        first = jnp.logical_and(jnp.logical_and(i == 0, j == 0), k == 0)
        last = jnp.logical_and(jnp.logical_and(i == ni - 1, j == nj - 1), k == nk - 1)
        host.run(first, last, r_ins, r_outs, sems, compute)

    sem = ("arbitrary",) * 3 if rider else ("parallel", "parallel", "arbitrary")
    res = pl.pallas_call(
        body,
        name=name,
        grid=(ni, nj, nk),
        in_specs=[a_spec, b_spec] + [tile_spec for _ in extras] + host.in_specs,
        out_specs=out_specs + host.out_specs,
        out_shape=out_shape + host.out_shapes,
        scratch_shapes=([pltpu.VMEM((tm, tn), F32)] if n_acc else []) + host.scratch,
        compiler_params=_params(sem),
    )(a, b, *extras, *host.ins)
    return res[0] if len(res) == 1 else res


def _rms_fwd(name, x, g, tm=512):
    t, d = x.shape
    tm = _tile(t, tm)

    def body(x_ref, g_ref, h_ref):
        xf = x_ref[...]
        r = lax.rsqrt(jnp.mean(xf * xf, axis=-1, keepdims=True) + EPS)
        h_ref[...] = (xf * r * g_ref[...]).astype(h_ref.dtype)

    return pl.pallas_call(
        body,
        name=name,
        grid=(t // tm,),
        in_specs=[pl.BlockSpec((tm, d), lambda i: (i, 0)), pl.BlockSpec((1, d), lambda i: (0, 0))],
        out_specs=pl.BlockSpec((tm, d), lambda i: (i, 0)),
        out_shape=jax.ShapeDtypeStruct((t, d), BF16),
        compiler_params=_params(("parallel",)),
    )(x, g.reshape(1, d))


def _rms_bwd(name, dh, x, g, res=None, tm=256):
    t, d = x.shape
    tm = _tile(t, tm)
    has_res = res is not None

    def body(*refs):
        if has_res:
            dh_ref, x_ref, g_ref, res_ref, dx_ref, dxb_ref, gg_ref = refs
        else:
            dh_ref, x_ref, g_ref, dx_ref, dxb_ref, gg_ref = refs
        i = pl.program_id(0)
        xf = x_ref[...]
        r = lax.rsqrt(jnp.mean(xf * xf, axis=-1, keepdims=True) + EPS)
        xh = xf * r
        dhf = dh_ref[...].astype(F32)
        dxh = dhf * g_ref[...]
        dx = r * (dxh - xh * jnp.mean(dxh * xh, axis=-1, keepdims=True))
        if has_res:
            dx = dx + res_ref[...]
        dx_ref[...] = dx
        dxb_ref[...] = dx.astype(BF16)

        @pl.when(i == 0)
        def _():
            gg_ref[...] = jnp.zeros_like(gg_ref)

        gg_ref[...] += jnp.sum(dhf * xh, axis=0, keepdims=True)

    row = pl.BlockSpec((tm, d), lambda i: (i, 0))
    vec = pl.BlockSpec((1, d), lambda i: (0, 0))
    ins = [dh, x, g.reshape(1, d)] + ([res] if has_res else [])
    dx, dxb, gg = pl.pallas_call(
        body,
        name=name,
        grid=(t // tm,),
        in_specs=[row, row, vec] + ([row] if has_res else []),
        out_specs=[row, row, vec],
        out_shape=[jax.ShapeDtypeStruct((t, d), F32), jax.ShapeDtypeStruct((t, d), BF16), jax.ShapeDtypeStruct((1, d), F32)],
        compiler_params=_params(("arbitrary",)),
    )(*ins)
    return dx, dxb, gg.reshape(d)


def _head_rms(xf):
    r = lax.rsqrt(jnp.mean(xf * xf, axis=-1, keepdims=True) + EPS)
    return xf * r, r


def _head_rms_bwd(dy, xn, r, g):
    dxh = dy * g
    dx = r * (dxh - xn * jnp.mean(dxh * xn, axis=-1, keepdims=True))
    return dx, jnp.sum(dy * xn, axis=0, keepdims=True)


def _col_to_row(col):
    n = col.shape[0]
    eye = lax.broadcasted_iota(jnp.int32, (n, n), 0) == lax.broadcasted_iota(jnp.int32, (n, n), 1)
    return jnp.sum(jnp.where(eye, col, 0.0), axis=0, keepdims=True)


def _row_to_col(row):
    n = row.shape[1]
    eye = lax.broadcasted_iota(jnp.int32, (n, n), 0) == lax.broadcasted_iota(jnp.int32, (n, n), 1)
    return jnp.sum(jnp.where(eye, row, 0.0), axis=1, keepdims=True)


def _dproj_args(dproj, n_in):
    if dproj is None:
        return [], [], {}
    return [dproj], [ANY], {n_in: 0}


def _shift_down(u, s, rows):
    return jnp.where(rows >= s, pltpu.roll(u, s, axis=0), 0.0)


def _shift_up(u, s, rows, t):
    return jnp.where(rows < t - s, pltpu.roll(u, t - s, axis=0), 0.0)


def _conv_fwd(proj, off, conv_w, cb):
    t = proj.shape[0]
    c = conv_w.shape[1]
    blk0 = off // (3 * cb)

    def body(p_ref, w_ref, y_ref):
        rows = lax.broadcasted_iota(jnp.int32, (t, cb), 0)
        bg = p_ref[:, 0:cb].astype(F32)
        u = p_ref[:, cb:2 * cb].astype(F32) * p_ref[:, 2 * cb:3 * cb].astype(F32)
        w = w_ref[...]
        conv = w[2:3] * u + w[1:2] * _shift_down(u, 1, rows) + w[0:1] * _shift_down(u, 2, rows)
        y_ref[...] = (bg * conv).astype(y_ref.dtype)

    return pl.pallas_call(
        body,
        name="conv_fwd",
        grid=(c // cb,),
        in_specs=[pl.BlockSpec((t, 3 * cb), lambda j: (0, blk0 + j)), pl.BlockSpec((CONV_TAPS, cb), lambda j: (0, j))],
        out_specs=pl.BlockSpec((t, cb), lambda j: (0, j)),
        out_shape=jax.ShapeDtypeStruct((t, c), BF16),
        compiler_params=_params(("parallel",)),
    )(proj, conv_w)


def _conv_bwd(proj, off, conv_w, dy, cb, dproj, rider=None):
    t = proj.shape[0]
    c = conv_w.shape[1]
    blk0 = off // (3 * cb)
    nj = c // cb
    host = _Host(rider)

    def body(*refs):
        p_ref, w_ref, dy_ref = refs[:3]
        r_ins = refs[4:4 + host.n_in]
        dp_ref, gw_ref = refs[4 + host.n_in:6 + host.n_in]
        r_outs = refs[6 + host.n_in:6 + host.n_in + host.n_out]
        sems = refs[6 + host.n_in + host.n_out:]
        j = pl.program_id(0)

        def compute():
            rows = lax.broadcasted_iota(jnp.int32, (t, cb), 0)
            bg = p_ref[:, 0:cb].astype(F32)
            cg = p_ref[:, cb:2 * cb].astype(F32)
            v = p_ref[:, 2 * cb:3 * cb].astype(F32)
            u = cg * v
            w = w_ref[...]
            u1 = _shift_down(u, 1, rows)
            u2 = _shift_down(u, 2, rows)
            conv = w[2:3] * u + w[1:2] * u1 + w[0:1] * u2
            dyf = dy_ref[...].astype(F32)
            dconv = dyf * bg
            du = w[2:3] * dconv + w[1:2] * _shift_up(dconv, 1, rows, t) + w[0:1] * _shift_up(dconv, 2, rows, t)
            dp_ref[:, 0:cb] = (dyf * conv).astype(dp_ref.dtype)
            dp_ref[:, cb:2 * cb] = (du * v).astype(dp_ref.dtype)
            dp_ref[:, 2 * cb:3 * cb] = (du * cg).astype(dp_ref.dtype)
            gw_ref[0:1, :] = jnp.sum(dconv * u2, axis=0, keepdims=True)
            gw_ref[1:2, :] = jnp.sum(dconv * u1, axis=0, keepdims=True)
            gw_ref[2:3, :] = jnp.sum(dconv * u, axis=0, keepdims=True)

        host.run(j == 0, j == nj - 1, r_ins, r_outs, sems, compute)

    res = pl.pallas_call(
        body,
        name="conv_bwd",
        grid=(nj,),
        in_specs=[
            pl.BlockSpec((t, 3 * cb), lambda j: (0, blk0 + j)),
            pl.BlockSpec((CONV_TAPS, cb), lambda j: (0, j)),
            pl.BlockSpec((t, cb), lambda j: (0, j)),
            ANY,
        ] + host.in_specs,
        out_specs=[pl.BlockSpec((t, 3 * cb), lambda j: (0, blk0 + j)), pl.BlockSpec((CONV_TAPS, cb), lambda j: (0, j))] + host.out_specs,
        out_shape=[jax.ShapeDtypeStruct(dproj.shape, dproj.dtype), jax.ShapeDtypeStruct((CONV_TAPS, c), F32)] + host.out_shapes,
        input_output_aliases={3: 0},
        scratch_shapes=host.scratch,
        compiler_params=_params(("arbitrary",)),
    )(proj, conv_w, dy, dproj, *host.ins)
    return res


def _lane_scan(x, reverse):
    lane = lax.broadcasted_iota(jnp.int32, x.shape, 1)
    s = 1
    while s < LANES:
        if reverse:
            x = x + jnp.where(lane < LANES - s, pltpu.roll(x, LANES - s, axis=1), 0.0)
        else:
            x = x + jnp.where(lane >= s, pltpu.roll(x, s, axis=1), 0.0)
        s *= 2
    return x


def _scan_rows(src_ref, dst_ref, t, reverse, fn=None):
    groups = list(range(t // LANES))
    if reverse:
        groups = groups[::-1]
    carry = None
    for gi in groups:
        sl = slice(gi * LANES, (gi + 1) * LANES)
        blk = src_ref[:, sl]
        if fn is not None:
            blk = fn(blk)
        blk = _lane_scan(blk, reverse)
        if carry is not None:
            blk = blk + carry
        dst_ref[:, sl] = blk
        carry = blk[:, 0:1] if reverse else blk[:, LANES - 1:LANES]


def _forget_fwd(z_row, b_col):
    rows, t = z_row.shape

    def body(z_ref, b_ref, c_ref):
        def logf(z):
            zz = z + b_ref[...]
            return jnp.minimum(zz, 0.0) - jnp.log(1.0 + jnp.exp(-jnp.abs(zz)))

        _scan_rows(z_ref, c_ref, t, False, logf)

    return pl.pallas_call(
        body,
        name="forget_fwd",
        out_shape=jax.ShapeDtypeStruct((rows, t), F32),
        compiler_params=pltpu.CompilerParams(vmem_limit_bytes=VMEM_LIMIT),
    )(z_row, b_col)


def _rows_to_colb(c_row3, tq):
    heads, _, t = c_row3.shape

    def body(r_ref, o_ref):
        o_ref[...] = jnp.broadcast_to(_row_to_col(r_ref[...]), (tq, LANES))

    return pl.pallas_call(
        body,
        name="rows_to_colb",
        grid=(heads, t // tq),
        in_specs=[pl.BlockSpec((None, 1, tq), lambda h, i: (h, 0, i))],
        out_specs=pl.BlockSpec((None, tq, LANES), lambda h, i: (h, i, 0)),
        out_shape=jax.ShapeDtypeStruct((heads, t, LANES), F32),
        compiler_params=_params(("parallel", "parallel")),
    )(c_row3)


def _forget_bwd(z_row, b_col, dc_row):
    rows, t = z_row.shape

    def body(z_ref, b_ref, dc_ref, dz_ref, db_ref, tmp_ref):
        _scan_rows(dc_ref, tmp_ref, t, True)
        zz = z_ref[...] + b_ref[...]
        dz = tmp_ref[...] * (1.0 / (1.0 + jnp.exp(zz)))
        dz_ref[...] = dz.astype(dz_ref.dtype)
        db_ref[...] = jnp.sum(dz, axis=1, keepdims=True)

    return pl.pallas_call(
        body,
        name="forget_bwd",
        out_shape=[jax.ShapeDtypeStruct((rows, t), BF16), jax.ShapeDtypeStruct((rows, 1), F32)],
        scratch_shapes=[pltpu.VMEM((rows, t), F32)],
        compiler_params=pltpu.CompilerParams(vmem_limit_bytes=VMEM_LIMIT),
    )(z_row, b_col, dc_row)


def _fox_fwd(proj, off, gq, gk, c_row3, c_colb, heads, tq, rider=None):
    t = proj.shape[0]
    hd = FOX_HEAD_DIM
    tq = _tile(t, tq)
    nq = t // tq
    blk0 = off // hd
    scale = 1.0 / math.sqrt(hd)
    host = _Host(rider)

    def body(*refs):
        q_ref, k_ref, v_ref, gq_ref, gk_ref, crow_ref, ccol_ref = refs[:7]
        r_ins = refs[7:7 + host.n_in]
        o_ref, lse_ref = refs[7 + host.n_in:9 + host.n_in]
        r_outs = refs[9 + host.n_in:9 + host.n_in + host.n_out]
        khat_ref = refs[9 + host.n_in + host.n_out]
        sems = refs[10 + host.n_in + host.n_out:]
        h, qi = pl.program_id(0), pl.program_id(1)

        def compute():
            @pl.when(qi == 0)
            def _():
                kn, _ = _head_rms(k_ref[...].astype(F32))
                khat_ref[...] = (kn * gk_ref[...]).astype(BF16)

            qn, _ = _head_rms(q_ref[...].astype(F32))
            qhat = (qn * gq_ref[...]).astype(BF16)
            ccol = ccol_ref[:, 0:1]
            rows = qi * tq + lax.broadcasted_iota(jnp.int32, (tq, tq), 0)

            def step(j, carry):
                m, l, acc = carry
                ks = pl.multiple_of(j * tq, tq)
                s = _dot(qhat, khat_ref[pl.ds(ks, tq), :], NT) * scale + (ccol - crow_ref[:, pl.ds(ks, tq)])
                cols = ks + lax.broadcasted_iota(jnp.int32, (tq, tq), 1)
                s = jnp.where(rows >= cols, s, NEG)
                m_new = jnp.maximum(m, jnp.max(s, axis=-1, keepdims=True))
                alpha = jnp.exp(m - m_new)
                p = jnp.exp(s - m_new)
                l = alpha * l + jnp.sum(p, axis=-1, keepdims=True)
                acc = alpha * acc + _dot(p.astype(BF16), v_ref[pl.ds(ks, tq), :], NN)
                return m_new, l, acc

            init = (jnp.full((tq, 1), NEG, F32), jnp.zeros((tq, 1), F32), jnp.zeros((tq, hd), F32))
            m, l, acc = lax.fori_loop(0, qi + 1, step, init)
            o_ref[...] = (acc / l).astype(o_ref.dtype)
            lse_ref[...] = jnp.broadcast_to(m + jnp.log(l), (tq, LANES))

        first = jnp.logical_and(h == 0, qi == 0)
        last = jnp.logical_and(h == heads - 1, qi == nq - 1)
        host.run(first, last, r_ins, r_outs, sems, compute)

    res = pl.pallas_call(
        body,
        name="fox_fwd",
        grid=(heads, nq),
        in_specs=[
            pl.BlockSpec((tq, hd), lambda h, i: (i, blk0 + 3 * h)),
            pl.BlockSpec((t, hd), lambda h, i: (0, blk0 + 3 * h + 1)),
            pl.BlockSpec((t, hd), lambda h, i: (0, blk0 + 3 * h + 2)),
            pl.BlockSpec((1, hd), lambda h, i: (0, 0)),
            pl.BlockSpec((1, hd), lambda h, i: (0, 0)),
            pl.BlockSpec((None, 1, t), lambda h, i: (h, 0, 0)),
            pl.BlockSpec((None, tq, LANES), lambda h, i: (h, i, 0)),
        ] + host.in_specs,
        out_specs=[pl.BlockSpec((tq, hd), lambda h, i: (i, h)), pl.BlockSpec((None, tq, LANES), lambda h, i: (h, i, 0))] + host.out_specs,
        out_shape=[jax.ShapeDtypeStruct((t, heads * hd), BF16), jax.ShapeDtypeStruct((heads, t, LANES), F32)] + host.out_shapes,
        scratch_shapes=[pltpu.VMEM((t, hd), BF16)] + host.scratch,
        compiler_params=_params(("arbitrary", "arbitrary")),
    )(proj, proj, proj, gq.reshape(1, hd), gk.reshape(1, hd), c_row3, c_colb, *host.ins)
    return res


def _fox_bwd(proj, off, o, do, gq, gk, c_row3, c_colb, lse, heads, tq, dproj, rider=None):
    t = proj.shape[0]
    hd = FOX_HEAD_DIM
    tq = _tile(t, tq)
    nb = t // tq
    blk0 = off // hd
    scale = 1.0 / math.sqrt(hd)
    host = _Host(rider)
    n_fixed_in = 11

    def body(*refs):
        q_ref, k_ref, v_ref, o_ref, do_ref, gq_ref, gk_ref, crow_ref, ccol_ref, lse_ref = refs[:10]
        pos = n_fixed_in
        r_ins = refs[pos:pos + host.n_in]; pos += host.n_in
        dp_ref, dc_ref, ggq_ref, ggk_ref = refs[pos:pos + 4]; pos += 4
        r_outs = refs[pos:pos + host.n_out]; pos += host.n_out
        qhat_ref, khat_ref, dq_ref, dk_ref, dcq_ref, dck_ref, delta_ref = refs[pos:pos + 7]; pos += 7
        sems = refs[pos:]
        h = pl.program_id(0)

        def compute():
            qn, rq = _head_rms(q_ref[...].astype(F32))
            qhat_ref[...] = (qn * gq_ref[...]).astype(BF16)
            kn, rk = _head_rms(k_ref[...].astype(F32))
            khat_ref[...] = (kn * gk_ref[...]).astype(BF16)
            delta_ref[...] = jnp.sum(do_ref[...].astype(F32) * o_ref[...].astype(F32), axis=-1, keepdims=True)
            dq_ref[...] = jnp.zeros_like(dq_ref)
            dcq_ref[...] = jnp.zeros_like(dcq_ref)

            def kv_block(j, _):
                ks = pl.multiple_of(j * tq, tq)
                kh = khat_ref[pl.ds(ks, tq), :]
                vv = v_ref[pl.ds(ks, tq), :]
                crow = crow_ref[:, pl.ds(ks, tq)]
                cols = ks + lax.broadcasted_iota(jnp.int32, (tq, tq), 1)

                def q_block(i, carry):
                    dk, dv, dck = carry
                    qs = pl.multiple_of(i * tq, tq)
                    qh = qhat_ref[pl.ds(qs, tq), :]
                    dob = do_ref[pl.ds(qs, tq), :]
                    s = _dot(qh, kh, NT) * scale + (ccol_ref[pl.ds(qs, tq), 0:1] - crow)
                    rows = qs + lax.broadcasted_iota(jnp.int32, (tq, tq), 0)
                    p = jnp.where(rows >= cols, jnp.exp(s - lse_ref[pl.ds(qs, tq), 0:1]), 0.0)
                    ds = p * (_dot(dob, vv, NT) - delta_ref[pl.ds(qs, tq), :])
                    dsb = ds.astype(BF16)
                    dv = dv + _dot(p.astype(BF16), dob, TN)
                    dk = dk + _dot(dsb, qh, TN)
                    dq_ref[pl.ds(qs, tq), :] += _dot(dsb, kh, NN)
                    dcq_ref[pl.ds(qs, tq), :] += jnp.sum(ds, axis=-1, keepdims=True)
                    dck = dck + jnp.sum(ds, axis=0, keepdims=True)
                    return dk, dv, dck

                zero = jnp.zeros((tq, hd), F32)
                dk, dv, dck = lax.fori_loop(j, nb, q_block, (zero, zero, jnp.zeros((1, tq), F32)))
                dk_ref[pl.ds(ks, tq), :] = dk * scale
                dp_ref[pl.ds(ks, tq), 2 * hd:3 * hd] = dv.astype(dp_ref.dtype)
                dck_ref[:, pl.ds(ks, tq)] = dck
                return 0

            lax.fori_loop(0, nb, kv_block, 0)

            dq, ggq = _head_rms_bwd(dq_ref[...] * scale, qn, rq, gq_ref[...])
            dk, ggk = _head_rms_bwd(dk_ref[...], kn, rk, gk_ref[...])
            dp_ref[:, 0:hd] = dq.astype(dp_ref.dtype)
            dp_ref[:, hd:2 * hd] = dk.astype(dp_ref.dtype)
            for b in range(nb):
                sl = slice(b * tq, (b + 1) * tq)
                dc_ref[:, sl] = _col_to_row(dcq_ref[sl, :]) - dck_ref[:, sl]

            @pl.when(h == 0)
            def _():
                ggq_ref[...] = jnp.zeros_like(ggq_ref)
                ggk_ref[...] = jnp.zeros_like(ggk_ref)

            ggq_ref[...] += ggq
            ggk_ref[...] += ggk

        host.run(h == 0, h == heads - 1, r_ins, r_outs, sems, compute)

    head_in = lambda part: pl.BlockSpec((t, hd), lambda h: (0, blk0 + 3 * h + part))
    vec = pl.BlockSpec((1, hd), lambda h: (0, 0))
    colb = pl.BlockSpec((None, t, LANES), lambda h: (h, 0, 0))
    res = pl.pallas_call(
        body,
        name="fox_bwd",
        grid=(heads,),
        in_specs=[
            head_in(0), head_in(1), head_in(2),
            pl.BlockSpec((t, hd), lambda h: (0, h)),
            pl.BlockSpec((t, hd), lambda h: (0, h)),
            vec, vec,
            pl.BlockSpec((None, 1, t), lambda h: (h, 0, 0)),
            colb, colb, ANY,
        ] + host.in_specs,
        out_specs=[
            pl.BlockSpec((t, 3 * hd), lambda h: (0, blk0 // 3 + h)),
            pl.BlockSpec((None, 1, t), lambda h: (h, 0, 0)),
            vec, vec,
        ] + host.out_specs,
        out_shape=[
            jax.ShapeDtypeStruct(dproj.shape, dproj.dtype),
            jax.ShapeDtypeStruct((heads, 1, t), F32),
            jax.ShapeDtypeStruct((1, hd), F32),
            jax.ShapeDtypeStruct((1, hd), F32),
        ] + host.out_shapes,
        input_output_aliases={10: 0},
        scratch_shapes=[
            pltpu.VMEM((t, hd), BF16), pltpu.VMEM((t, hd), BF16),
            pltpu.VMEM((t, hd), F32), pltpu.VMEM((t, hd), F32),
            pltpu.VMEM((t, 1), F32), pltpu.VMEM((1, t), F32), pltpu.VMEM((t, 1), F32),
        ] + host.scratch,
        compiler_params=_params(("arbitrary",)),
    )(proj, proj, proj, o, do, gq.reshape(1, hd), gk.reshape(1, hd), c_row3, c_colb, lse, dproj, *host.ins)
    return res


def _mem_fwd(proj, off, kv, gq, gk, tq):
    t = proj.shape[0]
    m, width = kv.shape[0], kv.shape[1] // 2
    hd = width // MEM_HEADS
    tq = _tile(t, tq)
    blk0 = off // hd
    scale = 1.0 / math.sqrt(hd)

    def body(q_ref, k_ref, v_ref, gq_ref, gk_ref, o_ref):
        qn, _ = _head_rms(q_ref[...].astype(F32))
        kn, _ = _head_rms(k_ref[...])
        s = _dot((qn * gq_ref[...]).astype(BF16), (kn * gk_ref[...]).astype(BF16), NT) * scale
        p = jnp.exp(s - jnp.max(s, axis=-1, keepdims=True))
        p = p / jnp.sum(p, axis=-1, keepdims=True)
        o_ref[...] = _dot(p.astype(BF16), v_ref[...].astype(BF16), NN).astype(o_ref.dtype)

    vec = pl.BlockSpec((1, hd), lambda h, i: (0, 0))
    return pl.pallas_call(
        body,
        name="mem_fwd",
        grid=(MEM_HEADS, t // tq),
        in_specs=[
            pl.BlockSpec((tq, hd), lambda h, i: (i, blk0 + h)),
            pl.BlockSpec((m, hd), lambda h, i: (0, h)),
            pl.BlockSpec((m, hd), lambda h, i: (0, MEM_HEADS + h)),
            vec, vec,
        ],
        out_specs=pl.BlockSpec((tq, hd), lambda h, i: (i, h)),
        out_shape=jax.ShapeDtypeStruct((t, width), BF16),
        compiler_params=_params(("parallel", "parallel")),
    )(proj, kv, kv, gq.reshape(1, hd), gk.reshape(1, hd))


def _mem_bwd(proj, off, kv, do, gq, gk, tq, dproj, rider=None):
    t = proj.shape[0]
    m, width = kv.shape[0], kv.shape[1] // 2
    hd = width // MEM_HEADS
    tq = _tile(t, tq)
    nq = t // tq
    blk0 = off // hd
    scale = 1.0 / math.sqrt(hd)
    host = _Host(rider)

    def body(*refs):
        q_ref, k_ref, v_ref, do_ref, gq_ref, gk_ref = refs[:6]
        pos = 7
        r_ins = refs[pos:pos + host.n_in]; pos += host.n_in
        dq_ref, dk_ref, dv_ref, ggq_ref, ggk_ref = refs[pos:pos + 5]; pos += 5
        r_outs = refs[pos:pos + host.n_out]; pos += host.n_out
        dkh_ref, dvh_ref = refs[pos:pos + 2]; pos += 2
        sems = refs[pos:]
        h, i = pl.program_id(0), pl.program_id(1)

        def compute():
            qn, rq = _head_rms(q_ref[...].astype(F32))
            kn, rk = _head_rms(k_ref[...])
            qhat = (qn * gq_ref[...]).astype(BF16)
            khat = (kn * gk_ref[...]).astype(BF16)
            vb = v_ref[...].astype(BF16)
            dob = do_ref[...]
            s = _dot(qhat, khat, NT) * scale
            p = jnp.exp(s - jnp.max(s, axis=-1, keepdims=True))
            p = p / jnp.sum(p, axis=-1, keepdims=True)
            dp = _dot(dob, vb, NT)
            ds = p * (dp - jnp.sum(dp * p, axis=-1, keepdims=True))
            dsb = ds.astype(BF16)
            dq, ggq = _head_rms_bwd(_dot(dsb, khat, NN) * scale, qn, rq, gq_ref[...])
            dq_ref[...] = dq.astype(dq_ref.dtype)

            @pl.when(i == 0)
            def _():
                dkh_ref[...] = jnp.zeros_like(dkh_ref)
                dvh_ref[...] = jnp.zeros_like(dvh_ref)

            @pl.when(jnp.logical_and(h == 0, i == 0))
            def _():
                ggq_ref[...] = jnp.zeros_like(ggq_ref)
                ggk_ref[...] = jnp.zeros_like(ggk_ref)

            dkh_ref[...] += _dot(dsb, qhat, TN)
            dvh_ref[...] += _dot(p.astype(BF16), dob, TN)
            ggq_ref[...] += ggq

            @pl.when(i == nq - 1)
            def _():
                dk, ggk = _head_rms_bwd(dkh_ref[...] * scale, kn, rk, gk_ref[...])
                dk_ref[...] = dk.astype(dk_ref.dtype)
                dv_ref[...] = dvh_ref[...].astype(dv_ref.dtype)
                ggk_ref[...] += ggk

        first = jnp.logical_and(h == 0, i == 0)
        last = jnp.logical_and(h == MEM_HEADS - 1, i == nq - 1)
        host.run(first, last, r_ins, r_outs, sems, compute)

    vec = pl.BlockSpec((1, hd), lambda h, i: (0, 0))
    kblk = pl.BlockSpec((m, hd), lambda h, i: (0, h))
    res = pl.pallas_call(
        body,
        name="mem_bwd",
        grid=(MEM_HEADS, nq),
        in_specs=[
            pl.BlockSpec((tq, hd), lambda h, i: (i, blk0 + h)), kblk,
            pl.BlockSpec((m, hd), lambda h, i: (0, MEM_HEADS + h)),
            pl.BlockSpec((tq, hd), lambda h, i: (i, h)), vec, vec, ANY,
        ] + host.in_specs,
        out_specs=[pl.BlockSpec((tq, hd), lambda h, i: (i, blk0 + h)), kblk, kblk, vec, vec] + host.out_specs,
        out_shape=[
            jax.ShapeDtypeStruct(dproj.shape, dproj.dtype),
            jax.ShapeDtypeStruct((m, width), BF16),
            jax.ShapeDtypeStruct((m, width), BF16),
            jax.ShapeDtypeStruct((1, hd), F32),
            jax.ShapeDtypeStruct((1, hd), F32),
        ] + host.out_shapes,
        input_output_aliases={6: 0},
        scratch_shapes=[pltpu.VMEM((m, hd), F32), pltpu.VMEM((m, hd), F32)] + host.scratch,
        compiler_params=_params(("arbitrary", "arbitrary")),
    )(proj, kv, kv, do, gq.reshape(1, hd), gk.reshape(1, hd), dproj, *host.ins)
    dproj, dk, dv, ggq, ggk = res[:5]
    return (dproj, jnp.concatenate([dk, dv], axis=1), ggq.reshape(hd), ggk.reshape(hd), *res[5:])


def _sigmoid(z):
    return 1.0 / (1.0 + jnp.exp(-z))


def _merge_fwd(proj, o3, tm, tc):
    t, d = o3[0].shape
    tm = _tile(t, tm)

    def body(g_ref, oa_ref, ob_ref, oc_ref, out_ref):
        acc = jnp.zeros((tm, tc), F32)
        for s, o_ref in enumerate((oa_ref, ob_ref, oc_ref)):
            acc = acc + _sigmoid(g_ref[:, s * tc:(s + 1) * tc].astype(F32)) * o_ref[...].astype(F32)
        out_ref[...] = acc.astype(out_ref.dtype)

    blk = pl.BlockSpec((tm, tc), lambda i, j: (i, j))
    return pl.pallas_call(
        body,
        name="merge_fwd",
        grid=(t // tm, d // tc),
        in_specs=[pl.BlockSpec((tm, 3 * tc), lambda i, j: (i, j)), blk, blk, blk],
        out_specs=blk,
        out_shape=jax.ShapeDtypeStruct((t, d), BF16),
        compiler_params=_params(("parallel", "parallel")),
    )(proj, *o3)


def _merge_bwd(proj, o3, dm, tm, tc):
    t, d = dm.shape
    tm = _tile(t, tm)

    def body(g_ref, oa_ref, ob_ref, oc_ref, dm_ref, dg_ref, da_ref, db_ref, dc_ref):
        dmf = dm_ref[...].astype(F32)
        for s, (o_ref, do_ref) in enumerate(((oa_ref, da_ref), (ob_ref, db_ref), (oc_ref, dc_ref))):
            g = _sigmoid(g_ref[:, s * tc:(s + 1) * tc].astype(F32))
            do_ref[...] = (dmf * g).astype(do_ref.dtype)
            dg_ref[:, s * tc:(s + 1) * tc] = (dmf * o_ref[...].astype(F32) * g * (1.0 - g)).astype(dg_ref.dtype)

    blk = pl.BlockSpec((tm, tc), lambda i, j: (i, j))
    wide = pl.BlockSpec((tm, 3 * tc), lambda i, j: (i, j))
    return pl.pallas_call(
        body,
        name="merge_bwd",
        grid=(t // tm, d // tc),
        in_specs=[wide, blk, blk, blk, blk],
        out_specs=[wide, blk, blk, blk],
        out_shape=[jax.ShapeDtypeStruct(proj.shape, BF16)] + [jax.ShapeDtypeStruct((t, d), BF16)] * 3,
        compiler_params=_params(("parallel", "parallel")),
    )(proj, *o3, dm)


def _loss(dy, d):
    t = dy.shape[0]
    tm = _tile(t, 512)

    def body(dy_ref, out_ref):
        i = pl.program_id(0)

        @pl.when(i == 0)
        def _():
            out_ref[...] = jnp.zeros_like(out_ref)

        e = dy_ref[...]
        out_ref[...] += jnp.sum(jnp.sum(e * e, axis=0, keepdims=True), axis=1, keepdims=True) * (0.5 * d)

    out = pl.pallas_call(
        body,
        name="loss",
        grid=(t // tm,),
        in_specs=[pl.BlockSpec((tm, dy.shape[1]), lambda i: (i, 0))],
        out_specs=pl.BlockSpec((1, 1), lambda i: (0, 0)),
        out_shape=jax.ShapeDtypeStruct((1, 1), F32),
        compiler_params=_params(("arbitrary",)),
    )(dy)
    return out[0, 0]


def _pack_w_in(w_in, d, tc):
    cw = d // 2
    heads = cw // FOX_HEAD_DIM
    k = w_in.shape[0]
    o = 0
    conv = w_in[:, o:o + 3 * cw]; o += 3 * cw
    fox = w_in[:, o:o + 3 * cw]; o += 3 * cw
    f = w_in[:, o:o + heads]; o += heads
    mq = w_in[:, o:o + cw]; o += cw
    gate = w_in[:, o:o + N_BRANCHES * d]
    conv = conv.reshape(k, 3, cw // LANES, LANES).transpose(0, 2, 1, 3).reshape(k, 3 * cw)
    fox = fox.reshape(k, 3, heads, FOX_HEAD_DIM).transpose(0, 2, 1, 3).reshape(k, 3 * cw)
    gate = gate.reshape(k, N_BRANCHES, d // tc, tc).transpose(0, 2, 1, 3).reshape(k, N_BRANCHES * d)
    f_t = jnp.pad(f.T, ((0, F_ROWS - heads), (0, 0)))
    return jnp.concatenate([gate, conv, fox, mq], axis=1), f_t


def _unpack_g_in(g_all, g_f, d, tc):
    cw = d // 2
    heads = cw // FOX_HEAD_DIM
    k = g_all.shape[0]
    o = 0
    gate = g_all[:, o:o + N_BRANCHES * d]; o += N_BRANCHES * d
    conv = g_all[:, o:o + 3 * cw]; o += 3 * cw
    fox = g_all[:, o:o + 3 * cw]; o += 3 * cw
    mq = g_all[:, o:o + cw]
    conv = conv.reshape(k, cw // LANES, 3, LANES).transpose(0, 2, 1, 3).reshape(k, 3 * cw)
    fox = fox.reshape(k, heads, 3, FOX_HEAD_DIM).transpose(0, 2, 1, 3).reshape(k, 3 * cw)
    gate = gate.reshape(k, d // tc, N_BRANCHES, tc).transpose(0, 2, 1, 3).reshape(k, N_BRANCHES * d)
    return jnp.concatenate([conv, fox, g_f[:heads].T, mq, gate], axis=1)


def _unblock(w8):
    return w8.transpose(1, 0, 2).reshape(w8.shape[1], -1)


def _pair_sum(name, g8, got, c, tr=256):
    _, r, cols = g8.shape
    tr = _tile(r, tr)

    def body(c_ref, g_ref, s_ref, o_ref):
        o_ref[...] = (g_ref[...].astype(F32) + s_ref[...].astype(F32)).astype(o_ref.dtype)

    blk = pl.BlockSpec((None, tr, cols), lambda q, i, c_ref: (q, i, 0))
    return pl.pallas_call(
        body,
        name=name,
        grid_spec=pltpu.PrefetchScalarGridSpec(
            num_scalar_prefetch=1,
            grid=(N_CHIPS, r // tr),
            in_specs=[pl.BlockSpec((None, tr, cols), lambda q, i, c_ref: (2 * q + c_ref[0], i, 0)), blk],
            out_specs=blk,
        ),
        out_shape=jax.ShapeDtypeStruct((N_CHIPS, r, cols), BF16),
        compiler_params=_params(("parallel", "parallel")),
    )(c, g8, got)


def _local_step(x, mem, target, w, small, comm=None):
    t, d = x.shape
    cw = d // 2
    heads = cw // FOX_HEAD_DIM
    tc = min(512, d)
    tq = min(512, t)
    off_conv, off_fox, off_mq = 3 * d, 3 * d + 3 * cw, 3 * d + 6 * cw
    w = dict(w)
    w_all, w_f = _pack_w_in(w["w_in"], d, tc)
    big = dict(tm=1024, tn=512, tk=2048)
    wide_k = dict(tm=512, tn=1024, tk=4096)

    h = _rms_fwd("rms1_fwd", x, small["norm1_g"])
    if comm:
        early = ("w_conv_out", "w_fox_out", "w_mem_out", "w_out", "w_up")
        proj, *got = _matmul("proj", "nn", h, w_all, outs=[BF16], rider=_gather_rider([comm["shards"][n] for n in early]), **big)
        for n, val in zip(early, got):
            w[n] = val.reshape(-1, val.shape[-1]) if n == "w_out" else _unblock(val)
    else:
        proj = _matmul("proj", "nn", h, w_all, outs=[BF16], **big)
    z_row = _matmul("proj_f", "nt", w_f, h, outs=[F32], tm=F_ROWS, tn=512, tk=2048)

    y_conv = _conv_fwd(proj, off_conv, small["conv_w"], LANES)

    b_col = jnp.pad(small["b_f"], (0, F_ROWS - heads)).reshape(F_ROWS, 1)
    c_row3 = _forget_fwd(z_row, b_col)[:heads].reshape(heads, 1, t)
    c_colb = _rows_to_colb(c_row3, tq)
    if comm:
        y_fox, lse, got = _fox_fwd(proj, off_fox, small["fox_q_g"], small["fox_k_g"], c_row3, c_colb, heads, tq,
                                   rider=_gather_rider([comm["shards"]["w_down"]]))
        w["w_down"] = got.reshape(-1, got.shape[-1])
    else:
        y_fox, lse = _fox_fwd(proj, off_fox, small["fox_q_g"], small["fox_k_g"], c_row3, c_colb, heads, tq)

    nm = _rms_fwd("mem_rms_fwd", mem, small["mem_norm_g"])
    kv = _matmul("mem_kv", "nn", nm, w["w_mem_kv"], outs=[F32], tm=256, tn=512, tk=2048)
    y_mem = _mem_fwd(proj, off_mq, kv, small["mem_q_g"], small["mem_k_g"], tq)

    ys = (y_conv, y_fox, y_mem)
    w_outs = (w["w_conv_out"], w["w_fox_out"], w["w_mem_out"])
    o3 = [_matmul(f"branch_out{s}", "nn", ys[s], w_outs[s], outs=[BF16], **big) for s in range(3)]
    merged = _merge_fwd(proj, o3, 512, tc)
    x1 = _matmul("out_proj", "nn", merged, w["w_out"], outs=[F32], extras=[x],
                 epilogue=lambda acc, xr: (acc + xr,), **big)
    h2 = _rms_fwd("rms2_fwd", x1, small["norm2_g"])

    def up_epilogue(acc):
        return acc, jnp.square(jnp.maximum(acc, 0.0))

    up, act = _matmul("mlp_up", "nn", h2, w["w_up"], outs=[BF16, BF16], epilogue=up_epilogue, **big)

    def loss_epilogue(acc, x1r, tr):
        dy = (acc + x1r - tr) * (1.0 / d)
        return dy, dy

    dy, dyb = _matmul("mlp_down", "nn", act, w["w_down"], outs=[F32, BF16], extras=[x1, target],
                      epilogue=loss_epilogue, **big)

    def dup_epilogue(acc, upr):
        return (acc * 2.0 * jnp.maximum(upr.astype(F32), 0.0),)

    def by_owner(g):
        return g.reshape(N_DEV, -1, g.shape[-1])

    g, parts = {}, {}
    g["w_down"] = _matmul("d_w_down", "tn", act, dyb, outs=[BF16], **wide_k)
    if comm:
        dup, got = _matmul("d_act", "nt", dyb, w["w_down"], outs=[BF16], extras=[up], epilogue=dup_epilogue,
                           rider=_pair_rider([by_owner(g["w_down"])]), **big)
        pair = _pair_sum("pair_w_down", by_owner(g["w_down"]), got, comm["c"])
        g["w_up"], parts["w_down"] = _matmul("d_w_up", "tn", h2, dup, outs=[BF16], out_blocks=True,
                                             rider=_chip_rider([pair]), **wide_k)
        dh2, got = _matmul("d_h2", "nt", dup, w["w_up"], outs=[F32], rider=_pair_rider([g["w_up"]]), **big)
        pair_up = _pair_sum("pair_w_up", g["w_up"], got, comm["c"])
    else:
        dup = _matmul("d_act", "nt", dyb, w["w_down"], outs=[BF16], extras=[up], epilogue=dup_epilogue, **big)
        g["w_up"] = _matmul("d_w_up", "tn", h2, dup, outs=[BF16], out_blocks=True, **wide_k)
        dh2 = _matmul("d_h2", "nt", dup, w["w_up"], outs=[F32], **big)
    dx1, dx1b, g_norm2 = _rms_bwd("rms2_bwd", dh2, x1, small["norm2_g"], res=dy)
    loss = _loss(dy, d)

    g["w_out"] = _matmul("d_w_out", "tn", merged, dx1b, outs=[BF16], **wide_k)
    dmerged = _matmul("d_merged", "nt", dx1b, w["w_out"], outs=[BF16], **big)
    dproj, *do3 = _merge_bwd(proj, o3, dmerged, 512, tc)
    names = ("w_conv_out", "w_fox_out", "w_mem_out")
    dys = []
    for s in range(3):
        g[names[s]] = _matmul(f"d_w_branch{s}", "tn", ys[s], do3[s], outs=[BF16], out_blocks=True, **wide_k)
        dys.append(_matmul(f"d_branch{s}", "nt", do3[s], w_outs[s], outs=[BF16], **big))

    dproj, dkv, g_mq, g_mk = _mem_bwd(proj, off_mq, kv, dys[2], small["mem_q_g"], small["mem_k_g"], tq, dproj)
    g["w_mem_kv"] = _matmul("d_w_mem_kv", "tn", nm, dkv, outs=[BF16], **wide_k)
    dnm = _matmul("d_mem_norm", "nt", dkv, w["w_mem_kv"], outs=[F32], tm=256, tn=512, tk=2048)
    _, _, g_mem_norm = _rms_bwd("mem_rms_bwd", dnm, mem, small["mem_norm_g"])

    mid = ("w_out", "w_conv_out", "w_fox_out", "w_mem_out", "w_mem_kv")
    if comm:
        mid8 = [g[n] if n in names else by_owner(g[n]) for n in mid]
        dproj, g_conv_w, *got = _conv_bwd(proj, off_conv, small["conv_w"], dys[0], LANES, dproj, rider=_pair_rider(mid8))
        pairs = [pair_up] + [_pair_sum("pair_" + n, g8, s4, comm["c"]) for n, g8, s4 in zip(mid, mid8, got)]
        dproj, dc, g_fq, g_fk, *got = _fox_bwd(proj, off_fox, y_fox, dys[1], small["fox_q_g"], small["fox_k_g"], c_row3, c_colb,
                                               lse, heads, tq, dproj, rider=_chip_rider(pairs))
        parts.update(zip(("w_up",) + mid, got))
    else:
        dproj, g_conv_w = _conv_bwd(proj, off_conv, small["conv_w"], dys[0], LANES, dproj)
        dproj, dc, g_fq, g_fk = _fox_bwd(proj, off_fox, y_fox, dys[1], small["fox_q_g"], small["fox_k_g"], c_row3, c_colb,
                                         lse, heads, tq, dproj)
    dc_row = jnp.pad(dc.reshape(heads, t), ((0, F_ROWS - heads), (0, 0)))
    dz_row, db = _forget_bwd(z_row, b_col, dc_row)

    g_all = _matmul("d_w_in", "tn", h, dproj, outs=[BF16], **wide_k)
    g_wf = _matmul("d_w_f", "nn", dz_row, h, outs=[BF16], tm=F_ROWS, tn=512, tk=4096)
    g["w_in"] = _unpack_g_in(g_all, g_wf, d, tc)
    dh = _matmul("d_h_f", "tn", dz_row, w_f, outs=[F32], tm=1024, tn=512, tk=F_ROWS)
    add_prev = lambda acc, prev: (acc + prev,)
    if comm:
        gi = g["w_in"]
        g_in8 = gi.reshape(gi.shape[0], N_DEV, -1).transpose(1, 0, 2)
        got = _run_rider("pair_exchange_w_in", _pair_rider([g_in8]))[0]
        pair = _pair_sum("pair_w_in", g_in8, got, comm["c"], tr=128)
        dh, parts["w_in"] = _matmul("d_h", "nt", dproj, w_all, outs=[F32], extras=[dh], epilogue=add_prev,
                                    rider=_chip_rider([pair]), tm=1024, tn=512, tk=3328)
    else:
        dh = _matmul("d_h", "nt", dproj, w_all, outs=[F32], extras=[dh], epilogue=add_prev, tm=1024, tn=512, tk=3328)
    grad_x, _, g_norm1 = _rms_bwd("rms1_bwd", dh, x, small["norm1_g"], res=dx1)

    gs = dict(norm1_g=g_norm1, b_f=db[:heads, 0], conv_w=g_conv_w, fox_q_g=g_fq.reshape(-1), fox_k_g=g_fk.reshape(-1),
              mem_norm_g=g_mem_norm, mem_q_g=g_mq, mem_k_g=g_mk, norm2_g=g_norm2)
    return loss, grad_x, (parts if comm else g), gs


def _adamw_math(w, g, m, v):
    m = ADAM_B1 * m + (1.0 - ADAM_B1) * g
    v = ADAM_B2 * v + (1.0 - ADAM_B2) * jnp.square(g)
    m_hat = m / (1.0 - ADAM_B1 ** ADAM_STEP)
    v_hat = v / (1.0 - ADAM_B2 ** ADAM_STEP)
    delta = -ADAM_LR * (m_hat / (jnp.sqrt(v_hat) + ADAM_EPS) + ADAM_WD * w)
    return delta, m, v


def _adamw(name, parts, w, m, v, tr):
    r, c = w.shape
    tr = _tile(r, tr)
    n_parts = parts.shape[0]

    def body(p_ref, w_ref, m_ref, v_ref, g_ref, d_ref, nm_ref, nv_ref):
        g = p_ref[0].astype(F32)
        for s in range(1, n_parts):
            g = g + p_ref[s].astype(F32)
        delta, nm, nv = _adamw_math(w_ref[...], g, m_ref[...], v_ref[...])
        g_ref[...] = g
        d_ref[...] = delta
        nm_ref[...] = nm
        nv_ref[...] = nv

    blk = pl.BlockSpec((tr, c), lambda i: (i, 0))
    return pl.pallas_call(
        body,
        name=name,
        grid=(r // tr,),
        in_specs=[pl.BlockSpec((n_parts, tr, c), lambda i: (0, i, 0)), blk, blk, blk],
        out_specs=[blk] * 4,
        out_shape=[jax.ShapeDtypeStruct((r, c), F32)] * 4,
        compiler_params=_params(("parallel",)),
    )(parts, w, m, v)


def _sum_parts(name, parts):
    n_parts, r, c = parts.shape

    def body(p_ref, o_ref):
        acc = p_ref[0]
        for s in range(1, n_parts):
            acc = acc + p_ref[s]
        o_ref[...] = acc

    return pl.pallas_call(body, name=name, out_shape=jax.ShapeDtypeStruct((r, c), F32))(parts)


BIG = ("w_in", "w_mem_kv", "w_conv_out", "w_fox_out", "w_mem_out", "w_out", "w_up", "w_down")
COLUMN_SPLIT = ("w_in", "w_conv_out", "w_fox_out", "w_mem_out", "w_up")
SMALL = ("norm1_g", "b_f", "conv_w", "fox_q_g", "fox_k_g", "mem_norm_g", "mem_q_g", "mem_k_g", "norm2_g")
WEIGHTS = ("norm1_g", "w_in", "b_f", "conv_w", "fox_q_g", "fox_k_g", "mem_norm_g", "w_mem_kv", "mem_q_g", "mem_k_g",
           "w_conv_out", "w_fox_out", "w_mem_out", "w_out", "norm2_g", "w_up", "w_down")


def _pack(vectors):
    rows = []
    for vec in vectors:
        n = vec.shape[0]
        rows.append(jnp.pad(vec, (0, -n % LANES)).reshape(-1, LANES))
    out = jnp.concatenate(rows, axis=0)
    return jnp.pad(out, ((0, -out.shape[0] % 8), (0, 0)))


def _unpack(packed, sizes):
    out, row = [], 0
    for n in sizes:
        nr = -(-n // LANES)
        out.append(packed[row:row + nr].reshape(-1)[:n])
        row += nr
    return out


def kernel(x, mem, norm1_g, w_in, b_f, conv_w, fox_q_g, fox_k_g, mem_norm_g, w_mem_kv, mem_q_g, mem_k_g, w_conv_out, w_fox_out, w_mem_out, w_out, norm2_g, w_up, w_down, loss_target, m_norm1_g, m_w_in, m_b_f, m_conv_w, m_fox_q_g, m_fox_k_g, m_mem_norm_g, m_w_mem_kv, m_mem_q_g, m_mem_k_g, m_w_conv_out, m_w_fox_out, m_w_mem_out, m_w_out, m_norm2_g, m_w_up, m_w_down, v_norm1_g, v_w_in, v_b_f, v_conv_w, v_fox_q_g, v_fox_k_g, v_mem_norm_g, v_w_mem_kv, v_mem_q_g, v_mem_k_g, v_w_conv_out, v_w_fox_out, v_w_mem_out, v_w_out, v_norm2_g, v_w_up, v_w_down):
    args = dict(locals())
    wts = {n: args[n] for n in WEIGHTS}
    ms = {n: args["m_" + n] for n in WEIGHTS}
    vs = {n: args["v_" + n] for n in WEIGHTS}
    x_pos, y_pos, c_pos = _position()
    me = _index(x_pos, y_pos, c_pos)

    shards = {n: wts[n].astype(BF16) for n in BIG}
    wi, wkv, cw8 = _run_rider("all_gather_first", _gather_rider([shards["w_in"], shards["w_mem_kv"], conv_w]))
    full = {"w_in": _unblock(wi), "w_mem_kv": wkv.reshape(-1, wkv.shape[-1])}
    small = {n: wts[n] for n in SMALL}
    small["conv_w"] = _unblock(cw8)
    comm = {"shards": shards, "c": c_pos.astype(jnp.int32).reshape(1)}

    loss, grad_x, parts, gs = _local_step(x[0], mem[0], loss_target[0], full, small, comm)

    out_g, out_d, out_m, out_v = {}, {}, {}, {}
    for n in BIG:
        out_g[n], out_d[n], out_m[n], out_v[n] = _adamw("adamw_" + n, parts[n], wts[n], ms[n], vs[n], 128)

    small_sizes = [int(math.prod(gs[n].shape)) for n in SMALL]
    packed = _pack([gs[n].reshape(-1) for n in SMALL])
    gsum = _sum_parts("sum_small", _run_rider("exchange_small", _broadcast_rider([packed]))[0])
    gsmall = dict(zip(SMALL, _unpack(gsum, small_sizes)))
    cols = conv_w.shape[1]
    gsmall["conv_w"] = lax.dynamic_slice(gsmall["conv_w"].reshape(CONV_TAPS, -1), (0, me * cols), (CONV_TAPS, cols)).reshape(-1)
    pg, pw, pm, pv = (_pack([src[n].reshape(-1) for n in SMALL]) for src in (gsmall, wts, ms, vs))
    _, sd, sm, sv = _adamw("adamw_small", pg[None], pw, pm, pv, pw.shape[0])
    local_sizes = [int(math.prod(wts[n].shape)) for n in SMALL]
    for dst, src in ((out_d, sd), (out_m, sm), (out_v, sv)):
        for n, val in zip(SMALL, _unpack(src, local_sizes)):
            dst[n] = val.reshape(wts[n].shape)
    for n in SMALL:
        out_g[n] = gsmall[n].reshape(wts[n].shape)

    loss = lax.psum(loss, MESH_AXES)
    return (loss, grad_x[None], *[out_g[n] for n in WEIGHTS], *[out_d[n] for n in WEIGHTS],
            *[out_m[n] for n in WEIGHTS], *[out_v[n] for n in WEIGHTS])
```

```python
import math

import jax
import jax.numpy as jnp
from jax import lax
from jax.experimental import pallas as pl
from jax.experimental.pallas import tpu as pltpu

F32 = jnp.float32
BF16 = jnp.bfloat16

EPS = 1e-6
N_DEV = 8
N_CHIPS = 4
FOX_HEAD_DIM = 128
MEM_HEADS = 4
CONV_TAPS = 3
N_BRANCHES = 3
F_ROWS = 16

ADAM_LR = 0.001
ADAM_B1 = 0.9
ADAM_B2 = 0.999
ADAM_EPS = 1e-08
ADAM_WD = 0.01
ADAM_STEP = 10

V7X_VMEM_BYTES = 64 * 1024 * 1024
VMEM_LIMIT = V7X_VMEM_BYTES * 3 // 4
LANES = 128
NEG = -1e30

MESH_AXES = ("x", "y", "c")
MESH = pl.DeviceIdType.MESH
ANY = pl.BlockSpec(memory_space=pl.ANY)

NN = (((1,), (0,)), ((), ()))
NT = (((1,), (1,)), ((), ()))
TN = (((0,), (0,)), ((), ()))


def _params(sem):
    return pltpu.CompilerParams(dimension_semantics=sem, vmem_limit_bytes=VMEM_LIMIT)


def _dot(a, b, dn):
    return lax.dot_general(a, b, dn, preferred_element_type=F32)


def _tile(n, t):
    if n <= t:
        return n
    for cand in range(t - t % LANES, 0, -LANES):
        if n % cand == 0:
            return cand
    raise ValueError((n, t))


class _Rider:
    def __init__(self, ins, out_shapes, sem_shapes, start, finish):
        self.ins, self.out_shapes, self.sem_shapes = list(ins), list(out_shapes), list(sem_shapes)
        self.start, self.finish = start, finish


def _position():
    return lax.axis_index("x"), lax.axis_index("y"), lax.axis_index("c")


def _index(px, py, pc):
    return 4 * px + 2 * py + pc


def _dma_sems(n, per):
    return [pltpu.SemaphoreType.DMA((n, per)), pltpu.SemaphoreType.DMA((n, per)), pltpu.SemaphoreType.DMA((n,))]


def _gather_rider(shards):
    n = len(shards)

    def copies(ins, outs, sems):
        send_sems, recv_sems, local_sems = sems
        x, y, c = _position()
        me, sibling = (x, y, c), (x, y, 1 - c)
        chips = [(1 - x, y), (x, 1 - y), (1 - x, 1 - y)]

        def copy(a, k, block, to, src=None):
            rows = outs[a].at[_index(*block)]
            return pltpu.make_async_remote_copy(
                src_ref=rows if src is None else src, dst_ref=rows,
                send_sem=send_sems.at[a, k], recv_sem=recv_sems.at[a, k], device_id=to, device_id_type=MESH)

        mine = [pltpu.make_async_copy(ins[a], outs[a].at[_index(*me)], local_sems.at[a]) for a in range(n)]
        first = []
        for a in range(n):
            first.append(copy(a, 0, me, sibling, src=ins[a]))
            first += [copy(a, 1 + j, me, (*chip, c), src=ins[a]) for j, chip in enumerate(chips)]
        return copy, mine, first, me, sibling, chips, c

    def start(ins, outs, sems):
        _, mine, first, *_ = copies(ins, outs, sems)
        for cp in mine + first:
            cp.start()

    def finish(ins, outs, sems):
        copy, mine, first, me, sibling, chips, c = copies(ins, outs, sems)
        passed = []
        for a in range(n):
            for j, chip in enumerate(chips):
                copy(a, 1 + j, (*chip, c), me).wait_recv()
                fwd = copy(a, 4 + j, (*chip, c), sibling)
                fwd.start()
                passed.append(fwd)
        for a in range(n):
            copy(a, 0, sibling, me).wait_recv()
            for j, chip in enumerate(chips):
                copy(a, 4 + j, (*chip, 1 - c), me).wait_recv()
        for cp in first + passed:
            cp.wait_send()
        for cp in mine:
            cp.wait()

    out_shapes = [jax.ShapeDtypeStruct((N_DEV,) + s.shape, s.dtype) for s in shards]
    return _Rider(shards, out_shapes, _dma_sems(n, 7), start, finish)


def _pair_rider(grads):
    n = len(grads)

    def copies(ins, outs, sems):
        send_sems, recv_sems, _ = sems
        x, y, c = _position()
        return [pltpu.make_async_remote_copy(
            src_ref=ins[a].at[2 * q + (1 - c)], dst_ref=outs[a].at[q],
            send_sem=send_sems.at[a, q], recv_sem=recv_sems.at[a, q], device_id=(x, y, 1 - c), device_id_type=MESH)
            for a in range(n) for q in range(N_CHIPS)]

    def start(ins, outs, sems):
        for cp in copies(ins, outs, sems):
            cp.start()

    def finish(ins, outs, sems):
        cps = copies(ins, outs, sems)
        for cp in cps:
            cp.wait_recv()
        for cp in cps:
            cp.wait_send()

    out_shapes = [jax.ShapeDtypeStruct((N_CHIPS,) + g.shape[1:], g.dtype) for g in grads]
    return _Rider(grads, out_shapes, _dma_sems(n, N_CHIPS), start, finish)


def _chip_rider(parts):
    n = len(parts)

    def copies(ins, outs, sems):
        send_sems, recv_sems, local_sems = sems
        x, y, c = _position()
        q_me = 2 * x + y
        chips = [(1 - x, y), (x, 1 - y), (1 - x, 1 - y)]
        mine = [pltpu.make_async_copy(ins[a].at[q_me], outs[a].at[q_me], local_sems.at[a]) for a in range(n)]
        sends, arrivals = [], []
        for a in range(n):
            for j, (tx, ty) in enumerate(chips):
                q_t = 2 * tx + ty
                sends.append(pltpu.make_async_remote_copy(
                    src_ref=ins[a].at[q_t], dst_ref=outs[a].at[q_me],
                    send_sem=send_sems.at[a, j], recv_sem=recv_sems.at[a, j], device_id=(tx, ty, c), device_id_type=MESH))
                arrivals.append(pltpu.make_async_remote_copy(
                    src_ref=ins[a].at[q_t], dst_ref=outs[a].at[q_t],
                    send_sem=send_sems.at[a, j], recv_sem=recv_sems.at[a, j], device_id=(tx, ty, c), device_id_type=MESH))
        return mine, sends, arrivals

    def start(ins, outs, sems):
        mine, sends, _ = copies(ins, outs, sems)
        for cp in mine + sends:
            cp.start()

    def finish(ins, outs, sems):
        mine, sends, arrivals = copies(ins, outs, sems)
        for cp in arrivals:
            cp.wait_recv()
        for cp in sends:
            cp.wait_send()
        for cp in mine:
            cp.wait()

    out_shapes = [jax.ShapeDtypeStruct(p.shape, p.dtype) for p in parts]
    return _Rider(parts, out_shapes, _dma_sems(n, 3), start, finish)


def _broadcast_rider(values):
    n = len(values)

    def copies(ins, outs, sems):
        send_sems, recv_sems, local_sems = sems
        x, y, c = _position()
        me = _index(x, y, c)

        def peer(k):
            return (1 - x if k & 4 else x, 1 - y if k & 2 else y, 1 - c if k & 1 else c)

        mine = [pltpu.make_async_copy(ins[a], outs[a].at[me], local_sems.at[a]) for a in range(n)]
        sends, arrivals = [], []
        for a in range(n):
            for k in range(1, N_DEV):
                common = dict(send_sem=send_sems.at[a, k - 1], recv_sem=recv_sems.at[a, k - 1], device_id=peer(k), device_id_type=MESH)
                sends.append(pltpu.make_async_remote_copy(src_ref=ins[a], dst_ref=outs[a].at[me], **common))
                arrivals.append(pltpu.make_async_remote_copy(src_ref=ins[a], dst_ref=outs[a].at[_index(*peer(k))], **common))
        return mine, sends, arrivals

    def start(ins, outs, sems):
        mine, sends, _ = copies(ins, outs, sems)
        for cp in mine + sends:
            cp.start()

    def finish(ins, outs, sems):
        mine, sends, arrivals = copies(ins, outs, sems)
        for cp in arrivals:
            cp.wait_recv()
        for cp in sends:
            cp.wait_send()
        for cp in mine:
            cp.wait()

    out_shapes = [jax.ShapeDtypeStruct((N_DEV,) + v.shape, v.dtype) for v in values]
    return _Rider(values, out_shapes, _dma_sems(n, 7), start, finish)


def _run_rider(name, rider):
    n_in, n_out = len(rider.ins), len(rider.out_shapes)

    def body(*refs):
        ins, outs, sems = refs[:n_in], refs[n_in:n_in + n_out], refs[n_in + n_out:]
        rider.start(ins, outs, sems)
        rider.finish(ins, outs, sems)

    return pl.pallas_call(
        body, name=name, in_specs=[ANY] * n_in, out_specs=[ANY] * n_out, out_shape=rider.out_shapes,
        scratch_shapes=rider.sem_shapes)(*rider.ins)


class _Host:
    def __init__(self, rider):
        self.rider = rider
        self.n_in = len(rider.ins) if rider else 0
        self.n_out = len(rider.out_shapes) if rider else 0
        self.n_sem = len(rider.sem_shapes) if rider else 0
        self.ins = rider.ins if rider else []
        self.in_specs = [ANY] * self.n_in
        self.out_specs = [ANY] * self.n_out
        self.out_shapes = rider.out_shapes if rider else []
        self.scratch = rider.sem_shapes if rider else []

    def run(self, first, last, ins, outs, sems, compute):
        if self.rider is None:
            compute()
            return

        @pl.when(first)
        def _():
            self.rider.start(ins, outs, sems)

        compute()

        @pl.when(last)
        def _():
            self.rider.finish(ins, outs, sems)


def _matmul(name, kind, a, b, *, tm, tn, tk, outs, epilogue=None, extras=(), out_blocks=False, rider=None, j_outer=False):
    if kind == "nn":
        (m, kdim), n = a.shape, b.shape[1]
    elif kind == "nt":
        (m, kdim), n = a.shape, b.shape[0]
    else:
        (kdim, m), n = a.shape, b.shape[1]
    if out_blocks:
        tn = min(tn, n // N_DEV)
    tm, tn, tk = _tile(m, tm), _tile(n, tn), _tile(kdim, tk)
    ni, nj, nk = m // tm, n // tn, kdim // tk

    def spec(shape, fn):
        return pl.BlockSpec(shape, (lambda g0, g1, k: fn(g1, g0, k)) if j_outer else fn)

    a_spec = spec((tk, tm), lambda i, j, k: (k, i)) if kind == "tn" else spec((tm, tk), lambda i, j, k: (i, k))
    b_spec = spec((tn, tk), lambda i, j, k: (j, k)) if kind == "nt" else spec((tk, tn), lambda i, j, k: (k, j))
    dn = {"nn": NN, "nt": NT, "tn": TN}[kind]

    tile_spec = spec((tm, tn), lambda i, j, k: (i, j))
    if out_blocks:
        width = n // N_DEV
        r_out = width // tn
        out_shape = [jax.ShapeDtypeStruct((N_DEV, m, width), dt) for dt in outs]
        out_specs = [spec((None, tm, tn), lambda i, j, k: (j // r_out, i, j % r_out)) for _ in outs]
    else:
        out_shape = [jax.ShapeDtypeStruct((m, n), dt) for dt in outs]
        out_specs = [tile_spec for _ in outs]
    n_ex, n_out = len(extras), len(outs)
    host = _Host(rider)
    n_acc = 1 if nk > 1 else 0

    def body(*refs):
        a_ref, b_ref = refs[0], refs[1]
        pos = 2
        ex_refs = refs[pos:pos + n_ex]; pos += n_ex
        r_ins = refs[pos:pos + host.n_in]; pos += host.n_in
        out_refs = refs[pos:pos + n_out]; pos += n_out
        r_outs = refs[pos:pos + host.n_out]; pos += host.n_out
        acc_ref = refs[pos] if n_acc else None
        sems = refs[pos + n_acc:]
        i, j, k = pl.program_id(1 if j_outer else 0), pl.program_id(0 if j_outer else 1), pl.program_id(2)

        def finish_tile(acc):
            vals = (acc,) if epilogue is None else epilogue(acc, *[e[...] for e in ex_refs])
            for o_ref, v in zip(out_refs, vals):
                o_ref[...] = v.astype(o_ref.dtype)

        def compute():
            part = _dot(a_ref[...], b_ref[...], dn)
            if nk == 1:
                finish_tile(part)
                return

            @pl.when(k == 0)
            def _():
                acc_ref[...] = part

            @pl.when(jnp.logical_and(k > 0, k < nk - 1))
            def _():
                acc_ref[...] += part

            @pl.when(k == nk - 1)
            def _():
                finish_tile(acc_ref[...] + part)

        first = jnp.logical_and(jnp.logical_and(i == 0, j == 0), k == 0)
        last = jnp.logical_and(jnp.logical_and(i == ni - 1, j == nj - 1), k == nk - 1)
        host.run(first, last, r_ins, r_outs, sems, compute)

    sem = ("arbitrary",) * 3 if rider else ("parallel", "parallel", "arbitrary")
    res = pl.pallas_call(
        body,
        name=name,
        grid=(nj, ni, nk) if j_outer else (ni, nj, nk),
        in_specs=[a_spec, b_spec] + [tile_spec for _ in extras] + host.in_specs,
        out_specs=out_specs + host.out_specs,
        out_shape=out_shape + host.out_shapes,
        scratch_shapes=([pltpu.VMEM((tm, tn), F32)] if n_acc else []) + host.scratch,
        compiler_params=_params(sem),
    )(a, b, *extras, *host.ins)
    return res[0] if len(res) == 1 else res


def _rms_fwd(name, x, g, tm=512):
    t, d = x.shape
    tm = _tile(t, tm)

    def body(x_ref, g_ref, h_ref):
        xf = x_ref[...]
        r = lax.rsqrt(jnp.mean(xf * xf, axis=-1, keepdims=True) + EPS)
        h_ref[...] = (xf * r * g_ref[...]).astype(h_ref.dtype)

    return pl.pallas_call(
        body,
        name=name,
        grid=(t // tm,),
        in_specs=[pl.BlockSpec((tm, d), lambda i: (i, 0)), pl.BlockSpec((1, d), lambda i: (0, 0))],
        out_specs=pl.BlockSpec((tm, d), lambda i: (i, 0)),
        out_shape=jax.ShapeDtypeStruct((t, d), BF16),
        compiler_params=_params(("parallel",)),
    )(x, g.reshape(1, d))


def _rms_bwd(name, dh, x, g, res=None, tm=256):
    t, d = x.shape
    tm = _tile(t, tm)
    has_res = res is not None

    def body(*refs):
        if has_res:
            dh_ref, x_ref, g_ref, res_ref, dx_ref, dxb_ref, gg_ref = refs
        else:
            dh_ref, x_ref, g_ref, dx_ref, dxb_ref, gg_ref = refs
        i = pl.program_id(0)
        xf = x_ref[...]
        r = lax.rsqrt(jnp.mean(xf * xf, axis=-1, keepdims=True) + EPS)
        xh = xf * r
        dhf = dh_ref[...].astype(F32)
        dxh = dhf * g_ref[...]
        dx = r * (dxh - xh * jnp.mean(dxh * xh, axis=-1, keepdims=True))
        if has_res:
            dx = dx + res_ref[...]
        dx_ref[...] = dx
        dxb_ref[...] = dx.astype(BF16)

        @pl.when(i == 0)
        def _():
            gg_ref[...] = jnp.zeros_like(gg_ref)

        gg_ref[...] += jnp.sum(dhf * xh, axis=0, keepdims=True)

    row = pl.BlockSpec((tm, d), lambda i: (i, 0))
    vec = pl.BlockSpec((1, d), lambda i: (0, 0))
    ins = [dh, x, g.reshape(1, d)] + ([res] if has_res else [])
    dx, dxb, gg = pl.pallas_call(
        body,
        name=name,
        grid=(t // tm,),
        in_specs=[row, row, vec] + ([row] if has_res else []),
        out_specs=[row, row, vec],
        out_shape=[jax.ShapeDtypeStruct((t, d), F32), jax.ShapeDtypeStruct((t, d), BF16), jax.ShapeDtypeStruct((1, d), F32)],
        compiler_params=_params(("arbitrary",)),
    )(*ins)
    return dx, dxb, gg.reshape(d)


def _head_rms(xf):
    r = lax.rsqrt(jnp.mean(xf * xf, axis=-1, keepdims=True) + EPS)
    return xf * r, r


def _head_rms_bwd(dy, xn, r, g):
    dxh = dy * g
    dx = r * (dxh - xn * jnp.mean(dxh * xn, axis=-1, keepdims=True))
    return dx, jnp.sum(dy * xn, axis=0, keepdims=True)


def _col_to_row(col):
    n = col.shape[0]
    eye = lax.broadcasted_iota(jnp.int32, (n, n), 0) == lax.broadcasted_iota(jnp.int32, (n, n), 1)
    return jnp.sum(jnp.where(eye, col, 0.0), axis=0, keepdims=True)


def _row_to_col(row):
    n = row.shape[1]
    eye = lax.broadcasted_iota(jnp.int32, (n, n), 0) == lax.broadcasted_iota(jnp.int32, (n, n), 1)
    return jnp.sum(jnp.where(eye, row, 0.0), axis=1, keepdims=True)


def _dproj_args(dproj, n_in):
    if dproj is None:
        return [], [], {}
    return [dproj], [ANY], {n_in: 0}


def _shift_down(u, s, rows):
    return jnp.where(rows >= s, pltpu.roll(u, s, axis=0), 0.0)


def _shift_up(u, s, rows, t):
    return jnp.where(rows < t - s, pltpu.roll(u, t - s, axis=0), 0.0)


def _conv_fwd(proj, off, conv_w, cb):
    t = proj.shape[0]
    c = conv_w.shape[1]
    blk0 = off // (3 * cb)

    def body(p_ref, w_ref, y_ref):
        rows = lax.broadcasted_iota(jnp.int32, (t, cb), 0)
        bg = p_ref[:, 0:cb].astype(F32)
        u = p_ref[:, cb:2 * cb].astype(F32) * p_ref[:, 2 * cb:3 * cb].astype(F32)
        w = w_ref[...]
        conv = w[2:3] * u + w[1:2] * _shift_down(u, 1, rows) + w[0:1] * _shift_down(u, 2, rows)
        y_ref[...] = (bg * conv).astype(y_ref.dtype)

    return pl.pallas_call(
        body,
        name="conv_fwd",
        grid=(c // cb,),
        in_specs=[pl.BlockSpec((t, 3 * cb), lambda j: (0, blk0 + j)), pl.BlockSpec((CONV_TAPS, cb), lambda j: (0, j))],
        out_specs=pl.BlockSpec((t, cb), lambda j: (0, j)),
        out_shape=jax.ShapeDtypeStruct((t, c), BF16),
        compiler_params=_params(("parallel",)),
    )(proj, conv_w)


def _conv_bwd(proj, off, conv_w, dy, cb, dproj, rider=None):
    t = proj.shape[0]
    c = conv_w.shape[1]
    blk0 = off // (3 * cb)
    nj = c // cb
    host = _Host(rider)

    def body(*refs):
        p_ref, w_ref, dy_ref = refs[:3]
        r_ins = refs[4:4 + host.n_in]
        dp_ref, gw_ref = refs[4 + host.n_in:6 + host.n_in]
        r_outs = refs[6 + host.n_in:6 + host.n_in + host.n_out]
        sems = refs[6 + host.n_in + host.n_out:]
        j = pl.program_id(0)

        def compute():
            rows = lax.broadcasted_iota(jnp.int32, (t, cb), 0)
            bg = p_ref[:, 0:cb].astype(F32)
            cg = p_ref[:, cb:2 * cb].astype(F32)
            v = p_ref[:, 2 * cb:3 * cb].astype(F32)
            u = cg * v
            w = w_ref[...]
            u1 = _shift_down(u, 1, rows)
            u2 = _shift_down(u, 2, rows)
            conv = w[2:3] * u + w[1:2] * u1 + w[0:1] * u2
            dyf = dy_ref[...].astype(F32)
            dconv = dyf * bg
            du = w[2:3] * dconv + w[1:2] * _shift_up(dconv, 1, rows, t) + w[0:1] * _shift_up(dconv, 2, rows, t)
            dp_ref[:, 0:cb] = (dyf * conv).astype(dp_ref.dtype)
            dp_ref[:, cb:2 * cb] = (du * v).astype(dp_ref.dtype)
            dp_ref[:, 2 * cb:3 * cb] = (du * cg).astype(dp_ref.dtype)
            gw_ref[0:1, :] = jnp.sum(dconv * u2, axis=0, keepdims=True)
            gw_ref[1:2, :] = jnp.sum(dconv * u1, axis=0, keepdims=True)
            gw_ref[2:3, :] = jnp.sum(dconv * u, axis=0, keepdims=True)

        host.run(j == 0, j == nj - 1, r_ins, r_outs, sems, compute)

    res = pl.pallas_call(
        body,
        name="conv_bwd",
        grid=(nj,),
        in_specs=[
            pl.BlockSpec((t, 3 * cb), lambda j: (0, blk0 + j)),
            pl.BlockSpec((CONV_TAPS, cb), lambda j: (0, j)),
            pl.BlockSpec((t, cb), lambda j: (0, j)),
            ANY,
        ] + host.in_specs,
        out_specs=[pl.BlockSpec((t, 3 * cb), lambda j: (0, blk0 + j)), pl.BlockSpec((CONV_TAPS, cb), lambda j: (0, j))] + host.out_specs,
        out_shape=[jax.ShapeDtypeStruct(dproj.shape, dproj.dtype), jax.ShapeDtypeStruct((CONV_TAPS, c), F32)] + host.out_shapes,
        input_output_aliases={3: 0},
        scratch_shapes=host.scratch,
        compiler_params=_params(("arbitrary",)),
    )(proj, conv_w, dy, dproj, *host.ins)
    return res


def _lane_scan(x, reverse):
    lane = lax.broadcasted_iota(jnp.int32, x.shape, 1)
    s = 1
    while s < LANES:
        if reverse:
            x = x + jnp.where(lane < LANES - s, pltpu.roll(x, LANES - s, axis=1), 0.0)
        else:
            x = x + jnp.where(lane >= s, pltpu.roll(x, s, axis=1), 0.0)
        s *= 2
    return x


def _scan_rows(src_ref, dst_ref, t, reverse, fn=None):
    groups = list(range(t // LANES))
    if reverse:
        groups = groups[::-1]
    carry = None
    for gi in groups:
        sl = slice(gi * LANES, (gi + 1) * LANES)
        blk = src_ref[:, sl]
        if fn is not None:
            blk = fn(blk)
        blk = _lane_scan(blk, reverse)
        if carry is not None:
            blk = blk + carry
        dst_ref[:, sl] = blk
        carry = blk[:, 0:1] if reverse else blk[:, LANES - 1:LANES]


def _forget_fwd(z_row, b_col):
    rows, t = z_row.shape

    def body(z_ref, b_ref, c_ref):
        def logf(z):
            zz = z + b_ref[...]
            return jnp.minimum(zz, 0.0) - jnp.log(1.0 + jnp.exp(-jnp.abs(zz)))

        _scan_rows(z_ref, c_ref, t, False, logf)

    return pl.pallas_call(
        body,
        name="forget_fwd",
        out_shape=jax.ShapeDtypeStruct((rows, t), F32),
        compiler_params=pltpu.CompilerParams(vmem_limit_bytes=VMEM_LIMIT),
    )(z_row, b_col)


def _rows_to_colb(c_row3, tq):
    heads, _, t = c_row3.shape

    def body(r_ref, o_ref):
        o_ref[...] = jnp.broadcast_to(_row_to_col(r_ref[...]), (tq, LANES))

    return pl.pallas_call(
        body,
        name="rows_to_colb",
        grid=(heads, t // tq),
        in_specs=[pl.BlockSpec((None, 1, tq), lambda h, i: (h, 0, i))],
        out_specs=pl.BlockSpec((None, tq, LANES), lambda h, i: (h, i, 0)),
        out_shape=jax.ShapeDtypeStruct((heads, t, LANES), F32),
        compiler_params=_params(("parallel", "parallel")),
    )(c_row3)


def _forget_bwd(z_row, b_col, dc_row):
    rows, t = z_row.shape

    def body(z_ref, b_ref, dc_ref, dz_ref, db_ref, tmp_ref):
        _scan_rows(dc_ref, tmp_ref, t, True)
        zz = z_ref[...] + b_ref[...]
        dz = tmp_ref[...] * (1.0 / (1.0 + jnp.exp(zz)))
        dz_ref[...] = dz.astype(dz_ref.dtype)
        db_ref[...] = jnp.sum(dz, axis=1, keepdims=True)

    return pl.pallas_call(
        body,
        name="forget_bwd",
        out_shape=[jax.ShapeDtypeStruct((rows, t), BF16), jax.ShapeDtypeStruct((rows, 1), F32)],
        scratch_shapes=[pltpu.VMEM((rows, t), F32)],
        compiler_params=pltpu.CompilerParams(vmem_limit_bytes=VMEM_LIMIT),
    )(z_row, b_col, dc_row)


def _fox_fwd(proj, off, gq, gk, c_row3, c_colb, heads, tq, rider=None):
    t = proj.shape[0]
    hd = FOX_HEAD_DIM
    tq = _tile(t, tq)
    nq = t // tq
    blk0 = off // hd
    scale = 1.0 / math.sqrt(hd)
    host = _Host(rider)

    def body(*refs):
        q_ref, k_ref, v_ref, gq_ref, gk_ref, crow_ref, ccol_ref = refs[:7]
        r_ins = refs[7:7 + host.n_in]
        o_ref, lse_ref = refs[7 + host.n_in:9 + host.n_in]
        r_outs = refs[9 + host.n_in:9 + host.n_in + host.n_out]
        khat_ref = refs[9 + host.n_in + host.n_out]
        sems = refs[10 + host.n_in + host.n_out:]
        h, qi = pl.program_id(0), pl.program_id(1)

        def compute():
            @pl.when(qi == 0)
            def _():
                kn, _ = _head_rms(k_ref[...].astype(F32))
                khat_ref[...] = (kn * gk_ref[...]).astype(BF16)

            qn, _ = _head_rms(q_ref[...].astype(F32))
            qhat = (qn * gq_ref[...]).astype(BF16)
            ccol = ccol_ref[:, 0:1]
            rows = qi * tq + lax.broadcasted_iota(jnp.int32, (tq, tq), 0)

            def step(j, carry):
                m, l, acc = carry
                ks = pl.multiple_of(j * tq, tq)
                s = _dot(qhat, khat_ref[pl.ds(ks, tq), :], NT) * scale + (ccol - crow_ref[:, pl.ds(ks, tq)])
                cols = ks + lax.broadcasted_iota(jnp.int32, (tq, tq), 1)
                s = jnp.where(rows >= cols, s, NEG)
                m_new = jnp.maximum(m, jnp.max(s, axis=-1, keepdims=True))
                alpha = jnp.exp(m - m_new)
                p = jnp.exp(s - m_new)
                l = alpha * l + jnp.sum(p, axis=-1, keepdims=True)
                acc = alpha * acc + _dot(p.astype(BF16), v_ref[pl.ds(ks, tq), :], NN)
                return m_new, l, acc

            init = (jnp.full((tq, 1), NEG, F32), jnp.zeros((tq, 1), F32), jnp.zeros((tq, hd), F32))
            m, l, acc = lax.fori_loop(0, qi + 1, step, init)
            o_ref[...] = (acc / l).astype(o_ref.dtype)
            lse_ref[...] = jnp.broadcast_to(m + jnp.log(l), (tq, LANES))

        first = jnp.logical_and(h == 0, qi == 0)
        last = jnp.logical_and(h == heads - 1, qi == nq - 1)
        host.run(first, last, r_ins, r_outs, sems, compute)

    res = pl.pallas_call(
        body,
        name="fox_fwd",
        grid=(heads, nq),
        in_specs=[
            pl.BlockSpec((tq, hd), lambda h, i: (i, blk0 + 3 * h)),
            pl.BlockSpec((t, hd), lambda h, i: (0, blk0 + 3 * h + 1)),
            pl.BlockSpec((t, hd), lambda h, i: (0, blk0 + 3 * h + 2)),
            pl.BlockSpec((1, hd), lambda h, i: (0, 0)),
            pl.BlockSpec((1, hd), lambda h, i: (0, 0)),
            pl.BlockSpec((None, 1, t), lambda h, i: (h, 0, 0)),
            pl.BlockSpec((None, tq, LANES), lambda h, i: (h, i, 0)),
        ] + host.in_specs,
        out_specs=[pl.BlockSpec((tq, hd), lambda h, i: (i, h)), pl.BlockSpec((None, tq, LANES), lambda h, i: (h, i, 0))] + host.out_specs,
        out_shape=[jax.ShapeDtypeStruct((t, heads * hd), BF16), jax.ShapeDtypeStruct((heads, t, LANES), F32)] + host.out_shapes,
        scratch_shapes=[pltpu.VMEM((t, hd), BF16)] + host.scratch,
        compiler_params=_params(("arbitrary", "arbitrary")),
    )(proj, proj, proj, gq.reshape(1, hd), gk.reshape(1, hd), c_row3, c_colb, *host.ins)
    return res


def _fox_bwd(proj, off, o, do, gq, gk, c_row3, c_colb, lse, heads, tq, dproj, rider=None):
    t = proj.shape[0]
    hd = FOX_HEAD_DIM
    tq = _tile(t, tq)
    nb = t // tq
    blk0 = off // hd
    scale = 1.0 / math.sqrt(hd)
    host = _Host(rider)
    n_fixed_in = 11

    def body(*refs):
        q_ref, k_ref, v_ref, o_ref, do_ref, gq_ref, gk_ref, crow_ref, ccol_ref, lse_ref = refs[:10]
        pos = n_fixed_in
        r_ins = refs[pos:pos + host.n_in]; pos += host.n_in
        dp_ref, dc_ref, ggq_ref, ggk_ref = refs[pos:pos + 4]; pos += 4
        r_outs = refs[pos:pos + host.n_out]; pos += host.n_out
        qhat_ref, khat_ref, dq_ref, dk_ref, dcq_ref, dck_ref, delta_ref = refs[pos:pos + 7]; pos += 7
        sems = refs[pos:]
        h = pl.program_id(0)

        def compute():
            qn, rq = _head_rms(q_ref[...].astype(F32))
            qhat_ref[...] = (qn * gq_ref[...]).astype(BF16)
            kn, rk = _head_rms(k_ref[...].astype(F32))
            khat_ref[...] = (kn * gk_ref[...]).astype(BF16)
            delta_ref[...] = jnp.sum(do_ref[...].astype(F32) * o_ref[...].astype(F32), axis=-1, keepdims=True)
            dq_ref[...] = jnp.zeros_like(dq_ref)
            dcq_ref[...] = jnp.zeros_like(dcq_ref)

            def kv_block(j, _):
                ks = pl.multiple_of(j * tq, tq)
                kh = khat_ref[pl.ds(ks, tq), :]
                vv = v_ref[pl.ds(ks, tq), :]
                crow = crow_ref[:, pl.ds(ks, tq)]
                cols = ks + lax.broadcasted_iota(jnp.int32, (tq, tq), 1)

                def q_block(i, carry):
                    dk, dv, dck = carry
                    qs = pl.multiple_of(i * tq, tq)
                    qh = qhat_ref[pl.ds(qs, tq), :]
                    dob = do_ref[pl.ds(qs, tq), :]
                    s = _dot(qh, kh, NT) * scale + (ccol_ref[pl.ds(qs, tq), 0:1] - crow)
                    rows = qs + lax.broadcasted_iota(jnp.int32, (tq, tq), 0)
                    p = jnp.where(rows >= cols, jnp.exp(s - lse_ref[pl.ds(qs, tq), 0:1]), 0.0)
                    ds = p * (_dot(dob, vv, NT) - delta_ref[pl.ds(qs, tq), :])
                    dsb = ds.astype(BF16)
                    dv = dv + _dot(p.astype(BF16), dob, TN)
                    dk = dk + _dot(dsb, qh, TN)
                    dq_ref[pl.ds(qs, tq), :] += _dot(dsb, kh, NN)
                    dcq_ref[pl.ds(qs, tq), :] += jnp.sum(ds, axis=-1, keepdims=True)
                    dck = dck + jnp.sum(ds, axis=0, keepdims=True)
                    return dk, dv, dck

                zero = jnp.zeros((tq, hd), F32)
                dk, dv, dck = lax.fori_loop(j, nb, q_block, (zero, zero, jnp.zeros((1, tq), F32)))
                dk_ref[pl.ds(ks, tq), :] = dk * scale
                dp_ref[pl.ds(ks, tq), 2 * hd:3 * hd] = dv.astype(dp_ref.dtype)
                dck_ref[:, pl.ds(ks, tq)] = dck
                return 0

            lax.fori_loop(0, nb, kv_block, 0)

            dq, ggq = _head_rms_bwd(dq_ref[...] * scale, qn, rq, gq_ref[...])
            dk, ggk = _head_rms_bwd(dk_ref[...], kn, rk, gk_ref[...])
            dp_ref[:, 0:hd] = dq.astype(dp_ref.dtype)
            dp_ref[:, hd:2 * hd] = dk.astype(dp_ref.dtype)
            for b in range(nb):
                sl = slice(b * tq, (b + 1) * tq)
                dc_ref[:, sl] = _col_to_row(dcq_ref[sl, :]) - dck_ref[:, sl]

            @pl.when(h == 0)
            def _():
                ggq_ref[...] = jnp.zeros_like(ggq_ref)
                ggk_ref[...] = jnp.zeros_like(ggk_ref)

            ggq_ref[...] += ggq
            ggk_ref[...] += ggk

        host.run(h == 0, h == heads - 1, r_ins, r_outs, sems, compute)

    head_in = lambda part: pl.BlockSpec((t, hd), lambda h: (0, blk0 + 3 * h + part))
    vec = pl.BlockSpec((1, hd), lambda h: (0, 0))
    colb = pl.BlockSpec((None, t, LANES), lambda h: (h, 0, 0))
    res = pl.pallas_call(
        body,
        name="fox_bwd",
        grid=(heads,),
        in_specs=[
            head_in(0), head_in(1), head_in(2),
            pl.BlockSpec((t, hd), lambda h: (0, h)),
            pl.BlockSpec((t, hd), lambda h: (0, h)),
            vec, vec,
            pl.BlockSpec((None, 1, t), lambda h: (h, 0, 0)),
            colb, colb, ANY,
        ] + host.in_specs,
        out_specs=[
            pl.BlockSpec((t, 3 * hd), lambda h: (0, blk0 // 3 + h)),
            pl.BlockSpec((None, 1, t), lambda h: (h, 0, 0)),
            vec, vec,
        ] + host.out_specs,
        out_shape=[
            jax.ShapeDtypeStruct(dproj.shape, dproj.dtype),
            jax.ShapeDtypeStruct((heads, 1, t), F32),
            jax.ShapeDtypeStruct((1, hd), F32),
            jax.ShapeDtypeStruct((1, hd), F32),
        ] + host.out_shapes,
        input_output_aliases={10: 0},
        scratch_shapes=[
            pltpu.VMEM((t, hd), BF16), pltpu.VMEM((t, hd), BF16),
            pltpu.VMEM((t, hd), F32), pltpu.VMEM((t, hd), F32),
            pltpu.VMEM((t, 1), F32), pltpu.VMEM((1, t), F32), pltpu.VMEM((t, 1), F32),
        ] + host.scratch,
        compiler_params=_params(("arbitrary",)),
    )(proj, proj, proj, o, do, gq.reshape(1, hd), gk.reshape(1, hd), c_row3, c_colb, lse, dproj, *host.ins)
    return res


def _mem_fwd(proj, off, kv, gq, gk, tq):
    t = proj.shape[0]
    m, width = kv.shape[0], kv.shape[1] // 2
    hd = width // MEM_HEADS
    tq = _tile(t, tq)
    blk0 = off // hd
    scale = 1.0 / math.sqrt(hd)

    def body(q_ref, k_ref, v_ref, gq_ref, gk_ref, o_ref):
        qn, _ = _head_rms(q_ref[...].astype(F32))
        kn, _ = _head_rms(k_ref[...])
        s = _dot((qn * gq_ref[...]).astype(BF16), (kn * gk_ref[...]).astype(BF16), NT) * scale
        p = jnp.exp(s - jnp.max(s, axis=-1, keepdims=True))
        p = p / jnp.sum(p, axis=-1, keepdims=True)
        o_ref[...] = _dot(p.astype(BF16), v_ref[...].astype(BF16), NN).astype(o_ref.dtype)

    vec = pl.BlockSpec((1, hd), lambda h, i: (0, 0))
    return pl.pallas_call(
        body,
        name="mem_fwd",
        grid=(MEM_HEADS, t // tq),
        in_specs=[
            pl.BlockSpec((tq, hd), lambda h, i: (i, blk0 + h)),
            pl.BlockSpec((m, hd), lambda h, i: (0, h)),
            pl.BlockSpec((m, hd), lambda h, i: (0, MEM_HEADS + h)),
            vec, vec,
        ],
        out_specs=pl.BlockSpec((tq, hd), lambda h, i: (i, h)),
        out_shape=jax.ShapeDtypeStruct((t, width), BF16),
        compiler_params=_params(("parallel", "parallel")),
    )(proj, kv, kv, gq.reshape(1, hd), gk.reshape(1, hd))


def _mem_bwd(proj, off, kv, do, gq, gk, tq, dproj, rider=None):
    t = proj.shape[0]
    m, width = kv.shape[0], kv.shape[1] // 2
    hd = width // MEM_HEADS
    tq = _tile(t, tq)
    nq = t // tq
    blk0 = off // hd
    scale = 1.0 / math.sqrt(hd)
    host = _Host(rider)

    def body(*refs):
        q_ref, k_ref, v_ref, do_ref, gq_ref, gk_ref = refs[:6]
        pos = 7
        r_ins = refs[pos:pos + host.n_in]; pos += host.n_in
        dq_ref, dk_ref, dv_ref, ggq_ref, ggk_ref = refs[pos:pos + 5]; pos += 5
        r_outs = refs[pos:pos + host.n_out]; pos += host.n_out
        dkh_ref, dvh_ref = refs[pos:pos + 2]; pos += 2
        sems = refs[pos:]
        h, i = pl.program_id(0), pl.program_id(1)

        def compute():
            qn, rq = _head_rms(q_ref[...].astype(F32))
            kn, rk = _head_rms(k_ref[...])
            qhat = (qn * gq_ref[...]).astype(BF16)
            khat = (kn * gk_ref[...]).astype(BF16)
            vb = v_ref[...].astype(BF16)
            dob = do_ref[...]
            s = _dot(qhat, khat, NT) * scale
            p = jnp.exp(s - jnp.max(s, axis=-1, keepdims=True))
            p = p / jnp.sum(p, axis=-1, keepdims=True)
            dp = _dot(dob, vb, NT)
            ds = p * (dp - jnp.sum(dp * p, axis=-1, keepdims=True))
            dsb = ds.astype(BF16)
            dq, ggq = _head_rms_bwd(_dot(dsb, khat, NN) * scale, qn, rq, gq_ref[...])
            dq_ref[...] = dq.astype(dq_ref.dtype)

            @pl.when(i == 0)
            def _():
                dkh_ref[...] = jnp.zeros_like(dkh_ref)
                dvh_ref[...] = jnp.zeros_like(dvh_ref)

            @pl.when(jnp.logical_and(h == 0, i == 0))
            def _():
                ggq_ref[...] = jnp.zeros_like(ggq_ref)
                ggk_ref[...] = jnp.zeros_like(ggk_ref)

            dkh_ref[...] += _dot(dsb, qhat, TN)
            dvh_ref[...] += _dot(p.astype(BF16), dob, TN)
            ggq_ref[...] += ggq

            @pl.when(i == nq - 1)
            def _():
                dk, ggk = _head_rms_bwd(dkh_ref[...] * scale, kn, rk, gk_ref[...])
                dk_ref[...] = dk.astype(dk_ref.dtype)
                dv_ref[...] = dvh_ref[...].astype(dv_ref.dtype)
                ggk_ref[...] += ggk

        first = jnp.logical_and(h == 0, i == 0)
        last = jnp.logical_and(h == MEM_HEADS - 1, i == nq - 1)
        host.run(first, last, r_ins, r_outs, sems, compute)

    vec = pl.BlockSpec((1, hd), lambda h, i: (0, 0))
    kblk = pl.BlockSpec((m, hd), lambda h, i: (0, h))
    res = pl.pallas_call(
        body,
        name="mem_bwd",
        grid=(MEM_HEADS, nq),
        in_specs=[
            pl.BlockSpec((tq, hd), lambda h, i: (i, blk0 + h)), kblk,
            pl.BlockSpec((m, hd), lambda h, i: (0, MEM_HEADS + h)),
            pl.BlockSpec((tq, hd), lambda h, i: (i, h)), vec, vec, ANY,
        ] + host.in_specs,
        out_specs=[pl.BlockSpec((tq, hd), lambda h, i: (i, blk0 + h)), kblk, kblk, vec, vec] + host.out_specs,
        out_shape=[
            jax.ShapeDtypeStruct(dproj.shape, dproj.dtype),
            jax.ShapeDtypeStruct((m, width), BF16),
            jax.ShapeDtypeStruct((m, width), BF16),
            jax.ShapeDtypeStruct((1, hd), F32),
            jax.ShapeDtypeStruct((1, hd), F32),
        ] + host.out_shapes,
        input_output_aliases={6: 0},
        scratch_shapes=[pltpu.VMEM((m, hd), F32), pltpu.VMEM((m, hd), F32)] + host.scratch,
        compiler_params=_params(("arbitrary", "arbitrary")),
    )(proj, kv, kv, do, gq.reshape(1, hd), gk.reshape(1, hd), dproj, *host.ins)
    dproj, dk, dv, ggq, ggk = res[:5]
    return (dproj, jnp.concatenate([dk, dv], axis=1), ggq.reshape(hd), ggk.reshape(hd), *res[5:])


def _sigmoid(z):
    return 1.0 / (1.0 + jnp.exp(-z))


def _merge_fwd(proj, o3, tm, tc):
    t, d = o3[0].shape
    tm = _tile(t, tm)

    def body(g_ref, oa_ref, ob_ref, oc_ref, out_ref):
        acc = jnp.zeros((tm, tc), F32)
        for s, o_ref in enumerate((oa_ref, ob_ref, oc_ref)):
            acc = acc + _sigmoid(g_ref[:, s * tc:(s + 1) * tc].astype(F32)) * o_ref[...].astype(F32)
        out_ref[...] = acc.astype(out_ref.dtype)

    blk = pl.BlockSpec((tm, tc), lambda i, j: (i, j))
    return pl.pallas_call(
        body,
        name="merge_fwd",
        grid=(t // tm, d // tc),
        in_specs=[pl.BlockSpec((tm, 3 * tc), lambda i, j: (i, j)), blk, blk, blk],
        out_specs=blk,
        out_shape=jax.ShapeDtypeStruct((t, d), BF16),
        compiler_params=_params(("parallel", "parallel")),
    )(proj, *o3)


def _merge_bwd(proj, o3, dm, tm, tc):
    t, d = dm.shape
    tm = _tile(t, tm)

    def body(g_ref, oa_ref, ob_ref, oc_ref, dm_ref, dg_ref, da_ref, db_ref, dc_ref):
        dmf = dm_ref[...].astype(F32)
        for s, (o_ref, do_ref) in enumerate(((oa_ref, da_ref), (ob_ref, db_ref), (oc_ref, dc_ref))):
            g = _sigmoid(g_ref[:, s * tc:(s + 1) * tc].astype(F32))
            do_ref[...] = (dmf * g).astype(do_ref.dtype)
            dg_ref[:, s * tc:(s + 1) * tc] = (dmf * o_ref[...].astype(F32) * g * (1.0 - g)).astype(dg_ref.dtype)

    blk = pl.BlockSpec((tm, tc), lambda i, j: (i, j))
    wide = pl.BlockSpec((tm, 3 * tc), lambda i, j: (i, j))
    return pl.pallas_call(
        body,
        name="merge_bwd",
        grid=(t // tm, d // tc),
        in_specs=[wide, blk, blk, blk, blk],
        out_specs=[wide, blk, blk, blk],
        out_shape=[jax.ShapeDtypeStruct(proj.shape, BF16)] + [jax.ShapeDtypeStruct((t, d), BF16)] * 3,
        compiler_params=_params(("parallel", "parallel")),
    )(proj, *o3, dm)


def _loss(dy, d):
    t = dy.shape[0]
    tm = _tile(t, 512)

    def body(dy_ref, out_ref):
        i = pl.program_id(0)

        @pl.when(i == 0)
        def _():
            out_ref[...] = jnp.zeros_like(out_ref)

        e = dy_ref[...]
        out_ref[...] += jnp.sum(jnp.sum(e * e, axis=0, keepdims=True), axis=1, keepdims=True) * (0.5 * d)

    out = pl.pallas_call(
        body,
        name="loss",
        grid=(t // tm,),
        in_specs=[pl.BlockSpec((tm, dy.shape[1]), lambda i: (i, 0))],
        out_specs=pl.BlockSpec((1, 1), lambda i: (0, 0)),
        out_shape=jax.ShapeDtypeStruct((1, 1), F32),
        compiler_params=_params(("arbitrary",)),
    )(dy)
    return out[0, 0]


def _pack_w_in(w_in_t, d, tc):
    cw = d // 2
    heads = cw // FOX_HEAD_DIM
    k = w_in_t.shape[1]
    o = 0
    conv = w_in_t[o:o + 3 * cw]; o += 3 * cw
    fox = w_in_t[o:o + 3 * cw]; o += 3 * cw
    f = w_in_t[o:o + heads]; o += heads
    mq = w_in_t[o:o + cw]; o += cw
    gate = w_in_t[o:o + N_BRANCHES * d]
    conv = conv.reshape(3, cw // LANES, LANES, k).transpose(1, 0, 2, 3).reshape(3 * cw, k)
    fox = fox.reshape(3, heads, FOX_HEAD_DIM, k).transpose(1, 0, 2, 3).reshape(3 * cw, k)
    gate = gate.reshape(N_BRANCHES, d // tc, tc, k).transpose(1, 0, 2, 3).reshape(N_BRANCHES * d, k)
    return jnp.concatenate([gate, conv, fox, mq], axis=0), jnp.pad(f, ((0, F_ROWS - heads), (0, 0)))


def _unpack_g_in(g_all, g_f, d, tc):
    cw = d // 2
    heads = cw // FOX_HEAD_DIM
    k = g_all.shape[1]
    o = 0
    gate = g_all[o:o + N_BRANCHES * d]; o += N_BRANCHES * d
    conv = g_all[o:o + 3 * cw]; o += 3 * cw
    fox = g_all[o:o + 3 * cw]; o += 3 * cw
    mq = g_all[o:o + cw]
    conv = conv.reshape(cw // LANES, 3, LANES, k).transpose(1, 0, 2, 3).reshape(3 * cw, k)
    fox = fox.reshape(heads, 3, FOX_HEAD_DIM, k).transpose(1, 0, 2, 3).reshape(3 * cw, k)
    gate = gate.reshape(d // tc, N_BRANCHES, tc, k).transpose(1, 0, 2, 3).reshape(N_BRANCHES * d, k)
    return jnp.concatenate([conv, fox, g_f[:heads], mq, gate], axis=0)


def _unblock(w8):
    return w8.transpose(1, 0, 2).reshape(w8.shape[1], -1)


def _tile2(r, cols, tr, tcols):
    if r % 8 == 0:
        return _tile(r, tr), cols
    return r, _tile(cols, tcols)


def _pair_sum(name, g8, got, c):
    _, r, cols = g8.shape
    tr, tcols = _tile2(r, cols, 256, 256)

    def body(c_ref, g_ref, s_ref, o_ref):
        o_ref[...] = (g_ref[...].astype(F32) + s_ref[...].astype(F32)).astype(o_ref.dtype)

    blk = pl.BlockSpec((None, tr, tcols), lambda q, i, j, c_ref: (q, i, j))
    return pl.pallas_call(
        body,
        name=name,
        grid_spec=pltpu.PrefetchScalarGridSpec(
            num_scalar_prefetch=1,
            grid=(N_CHIPS, r // tr, cols // tcols),
            in_specs=[pl.BlockSpec((None, tr, tcols), lambda q, i, j, c_ref: (2 * q + c_ref[0], i, j)), blk],
            out_specs=blk,
        ),
        out_shape=jax.ShapeDtypeStruct((N_CHIPS, r, cols), BF16),
        compiler_params=_params(("parallel", "parallel", "parallel")),
    )(c, g8, got)


def _local_step(x, mem, target, w, small, comm=None):
    t, d = x.shape
    cw = d // 2
    heads = cw // FOX_HEAD_DIM
    tc = min(512, d)
    tq = min(512, t)
    off_conv, off_fox, off_mq = 3 * d, 3 * d + 3 * cw, 3 * d + 6 * cw
    w = dict(w)
    w_all, w_f = _pack_w_in(w["w_in"], d, tc)
    big = dict(tm=1024, tn=512, tk=2048)
    wide_k = dict(tm=512, tn=1024, tk=4096)

    h = _rms_fwd("rms1_fwd", x, small["norm1_g"])
    if comm:
        early = ("w_conv_out", "w_fox_out", "w_mem_out", "w_out", "w_up")
        proj, *got = _matmul("proj", "nt", h, w_all, outs=[BF16], rider=_gather_rider([comm["shards"][n] for n in early]), **big)
        for n, val in zip(early, got):
            w[n] = val.reshape(-1, val.shape[-1]) if n == "w_out" else _unblock(val)
    else:
        proj = _matmul("proj", "nt", h, w_all, outs=[BF16], **big)
    z_row = _matmul("proj_f", "nt", w_f, h, outs=[F32], tm=F_ROWS, tn=512, tk=2048)

    y_conv = _conv_fwd(proj, off_conv, small["conv_w"], LANES)

    b_col = jnp.pad(small["b_f"], (0, F_ROWS - heads)).reshape(F_ROWS, 1)
    c_row3 = _forget_fwd(z_row, b_col)[:heads].reshape(heads, 1, t)
    c_colb = _rows_to_colb(c_row3, tq)
    if comm:
        y_fox, lse, got = _fox_fwd(proj, off_fox, small["fox_q_g"], small["fox_k_g"], c_row3, c_colb, heads, tq,
                                   rider=_gather_rider([comm["shards"]["w_down"]]))
        w["w_down"] = got.reshape(-1, got.shape[-1])
    else:
        y_fox, lse = _fox_fwd(proj, off_fox, small["fox_q_g"], small["fox_k_g"], c_row3, c_colb, heads, tq)

    nm = _rms_fwd("mem_rms_fwd", mem, small["mem_norm_g"])
    kv = _matmul("mem_kv", "nn", nm, w["w_mem_kv"], outs=[F32], tm=256, tn=512, tk=2048)
    y_mem = _mem_fwd(proj, off_mq, kv, small["mem_q_g"], small["mem_k_g"], tq)

    ys = (y_conv, y_fox, y_mem)
    w_outs = (w["w_conv_out"], w["w_fox_out"], w["w_mem_out"])
    o3 = [_matmul(f"branch_out{s}", "nn", ys[s], w_outs[s], outs=[BF16], **big) for s in range(3)]
    merged = _merge_fwd(proj, o3, 512, tc)
    x1 = _matmul("out_proj", "nn", merged, w["w_out"], outs=[F32], extras=[x],
                 epilogue=lambda acc, xr: (acc + xr,), **big)
    h2 = _rms_fwd("rms2_fwd", x1, small["norm2_g"])

    def up_epilogue(acc):
        return acc, jnp.square(jnp.maximum(acc, 0.0))

    up, act = _matmul("mlp_up", "nn", h2, w["w_up"], outs=[BF16, BF16], epilogue=up_epilogue, **big)

    def loss_epilogue(acc, x1r, tr):
        dy = (acc + x1r - tr) * (1.0 / d)
        return dy, dy

    dy, dyb = _matmul("mlp_down", "nn", act, w["w_down"], outs=[F32, BF16], extras=[x1, target],
                      epilogue=loss_epilogue, **big)

    def dup_epilogue(acc, upr):
        return (acc * 2.0 * jnp.maximum(upr.astype(F32), 0.0),)

    def by_owner(g):
        return g.reshape(N_DEV, -1, g.shape[-1])

    g, parts = {}, {}
    g["w_down"] = _matmul("d_w_down", "tn", act, dyb, outs=[BF16], **wide_k)
    if comm:
        dup, got = _matmul("d_act", "nt", dyb, w["w_down"], outs=[BF16], extras=[up], epilogue=dup_epilogue,
                           rider=_pair_rider([by_owner(g["w_down"])]), **big)
        pair = _pair_sum("pair_w_down", by_owner(g["w_down"]), got, comm["c"])
        g["w_up"], parts["w_down"] = _matmul("d_w_up", "tn", h2, dup, outs=[BF16], out_blocks=True,
                                             rider=_chip_rider([pair]), **wide_k)
        dh2, got = _matmul("d_h2", "nt", dup, w["w_up"], outs=[F32], rider=_pair_rider([g["w_up"]]), **big)
        pair_up = _pair_sum("pair_w_up", g["w_up"], got, comm["c"])
    else:
        dup = _matmul("d_act", "nt", dyb, w["w_down"], outs=[BF16], extras=[up], epilogue=dup_epilogue, **big)
        g["w_up"] = _matmul("d_w_up", "tn", h2, dup, outs=[BF16], out_blocks=True, **wide_k)
        dh2 = _matmul("d_h2", "nt", dup, w["w_up"], outs=[F32], **big)
    dx1, dx1b, g_norm2 = _rms_bwd("rms2_bwd", dh2, x1, small["norm2_g"], res=dy)
    loss = _loss(dy, d)

    g["w_out"] = _matmul("d_w_out", "tn", merged, dx1b, outs=[BF16], **wide_k)
    dmerged = _matmul("d_merged", "nt", dx1b, w["w_out"], outs=[BF16], **big)
    dproj, *do3 = _merge_bwd(proj, o3, dmerged, 512, tc)
    names = ("w_conv_out", "w_fox_out", "w_mem_out")
    dys = []
    for s in range(3):
        g[names[s]] = _matmul(f"d_w_branch{s}", "tn", ys[s], do3[s], outs=[BF16], out_blocks=True, **wide_k)
        dys.append(_matmul(f"d_branch{s}", "nt", do3[s], w_outs[s], outs=[BF16], **big))

    dproj, dkv, g_mq, g_mk = _mem_bwd(proj, off_mq, kv, dys[2], small["mem_q_g"], small["mem_k_g"], tq, dproj)
    g["w_mem_kv"] = _matmul("d_w_mem_kv", "tn", nm, dkv, outs=[BF16], **wide_k)
    dnm = _matmul("d_mem_norm", "nt", dkv, w["w_mem_kv"], outs=[F32], tm=256, tn=512, tk=2048)
    _, _, g_mem_norm = _rms_bwd("mem_rms_bwd", dnm, mem, small["mem_norm_g"])

    mid = ("w_out", "w_conv_out", "w_fox_out", "w_mem_out", "w_mem_kv")
    if comm:
        mid8 = [g[n] if n in names else by_owner(g[n]) for n in mid]
        dproj, g_conv_w, *got = _conv_bwd(proj, off_conv, small["conv_w"], dys[0], LANES, dproj, rider=_pair_rider(mid8))
        pairs = [pair_up] + [_pair_sum("pair_" + n, g8, s4, comm["c"]) for n, g8, s4 in zip(mid, mid8, got)]
        dproj, dc, g_fq, g_fk, *got = _fox_bwd(proj, off_fox, y_fox, dys[1], small["fox_q_g"], small["fox_k_g"], c_row3, c_colb,
                                               lse, heads, tq, dproj, rider=_chip_rider(pairs))
        parts.update(zip(("w_up",) + mid, got))
    else:
        dproj, g_conv_w = _conv_bwd(proj, off_conv, small["conv_w"], dys[0], LANES, dproj)
        dproj, dc, g_fq, g_fk = _fox_bwd(proj, off_fox, y_fox, dys[1], small["fox_q_g"], small["fox_k_g"], c_row3, c_colb,
                                         lse, heads, tq, dproj)
    dc_row = jnp.pad(dc.reshape(heads, t), ((0, F_ROWS - heads), (0, 0)))
    dz_row, db = _forget_bwd(z_row, b_col, dc_row)

    g_all = _matmul("d_w_in", "tn", dproj, h, outs=[BF16], j_outer=True, **wide_k)
    g_wf = _matmul("d_w_f", "nn", dz_row, h, outs=[BF16], tm=F_ROWS, tn=512, tk=4096)
    g["w_in"] = _unpack_g_in(g_all, g_wf, d, tc)
    dh = _matmul("d_h_f", "tn", dz_row, w_f, outs=[F32], tm=1024, tn=512, tk=F_ROWS)
    add_prev = lambda acc, prev: (acc + prev,)
    if comm:
        g_in8 = by_owner(g["w_in"])
        got = _run_rider("pair_exchange_w_in", _pair_rider([g_in8]))[0]
        pair = _pair_sum("pair_w_in", g_in8, got, comm["c"])
        dh, parts["w_in"] = _matmul("d_h", "nn", dproj, w_all, outs=[F32], extras=[dh], epilogue=add_prev,
                                    rider=_chip_rider([pair]), tm=1024, tn=512, tk=3328)
    else:
        dh = _matmul("d_h", "nn", dproj, w_all, outs=[F32], extras=[dh], epilogue=add_prev, tm=1024, tn=512, tk=3328)
    grad_x, _, g_norm1 = _rms_bwd("rms1_bwd", dh, x, small["norm1_g"], res=dx1)

    gs = dict(norm1_g=g_norm1, b_f=db[:heads, 0], conv_w=g_conv_w, fox_q_g=g_fq.reshape(-1), fox_k_g=g_fk.reshape(-1),
              mem_norm_g=g_mem_norm, mem_q_g=g_mq, mem_k_g=g_mk, norm2_g=g_norm2)
    return loss, grad_x, (parts if comm else g), gs


def _adamw_math(w, g, m, v):
    m = ADAM_B1 * m + (1.0 - ADAM_B1) * g
    v = ADAM_B2 * v + (1.0 - ADAM_B2) * jnp.square(g)
    m_hat = m / (1.0 - ADAM_B1 ** ADAM_STEP)
    v_hat = v / (1.0 - ADAM_B2 ** ADAM_STEP)
    delta = -ADAM_LR * (m_hat / (jnp.sqrt(v_hat) + ADAM_EPS) + ADAM_WD * w)
    return delta, m, v


def _adamw(name, parts, w, m, v):
    r, c = w.shape
    tr, tc = _tile2(r, c, 128, 256)
    n_parts = parts.shape[0]

    def body(p_ref, w_ref, m_ref, v_ref, g_ref, d_ref, nm_ref, nv_ref):
        g = p_ref[0].astype(F32)
        for s in range(1, n_parts):
            g = g + p_ref[s].astype(F32)
        delta, nm, nv = _adamw_math(w_ref[...], g, m_ref[...], v_ref[...])
        g_ref[...] = g
        d_ref[...] = delta
        nm_ref[...] = nm
        nv_ref[...] = nv

    blk = pl.BlockSpec((tr, tc), lambda i, j: (i, j))
    return pl.pallas_call(
        body,
        name=name,
        grid=(r // tr, c // tc),
        in_specs=[pl.BlockSpec((n_parts, tr, tc), lambda i, j: (0, i, j)), blk, blk, blk],
        out_specs=[blk] * 4,
        out_shape=[jax.ShapeDtypeStruct((r, c), F32)] * 4,
        compiler_params=_params(("parallel", "parallel")),
    )(parts, w, m, v)


def _sum_parts(name, parts):
    n_parts, r, c = parts.shape

    def body(p_ref, o_ref):
        acc = p_ref[0]
        for s in range(1, n_parts):
            acc = acc + p_ref[s]
        o_ref[...] = acc

    return pl.pallas_call(body, name=name, out_shape=jax.ShapeDtypeStruct((r, c), F32))(parts)


BIG = ("w_in", "w_mem_kv", "w_conv_out", "w_fox_out", "w_mem_out", "w_out", "w_up", "w_down")
COLUMN_SPLIT = ("w_in", "w_conv_out", "w_fox_out", "w_mem_out", "w_up")
SMALL = ("norm1_g", "b_f", "conv_w", "fox_q_g", "fox_k_g", "mem_norm_g", "mem_q_g", "mem_k_g", "norm2_g")
WEIGHTS = ("norm1_g", "w_in", "b_f", "conv_w", "fox_q_g", "fox_k_g", "mem_norm_g", "w_mem_kv", "mem_q_g", "mem_k_g",
           "w_conv_out", "w_fox_out", "w_mem_out", "w_out", "norm2_g", "w_up", "w_down")


def _pack(vectors):
    rows = []
    for vec in vectors:
        n = vec.shape[0]
        rows.append(jnp.pad(vec, (0, -n % LANES)).reshape(-1, LANES))
    out = jnp.concatenate(rows, axis=0)
    return jnp.pad(out, ((0, -out.shape[0] % 8), (0, 0)))


def _unpack(packed, sizes):
    out, row = [], 0
    for n in sizes:
        nr = -(-n // LANES)
        out.append(packed[row:row + nr].reshape(-1)[:n])
        row += nr
    return out


def kernel(x, mem, norm1_g, w_in, b_f, conv_w, fox_q_g, fox_k_g, mem_norm_g, w_mem_kv, mem_q_g, mem_k_g, w_conv_out, w_fox_out, w_mem_out, w_out, norm2_g, w_up, w_down, loss_target, m_norm1_g, m_w_in, m_b_f, m_conv_w, m_fox_q_g, m_fox_k_g, m_mem_norm_g, m_w_mem_kv, m_mem_q_g, m_mem_k_g, m_w_conv_out, m_w_fox_out, m_w_mem_out, m_w_out, m_norm2_g, m_w_up, m_w_down, v_norm1_g, v_w_in, v_b_f, v_conv_w, v_fox_q_g, v_fox_k_g, v_mem_norm_g, v_w_mem_kv, v_mem_q_g, v_mem_k_g, v_w_conv_out, v_w_fox_out, v_w_mem_out, v_w_out, v_norm2_g, v_w_up, v_w_down):
    args = dict(locals())
    wts = {n: args[n] for n in WEIGHTS}
    ms = {n: args["m_" + n] for n in WEIGHTS}
    vs = {n: args["v_" + n] for n in WEIGHTS}
    x_pos, y_pos, c_pos = _position()
    me = _index(x_pos, y_pos, c_pos)

    shards = {n: (wts[n].T if n == "w_in" else wts[n]).astype(BF16) for n in BIG}
    wi, wkv, cw8 = _run_rider("all_gather_first", _gather_rider([shards["w_in"], shards["w_mem_kv"], conv_w]))
    full = {"w_in": wi.reshape(-1, wi.shape[-1]), "w_mem_kv": wkv.reshape(-1, wkv.shape[-1])}
    small = {n: wts[n] for n in SMALL}
    small["conv_w"] = _unblock(cw8)
    comm = {"shards": shards, "c": c_pos.astype(jnp.int32).reshape(1)}

    loss, grad_x, parts, gs = _local_step(x[0], mem[0], loss_target[0], full, small, comm)

    out_g, out_d, out_m, out_v = {}, {}, {}, {}
    for n in BIG:
        if n == "w_in":
            res = _adamw("adamw_" + n, parts[n], wts[n].T, ms[n].T, vs[n].T)
            out_g[n], out_d[n], out_m[n], out_v[n] = (r.T for r in res)
        else:
            out_g[n], out_d[n], out_m[n], out_v[n] = _adamw("adamw_" + n, parts[n], wts[n], ms[n], vs[n])

    small_sizes = [int(math.prod(gs[n].shape)) for n in SMALL]
    packed = _pack([gs[n].reshape(-1) for n in SMALL])
    gsum = _sum_parts("sum_small", _run_rider("exchange_small", _broadcast_rider([packed]))[0])
    gsmall = dict(zip(SMALL, _unpack(gsum, small_sizes)))
    cols = conv_w.shape[1]
    gsmall["conv_w"] = lax.dynamic_slice(gsmall["conv_w"].reshape(CONV_TAPS, -1), (0, me * cols), (CONV_TAPS, cols)).reshape(-1)
    pg, pw, pm, pv = (_pack([src[n].reshape(-1) for n in SMALL]) for src in (gsmall, wts, ms, vs))
    _, sd, sm, sv = _adamw("adamw_small", pg[None], pw, pm, pv)
    local_sizes = [int(math.prod(wts[n].shape)) for n in SMALL]
    for dst, src in ((out_d, sd), (out_m, sm), (out_v, sv)):
        for n, val in zip(SMALL, _unpack(src, local_sizes)):
            dst[n] = val.reshape(wts[n].shape)
    for n in SMALL:
        out_g[n] = gsmall[n].reshape(wts[n].shape)

    loss = lax.psum(loss, MESH_AXES)
    return (loss, grad_x[None], *[out_g[n] for n in WEIGHTS], *[out_d[n] for n in WEIGHTS],
            *[out_m[n] for n in WEIGHTS], *[out_v[n] for n in WEIGHTS])
```

```python
import math

import jax
import jax.numpy as jnp
from jax import lax
from jax.experimental import pallas as pl
from jax.experimental.pallas import tpu as pltpu

F32 = jnp.float32
BF16 = jnp.bfloat16

EPS = 1e-6
N_DEV = 8
N_CHIPS = 4
FOX_HEAD_DIM = 128
MEM_HEADS = 4
CONV_TAPS = 3
N_BRANCHES = 3
F_ROWS = 16

ADAM_LR = 0.001
ADAM_B1 = 0.9
ADAM_B2 = 0.999
ADAM_EPS = 1e-08
ADAM_WD = 0.01
ADAM_STEP = 10

V7X_VMEM_BYTES = 64 * 1024 * 1024
VMEM_LIMIT = V7X_VMEM_BYTES * 3 // 4
LANES = 128
NEG = -1e30

MESH_AXES = ("x", "y", "c")
MESH = pl.DeviceIdType.MESH
ANY = pl.BlockSpec(memory_space=pl.ANY)

NN = (((1,), (0,)), ((), ()))
NT = (((1,), (1,)), ((), ()))
TN = (((0,), (0,)), ((), ()))


def _params(sem):
    return pltpu.CompilerParams(dimension_semantics=sem, vmem_limit_bytes=VMEM_LIMIT)


def _dot(a, b, dn):
    return lax.dot_general(a, b, dn, preferred_element_type=F32)


def _tile(n, t):
    if n <= t:
        return n
    for cand in range(t - t % LANES, 0, -LANES):
        if n % cand == 0:
            return cand
    raise ValueError((n, t))


class _Rider:
    def __init__(self, ins, out_shapes, sem_shapes, start, finish):
        self.ins, self.out_shapes, self.sem_shapes = list(ins), list(out_shapes), list(sem_shapes)
        self.start, self.finish = start, finish


def _position():
    return lax.axis_index("x"), lax.axis_index("y"), lax.axis_index("c")


def _index(px, py, pc):
    return 4 * px + 2 * py + pc


def _dma_sems(n, per):
    return [pltpu.SemaphoreType.DMA((n, per)), pltpu.SemaphoreType.DMA((n, per)), pltpu.SemaphoreType.DMA((n,))]


def _gather_rider(shards):
    n = len(shards)

    def copies(ins, outs, sems):
        send_sems, recv_sems, local_sems = sems
        x, y, c = _position()
        me, sibling = (x, y, c), (x, y, 1 - c)
        chips = [(1 - x, y), (x, 1 - y), (1 - x, 1 - y)]

        def copy(a, k, block, to, src=None):
            rows = outs[a].at[_index(*block)]
            return pltpu.make_async_remote_copy(
                src_ref=rows if src is None else src, dst_ref=rows,
                send_sem=send_sems.at[a, k], recv_sem=recv_sems.at[a, k], device_id=to, device_id_type=MESH)

        mine = [pltpu.make_async_copy(ins[a], outs[a].at[_index(*me)], local_sems.at[a]) for a in range(n)]
        first = []
        for a in range(n):
            first.append(copy(a, 0, me, sibling, src=ins[a]))
            first += [copy(a, 1 + j, me, (*chip, c), src=ins[a]) for j, chip in enumerate(chips)]
        return copy, mine, first, me, sibling, chips, c

    def start(ins, outs, sems):
        _, mine, first, *_ = copies(ins, outs, sems)
        for cp in mine + first:
            cp.start()

    def finish(ins, outs, sems):
        copy, mine, first, me, sibling, chips, c = copies(ins, outs, sems)
        passed = []
        for a in range(n):
            for j, chip in enumerate(chips):
                copy(a, 1 + j, (*chip, c), me).wait_recv()
                fwd = copy(a, 4 + j, (*chip, c), sibling)
                fwd.start()
                passed.append(fwd)
        for a in range(n):
            copy(a, 0, sibling, me).wait_recv()
            for j, chip in enumerate(chips):
                copy(a, 4 + j, (*chip, 1 - c), me).wait_recv()
        for cp in first + passed:
            cp.wait_send()
        for cp in mine:
            cp.wait()

    out_shapes = [jax.ShapeDtypeStruct((N_DEV,) + s.shape, s.dtype) for s in shards]
    return _Rider(shards, out_shapes, _dma_sems(n, 7), start, finish)


def _pair_rider(grads):
    n = len(grads)

    def copies(ins, outs, sems):
        send_sems, recv_sems, _ = sems
        x, y, c = _position()
        return [pltpu.make_async_remote_copy(
            src_ref=ins[a].at[2 * q + (1 - c)], dst_ref=outs[a].at[q],
            send_sem=send_sems.at[a, q], recv_sem=recv_sems.at[a, q], device_id=(x, y, 1 - c), device_id_type=MESH)
            for a in range(n) for q in range(N_CHIPS)]

    def start(ins, outs, sems):
        for cp in copies(ins, outs, sems):
            cp.start()

    def finish(ins, outs, sems):
        cps = copies(ins, outs, sems)
        for cp in cps:
            cp.wait_recv()
        for cp in cps:
            cp.wait_send()

    out_shapes = [jax.ShapeDtypeStruct((N_CHIPS,) + g.shape[1:], g.dtype) for g in grads]
    return _Rider(grads, out_shapes, _dma_sems(n, N_CHIPS), start, finish)


def _chip_rider(parts):
    n = len(parts)

    def copies(ins, outs, sems):
        send_sems, recv_sems, local_sems = sems
        x, y, c = _position()
        q_me = 2 * x + y
        chips = [(1 - x, y), (x, 1 - y), (1 - x, 1 - y)]
        mine = [pltpu.make_async_copy(ins[a].at[q_me], outs[a].at[q_me], local_sems.at[a]) for a in range(n)]
        sends, arrivals = [], []
        for a in range(n):
            for j, (tx, ty) in enumerate(chips):
                q_t = 2 * tx + ty
                sends.append(pltpu.make_async_remote_copy(
                    src_ref=ins[a].at[q_t], dst_ref=outs[a].at[q_me],
                    send_sem=send_sems.at[a, j], recv_sem=recv_sems.at[a, j], device_id=(tx, ty, c), device_id_type=MESH))
                arrivals.append(pltpu.make_async_remote_copy(
                    src_ref=ins[a].at[q_t], dst_ref=outs[a].at[q_t],
                    send_sem=send_sems.at[a, j], recv_sem=recv_sems.at[a, j], device_id=(tx, ty, c), device_id_type=MESH))
        return mine, sends, arrivals

    def start(ins, outs, sems):
        mine, sends, _ = copies(ins, outs, sems)
        for cp in mine + sends:
            cp.start()

    def finish(ins, outs, sems):
        mine, sends, arrivals = copies(ins, outs, sems)
        for cp in arrivals:
            cp.wait_recv()
        for cp in sends:
            cp.wait_send()
        for cp in mine:
            cp.wait()

    out_shapes = [jax.ShapeDtypeStruct(p.shape, p.dtype) for p in parts]
    return _Rider(parts, out_shapes, _dma_sems(n, 3), start, finish)


def _broadcast_rider(values):
    n = len(values)

    def copies(ins, outs, sems):
        send_sems, recv_sems, local_sems = sems
        x, y, c = _position()
        me = _index(x, y, c)

        def peer(k):
            return (1 - x if k & 4 else x, 1 - y if k & 2 else y, 1 - c if k & 1 else c)

        mine = [pltpu.make_async_copy(ins[a], outs[a].at[me], local_sems.at[a]) for a in range(n)]
        sends, arrivals = [], []
        for a in range(n):
            for k in range(1, N_DEV):
                common = dict(send_sem=send_sems.at[a, k - 1], recv_sem=recv_sems.at[a, k - 1], device_id=peer(k), device_id_type=MESH)
                sends.append(pltpu.make_async_remote_copy(src_ref=ins[a], dst_ref=outs[a].at[me], **common))
                arrivals.append(pltpu.make_async_remote_copy(src_ref=ins[a], dst_ref=outs[a].at[_index(*peer(k))], **common))
        return mine, sends, arrivals

    def start(ins, outs, sems):
        mine, sends, _ = copies(ins, outs, sems)
        for cp in mine + sends:
            cp.start()

    def finish(ins, outs, sems):
        mine, sends, arrivals = copies(ins, outs, sems)
        for cp in arrivals:
            cp.wait_recv()
        for cp in sends:
            cp.wait_send()
        for cp in mine:
            cp.wait()

    out_shapes = [jax.ShapeDtypeStruct((N_DEV,) + v.shape, v.dtype) for v in values]
    return _Rider(values, out_shapes, _dma_sems(n, 7), start, finish)


def _run_rider(name, rider):
    n_in, n_out = len(rider.ins), len(rider.out_shapes)

    def body(*refs):
        ins, outs, sems = refs[:n_in], refs[n_in:n_in + n_out], refs[n_in + n_out:]
        rider.start(ins, outs, sems)
        rider.finish(ins, outs, sems)

    return pl.pallas_call(
        body, name=name, in_specs=[ANY] * n_in, out_specs=[ANY] * n_out, out_shape=rider.out_shapes,
        scratch_shapes=rider.sem_shapes)(*rider.ins)


class _Host:
    def __init__(self, rider):
        self.rider = rider
        self.n_in = len(rider.ins) if rider else 0
        self.n_out = len(rider.out_shapes) if rider else 0
        self.n_sem = len(rider.sem_shapes) if rider else 0
        self.ins = rider.ins if rider else []
        self.in_specs = [ANY] * self.n_in
        self.out_specs = [ANY] * self.n_out
        self.out_shapes = rider.out_shapes if rider else []
        self.scratch = rider.sem_shapes if rider else []

    def run(self, first, last, ins, outs, sems, compute):
        if self.rider is None:
            compute()
            return

        @pl.when(first)
        def _():
            self.rider.start(ins, outs, sems)

        compute()

        @pl.when(last)
        def _():
            self.rider.finish(ins, outs, sems)


def _matmul(name, kind, a, b, *, tm, tn, tk, outs, epilogue=None, extras=(), out_blocks=False, rider=None, j_outer=False):
    if kind == "nn":
        (m, kdim), n = a.shape, b.shape[1]
    elif kind == "nt":
        (m, kdim), n = a.shape, b.shape[0]
    else:
        (kdim, m), n = a.shape, b.shape[1]
    if out_blocks:
        tn = min(tn, n // N_DEV)
    tm, tn, tk = _tile(m, tm), _tile(n, tn), _tile(kdim, tk)
    ni, nj, nk = m // tm, n // tn, kdim // tk

    def spec(shape, fn):
        return pl.BlockSpec(shape, (lambda g0, g1, k: fn(g1, g0, k)) if j_outer else fn)

    a_spec = spec((tk, tm), lambda i, j, k: (k, i)) if kind == "tn" else spec((tm, tk), lambda i, j, k: (i, k))
    b_spec = spec((tn, tk), lambda i, j, k: (j, k)) if kind == "nt" else spec((tk, tn), lambda i, j, k: (k, j))
    dn = {"nn": NN, "nt": NT, "tn": TN}[kind]

    tile_spec = spec((tm, tn), lambda i, j, k: (i, j))
    if out_blocks:
        width = n // N_DEV
        r_out = width // tn
        out_shape = [jax.ShapeDtypeStruct((N_DEV, m, width), dt) for dt in outs]
        out_specs = [spec((None, tm, tn), lambda i, j, k: (j // r_out, i, j % r_out)) for _ in outs]
    else:
        out_shape = [jax.ShapeDtypeStruct((m, n), dt) for dt in outs]
        out_specs = [tile_spec for _ in outs]
    n_ex, n_out = len(extras), len(outs)
    host = _Host(rider)
    n_acc = 1 if nk > 1 else 0

    def body(*refs):
        a_ref, b_ref = refs[0], refs[1]
        pos = 2
        ex_refs = refs[pos:pos + n_ex]; pos += n_ex
        r_ins = refs[pos:pos + host.n_in]; pos += host.n_in
        out_refs = refs[pos:pos + n_out]; pos += n_out
        r_outs = refs[pos:pos + host.n_out]; pos += host.n_out
        acc_ref = refs[pos] if n_acc else None
        sems = refs[pos + n_acc:]
        i, j, k = pl.program_id(1 if j_outer else 0), pl.program_id(0 if j_outer else 1), pl.program_id(2)

        def finish_tile(acc):
            vals = (acc,) if epilogue is None else epilogue(acc, *[e[...] for e in ex_refs])
            for o_ref, v in zip(out_refs, vals):
                o_ref[...] = v.astype(o_ref.dtype)

        def compute():
            part = _dot(a_ref[...], b_ref[...], dn)
            if nk == 1:
                finish_tile(part)
                return

            @pl.when(k == 0)
            def _():
                acc_ref[...] = part

            @pl.when(jnp.logical_and(k > 0, k < nk - 1))
            def _():
                acc_ref[...] += part

            @pl.when(k == nk - 1)
            def _():
                finish_tile(acc_ref[...] + part)

        first = jnp.logical_and(jnp.logical_and(i == 0, j == 0), k == 0)
        last = jnp.logical_and(jnp.logical_and(i == ni - 1, j == nj - 1), k == nk - 1)
        host.run(first, last, r_ins, r_outs, sems, compute)

    sem = ("arbitrary",) * 3 if rider else ("parallel", "parallel", "arbitrary")
    res = pl.pallas_call(
        body,
        name=name,
        grid=(nj, ni, nk) if j_outer else (ni, nj, nk),
        in_specs=[a_spec, b_spec] + [tile_spec for _ in extras] + host.in_specs,
        out_specs=out_specs + host.out_specs,
        out_shape=out_shape + host.out_shapes,
        scratch_shapes=([pltpu.VMEM((tm, tn), F32)] if n_acc else []) + host.scratch,
        compiler_params=_params(sem),
    )(a, b, *extras, *host.ins)
    return res[0] if len(res) == 1 else res


def _rms_fwd(name, x, g, tm=512):
    t, d = x.shape
    tm = _tile(t, tm)

    def body(x_ref, g_ref, h_ref):
        xf = x_ref[...]
        r = lax.rsqrt(jnp.mean(xf * xf, axis=-1, keepdims=True) + EPS)
        h_ref[...] = (xf * r * g_ref[...]).astype(h_ref.dtype)

    return pl.pallas_call(
        body,
        name=name,
        grid=(t // tm,),
        in_specs=[pl.BlockSpec((tm, d), lambda i: (i, 0)), pl.BlockSpec((1, d), lambda i: (0, 0))],
        out_specs=pl.BlockSpec((tm, d), lambda i: (i, 0)),
        out_shape=jax.ShapeDtypeStruct((t, d), BF16),
        compiler_params=_params(("parallel",)),
    )(x, g.reshape(1, d))


def _rms_bwd(name, dh, x, g, res=None, tm=256):
    t, d = x.shape
    tm = _tile(t, tm)
    has_res = res is not None

    def body(*refs):
        if has_res:
            dh_ref, x_ref, g_ref, res_ref, dx_ref, dxb_ref, gg_ref = refs
        else:
            dh_ref, x_ref, g_ref, dx_ref, dxb_ref, gg_ref = refs
        i = pl.program_id(0)
        xf = x_ref[...]
        r = lax.rsqrt(jnp.mean(xf * xf, axis=-1, keepdims=True) + EPS)
        xh = xf * r
        dhf = dh_ref[...].astype(F32)
        dxh = dhf * g_ref[...]
        dx = r * (dxh - xh * jnp.mean(dxh * xh, axis=-1, keepdims=True))
        if has_res:
            dx = dx + res_ref[...]
        dx_ref[...] = dx
        dxb_ref[...] = dx.astype(BF16)

        @pl.when(i == 0)
        def _():
            gg_ref[...] = jnp.zeros_like(gg_ref)

        gg_ref[...] += jnp.sum(dhf * xh, axis=0, keepdims=True)

    row = pl.BlockSpec((tm, d), lambda i: (i, 0))
    vec = pl.BlockSpec((1, d), lambda i: (0, 0))
    ins = [dh, x, g.reshape(1, d)] + ([res] if has_res else [])
    dx, dxb, gg = pl.pallas_call(
        body,
        name=name,
        grid=(t // tm,),
        in_specs=[row, row, vec] + ([row] if has_res else []),
        out_specs=[row, row, vec],
        out_shape=[jax.ShapeDtypeStruct((t, d), F32), jax.ShapeDtypeStruct((t, d), BF16), jax.ShapeDtypeStruct((1, d), F32)],
        compiler_params=_params(("arbitrary",)),
    )(*ins)
    return dx, dxb, gg.reshape(d)


def _head_rms(xf):
    r = lax.rsqrt(jnp.mean(xf * xf, axis=-1, keepdims=True) + EPS)
    return xf * r, r


def _head_rms_bwd(dy, xn, r, g):
    dxh = dy * g
    dx = r * (dxh - xn * jnp.mean(dxh * xn, axis=-1, keepdims=True))
    return dx, jnp.sum(dy * xn, axis=0, keepdims=True)


def _col_to_row(col):
    n = col.shape[0]
    eye = lax.broadcasted_iota(jnp.int32, (n, n), 0) == lax.broadcasted_iota(jnp.int32, (n, n), 1)
    return jnp.sum(jnp.where(eye, col, 0.0), axis=0, keepdims=True)


def _row_to_col(row):
    n = row.shape[1]
    eye = lax.broadcasted_iota(jnp.int32, (n, n), 0) == lax.broadcasted_iota(jnp.int32, (n, n), 1)
    return jnp.sum(jnp.where(eye, row, 0.0), axis=1, keepdims=True)


def _dproj_args(dproj, n_in):
    if dproj is None:
        return [], [], {}
    return [dproj], [ANY], {n_in: 0}


def _shift_down(u, s, rows):
    return jnp.where(rows >= s, pltpu.roll(u, s, axis=0), 0.0)


def _shift_up(u, s, rows, t):
    return jnp.where(rows < t - s, pltpu.roll(u, t - s, axis=0), 0.0)


def _conv_fwd(proj, off, conv_w, cb):
    t = proj.shape[0]
    c = conv_w.shape[1]
    blk0 = off // (3 * cb)

    def body(p_ref, w_ref, y_ref):
        rows = lax.broadcasted_iota(jnp.int32, (t, cb), 0)
        bg = p_ref[:, 0:cb].astype(F32)
        u = p_ref[:, cb:2 * cb].astype(F32) * p_ref[:, 2 * cb:3 * cb].astype(F32)
        w = w_ref[...]
        conv = w[2:3] * u + w[1:2] * _shift_down(u, 1, rows) + w[0:1] * _shift_down(u, 2, rows)
        y_ref[...] = (bg * conv).astype(y_ref.dtype)

    return pl.pallas_call(
        body,
        name="conv_fwd",
        grid=(c // cb,),
        in_specs=[pl.BlockSpec((t, 3 * cb), lambda j: (0, blk0 + j)), pl.BlockSpec((CONV_TAPS, cb), lambda j: (0, j))],
        out_specs=pl.BlockSpec((t, cb), lambda j: (0, j)),
        out_shape=jax.ShapeDtypeStruct((t, c), BF16),
        compiler_params=_params(("parallel",)),
    )(proj, conv_w)


def _conv_bwd(proj, off, conv_w, dy, cb, dproj, rider=None):
    t = proj.shape[0]
    c = conv_w.shape[1]
    blk0 = off // (3 * cb)
    nj = c // cb
    host = _Host(rider)

    def body(*refs):
        p_ref, w_ref, dy_ref = refs[:3]
        r_ins = refs[4:4 + host.n_in]
        dp_ref, gw_ref = refs[4 + host.n_in:6 + host.n_in]
        r_outs = refs[6 + host.n_in:6 + host.n_in + host.n_out]
        sems = refs[6 + host.n_in + host.n_out:]
        j = pl.program_id(0)

        def compute():
            rows = lax.broadcasted_iota(jnp.int32, (t, cb), 0)
            bg = p_ref[:, 0:cb].astype(F32)
            cg = p_ref[:, cb:2 * cb].astype(F32)
            v = p_ref[:, 2 * cb:3 * cb].astype(F32)
            u = cg * v
            w = w_ref[...]
            u1 = _shift_down(u, 1, rows)
            u2 = _shift_down(u, 2, rows)
            conv = w[2:3] * u + w[1:2] * u1 + w[0:1] * u2
            dyf = dy_ref[...].astype(F32)
            dconv = dyf * bg
            du = w[2:3] * dconv + w[1:2] * _shift_up(dconv, 1, rows, t) + w[0:1] * _shift_up(dconv, 2, rows, t)
            dp_ref[:, 0:cb] = (dyf * conv).astype(dp_ref.dtype)
            dp_ref[:, cb:2 * cb] = (du * v).astype(dp_ref.dtype)
            dp_ref[:, 2 * cb:3 * cb] = (du * cg).astype(dp_ref.dtype)
            gw_ref[0:1, :] = jnp.sum(dconv * u2, axis=0, keepdims=True)
            gw_ref[1:2, :] = jnp.sum(dconv * u1, axis=0, keepdims=True)
            gw_ref[2:3, :] = jnp.sum(dconv * u, axis=0, keepdims=True)

        host.run(j == 0, j == nj - 1, r_ins, r_outs, sems, compute)

    res = pl.pallas_call(
        body,
        name="conv_bwd",
        grid=(nj,),
        in_specs=[
            pl.BlockSpec((t, 3 * cb), lambda j: (0, blk0 + j)),
            pl.BlockSpec((CONV_TAPS, cb), lambda j: (0, j)),
            pl.BlockSpec((t, cb), lambda j: (0, j)),
            ANY,
        ] + host.in_specs,
        out_specs=[pl.BlockSpec((t, 3 * cb), lambda j: (0, blk0 + j)), pl.BlockSpec((CONV_TAPS, cb), lambda j: (0, j))] + host.out_specs,
        out_shape=[jax.ShapeDtypeStruct(dproj.shape, dproj.dtype), jax.ShapeDtypeStruct((CONV_TAPS, c), F32)] + host.out_shapes,
        input_output_aliases={3: 0},
        scratch_shapes=host.scratch,
        compiler_params=_params(("arbitrary",)),
    )(proj, conv_w, dy, dproj, *host.ins)
    return res


def _lane_scan(x, reverse):
    lane = lax.broadcasted_iota(jnp.int32, x.shape, 1)
    s = 1
    while s < LANES:
        if reverse:
            x = x + jnp.where(lane < LANES - s, pltpu.roll(x, LANES - s, axis=1), 0.0)
        else:
            x = x + jnp.where(lane >= s, pltpu.roll(x, s, axis=1), 0.0)
        s *= 2
    return x


def _scan_rows(src_ref, dst_ref, t, reverse, fn=None):
    groups = list(range(t // LANES))
    if reverse:
        groups = groups[::-1]
    carry = None
    for gi in groups:
        sl = slice(gi * LANES, (gi + 1) * LANES)
        blk = src_ref[:, sl]
        if fn is not None:
            blk = fn(blk)
        blk = _lane_scan(blk, reverse)
        if carry is not None:
            blk = blk + carry
        dst_ref[:, sl] = blk
        carry = blk[:, 0:1] if reverse else blk[:, LANES - 1:LANES]


def _forget_fwd(z_row, b_col):
    rows, t = z_row.shape

    def body(z_ref, b_ref, c_ref):
        def logf(z):
            zz = z + b_ref[...]
            return jnp.minimum(zz, 0.0) - jnp.log(1.0 + jnp.exp(-jnp.abs(zz)))

        _scan_rows(z_ref, c_ref, t, False, logf)

    return pl.pallas_call(
        body,
        name="forget_fwd",
        out_shape=jax.ShapeDtypeStruct((rows, t), F32),
        compiler_params=pltpu.CompilerParams(vmem_limit_bytes=VMEM_LIMIT),
    )(z_row, b_col)


def _rows_to_colb(c_row3, tq):
    heads, _, t = c_row3.shape

    def body(r_ref, o_ref):
        o_ref[...] = jnp.broadcast_to(_row_to_col(r_ref[...]), (tq, LANES))

    return pl.pallas_call(
        body,
        name="rows_to_colb",
        grid=(heads, t // tq),
        in_specs=[pl.BlockSpec((None, 1, tq), lambda h, i: (h, 0, i))],
        out_specs=pl.BlockSpec((None, tq, LANES), lambda h, i: (h, i, 0)),
        out_shape=jax.ShapeDtypeStruct((heads, t, LANES), F32),
        compiler_params=_params(("parallel", "parallel")),
    )(c_row3)


def _forget_bwd(z_row, b_col, dc_row):
    rows, t = z_row.shape

    def body(z_ref, b_ref, dc_ref, dz_ref, db_ref, tmp_ref):
        _scan_rows(dc_ref, tmp_ref, t, True)
        zz = z_ref[...] + b_ref[...]
        dz = tmp_ref[...] * (1.0 / (1.0 + jnp.exp(zz)))
        dz_ref[...] = dz.astype(dz_ref.dtype)
        db_ref[...] = jnp.sum(dz, axis=1, keepdims=True)

    return pl.pallas_call(
        body,
        name="forget_bwd",
        out_shape=[jax.ShapeDtypeStruct((rows, t), BF16), jax.ShapeDtypeStruct((rows, 1), F32)],
        scratch_shapes=[pltpu.VMEM((rows, t), F32)],
        compiler_params=pltpu.CompilerParams(vmem_limit_bytes=VMEM_LIMIT),
    )(z_row, b_col, dc_row)


def _fox_fwd(proj, off, gq, gk, c_row3, c_colb, heads, tq, rider=None):
    t = proj.shape[0]
    hd = FOX_HEAD_DIM
    tq = _tile(t, tq)
    nq = t // tq
    blk0 = off // hd
    scale = 1.0 / math.sqrt(hd)
    host = _Host(rider)

    def body(*refs):
        q_ref, k_ref, v_ref, gq_ref, gk_ref, crow_ref, ccol_ref = refs[:7]
        r_ins = refs[7:7 + host.n_in]
        o_ref, lse_ref = refs[7 + host.n_in:9 + host.n_in]
        r_outs = refs[9 + host.n_in:9 + host.n_in + host.n_out]
        khat_ref, v_t_ref = refs[9 + host.n_in + host.n_out:11 + host.n_in + host.n_out]
        sems = refs[11 + host.n_in + host.n_out:]
        h, qi = pl.program_id(0), pl.program_id(1)

        def compute():
            eye = (lax.broadcasted_iota(jnp.int32, (hd, hd), 0) == lax.broadcasted_iota(jnp.int32, (hd, hd), 1)).astype(BF16)

            @pl.when(qi == 0)
            def _():
                kn, _ = _head_rms(k_ref[...].astype(F32))
                khat_ref[...] = (kn * gk_ref[...]).astype(BF16)
                v_t_ref[...] = _dot(eye, v_ref[...], NT).astype(BF16)

            qn, _ = _head_rms(q_ref[...].astype(F32))
            qhat = (qn * (gq_ref[...] * scale)).astype(BF16)
            crow = crow_ref[:, pl.ds(pl.multiple_of(qi * tq, tq), tq)]
            above = lax.broadcasted_iota(jnp.int32, (tq, tq), 1) >= lax.broadcasted_iota(jnp.int32, (tq, tq), 0)

            def tile(j, carry, diagonal):
                m, l, acc_t = carry
                ks = pl.multiple_of(j * tq, tq)
                s_t = _dot(khat_ref[pl.ds(ks, tq), :], qhat, NT) - ccol_ref[pl.ds(ks, tq), 0:1]
                if diagonal:
                    s_t = jnp.where(above, s_t, NEG)
                m_new = jnp.maximum(m, jnp.max(s_t, axis=0, keepdims=True) + crow)
                alpha = jnp.exp(m - m_new)
                p_t = jnp.exp(s_t + (crow - m_new))
                l = alpha * l + jnp.sum(p_t, axis=0, keepdims=True)
                acc_t = alpha * acc_t + _dot(v_t_ref[:, pl.ds(ks, tq)], p_t.astype(BF16), NN)
                return m_new, l, acc_t

            init = (jnp.full((1, tq), NEG, F32), jnp.zeros((1, tq), F32), jnp.zeros((hd, tq), F32))
            carry = lax.fori_loop(0, qi, lambda j, c: tile(j, c, False), init)
            m, l, acc_t = tile(qi, carry, True)
            o_ref[...] = _dot((acc_t / l).astype(BF16), eye, TN).astype(o_ref.dtype)
            lse_ref[...] = m + jnp.log(l)

        first = jnp.logical_and(h == 0, qi == 0)
        last = jnp.logical_and(h == heads - 1, qi == nq - 1)
        host.run(first, last, r_ins, r_outs, sems, compute)

    res = pl.pallas_call(
        body,
        name="fox_fwd",
        grid=(heads, nq),
        in_specs=[
            pl.BlockSpec((tq, hd), lambda h, i: (i, blk0 + 3 * h)),
            pl.BlockSpec((t, hd), lambda h, i: (0, blk0 + 3 * h + 1)),
            pl.BlockSpec((t, hd), lambda h, i: (0, blk0 + 3 * h + 2)),
            pl.BlockSpec((1, hd), lambda h, i: (0, 0)),
            pl.BlockSpec((1, hd), lambda h, i: (0, 0)),
            pl.BlockSpec((None, 1, t), lambda h, i: (h, 0, 0)),
            pl.BlockSpec((None, t, LANES), lambda h, i: (h, 0, 0)),
        ] + host.in_specs,
        out_specs=[pl.BlockSpec((tq, hd), lambda h, i: (i, h)), pl.BlockSpec((None, 1, tq), lambda h, i: (h, 0, i))] + host.out_specs,
        out_shape=[jax.ShapeDtypeStruct((t, heads * hd), BF16), jax.ShapeDtypeStruct((heads, 1, t), F32)] + host.out_shapes,
        scratch_shapes=[pltpu.VMEM((t, hd), BF16), pltpu.VMEM((hd, t), BF16)] + host.scratch,
        compiler_params=_params(("arbitrary", "arbitrary")),
    )(proj, proj, proj, gq.reshape(1, hd), gk.reshape(1, hd), c_row3, c_colb, *host.ins)
    return res


def _fox_bwd(proj, off, o, do, gq, gk, c_row3, c_colb, lse, heads, tq, dproj, rider=None):
    t = proj.shape[0]
    hd = FOX_HEAD_DIM
    tq = _tile(t, tq)
    nb = t // tq
    blk0 = off // hd
    scale = 1.0 / math.sqrt(hd)
    host = _Host(rider)
    n_fixed_in = 11

    def body(*refs):
        q_ref, k_ref, v_ref, o_ref, do_ref, gq_ref, gk_ref, crow_ref, ccol_ref, lse_ref = refs[:10]
        pos = n_fixed_in
        r_ins = refs[pos:pos + host.n_in]; pos += host.n_in
        dp_ref, dc_ref, ggq_ref, ggk_ref = refs[pos:pos + 4]; pos += 4
        r_outs = refs[pos:pos + host.n_out]; pos += host.n_out
        qhat_ref, khat_ref, khat_t_ref, dq_t_ref, dk_ref, dcq_ref, dck_ref, delta_ref = refs[pos:pos + 8]; pos += 8
        sems = refs[pos:]
        h = pl.program_id(0)

        def compute():
            qn, rq = _head_rms(q_ref[...].astype(F32))
            qhat_ref[...] = (qn * (gq_ref[...] * scale)).astype(BF16)
            kn, rk = _head_rms(k_ref[...].astype(F32))
            khat_ref[...] = (kn * gk_ref[...]).astype(BF16)
            eye = (lax.broadcasted_iota(jnp.int32, (hd, hd), 0) == lax.broadcasted_iota(jnp.int32, (hd, hd), 1)).astype(BF16)
            khat_t_ref[...] = _dot(eye, khat_ref[...], NT).astype(BF16)
            delta = jnp.sum(do_ref[...].astype(F32) * o_ref[...].astype(F32), axis=-1, keepdims=True)
            for b in range(nb):
                sl = slice(b * tq, (b + 1) * tq)
                delta_ref[:, sl] = _col_to_row(delta[sl, :])
            dq_t_ref[...] = jnp.zeros_like(dq_t_ref)
            dcq_ref[...] = jnp.zeros_like(dcq_ref)
            above = lax.broadcasted_iota(jnp.int32, (tq, tq), 1) >= lax.broadcasted_iota(jnp.int32, (tq, tq), 0)

            def kv_block(j, _):
                ks = pl.multiple_of(j * tq, tq)
                kh = khat_ref[pl.ds(ks, tq), :]
                kh_t = khat_t_ref[:, pl.ds(ks, tq)]
                vv = v_ref[pl.ds(ks, tq), :]
                ccol = ccol_ref[pl.ds(ks, tq), 0:1]

                def q_block(i, carry, diagonal):
                    dk, dv, dck = carry
                    qs = pl.multiple_of(i * tq, tq)
                    qh = qhat_ref[pl.ds(qs, tq), :]
                    dob = do_ref[pl.ds(qs, tq), :]
                    s_t = _dot(kh, qh, NT) + ((crow_ref[:, pl.ds(qs, tq)] - lse_ref[:, pl.ds(qs, tq)]) - ccol)
                    p_t = jnp.exp(s_t)
                    if diagonal:
                        p_t = jnp.where(above, p_t, 0.0)
                    ds_t = p_t * (_dot(vv, dob, NT) - delta_ref[:, pl.ds(qs, tq)])
                    dsb = ds_t.astype(BF16)
                    dv = dv + _dot(p_t.astype(BF16), dob, NN)
                    dk = dk + _dot(dsb, qh, NN)
                    dq_t_ref[:, pl.ds(qs, tq)] += _dot(kh_t, dsb, NN)
                    dcq_ref[:, pl.ds(qs, tq)] += jnp.sum(ds_t, axis=0, keepdims=True)
                    dck = dck + jnp.sum(ds_t, axis=-1, keepdims=True)
                    return dk, dv, dck

                zero = jnp.zeros((tq, hd), F32)
                carry = q_block(j, (zero, zero, jnp.zeros((tq, 1), F32)), True)
                dk, dv, dck = lax.fori_loop(j + 1, nb, lambda i, c: q_block(i, c, False), carry)
                dk_ref[pl.ds(ks, tq), :] = dk
                dp_ref[pl.ds(ks, tq), 2 * hd:3 * hd] = dv.astype(dp_ref.dtype)
                dck_ref[pl.ds(ks, tq), :] = dck
                return 0

            lax.fori_loop(0, nb, kv_block, 0)

            dq, ggq = _head_rms_bwd(dq_t_ref[...].T * scale, qn, rq, gq_ref[...])
            dk, ggk = _head_rms_bwd(dk_ref[...], kn, rk, gk_ref[...])
            dp_ref[:, 0:hd] = dq.astype(dp_ref.dtype)
            dp_ref[:, hd:2 * hd] = dk.astype(dp_ref.dtype)
            for b in range(nb):
                sl = slice(b * tq, (b + 1) * tq)
                dc_ref[:, sl] = dcq_ref[:, sl] - _col_to_row(dck_ref[sl, :])

            @pl.when(h == 0)
            def _():
                ggq_ref[...] = jnp.zeros_like(ggq_ref)
                ggk_ref[...] = jnp.zeros_like(ggk_ref)

            ggq_ref[...] += ggq
            ggk_ref[...] += ggk

        host.run(h == 0, h == heads - 1, r_ins, r_outs, sems, compute)

    head_in = lambda part: pl.BlockSpec((t, hd), lambda h: (0, blk0 + 3 * h + part))
    vec = pl.BlockSpec((1, hd), lambda h: (0, 0))
    colb = pl.BlockSpec((None, t, LANES), lambda h: (h, 0, 0))
    res = pl.pallas_call(
        body,
        name="fox_bwd",
        grid=(heads,),
        in_specs=[
            head_in(0), head_in(1), head_in(2),
            pl.BlockSpec((t, hd), lambda h: (0, h)),
            pl.BlockSpec((t, hd), lambda h: (0, h)),
            vec, vec,
            pl.BlockSpec((None, 1, t), lambda h: (h, 0, 0)),
            colb,
            pl.BlockSpec((None, 1, t), lambda h: (h, 0, 0)),
            ANY,
        ] + host.in_specs,
        out_specs=[
            pl.BlockSpec((t, 3 * hd), lambda h: (0, blk0 // 3 + h)),
            pl.BlockSpec((None, 1, t), lambda h: (h, 0, 0)),
            vec, vec,
        ] + host.out_specs,
        out_shape=[
            jax.ShapeDtypeStruct(dproj.shape, dproj.dtype),
            jax.ShapeDtypeStruct((heads, 1, t), F32),
            jax.ShapeDtypeStruct((1, hd), F32),
            jax.ShapeDtypeStruct((1, hd), F32),
        ] + host.out_shapes,
        input_output_aliases={10: 0},
        scratch_shapes=[
            pltpu.VMEM((t, hd), BF16), pltpu.VMEM((t, hd), BF16), pltpu.VMEM((hd, t), BF16),
            pltpu.VMEM((hd, t), F32), pltpu.VMEM((t, hd), F32),
            pltpu.VMEM((1, t), F32), pltpu.VMEM((t, 1), F32), pltpu.VMEM((1, t), F32),
        ] + host.scratch,
        compiler_params=_params(("arbitrary",)),
    )(proj, proj, proj, o, do, gq.reshape(1, hd), gk.reshape(1, hd), c_row3, c_colb, lse, dproj, *host.ins)
    return res


def _mem_fwd(proj, off, kv, gq, gk, tq):
    t = proj.shape[0]
    m, width = kv.shape[0], kv.shape[1] // 2
    hd = width // MEM_HEADS
    tq = _tile(t, tq)
    blk0 = off // hd
    scale = 1.0 / math.sqrt(hd)

    def body(q_ref, k_ref, v_ref, gq_ref, gk_ref, o_ref):
        qn, _ = _head_rms(q_ref[...].astype(F32))
        kn, _ = _head_rms(k_ref[...])
        s = _dot((qn * gq_ref[...]).astype(BF16), (kn * gk_ref[...]).astype(BF16), NT) * scale
        p = jnp.exp(s - jnp.max(s, axis=-1, keepdims=True))
        p = p / jnp.sum(p, axis=-1, keepdims=True)
        o_ref[...] = _dot(p.astype(BF16), v_ref[...].astype(BF16), NN).astype(o_ref.dtype)

    vec = pl.BlockSpec((1, hd), lambda h, i: (0, 0))
    return pl.pallas_call(
        body,
        name="mem_fwd",
        grid=(MEM_HEADS, t // tq),
        in_specs=[
            pl.BlockSpec((tq, hd), lambda h, i: (i, blk0 + h)),
            pl.BlockSpec((m, hd), lambda h, i: (0, h)),
            pl.BlockSpec((m, hd), lambda h, i: (0, MEM_HEADS + h)),
            vec, vec,
        ],
        out_specs=pl.BlockSpec((tq, hd), lambda h, i: (i, h)),
        out_shape=jax.ShapeDtypeStruct((t, width), BF16),
        compiler_params=_params(("parallel", "parallel")),
    )(proj, kv, kv, gq.reshape(1, hd), gk.reshape(1, hd))


def _mem_bwd(proj, off, kv, do, gq, gk, tq, dproj, rider=None):
    t = proj.shape[0]
    m, width = kv.shape[0], kv.shape[1] // 2
    hd = width // MEM_HEADS
    tq = _tile(t, tq)
    nq = t // tq
    blk0 = off // hd
    scale = 1.0 / math.sqrt(hd)
    host = _Host(rider)

    def body(*refs):
        q_ref, k_ref, v_ref, do_ref, gq_ref, gk_ref = refs[:6]
        pos = 7
        r_ins = refs[pos:pos + host.n_in]; pos += host.n_in
        dq_ref, dk_ref, dv_ref, ggq_ref, ggk_ref = refs[pos:pos + 5]; pos += 5
        r_outs = refs[pos:pos + host.n_out]; pos += host.n_out
        dkh_ref, dvh_ref = refs[pos:pos + 2]; pos += 2
        sems = refs[pos:]
        h, i = pl.program_id(0), pl.program_id(1)

        def compute():
            qn, rq = _head_rms(q_ref[...].astype(F32))
            kn, rk = _head_rms(k_ref[...])
            qhat = (qn * gq_ref[...]).astype(BF16)
            khat = (kn * gk_ref[...]).astype(BF16)
            vb = v_ref[...].astype(BF16)
            dob = do_ref[...]
            s = _dot(qhat, khat, NT) * scale
            p = jnp.exp(s - jnp.max(s, axis=-1, keepdims=True))
            p = p / jnp.sum(p, axis=-1, keepdims=True)
            dp = _dot(dob, vb, NT)
            ds = p * (dp - jnp.sum(dp * p, axis=-1, keepdims=True))
            dsb = ds.astype(BF16)
            dq, ggq = _head_rms_bwd(_dot(dsb, khat, NN) * scale, qn, rq, gq_ref[...])
            dq_ref[...] = dq.astype(dq_ref.dtype)

            @pl.when(i == 0)
            def _():
                dkh_ref[...] = jnp.zeros_like(dkh_ref)
                dvh_ref[...] = jnp.zeros_like(dvh_ref)

            @pl.when(jnp.logical_and(h == 0, i == 0))
            def _():
                ggq_ref[...] = jnp.zeros_like(ggq_ref)
                ggk_ref[...] = jnp.zeros_like(ggk_ref)

            dkh_ref[...] += _dot(dsb, qhat, TN)
            dvh_ref[...] += _dot(p.astype(BF16), dob, TN)
            ggq_ref[...] += ggq

            @pl.when(i == nq - 1)
            def _():
                dk, ggk = _head_rms_bwd(dkh_ref[...] * scale, kn, rk, gk_ref[...])
                dk_ref[...] = dk.astype(dk_ref.dtype)
                dv_ref[...] = dvh_ref[...].astype(dv_ref.dtype)
                ggk_ref[...] += ggk

        first = jnp.logical_and(h == 0, i == 0)
        last = jnp.logical_and(h == MEM_HEADS - 1, i == nq - 1)
        host.run(first, last, r_ins, r_outs, sems, compute)

    vec = pl.BlockSpec((1, hd), lambda h, i: (0, 0))
    kblk = pl.BlockSpec((m, hd), lambda h, i: (0, h))
    res = pl.pallas_call(
        body,
        name="mem_bwd",
        grid=(MEM_HEADS, nq),
        in_specs=[
            pl.BlockSpec((tq, hd), lambda h, i: (i, blk0 + h)), kblk,
            pl.BlockSpec((m, hd), lambda h, i: (0, MEM_HEADS + h)),
            pl.BlockSpec((tq, hd), lambda h, i: (i, h)), vec, vec, ANY,
        ] + host.in_specs,
        out_specs=[pl.BlockSpec((tq, hd), lambda h, i: (i, blk0 + h)), kblk, kblk, vec, vec] + host.out_specs,
        out_shape=[
            jax.ShapeDtypeStruct(dproj.shape, dproj.dtype),
            jax.ShapeDtypeStruct((m, width), BF16),
            jax.ShapeDtypeStruct((m, width), BF16),
            jax.ShapeDtypeStruct((1, hd), F32),
            jax.ShapeDtypeStruct((1, hd), F32),
        ] + host.out_shapes,
        input_output_aliases={6: 0},
        scratch_shapes=[pltpu.VMEM((m, hd), F32), pltpu.VMEM((m, hd), F32)] + host.scratch,
        compiler_params=_params(("arbitrary", "arbitrary")),
    )(proj, kv, kv, do, gq.reshape(1, hd), gk.reshape(1, hd), dproj, *host.ins)
    dproj, dk, dv, ggq, ggk = res[:5]
    return (dproj, jnp.concatenate([dk, dv], axis=1), ggq.reshape(hd), ggk.reshape(hd), *res[5:])


def _sigmoid(z):
    return 1.0 / (1.0 + jnp.exp(-z))


def _merge_fwd(proj, o3, tm, tc):
    t, d = o3[0].shape
    tm = _tile(t, tm)

    def body(g_ref, oa_ref, ob_ref, oc_ref, out_ref):
        acc = jnp.zeros((tm, tc), F32)
        for s, o_ref in enumerate((oa_ref, ob_ref, oc_ref)):
            acc = acc + _sigmoid(g_ref[:, s * tc:(s + 1) * tc].astype(F32)) * o_ref[...].astype(F32)
        out_ref[...] = acc.astype(out_ref.dtype)

    blk = pl.BlockSpec((tm, tc), lambda i, j: (i, j))
    return pl.pallas_call(
        body,
        name="merge_fwd",
        grid=(t // tm, d // tc),
        in_specs=[pl.BlockSpec((tm, 3 * tc), lambda i, j: (i, j)), blk, blk, blk],
        out_specs=blk,
        out_shape=jax.ShapeDtypeStruct((t, d), BF16),
        compiler_params=_params(("parallel", "parallel")),
    )(proj, *o3)


def _merge_bwd(proj, o3, dm, tm, tc):
    t, d = dm.shape
    tm = _tile(t, tm)

    def body(g_ref, oa_ref, ob_ref, oc_ref, dm_ref, dg_ref, da_ref, db_ref, dc_ref):
        dmf = dm_ref[...].astype(F32)
        for s, (o_ref, do_ref) in enumerate(((oa_ref, da_ref), (ob_ref, db_ref), (oc_ref, dc_ref))):
            g = _sigmoid(g_ref[:, s * tc:(s + 1) * tc].astype(F32))
            do_ref[...] = (dmf * g).astype(do_ref.dtype)
            dg_ref[:, s * tc:(s + 1) * tc] = (dmf * o_ref[...].astype(F32) * g * (1.0 - g)).astype(dg_ref.dtype)

    blk = pl.BlockSpec((tm, tc), lambda i, j: (i, j))
    wide = pl.BlockSpec((tm, 3 * tc), lambda i, j: (i, j))
    return pl.pallas_call(
        body,
        name="merge_bwd",
        grid=(t // tm, d // tc),
        in_specs=[wide, blk, blk, blk, blk],
        out_specs=[wide, blk, blk, blk],
        out_shape=[jax.ShapeDtypeStruct(proj.shape, BF16)] + [jax.ShapeDtypeStruct((t, d), BF16)] * 3,
        compiler_params=_params(("parallel", "parallel")),
    )(proj, *o3, dm)


def _loss(dy, d):
    t = dy.shape[0]
    tm = _tile(t, 512)

    def body(dy_ref, out_ref):
        i = pl.program_id(0)

        @pl.when(i == 0)
        def _():
            out_ref[...] = jnp.zeros_like(out_ref)

        e = dy_ref[...]
        out_ref[...] += jnp.sum(jnp.sum(e * e, axis=0, keepdims=True), axis=1, keepdims=True) * (0.5 * d)

    out = pl.pallas_call(
        body,
        name="loss",
        grid=(t // tm,),
        in_specs=[pl.BlockSpec((tm, dy.shape[1]), lambda i: (i, 0))],
        out_specs=pl.BlockSpec((1, 1), lambda i: (0, 0)),
        out_shape=jax.ShapeDtypeStruct((1, 1), F32),
        compiler_params=_params(("arbitrary",)),
    )(dy)
    return out[0, 0]


def _w_in_chunks(d, tc):
    cw = d // 2
    heads = cw // FOX_HEAD_DIM
    conv0, fox0, f0, mq0, gate0 = 0, 3 * cw, 6 * cw, 6 * cw + heads, 7 * cw + heads
    chunks = [(gate0 + s * d + j * tc, gate0 + s * d + (j + 1) * tc) for j in range(d // tc) for s in range(N_BRANCHES)]
    chunks += [(conv0 + s * cw + j * LANES, conv0 + s * cw + (j + 1) * LANES) for j in range(cw // LANES) for s in range(3)]
    chunks += [(fox0 + s * cw + j * FOX_HEAD_DIM, fox0 + s * cw + (j + 1) * FOX_HEAD_DIM) for j in range(heads) for s in range(3)]
    chunks.append((mq0, mq0 + cw))
    return chunks, (f0, f0 + heads)


def _pack_w_in(w8, d, tc):
    r = w8.shape[1]

    def rows(lo, hi):
        return [w8[b, max(lo, b * r) - b * r:min(hi, (b + 1) * r) - b * r] for b in range(lo // r, (hi - 1) // r + 1)]

    chunks, (f_lo, f_hi) = _w_in_chunks(d, tc)
    w_all = jnp.concatenate([piece for lo, hi in chunks for piece in rows(lo, hi)], axis=0)
    f = jnp.concatenate(rows(f_lo, f_hi), axis=0)
    return w_all, jnp.pad(f, ((0, F_ROWS - (f_hi - f_lo)), (0, 0)))


def _unpack_g_in(g_all, g_f, d, tc, blocks):
    chunks, (f_lo, f_hi) = _w_in_chunks(d, tc)
    pos, sources = 0, [(f_lo, f_hi, g_f, 0)]
    for lo, hi in chunks:
        sources.append((lo, hi, g_all, pos))
        pos += hi - lo
    sources.sort(key=lambda s: s[0])
    r = sources[-1][1] // blocks
    out = []
    for b in range(blocks):
        pieces = [src[p + max(lo, b * r) - lo:p + min(hi, (b + 1) * r) - lo]
                  for lo, hi, src, p in sources if lo < (b + 1) * r and hi > b * r]
        out.append(jnp.concatenate(pieces, axis=0))
    return jnp.stack(out)


def _unblock(w8):
    return w8.transpose(1, 0, 2).reshape(w8.shape[1], -1)


def _tile2(r, cols, tr, tcols):
    if r % 8 == 0:
        return _tile(r, tr), cols
    return r, _tile(cols, tcols)


def _pair_sum(name, g8, got, c):
    _, r, cols = g8.shape
    tr, tcols = _tile2(r, cols, 256, 256)

    def body(c_ref, g_ref, s_ref, o_ref):
        o_ref[...] = (g_ref[...].astype(F32) + s_ref[...].astype(F32)).astype(o_ref.dtype)

    blk = pl.BlockSpec((None, tr, tcols), lambda q, i, j, c_ref: (q, i, j))
    return pl.pallas_call(
        body,
        name=name,
        grid_spec=pltpu.PrefetchScalarGridSpec(
            num_scalar_prefetch=1,
            grid=(N_CHIPS, r // tr, cols // tcols),
            in_specs=[pl.BlockSpec((None, tr, tcols), lambda q, i, j, c_ref: (2 * q + c_ref[0], i, j)), blk],
            out_specs=blk,
        ),
        out_shape=jax.ShapeDtypeStruct((N_CHIPS, r, cols), BF16),
        compiler_params=_params(("parallel", "parallel", "parallel")),
    )(c, g8, got)


def _local_step(x, mem, target, w, small, comm=None):
    t, d = x.shape
    cw = d // 2
    heads = cw // FOX_HEAD_DIM
    tc = min(512, d)
    tq = min(512, t)
    off_conv, off_fox, off_mq = 3 * d, 3 * d + 3 * cw, 3 * d + 6 * cw
    w = dict(w)
    w_all, w_f = _pack_w_in(w["w_in"], d, tc)
    big = dict(tm=1024, tn=512, tk=2048)
    wide_k = dict(tm=512, tn=1024, tk=4096)

    h = _rms_fwd("rms1_fwd", x, small["norm1_g"])
    if comm:
        early = ("w_conv_out", "w_fox_out", "w_mem_out", "w_out", "w_up")
        proj, *got = _matmul("proj", "nt", h, w_all, outs=[BF16], rider=_gather_rider([comm["shards"][n] for n in early]), **big)
        for n, val in zip(early, got):
            w[n] = val.reshape(-1, val.shape[-1]) if n == "w_out" else _unblock(val)
    else:
        proj = _matmul("proj", "nt", h, w_all, outs=[BF16], **big)
    z_row = _matmul("proj_f", "nt", w_f, h, outs=[F32], tm=F_ROWS, tn=512, tk=2048)

    y_conv = _conv_fwd(proj, off_conv, small["conv_w"], LANES)

    b_col = jnp.pad(small["b_f"], (0, F_ROWS - heads)).reshape(F_ROWS, 1)
    c_row3 = _forget_fwd(z_row, b_col)[:heads].reshape(heads, 1, t)
    c_colb = _rows_to_colb(c_row3, tq)
    if comm:
        y_fox, lse, got = _fox_fwd(proj, off_fox, small["fox_q_g"], small["fox_k_g"], c_row3, c_colb, heads, tq,
                                   rider=_gather_rider([comm["shards"]["w_down"]]))
        w["w_down"] = got.reshape(-1, got.shape[-1])
    else:
        y_fox, lse = _fox_fwd(proj, off_fox, small["fox_q_g"], small["fox_k_g"], c_row3, c_colb, heads, tq)

    nm = _rms_fwd("mem_rms_fwd", mem, small["mem_norm_g"])
    kv = _matmul("mem_kv", "nn", nm, w["w_mem_kv"], outs=[F32], tm=256, tn=512, tk=2048)
    y_mem = _mem_fwd(proj, off_mq, kv, small["mem_q_g"], small["mem_k_g"], tq)

    ys = (y_conv, y_fox, y_mem)
    w_outs = (w["w_conv_out"], w["w_fox_out"], w["w_mem_out"])
    o3 = [_matmul(f"branch_out{s}", "nn", ys[s], w_outs[s], outs=[BF16], **big) for s in range(3)]
    merged = _merge_fwd(proj, o3, 512, tc)
    x1 = _matmul("out_proj", "nn", merged, w["w_out"], outs=[F32], extras=[x],
                 epilogue=lambda acc, xr: (acc + xr,), **big)
    h2 = _rms_fwd("rms2_fwd", x1, small["norm2_g"])

    def up_epilogue(acc):
        return acc, jnp.square(jnp.maximum(acc, 0.0))

    up, act = _matmul("mlp_up", "nn", h2, w["w_up"], outs=[BF16, BF16], epilogue=up_epilogue, **big)

    def loss_epilogue(acc, x1r, tr):
        dy = (acc + x1r - tr) * (1.0 / d)
        return dy, dy

    dy, dyb = _matmul("mlp_down", "nn", act, w["w_down"], outs=[F32, BF16], extras=[x1, target],
                      epilogue=loss_epilogue, **big)

    def dup_epilogue(acc, upr):
        return (acc * 2.0 * jnp.maximum(upr.astype(F32), 0.0),)

    def by_owner(g):
        return g.reshape(N_DEV, -1, g.shape[-1])

    g, parts = {}, {}
    g["w_down"] = _matmul("d_w_down", "tn", act, dyb, outs=[BF16], **wide_k)
    if comm:
        dup, got = _matmul("d_act", "nt", dyb, w["w_down"], outs=[BF16], extras=[up], epilogue=dup_epilogue,
                           rider=_pair_rider([by_owner(g["w_down"])]), **big)
        pair = _pair_sum("pair_w_down", by_owner(g["w_down"]), got, comm["c"])
        g["w_up"], parts["w_down"] = _matmul("d_w_up", "tn", h2, dup, outs=[BF16], out_blocks=True,
                                             rider=_chip_rider([pair]), **wide_k)
        dh2, got = _matmul("d_h2", "nt", dup, w["w_up"], outs=[F32], rider=_pair_rider([g["w_up"]]), **big)
        pair_up = _pair_sum("pair_w_up", g["w_up"], got, comm["c"])
    else:
        dup = _matmul("d_act", "nt", dyb, w["w_down"], outs=[BF16], extras=[up], epilogue=dup_epilogue, **big)
        g["w_up"] = _matmul("d_w_up", "tn", h2, dup, outs=[BF16], out_blocks=True, **wide_k)
        dh2 = _matmul("d_h2", "nt", dup, w["w_up"], outs=[F32], **big)
    dx1, dx1b, g_norm2 = _rms_bwd("rms2_bwd", dh2, x1, small["norm2_g"], res=dy)
    loss = _loss(dy, d)

    g["w_out"] = _matmul("d_w_out", "tn", merged, dx1b, outs=[BF16], **wide_k)
    dmerged = _matmul("d_merged", "nt", dx1b, w["w_out"], outs=[BF16], **big)
    dproj, *do3 = _merge_bwd(proj, o3, dmerged, 512, tc)
    names = ("w_conv_out", "w_fox_out", "w_mem_out")
    dys = []
    for s in range(3):
        g[names[s]] = _matmul(f"d_w_branch{s}", "tn", ys[s], do3[s], outs=[BF16], out_blocks=True, **wide_k)
        dys.append(_matmul(f"d_branch{s}", "nt", do3[s], w_outs[s], outs=[BF16], **big))

    dproj, dkv, g_mq, g_mk = _mem_bwd(proj, off_mq, kv, dys[2], small["mem_q_g"], small["mem_k_g"], tq, dproj)
    g["w_mem_kv"] = _matmul("d_w_mem_kv", "tn", nm, dkv, outs=[BF16], **wide_k)
    dnm = _matmul("d_mem_norm", "nt", dkv, w["w_mem_kv"], outs=[F32], tm=256, tn=512, tk=2048)
    _, _, g_mem_norm = _rms_bwd("mem_rms_bwd", dnm, mem, small["mem_norm_g"])

    mid = ("w_out", "w_conv_out", "w_fox_out", "w_mem_out", "w_mem_kv")
    if comm:
        mid8 = [g[n] if n in names else by_owner(g[n]) for n in mid]
        dproj, g_conv_w, *got = _conv_bwd(proj, off_conv, small["conv_w"], dys[0], LANES, dproj, rider=_pair_rider(mid8))
        pairs = [pair_up] + [_pair_sum("pair_" + n, g8, s4, comm["c"]) for n, g8, s4 in zip(mid, mid8, got)]
        dproj, dc, g_fq, g_fk, *got = _fox_bwd(proj, off_fox, y_fox, dys[1], small["fox_q_g"], small["fox_k_g"], c_row3, c_colb,
                                               lse, heads, tq, dproj, rider=_chip_rider(pairs))
        parts.update(zip(("w_up",) + mid, got))
    else:
        dproj, g_conv_w = _conv_bwd(proj, off_conv, small["conv_w"], dys[0], LANES, dproj)
        dproj, dc, g_fq, g_fk = _fox_bwd(proj, off_fox, y_fox, dys[1], small["fox_q_g"], small["fox_k_g"], c_row3, c_colb,
                                         lse, heads, tq, dproj)
    dc_row = jnp.pad(dc.reshape(heads, t), ((0, F_ROWS - heads), (0, 0)))
    dz_row, db = _forget_bwd(z_row, b_col, dc_row)

    g_all = _matmul("d_w_in", "tn", dproj, h, outs=[BF16], j_outer=True, **wide_k)
    g_wf = _matmul("d_w_f", "nn", dz_row, h, outs=[BF16], tm=F_ROWS, tn=512, tk=4096)
    g["w_in"] = _unpack_g_in(g_all, g_wf, d, tc, w["w_in"].shape[0])
    dh = _matmul("d_h_f", "tn", dz_row, w_f, outs=[F32], tm=1024, tn=512, tk=F_ROWS)
    add_prev = lambda acc, prev: (acc + prev,)
    if comm:
        g_in8 = g["w_in"]
        got = _run_rider("pair_exchange_w_in", _pair_rider([g_in8]))[0]
        pair = _pair_sum("pair_w_in", g_in8, got, comm["c"])
        dh, parts["w_in"] = _matmul("d_h", "nn", dproj, w_all, outs=[F32], extras=[dh], epilogue=add_prev,
                                    rider=_chip_rider([pair]), tm=1024, tn=512, tk=3328)
    else:
        dh = _matmul("d_h", "nn", dproj, w_all, outs=[F32], extras=[dh], epilogue=add_prev, tm=1024, tn=512, tk=3328)
    grad_x, _, g_norm1 = _rms_bwd("rms1_bwd", dh, x, small["norm1_g"], res=dx1)

    gs = dict(norm1_g=g_norm1, b_f=db[:heads, 0], conv_w=g_conv_w, fox_q_g=g_fq.reshape(-1), fox_k_g=g_fk.reshape(-1),
              mem_norm_g=g_mem_norm, mem_q_g=g_mq, mem_k_g=g_mk, norm2_g=g_norm2)
    return loss, grad_x, (parts if comm else g), gs


def _adamw_math(w, g, m, v):
    m = ADAM_B1 * m + (1.0 - ADAM_B1) * g
    v = ADAM_B2 * v + (1.0 - ADAM_B2) * jnp.square(g)
    m_hat = m / (1.0 - ADAM_B1 ** ADAM_STEP)
    v_hat = v / (1.0 - ADAM_B2 ** ADAM_STEP)
    delta = -ADAM_LR * (m_hat / (jnp.sqrt(v_hat) + ADAM_EPS) + ADAM_WD * w)
    return delta, m, v


def _adamw(name, parts, w, m, v):
    r, c = w.shape
    tr, tc = _tile2(r, c, 128, 256)
    n_parts = parts.shape[0]

    def body(p_ref, w_ref, m_ref, v_ref, g_ref, d_ref, nm_ref, nv_ref):
        g = p_ref[0].astype(F32)
        for s in range(1, n_parts):
            g = g + p_ref[s].astype(F32)
        delta, nm, nv = _adamw_math(w_ref[...], g, m_ref[...], v_ref[...])
        g_ref[...] = g
        d_ref[...] = delta
        nm_ref[...] = nm
        nv_ref[...] = nv

    blk = pl.BlockSpec((tr, tc), lambda i, j: (i, j))
    return pl.pallas_call(
        body,
        name=name,
        grid=(r // tr, c // tc),
        in_specs=[pl.BlockSpec((n_parts, tr, tc), lambda i, j: (0, i, j)), blk, blk, blk],
        out_specs=[blk] * 4,
        out_shape=[jax.ShapeDtypeStruct((r, c), F32)] * 4,
        compiler_params=_params(("parallel", "parallel")),
    )(parts, w, m, v)


def _sum_parts(name, parts):
    n_parts, r, c = parts.shape

    def body(p_ref, o_ref):
        acc = p_ref[0]
        for s in range(1, n_parts):
            acc = acc + p_ref[s]
        o_ref[...] = acc

    return pl.pallas_call(body, name=name, out_shape=jax.ShapeDtypeStruct((r, c), F32))(parts)


BIG = ("w_in", "w_mem_kv", "w_conv_out", "w_fox_out", "w_mem_out", "w_out", "w_up", "w_down")
COLUMN_SPLIT = ("w_in", "w_conv_out", "w_fox_out", "w_mem_out", "w_up")
SMALL = ("norm1_g", "b_f", "conv_w", "fox_q_g", "fox_k_g", "mem_norm_g", "mem_q_g", "mem_k_g", "norm2_g")
WEIGHTS = ("norm1_g", "w_in", "b_f", "conv_w", "fox_q_g", "fox_k_g", "mem_norm_g", "w_mem_kv", "mem_q_g", "mem_k_g",
           "w_conv_out", "w_fox_out", "w_mem_out", "w_out", "norm2_g", "w_up", "w_down")


def _pack(vectors):
    rows = []
    for vec in vectors:
        n = vec.shape[0]
        rows.append(jnp.pad(vec, (0, -n % LANES)).reshape(-1, LANES))
    out = jnp.concatenate(rows, axis=0)
    return jnp.pad(out, ((0, -out.shape[0] % 8), (0, 0)))


def _unpack(packed, sizes):
    out, row = [], 0
    for n in sizes:
        nr = -(-n // LANES)
        out.append(packed[row:row + nr].reshape(-1)[:n])
        row += nr
    return out


def kernel(x, mem, norm1_g, w_in, b_f, conv_w, fox_q_g, fox_k_g, mem_norm_g, w_mem_kv, mem_q_g, mem_k_g, w_conv_out, w_fox_out, w_mem_out, w_out, norm2_g, w_up, w_down, loss_target, m_norm1_g, m_w_in, m_b_f, m_conv_w, m_fox_q_g, m_fox_k_g, m_mem_norm_g, m_w_mem_kv, m_mem_q_g, m_mem_k_g, m_w_conv_out, m_w_fox_out, m_w_mem_out, m_w_out, m_norm2_g, m_w_up, m_w_down, v_norm1_g, v_w_in, v_b_f, v_conv_w, v_fox_q_g, v_fox_k_g, v_mem_norm_g, v_w_mem_kv, v_mem_q_g, v_mem_k_g, v_w_conv_out, v_w_fox_out, v_w_mem_out, v_w_out, v_norm2_g, v_w_up, v_w_down):
    args = dict(locals())
    wts = {n: args[n] for n in WEIGHTS}
    ms = {n: args["m_" + n] for n in WEIGHTS}
    vs = {n: args["v_" + n] for n in WEIGHTS}
    x_pos, y_pos, c_pos = _position()
    me = _index(x_pos, y_pos, c_pos)

    shards = {n: (wts[n].T if n == "w_in" else wts[n]).astype(BF16) for n in BIG}
    wi, wkv, cw8 = _run_rider("all_gather_first", _gather_rider([shards["w_in"], shards["w_mem_kv"], conv_w]))
    full = {"w_in": wi, "w_mem_kv": wkv.reshape(-1, wkv.shape[-1])}
    small = {n: wts[n] for n in SMALL}
    small["conv_w"] = _unblock(cw8)
    comm = {"shards": shards, "c": c_pos.astype(jnp.int32).reshape(1)}

    loss, grad_x, parts, gs = _local_step(x[0], mem[0], loss_target[0], full, small, comm)

    out_g, out_d, out_m, out_v = {}, {}, {}, {}
    for n in BIG:
        if n == "w_in":
            res = _adamw("adamw_" + n, parts[n], wts[n].T, ms[n].T, vs[n].T)
            out_g[n], out_d[n], out_m[n], out_v[n] = (r.T for r in res)
        else:
            out_g[n], out_d[n], out_m[n], out_v[n] = _adamw("adamw_" + n, parts[n], wts[n], ms[n], vs[n])

    small_sizes = [int(math.prod(gs[n].shape)) for n in SMALL]
    packed = _pack([gs[n].reshape(-1) for n in SMALL])
    gsum = _sum_parts("sum_small", _run_rider("exchange_small", _broadcast_rider([packed]))[0])
    gsmall = dict(zip(SMALL, _unpack(gsum, small_sizes)))
    cols = conv_w.shape[1]
    gsmall["conv_w"] = lax.dynamic_slice(gsmall["conv_w"].reshape(CONV_TAPS, -1), (0, me * cols), (CONV_TAPS, cols)).reshape(-1)
    pg, pw, pm, pv = (_pack([src[n].reshape(-1) for n in SMALL]) for src in (gsmall, wts, ms, vs))
    _, sd, sm, sv = _adamw("adamw_small", pg[None], pw, pm, pv)
    local_sizes = [int(math.prod(wts[n].shape)) for n in SMALL]
    for dst, src in ((out_d, sd), (out_m, sm), (out_v, sv)):
        for n, val in zip(SMALL, _unpack(src, local_sizes)):
            dst[n] = val.reshape(wts[n].shape)
    for n in SMALL:
        out_g[n] = gsmall[n].reshape(wts[n].shape)

    loss = lax.psum(loss, MESH_AXES)
    return (loss, grad_x[None], *[out_g[n] for n in WEIGHTS], *[out_d[n] for n in WEIGHTS],
            *[out_m[n] for n in WEIGHTS], *[out_v[n] for n in WEIGHTS])
```

```python
import math

import jax
import jax.numpy as jnp
from jax import lax
from jax.experimental import pallas as pl
from jax.experimental.pallas import tpu as pltpu

F32 = jnp.float32
BF16 = jnp.bfloat16

EPS = 1e-6
N_DEV = 8
N_CHIPS = 4
FOX_HEAD_DIM = 128
MEM_HEADS = 4
CONV_TAPS = 3
N_BRANCHES = 3
F_ROWS = 16

ADAM_LR = 0.001
ADAM_B1 = 0.9
ADAM_B2 = 0.999
ADAM_EPS = 1e-08
ADAM_WD = 0.01
ADAM_STEP = 10

V7X_VMEM_BYTES = 64 * 1024 * 1024
VMEM_LIMIT = V7X_VMEM_BYTES * 3 // 4
LANES = 128
NEG = -1e30

MESH_AXES = ("x", "y", "c")
MESH = pl.DeviceIdType.MESH
ANY = pl.BlockSpec(memory_space=pl.ANY)

NN = (((1,), (0,)), ((), ()))
NT = (((1,), (1,)), ((), ()))
TN = (((0,), (0,)), ((), ()))


def _params(sem):
    return pltpu.CompilerParams(dimension_semantics=sem, vmem_limit_bytes=VMEM_LIMIT)


def _dot(a, b, dn):
    return lax.dot_general(a, b, dn, preferred_element_type=F32)


def _tile(n, t):
    if n <= t:
        return n
    for cand in range(t - t % LANES, 0, -LANES):
        if n % cand == 0:
            return cand
    raise ValueError((n, t))


class _Rider:
    def __init__(self, ins, out_shapes, sem_shapes, start, finish):
        self.ins, self.out_shapes, self.sem_shapes = list(ins), list(out_shapes), list(sem_shapes)
        self.start, self.finish = start, finish


def _position():
    return lax.axis_index("x"), lax.axis_index("y"), lax.axis_index("c")


def _index(px, py, pc):
    return 4 * px + 2 * py + pc


def _dma_sems(n, per):
    return [pltpu.SemaphoreType.DMA((n, per)), pltpu.SemaphoreType.DMA((n, per)), pltpu.SemaphoreType.DMA((n,))]


def _gather_rider(shards):
    n = len(shards)

    def copies(ins, outs, sems):
        send_sems, recv_sems, local_sems = sems
        x, y, c = _position()
        me, sibling = (x, y, c), (x, y, 1 - c)
        chips = [(1 - x, y), (x, 1 - y), (1 - x, 1 - y)]

        def copy(a, k, block, to, src=None):
            rows = outs[a].at[_index(*block)]
            return pltpu.make_async_remote_copy(
                src_ref=rows if src is None else src, dst_ref=rows,
                send_sem=send_sems.at[a, k], recv_sem=recv_sems.at[a, k], device_id=to, device_id_type=MESH)

        mine = [pltpu.make_async_copy(ins[a], outs[a].at[_index(*me)], local_sems.at[a]) for a in range(n)]
        first = []
        for a in range(n):
            first.append(copy(a, 0, me, sibling, src=ins[a]))
            first += [copy(a, 1 + j, me, (*chip, c), src=ins[a]) for j, chip in enumerate(chips)]
        return copy, mine, first, me, sibling, chips, c

    def start(ins, outs, sems):
        _, mine, first, *_ = copies(ins, outs, sems)
        for cp in mine + first:
            cp.start()

    def finish(ins, outs, sems):
        copy, mine, first, me, sibling, chips, c = copies(ins, outs, sems)
        passed = []
        for a in range(n):
            for j, chip in enumerate(chips):
                copy(a, 1 + j, (*chip, c), me).wait_recv()
                fwd = copy(a, 4 + j, (*chip, c), sibling)
                fwd.start()
                passed.append(fwd)
        for a in range(n):
            copy(a, 0, sibling, me).wait_recv()
            for j, chip in enumerate(chips):
                copy(a, 4 + j, (*chip, 1 - c), me).wait_recv()
        for cp in first + passed:
            cp.wait_send()
        for cp in mine:
            cp.wait()

    out_shapes = [jax.ShapeDtypeStruct((N_DEV,) + s.shape, s.dtype) for s in shards]
    return _Rider(shards, out_shapes, _dma_sems(n, 7), start, finish)


def _pair_rider(grads):
    n = len(grads)

    def copies(ins, outs, sems):
        send_sems, recv_sems, _ = sems
        x, y, c = _position()
        return [pltpu.make_async_remote_copy(
            src_ref=ins[a].at[2 * q + (1 - c)], dst_ref=outs[a].at[q],
            send_sem=send_sems.at[a, q], recv_sem=recv_sems.at[a, q], device_id=(x, y, 1 - c), device_id_type=MESH)
            for a in range(n) for q in range(N_CHIPS)]

    def start(ins, outs, sems):
        for cp in copies(ins, outs, sems):
            cp.start()

    def finish(ins, outs, sems):
        cps = copies(ins, outs, sems)
        for cp in cps:
            cp.wait_recv()
        for cp in cps:
            cp.wait_send()

    out_shapes = [jax.ShapeDtypeStruct((N_CHIPS,) + g.shape[1:], g.dtype) for g in grads]
    return _Rider(grads, out_shapes, _dma_sems(n, N_CHIPS), start, finish)


def _chip_rider(parts):
    n = len(parts)

    def copies(ins, outs, sems):
        send_sems, recv_sems, local_sems = sems
        x, y, c = _position()
        q_me = 2 * x + y
        chips = [(1 - x, y), (x, 1 - y), (1 - x, 1 - y)]
        mine = [pltpu.make_async_copy(ins[a].at[q_me], outs[a].at[q_me], local_sems.at[a]) for a in range(n)]
        sends, arrivals = [], []
        for a in range(n):
            for j, (tx, ty) in enumerate(chips):
                q_t = 2 * tx + ty
                sends.append(pltpu.make_async_remote_copy(
                    src_ref=ins[a].at[q_t], dst_ref=outs[a].at[q_me],
                    send_sem=send_sems.at[a, j], recv_sem=recv_sems.at[a, j], device_id=(tx, ty, c), device_id_type=MESH))
                arrivals.append(pltpu.make_async_remote_copy(
                    src_ref=ins[a].at[q_t], dst_ref=outs[a].at[q_t],
                    send_sem=send_sems.at[a, j], recv_sem=recv_sems.at[a, j], device_id=(tx, ty, c), device_id_type=MESH))
        return mine, sends, arrivals

    def start(ins, outs, sems):
        mine, sends, _ = copies(ins, outs, sems)
        for cp in mine + sends:
            cp.start()

    def finish(ins, outs, sems):
        mine, sends, arrivals = copies(ins, outs, sems)
        for cp in arrivals:
            cp.wait_recv()
        for cp in sends:
            cp.wait_send()
        for cp in mine:
            cp.wait()

    out_shapes = [jax.ShapeDtypeStruct(p.shape, p.dtype) for p in parts]
    return _Rider(parts, out_shapes, _dma_sems(n, 3), start, finish)


def _broadcast_rider(values):
    n = len(values)

    def copies(ins, outs, sems):
        send_sems, recv_sems, local_sems = sems
        x, y, c = _position()
        me = _index(x, y, c)

        def peer(k):
            return (1 - x if k & 4 else x, 1 - y if k & 2 else y, 1 - c if k & 1 else c)

        mine = [pltpu.make_async_copy(ins[a], outs[a].at[me], local_sems.at[a]) for a in range(n)]
        sends, arrivals = [], []
        for a in range(n):
            for k in range(1, N_DEV):
                common = dict(send_sem=send_sems.at[a, k - 1], recv_sem=recv_sems.at[a, k - 1], device_id=peer(k), device_id_type=MESH)
                sends.append(pltpu.make_async_remote_copy(src_ref=ins[a], dst_ref=outs[a].at[me], **common))
                arrivals.append(pltpu.make_async_remote_copy(src_ref=ins[a], dst_ref=outs[a].at[_index(*peer(k))], **common))
        return mine, sends, arrivals

    def start(ins, outs, sems):
        mine, sends, _ = copies(ins, outs, sems)
        for cp in mine + sends:
            cp.start()

    def finish(ins, outs, sems):
        mine, sends, arrivals = copies(ins, outs, sems)
        for cp in arrivals:
            cp.wait_recv()
        for cp in sends:
            cp.wait_send()
        for cp in mine:
            cp.wait()

    out_shapes = [jax.ShapeDtypeStruct((N_DEV,) + v.shape, v.dtype) for v in values]
    return _Rider(values, out_shapes, _dma_sems(n, 7), start, finish)


def _run_rider(name, rider):
    n_in, n_out = len(rider.ins), len(rider.out_shapes)

    def body(*refs):
        ins, outs, sems = refs[:n_in], refs[n_in:n_in + n_out], refs[n_in + n_out:]
        rider.start(ins, outs, sems)
        rider.finish(ins, outs, sems)

    return pl.pallas_call(
        body, name=name, in_specs=[ANY] * n_in, out_specs=[ANY] * n_out, out_shape=rider.out_shapes,
        scratch_shapes=rider.sem_shapes)(*rider.ins)


class _Host:
    def __init__(self, rider):
        self.rider = rider
        self.n_in = len(rider.ins) if rider else 0
        self.n_out = len(rider.out_shapes) if rider else 0
        self.n_sem = len(rider.sem_shapes) if rider else 0
        self.ins = rider.ins if rider else []
        self.in_specs = [ANY] * self.n_in
        self.out_specs = [ANY] * self.n_out
        self.out_shapes = rider.out_shapes if rider else []
        self.scratch = rider.sem_shapes if rider else []

    def run(self, first, last, ins, outs, sems, compute):
        if self.rider is None:
            compute()
            return

        @pl.when(first)
        def _():
            self.rider.start(ins, outs, sems)

        compute()

        @pl.when(last)
        def _():
            self.rider.finish(ins, outs, sems)


def _matmul(name, kind, a, b, *, tm, tn, tk, outs, epilogue=None, extras=(), out_blocks=False, rider=None, j_outer=False):
    if kind == "nn":
        (m, kdim), n = a.shape, b.shape[1]
    elif kind == "nt":
        (m, kdim), n = a.shape, b.shape[0]
    else:
        (kdim, m), n = a.shape, b.shape[1]
    if out_blocks:
        tn = min(tn, n // N_DEV)
    tm, tn, tk = _tile(m, tm), _tile(n, tn), _tile(kdim, tk)
    ni, nj, nk = m // tm, n // tn, kdim // tk

    def spec(shape, fn):
        return pl.BlockSpec(shape, (lambda g0, g1, k: fn(g1, g0, k)) if j_outer else fn)

    a_spec = spec((tk, tm), lambda i, j, k: (k, i)) if kind == "tn" else spec((tm, tk), lambda i, j, k: (i, k))
    b_spec = spec((tn, tk), lambda i, j, k: (j, k)) if kind == "nt" else spec((tk, tn), lambda i, j, k: (k, j))
    dn = {"nn": NN, "nt": NT, "tn": TN}[kind]

    tile_spec = spec((tm, tn), lambda i, j, k: (i, j))
    if out_blocks:
        width = n // N_DEV
        r_out = width // tn
        out_shape = [jax.ShapeDtypeStruct((N_DEV, m, width), dt) for dt in outs]
        out_specs = [spec((None, tm, tn), lambda i, j, k: (j // r_out, i, j % r_out)) for _ in outs]
    else:
        out_shape = [jax.ShapeDtypeStruct((m, n), dt) for dt in outs]
        out_specs = [tile_spec for _ in outs]
    n_ex, n_out = len(extras), len(outs)
    host = _Host(rider)
    n_acc = 1 if nk > 1 else 0

    def body(*refs):
        a_ref, b_ref = refs[0], refs[1]
        pos = 2
        ex_refs = refs[pos:pos + n_ex]; pos += n_ex
        r_ins = refs[pos:pos + host.n_in]; pos += host.n_in
        out_refs = refs[pos:pos + n_out]; pos += n_out
        r_outs = refs[pos:pos + host.n_out]; pos += host.n_out
        acc_ref = refs[pos] if n_acc else None
        sems = refs[pos + n_acc:]
        i, j, k = pl.program_id(1 if j_outer else 0), pl.program_id(0 if j_outer else 1), pl.program_id(2)

        def finish_tile(acc):
            vals = (acc,) if epilogue is None else epilogue(acc, *[e[...] for e in ex_refs])
            for o_ref, v in zip(out_refs, vals):
                o_ref[...] = v.astype(o_ref.dtype)

        def compute():
            part = _dot(a_ref[...], b_ref[...], dn)
            if nk == 1:
                finish_tile(part)
                return

            @pl.when(k == 0)
            def _():
                acc_ref[...] = part

            @pl.when(jnp.logical_and(k > 0, k < nk - 1))
            def _():
                acc_ref[...] += part

            @pl.when(k == nk - 1)
            def _():
                finish_tile(acc_ref[...] + part)

        first = jnp.logical_and(jnp.logical_and(i == 0, j == 0), k == 0)
        last = jnp.logical_and(jnp.logical_and(i == ni - 1, j == nj - 1), k == nk - 1)
        host.run(first, last, r_ins, r_outs, sems, compute)

    sem = ("arbitrary",) * 3 if rider else ("parallel", "parallel", "arbitrary")
    res = pl.pallas_call(
        body,
        name=name,
        grid=(nj, ni, nk) if j_outer else (ni, nj, nk),
        in_specs=[a_spec, b_spec] + [tile_spec for _ in extras] + host.in_specs,
        out_specs=out_specs + host.out_specs,
        out_shape=out_shape + host.out_shapes,
        scratch_shapes=([pltpu.VMEM((tm, tn), F32)] if n_acc else []) + host.scratch,
        compiler_params=_params(sem),
    )(a, b, *extras, *host.ins)
    return res[0] if len(res) == 1 else res


def _rms_fwd(name, x, g, tm=512):
    t, d = x.shape
    tm = _tile(t, tm)

    def body(x_ref, g_ref, h_ref):
        xf = x_ref[...]
        r = lax.rsqrt(jnp.mean(xf * xf, axis=-1, keepdims=True) + EPS)
        h_ref[...] = (xf * r * g_ref[...]).astype(h_ref.dtype)

    return pl.pallas_call(
        body,
        name=name,
        grid=(t // tm,),
        in_specs=[pl.BlockSpec((tm, d), lambda i: (i, 0)), pl.BlockSpec((1, d), lambda i: (0, 0))],
        out_specs=pl.BlockSpec((tm, d), lambda i: (i, 0)),
        out_shape=jax.ShapeDtypeStruct((t, d), BF16),
        compiler_params=_params(("parallel",)),
    )(x, g.reshape(1, d))


def _rms_bwd(name, dh, x, g, res=None, tm=256):
    t, d = x.shape
    tm = _tile(t, tm)
    has_res = res is not None

    def body(*refs):
        if has_res:
            dh_ref, x_ref, g_ref, res_ref, dx_ref, dxb_ref, gg_ref = refs
        else:
            dh_ref, x_ref, g_ref, dx_ref, dxb_ref, gg_ref = refs
        i = pl.program_id(0)
        xf = x_ref[...]
        r = lax.rsqrt(jnp.mean(xf * xf, axis=-1, keepdims=True) + EPS)
        xh = xf * r
        dhf = dh_ref[...].astype(F32)
        dxh = dhf * g_ref[...]
        dx = r * (dxh - xh * jnp.mean(dxh * xh, axis=-1, keepdims=True))
        if has_res:
            dx = dx + res_ref[...]
        dx_ref[...] = dx
        dxb_ref[...] = dx.astype(BF16)

        @pl.when(i == 0)
        def _():
            gg_ref[...] = jnp.zeros_like(gg_ref)

        gg_ref[...] += jnp.sum(dhf * xh, axis=0, keepdims=True)

    row = pl.BlockSpec((tm, d), lambda i: (i, 0))
    vec = pl.BlockSpec((1, d), lambda i: (0, 0))
    ins = [dh, x, g.reshape(1, d)] + ([res] if has_res else [])
    dx, dxb, gg = pl.pallas_call(
        body,
        name=name,
        grid=(t // tm,),
        in_specs=[row, row, vec] + ([row] if has_res else []),
        out_specs=[row, row, vec],
        out_shape=[jax.ShapeDtypeStruct((t, d), F32), jax.ShapeDtypeStruct((t, d), BF16), jax.ShapeDtypeStruct((1, d), F32)],
        compiler_params=_params(("arbitrary",)),
    )(*ins)
    return dx, dxb, gg.reshape(d)


def _head_rms(xf):
    r = lax.rsqrt(jnp.mean(xf * xf, axis=-1, keepdims=True) + EPS)
    return xf * r, r


def _head_rms_bwd(dy, xn, r, g):
    dxh = dy * g
    dx = r * (dxh - xn * jnp.mean(dxh * xn, axis=-1, keepdims=True))
    return dx, jnp.sum(dy * xn, axis=0, keepdims=True)


def _col_to_row(col):
    n = col.shape[0]
    eye = lax.broadcasted_iota(jnp.int32, (n, n), 0) == lax.broadcasted_iota(jnp.int32, (n, n), 1)
    return jnp.sum(jnp.where(eye, col, 0.0), axis=0, keepdims=True)


def _row_to_col(row):
    n = row.shape[1]
    eye = lax.broadcasted_iota(jnp.int32, (n, n), 0) == lax.broadcasted_iota(jnp.int32, (n, n), 1)
    return jnp.sum(jnp.where(eye, row, 0.0), axis=1, keepdims=True)


def _dproj_args(dproj, n_in):
    if dproj is None:
        return [], [], {}
    return [dproj], [ANY], {n_in: 0}


def _shift_down(u, s, rows):
    return jnp.where(rows >= s, pltpu.roll(u, s, axis=0), 0.0)


def _shift_up(u, s, rows, t):
    return jnp.where(rows < t - s, pltpu.roll(u, t - s, axis=0), 0.0)


def _conv_fwd(proj, off, conv_w, cb):
    t = proj.shape[0]
    c = conv_w.shape[1]
    blk0 = off // (3 * cb)

    def body(p_ref, w_ref, y_ref):
        rows = lax.broadcasted_iota(jnp.int32, (t, cb), 0)
        bg = p_ref[:, 0:cb].astype(F32)
        u = p_ref[:, cb:2 * cb].astype(F32) * p_ref[:, 2 * cb:3 * cb].astype(F32)
        w = w_ref[...]
        conv = w[2:3] * u + w[1:2] * _shift_down(u, 1, rows) + w[0:1] * _shift_down(u, 2, rows)
        y_ref[...] = (bg * conv).astype(y_ref.dtype)

    return pl.pallas_call(
        body,
        name="conv_fwd",
        grid=(c // cb,),
        in_specs=[pl.BlockSpec((t, 3 * cb), lambda j: (0, blk0 + j)), pl.BlockSpec((CONV_TAPS, cb), lambda j: (0, j))],
        out_specs=pl.BlockSpec((t, cb), lambda j: (0, j)),
        out_shape=jax.ShapeDtypeStruct((t, c), BF16),
        compiler_params=_params(("parallel",)),
    )(proj, conv_w)


def _conv_bwd(proj, off, conv_w, dy, cb, dproj, rider=None):
    t = proj.shape[0]
    c = conv_w.shape[1]
    blk0 = off // (3 * cb)
    nj = c // cb
    host = _Host(rider)

    def body(*refs):
        p_ref, w_ref, dy_ref = refs[:3]
        r_ins = refs[4:4 + host.n_in]
        dp_ref, gw_ref = refs[4 + host.n_in:6 + host.n_in]
        r_outs = refs[6 + host.n_in:6 + host.n_in + host.n_out]
        sems = refs[6 + host.n_in + host.n_out:]
        j = pl.program_id(0)

        def compute():
            rows = lax.broadcasted_iota(jnp.int32, (t, cb), 0)
            bg = p_ref[:, 0:cb].astype(F32)
            cg = p_ref[:, cb:2 * cb].astype(F32)
            v = p_ref[:, 2 * cb:3 * cb].astype(F32)
            u = cg * v
            w = w_ref[...]
            u1 = _shift_down(u, 1, rows)
            u2 = _shift_down(u, 2, rows)
            conv = w[2:3] * u + w[1:2] * u1 + w[0:1] * u2
            dyf = dy_ref[...].astype(F32)
            dconv = dyf * bg
            du = w[2:3] * dconv + w[1:2] * _shift_up(dconv, 1, rows, t) + w[0:1] * _shift_up(dconv, 2, rows, t)
            dp_ref[:, 0:cb] = (dyf * conv).astype(dp_ref.dtype)
            dp_ref[:, cb:2 * cb] = (du * v).astype(dp_ref.dtype)
            dp_ref[:, 2 * cb:3 * cb] = (du * cg).astype(dp_ref.dtype)
            gw_ref[0:1, :] = jnp.sum(dconv * u2, axis=0, keepdims=True)
            gw_ref[1:2, :] = jnp.sum(dconv * u1, axis=0, keepdims=True)
            gw_ref[2:3, :] = jnp.sum(dconv * u, axis=0, keepdims=True)

        host.run(j == 0, j == nj - 1, r_ins, r_outs, sems, compute)

    res = pl.pallas_call(
        body,
        name="conv_bwd",
        grid=(nj,),
        in_specs=[
            pl.BlockSpec((t, 3 * cb), lambda j: (0, blk0 + j)),
            pl.BlockSpec((CONV_TAPS, cb), lambda j: (0, j)),
            pl.BlockSpec((t, cb), lambda j: (0, j)),
            ANY,
        ] + host.in_specs,
        out_specs=[pl.BlockSpec((t, 3 * cb), lambda j: (0, blk0 + j)), pl.BlockSpec((CONV_TAPS, cb), lambda j: (0, j))] + host.out_specs,
        out_shape=[jax.ShapeDtypeStruct(dproj.shape, dproj.dtype), jax.ShapeDtypeStruct((CONV_TAPS, c), F32)] + host.out_shapes,
        input_output_aliases={3: 0},
        scratch_shapes=host.scratch,
        compiler_params=_params(("arbitrary",)),
    )(proj, conv_w, dy, dproj, *host.ins)
    return res


def _lane_scan(x, reverse):
    lane = lax.broadcasted_iota(jnp.int32, x.shape, 1)
    s = 1
    while s < LANES:
        if reverse:
            x = x + jnp.where(lane < LANES - s, pltpu.roll(x, LANES - s, axis=1), 0.0)
        else:
            x = x + jnp.where(lane >= s, pltpu.roll(x, s, axis=1), 0.0)
        s *= 2
    return x


def _scan_rows(src_ref, dst_ref, t, reverse, fn=None):
    groups = list(range(t // LANES))
    if reverse:
        groups = groups[::-1]
    carry = None
    for gi in groups:
        sl = slice(gi * LANES, (gi + 1) * LANES)
        blk = src_ref[:, sl]
        if fn is not None:
            blk = fn(blk)
        blk = _lane_scan(blk, reverse)
        if carry is not None:
            blk = blk + carry
        dst_ref[:, sl] = blk
        carry = blk[:, 0:1] if reverse else blk[:, LANES - 1:LANES]


def _forget_fwd(z_row, b_col):
    rows, t = z_row.shape

    def body(z_ref, b_ref, c_ref):
        def logf(z):
            zz = z + b_ref[...]
            return jnp.minimum(zz, 0.0) - jnp.log(1.0 + jnp.exp(-jnp.abs(zz)))

        _scan_rows(z_ref, c_ref, t, False, logf)

    return pl.pallas_call(
        body,
        name="forget_fwd",
        out_shape=jax.ShapeDtypeStruct((rows, t), F32),
        compiler_params=pltpu.CompilerParams(vmem_limit_bytes=VMEM_LIMIT),
    )(z_row, b_col)


def _rows_to_colb(c_row3, tq):
    heads, _, t = c_row3.shape

    def body(r_ref, o_ref):
        o_ref[...] = jnp.broadcast_to(_row_to_col(r_ref[...]), (tq, LANES))

    return pl.pallas_call(
        body,
        name="rows_to_colb",
        grid=(heads, t // tq),
        in_specs=[pl.BlockSpec((None, 1, tq), lambda h, i: (h, 0, i))],
        out_specs=pl.BlockSpec((None, tq, LANES), lambda h, i: (h, i, 0)),
        out_shape=jax.ShapeDtypeStruct((heads, t, LANES), F32),
        compiler_params=_params(("parallel", "parallel")),
    )(c_row3)


def _forget_bwd(z_row, b_col, dc_row):
    rows, t = z_row.shape

    def body(z_ref, b_ref, dc_ref, dz_ref, db_ref, tmp_ref):
        _scan_rows(dc_ref, tmp_ref, t, True)
        zz = z_ref[...] + b_ref[...]
        dz = tmp_ref[...] * (1.0 / (1.0 + jnp.exp(zz)))
        dz_ref[...] = dz.astype(dz_ref.dtype)
        db_ref[...] = jnp.sum(dz, axis=1, keepdims=True)

    return pl.pallas_call(
        body,
        name="forget_bwd",
        out_shape=[jax.ShapeDtypeStruct((rows, t), BF16), jax.ShapeDtypeStruct((rows, 1), F32)],
        scratch_shapes=[pltpu.VMEM((rows, t), F32)],
        compiler_params=pltpu.CompilerParams(vmem_limit_bytes=VMEM_LIMIT),
    )(z_row, b_col, dc_row)


def _fox_fwd(proj, off, gq, gk, c_row3, c_colb, heads, tq, rider=None):
    t = proj.shape[0]
    hd = FOX_HEAD_DIM
    tq = _tile(t, tq)
    nq = t // tq
    blk0 = off // hd
    scale = 1.0 / math.sqrt(hd)
    host = _Host(rider)

    def body(*refs):
        q_ref, k_ref, v_ref, gq_ref, gk_ref, crow_ref, ccol_ref = refs[:7]
        r_ins = refs[7:7 + host.n_in]
        o_ref, lse_ref = refs[7 + host.n_in:9 + host.n_in]
        r_outs = refs[9 + host.n_in:9 + host.n_in + host.n_out]
        khat_ref, v_t_ref = refs[9 + host.n_in + host.n_out:11 + host.n_in + host.n_out]
        sems = refs[11 + host.n_in + host.n_out:]
        h, qi = pl.program_id(0), pl.program_id(1)

        def compute():
            eye = (lax.broadcasted_iota(jnp.int32, (hd, hd), 0) == lax.broadcasted_iota(jnp.int32, (hd, hd), 1)).astype(BF16)

            @pl.when(qi == 0)
            def _():
                kn, _ = _head_rms(k_ref[...].astype(F32))
                khat_ref[...] = (kn * gk_ref[...]).astype(BF16)
                v_t_ref[...] = _dot(eye, v_ref[...], NT).astype(BF16)

            qn, _ = _head_rms(q_ref[...].astype(F32))
            qhat = (qn * (gq_ref[...] * scale)).astype(BF16)
            crow = crow_ref[:, pl.ds(pl.multiple_of(qi * tq, tq), tq)]
            above = lax.broadcasted_iota(jnp.int32, (tq, tq), 1) >= lax.broadcasted_iota(jnp.int32, (tq, tq), 0)

            def tile(j, carry, diagonal):
                m, l, acc_t = carry
                ks = pl.multiple_of(j * tq, tq)
                s_t = _dot(khat_ref[pl.ds(ks, tq), :], qhat, NT) - ccol_ref[pl.ds(ks, tq), 0:1]
                if diagonal:
                    s_t = jnp.where(above, s_t, NEG)
                m_new = jnp.maximum(m, jnp.max(s_t, axis=0, keepdims=True) + crow)
                alpha = jnp.exp(m - m_new)
                p_t = jnp.exp(s_t + (crow - m_new))
                l = alpha * l + jnp.sum(p_t, axis=0, keepdims=True)
                acc_t = alpha * acc_t + _dot(v_t_ref[:, pl.ds(ks, tq)], p_t.astype(BF16), NN)
                return m_new, l, acc_t

            init = (jnp.full((1, tq), NEG, F32), jnp.zeros((1, tq), F32), jnp.zeros((hd, tq), F32))
            carry = lax.fori_loop(0, qi, lambda j, c: tile(j, c, False), init)
            m, l, acc_t = tile(qi, carry, True)
            o_ref[...] = _dot((acc_t / l).astype(BF16), eye, TN).astype(o_ref.dtype)
            lse_ref[...] = m + jnp.log(l)

        first = jnp.logical_and(h == 0, qi == 0)
        last = jnp.logical_and(h == heads - 1, qi == nq - 1)
        host.run(first, last, r_ins, r_outs, sems, compute)

    res = pl.pallas_call(
        body,
        name="fox_fwd",
        grid=(heads, nq),
        in_specs=[
            pl.BlockSpec((tq, hd), lambda h, i: (i, blk0 + 3 * h)),
            pl.BlockSpec((t, hd), lambda h, i: (0, blk0 + 3 * h + 1)),
            pl.BlockSpec((t, hd), lambda h, i: (0, blk0 + 3 * h + 2)),
            pl.BlockSpec((1, hd), lambda h, i: (0, 0)),
            pl.BlockSpec((1, hd), lambda h, i: (0, 0)),
            pl.BlockSpec((None, 1, t), lambda h, i: (h, 0, 0)),
            pl.BlockSpec((None, t, LANES), lambda h, i: (h, 0, 0)),
        ] + host.in_specs,
        out_specs=[pl.BlockSpec((tq, hd), lambda h, i: (i, h)), pl.BlockSpec((None, 1, tq), lambda h, i: (h, 0, i))] + host.out_specs,
        out_shape=[jax.ShapeDtypeStruct((t, heads * hd), BF16), jax.ShapeDtypeStruct((heads, 1, t), F32)] + host.out_shapes,
        scratch_shapes=[pltpu.VMEM((t, hd), BF16), pltpu.VMEM((hd, t), BF16)] + host.scratch,
        compiler_params=_params(("arbitrary", "arbitrary")),
    )(proj, proj, proj, gq.reshape(1, hd), gk.reshape(1, hd), c_row3, c_colb, *host.ins)
    return res


def _fox_bwd(proj, off, o, do, gq, gk, c_row3, c_colb, lse, heads, tq, dproj, rider=None):
    t = proj.shape[0]
    hd = FOX_HEAD_DIM
    tq = _tile(t, tq)
    nb = t // tq
    blk0 = off // hd
    scale = 1.0 / math.sqrt(hd)
    host = _Host(rider)
    n_fixed_in = 11

    def body(*refs):
        q_ref, k_ref, v_ref, o_ref, do_ref, gq_ref, gk_ref, crow_ref, ccol_ref, lse_ref = refs[:10]
        pos = n_fixed_in
        r_ins = refs[pos:pos + host.n_in]; pos += host.n_in
        dp_ref, dc_ref, ggq_ref, ggk_ref = refs[pos:pos + 4]; pos += 4
        r_outs = refs[pos:pos + host.n_out]; pos += host.n_out
        qhat_ref, khat_ref, khat_t_ref, dq_t_ref, dk_ref, dcq_ref, dck_ref, delta_ref = refs[pos:pos + 8]; pos += 8
        sems = refs[pos:]
        h = pl.program_id(0)

        def compute():
            qn, rq = _head_rms(q_ref[...].astype(F32))
            qhat_ref[...] = (qn * (gq_ref[...] * scale)).astype(BF16)
            kn, rk = _head_rms(k_ref[...].astype(F32))
            khat_ref[...] = (kn * gk_ref[...]).astype(BF16)
            eye = (lax.broadcasted_iota(jnp.int32, (hd, hd), 0) == lax.broadcasted_iota(jnp.int32, (hd, hd), 1)).astype(BF16)
            khat_t_ref[...] = _dot(eye, khat_ref[...], NT).astype(BF16)
            delta = jnp.sum(do_ref[...].astype(F32) * o_ref[...].astype(F32), axis=-1, keepdims=True)
            for b in range(nb):
                sl = slice(b * tq, (b + 1) * tq)
                delta_ref[:, sl] = _col_to_row(delta[sl, :])
            dq_t_ref[...] = jnp.zeros_like(dq_t_ref)
            dcq_ref[...] = jnp.zeros_like(dcq_ref)
            above = lax.broadcasted_iota(jnp.int32, (tq, tq), 1) >= lax.broadcasted_iota(jnp.int32, (tq, tq), 0)

            def kv_block(j, _):
                ks = pl.multiple_of(j * tq, tq)
                kh = khat_ref[pl.ds(ks, tq), :]
                kh_t = khat_t_ref[:, pl.ds(ks, tq)]
                vv = v_ref[pl.ds(ks, tq), :]
                ccol = ccol_ref[pl.ds(ks, tq), 0:1]

                def q_block(i, carry, diagonal):
                    dk, dv, dck = carry
                    qs = pl.multiple_of(i * tq, tq)
                    qh = qhat_ref[pl.ds(qs, tq), :]
                    dob = do_ref[pl.ds(qs, tq), :]
                    s_t = _dot(kh, qh, NT) + ((crow_ref[:, pl.ds(qs, tq)] - lse_ref[:, pl.ds(qs, tq)]) - ccol)
                    p_t = jnp.exp(s_t)
                    if diagonal:
                        p_t = jnp.where(above, p_t, 0.0)
                    ds_t = p_t * (_dot(vv, dob, NT) - delta_ref[:, pl.ds(qs, tq)])
                    dsb = ds_t.astype(BF16)
                    dv = dv + _dot(p_t.astype(BF16), dob, NN)
                    dk = dk + _dot(dsb, qh, NN)
                    dq_t_ref[:, pl.ds(qs, tq)] += _dot(kh_t, dsb, NN)
                    dcq_ref[:, pl.ds(qs, tq)] += jnp.sum(ds_t, axis=0, keepdims=True)
                    dck = dck + jnp.sum(ds_t, axis=-1, keepdims=True)
                    return dk, dv, dck

                zero = jnp.zeros((tq, hd), F32)
                carry = q_block(j, (zero, zero, jnp.zeros((tq, 1), F32)), True)
                dk, dv, dck = lax.fori_loop(j + 1, nb, lambda i, c: q_block(i, c, False), carry)
                dk_ref[pl.ds(ks, tq), :] = dk
                dp_ref[pl.ds(ks, tq), 2 * hd:3 * hd] = dv.astype(dp_ref.dtype)
                dck_ref[pl.ds(ks, tq), :] = dck
                return 0

            lax.fori_loop(0, nb, kv_block, 0)

            dq, ggq = _head_rms_bwd(dq_t_ref[...].T * scale, qn, rq, gq_ref[...])
            dk, ggk = _head_rms_bwd(dk_ref[...], kn, rk, gk_ref[...])
            dp_ref[:, 0:hd] = dq.astype(dp_ref.dtype)
            dp_ref[:, hd:2 * hd] = dk.astype(dp_ref.dtype)
            for b in range(nb):
                sl = slice(b * tq, (b + 1) * tq)
                dc_ref[:, sl] = dcq_ref[:, sl] - _col_to_row(dck_ref[sl, :])

            @pl.when(h == 0)
            def _():
                ggq_ref[...] = jnp.zeros_like(ggq_ref)
                ggk_ref[...] = jnp.zeros_like(ggk_ref)

            ggq_ref[...] += ggq
            ggk_ref[...] += ggk

        host.run(h == 0, h == heads - 1, r_ins, r_outs, sems, compute)

    head_in = lambda part: pl.BlockSpec((t, hd), lambda h: (0, blk0 + 3 * h + part))
    vec = pl.BlockSpec((1, hd), lambda h: (0, 0))
    colb = pl.BlockSpec((None, t, LANES), lambda h: (h, 0, 0))
    res = pl.pallas_call(
        body,
        name="fox_bwd",
        grid=(heads,),
        in_specs=[
            head_in(0), head_in(1), head_in(2),
            pl.BlockSpec((t, hd), lambda h: (0, h)),
            pl.BlockSpec((t, hd), lambda h: (0, h)),
            vec, vec,
            pl.BlockSpec((None, 1, t), lambda h: (h, 0, 0)),
            colb,
            pl.BlockSpec((None, 1, t), lambda h: (h, 0, 0)),
            ANY,
        ] + host.in_specs,
        out_specs=[
            pl.BlockSpec((t, 3 * hd), lambda h: (0, blk0 // 3 + h)),
            pl.BlockSpec((None, 1, t), lambda h: (h, 0, 0)),
            vec, vec,
        ] + host.out_specs,
        out_shape=[
            jax.ShapeDtypeStruct(dproj.shape, dproj.dtype),
            jax.ShapeDtypeStruct((heads, 1, t), F32),
            jax.ShapeDtypeStruct((1, hd), F32),
            jax.ShapeDtypeStruct((1, hd), F32),
        ] + host.out_shapes,
        input_output_aliases={10: 0},
        scratch_shapes=[
            pltpu.VMEM((t, hd), BF16), pltpu.VMEM((t, hd), BF16), pltpu.VMEM((hd, t), BF16),
            pltpu.VMEM((hd, t), F32), pltpu.VMEM((t, hd), F32),
            pltpu.VMEM((1, t), F32), pltpu.VMEM((t, 1), F32), pltpu.VMEM((1, t), F32),
        ] + host.scratch,
        compiler_params=_params(("arbitrary",)),
    )(proj, proj, proj, o, do, gq.reshape(1, hd), gk.reshape(1, hd), c_row3, c_colb, lse, dproj, *host.ins)
    return res


def _mem_fwd(proj, off, kv, gq, gk, tq):
    t = proj.shape[0]
    m, width = kv.shape[0], kv.shape[1] // 2
    hd = width // MEM_HEADS
    tq = _tile(t, tq)
    blk0 = off // hd
    scale = 1.0 / math.sqrt(hd)

    def body(q_ref, k_ref, v_ref, gq_ref, gk_ref, o_ref):
        qn, _ = _head_rms(q_ref[...].astype(F32))
        kn, _ = _head_rms(k_ref[...])
        s = _dot((qn * gq_ref[...]).astype(BF16), (kn * gk_ref[...]).astype(BF16), NT) * scale
        p = jnp.exp(s - jnp.max(s, axis=-1, keepdims=True))
        p = p / jnp.sum(p, axis=-1, keepdims=True)
        o_ref[...] = _dot(p.astype(BF16), v_ref[...].astype(BF16), NN).astype(o_ref.dtype)

    vec = pl.BlockSpec((1, hd), lambda h, i: (0, 0))
    return pl.pallas_call(
        body,
        name="mem_fwd",
        grid=(MEM_HEADS, t // tq),
        in_specs=[
            pl.BlockSpec((tq, hd), lambda h, i: (i, blk0 + h)),
            pl.BlockSpec((m, hd), lambda h, i: (0, h)),
            pl.BlockSpec((m, hd), lambda h, i: (0, MEM_HEADS + h)),
            vec, vec,
        ],
        out_specs=pl.BlockSpec((tq, hd), lambda h, i: (i, h)),
        out_shape=jax.ShapeDtypeStruct((t, width), BF16),
        compiler_params=_params(("parallel", "parallel")),
    )(proj, kv, kv, gq.reshape(1, hd), gk.reshape(1, hd))


def _mem_bwd(proj, off, kv, do, gq, gk, tq, dproj, rider=None):
    t = proj.shape[0]
    m, width = kv.shape[0], kv.shape[1] // 2
    hd = width // MEM_HEADS
    tq = _tile(t, tq)
    nq = t // tq
    blk0 = off // hd
    scale = 1.0 / math.sqrt(hd)
    host = _Host(rider)

    def body(*refs):
        q_ref, k_ref, v_ref, do_ref, gq_ref, gk_ref = refs[:6]
        pos = 7
        r_ins = refs[pos:pos + host.n_in]; pos += host.n_in
        dq_ref, dk_ref, dv_ref, ggq_ref, ggk_ref = refs[pos:pos + 5]; pos += 5
        r_outs = refs[pos:pos + host.n_out]; pos += host.n_out
        dkh_ref, dvh_ref = refs[pos:pos + 2]; pos += 2
        sems = refs[pos:]
        h, i = pl.program_id(0), pl.program_id(1)

        def compute():
            qn, rq = _head_rms(q_ref[...].astype(F32))
            kn, rk = _head_rms(k_ref[...])
            qhat = (qn * gq_ref[...]).astype(BF16)
            khat = (kn * gk_ref[...]).astype(BF16)
            vb = v_ref[...].astype(BF16)
            dob = do_ref[...]
            s = _dot(qhat, khat, NT) * scale
            p = jnp.exp(s - jnp.max(s, axis=-1, keepdims=True))
            p = p / jnp.sum(p, axis=-1, keepdims=True)
            dp = _dot(dob, vb, NT)
            ds = p * (dp - jnp.sum(dp * p, axis=-1, keepdims=True))
            dsb = ds.astype(BF16)
            dq, ggq = _head_rms_bwd(_dot(dsb, khat, NN) * scale, qn, rq, gq_ref[...])
            dq_ref[...] = dq.astype(dq_ref.dtype)

            @pl.when(i == 0)
            def _():
                dkh_ref[...] = jnp.zeros_like(dkh_ref)
                dvh_ref[...] = jnp.zeros_like(dvh_ref)

            @pl.when(jnp.logical_and(h == 0, i == 0))
            def _():
                ggq_ref[...] = jnp.zeros_like(ggq_ref)
                ggk_ref[...] = jnp.zeros_like(ggk_ref)

            dkh_ref[...] += _dot(dsb, qhat, TN)
            dvh_ref[...] += _dot(p.astype(BF16), dob, TN)
            ggq_ref[...] += ggq

            @pl.when(i == nq - 1)
            def _():
                dk, ggk = _head_rms_bwd(dkh_ref[...] * scale, kn, rk, gk_ref[...])
                dk_ref[...] = dk.astype(dk_ref.dtype)
                dv_ref[...] = dvh_ref[...].astype(dv_ref.dtype)
                ggk_ref[...] += ggk

        first = jnp.logical_and(h == 0, i == 0)
        last = jnp.logical_and(h == MEM_HEADS - 1, i == nq - 1)
        host.run(first, last, r_ins, r_outs, sems, compute)

    vec = pl.BlockSpec((1, hd), lambda h, i: (0, 0))
    kblk = pl.BlockSpec((m, hd), lambda h, i: (0, h))
    res = pl.pallas_call(
        body,
        name="mem_bwd",
        grid=(MEM_HEADS, nq),
        in_specs=[
            pl.BlockSpec((tq, hd), lambda h, i: (i, blk0 + h)), kblk,
            pl.BlockSpec((m, hd), lambda h, i: (0, MEM_HEADS + h)),
            pl.BlockSpec((tq, hd), lambda h, i: (i, h)), vec, vec, ANY,
        ] + host.in_specs,
        out_specs=[pl.BlockSpec((tq, hd), lambda h, i: (i, blk0 + h)), kblk, kblk, vec, vec] + host.out_specs,
        out_shape=[
            jax.ShapeDtypeStruct(dproj.shape, dproj.dtype),
            jax.ShapeDtypeStruct((m, width), BF16),
            jax.ShapeDtypeStruct((m, width), BF16),
            jax.ShapeDtypeStruct((1, hd), F32),
            jax.ShapeDtypeStruct((1, hd), F32),
        ] + host.out_shapes,
        input_output_aliases={6: 0},
        scratch_shapes=[pltpu.VMEM((m, hd), F32), pltpu.VMEM((m, hd), F32)] + host.scratch,
        compiler_params=_params(("arbitrary", "arbitrary")),
    )(proj, kv, kv, do, gq.reshape(1, hd), gk.reshape(1, hd), dproj, *host.ins)
    dproj, dk, dv, ggq, ggk = res[:5]
    return (dproj, jnp.concatenate([dk, dv], axis=1), ggq.reshape(hd), ggk.reshape(hd), *res[5:])


def _sigmoid(z):
    return 1.0 / (1.0 + jnp.exp(-z))


def _merge_fwd(proj, o3, tm, tc):
    t, d = o3[0].shape
    tm = _tile(t, tm)

    def body(g_ref, oa_ref, ob_ref, oc_ref, out_ref):
        acc = jnp.zeros((tm, tc), F32)
        for s, o_ref in enumerate((oa_ref, ob_ref, oc_ref)):
            acc = acc + _sigmoid(g_ref[:, s * tc:(s + 1) * tc].astype(F32)) * o_ref[...].astype(F32)
        out_ref[...] = acc.astype(out_ref.dtype)

    blk = pl.BlockSpec((tm, tc), lambda i, j: (i, j))
    return pl.pallas_call(
        body,
        name="merge_fwd",
        grid=(t // tm, d // tc),
        in_specs=[pl.BlockSpec((tm, 3 * tc), lambda i, j: (i, j)), blk, blk, blk],
        out_specs=blk,
        out_shape=jax.ShapeDtypeStruct((t, d), BF16),
        compiler_params=_params(("parallel", "parallel")),
    )(proj, *o3)


def _merge_bwd(proj, o3, dm, tm, tc):
    t, d = dm.shape
    tm = _tile(t, tm)

    def body(g_ref, oa_ref, ob_ref, oc_ref, dm_ref, dg_ref, da_ref, db_ref, dc_ref):
        dmf = dm_ref[...].astype(F32)
        for s, (o_ref, do_ref) in enumerate(((oa_ref, da_ref), (ob_ref, db_ref), (oc_ref, dc_ref))):
            g = _sigmoid(g_ref[:, s * tc:(s + 1) * tc].astype(F32))
            do_ref[...] = (dmf * g).astype(do_ref.dtype)
            dg_ref[:, s * tc:(s + 1) * tc] = (dmf * o_ref[...].astype(F32) * g * (1.0 - g)).astype(dg_ref.dtype)

    blk = pl.BlockSpec((tm, tc), lambda i, j: (i, j))
    wide = pl.BlockSpec((tm, 3 * tc), lambda i, j: (i, j))
    return pl.pallas_call(
        body,
        name="merge_bwd",
        grid=(t // tm, d // tc),
        in_specs=[wide, blk, blk, blk, blk],
        out_specs=[wide, blk, blk, blk],
        out_shape=[jax.ShapeDtypeStruct(proj.shape, BF16)] + [jax.ShapeDtypeStruct((t, d), BF16)] * 3,
        compiler_params=_params(("parallel", "parallel")),
    )(proj, *o3, dm)


def _loss(dy, d):
    t = dy.shape[0]
    tm = _tile(t, 512)

    def body(dy_ref, out_ref):
        i = pl.program_id(0)

        @pl.when(i == 0)
        def _():
            out_ref[...] = jnp.zeros_like(out_ref)

        e = dy_ref[...]
        out_ref[...] += jnp.sum(jnp.sum(e * e, axis=0, keepdims=True), axis=1, keepdims=True) * (0.5 * d)

    out = pl.pallas_call(
        body,
        name="loss",
        grid=(t // tm,),
        in_specs=[pl.BlockSpec((tm, dy.shape[1]), lambda i: (i, 0))],
        out_specs=pl.BlockSpec((1, 1), lambda i: (0, 0)),
        out_shape=jax.ShapeDtypeStruct((1, 1), F32),
        compiler_params=_params(("arbitrary",)),
    )(dy)
    return out[0, 0]


def _w_in_chunks(d, tc):
    cw = d // 2
    heads = cw // FOX_HEAD_DIM
    conv0, fox0, f0, mq0, gate0 = 0, 3 * cw, 6 * cw, 6 * cw + heads, 7 * cw + heads
    chunks = [(gate0 + s * d + j * tc, gate0 + s * d + (j + 1) * tc) for j in range(d // tc) for s in range(N_BRANCHES)]
    chunks += [(conv0 + s * cw + j * LANES, conv0 + s * cw + (j + 1) * LANES) for j in range(cw // LANES) for s in range(3)]
    chunks += [(fox0 + s * cw + j * FOX_HEAD_DIM, fox0 + s * cw + (j + 1) * FOX_HEAD_DIM) for j in range(heads) for s in range(3)]
    chunks.append((mq0, mq0 + cw))
    return chunks, (f0, f0 + heads)


def _row_tiles(a):
    return a.reshape(a.shape[:-1] + (a.shape[-1] // LANES, LANES))


def _pack_w_in(w8, d, tc):
    r = w8.shape[1]
    k = w8.shape[2] * w8.shape[3]

    def rows(lo, hi):
        return [w8[b, max(lo, b * r) - b * r:min(hi, (b + 1) * r) - b * r] for b in range(lo // r, (hi - 1) // r + 1)]

    chunks, (f_lo, f_hi) = _w_in_chunks(d, tc)
    w_all = jnp.concatenate([piece for lo, hi in chunks for piece in rows(lo, hi)], axis=0).reshape(-1, k)
    f = jnp.concatenate(rows(f_lo, f_hi), axis=0).reshape(-1, k)
    return w_all, jnp.pad(f, ((0, F_ROWS - (f_hi - f_lo)), (0, 0)))


def _unpack_g_in(g_all, g_f, d, tc, blocks):
    g_all, g_f = _row_tiles(g_all), _row_tiles(g_f)
    chunks, (f_lo, f_hi) = _w_in_chunks(d, tc)
    pos, sources = 0, [(f_lo, f_hi, g_f, 0)]
    for lo, hi in chunks:
        sources.append((lo, hi, g_all, pos))
        pos += hi - lo
    sources.sort(key=lambda s: s[0])
    r = sources[-1][1] // blocks
    out = []
    for b in range(blocks):
        pieces = [src[p + max(lo, b * r) - lo:p + min(hi, (b + 1) * r) - lo]
                  for lo, hi, src, p in sources if lo < (b + 1) * r and hi > b * r]
        out.append(jnp.concatenate(pieces, axis=0))
    return jnp.stack(out)


def _unblock(w8):
    return w8.transpose(1, 0, 2).reshape(w8.shape[1], -1)


def _tile2(r, cols, tr, tcols):
    if r % 8 == 0:
        return _tile(r, tr), cols
    return r, _tile(cols, tcols)


def _pair_sum(name, g8, got, c):
    def body(c_ref, g_ref, s_ref, o_ref):
        o_ref[...] = (g_ref[...].astype(F32) + s_ref[...].astype(F32)).astype(o_ref.dtype)

    if g8.ndim == 4:
        _, r, k1, k2 = g8.shape
        tr = max(cand for cand in range(1, 385) if r % cand == 0)
        grid = (N_CHIPS, r // tr)
        shape = (None, tr, k1, k2)
        own = pl.BlockSpec(shape, lambda q, i, c_ref: (2 * q + c_ref[0], i, 0, 0))
        blk = pl.BlockSpec(shape, lambda q, i, c_ref: (q, i, 0, 0))
    else:
        _, r, cols = g8.shape
        tr, tcols = _tile2(r, cols, 256, 256)
        grid = (N_CHIPS, r // tr, cols // tcols)
        own = pl.BlockSpec((None, tr, tcols), lambda q, i, j, c_ref: (2 * q + c_ref[0], i, j))
        blk = pl.BlockSpec((None, tr, tcols), lambda q, i, j, c_ref: (q, i, j))
    return pl.pallas_call(
        body,
        name=name,
        grid_spec=pltpu.PrefetchScalarGridSpec(num_scalar_prefetch=1, grid=grid, in_specs=[own, blk], out_specs=blk),
        out_shape=jax.ShapeDtypeStruct((N_CHIPS,) + g8.shape[1:], BF16),
        compiler_params=_params(("parallel",) * len(grid)),
    )(c, g8, got)


def _local_step(x, mem, target, w, small, comm=None):
    t, d = x.shape
    cw = d // 2
    heads = cw // FOX_HEAD_DIM
    tc = min(512, d)
    tq = min(512, t)
    off_conv, off_fox, off_mq = 3 * d, 3 * d + 3 * cw, 3 * d + 6 * cw
    w = dict(w)
    w_all, w_f = _pack_w_in(w["w_in"], d, tc)
    big = dict(tm=1024, tn=512, tk=2048)
    wide_k = dict(tm=512, tn=1024, tk=4096)

    h = _rms_fwd("rms1_fwd", x, small["norm1_g"])
    if comm:
        early = ("w_conv_out", "w_fox_out", "w_mem_out", "w_out", "w_up")
        proj, *got = _matmul("proj", "nt", h, w_all, outs=[BF16], rider=_gather_rider([comm["shards"][n] for n in early]), **big)
        for n, val in zip(early, got):
            w[n] = val.reshape(-1, val.shape[-1]) if n == "w_out" else _unblock(val)
    else:
        proj = _matmul("proj", "nt", h, w_all, outs=[BF16], **big)
    z_row = _matmul("proj_f", "nt", w_f, h, outs=[F32], tm=F_ROWS, tn=512, tk=2048)

    y_conv = _conv_fwd(proj, off_conv, small["conv_w"], LANES)

    b_col = jnp.pad(small["b_f"], (0, F_ROWS - heads)).reshape(F_ROWS, 1)
    c_row3 = _forget_fwd(z_row, b_col)[:heads].reshape(heads, 1, t)
    c_colb = _rows_to_colb(c_row3, tq)
    if comm:
        y_fox, lse, got = _fox_fwd(proj, off_fox, small["fox_q_g"], small["fox_k_g"], c_row3, c_colb, heads, tq,
                                   rider=_gather_rider([comm["shards"]["w_down"]]))
        w["w_down"] = got.reshape(-1, got.shape[-1])
    else:
        y_fox, lse = _fox_fwd(proj, off_fox, small["fox_q_g"], small["fox_k_g"], c_row3, c_colb, heads, tq)

    nm = _rms_fwd("mem_rms_fwd", mem, small["mem_norm_g"])
    kv = _matmul("mem_kv", "nn", nm, w["w_mem_kv"], outs=[F32], tm=256, tn=512, tk=2048)
    y_mem = _mem_fwd(proj, off_mq, kv, small["mem_q_g"], small["mem_k_g"], tq)

    ys = (y_conv, y_fox, y_mem)
    w_outs = (w["w_conv_out"], w["w_fox_out"], w["w_mem_out"])
    o3 = [_matmul(f"branch_out{s}", "nn", ys[s], w_outs[s], outs=[BF16], **big) for s in range(3)]
    merged = _merge_fwd(proj, o3, 512, tc)
    x1 = _matmul("out_proj", "nn", merged, w["w_out"], outs=[F32], extras=[x],
                 epilogue=lambda acc, xr: (acc + xr,), **big)
    h2 = _rms_fwd("rms2_fwd", x1, small["norm2_g"])

    def up_epilogue(acc):
        return acc, jnp.square(jnp.maximum(acc, 0.0))

    up, act = _matmul("mlp_up", "nn", h2, w["w_up"], outs=[BF16, BF16], epilogue=up_epilogue, **big)

    def loss_epilogue(acc, x1r, tr):
        dy = (acc + x1r - tr) * (1.0 / d)
        return dy, dy

    dy, dyb = _matmul("mlp_down", "nn", act, w["w_down"], outs=[F32, BF16], extras=[x1, target],
                      epilogue=loss_epilogue, **big)

    def dup_epilogue(acc, upr):
        return (acc * 2.0 * jnp.maximum(upr.astype(F32), 0.0),)

    def by_owner(g):
        return g.reshape(N_DEV, -1, g.shape[-1])

    g, parts = {}, {}
    g["w_down"] = _matmul("d_w_down", "tn", act, dyb, outs=[BF16], **wide_k)
    if comm:
        dup, got = _matmul("d_act", "nt", dyb, w["w_down"], outs=[BF16], extras=[up], epilogue=dup_epilogue,
                           rider=_pair_rider([by_owner(g["w_down"])]), **big)
        pair = _pair_sum("pair_w_down", by_owner(g["w_down"]), got, comm["c"])
        g["w_up"], parts["w_down"] = _matmul("d_w_up", "tn", h2, dup, outs=[BF16], out_blocks=True,
                                             rider=_chip_rider([pair]), **wide_k)
        dh2, got = _matmul("d_h2", "nt", dup, w["w_up"], outs=[F32], rider=_pair_rider([g["w_up"]]), **big)
        pair_up = _pair_sum("pair_w_up", g["w_up"], got, comm["c"])
    else:
        dup = _matmul("d_act", "nt", dyb, w["w_down"], outs=[BF16], extras=[up], epilogue=dup_epilogue, **big)
        g["w_up"] = _matmul("d_w_up", "tn", h2, dup, outs=[BF16], out_blocks=True, **wide_k)
        dh2 = _matmul("d_h2", "nt", dup, w["w_up"], outs=[F32], **big)
    dx1, dx1b, g_norm2 = _rms_bwd("rms2_bwd", dh2, x1, small["norm2_g"], res=dy)
    loss = _loss(dy, d)

    g["w_out"] = _matmul("d_w_out", "tn", merged, dx1b, outs=[BF16], **wide_k)
    dmerged = _matmul("d_merged", "nt", dx1b, w["w_out"], outs=[BF16], **big)
    dproj, *do3 = _merge_bwd(proj, o3, dmerged, 512, tc)
    names = ("w_conv_out", "w_fox_out", "w_mem_out")
    dys = []
    for s in range(3):
        g[names[s]] = _matmul(f"d_w_branch{s}", "tn", ys[s], do3[s], outs=[BF16], out_blocks=True, **wide_k)
        dys.append(_matmul(f"d_branch{s}", "nt", do3[s], w_outs[s], outs=[BF16], **big))

    dproj, dkv, g_mq, g_mk = _mem_bwd(proj, off_mq, kv, dys[2], small["mem_q_g"], small["mem_k_g"], tq, dproj)
    g["w_mem_kv"] = _matmul("d_w_mem_kv", "tn", nm, dkv, outs=[BF16], **wide_k)
    dnm = _matmul("d_mem_norm", "nt", dkv, w["w_mem_kv"], outs=[F32], tm=256, tn=512, tk=2048)
    _, _, g_mem_norm = _rms_bwd("mem_rms_bwd", dnm, mem, small["mem_norm_g"])

    mid = ("w_out", "w_conv_out", "w_fox_out", "w_mem_out", "w_mem_kv")
    if comm:
        mid8 = [g[n] if n in names else by_owner(g[n]) for n in mid]
        dproj, g_conv_w, *got = _conv_bwd(proj, off_conv, small["conv_w"], dys[0], LANES, dproj, rider=_pair_rider(mid8))
        pairs = [pair_up] + [_pair_sum("pair_" + n, g8, s4, comm["c"]) for n, g8, s4 in zip(mid, mid8, got)]
        dproj, dc, g_fq, g_fk, *got = _fox_bwd(proj, off_fox, y_fox, dys[1], small["fox_q_g"], small["fox_k_g"], c_row3, c_colb,
                                               lse, heads, tq, dproj, rider=_chip_rider(pairs))
        parts.update(zip(("w_up",) + mid, got))
    else:
        dproj, g_conv_w = _conv_bwd(proj, off_conv, small["conv_w"], dys[0], LANES, dproj)
        dproj, dc, g_fq, g_fk = _fox_bwd(proj, off_fox, y_fox, dys[1], small["fox_q_g"], small["fox_k_g"], c_row3, c_colb,
                                         lse, heads, tq, dproj)
    dc_row = jnp.pad(dc.reshape(heads, t), ((0, F_ROWS - heads), (0, 0)))
    dz_row, db = _forget_bwd(z_row, b_col, dc_row)

    g_all = _matmul("d_w_in", "tn", dproj, h, outs=[BF16], j_outer=True, **wide_k)
    g_wf = _matmul("d_w_f", "nn", dz_row, h, outs=[BF16], tm=F_ROWS, tn=512, tk=4096)
    g["w_in"] = _unpack_g_in(g_all, g_wf, d, tc, w["w_in"].shape[0])
    dh = _matmul("d_h_f", "tn", dz_row, w_f, outs=[F32], tm=1024, tn=512, tk=F_ROWS)
    add_prev = lambda acc, prev: (acc + prev,)
    if comm:
        g_in8 = g["w_in"]
        got = _run_rider("pair_exchange_w_in", _pair_rider([g_in8]))[0]
        pair = _pair_sum("pair_w_in", g_in8, got, comm["c"])
        dh, parts["w_in"] = _matmul("d_h", "nn", dproj, w_all, outs=[F32], extras=[dh], epilogue=add_prev,
                                    rider=_chip_rider([pair]), tm=1024, tn=512, tk=3328)
    else:
        dh = _matmul("d_h", "nn", dproj, w_all, outs=[F32], extras=[dh], epilogue=add_prev, tm=1024, tn=512, tk=3328)
    grad_x, _, g_norm1 = _rms_bwd("rms1_bwd", dh, x, small["norm1_g"], res=dx1)

    gs = dict(norm1_g=g_norm1, b_f=db[:heads, 0], conv_w=g_conv_w, fox_q_g=g_fq.reshape(-1), fox_k_g=g_fk.reshape(-1),
              mem_norm_g=g_mem_norm, mem_q_g=g_mq, mem_k_g=g_mk, norm2_g=g_norm2)
    return loss, grad_x, (parts if comm else g), gs


def _adamw_math(w, g, m, v):
    m = ADAM_B1 * m + (1.0 - ADAM_B1) * g
    v = ADAM_B2 * v + (1.0 - ADAM_B2) * jnp.square(g)
    m_hat = m / (1.0 - ADAM_B1 ** ADAM_STEP)
    v_hat = v / (1.0 - ADAM_B2 ** ADAM_STEP)
    delta = -ADAM_LR * (m_hat / (jnp.sqrt(v_hat) + ADAM_EPS) + ADAM_WD * w)
    return delta, m, v


def _adamw(name, parts, w, m, v):
    r, c = w.shape
    tr, tc = _tile2(r, c, 128, 256)
    n_parts = parts.shape[0]

    def body(p_ref, w_ref, m_ref, v_ref, g_ref, d_ref, nm_ref, nv_ref):
        g = p_ref[0].astype(F32)
        for s in range(1, n_parts):
            g = g + p_ref[s].astype(F32)
        delta, nm, nv = _adamw_math(w_ref[...], g, m_ref[...], v_ref[...])
        g_ref[...] = g
        d_ref[...] = delta
        nm_ref[...] = nm
        nv_ref[...] = nv

    blk = pl.BlockSpec((tr, tc), lambda i, j: (i, j))
    return pl.pallas_call(
        body,
        name=name,
        grid=(r // tr, c // tc),
        in_specs=[pl.BlockSpec((n_parts, tr, tc), lambda i, j: (0, i, j)), blk, blk, blk],
        out_specs=[blk] * 4,
        out_shape=[jax.ShapeDtypeStruct((r, c), F32)] * 4,
        compiler_params=_params(("parallel", "parallel")),
    )(parts, w, m, v)


def _sum_parts(name, parts):
    n_parts, r, c = parts.shape

    def body(p_ref, o_ref):
        acc = p_ref[0]
        for s in range(1, n_parts):
            acc = acc + p_ref[s]
        o_ref[...] = acc

    return pl.pallas_call(body, name=name, out_shape=jax.ShapeDtypeStruct((r, c), F32))(parts)


BIG = ("w_in", "w_mem_kv", "w_conv_out", "w_fox_out", "w_mem_out", "w_out", "w_up", "w_down")
COLUMN_SPLIT = ("w_in", "w_conv_out", "w_fox_out", "w_mem_out", "w_up")
SMALL = ("norm1_g", "b_f", "conv_w", "fox_q_g", "fox_k_g", "mem_norm_g", "mem_q_g", "mem_k_g", "norm2_g")
WEIGHTS = ("norm1_g", "w_in", "b_f", "conv_w", "fox_q_g", "fox_k_g", "mem_norm_g", "w_mem_kv", "mem_q_g", "mem_k_g",
           "w_conv_out", "w_fox_out", "w_mem_out", "w_out", "norm2_g", "w_up", "w_down")


def _pack(vectors):
    rows = []
    for vec in vectors:
        n = vec.shape[0]
        rows.append(jnp.pad(vec, (0, -n % LANES)).reshape(-1, LANES))
    out = jnp.concatenate(rows, axis=0)
    return jnp.pad(out, ((0, -out.shape[0] % 8), (0, 0)))


def _unpack(packed, sizes):
    out, row = [], 0
    for n in sizes:
        nr = -(-n // LANES)
        out.append(packed[row:row + nr].reshape(-1)[:n])
        row += nr
    return out


def kernel(x, mem, norm1_g, w_in, b_f, conv_w, fox_q_g, fox_k_g, mem_norm_g, w_mem_kv, mem_q_g, mem_k_g, w_conv_out, w_fox_out, w_mem_out, w_out, norm2_g, w_up, w_down, loss_target, m_norm1_g, m_w_in, m_b_f, m_conv_w, m_fox_q_g, m_fox_k_g, m_mem_norm_g, m_w_mem_kv, m_mem_q_g, m_mem_k_g, m_w_conv_out, m_w_fox_out, m_w_mem_out, m_w_out, m_norm2_g, m_w_up, m_w_down, v_norm1_g, v_w_in, v_b_f, v_conv_w, v_fox_q_g, v_fox_k_g, v_mem_norm_g, v_w_mem_kv, v_mem_q_g, v_mem_k_g, v_w_conv_out, v_w_fox_out, v_w_mem_out, v_w_out, v_norm2_g, v_w_up, v_w_down):
    args = dict(locals())
    wts = {n: args[n] for n in WEIGHTS}
    ms = {n: args["m_" + n] for n in WEIGHTS}
    vs = {n: args["v_" + n] for n in WEIGHTS}
    x_pos, y_pos, c_pos = _position()
    me = _index(x_pos, y_pos, c_pos)

    shards = {n: (_row_tiles(wts[n].T.astype(BF16)) if n == "w_in" else wts[n].astype(BF16)) for n in BIG}
    wi, wkv, cw8 = _run_rider("all_gather_first", _gather_rider([shards["w_in"], shards["w_mem_kv"], conv_w]))
    full = {"w_in": wi, "w_mem_kv": wkv.reshape(-1, wkv.shape[-1])}
    small = {n: wts[n] for n in SMALL}
    small["conv_w"] = _unblock(cw8)
    comm = {"shards": shards, "c": c_pos.astype(jnp.int32).reshape(1)}

    loss, grad_x, parts, gs = _local_step(x[0], mem[0], loss_target[0], full, small, comm)

    out_g, out_d, out_m, out_v = {}, {}, {}, {}
    for n in BIG:
        if n == "w_in":
            rows_t = parts[n].reshape(parts[n].shape[:2] + (-1,))
            res = _adamw("adamw_" + n, rows_t, wts[n].T, ms[n].T, vs[n].T)
            out_g[n], out_d[n], out_m[n], out_v[n] = (r.T for r in res)
        else:
            out_g[n], out_d[n], out_m[n], out_v[n] = _adamw("adamw_" + n, parts[n], wts[n], ms[n], vs[n])

    small_sizes = [int(math.prod(gs[n].shape)) for n in SMALL]
    packed = _pack([gs[n].reshape(-1) for n in SMALL])
    gsum = _sum_parts("sum_small", _run_rider("exchange_small", _broadcast_rider([packed]))[0])
    gsmall = dict(zip(SMALL, _unpack(gsum, small_sizes)))
    cols = conv_w.shape[1]
    gsmall["conv_w"] = lax.dynamic_slice(gsmall["conv_w"].reshape(CONV_TAPS, -1), (0, me * cols), (CONV_TAPS, cols)).reshape(-1)
    pg, pw, pm, pv = (_pack([src[n].reshape(-1) for n in SMALL]) for src in (gsmall, wts, ms, vs))
    _, sd, sm, sv = _adamw("adamw_small", pg[None], pw, pm, pv)
    local_sizes = [int(math.prod(wts[n].shape)) for n in SMALL]
    for dst, src in ((out_d, sd), (out_m, sm), (out_v, sv)):
        for n, val in zip(SMALL, _unpack(src, local_sizes)):
            dst[n] = val.reshape(wts[n].shape)
    for n in SMALL:
        out_g[n] = gsmall[n].reshape(wts[n].shape)

    loss = lax.psum(loss, MESH_AXES)
    return (loss, grad_x[None], *[out_g[n] for n in WEIGHTS], *[out_d[n] for n in WEIGHTS],
            *[out_m[n] for n in WEIGHTS], *[out_v[n] for n in WEIGHTS])
```

```python
import math

import jax
import jax.numpy as jnp
from jax import lax
from jax.experimental import pallas as pl
from jax.experimental.pallas import tpu as pltpu

F32 = jnp.float32
BF16 = jnp.bfloat16

EPS = 1e-6
N_DEV = 8
N_CHIPS = 4
FOX_HEAD_DIM = 128
MEM_HEADS = 4
CONV_TAPS = 3
N_BRANCHES = 3
F_ROWS = 16

ADAM_LR = 0.001
ADAM_B1 = 0.9
ADAM_B2 = 0.999
ADAM_EPS = 1e-08
ADAM_WD = 0.01
ADAM_STEP = 10

V7X_VMEM_BYTES = 64 * 1024 * 1024
VMEM_LIMIT = V7X_VMEM_BYTES * 3 // 4
LANES = 128
NEG = -1e30

MESH_AXES = ("x", "y", "c")
MESH = pl.DeviceIdType.MESH
ANY = pl.BlockSpec(memory_space=pl.ANY)

NN = (((1,), (0,)), ((), ()))
NT = (((1,), (1,)), ((), ()))
TN = (((0,), (0,)), ((), ()))


def _params(sem):
    return pltpu.CompilerParams(dimension_semantics=sem, vmem_limit_bytes=VMEM_LIMIT)


def _dot(a, b, dn):
    return lax.dot_general(a, b, dn, preferred_element_type=F32)


def _tile(n, t):
    if n <= t:
        return n
    for cand in range(t - t % LANES, 0, -LANES):
        if n % cand == 0:
            return cand
    raise ValueError((n, t))


class _Rider:
    def __init__(self, ins, out_shapes, sem_shapes, start, finish):
        self.ins, self.out_shapes, self.sem_shapes = list(ins), list(out_shapes), list(sem_shapes)
        self.start, self.finish = start, finish


def _position():
    return lax.axis_index("x"), lax.axis_index("y"), lax.axis_index("c")


def _index(px, py, pc):
    return 4 * px + 2 * py + pc


def _dma_sems(n, per):
    return [pltpu.SemaphoreType.DMA((n, per)), pltpu.SemaphoreType.DMA((n, per)), pltpu.SemaphoreType.DMA((n,))]


def _gather_rider(shards):
    n = len(shards)

    def copies(ins, outs, sems):
        send_sems, recv_sems, local_sems = sems
        x, y, c = _position()
        me, sibling = (x, y, c), (x, y, 1 - c)
        chips = [(1 - x, y), (x, 1 - y), (1 - x, 1 - y)]

        def copy(a, k, block, to, src=None):
            rows = outs[a].at[_index(*block)]
            return pltpu.make_async_remote_copy(
                src_ref=rows if src is None else src, dst_ref=rows,
                send_sem=send_sems.at[a, k], recv_sem=recv_sems.at[a, k], device_id=to, device_id_type=MESH)

        mine = [pltpu.make_async_copy(ins[a], outs[a].at[_index(*me)], local_sems.at[a]) for a in range(n)]
        first = []
        for a in range(n):
            first.append(copy(a, 0, me, sibling, src=ins[a]))
            first += [copy(a, 1 + j, me, (*chip, c), src=ins[a]) for j, chip in enumerate(chips)]
        return copy, mine, first, me, sibling, chips, c

    def start(ins, outs, sems):
        _, mine, first, *_ = copies(ins, outs, sems)
        for cp in mine + first:
            cp.start()

    def finish(ins, outs, sems):
        copy, mine, first, me, sibling, chips, c = copies(ins, outs, sems)
        passed = []
        for a in range(n):
            for j, chip in enumerate(chips):
                copy(a, 1 + j, (*chip, c), me).wait_recv()
                fwd = copy(a, 4 + j, (*chip, c), sibling)
                fwd.start()
                passed.append(fwd)
        for a in range(n):
            copy(a, 0, sibling, me).wait_recv()
            for j, chip in enumerate(chips):
                copy(a, 4 + j, (*chip, 1 - c), me).wait_recv()
        for cp in first + passed:
            cp.wait_send()
        for cp in mine:
            cp.wait()

    out_shapes = [jax.ShapeDtypeStruct((N_DEV,) + s.shape, s.dtype) for s in shards]
    return _Rider(shards, out_shapes, _dma_sems(n, 7), start, finish)


def _pair_rider(grads):
    n = len(grads)

    def copies(ins, outs, sems):
        send_sems, recv_sems, _ = sems
        x, y, c = _position()
        return [pltpu.make_async_remote_copy(
            src_ref=ins[a].at[2 * q + (1 - c)], dst_ref=outs[a].at[q],
            send_sem=send_sems.at[a, q], recv_sem=recv_sems.at[a, q], device_id=(x, y, 1 - c), device_id_type=MESH)
            for a in range(n) for q in range(N_CHIPS)]

    def start(ins, outs, sems):
        for cp in copies(ins, outs, sems):
            cp.start()

    def finish(ins, outs, sems):
        cps = copies(ins, outs, sems)
        for cp in cps:
            cp.wait_recv()
        for cp in cps:
            cp.wait_send()

    out_shapes = [jax.ShapeDtypeStruct((N_CHIPS,) + g.shape[1:], g.dtype) for g in grads]
    return _Rider(grads, out_shapes, _dma_sems(n, N_CHIPS), start, finish)


def _chip_rider(parts):
    n = len(parts)

    def copies(ins, outs, sems):
        send_sems, recv_sems, local_sems = sems
        x, y, c = _position()
        q_me = 2 * x + y
        chips = [(1 - x, y), (x, 1 - y), (1 - x, 1 - y)]
        mine = [pltpu.make_async_copy(ins[a].at[q_me], outs[a].at[q_me], local_sems.at[a]) for a in range(n)]
        sends, arrivals = [], []
        for a in range(n):
            for j, (tx, ty) in enumerate(chips):
                q_t = 2 * tx + ty
                sends.append(pltpu.make_async_remote_copy(
                    src_ref=ins[a].at[q_t], dst_ref=outs[a].at[q_me],
                    send_sem=send_sems.at[a, j], recv_sem=recv_sems.at[a, j], device_id=(tx, ty, c), device_id_type=MESH))
                arrivals.append(pltpu.make_async_remote_copy(
                    src_ref=ins[a].at[q_t], dst_ref=outs[a].at[q_t],
                    send_sem=send_sems.at[a, j], recv_sem=recv_sems.at[a, j], device_id=(tx, ty, c), device_id_type=MESH))
        return mine, sends, arrivals

    def start(ins, outs, sems):
        mine, sends, _ = copies(ins, outs, sems)
        for cp in mine + sends:
            cp.start()

    def finish(ins, outs, sems):
        mine, sends, arrivals = copies(ins, outs, sems)
        for cp in arrivals:
            cp.wait_recv()
        for cp in sends:
            cp.wait_send()
        for cp in mine:
            cp.wait()

    out_shapes = [jax.ShapeDtypeStruct(p.shape, p.dtype) for p in parts]
    return _Rider(parts, out_shapes, _dma_sems(n, 3), start, finish)


def _broadcast_rider(values):
    n = len(values)

    def copies(ins, outs, sems):
        send_sems, recv_sems, local_sems = sems
        x, y, c = _position()
        me = _index(x, y, c)

        def peer(k):
            return (1 - x if k & 4 else x, 1 - y if k & 2 else y, 1 - c if k & 1 else c)

        mine = [pltpu.make_async_copy(ins[a], outs[a].at[me], local_sems.at[a]) for a in range(n)]
        sends, arrivals = [], []
        for a in range(n):
            for k in range(1, N_DEV):
                common = dict(send_sem=send_sems.at[a, k - 1], recv_sem=recv_sems.at[a, k - 1], device_id=peer(k), device_id_type=MESH)
                sends.append(pltpu.make_async_remote_copy(src_ref=ins[a], dst_ref=outs[a].at[me], **common))
                arrivals.append(pltpu.make_async_remote_copy(src_ref=ins[a], dst_ref=outs[a].at[_index(*peer(k))], **common))
        return mine, sends, arrivals

    def start(ins, outs, sems):
        mine, sends, _ = copies(ins, outs, sems)
        for cp in mine + sends:
            cp.start()

    def finish(ins, outs, sems):
        mine, sends, arrivals = copies(ins, outs, sems)
        for cp in arrivals:
            cp.wait_recv()
        for cp in sends:
            cp.wait_send()
        for cp in mine:
            cp.wait()

    out_shapes = [jax.ShapeDtypeStruct((N_DEV,) + v.shape, v.dtype) for v in values]
    return _Rider(values, out_shapes, _dma_sems(n, 7), start, finish)


def _run_rider(name, rider):
    n_in, n_out = len(rider.ins), len(rider.out_shapes)

    def body(*refs):
        ins, outs, sems = refs[:n_in], refs[n_in:n_in + n_out], refs[n_in + n_out:]
        rider.start(ins, outs, sems)
        rider.finish(ins, outs, sems)

    return pl.pallas_call(
        body, name=name, in_specs=[ANY] * n_in, out_specs=[ANY] * n_out, out_shape=rider.out_shapes,
        scratch_shapes=rider.sem_shapes)(*rider.ins)


class _Host:
    def __init__(self, rider):
        self.rider = rider
        self.n_in = len(rider.ins) if rider else 0
        self.n_out = len(rider.out_shapes) if rider else 0
        self.n_sem = len(rider.sem_shapes) if rider else 0
        self.ins = rider.ins if rider else []
        self.in_specs = [ANY] * self.n_in
        self.out_specs = [ANY] * self.n_out
        self.out_shapes = rider.out_shapes if rider else []
        self.scratch = rider.sem_shapes if rider else []

    def run(self, first, last, ins, outs, sems, compute):
        if self.rider is None:
            compute()
            return

        @pl.when(first)
        def _():
            self.rider.start(ins, outs, sems)

        compute()

        @pl.when(last)
        def _():
            self.rider.finish(ins, outs, sems)


def _matmul(name, kind, a, b, *, tm, tn, tk, outs, epilogue=None, extras=(), out_blocks=False, rider=None, j_outer=False):
    if kind == "nn":
        (m, kdim), n = a.shape, b.shape[1]
    elif kind == "nt":
        (m, kdim), n = a.shape, b.shape[0]
    else:
        (kdim, m), n = a.shape, b.shape[1]
    if out_blocks:
        tn = min(tn, n // N_DEV)
    tm, tn, tk = _tile(m, tm), _tile(n, tn), _tile(kdim, tk)
    ni, nj, nk = m // tm, n // tn, kdim // tk

    def spec(shape, fn):
        return pl.BlockSpec(shape, (lambda g0, g1, k: fn(g1, g0, k)) if j_outer else fn)

    a_spec = spec((tk, tm), lambda i, j, k: (k, i)) if kind == "tn" else spec((tm, tk), lambda i, j, k: (i, k))
    b_spec = spec((tn, tk), lambda i, j, k: (j, k)) if kind == "nt" else spec((tk, tn), lambda i, j, k: (k, j))
    dn = {"nn": NN, "nt": NT, "tn": TN}[kind]

    tile_spec = spec((tm, tn), lambda i, j, k: (i, j))
    if out_blocks:
        width = n // N_DEV
        r_out = width // tn
        out_shape = [jax.ShapeDtypeStruct((N_DEV, m, width), dt) for dt in outs]
        out_specs = [spec((None, tm, tn), lambda i, j, k: (j // r_out, i, j % r_out)) for _ in outs]
    else:
        out_shape = [jax.ShapeDtypeStruct((m, n), dt) for dt in outs]
        out_specs = [tile_spec for _ in outs]
    n_ex, n_out = len(extras), len(outs)
    host = _Host(rider)
    n_acc = 1 if nk > 1 else 0

    def body(*refs):
        a_ref, b_ref = refs[0], refs[1]
        pos = 2
        ex_refs = refs[pos:pos + n_ex]; pos += n_ex
        r_ins = refs[pos:pos + host.n_in]; pos += host.n_in
        out_refs = refs[pos:pos + n_out]; pos += n_out
        r_outs = refs[pos:pos + host.n_out]; pos += host.n_out
        acc_ref = refs[pos] if n_acc else None
        sems = refs[pos + n_acc:]
        i, j, k = pl.program_id(1 if j_outer else 0), pl.program_id(0 if j_outer else 1), pl.program_id(2)

        def finish_tile(acc):
            vals = (acc,) if epilogue is None else epilogue(acc, *[e[...] for e in ex_refs])
            for o_ref, v in zip(out_refs, vals):
                o_ref[...] = v.astype(o_ref.dtype)

        def compute():
            part = _dot(a_ref[...], b_ref[...], dn)
            if nk == 1:
                finish_tile(part)
                return

            @pl.when(k == 0)
            def _():
                acc_ref[...] = part

            @pl.when(jnp.logical_and(k > 0, k < nk - 1))
            def _():
                acc_ref[...] += part

            @pl.when(k == nk - 1)
            def _():
                finish_tile(acc_ref[...] + part)

        first = jnp.logical_and(jnp.logical_and(i == 0, j == 0), k == 0)
        last = jnp.logical_and(jnp.logical_and(i == ni - 1, j == nj - 1), k == nk - 1)
        host.run(first, last, r_ins, r_outs, sems, compute)

    sem = ("arbitrary",) * 3 if rider else ("parallel", "parallel", "arbitrary")
    res = pl.pallas_call(
        body,
        name=name,
        grid=(nj, ni, nk) if j_outer else (ni, nj, nk),
        in_specs=[a_spec, b_spec] + [tile_spec for _ in extras] + host.in_specs,
        out_specs=out_specs + host.out_specs,
        out_shape=out_shape + host.out_shapes,
        scratch_shapes=([pltpu.VMEM((tm, tn), F32)] if n_acc else []) + host.scratch,
        compiler_params=_params(sem),
    )(a, b, *extras, *host.ins)
    return res[0] if len(res) == 1 else res


def _rms_fwd(name, x, g, tm=512):
    t, d = x.shape
    tm = _tile(t, tm)

    def body(x_ref, g_ref, h_ref):
        xf = x_ref[...]
        r = lax.rsqrt(jnp.mean(xf * xf, axis=-1, keepdims=True) + EPS)
        h_ref[...] = (xf * r * g_ref[...]).astype(h_ref.dtype)

    return pl.pallas_call(
        body,
        name=name,
        grid=(t // tm,),
        in_specs=[pl.BlockSpec((tm, d), lambda i: (i, 0)), pl.BlockSpec((1, d), lambda i: (0, 0))],
        out_specs=pl.BlockSpec((tm, d), lambda i: (i, 0)),
        out_shape=jax.ShapeDtypeStruct((t, d), BF16),
        compiler_params=_params(("parallel",)),
    )(x, g.reshape(1, d))


def _rms_bwd(name, dh, x, g, res=None, tm=256):
    t, d = x.shape
    tm = _tile(t, tm)
    has_res = res is not None

    def body(*refs):
        if has_res:
            dh_ref, x_ref, g_ref, res_ref, dx_ref, dxb_ref, gg_ref = refs
        else:
            dh_ref, x_ref, g_ref, dx_ref, dxb_ref, gg_ref = refs
        i = pl.program_id(0)
        xf = x_ref[...]
        r = lax.rsqrt(jnp.mean(xf * xf, axis=-1, keepdims=True) + EPS)
        xh = xf * r
        dhf = dh_ref[...].astype(F32)
        dxh = dhf * g_ref[...]
        dx = r * (dxh - xh * jnp.mean(dxh * xh, axis=-1, keepdims=True))
        if has_res:
            dx = dx + res_ref[...]
        dx_ref[...] = dx
        dxb_ref[...] = dx.astype(BF16)

        @pl.when(i == 0)
        def _():
            gg_ref[...] = jnp.zeros_like(gg_ref)

        gg_ref[...] += jnp.sum(dhf * xh, axis=0, keepdims=True)

    row = pl.BlockSpec((tm, d), lambda i: (i, 0))
    vec = pl.BlockSpec((1, d), lambda i: (0, 0))
    ins = [dh, x, g.reshape(1, d)] + ([res] if has_res else [])
    dx, dxb, gg = pl.pallas_call(
        body,
        name=name,
        grid=(t // tm,),
        in_specs=[row, row, vec] + ([row] if has_res else []),
        out_specs=[row, row, vec],
        out_shape=[jax.ShapeDtypeStruct((t, d), F32), jax.ShapeDtypeStruct((t, d), BF16), jax.ShapeDtypeStruct((1, d), F32)],
        compiler_params=_params(("arbitrary",)),
    )(*ins)
    return dx, dxb, gg.reshape(d)


def _head_rms(xf):
    r = lax.rsqrt(jnp.mean(xf * xf, axis=-1, keepdims=True) + EPS)
    return xf * r, r


def _head_rms_bwd(dy, xn, r, g):
    dxh = dy * g
    dx = r * (dxh - xn * jnp.mean(dxh * xn, axis=-1, keepdims=True))
    return dx, jnp.sum(dy * xn, axis=0, keepdims=True)


def _col_to_row(col):
    n = col.shape[0]
    eye = lax.broadcasted_iota(jnp.int32, (n, n), 0) == lax.broadcasted_iota(jnp.int32, (n, n), 1)
    return jnp.sum(jnp.where(eye, col, 0.0), axis=0, keepdims=True)


def _row_to_col(row):
    n = row.shape[1]
    eye = lax.broadcasted_iota(jnp.int32, (n, n), 0) == lax.broadcasted_iota(jnp.int32, (n, n), 1)
    return jnp.sum(jnp.where(eye, row, 0.0), axis=1, keepdims=True)


def _dproj_args(dproj, n_in):
    if dproj is None:
        return [], [], {}
    return [dproj], [ANY], {n_in: 0}


def _shift_down(u, s, rows):
    return jnp.where(rows >= s, pltpu.roll(u, s, axis=0), 0.0)


def _shift_up(u, s, rows, t):
    return jnp.where(rows < t - s, pltpu.roll(u, t - s, axis=0), 0.0)


def _conv_fwd(proj, off, conv_w, cb):
    t = proj.shape[0]
    c = conv_w.shape[1]
    blk0 = off // (3 * cb)

    def body(p_ref, w_ref, y_ref):
        rows = lax.broadcasted_iota(jnp.int32, (t, cb), 0)
        bg = p_ref[:, 0:cb].astype(F32)
        u = p_ref[:, cb:2 * cb].astype(F32) * p_ref[:, 2 * cb:3 * cb].astype(F32)
        w = w_ref[...]
        conv = w[2:3] * u + w[1:2] * _shift_down(u, 1, rows) + w[0:1] * _shift_down(u, 2, rows)
        y_ref[...] = (bg * conv).astype(y_ref.dtype)

    return pl.pallas_call(
        body,
        name="conv_fwd",
        grid=(c // cb,),
        in_specs=[pl.BlockSpec((t, 3 * cb), lambda j: (0, blk0 + j)), pl.BlockSpec((CONV_TAPS, cb), lambda j: (0, j))],
        out_specs=pl.BlockSpec((t, cb), lambda j: (0, j)),
        out_shape=jax.ShapeDtypeStruct((t, c), BF16),
        compiler_params=_params(("parallel",)),
    )(proj, conv_w)


def _conv_bwd(proj, off, conv_w, dy, cb, dproj, rider=None):
    t = proj.shape[0]
    c = conv_w.shape[1]
    blk0 = off // (3 * cb)
    nj = c // cb
    host = _Host(rider)

    def body(*refs):
        p_ref, w_ref, dy_ref = refs[:3]
        r_ins = refs[4:4 + host.n_in]
        dp_ref, gw_ref = refs[4 + host.n_in:6 + host.n_in]
        r_outs = refs[6 + host.n_in:6 + host.n_in + host.n_out]
        sems = refs[6 + host.n_in + host.n_out:]
        j = pl.program_id(0)

        def compute():
            rows = lax.broadcasted_iota(jnp.int32, (t, cb), 0)
            bg = p_ref[:, 0:cb].astype(F32)
            cg = p_ref[:, cb:2 * cb].astype(F32)
            v = p_ref[:, 2 * cb:3 * cb].astype(F32)
            u = cg * v
            w = w_ref[...]
            u1 = _shift_down(u, 1, rows)
            u2 = _shift_down(u, 2, rows)
            conv = w[2:3] * u + w[1:2] * u1 + w[0:1] * u2
            dyf = dy_ref[...].astype(F32)
            dconv = dyf * bg
            du = w[2:3] * dconv + w[1:2] * _shift_up(dconv, 1, rows, t) + w[0:1] * _shift_up(dconv, 2, rows, t)
            dp_ref[:, 0:cb] = (dyf * conv).astype(dp_ref.dtype)
            dp_ref[:, cb:2 * cb] = (du * v).astype(dp_ref.dtype)
            dp_ref[:, 2 * cb:3 * cb] = (du * cg).astype(dp_ref.dtype)
            gw_ref[0:1, :] = jnp.sum(dconv * u2, axis=0, keepdims=True)
            gw_ref[1:2, :] = jnp.sum(dconv * u1, axis=0, keepdims=True)
            gw_ref[2:3, :] = jnp.sum(dconv * u, axis=0, keepdims=True)

        host.run(j == 0, j == nj - 1, r_ins, r_outs, sems, compute)

    res = pl.pallas_call(
        body,
        name="conv_bwd",
        grid=(nj,),
        in_specs=[
            pl.BlockSpec((t, 3 * cb), lambda j: (0, blk0 + j)),
            pl.BlockSpec((CONV_TAPS, cb), lambda j: (0, j)),
            pl.BlockSpec((t, cb), lambda j: (0, j)),
            ANY,
        ] + host.in_specs,
        out_specs=[pl.BlockSpec((t, 3 * cb), lambda j: (0, blk0 + j)), pl.BlockSpec((CONV_TAPS, cb), lambda j: (0, j))] + host.out_specs,
        out_shape=[jax.ShapeDtypeStruct(dproj.shape, dproj.dtype), jax.ShapeDtypeStruct((CONV_TAPS, c), F32)] + host.out_shapes,
        input_output_aliases={3: 0},
        scratch_shapes=host.scratch,
        compiler_params=_params(("arbitrary",)),
    )(proj, conv_w, dy, dproj, *host.ins)
    return res


def _lane_scan(x, reverse):
    lane = lax.broadcasted_iota(jnp.int32, x.shape, 1)
    s = 1
    while s < LANES:
        if reverse:
            x = x + jnp.where(lane < LANES - s, pltpu.roll(x, LANES - s, axis=1), 0.0)
        else:
            x = x + jnp.where(lane >= s, pltpu.roll(x, s, axis=1), 0.0)
        s *= 2
    return x


def _scan_rows(src_ref, dst_ref, t, reverse, fn=None):
    groups = list(range(t // LANES))
    if reverse:
        groups = groups[::-1]
    carry = None
    for gi in groups:
        sl = slice(gi * LANES, (gi + 1) * LANES)
        blk = src_ref[:, sl]
        if fn is not None:
            blk = fn(blk)
        blk = _lane_scan(blk, reverse)
        if carry is not None:
            blk = blk + carry
        dst_ref[:, sl] = blk
        carry = blk[:, 0:1] if reverse else blk[:, LANES - 1:LANES]


def _forget_fwd(z_row, b_col):
    rows, t = z_row.shape

    def body(z_ref, b_ref, c_ref):
        def logf(z):
            zz = z + b_ref[...]
            return jnp.minimum(zz, 0.0) - jnp.log(1.0 + jnp.exp(-jnp.abs(zz)))

        _scan_rows(z_ref, c_ref, t, False, logf)

    return pl.pallas_call(
        body,
        name="forget_fwd",
        out_shape=jax.ShapeDtypeStruct((rows, t), F32),
        compiler_params=pltpu.CompilerParams(vmem_limit_bytes=VMEM_LIMIT),
    )(z_row, b_col)


def _rows_to_colb(c_row3, tq):
    heads, _, t = c_row3.shape

    def body(r_ref, o_ref):
        o_ref[...] = jnp.broadcast_to(_row_to_col(r_ref[...]), (tq, LANES))

    return pl.pallas_call(
        body,
        name="rows_to_colb",
        grid=(heads, t // tq),
        in_specs=[pl.BlockSpec((None, 1, tq), lambda h, i: (h, 0, i))],
        out_specs=pl.BlockSpec((None, tq, LANES), lambda h, i: (h, i, 0)),
        out_shape=jax.ShapeDtypeStruct((heads, t, LANES), F32),
        compiler_params=_params(("parallel", "parallel")),
    )(c_row3)


def _forget_bwd(z_row, b_col, dc_row):
    rows, t = z_row.shape

    def body(z_ref, b_ref, dc_ref, dz_ref, db_ref, tmp_ref):
        _scan_rows(dc_ref, tmp_ref, t, True)
        zz = z_ref[...] + b_ref[...]
        dz = tmp_ref[...] * (1.0 / (1.0 + jnp.exp(zz)))
        dz_ref[...] = dz.astype(dz_ref.dtype)
        db_ref[...] = jnp.sum(dz, axis=1, keepdims=True)

    return pl.pallas_call(
        body,
        name="forget_bwd",
        out_shape=[jax.ShapeDtypeStruct((rows, t), BF16), jax.ShapeDtypeStruct((rows, 1), F32)],
        scratch_shapes=[pltpu.VMEM((rows, t), F32)],
        compiler_params=pltpu.CompilerParams(vmem_limit_bytes=VMEM_LIMIT),
    )(z_row, b_col, dc_row)


def _fox_fwd(proj, off, gq, gk, c_row3, c_colb, heads, tq, rider=None):
    t = proj.shape[0]
    hd = FOX_HEAD_DIM
    tq = _tile(t, tq)
    nq = t // tq
    blk0 = off // hd
    scale = 1.0 / math.sqrt(hd)
    host = _Host(rider)

    def body(*refs):
        q_ref, k_ref, v_ref, gq_ref, gk_ref, crow_ref, ccol_ref = refs[:7]
        r_ins = refs[7:7 + host.n_in]
        o_ref, lse_ref = refs[7 + host.n_in:9 + host.n_in]
        r_outs = refs[9 + host.n_in:9 + host.n_in + host.n_out]
        khat_ref, v_t_ref = refs[9 + host.n_in + host.n_out:11 + host.n_in + host.n_out]
        sems = refs[11 + host.n_in + host.n_out:]
        h, qi = pl.program_id(0), pl.program_id(1)

        def compute():
            eye = (lax.broadcasted_iota(jnp.int32, (hd, hd), 0) == lax.broadcasted_iota(jnp.int32, (hd, hd), 1)).astype(BF16)

            @pl.when(qi == 0)
            def _():
                kn, _ = _head_rms(k_ref[...].astype(F32))
                khat_ref[...] = (kn * gk_ref[...]).astype(BF16)
                v_t_ref[...] = _dot(eye, v_ref[...], NT).astype(BF16)

            qn, _ = _head_rms(q_ref[...].astype(F32))
            qhat = (qn * (gq_ref[...] * scale)).astype(BF16)
            crow = crow_ref[:, pl.ds(pl.multiple_of(qi * tq, tq), tq)]
            above = lax.broadcasted_iota(jnp.int32, (tq, tq), 1) >= lax.broadcasted_iota(jnp.int32, (tq, tq), 0)

            def tile(j, carry, diagonal):
                m, l, acc_t = carry
                ks = pl.multiple_of(j * tq, tq)
                s_t = _dot(khat_ref[pl.ds(ks, tq), :], qhat, NT) - ccol_ref[pl.ds(ks, tq), 0:1]
                if diagonal:
                    s_t = jnp.where(above, s_t, NEG)
                m_new = jnp.maximum(m, jnp.max(s_t, axis=0, keepdims=True) + crow)
                alpha = jnp.exp(m - m_new)
                p_t = jnp.exp(s_t + (crow - m_new))
                l = alpha * l + jnp.sum(p_t, axis=0, keepdims=True)
                acc_t = alpha * acc_t + _dot(v_t_ref[:, pl.ds(ks, tq)], p_t.astype(BF16), NN)
                return m_new, l, acc_t

            init = (jnp.full((1, tq), NEG, F32), jnp.zeros((1, tq), F32), jnp.zeros((hd, tq), F32))
            carry = lax.fori_loop(0, qi, lambda j, c: tile(j, c, False), init)
            m, l, acc_t = tile(qi, carry, True)
            o_ref[...] = _dot((acc_t / l).astype(BF16), eye, TN).astype(o_ref.dtype)
            lse_ref[...] = m + jnp.log(l)

        first = jnp.logical_and(h == 0, qi == 0)
        last = jnp.logical_and(h == heads - 1, qi == nq - 1)
        host.run(first, last, r_ins, r_outs, sems, compute)

    res = pl.pallas_call(
        body,
        name="fox_fwd",
        grid=(heads, nq),
        in_specs=[
            pl.BlockSpec((tq, hd), lambda h, i: (i, blk0 + 3 * h)),
            pl.BlockSpec((t, hd), lambda h, i: (0, blk0 + 3 * h + 1)),
            pl.BlockSpec((t, hd), lambda h, i: (0, blk0 + 3 * h + 2)),
            pl.BlockSpec((1, hd), lambda h, i: (0, 0)),
            pl.BlockSpec((1, hd), lambda h, i: (0, 0)),
            pl.BlockSpec((None, 1, t), lambda h, i: (h, 0, 0)),
            pl.BlockSpec((None, t, LANES), lambda h, i: (h, 0, 0)),
        ] + host.in_specs,
        out_specs=[pl.BlockSpec((tq, hd), lambda h, i: (i, h)), pl.BlockSpec((None, 1, tq), lambda h, i: (h, 0, i))] + host.out_specs,
        out_shape=[jax.ShapeDtypeStruct((t, heads * hd), BF16), jax.ShapeDtypeStruct((heads, 1, t), F32)] + host.out_shapes,
        scratch_shapes=[pltpu.VMEM((t, hd), BF16), pltpu.VMEM((hd, t), BF16)] + host.scratch,
        compiler_params=_params(("arbitrary", "arbitrary")),
    )(proj, proj, proj, gq.reshape(1, hd), gk.reshape(1, hd), c_row3, c_colb, *host.ins)
    return res


def _fox_bwd(proj, off, o, do, gq, gk, c_row3, c_colb, lse, heads, tq, dproj, rider=None):
    t = proj.shape[0]
    hd = FOX_HEAD_DIM
    tq = _tile(t, tq)
    nb = t // tq
    blk0 = off // hd
    scale = 1.0 / math.sqrt(hd)
    host = _Host(rider)
    n_fixed_in = 11

    def body(*refs):
        q_ref, k_ref, v_ref, o_ref, do_ref, gq_ref, gk_ref, crow_ref, ccol_ref, lse_ref = refs[:10]
        pos = n_fixed_in
        r_ins = refs[pos:pos + host.n_in]; pos += host.n_in
        dp_ref, dc_ref, ggq_ref, ggk_ref = refs[pos:pos + 4]; pos += 4
        r_outs = refs[pos:pos + host.n_out]; pos += host.n_out
        qhat_ref, khat_ref, khat_t_ref, dq_t_ref, dk_ref, dcq_ref, dck_ref, delta_ref = refs[pos:pos + 8]; pos += 8
        sems = refs[pos:]
        h = pl.program_id(0)

        def compute():
            qn, rq = _head_rms(q_ref[...].astype(F32))
            qhat_ref[...] = (qn * (gq_ref[...] * scale)).astype(BF16)
            kn, rk = _head_rms(k_ref[...].astype(F32))
            khat_ref[...] = (kn * gk_ref[...]).astype(BF16)
            eye = (lax.broadcasted_iota(jnp.int32, (hd, hd), 0) == lax.broadcasted_iota(jnp.int32, (hd, hd), 1)).astype(BF16)
            khat_t_ref[...] = _dot(eye, khat_ref[...], NT).astype(BF16)
            delta = jnp.sum(do_ref[...].astype(F32) * o_ref[...].astype(F32), axis=-1, keepdims=True)
            for b in range(nb):
                sl = slice(b * tq, (b + 1) * tq)
                delta_ref[:, sl] = _col_to_row(delta[sl, :])
            dq_t_ref[...] = jnp.zeros_like(dq_t_ref)
            dcq_ref[...] = jnp.zeros_like(dcq_ref)
            above = lax.broadcasted_iota(jnp.int32, (tq, tq), 1) >= lax.broadcasted_iota(jnp.int32, (tq, tq), 0)

            def kv_block(j, _):
                ks = pl.multiple_of(j * tq, tq)
                kh = khat_ref[pl.ds(ks, tq), :]
                kh_t = khat_t_ref[:, pl.ds(ks, tq)]
                vv = v_ref[pl.ds(ks, tq), :]
                ccol = ccol_ref[pl.ds(ks, tq), 0:1]

                def q_block(i, carry, diagonal):
                    dk, dv, dck = carry
                    qs = pl.multiple_of(i * tq, tq)
                    qh = qhat_ref[pl.ds(qs, tq), :]
                    dob = do_ref[pl.ds(qs, tq), :]
                    s_t = _dot(kh, qh, NT) + ((crow_ref[:, pl.ds(qs, tq)] - lse_ref[:, pl.ds(qs, tq)]) - ccol)
                    p_t = jnp.exp(s_t)
                    if diagonal:
                        p_t = jnp.where(above, p_t, 0.0)
                    ds_t = p_t * (_dot(vv, dob, NT) - delta_ref[:, pl.ds(qs, tq)])
                    dsb = ds_t.astype(BF16)
                    dv = dv + _dot(p_t.astype(BF16), dob, NN)
                    dk = dk + _dot(dsb, qh, NN)
                    dq_t_ref[:, pl.ds(qs, tq)] += _dot(kh_t, dsb, NN)
                    dcq_ref[:, pl.ds(qs, tq)] += jnp.sum(ds_t, axis=0, keepdims=True)
                    dck = dck + jnp.sum(ds_t, axis=-1, keepdims=True)
                    return dk, dv, dck

                zero = jnp.zeros((tq, hd), F32)
                carry = q_block(j, (zero, zero, jnp.zeros((tq, 1), F32)), True)
                dk, dv, dck = lax.fori_loop(j + 1, nb, lambda i, c: q_block(i, c, False), carry)
                dk_ref[pl.ds(ks, tq), :] = dk
                dp_ref[pl.ds(ks, tq), 2 * hd:3 * hd] = dv.astype(dp_ref.dtype)
                dck_ref[pl.ds(ks, tq), :] = dck
                return 0

            lax.fori_loop(0, nb, kv_block, 0)

            dq, ggq = _head_rms_bwd(dq_t_ref[...].T * scale, qn, rq, gq_ref[...])
            dk, ggk = _head_rms_bwd(dk_ref[...], kn, rk, gk_ref[...])
            dp_ref[:, 0:hd] = dq.astype(dp_ref.dtype)
            dp_ref[:, hd:2 * hd] = dk.astype(dp_ref.dtype)
            for b in range(nb):
                sl = slice(b * tq, (b + 1) * tq)
                dc_ref[:, sl] = dcq_ref[:, sl] - _col_to_row(dck_ref[sl, :])

            @pl.when(h == 0)
            def _():
                ggq_ref[...] = jnp.zeros_like(ggq_ref)
                ggk_ref[...] = jnp.zeros_like(ggk_ref)

            ggq_ref[...] += ggq
            ggk_ref[...] += ggk

        host.run(h == 0, h == heads - 1, r_ins, r_outs, sems, compute)

    head_in = lambda part: pl.BlockSpec((t, hd), lambda h: (0, blk0 + 3 * h + part))
    vec = pl.BlockSpec((1, hd), lambda h: (0, 0))
    colb = pl.BlockSpec((None, t, LANES), lambda h: (h, 0, 0))
    res = pl.pallas_call(
        body,
        name="fox_bwd",
        grid=(heads,),
        in_specs=[
            head_in(0), head_in(1), head_in(2),
            pl.BlockSpec((t, hd), lambda h: (0, h)),
            pl.BlockSpec((t, hd), lambda h: (0, h)),
            vec, vec,
            pl.BlockSpec((None, 1, t), lambda h: (h, 0, 0)),
            colb,
            pl.BlockSpec((None, 1, t), lambda h: (h, 0, 0)),
            ANY,
        ] + host.in_specs,
        out_specs=[
            pl.BlockSpec((t, 3 * hd), lambda h: (0, blk0 // 3 + h)),
            pl.BlockSpec((None, 1, t), lambda h: (h, 0, 0)),
            vec, vec,
        ] + host.out_specs,
        out_shape=[
            jax.ShapeDtypeStruct(dproj.shape, dproj.dtype),
            jax.ShapeDtypeStruct((heads, 1, t), F32),
            jax.ShapeDtypeStruct((1, hd), F32),
            jax.ShapeDtypeStruct((1, hd), F32),
        ] + host.out_shapes,
        input_output_aliases={10: 0},
        scratch_shapes=[
            pltpu.VMEM((t, hd), BF16), pltpu.VMEM((t, hd), BF16), pltpu.VMEM((hd, t), BF16),
            pltpu.VMEM((hd, t), F32), pltpu.VMEM((t, hd), F32),
            pltpu.VMEM((1, t), F32), pltpu.VMEM((t, 1), F32), pltpu.VMEM((1, t), F32),
        ] + host.scratch,
        compiler_params=_params(("arbitrary",)),
    )(proj, proj, proj, o, do, gq.reshape(1, hd), gk.reshape(1, hd), c_row3, c_colb, lse, dproj, *host.ins)
    return res


def _mem_fwd(proj, off, kv, gq, gk, tq):
    t = proj.shape[0]
    m, width = kv.shape[0], kv.shape[1] // 2
    hd = width // MEM_HEADS
    tq = _tile(t, tq)
    blk0 = off // hd
    scale = 1.0 / math.sqrt(hd)

    def body(q_ref, k_ref, v_ref, gq_ref, gk_ref, o_ref):
        qn, _ = _head_rms(q_ref[...].astype(F32))
        kn, _ = _head_rms(k_ref[...])
        s = _dot((qn * gq_ref[...]).astype(BF16), (kn * gk_ref[...]).astype(BF16), NT) * scale
        p = jnp.exp(s - jnp.max(s, axis=-1, keepdims=True))
        p = p / jnp.sum(p, axis=-1, keepdims=True)
        o_ref[...] = _dot(p.astype(BF16), v_ref[...].astype(BF16), NN).astype(o_ref.dtype)

    vec = pl.BlockSpec((1, hd), lambda h, i: (0, 0))
    return pl.pallas_call(
        body,
        name="mem_fwd",
        grid=(MEM_HEADS, t // tq),
        in_specs=[
            pl.BlockSpec((tq, hd), lambda h, i: (i, blk0 + h)),
            pl.BlockSpec((m, hd), lambda h, i: (0, h)),
            pl.BlockSpec((m, hd), lambda h, i: (0, MEM_HEADS + h)),
            vec, vec,
        ],
        out_specs=pl.BlockSpec((tq, hd), lambda h, i: (i, h)),
        out_shape=jax.ShapeDtypeStruct((t, width), BF16),
        compiler_params=_params(("parallel", "parallel")),
    )(proj, kv, kv, gq.reshape(1, hd), gk.reshape(1, hd))


def _mem_bwd(proj, off, kv, do, gq, gk, tq, dproj, rider=None):
    t = proj.shape[0]
    m, width = kv.shape[0], kv.shape[1] // 2
    hd = width // MEM_HEADS
    tq = _tile(t, tq)
    nq = t // tq
    blk0 = off // hd
    scale = 1.0 / math.sqrt(hd)
    host = _Host(rider)

    def body(*refs):
        q_ref, k_ref, v_ref, do_ref, gq_ref, gk_ref = refs[:6]
        pos = 7
        r_ins = refs[pos:pos + host.n_in]; pos += host.n_in
        dq_ref, dk_ref, dv_ref, ggq_ref, ggk_ref = refs[pos:pos + 5]; pos += 5
        r_outs = refs[pos:pos + host.n_out]; pos += host.n_out
        dkh_ref, dvh_ref = refs[pos:pos + 2]; pos += 2
        sems = refs[pos:]
        h, i = pl.program_id(0), pl.program_id(1)

        def compute():
            qn, rq = _head_rms(q_ref[...].astype(F32))
            kn, rk = _head_rms(k_ref[...])
            qhat = (qn * gq_ref[...]).astype(BF16)
            khat = (kn * gk_ref[...]).astype(BF16)
            vb = v_ref[...].astype(BF16)
            dob = do_ref[...]
            s = _dot(qhat, khat, NT) * scale
            p = jnp.exp(s - jnp.max(s, axis=-1, keepdims=True))
            p = p / jnp.sum(p, axis=-1, keepdims=True)
            dp = _dot(dob, vb, NT)
            ds = p * (dp - jnp.sum(dp * p, axis=-1, keepdims=True))
            dsb = ds.astype(BF16)
            dq, ggq = _head_rms_bwd(_dot(dsb, khat, NN) * scale, qn, rq, gq_ref[...])
            dq_ref[...] = dq.astype(dq_ref.dtype)

            @pl.when(i == 0)
            def _():
                dkh_ref[...] = jnp.zeros_like(dkh_ref)
                dvh_ref[...] = jnp.zeros_like(dvh_ref)

            @pl.when(jnp.logical_and(h == 0, i == 0))
            def _():
                ggq_ref[...] = jnp.zeros_like(ggq_ref)
                ggk_ref[...] = jnp.zeros_like(ggk_ref)

            dkh_ref[...] += _dot(dsb, qhat, TN)
            dvh_ref[...] += _dot(p.astype(BF16), dob, TN)
            ggq_ref[...] += ggq

            @pl.when(i == nq - 1)
            def _():
                dk, ggk = _head_rms_bwd(dkh_ref[...] * scale, kn, rk, gk_ref[...])
                dk_ref[...] = dk.astype(dk_ref.dtype)
                dv_ref[...] = dvh_ref[...].astype(dv_ref.dtype)
                ggk_ref[...] += ggk

        first = jnp.logical_and(h == 0, i == 0)
        last = jnp.logical_and(h == MEM_HEADS - 1, i == nq - 1)
        host.run(first, last, r_ins, r_outs, sems, compute)

    vec = pl.BlockSpec((1, hd), lambda h, i: (0, 0))
    kblk = pl.BlockSpec((m, hd), lambda h, i: (0, h))
    res = pl.pallas_call(
        body,
        name="mem_bwd",
        grid=(MEM_HEADS, nq),
        in_specs=[
            pl.BlockSpec((tq, hd), lambda h, i: (i, blk0 + h)), kblk,
            pl.BlockSpec((m, hd), lambda h, i: (0, MEM_HEADS + h)),
            pl.BlockSpec((tq, hd), lambda h, i: (i, h)), vec, vec, ANY,
        ] + host.in_specs,
        out_specs=[pl.BlockSpec((tq, hd), lambda h, i: (i, blk0 + h)), kblk, kblk, vec, vec] + host.out_specs,
        out_shape=[
            jax.ShapeDtypeStruct(dproj.shape, dproj.dtype),
            jax.ShapeDtypeStruct((m, width), BF16),
            jax.ShapeDtypeStruct((m, width), BF16),
            jax.ShapeDtypeStruct((1, hd), F32),
            jax.ShapeDtypeStruct((1, hd), F32),
        ] + host.out_shapes,
        input_output_aliases={6: 0},
        scratch_shapes=[pltpu.VMEM((m, hd), F32), pltpu.VMEM((m, hd), F32)] + host.scratch,
        compiler_params=_params(("arbitrary", "arbitrary")),
    )(proj, kv, kv, do, gq.reshape(1, hd), gk.reshape(1, hd), dproj, *host.ins)
    dproj, dk, dv, ggq, ggk = res[:5]
    return (dproj, jnp.concatenate([dk, dv], axis=1), ggq.reshape(hd), ggk.reshape(hd), *res[5:])


def _sigmoid(z):
    return 1.0 / (1.0 + jnp.exp(-z))


def _merge_fwd(proj, o3, tm, tc):
    t, d = o3[0].shape
    tm = _tile(t, tm)

    def body(g_ref, oa_ref, ob_ref, oc_ref, out_ref):
        acc = jnp.zeros((tm, tc), F32)
        for s, o_ref in enumerate((oa_ref, ob_ref, oc_ref)):
            acc = acc + _sigmoid(g_ref[:, s * tc:(s + 1) * tc].astype(F32)) * o_ref[...].astype(F32)
        out_ref[...] = acc.astype(out_ref.dtype)

    blk = pl.BlockSpec((tm, tc), lambda i, j: (i, j))
    return pl.pallas_call(
        body,
        name="merge_fwd",
        grid=(t // tm, d // tc),
        in_specs=[pl.BlockSpec((tm, 3 * tc), lambda i, j: (i, j)), blk, blk, blk],
        out_specs=blk,
        out_shape=jax.ShapeDtypeStruct((t, d), BF16),
        compiler_params=_params(("parallel", "parallel")),
    )(proj, *o3)


def _merge_bwd(proj, o3, dm, tm, tc):
    t, d = dm.shape
    tm = _tile(t, tm)

    def body(g_ref, oa_ref, ob_ref, oc_ref, dm_ref, dg_ref, da_ref, db_ref, dc_ref):
        dmf = dm_ref[...].astype(F32)
        for s, (o_ref, do_ref) in enumerate(((oa_ref, da_ref), (ob_ref, db_ref), (oc_ref, dc_ref))):
            g = _sigmoid(g_ref[:, s * tc:(s + 1) * tc].astype(F32))
            do_ref[...] = (dmf * g).astype(do_ref.dtype)
            dg_ref[:, s * tc:(s + 1) * tc] = (dmf * o_ref[...].astype(F32) * g * (1.0 - g)).astype(dg_ref.dtype)

    blk = pl.BlockSpec((tm, tc), lambda i, j: (i, j))
    wide = pl.BlockSpec((tm, 3 * tc), lambda i, j: (i, j))
    return pl.pallas_call(
        body,
        name="merge_bwd",
        grid=(t // tm, d // tc),
        in_specs=[wide, blk, blk, blk, blk],
        out_specs=[wide, blk, blk, blk],
        out_shape=[jax.ShapeDtypeStruct(proj.shape, BF16)] + [jax.ShapeDtypeStruct((t, d), BF16)] * 3,
        compiler_params=_params(("parallel", "parallel")),
    )(proj, *o3, dm)


def _loss(dy, d):
    t = dy.shape[0]
    tm = _tile(t, 512)

    def body(dy_ref, out_ref):
        i = pl.program_id(0)

        @pl.when(i == 0)
        def _():
            out_ref[...] = jnp.zeros_like(out_ref)

        e = dy_ref[...]
        out_ref[...] += jnp.sum(jnp.sum(e * e, axis=0, keepdims=True), axis=1, keepdims=True) * (0.5 * d)

    out = pl.pallas_call(
        body,
        name="loss",
        grid=(t // tm,),
        in_specs=[pl.BlockSpec((tm, dy.shape[1]), lambda i: (i, 0))],
        out_specs=pl.BlockSpec((1, 1), lambda i: (0, 0)),
        out_shape=jax.ShapeDtypeStruct((1, 1), F32),
        compiler_params=_params(("arbitrary",)),
    )(dy)
    return out[0, 0]


def _w_in_chunks(d, tc):
    cw = d // 2
    heads = cw // FOX_HEAD_DIM
    conv0, fox0, f0, mq0, gate0 = 0, 3 * cw, 6 * cw, 6 * cw + heads, 7 * cw + heads
    chunks = [(gate0 + s * d + j * tc, gate0 + s * d + (j + 1) * tc) for j in range(d // tc) for s in range(N_BRANCHES)]
    chunks += [(conv0 + s * cw + j * LANES, conv0 + s * cw + (j + 1) * LANES) for j in range(cw // LANES) for s in range(3)]
    chunks += [(fox0 + s * cw + j * FOX_HEAD_DIM, fox0 + s * cw + (j + 1) * FOX_HEAD_DIM) for j in range(heads) for s in range(3)]
    chunks.append((mq0, mq0 + cw))
    return chunks, (f0, f0 + heads)


def _row_tiles(a):
    return a.reshape(a.shape[:-1] + (a.shape[-1] // LANES, LANES))


def _move_rows(name, srcs, out_shapes, moves):
    n_in, n_out = len(srcs), len(out_shapes)

    def body(*refs):
        ins, outs, sems = refs[:n_in], refs[n_in:n_in + n_out], refs[n_in + n_out]
        for s, s_idx, s_row, d, d_idx, d_row, rows in moves:
            src = ins[s].at[s_idx + (pl.ds(s_row, rows),)] if s_idx else ins[s].at[pl.ds(s_row, rows)]
            dst = outs[d].at[d_idx + (pl.ds(d_row, rows),)] if d_idx else outs[d].at[pl.ds(d_row, rows)]
            pltpu.make_async_copy(src, dst, sems.at[d]).start()
        for d in range(n_out):
            pltpu.make_async_copy(outs[d], outs[d], sems.at[d]).wait()

    return pl.pallas_call(
        body, name=name, in_specs=[ANY] * n_in, out_specs=[ANY] * n_out, out_shape=out_shapes,
        scratch_shapes=[pltpu.SemaphoreType.DMA((n_out,))])(*srcs)


def _pack_w_in(w8, d, tc):
    r = w8.shape[1]
    k = w8.shape[2] * w8.shape[3]
    chunks, (f_lo, f_hi) = _w_in_chunks(d, tc)
    moves, pos = [], 0
    for dst, ranges in ((0, chunks), (1, [(f_lo, f_hi)])):
        pos = 0
        for lo, hi in ranges:
            for b in range(lo // r, (hi - 1) // r + 1):
                a, e = max(lo, b * r), min(hi, (b + 1) * r)
                moves.append((0, (b,), a - b * r, dst, (), pos + a - lo, e - a))
            pos += hi - lo
    n_all = sum(hi - lo for lo, hi in chunks)
    tiles = w8.shape[2:]
    w_all, f = _move_rows("pack_w_in", [w8], [jax.ShapeDtypeStruct((n_all,) + tiles, w8.dtype),
                                               jax.ShapeDtypeStruct((f_hi - f_lo,) + tiles, w8.dtype)], moves)
    return w_all.reshape(-1, k), jnp.pad(f.reshape(-1, k), ((0, F_ROWS - (f_hi - f_lo)), (0, 0)))


def _unpack_g_in(g_all, g_f, d, tc, blocks):
    g_all, g_f = _row_tiles(g_all), _row_tiles(g_f)
    chunks, (f_lo, f_hi) = _w_in_chunks(d, tc)
    r = max(hi for _, hi in chunks) // blocks
    moves, pos = [], 0
    for src, ranges in ((0, chunks), (1, [(f_lo, f_hi)])):
        pos = 0
        for lo, hi in ranges:
            for b in range(lo // r, (hi - 1) // r + 1):
                a, e = max(lo, b * r), min(hi, (b + 1) * r)
                moves.append((src, (), pos + a - lo, 0, (b,), a - b * r, e - a))
            pos += hi - lo
    return _move_rows("unpack_g_in", [g_all, g_f], [jax.ShapeDtypeStruct((blocks, r) + g_all.shape[1:], g_all.dtype)], moves)[0]


def _unblock(w8):
    return w8.transpose(1, 0, 2).reshape(w8.shape[1], -1)


def _tile2(r, cols, tr, tcols):
    if r % 8 == 0:
        return _tile(r, tr), cols
    return r, _tile(cols, tcols)


def _pair_sum(name, g8, got, c):
    def body(c_ref, g_ref, s_ref, o_ref):
        o_ref[...] = (g_ref[...].astype(F32) + s_ref[...].astype(F32)).astype(o_ref.dtype)

    if g8.ndim == 4:
        _, r, k1, k2 = g8.shape
        tr = max(cand for cand in range(1, 385) if r % cand == 0)
        grid = (N_CHIPS, r // tr)
        shape = (None, tr, k1, k2)
        own = pl.BlockSpec(shape, lambda q, i, c_ref: (2 * q + c_ref[0], i, 0, 0))
        blk = pl.BlockSpec(shape, lambda q, i, c_ref: (q, i, 0, 0))
    else:
        _, r, cols = g8.shape
        tr, tcols = _tile2(r, cols, 256, 256)
        grid = (N_CHIPS, r // tr, cols // tcols)
        own = pl.BlockSpec((None, tr, tcols), lambda q, i, j, c_ref: (2 * q + c_ref[0], i, j))
        blk = pl.BlockSpec((None, tr, tcols), lambda q, i, j, c_ref: (q, i, j))
    return pl.pallas_call(
        body,
        name=name,
        grid_spec=pltpu.PrefetchScalarGridSpec(num_scalar_prefetch=1, grid=grid, in_specs=[own, blk], out_specs=blk),
        out_shape=jax.ShapeDtypeStruct((N_CHIPS,) + g8.shape[1:], BF16),
        compiler_params=_params(("parallel",) * len(grid)),
    )(c, g8, got)


def _local_step(x, mem, target, w, small, comm=None):
    t, d = x.shape
    cw = d // 2
    heads = cw // FOX_HEAD_DIM
    tc = min(512, d)
    tq = min(512, t)
    off_conv, off_fox, off_mq = 3 * d, 3 * d + 3 * cw, 3 * d + 6 * cw
    w = dict(w)
    w_all, w_f = _pack_w_in(w["w_in"], d, tc)
    big = dict(tm=1024, tn=512, tk=2048)
    wide_k = dict(tm=512, tn=1024, tk=4096)

    h = _rms_fwd("rms1_fwd", x, small["norm1_g"])
    if comm:
        early = ("w_conv_out", "w_fox_out", "w_mem_out", "w_out", "w_up")
        proj, *got = _matmul("proj", "nt", h, w_all, outs=[BF16], rider=_gather_rider([comm["shards"][n] for n in early]), **big)
        for n, val in zip(early, got):
            w[n] = val.reshape(-1, val.shape[-1]) if n == "w_out" else _unblock(val)
    else:
        proj = _matmul("proj", "nt", h, w_all, outs=[BF16], **big)
    z_row = _matmul("proj_f", "nt", w_f, h, outs=[F32], tm=F_ROWS, tn=512, tk=2048)

    y_conv = _conv_fwd(proj, off_conv, small["conv_w"], LANES)

    b_col = jnp.pad(small["b_f"], (0, F_ROWS - heads)).reshape(F_ROWS, 1)
    c_row3 = _forget_fwd(z_row, b_col)[:heads].reshape(heads, 1, t)
    c_colb = _rows_to_colb(c_row3, tq)
    if comm:
        y_fox, lse, got = _fox_fwd(proj, off_fox, small["fox_q_g"], small["fox_k_g"], c_row3, c_colb, heads, tq,
                                   rider=_gather_rider([comm["shards"]["w_down"]]))
        w["w_down"] = got.reshape(-1, got.shape[-1])
    else:
        y_fox, lse = _fox_fwd(proj, off_fox, small["fox_q_g"], small["fox_k_g"], c_row3, c_colb, heads, tq)

    nm = _rms_fwd("mem_rms_fwd", mem, small["mem_norm_g"])
    kv = _matmul("mem_kv", "nn", nm, w["w_mem_kv"], outs=[F32], tm=256, tn=512, tk=2048)
    y_mem = _mem_fwd(proj, off_mq, kv, small["mem_q_g"], small["mem_k_g"], tq)

    ys = (y_conv, y_fox, y_mem)
    w_outs = (w["w_conv_out"], w["w_fox_out"], w["w_mem_out"])
    o3 = [_matmul(f"branch_out{s}", "nn", ys[s], w_outs[s], outs=[BF16], **big) for s in range(3)]
    merged = _merge_fwd(proj, o3, 512, tc)
    x1 = _matmul("out_proj", "nn", merged, w["w_out"], outs=[F32], extras=[x],
                 epilogue=lambda acc, xr: (acc + xr,), **big)
    h2 = _rms_fwd("rms2_fwd", x1, small["norm2_g"])

    def up_epilogue(acc):
        return acc, jnp.square(jnp.maximum(acc, 0.0))

    up, act = _matmul("mlp_up", "nn", h2, w["w_up"], outs=[BF16, BF16], epilogue=up_epilogue, **big)

    def loss_epilogue(acc, x1r, tr):
        dy = (acc + x1r - tr) * (1.0 / d)
        return dy, dy

    dy, dyb = _matmul("mlp_down", "nn", act, w["w_down"], outs=[F32, BF16], extras=[x1, target],
                      epilogue=loss_epilogue, **big)

    def dup_epilogue(acc, upr):
        return (acc * 2.0 * jnp.maximum(upr.astype(F32), 0.0),)

    def by_owner(g):
        return g.reshape(N_DEV, -1, g.shape[-1])

    g, parts = {}, {}
    g["w_down"] = _matmul("d_w_down", "tn", act, dyb, outs=[BF16], **wide_k)
    if comm:
        dup, got = _matmul("d_act", "nt", dyb, w["w_down"], outs=[BF16], extras=[up], epilogue=dup_epilogue,
                           rider=_pair_rider([by_owner(g["w_down"])]), **big)
        pair = _pair_sum("pair_w_down", by_owner(g["w_down"]), got, comm["c"])
        g["w_up"], parts["w_down"] = _matmul("d_w_up", "tn", h2, dup, outs=[BF16], out_blocks=True,
                                             rider=_chip_rider([pair]), **wide_k)
        dh2, got = _matmul("d_h2", "nt", dup, w["w_up"], outs=[F32], rider=_pair_rider([g["w_up"]]), **big)
        pair_up = _pair_sum("pair_w_up", g["w_up"], got, comm["c"])
    else:
        dup = _matmul("d_act", "nt", dyb, w["w_down"], outs=[BF16], extras=[up], epilogue=dup_epilogue, **big)
        g["w_up"] = _matmul("d_w_up", "tn", h2, dup, outs=[BF16], out_blocks=True, **wide_k)
        dh2 = _matmul("d_h2", "nt", dup, w["w_up"], outs=[F32], **big)
    dx1, dx1b, g_norm2 = _rms_bwd("rms2_bwd", dh2, x1, small["norm2_g"], res=dy)
    loss = _loss(dy, d)

    g["w_out"] = _matmul("d_w_out", "tn", merged, dx1b, outs=[BF16], **wide_k)
    dmerged = _matmul("d_merged", "nt", dx1b, w["w_out"], outs=[BF16], **big)
    dproj, *do3 = _merge_bwd(proj, o3, dmerged, 512, tc)
    names = ("w_conv_out", "w_fox_out", "w_mem_out")
    dys = []
    for s in range(3):
        g[names[s]] = _matmul(f"d_w_branch{s}", "tn", ys[s], do3[s], outs=[BF16], out_blocks=True, **wide_k)
        dys.append(_matmul(f"d_branch{s}", "nt", do3[s], w_outs[s], outs=[BF16], **big))

    dproj, dkv, g_mq, g_mk = _mem_bwd(proj, off_mq, kv, dys[2], small["mem_q_g"], small["mem_k_g"], tq, dproj)
    g["w_mem_kv"] = _matmul("d_w_mem_kv", "tn", nm, dkv, outs=[BF16], **wide_k)
    dnm = _matmul("d_mem_norm", "nt", dkv, w["w_mem_kv"], outs=[F32], tm=256, tn=512, tk=2048)
    _, _, g_mem_norm = _rms_bwd("mem_rms_bwd", dnm, mem, small["mem_norm_g"])

    mid = ("w_out", "w_conv_out", "w_fox_out", "w_mem_out", "w_mem_kv")
    if comm:
        mid8 = [g[n] if n in names else by_owner(g[n]) for n in mid]
        dproj, g_conv_w, *got = _conv_bwd(proj, off_conv, small["conv_w"], dys[0], LANES, dproj, rider=_pair_rider(mid8))
        pairs = [pair_up] + [_pair_sum("pair_" + n, g8, s4, comm["c"]) for n, g8, s4 in zip(mid, mid8, got)]
        dproj, dc, g_fq, g_fk, *got = _fox_bwd(proj, off_fox, y_fox, dys[1], small["fox_q_g"], small["fox_k_g"], c_row3, c_colb,
                                               lse, heads, tq, dproj, rider=_chip_rider(pairs))
        parts.update(zip(("w_up",) + mid, got))
    else:
        dproj, g_conv_w = _conv_bwd(proj, off_conv, small["conv_w"], dys[0], LANES, dproj)
        dproj, dc, g_fq, g_fk = _fox_bwd(proj, off_fox, y_fox, dys[1], small["fox_q_g"], small["fox_k_g"], c_row3, c_colb,
                                         lse, heads, tq, dproj)
    dc_row = jnp.pad(dc.reshape(heads, t), ((0, F_ROWS - heads), (0, 0)))
    dz_row, db = _forget_bwd(z_row, b_col, dc_row)

    g_all = _matmul("d_w_in", "tn", dproj, h, outs=[BF16], j_outer=True, **wide_k)
    g_wf = _matmul("d_w_f", "nn", dz_row, h, outs=[BF16], tm=F_ROWS, tn=512, tk=4096)
    g["w_in"] = _unpack_g_in(g_all, g_wf, d, tc, w["w_in"].shape[0])
    dh = _matmul("d_h_f", "tn", dz_row, w_f, outs=[F32], tm=1024, tn=512, tk=F_ROWS)
    add_prev = lambda acc, prev: (acc + prev,)
    if comm:
        g_in8 = g["w_in"]
        got = _run_rider("pair_exchange_w_in", _pair_rider([g_in8]))[0]
        pair = _pair_sum("pair_w_in", g_in8, got, comm["c"])
        dh, parts["w_in"] = _matmul("d_h", "nn", dproj, w_all, outs=[F32], extras=[dh], epilogue=add_prev,
                                    rider=_chip_rider([pair]), tm=1024, tn=512, tk=3328)
    else:
        dh = _matmul("d_h", "nn", dproj, w_all, outs=[F32], extras=[dh], epilogue=add_prev, tm=1024, tn=512, tk=3328)
    grad_x, _, g_norm1 = _rms_bwd("rms1_bwd", dh, x, small["norm1_g"], res=dx1)

    gs = dict(norm1_g=g_norm1, b_f=db[:heads, 0], conv_w=g_conv_w, fox_q_g=g_fq.reshape(-1), fox_k_g=g_fk.reshape(-1),
              mem_norm_g=g_mem_norm, mem_q_g=g_mq, mem_k_g=g_mk, norm2_g=g_norm2)
    return loss, grad_x, (parts if comm else g), gs


def _adamw_math(w, g, m, v):
    m = ADAM_B1 * m + (1.0 - ADAM_B1) * g
    v = ADAM_B2 * v + (1.0 - ADAM_B2) * jnp.square(g)
    m_hat = m / (1.0 - ADAM_B1 ** ADAM_STEP)
    v_hat = v / (1.0 - ADAM_B2 ** ADAM_STEP)
    delta = -ADAM_LR * (m_hat / (jnp.sqrt(v_hat) + ADAM_EPS) + ADAM_WD * w)
    return delta, m, v


def _adamw(name, parts, w, m, v):
    r, c = w.shape
    tr, tc = _tile2(r, c, 128, 256)
    n_parts = parts.shape[0]

    def body(p_ref, w_ref, m_ref, v_ref, g_ref, d_ref, nm_ref, nv_ref):
        g = p_ref[0].astype(F32)
        for s in range(1, n_parts):
            g = g + p_ref[s].astype(F32)
        delta, nm, nv = _adamw_math(w_ref[...], g, m_ref[...], v_ref[...])
        g_ref[...] = g
        d_ref[...] = delta
        nm_ref[...] = nm
        nv_ref[...] = nv

    blk = pl.BlockSpec((tr, tc), lambda i, j: (i, j))
    return pl.pallas_call(
        body,
        name=name,
        grid=(r // tr, c // tc),
        in_specs=[pl.BlockSpec((n_parts, tr, tc), lambda i, j: (0, i, j)), blk, blk, blk],
        out_specs=[blk] * 4,
        out_shape=[jax.ShapeDtypeStruct((r, c), F32)] * 4,
        compiler_params=_params(("parallel", "parallel")),
    )(parts, w, m, v)


def _sum_parts(name, parts):
    n_parts, r, c = parts.shape

    def body(p_ref, o_ref):
        acc = p_ref[0]
        for s in range(1, n_parts):
            acc = acc + p_ref[s]
        o_ref[...] = acc

    return pl.pallas_call(body, name=name, out_shape=jax.ShapeDtypeStruct((r, c), F32))(parts)


BIG = ("w_in", "w_mem_kv", "w_conv_out", "w_fox_out", "w_mem_out", "w_out", "w_up", "w_down")
COLUMN_SPLIT = ("w_in", "w_conv_out", "w_fox_out", "w_mem_out", "w_up")
SMALL = ("norm1_g", "b_f", "conv_w", "fox_q_g", "fox_k_g", "mem_norm_g", "mem_q_g", "mem_k_g", "norm2_g")
WEIGHTS = ("norm1_g", "w_in", "b_f", "conv_w", "fox_q_g", "fox_k_g", "mem_norm_g", "w_mem_kv", "mem_q_g", "mem_k_g",
           "w_conv_out", "w_fox_out", "w_mem_out", "w_out", "norm2_g", "w_up", "w_down")


def _pack(vectors):
    rows = []
    for vec in vectors:
        n = vec.shape[0]
        rows.append(jnp.pad(vec, (0, -n % LANES)).reshape(-1, LANES))
    out = jnp.concatenate(rows, axis=0)
    return jnp.pad(out, ((0, -out.shape[0] % 8), (0, 0)))


def _unpack(packed, sizes):
    out, row = [], 0
    for n in sizes:
        nr = -(-n // LANES)
        out.append(packed[row:row + nr].reshape(-1)[:n])
        row += nr
    return out


def kernel(x, mem, norm1_g, w_in, b_f, conv_w, fox_q_g, fox_k_g, mem_norm_g, w_mem_kv, mem_q_g, mem_k_g, w_conv_out, w_fox_out, w_mem_out, w_out, norm2_g, w_up, w_down, loss_target, m_norm1_g, m_w_in, m_b_f, m_conv_w, m_fox_q_g, m_fox_k_g, m_mem_norm_g, m_w_mem_kv, m_mem_q_g, m_mem_k_g, m_w_conv_out, m_w_fox_out, m_w_mem_out, m_w_out, m_norm2_g, m_w_up, m_w_down, v_norm1_g, v_w_in, v_b_f, v_conv_w, v_fox_q_g, v_fox_k_g, v_mem_norm_g, v_w_mem_kv, v_mem_q_g, v_mem_k_g, v_w_conv_out, v_w_fox_out, v_w_mem_out, v_w_out, v_norm2_g, v_w_up, v_w_down):
    args = dict(locals())
    wts = {n: args[n] for n in WEIGHTS}
    ms = {n: args["m_" + n] for n in WEIGHTS}
    vs = {n: args["v_" + n] for n in WEIGHTS}
    x_pos, y_pos, c_pos = _position()
    me = _index(x_pos, y_pos, c_pos)

    shards = {n: (_row_tiles(wts[n].T.astype(BF16)) if n == "w_in" else wts[n].astype(BF16)) for n in BIG}
    wi, wkv, cw8 = _run_rider("all_gather_first", _gather_rider([shards["w_in"], shards["w_mem_kv"], conv_w]))
    full = {"w_in": wi, "w_mem_kv": wkv.reshape(-1, wkv.shape[-1])}
    small = {n: wts[n] for n in SMALL}
    small["conv_w"] = _unblock(cw8)
    comm = {"shards": shards, "c": c_pos.astype(jnp.int32).reshape(1)}

    loss, grad_x, parts, gs = _local_step(x[0], mem[0], loss_target[0], full, small, comm)

    out_g, out_d, out_m, out_v = {}, {}, {}, {}
    for n in BIG:
        if n == "w_in":
            rows_t = parts[n].reshape(parts[n].shape[:2] + (-1,))
            res = _adamw("adamw_" + n, rows_t, wts[n].T, ms[n].T, vs[n].T)
            out_g[n], out_d[n], out_m[n], out_v[n] = (r.T for r in res)
        else:
            out_g[n], out_d[n], out_m[n], out_v[n] = _adamw("adamw_" + n, parts[n], wts[n], ms[n], vs[n])

    small_sizes = [int(math.prod(gs[n].shape)) for n in SMALL]
    packed = _pack([gs[n].reshape(-1) for n in SMALL])
    gsum = _sum_parts("sum_small", _run_rider("exchange_small", _broadcast_rider([packed]))[0])
    gsmall = dict(zip(SMALL, _unpack(gsum, small_sizes)))
    cols = conv_w.shape[1]
    gsmall["conv_w"] = lax.dynamic_slice(gsmall["conv_w"].reshape(CONV_TAPS, -1), (0, me * cols), (CONV_TAPS, cols)).reshape(-1)
    pg, pw, pm, pv = (_pack([src[n].reshape(-1) for n in SMALL]) for src in (gsmall, wts, ms, vs))
    _, sd, sm, sv = _adamw("adamw_small", pg[None], pw, pm, pv)
    local_sizes = [int(math.prod(wts[n].shape)) for n in SMALL]
    for dst, src in ((out_d, sd), (out_m, sm), (out_v, sv)):
        for n, val in zip(SMALL, _unpack(src, local_sizes)):
            dst[n] = val.reshape(wts[n].shape)
    for n in SMALL:
        out_g[n] = gsmall[n].reshape(wts[n].shape)

    loss = lax.psum(loss, MESH_AXES)
    return (loss, grad_x[None], *[out_g[n] for n in WEIGHTS], *[out_d[n] for n in WEIGHTS],
            *[out_m[n] for n in WEIGHTS], *[out_v[n] for n in WEIGHTS])
```

```python
import math

import jax
import jax.numpy as jnp
from jax import lax
from jax.experimental import pallas as pl
from jax.experimental.pallas import tpu as pltpu

F32 = jnp.float32
BF16 = jnp.bfloat16

EPS = 1e-6
N_DEV = 8
N_CHIPS = 4
FOX_HEAD_DIM = 128
MEM_HEADS = 4
CONV_TAPS = 3
N_BRANCHES = 3
F_ROWS = 16

ADAM_LR = 0.001
ADAM_B1 = 0.9
ADAM_B2 = 0.999
ADAM_EPS = 1e-08
ADAM_WD = 0.01
ADAM_STEP = 10

V7X_VMEM_BYTES = 64 * 1024 * 1024
VMEM_LIMIT = V7X_VMEM_BYTES * 3 // 4
LANES = 128
NEG = -1e30

MESH_AXES = ("x", "y", "c")
MESH = pl.DeviceIdType.MESH
ANY = pl.BlockSpec(memory_space=pl.ANY)

NN = (((1,), (0,)), ((), ()))
NT = (((1,), (1,)), ((), ()))
TN = (((0,), (0,)), ((), ()))


def _params(sem):
    return pltpu.CompilerParams(dimension_semantics=sem, vmem_limit_bytes=VMEM_LIMIT)


def _dot(a, b, dn):
    return lax.dot_general(a, b, dn, preferred_element_type=F32)


def _tile(n, t):
    if n <= t:
        return n
    for cand in range(t - t % LANES, 0, -LANES):
        if n % cand == 0:
            return cand
    raise ValueError((n, t))


class _Rider:
    def __init__(self, ins, out_shapes, sem_shapes, start, finish):
        self.ins, self.out_shapes, self.sem_shapes = list(ins), list(out_shapes), list(sem_shapes)
        self.start, self.finish = start, finish


def _position():
    return lax.axis_index("x"), lax.axis_index("y"), lax.axis_index("c")


def _index(px, py, pc):
    return 4 * px + 2 * py + pc


def _dma_sems(n, per):
    return [pltpu.SemaphoreType.DMA((n, per)), pltpu.SemaphoreType.DMA((n, per)), pltpu.SemaphoreType.DMA((n,))]


def _gather_rider(shards):
    n = len(shards)

    def copies(ins, outs, sems):
        send_sems, recv_sems, local_sems = sems
        x, y, c = _position()
        me, sibling = (x, y, c), (x, y, 1 - c)
        chips = [(1 - x, y), (x, 1 - y), (1 - x, 1 - y)]

        def copy(a, k, block, to, src=None):
            rows = outs[a].at[_index(*block)]
            return pltpu.make_async_remote_copy(
                src_ref=rows if src is None else src, dst_ref=rows,
                send_sem=send_sems.at[a, k], recv_sem=recv_sems.at[a, k], device_id=to, device_id_type=MESH)

        mine = [pltpu.make_async_copy(ins[a], outs[a].at[_index(*me)], local_sems.at[a]) for a in range(n)]
        first = []
        for a in range(n):
            first.append(copy(a, 0, me, sibling, src=ins[a]))
            first += [copy(a, 1 + j, me, (*chip, c), src=ins[a]) for j, chip in enumerate(chips)]
        return copy, mine, first, me, sibling, chips, c

    def start(ins, outs, sems):
        _, mine, first, *_ = copies(ins, outs, sems)
        for cp in mine + first:
            cp.start()

    def finish(ins, outs, sems):
        copy, mine, first, me, sibling, chips, c = copies(ins, outs, sems)
        passed = []
        for a in range(n):
            for j, chip in enumerate(chips):
                copy(a, 1 + j, (*chip, c), me).wait_recv()
                fwd = copy(a, 4 + j, (*chip, c), sibling)
                fwd.start()
                passed.append(fwd)
        for a in range(n):
            copy(a, 0, sibling, me).wait_recv()
            for j, chip in enumerate(chips):
                copy(a, 4 + j, (*chip, 1 - c), me).wait_recv()
        for cp in first + passed:
            cp.wait_send()
        for cp in mine:
            cp.wait()

    out_shapes = [jax.ShapeDtypeStruct((N_DEV,) + s.shape, s.dtype) for s in shards]
    return _Rider(shards, out_shapes, _dma_sems(n, 7), start, finish)


def _pair_rider(grads):
    n = len(grads)

    def copies(ins, outs, sems):
        send_sems, recv_sems, _ = sems
        x, y, c = _position()
        return [pltpu.make_async_remote_copy(
            src_ref=ins[a].at[2 * q + (1 - c)], dst_ref=outs[a].at[q],
            send_sem=send_sems.at[a, q], recv_sem=recv_sems.at[a, q], device_id=(x, y, 1 - c), device_id_type=MESH)
            for a in range(n) for q in range(N_CHIPS)]

    def start(ins, outs, sems):
        for cp in copies(ins, outs, sems):
            cp.start()

    def finish(ins, outs, sems):
        cps = copies(ins, outs, sems)
        for cp in cps:
            cp.wait_recv()
        for cp in cps:
            cp.wait_send()

    out_shapes = [jax.ShapeDtypeStruct((N_CHIPS,) + g.shape[1:], g.dtype) for g in grads]
    return _Rider(grads, out_shapes, _dma_sems(n, N_CHIPS), start, finish)


def _chip_rider(parts):
    n = len(parts)

    def copies(ins, outs, sems):
        send_sems, recv_sems, local_sems = sems
        x, y, c = _position()
        q_me = 2 * x + y
        chips = [(1 - x, y), (x, 1 - y), (1 - x, 1 - y)]
        mine = [pltpu.make_async_copy(ins[a].at[q_me], outs[a].at[q_me], local_sems.at[a]) for a in range(n)]
        sends, arrivals = [], []
        for a in range(n):
            for j, (tx, ty) in enumerate(chips):
                q_t = 2 * tx + ty
                sends.append(pltpu.make_async_remote_copy(
                    src_ref=ins[a].at[q_t], dst_ref=outs[a].at[q_me],
                    send_sem=send_sems.at[a, j], recv_sem=recv_sems.at[a, j], device_id=(tx, ty, c), device_id_type=MESH))
                arrivals.append(pltpu.make_async_remote_copy(
                    src_ref=ins[a].at[q_t], dst_ref=outs[a].at[q_t],
                    send_sem=send_sems.at[a, j], recv_sem=recv_sems.at[a, j], device_id=(tx, ty, c), device_id_type=MESH))
        return mine, sends, arrivals

    def start(ins, outs, sems):
        mine, sends, _ = copies(ins, outs, sems)
        for cp in mine + sends:
            cp.start()

    def finish(ins, outs, sems):
        mine, sends, arrivals = copies(ins, outs, sems)
        for cp in arrivals:
            cp.wait_recv()
        for cp in sends:
            cp.wait_send()
        for cp in mine:
            cp.wait()

    out_shapes = [jax.ShapeDtypeStruct(p.shape, p.dtype) for p in parts]
    return _Rider(parts, out_shapes, _dma_sems(n, 3), start, finish)


def _broadcast_rider(values):
    n = len(values)

    def copies(ins, outs, sems):
        send_sems, recv_sems, local_sems = sems
        x, y, c = _position()
        me = _index(x, y, c)

        def peer(k):
            return (1 - x if k & 4 else x, 1 - y if k & 2 else y, 1 - c if k & 1 else c)

        mine = [pltpu.make_async_copy(ins[a], outs[a].at[me], local_sems.at[a]) for a in range(n)]
        sends, arrivals = [], []
        for a in range(n):
            for k in range(1, N_DEV):
                common = dict(send_sem=send_sems.at[a, k - 1], recv_sem=recv_sems.at[a, k - 1], device_id=peer(k), device_id_type=MESH)
                sends.append(pltpu.make_async_remote_copy(src_ref=ins[a], dst_ref=outs[a].at[me], **common))
                arrivals.append(pltpu.make_async_remote_copy(src_ref=ins[a], dst_ref=outs[a].at[_index(*peer(k))], **common))
        return mine, sends, arrivals

    def start(ins, outs, sems):
        mine, sends, _ = copies(ins, outs, sems)
        for cp in mine + sends:
            cp.start()

    def finish(ins, outs, sems):
        mine, sends, arrivals = copies(ins, outs, sems)
        for cp in arrivals:
            cp.wait_recv()
        for cp in sends:
            cp.wait_send()
        for cp in mine:
            cp.wait()

    out_shapes = [jax.ShapeDtypeStruct((N_DEV,) + v.shape, v.dtype) for v in values]
    return _Rider(values, out_shapes, _dma_sems(n, 7), start, finish)


def _run_rider(name, rider):
    n_in, n_out = len(rider.ins), len(rider.out_shapes)

    def body(*refs):
        ins, outs, sems = refs[:n_in], refs[n_in:n_in + n_out], refs[n_in + n_out:]
        rider.start(ins, outs, sems)
        rider.finish(ins, outs, sems)

    return pl.pallas_call(
        body, name=name, in_specs=[ANY] * n_in, out_specs=[ANY] * n_out, out_shape=rider.out_shapes,
        scratch_shapes=rider.sem_shapes)(*rider.ins)


class _Host:
    def __init__(self, rider):
        self.rider = rider
        self.n_in = len(rider.ins) if rider else 0
        self.n_out = len(rider.out_shapes) if rider else 0
        self.n_sem = len(rider.sem_shapes) if rider else 0
        self.ins = rider.ins if rider else []
        self.in_specs = [ANY] * self.n_in
        self.out_specs = [ANY] * self.n_out
        self.out_shapes = rider.out_shapes if rider else []
        self.scratch = rider.sem_shapes if rider else []

    def run(self, first, last, ins, outs, sems, compute):
        if self.rider is None:
            compute()
            return

        @pl.when(first)
        def _():
            self.rider.start(ins, outs, sems)

        compute()

        @pl.when(last)
        def _():
            self.rider.finish(ins, outs, sems)


def _matmul(name, kind, a, b, *, tm, tn, tk, outs, epilogue=None, extras=(), out_blocks=False, rider=None, j_outer=False):
    if kind == "nn":
        (m, kdim), n = a.shape, b.shape[1]
    elif kind == "nt":
        (m, kdim), n = a.shape, b.shape[0]
    else:
        (kdim, m), n = a.shape, b.shape[1]
    if out_blocks:
        tn = min(tn, n // N_DEV)
    tm, tn, tk = _tile(m, tm), _tile(n, tn), _tile(kdim, tk)
    ni, nj, nk = m // tm, n // tn, kdim // tk

    def spec(shape, fn):
        return pl.BlockSpec(shape, (lambda g0, g1, k: fn(g1, g0, k)) if j_outer else fn)

    a_spec = spec((tk, tm), lambda i, j, k: (k, i)) if kind == "tn" else spec((tm, tk), lambda i, j, k: (i, k))
    b_spec = spec((tn, tk), lambda i, j, k: (j, k)) if kind == "nt" else spec((tk, tn), lambda i, j, k: (k, j))
    dn = {"nn": NN, "nt": NT, "tn": TN}[kind]

    tile_spec = spec((tm, tn), lambda i, j, k: (i, j))
    if out_blocks:
        width = n // N_DEV
        r_out = width // tn
        out_shape = [jax.ShapeDtypeStruct((N_DEV, m, width), dt) for dt in outs]
        out_specs = [spec((None, tm, tn), lambda i, j, k: (j // r_out, i, j % r_out)) for _ in outs]
    else:
        out_shape = [jax.ShapeDtypeStruct((m, n), dt) for dt in outs]
        out_specs = [tile_spec for _ in outs]
    n_ex, n_out = len(extras), len(outs)
    host = _Host(rider)
    n_acc = 1 if nk > 1 else 0

    def body(*refs):
        a_ref, b_ref = refs[0], refs[1]
        pos = 2
        ex_refs = refs[pos:pos + n_ex]; pos += n_ex
        r_ins = refs[pos:pos + host.n_in]; pos += host.n_in
        out_refs = refs[pos:pos + n_out]; pos += n_out
        r_outs = refs[pos:pos + host.n_out]; pos += host.n_out
        acc_ref = refs[pos] if n_acc else None
        sems = refs[pos + n_acc:]
        i, j, k = pl.program_id(1 if j_outer else 0), pl.program_id(0 if j_outer else 1), pl.program_id(2)

        def finish_tile(acc):
            vals = (acc,) if epilogue is None else epilogue(acc, *[e[...] for e in ex_refs])
            for o_ref, v in zip(out_refs, vals):
                o_ref[...] = v.astype(o_ref.dtype)

        def compute():
            part = _dot(a_ref[...], b_ref[...], dn)
            if nk == 1:
                finish_tile(part)
                return

            @pl.when(k == 0)
            def _():
                acc_ref[...] = part

            @pl.when(jnp.logical_and(k > 0, k < nk - 1))
            def _():
                acc_ref[...] += part

            @pl.when(k == nk - 1)
            def _():
                finish_tile(acc_ref[...] + part)

        first = jnp.logical_and(jnp.logical_and(i == 0, j == 0), k == 0)
        last = jnp.logical_and(jnp.logical_and(i == ni - 1, j == nj - 1), k == nk - 1)
        host.run(first, last, r_ins, r_outs, sems, compute)

    sem = ("arbitrary",) * 3 if rider else ("parallel", "parallel", "arbitrary")
    res = pl.pallas_call(
        body,
        name=name,
        grid=(nj, ni, nk) if j_outer else (ni, nj, nk),
        in_specs=[a_spec, b_spec] + [tile_spec for _ in extras] + host.in_specs,
        out_specs=out_specs + host.out_specs,
        out_shape=out_shape + host.out_shapes,
        scratch_shapes=([pltpu.VMEM((tm, tn), F32)] if n_acc else []) + host.scratch,
        compiler_params=_params(sem),
    )(a, b, *extras, *host.ins)
    return res[0] if len(res) == 1 else res


def _rms_fwd(name, x, g, tm=512):
    t, d = x.shape
    tm = _tile(t, tm)

    def body(x_ref, g_ref, h_ref):
        xf = x_ref[...]
        r = lax.rsqrt(jnp.mean(xf * xf, axis=-1, keepdims=True) + EPS)
        h_ref[...] = (xf * r * g_ref[...]).astype(h_ref.dtype)

    return pl.pallas_call(
        body,
        name=name,
        grid=(t // tm,),
        in_specs=[pl.BlockSpec((tm, d), lambda i: (i, 0)), pl.BlockSpec((1, d), lambda i: (0, 0))],
        out_specs=pl.BlockSpec((tm, d), lambda i: (i, 0)),
        out_shape=jax.ShapeDtypeStruct((t, d), BF16),
        compiler_params=_params(("parallel",)),
    )(x, g.reshape(1, d))


def _rms_bwd(name, dh, x, g, res=None, tm=256):
    t, d = x.shape
    tm = _tile(t, tm)
    has_res = res is not None

    def body(*refs):
        if has_res:
            dh_ref, x_ref, g_ref, res_ref, dx_ref, dxb_ref, gg_ref = refs
        else:
            dh_ref, x_ref, g_ref, dx_ref, dxb_ref, gg_ref = refs
        i = pl.program_id(0)
        xf = x_ref[...]
        r = lax.rsqrt(jnp.mean(xf * xf, axis=-1, keepdims=True) + EPS)
        xh = xf * r
        dhf = dh_ref[...].astype(F32)
        dxh = dhf * g_ref[...]
        dx = r * (dxh - xh * jnp.mean(dxh * xh, axis=-1, keepdims=True))
        if has_res:
            dx = dx + res_ref[...]
        dx_ref[...] = dx
        dxb_ref[...] = dx.astype(BF16)

        @pl.when(i == 0)
        def _():
            gg_ref[...] = jnp.zeros_like(gg_ref)

        gg_ref[...] += jnp.sum(dhf * xh, axis=0, keepdims=True)

    row = pl.BlockSpec((tm, d), lambda i: (i, 0))
    vec = pl.BlockSpec((1, d), lambda i: (0, 0))
    ins = [dh, x, g.reshape(1, d)] + ([res] if has_res else [])
    dx, dxb, gg = pl.pallas_call(
        body,
        name=name,
        grid=(t // tm,),
        in_specs=[row, row, vec] + ([row] if has_res else []),
        out_specs=[row, row, vec],
        out_shape=[jax.ShapeDtypeStruct((t, d), F32), jax.ShapeDtypeStruct((t, d), BF16), jax.ShapeDtypeStruct((1, d), F32)],
        compiler_params=_params(("arbitrary",)),
    )(*ins)
    return dx, dxb, gg.reshape(d)


def _head_rms(xf):
    r = lax.rsqrt(jnp.mean(xf * xf, axis=-1, keepdims=True) + EPS)
    return xf * r, r


def _head_rms_bwd(dy, xn, r, g):
    dxh = dy * g
    dx = r * (dxh - xn * jnp.mean(dxh * xn, axis=-1, keepdims=True))
    return dx, jnp.sum(dy * xn, axis=0, keepdims=True)


def _col_to_row(col):
    n = col.shape[0]
    eye = lax.broadcasted_iota(jnp.int32, (n, n), 0) == lax.broadcasted_iota(jnp.int32, (n, n), 1)
    return jnp.sum(jnp.where(eye, col, 0.0), axis=0, keepdims=True)


def _row_to_col(row):
    n = row.shape[1]
    eye = lax.broadcasted_iota(jnp.int32, (n, n), 0) == lax.broadcasted_iota(jnp.int32, (n, n), 1)
    return jnp.sum(jnp.where(eye, row, 0.0), axis=1, keepdims=True)


def _dproj_args(dproj, n_in):
    if dproj is None:
        return [], [], {}
    return [dproj], [ANY], {n_in: 0}


def _shift_down(u, s, rows):
    return jnp.where(rows >= s, pltpu.roll(u, s, axis=0), 0.0)


def _shift_up(u, s, rows, t):
    return jnp.where(rows < t - s, pltpu.roll(u, t - s, axis=0), 0.0)


def _conv_fwd(proj, off, conv_w, cb):
    t = proj.shape[0]
    c = conv_w.shape[1]
    blk0 = off // (3 * cb)

    def body(p_ref, w_ref, y_ref):
        rows = lax.broadcasted_iota(jnp.int32, (t, cb), 0)
        bg = p_ref[:, 0:cb].astype(F32)
        u = p_ref[:, cb:2 * cb].astype(F32) * p_ref[:, 2 * cb:3 * cb].astype(F32)
        w = w_ref[...]
        conv = w[2:3] * u + w[1:2] * _shift_down(u, 1, rows) + w[0:1] * _shift_down(u, 2, rows)
        y_ref[...] = (bg * conv).astype(y_ref.dtype)

    return pl.pallas_call(
        body,
        name="conv_fwd",
        grid=(c // cb,),
        in_specs=[pl.BlockSpec((t, 3 * cb), lambda j: (0, blk0 + j)), pl.BlockSpec((CONV_TAPS, cb), lambda j: (0, j))],
        out_specs=pl.BlockSpec((t, cb), lambda j: (0, j)),
        out_shape=jax.ShapeDtypeStruct((t, c), BF16),
        compiler_params=_params(("parallel",)),
    )(proj, conv_w)


def _conv_bwd(proj, off, conv_w, dy, cb, dproj, rider=None):
    t = proj.shape[0]
    c = conv_w.shape[1]
    blk0 = off // (3 * cb)
    nj = c // cb
    host = _Host(rider)

    def body(*refs):
        p_ref, w_ref, dy_ref = refs[:3]
        r_ins = refs[4:4 + host.n_in]
        dp_ref, gw_ref = refs[4 + host.n_in:6 + host.n_in]
        r_outs = refs[6 + host.n_in:6 + host.n_in + host.n_out]
        sems = refs[6 + host.n_in + host.n_out:]
        j = pl.program_id(0)

        def compute():
            rows = lax.broadcasted_iota(jnp.int32, (t, cb), 0)
            bg = p_ref[:, 0:cb].astype(F32)
            cg = p_ref[:, cb:2 * cb].astype(F32)
            v = p_ref[:, 2 * cb:3 * cb].astype(F32)
            u = cg * v
            w = w_ref[...]
            u1 = _shift_down(u, 1, rows)
            u2 = _shift_down(u, 2, rows)
            conv = w[2:3] * u + w[1:2] * u1 + w[0:1] * u2
            dyf = dy_ref[...].astype(F32)
            dconv = dyf * bg
            du = w[2:3] * dconv + w[1:2] * _shift_up(dconv, 1, rows, t) + w[0:1] * _shift_up(dconv, 2, rows, t)
            dp_ref[:, 0:cb] = (dyf * conv).astype(dp_ref.dtype)
            dp_ref[:, cb:2 * cb] = (du * v).astype(dp_ref.dtype)
            dp_ref[:, 2 * cb:3 * cb] = (du * cg).astype(dp_ref.dtype)
            gw_ref[0:1, :] = jnp.sum(dconv * u2, axis=0, keepdims=True)
            gw_ref[1:2, :] = jnp.sum(dconv * u1, axis=0, keepdims=True)
            gw_ref[2:3, :] = jnp.sum(dconv * u, axis=0, keepdims=True)

        host.run(j == 0, j == nj - 1, r_ins, r_outs, sems, compute)

    res = pl.pallas_call(
        body,
        name="conv_bwd",
        grid=(nj,),
        in_specs=[
            pl.BlockSpec((t, 3 * cb), lambda j: (0, blk0 + j)),
            pl.BlockSpec((CONV_TAPS, cb), lambda j: (0, j)),
            pl.BlockSpec((t, cb), lambda j: (0, j)),
            ANY,
        ] + host.in_specs,
        out_specs=[pl.BlockSpec((t, 3 * cb), lambda j: (0, blk0 + j)), pl.BlockSpec((CONV_TAPS, cb), lambda j: (0, j))] + host.out_specs,
        out_shape=[jax.ShapeDtypeStruct(dproj.shape, dproj.dtype), jax.ShapeDtypeStruct((CONV_TAPS, c), F32)] + host.out_shapes,
        input_output_aliases={3: 0},
        scratch_shapes=host.scratch,
        compiler_params=_params(("arbitrary",)),
    )(proj, conv_w, dy, dproj, *host.ins)
    return res


def _lane_scan(x, reverse):
    lane = lax.broadcasted_iota(jnp.int32, x.shape, 1)
    s = 1
    while s < LANES:
        if reverse:
            x = x + jnp.where(lane < LANES - s, pltpu.roll(x, LANES - s, axis=1), 0.0)
        else:
            x = x + jnp.where(lane >= s, pltpu.roll(x, s, axis=1), 0.0)
        s *= 2
    return x


def _scan_rows(src_ref, dst_ref, t, reverse, fn=None):
    groups = list(range(t // LANES))
    if reverse:
        groups = groups[::-1]
    carry = None
    for gi in groups:
        sl = slice(gi * LANES, (gi + 1) * LANES)
        blk = src_ref[:, sl]
        if fn is not None:
            blk = fn(blk)
        blk = _lane_scan(blk, reverse)
        if carry is not None:
            blk = blk + carry
        dst_ref[:, sl] = blk
        carry = blk[:, 0:1] if reverse else blk[:, LANES - 1:LANES]


def _forget_fwd(z_row, b_col):
    rows, t = z_row.shape

    def body(z_ref, b_ref, c_ref):
        def logf(z):
            zz = z + b_ref[...]
            return jnp.minimum(zz, 0.0) - jnp.log(1.0 + jnp.exp(-jnp.abs(zz)))

        _scan_rows(z_ref, c_ref, t, False, logf)

    return pl.pallas_call(
        body,
        name="forget_fwd",
        out_shape=jax.ShapeDtypeStruct((rows, t), F32),
        compiler_params=pltpu.CompilerParams(vmem_limit_bytes=VMEM_LIMIT),
    )(z_row, b_col)


def _rows_to_colb(c_row3, tq):
    heads, _, t = c_row3.shape

    def body(r_ref, o_ref):
        o_ref[...] = jnp.broadcast_to(_row_to_col(r_ref[...]), (tq, LANES))

    return pl.pallas_call(
        body,
        name="rows_to_colb",
        grid=(heads, t // tq),
        in_specs=[pl.BlockSpec((None, 1, tq), lambda h, i: (h, 0, i))],
        out_specs=pl.BlockSpec((None, tq, LANES), lambda h, i: (h, i, 0)),
        out_shape=jax.ShapeDtypeStruct((heads, t, LANES), F32),
        compiler_params=_params(("parallel", "parallel")),
    )(c_row3)


def _forget_bwd(z_row, b_col, dc_row):
    rows, t = z_row.shape

    def body(z_ref, b_ref, dc_ref, dz_ref, db_ref, tmp_ref):
        _scan_rows(dc_ref, tmp_ref, t, True)
        zz = z_ref[...] + b_ref[...]
        dz = tmp_ref[...] * (1.0 / (1.0 + jnp.exp(zz)))
        dz_ref[...] = dz.astype(dz_ref.dtype)
        db_ref[...] = jnp.sum(dz, axis=1, keepdims=True)

    return pl.pallas_call(
        body,
        name="forget_bwd",
        out_shape=[jax.ShapeDtypeStruct((rows, t), BF16), jax.ShapeDtypeStruct((rows, 1), F32)],
        scratch_shapes=[pltpu.VMEM((rows, t), F32)],
        compiler_params=pltpu.CompilerParams(vmem_limit_bytes=VMEM_LIMIT),
    )(z_row, b_col, dc_row)


def _fox_fwd(proj, off, gq, gk, c_row3, c_colb, heads, tq, rider=None):
    t = proj.shape[0]
    hd = FOX_HEAD_DIM
    tq = _tile(t, tq)
    nq = t // tq
    blk0 = off // hd
    scale = 1.0 / math.sqrt(hd)
    host = _Host(rider)

    def body(*refs):
        q_ref, k_ref, v_ref, gq_ref, gk_ref, crow_ref, ccol_ref = refs[:7]
        r_ins = refs[7:7 + host.n_in]
        o_ref, lse_ref = refs[7 + host.n_in:9 + host.n_in]
        r_outs = refs[9 + host.n_in:9 + host.n_in + host.n_out]
        khat_ref, v_t_ref = refs[9 + host.n_in + host.n_out:11 + host.n_in + host.n_out]
        sems = refs[11 + host.n_in + host.n_out:]
        h, qi = pl.program_id(0), pl.program_id(1)

        def compute():
            eye = (lax.broadcasted_iota(jnp.int32, (hd, hd), 0) == lax.broadcasted_iota(jnp.int32, (hd, hd), 1)).astype(BF16)

            @pl.when(qi == 0)
            def _():
                kn, _ = _head_rms(k_ref[...].astype(F32))
                khat_ref[...] = (kn * gk_ref[...]).astype(BF16)
                v_t_ref[...] = _dot(eye, v_ref[...], NT).astype(BF16)

            qn, _ = _head_rms(q_ref[...].astype(F32))
            qhat = (qn * (gq_ref[...] * scale)).astype(BF16)
            crow = crow_ref[:, pl.ds(pl.multiple_of(qi * tq, tq), tq)]
            above = lax.broadcasted_iota(jnp.int32, (tq, tq), 1) >= lax.broadcasted_iota(jnp.int32, (tq, tq), 0)

            def tile(j, carry, diagonal):
                m, l, acc_t = carry
                ks = pl.multiple_of(j * tq, tq)
                s_t = _dot(khat_ref[pl.ds(ks, tq), :], qhat, NT) - ccol_ref[pl.ds(ks, tq), 0:1]
                if diagonal:
                    s_t = jnp.where(above, s_t, NEG)
                m_new = jnp.maximum(m, jnp.max(s_t, axis=0, keepdims=True) + crow)
                alpha = jnp.exp(m - m_new)
                p_t = jnp.exp(s_t + (crow - m_new))
                l = alpha * l + jnp.sum(p_t, axis=0, keepdims=True)
                acc_t = alpha * acc_t + _dot(v_t_ref[:, pl.ds(ks, tq)], p_t.astype(BF16), NN)
                return m_new, l, acc_t

            init = (jnp.full((1, tq), NEG, F32), jnp.zeros((1, tq), F32), jnp.zeros((hd, tq), F32))
            carry = lax.fori_loop(0, qi, lambda j, c: tile(j, c, False), init)
            m, l, acc_t = tile(qi, carry, True)
            o_ref[...] = _dot((acc_t / l).astype(BF16), eye, TN).astype(o_ref.dtype)
            lse_ref[...] = m + jnp.log(l)

        first = jnp.logical_and(h == 0, qi == 0)
        last = jnp.logical_and(h == heads - 1, qi == nq - 1)
        host.run(first, last, r_ins, r_outs, sems, compute)

    res = pl.pallas_call(
        body,
        name="fox_fwd",
        grid=(heads, nq),
        in_specs=[
            pl.BlockSpec((tq, hd), lambda h, i: (i, blk0 + 3 * h)),
            pl.BlockSpec((t, hd), lambda h, i: (0, blk0 + 3 * h + 1)),
            pl.BlockSpec((t, hd), lambda h, i: (0, blk0 + 3 * h + 2)),
            pl.BlockSpec((1, hd), lambda h, i: (0, 0)),
            pl.BlockSpec((1, hd), lambda h, i: (0, 0)),
            pl.BlockSpec((None, 1, t), lambda h, i: (h, 0, 0)),
            pl.BlockSpec((None, t, LANES), lambda h, i: (h, 0, 0)),
        ] + host.in_specs,
        out_specs=[pl.BlockSpec((tq, hd), lambda h, i: (i, h)), pl.BlockSpec((None, 1, tq), lambda h, i: (h, 0, i))] + host.out_specs,
        out_shape=[jax.ShapeDtypeStruct((t, heads * hd), BF16), jax.ShapeDtypeStruct((heads, 1, t), F32)] + host.out_shapes,
        scratch_shapes=[pltpu.VMEM((t, hd), BF16), pltpu.VMEM((hd, t), BF16)] + host.scratch,
        compiler_params=_params(("arbitrary", "arbitrary")),
    )(proj, proj, proj, gq.reshape(1, hd), gk.reshape(1, hd), c_row3, c_colb, *host.ins)
    return res


def _fox_bwd(proj, off, o, do, gq, gk, c_row3, c_colb, lse, heads, tq, dproj, rider=None):
    t = proj.shape[0]
    hd = FOX_HEAD_DIM
    tq = _tile(t, tq)
    nb = t // tq
    blk0 = off // hd
    scale = 1.0 / math.sqrt(hd)
    host = _Host(rider)
    n_fixed_in = 11

    def body(*refs):
        q_ref, k_ref, v_ref, o_ref, do_ref, gq_ref, gk_ref, crow_ref, ccol_ref, lse_ref = refs[:10]
        pos = n_fixed_in
        r_ins = refs[pos:pos + host.n_in]; pos += host.n_in
        dp_ref, dc_ref, ggq_ref, ggk_ref = refs[pos:pos + 4]; pos += 4
        r_outs = refs[pos:pos + host.n_out]; pos += host.n_out
        qhat_ref, khat_ref, khat_t_ref, dq_t_ref, dk_ref, dcq_ref, dck_ref, delta_ref = refs[pos:pos + 8]; pos += 8
        sems = refs[pos:]
        h = pl.program_id(0)

        def compute():
            qn, rq = _head_rms(q_ref[...].astype(F32))
            qhat_ref[...] = (qn * (gq_ref[...] * scale)).astype(BF16)
            kn, rk = _head_rms(k_ref[...].astype(F32))
            khat_ref[...] = (kn * gk_ref[...]).astype(BF16)
            eye = (lax.broadcasted_iota(jnp.int32, (hd, hd), 0) == lax.broadcasted_iota(jnp.int32, (hd, hd), 1)).astype(BF16)
            khat_t_ref[...] = _dot(eye, khat_ref[...], NT).astype(BF16)
            delta = jnp.sum(do_ref[...].astype(F32) * o_ref[...].astype(F32), axis=-1, keepdims=True)
            for b in range(nb):
                sl = slice(b * tq, (b + 1) * tq)
                delta_ref[:, sl] = _col_to_row(delta[sl, :])
            dq_t_ref[...] = jnp.zeros_like(dq_t_ref)
            dcq_ref[...] = jnp.zeros_like(dcq_ref)
            above = lax.broadcasted_iota(jnp.int32, (tq, tq), 1) >= lax.broadcasted_iota(jnp.int32, (tq, tq), 0)

            def kv_block(j, _):
                ks = pl.multiple_of(j * tq, tq)
                kh = khat_ref[pl.ds(ks, tq), :]
                kh_t = khat_t_ref[:, pl.ds(ks, tq)]
                vv = v_ref[pl.ds(ks, tq), :]
                ccol = ccol_ref[pl.ds(ks, tq), 0:1]

                def q_block(i, carry, diagonal):
                    dk, dv, dck = carry
                    qs = pl.multiple_of(i * tq, tq)
                    qh = qhat_ref[pl.ds(qs, tq), :]
                    dob = do_ref[pl.ds(qs, tq), :]
                    s_t = _dot(kh, qh, NT) + ((crow_ref[:, pl.ds(qs, tq)] - lse_ref[:, pl.ds(qs, tq)]) - ccol)
                    p_t = jnp.exp(s_t)
                    if diagonal:
                        p_t = jnp.where(above, p_t, 0.0)
                    ds_t = p_t * (_dot(vv, dob, NT) - delta_ref[:, pl.ds(qs, tq)])
                    dsb = ds_t.astype(BF16)
                    dv = dv + _dot(p_t.astype(BF16), dob, NN)
                    dk = dk + _dot(dsb, qh, NN)
                    dq_t_ref[:, pl.ds(qs, tq)] += _dot(kh_t, dsb, NN)
                    dcq_ref[:, pl.ds(qs, tq)] += jnp.sum(ds_t, axis=0, keepdims=True)
                    dck = dck + jnp.sum(ds_t, axis=-1, keepdims=True)
                    return dk, dv, dck

                zero = jnp.zeros((tq, hd), F32)
                carry = q_block(j, (zero, zero, jnp.zeros((tq, 1), F32)), True)
                dk, dv, dck = lax.fori_loop(j + 1, nb, lambda i, c: q_block(i, c, False), carry)
                dk_ref[pl.ds(ks, tq), :] = dk
                dp_ref[pl.ds(ks, tq), 2 * hd:3 * hd] = dv.astype(dp_ref.dtype)
                dck_ref[pl.ds(ks, tq), :] = dck
                return 0

            lax.fori_loop(0, nb, kv_block, 0)

            dq, ggq = _head_rms_bwd(dq_t_ref[...].T * scale, qn, rq, gq_ref[...])
            dk, ggk = _head_rms_bwd(dk_ref[...], kn, rk, gk_ref[...])
            dp_ref[:, 0:hd] = dq.astype(dp_ref.dtype)
            dp_ref[:, hd:2 * hd] = dk.astype(dp_ref.dtype)
            for b in range(nb):
                sl = slice(b * tq, (b + 1) * tq)
                dc_ref[:, sl] = dcq_ref[:, sl] - _col_to_row(dck_ref[sl, :])

            @pl.when(h == 0)
            def _():
                ggq_ref[...] = jnp.zeros_like(ggq_ref)
                ggk_ref[...] = jnp.zeros_like(ggk_ref)

            ggq_ref[...] += ggq
            ggk_ref[...] += ggk

        host.run(h == 0, h == heads - 1, r_ins, r_outs, sems, compute)

    head_in = lambda part: pl.BlockSpec((t, hd), lambda h: (0, blk0 + 3 * h + part))
    vec = pl.BlockSpec((1, hd), lambda h: (0, 0))
    colb = pl.BlockSpec((None, t, LANES), lambda h: (h, 0, 0))
    res = pl.pallas_call(
        body,
        name="fox_bwd",
        grid=(heads,),
        in_specs=[
            head_in(0), head_in(1), head_in(2),
            pl.BlockSpec((t, hd), lambda h: (0, h)),
            pl.BlockSpec((t, hd), lambda h: (0, h)),
            vec, vec,
            pl.BlockSpec((None, 1, t), lambda h: (h, 0, 0)),
            colb,
            pl.BlockSpec((None, 1, t), lambda h: (h, 0, 0)),
            ANY,
        ] + host.in_specs,
        out_specs=[
            pl.BlockSpec((t, 3 * hd), lambda h: (0, blk0 // 3 + h)),
            pl.BlockSpec((None, 1, t), lambda h: (h, 0, 0)),
            vec, vec,
        ] + host.out_specs,
        out_shape=[
            jax.ShapeDtypeStruct(dproj.shape, dproj.dtype),
            jax.ShapeDtypeStruct((heads, 1, t), F32),
            jax.ShapeDtypeStruct((1, hd), F32),
            jax.ShapeDtypeStruct((1, hd), F32),
        ] + host.out_shapes,
        input_output_aliases={10: 0},
        scratch_shapes=[
            pltpu.VMEM((t, hd), BF16), pltpu.VMEM((t, hd), BF16), pltpu.VMEM((hd, t), BF16),
            pltpu.VMEM((hd, t), F32), pltpu.VMEM((t, hd), F32),
            pltpu.VMEM((1, t), F32), pltpu.VMEM((t, 1), F32), pltpu.VMEM((1, t), F32),
        ] + host.scratch,
        compiler_params=_params(("arbitrary",)),
    )(proj, proj, proj, o, do, gq.reshape(1, hd), gk.reshape(1, hd), c_row3, c_colb, lse, dproj, *host.ins)
    return res


def _mem_fwd(proj, off, kv, gq, gk, tq):
    t = proj.shape[0]
    m, width = kv.shape[0], kv.shape[1] // 2
    hd = width // MEM_HEADS
    tq = _tile(t, tq)
    blk0 = off // hd
    scale = 1.0 / math.sqrt(hd)

    def body(q_ref, k_ref, v_ref, gq_ref, gk_ref, o_ref):
        qn, _ = _head_rms(q_ref[...].astype(F32))
        kn, _ = _head_rms(k_ref[...])
        s = _dot((qn * gq_ref[...]).astype(BF16), (kn * gk_ref[...]).astype(BF16), NT) * scale
        p = jnp.exp(s - jnp.max(s, axis=-1, keepdims=True))
        p = p / jnp.sum(p, axis=-1, keepdims=True)
        o_ref[...] = _dot(p.astype(BF16), v_ref[...].astype(BF16), NN).astype(o_ref.dtype)

    vec = pl.BlockSpec((1, hd), lambda h, i: (0, 0))
    return pl.pallas_call(
        body,
        name="mem_fwd",
        grid=(MEM_HEADS, t // tq),
        in_specs=[
            pl.BlockSpec((tq, hd), lambda h, i: (i, blk0 + h)),
            pl.BlockSpec((m, hd), lambda h, i: (0, h)),
            pl.BlockSpec((m, hd), lambda h, i: (0, MEM_HEADS + h)),
            vec, vec,
        ],
        out_specs=pl.BlockSpec((tq, hd), lambda h, i: (i, h)),
        out_shape=jax.ShapeDtypeStruct((t, width), BF16),
        compiler_params=_params(("parallel", "parallel")),
    )(proj, kv, kv, gq.reshape(1, hd), gk.reshape(1, hd))


def _mem_bwd(proj, off, kv, do, gq, gk, tq, dproj, rider=None):
    t = proj.shape[0]
    m, width = kv.shape[0], kv.shape[1] // 2
    hd = width // MEM_HEADS
    tq = _tile(t, tq)
    nq = t // tq
    blk0 = off // hd
    scale = 1.0 / math.sqrt(hd)
    host = _Host(rider)

    def body(*refs):
        q_ref, k_ref, v_ref, do_ref, gq_ref, gk_ref = refs[:6]
        pos = 7
        r_ins = refs[pos:pos + host.n_in]; pos += host.n_in
        dq_ref, dk_ref, dv_ref, ggq_ref, ggk_ref = refs[pos:pos + 5]; pos += 5
        r_outs = refs[pos:pos + host.n_out]; pos += host.n_out
        dkh_ref, dvh_ref = refs[pos:pos + 2]; pos += 2
        sems = refs[pos:]
        h, i = pl.program_id(0), pl.program_id(1)

        def compute():
            qn, rq = _head_rms(q_ref[...].astype(F32))
            kn, rk = _head_rms(k_ref[...])
            qhat = (qn * gq_ref[...]).astype(BF16)
            khat = (kn * gk_ref[...]).astype(BF16)
            vb = v_ref[...].astype(BF16)
            dob = do_ref[...]
            s = _dot(qhat, khat, NT) * scale
            p = jnp.exp(s - jnp.max(s, axis=-1, keepdims=True))
            p = p / jnp.sum(p, axis=-1, keepdims=True)
            dp = _dot(dob, vb, NT)
            ds = p * (dp - jnp.sum(dp * p, axis=-1, keepdims=True))
            dsb = ds.astype(BF16)
            dq, ggq = _head_rms_bwd(_dot(dsb, khat, NN) * scale, qn, rq, gq_ref[...])
            dq_ref[...] = dq.astype(dq_ref.dtype)

            @pl.when(i == 0)
            def _():
                dkh_ref[...] = jnp.zeros_like(dkh_ref)
                dvh_ref[...] = jnp.zeros_like(dvh_ref)

            @pl.when(jnp.logical_and(h == 0, i == 0))
            def _():
                ggq_ref[...] = jnp.zeros_like(ggq_ref)
                ggk_ref[...] = jnp.zeros_like(ggk_ref)

            dkh_ref[...] += _dot(dsb, qhat, TN)
            dvh_ref[...] += _dot(p.astype(BF16), dob, TN)
            ggq_ref[...] += ggq

            @pl.when(i == nq - 1)
            def _():
                dk, ggk = _head_rms_bwd(dkh_ref[...] * scale, kn, rk, gk_ref[...])
                dk_ref[...] = dk.astype(dk_ref.dtype)
                dv_ref[...] = dvh_ref[...].astype(dv_ref.dtype)
                ggk_ref[...] += ggk

        first = jnp.logical_and(h == 0, i == 0)
        last = jnp.logical_and(h == MEM_HEADS - 1, i == nq - 1)
        host.run(first, last, r_ins, r_outs, sems, compute)

    vec = pl.BlockSpec((1, hd), lambda h, i: (0, 0))
    kblk = pl.BlockSpec((m, hd), lambda h, i: (0, h))
    res = pl.pallas_call(
        body,
        name="mem_bwd",
        grid=(MEM_HEADS, nq),
        in_specs=[
            pl.BlockSpec((tq, hd), lambda h, i: (i, blk0 + h)), kblk,
            pl.BlockSpec((m, hd), lambda h, i: (0, MEM_HEADS + h)),
            pl.BlockSpec((tq, hd), lambda h, i: (i, h)), vec, vec, ANY,
        ] + host.in_specs,
        out_specs=[pl.BlockSpec((tq, hd), lambda h, i: (i, blk0 + h)), kblk, kblk, vec, vec] + host.out_specs,
        out_shape=[
            jax.ShapeDtypeStruct(dproj.shape, dproj.dtype),
            jax.ShapeDtypeStruct((m, width), BF16),
            jax.ShapeDtypeStruct((m, width), BF16),
            jax.ShapeDtypeStruct((1, hd), F32),
            jax.ShapeDtypeStruct((1, hd), F32),
        ] + host.out_shapes,
        input_output_aliases={6: 0},
        scratch_shapes=[pltpu.VMEM((m, hd), F32), pltpu.VMEM((m, hd), F32)] + host.scratch,
        compiler_params=_params(("arbitrary", "arbitrary")),
    )(proj, kv, kv, do, gq.reshape(1, hd), gk.reshape(1, hd), dproj, *host.ins)
    dproj, dk, dv, ggq, ggk = res[:5]
    return (dproj, jnp.concatenate([dk, dv], axis=1), ggq.reshape(hd), ggk.reshape(hd), *res[5:])


def _sigmoid(z):
    return 1.0 / (1.0 + jnp.exp(-z))


def _merge_fwd(proj, o3, tm, tc):
    t, d = o3[0].shape
    tm = _tile(t, tm)

    def body(g_ref, oa_ref, ob_ref, oc_ref, out_ref):
        acc = jnp.zeros((tm, tc), F32)
        for s, o_ref in enumerate((oa_ref, ob_ref, oc_ref)):
            acc = acc + _sigmoid(g_ref[:, s * tc:(s + 1) * tc].astype(F32)) * o_ref[...].astype(F32)
        out_ref[...] = acc.astype(out_ref.dtype)

    blk = pl.BlockSpec((tm, tc), lambda i, j: (i, j))
    return pl.pallas_call(
        body,
        name="merge_fwd",
        grid=(t // tm, d // tc),
        in_specs=[pl.BlockSpec((tm, 3 * tc), lambda i, j: (i, j)), blk, blk, blk],
        out_specs=blk,
        out_shape=jax.ShapeDtypeStruct((t, d), BF16),
        compiler_params=_params(("parallel", "parallel")),
    )(proj, *o3)


def _merge_bwd(proj, o3, dm, tm, tc):
    t, d = dm.shape
    tm = _tile(t, tm)

    def body(g_ref, oa_ref, ob_ref, oc_ref, dm_ref, dg_ref, da_ref, db_ref, dc_ref):
        dmf = dm_ref[...].astype(F32)
        for s, (o_ref, do_ref) in enumerate(((oa_ref, da_ref), (ob_ref, db_ref), (oc_ref, dc_ref))):
            g = _sigmoid(g_ref[:, s * tc:(s + 1) * tc].astype(F32))
            do_ref[...] = (dmf * g).astype(do_ref.dtype)
            dg_ref[:, s * tc:(s + 1) * tc] = (dmf * o_ref[...].astype(F32) * g * (1.0 - g)).astype(dg_ref.dtype)

    blk = pl.BlockSpec((tm, tc), lambda i, j: (i, j))
    wide = pl.BlockSpec((tm, 3 * tc), lambda i, j: (i, j))
    return pl.pallas_call(
        body,
        name="merge_bwd",
        grid=(t // tm, d // tc),
        in_specs=[wide, blk, blk, blk, blk],
        out_specs=[wide, blk, blk, blk],
        out_shape=[jax.ShapeDtypeStruct(proj.shape, BF16)] + [jax.ShapeDtypeStruct((t, d), BF16)] * 3,
        compiler_params=_params(("parallel", "parallel")),
    )(proj, *o3, dm)


def _loss(dy, d):
    t = dy.shape[0]
    tm = _tile(t, 512)

    def body(dy_ref, out_ref):
        i = pl.program_id(0)

        @pl.when(i == 0)
        def _():
            out_ref[...] = jnp.zeros_like(out_ref)

        e = dy_ref[...]
        out_ref[...] += jnp.sum(jnp.sum(e * e, axis=0, keepdims=True), axis=1, keepdims=True) * (0.5 * d)

    out = pl.pallas_call(
        body,
        name="loss",
        grid=(t // tm,),
        in_specs=[pl.BlockSpec((tm, dy.shape[1]), lambda i: (i, 0))],
        out_specs=pl.BlockSpec((1, 1), lambda i: (0, 0)),
        out_shape=jax.ShapeDtypeStruct((1, 1), F32),
        compiler_params=_params(("arbitrary",)),
    )(dy)
    return out[0, 0]


def _w_in_chunks(d, tc):
    cw = d // 2
    heads = cw // FOX_HEAD_DIM
    conv0, fox0, f0, mq0, gate0 = 0, 3 * cw, 6 * cw, 6 * cw + heads, 7 * cw + heads
    chunks = [(gate0 + s * d + j * tc, gate0 + s * d + (j + 1) * tc) for j in range(d // tc) for s in range(N_BRANCHES)]
    chunks += [(conv0 + s * cw + j * LANES, conv0 + s * cw + (j + 1) * LANES) for j in range(cw // LANES) for s in range(3)]
    chunks += [(fox0 + s * cw + j * FOX_HEAD_DIM, fox0 + s * cw + (j + 1) * FOX_HEAD_DIM) for j in range(heads) for s in range(3)]
    chunks.append((mq0, mq0 + cw))
    return chunks, (f0, f0 + heads)


def _pack_w_in(w_in_t, d, tc):
    cw = d // 2
    heads = cw // FOX_HEAD_DIM
    k = w_in_t.shape[1]
    o = 0
    conv = w_in_t[o:o + 3 * cw]; o += 3 * cw
    fox = w_in_t[o:o + 3 * cw]; o += 3 * cw
    f = w_in_t[o:o + heads]; o += heads
    mq = w_in_t[o:o + cw]; o += cw
    gate = w_in_t[o:o + N_BRANCHES * d]
    conv = conv.reshape(3, cw // LANES, LANES, k).transpose(1, 0, 2, 3).reshape(3 * cw, k)
    fox = fox.reshape(3, heads, FOX_HEAD_DIM, k).transpose(1, 0, 2, 3).reshape(3 * cw, k)
    gate = gate.reshape(N_BRANCHES, d // tc, tc, k).transpose(1, 0, 2, 3).reshape(N_BRANCHES * d, k)
    return jnp.concatenate([gate, conv, fox, mq], axis=0), jnp.pad(f, ((0, F_ROWS - heads), (0, 0)))


def _unpack_g_in(g_all, g_f, d, tc):
    cw = d // 2
    heads = cw // FOX_HEAD_DIM
    k = g_all.shape[1]
    o = 0
    gate = g_all[o:o + N_BRANCHES * d]; o += N_BRANCHES * d
    conv = g_all[o:o + 3 * cw]; o += 3 * cw
    fox = g_all[o:o + 3 * cw]; o += 3 * cw
    mq = g_all[o:o + cw]
    conv = conv.reshape(cw // LANES, 3, LANES, k).transpose(1, 0, 2, 3).reshape(3 * cw, k)
    fox = fox.reshape(heads, 3, FOX_HEAD_DIM, k).transpose(1, 0, 2, 3).reshape(3 * cw, k)
    gate = gate.reshape(d // tc, N_BRANCHES, tc, k).transpose(1, 0, 2, 3).reshape(N_BRANCHES * d, k)
    return jnp.concatenate([conv, fox, g_f[:heads], mq, gate], axis=0)


def _unblock(w8):
    return w8.transpose(1, 0, 2).reshape(w8.shape[1], -1)


def _tile2(r, cols, tr, tcols):
    if r % 8 == 0:
        return _tile(r, tr), cols
    return r, _tile(cols, tcols)


def _pair_sum(name, g8, got, c):
    def body(c_ref, g_ref, s_ref, o_ref):
        o_ref[...] = (g_ref[...].astype(F32) + s_ref[...].astype(F32)).astype(o_ref.dtype)

    if g8.ndim == 4:
        _, r, k1, k2 = g8.shape
        tr = max(cand for cand in range(1, 385) if r % cand == 0)
        grid = (N_CHIPS, r // tr)
        shape = (None, tr, k1, k2)
        own = pl.BlockSpec(shape, lambda q, i, c_ref: (2 * q + c_ref[0], i, 0, 0))
        blk = pl.BlockSpec(shape, lambda q, i, c_ref: (q, i, 0, 0))
    else:
        _, r, cols = g8.shape
        tr, tcols = _tile2(r, cols, 256, 256)
        grid = (N_CHIPS, r // tr, cols // tcols)
        own = pl.BlockSpec((None, tr, tcols), lambda q, i, j, c_ref: (2 * q + c_ref[0], i, j))
        blk = pl.BlockSpec((None, tr, tcols), lambda q, i, j, c_ref: (q, i, j))
    return pl.pallas_call(
        body,
        name=name,
        grid_spec=pltpu.PrefetchScalarGridSpec(num_scalar_prefetch=1, grid=grid, in_specs=[own, blk], out_specs=blk),
        out_shape=jax.ShapeDtypeStruct((N_CHIPS,) + g8.shape[1:], BF16),
        compiler_params=_params(("parallel",) * len(grid)),
    )(c, g8, got)


def _local_step(x, mem, target, w, small, comm=None):
    t, d = x.shape
    cw = d // 2
    heads = cw // FOX_HEAD_DIM
    tc = min(512, d)
    tq = min(512, t)
    off_conv, off_fox, off_mq = 3 * d, 3 * d + 3 * cw, 3 * d + 6 * cw
    w = dict(w)
    w_all, w_f = _pack_w_in(w["w_in"], d, tc)
    big = dict(tm=1024, tn=512, tk=2048)
    wide_k = dict(tm=512, tn=1024, tk=4096)

    h = _rms_fwd("rms1_fwd", x, small["norm1_g"])
    if comm:
        early = ("w_conv_out", "w_fox_out", "w_mem_out", "w_out", "w_up")
        proj, *got = _matmul("proj", "nt", h, w_all, outs=[BF16], rider=_gather_rider([comm["shards"][n] for n in early]), **big)
        for n, val in zip(early, got):
            w[n] = val.reshape(-1, val.shape[-1]) if n == "w_out" else _unblock(val)
    else:
        proj = _matmul("proj", "nt", h, w_all, outs=[BF16], **big)
    z_row = _matmul("proj_f", "nt", w_f, h, outs=[F32], tm=F_ROWS, tn=512, tk=2048)

    y_conv = _conv_fwd(proj, off_conv, small["conv_w"], LANES)

    b_col = jnp.pad(small["b_f"], (0, F_ROWS - heads)).reshape(F_ROWS, 1)
    c_row3 = _forget_fwd(z_row, b_col)[:heads].reshape(heads, 1, t)
    c_colb = _rows_to_colb(c_row3, tq)
    if comm:
        y_fox, lse, got = _fox_fwd(proj, off_fox, small["fox_q_g"], small["fox_k_g"], c_row3, c_colb, heads, tq,
                                   rider=_gather_rider([comm["shards"]["w_down"]]))
        w["w_down"] = got.reshape(-1, got.shape[-1])
    else:
        y_fox, lse = _fox_fwd(proj, off_fox, small["fox_q_g"], small["fox_k_g"], c_row3, c_colb, heads, tq)

    nm = _rms_fwd("mem_rms_fwd", mem, small["mem_norm_g"])
    kv = _matmul("mem_kv", "nn", nm, w["w_mem_kv"], outs=[F32], tm=256, tn=512, tk=2048)
    y_mem = _mem_fwd(proj, off_mq, kv, small["mem_q_g"], small["mem_k_g"], tq)

    ys = (y_conv, y_fox, y_mem)
    w_outs = (w["w_conv_out"], w["w_fox_out"], w["w_mem_out"])
    o3 = [_matmul(f"branch_out{s}", "nn", ys[s], w_outs[s], outs=[BF16], **big) for s in range(3)]
    merged = _merge_fwd(proj, o3, 512, tc)
    x1 = _matmul("out_proj", "nn", merged, w["w_out"], outs=[F32], extras=[x],
                 epilogue=lambda acc, xr: (acc + xr,), **big)
    h2 = _rms_fwd("rms2_fwd", x1, small["norm2_g"])

    def up_epilogue(acc):
        return acc, jnp.square(jnp.maximum(acc, 0.0))

    up, act = _matmul("mlp_up", "nn", h2, w["w_up"], outs=[BF16, BF16], epilogue=up_epilogue, **big)

    def loss_epilogue(acc, x1r, tr):
        dy = (acc + x1r - tr) * (1.0 / d)
        return dy, dy

    dy, dyb = _matmul("mlp_down", "nn", act, w["w_down"], outs=[F32, BF16], extras=[x1, target],
                      epilogue=loss_epilogue, **big)

    def dup_epilogue(acc, upr):
        return (acc * 2.0 * jnp.maximum(upr.astype(F32), 0.0),)

    def by_owner(g):
        return g.reshape(N_DEV, -1, g.shape[-1])

    g, parts = {}, {}
    g["w_down"] = _matmul("d_w_down", "tn", act, dyb, outs=[BF16], **wide_k)
    if comm:
        dup, got = _matmul("d_act", "nt", dyb, w["w_down"], outs=[BF16], extras=[up], epilogue=dup_epilogue,
                           rider=_pair_rider([by_owner(g["w_down"])]), **big)
        pair = _pair_sum("pair_w_down", by_owner(g["w_down"]), got, comm["c"])
        g["w_up"], parts["w_down"] = _matmul("d_w_up", "tn", h2, dup, outs=[BF16], out_blocks=True,
                                             rider=_chip_rider([pair]), **wide_k)
        dh2, got = _matmul("d_h2", "nt", dup, w["w_up"], outs=[F32], rider=_pair_rider([g["w_up"]]), **big)
        pair_up = _pair_sum("pair_w_up", g["w_up"], got, comm["c"])
    else:
        dup = _matmul("d_act", "nt", dyb, w["w_down"], outs=[BF16], extras=[up], epilogue=dup_epilogue, **big)
        g["w_up"] = _matmul("d_w_up", "tn", h2, dup, outs=[BF16], out_blocks=True, **wide_k)
        dh2 = _matmul("d_h2", "nt", dup, w["w_up"], outs=[F32], **big)
    dx1, dx1b, g_norm2 = _rms_bwd("rms2_bwd", dh2, x1, small["norm2_g"], res=dy)
    loss = _loss(dy, d)

    g["w_out"] = _matmul("d_w_out", "tn", merged, dx1b, outs=[BF16], **wide_k)
    dmerged = _matmul("d_merged", "nt", dx1b, w["w_out"], outs=[BF16], **big)
    dproj, *do3 = _merge_bwd(proj, o3, dmerged, 512, tc)
    names = ("w_conv_out", "w_fox_out", "w_mem_out")
    dys = []
    for s in range(3):
        g[names[s]] = _matmul(f"d_w_branch{s}", "tn", ys[s], do3[s], outs=[BF16], out_blocks=True, **wide_k)
        dys.append(_matmul(f"d_branch{s}", "nt", do3[s], w_outs[s], outs=[BF16], **big))

    dproj, dkv, g_mq, g_mk = _mem_bwd(proj, off_mq, kv, dys[2], small["mem_q_g"], small["mem_k_g"], tq, dproj)
    g["w_mem_kv"] = _matmul("d_w_mem_kv", "tn", nm, dkv, outs=[BF16], **wide_k)
    dnm = _matmul("d_mem_norm", "nt", dkv, w["w_mem_kv"], outs=[F32], tm=256, tn=512, tk=2048)
    _, _, g_mem_norm = _rms_bwd("mem_rms_bwd", dnm, mem, small["mem_norm_g"])

    mid = ("w_out", "w_conv_out", "w_fox_out", "w_mem_out", "w_mem_kv")
    if comm:
        mid8 = [g[n] if n in names else by_owner(g[n]) for n in mid]
        dproj, g_conv_w, *got = _conv_bwd(proj, off_conv, small["conv_w"], dys[0], LANES, dproj, rider=_pair_rider(mid8))
        pairs = [pair_up] + [_pair_sum("pair_" + n, g8, s4, comm["c"]) for n, g8, s4 in zip(mid, mid8, got)]
        dproj, dc, g_fq, g_fk, *got = _fox_bwd(proj, off_fox, y_fox, dys[1], small["fox_q_g"], small["fox_k_g"], c_row3, c_colb,
                                               lse, heads, tq, dproj, rider=_chip_rider(pairs))
        parts.update(zip(("w_up",) + mid, got))
    else:
        dproj, g_conv_w = _conv_bwd(proj, off_conv, small["conv_w"], dys[0], LANES, dproj)
        dproj, dc, g_fq, g_fk = _fox_bwd(proj, off_fox, y_fox, dys[1], small["fox_q_g"], small["fox_k_g"], c_row3, c_colb,
                                         lse, heads, tq, dproj)
    dc_row = jnp.pad(dc.reshape(heads, t), ((0, F_ROWS - heads), (0, 0)))
    dz_row, db = _forget_bwd(z_row, b_col, dc_row)

    g_all = _matmul("d_w_in", "tn", dproj, h, outs=[BF16], j_outer=True, **wide_k)
    g_wf = _matmul("d_w_f", "nn", dz_row, h, outs=[BF16], tm=F_ROWS, tn=512, tk=4096)
    g["w_in"] = _unpack_g_in(g_all, g_wf, d, tc)
    dh = _matmul("d_h_f", "tn", dz_row, w_f, outs=[F32], tm=1024, tn=512, tk=F_ROWS)
    add_prev = lambda acc, prev: (acc + prev,)
    if comm:
        g_in8 = by_owner(g["w_in"])
        got = _run_rider("pair_exchange_w_in", _pair_rider([g_in8]))[0]
        pair = _pair_sum("pair_w_in", g_in8, got, comm["c"])
        dh, parts["w_in"] = _matmul("d_h", "nn", dproj, w_all, outs=[F32], extras=[dh], epilogue=add_prev,
                                    rider=_chip_rider([pair]), tm=1024, tn=512, tk=3328)
    else:
        dh = _matmul("d_h", "nn", dproj, w_all, outs=[F32], extras=[dh], epilogue=add_prev, tm=1024, tn=512, tk=3328)
    grad_x, _, g_norm1 = _rms_bwd("rms1_bwd", dh, x, small["norm1_g"], res=dx1)

    gs = dict(norm1_g=g_norm1, b_f=db[:heads, 0], conv_w=g_conv_w, fox_q_g=g_fq.reshape(-1), fox_k_g=g_fk.reshape(-1),
              mem_norm_g=g_mem_norm, mem_q_g=g_mq, mem_k_g=g_mk, norm2_g=g_norm2)
    return loss, grad_x, (parts if comm else g), gs


def _adamw_math(w, g, m, v):
    m = ADAM_B1 * m + (1.0 - ADAM_B1) * g
    v = ADAM_B2 * v + (1.0 - ADAM_B2) * jnp.square(g)
    m_hat = m / (1.0 - ADAM_B1 ** ADAM_STEP)
    v_hat = v / (1.0 - ADAM_B2 ** ADAM_STEP)
    delta = -ADAM_LR * (m_hat / (jnp.sqrt(v_hat) + ADAM_EPS) + ADAM_WD * w)
    return delta, m, v


def _adamw(name, parts, w, m, v):
    r, c = w.shape
    tr, tc = _tile2(r, c, 128, 256)
    n_parts = parts.shape[0]

    def body(p_ref, w_ref, m_ref, v_ref, g_ref, d_ref, nm_ref, nv_ref):
        g = p_ref[0].astype(F32)
        for s in range(1, n_parts):
            g = g + p_ref[s].astype(F32)
        delta, nm, nv = _adamw_math(w_ref[...], g, m_ref[...], v_ref[...])
        g_ref[...] = g
        d_ref[...] = delta
        nm_ref[...] = nm
        nv_ref[...] = nv

    blk = pl.BlockSpec((tr, tc), lambda i, j: (i, j))
    return pl.pallas_call(
        body,
        name=name,
        grid=(r // tr, c // tc),
        in_specs=[pl.BlockSpec((n_parts, tr, tc), lambda i, j: (0, i, j)), blk, blk, blk],
        out_specs=[blk] * 4,
        out_shape=[jax.ShapeDtypeStruct((r, c), F32)] * 4,
        compiler_params=_params(("parallel", "parallel")),
    )(parts, w, m, v)


def _sum_parts(name, parts):
    n_parts, r, c = parts.shape

    def body(p_ref, o_ref):
        acc = p_ref[0]
        for s in range(1, n_parts):
            acc = acc + p_ref[s]
        o_ref[...] = acc

    return pl.pallas_call(body, name=name, out_shape=jax.ShapeDtypeStruct((r, c), F32))(parts)


BIG = ("w_in", "w_mem_kv", "w_conv_out", "w_fox_out", "w_mem_out", "w_out", "w_up", "w_down")
COLUMN_SPLIT = ("w_in", "w_conv_out", "w_fox_out", "w_mem_out", "w_up")
SMALL = ("norm1_g", "b_f", "conv_w", "fox_q_g", "fox_k_g", "mem_norm_g", "mem_q_g", "mem_k_g", "norm2_g")
WEIGHTS = ("norm1_g", "w_in", "b_f", "conv_w", "fox_q_g", "fox_k_g", "mem_norm_g", "w_mem_kv", "mem_q_g", "mem_k_g",
           "w_conv_out", "w_fox_out", "w_mem_out", "w_out", "norm2_g", "w_up", "w_down")


def _pack(vectors):
    rows = []
    for vec in vectors:
        n = vec.shape[0]
        rows.append(jnp.pad(vec, (0, -n % LANES)).reshape(-1, LANES))
    out = jnp.concatenate(rows, axis=0)
    return jnp.pad(out, ((0, -out.shape[0] % 8), (0, 0)))


def _unpack(packed, sizes):
    out, row = [], 0
    for n in sizes:
        nr = -(-n // LANES)
        out.append(packed[row:row + nr].reshape(-1)[:n])
        row += nr
    return out


def kernel(x, mem, norm1_g, w_in, b_f, conv_w, fox_q_g, fox_k_g, mem_norm_g, w_mem_kv, mem_q_g, mem_k_g, w_conv_out, w_fox_out, w_mem_out, w_out, norm2_g, w_up, w_down, loss_target, m_norm1_g, m_w_in, m_b_f, m_conv_w, m_fox_q_g, m_fox_k_g, m_mem_norm_g, m_w_mem_kv, m_mem_q_g, m_mem_k_g, m_w_conv_out, m_w_fox_out, m_w_mem_out, m_w_out, m_norm2_g, m_w_up, m_w_down, v_norm1_g, v_w_in, v_b_f, v_conv_w, v_fox_q_g, v_fox_k_g, v_mem_norm_g, v_w_mem_kv, v_mem_q_g, v_mem_k_g, v_w_conv_out, v_w_fox_out, v_w_mem_out, v_w_out, v_norm2_g, v_w_up, v_w_down):
    args = dict(locals())
    wts = {n: args[n] for n in WEIGHTS}
    ms = {n: args["m_" + n] for n in WEIGHTS}
    vs = {n: args["v_" + n] for n in WEIGHTS}
    x_pos, y_pos, c_pos = _position()
    me = _index(x_pos, y_pos, c_pos)

    shards = {n: (wts[n].T if n == "w_in" else wts[n]).astype(BF16) for n in BIG}
    wi, wkv, cw8 = _run_rider("all_gather_first", _gather_rider([shards["w_in"], shards["w_mem_kv"], conv_w]))
    full = {"w_in": wi.reshape(-1, wi.shape[-1]), "w_mem_kv": wkv.reshape(-1, wkv.shape[-1])}
    small = {n: wts[n] for n in SMALL}
    small["conv_w"] = _unblock(cw8)
    comm = {"shards": shards, "c": c_pos.astype(jnp.int32).reshape(1)}

    loss, grad_x, parts, gs = _local_step(x[0], mem[0], loss_target[0], full, small, comm)

    out_g, out_d, out_m, out_v = {}, {}, {}, {}
    for n in BIG:
        if n == "w_in":
            res = _adamw("adamw_" + n, parts[n], wts[n].T, ms[n].T, vs[n].T)
            out_g[n], out_d[n], out_m[n], out_v[n] = (r.T for r in res)
        else:
            out_g[n], out_d[n], out_m[n], out_v[n] = _adamw("adamw_" + n, parts[n], wts[n], ms[n], vs[n])

    small_sizes = [int(math.prod(gs[n].shape)) for n in SMALL]
    packed = _pack([gs[n].reshape(-1) for n in SMALL])
    gsum = _sum_parts("sum_small", _run_rider("exchange_small", _broadcast_rider([packed]))[0])
    gsmall = dict(zip(SMALL, _unpack(gsum, small_sizes)))
    cols = conv_w.shape[1]
    gsmall["conv_w"] = lax.dynamic_slice(gsmall["conv_w"].reshape(CONV_TAPS, -1), (0, me * cols), (CONV_TAPS, cols)).reshape(-1)
    pg, pw, pm, pv = (_pack([src[n].reshape(-1) for n in SMALL]) for src in (gsmall, wts, ms, vs))
    _, sd, sm, sv = _adamw("adamw_small", pg[None], pw, pm, pv)
    local_sizes = [int(math.prod(wts[n].shape)) for n in SMALL]
    for dst, src in ((out_d, sd), (out_m, sm), (out_v, sv)):
        for n, val in zip(SMALL, _unpack(src, local_sizes)):
            dst[n] = val.reshape(wts[n].shape)
    for n in SMALL:
        out_g[n] = gsmall[n].reshape(wts[n].shape)

    loss = lax.psum(loss, MESH_AXES)
    return (loss, grad_x[None], *[out_g[n] for n in WEIGHTS], *[out_d[n] for n in WEIGHTS],
            *[out_m[n] for n in WEIGHTS], *[out_v[n] for n in WEIGHTS])
```

```python
import math

import jax
import jax.numpy as jnp
from jax import lax
from jax.experimental import pallas as pl
from jax.experimental.pallas import tpu as pltpu

F32 = jnp.float32
BF16 = jnp.bfloat16

EPS = 1e-6
N_DEV = 8
N_CHIPS = 4
FOX_HEAD_DIM = 128
MEM_HEADS = 4
CONV_TAPS = 3
N_BRANCHES = 3
F_ROWS = 16

ADAM_LR = 0.001
ADAM_B1 = 0.9
ADAM_B2 = 0.999
ADAM_EPS = 1e-08
ADAM_WD = 0.01
ADAM_STEP = 10

V7X_VMEM_BYTES = 64 * 1024 * 1024
VMEM_LIMIT = V7X_VMEM_BYTES * 3 // 4
LANES = 128
NEG = -1e30

MESH_AXES = ("x", "y", "c")
MESH = pl.DeviceIdType.MESH
ANY = pl.BlockSpec(memory_space=pl.ANY)

NN = (((1,), (0,)), ((), ()))
NT = (((1,), (1,)), ((), ()))
TN = (((0,), (0,)), ((), ()))


def _params(sem):
    return pltpu.CompilerParams(dimension_semantics=sem, vmem_limit_bytes=VMEM_LIMIT)


def _dot(a, b, dn):
    return lax.dot_general(a, b, dn, preferred_element_type=F32)


def _tile(n, t):
    if n <= t:
        return n
    for cand in range(t - t % LANES, 0, -LANES):
        if n % cand == 0:
            return cand
    raise ValueError((n, t))


class _Rider:
    def __init__(self, ins, out_shapes, sem_shapes, start, finish):
        self.ins, self.out_shapes, self.sem_shapes = list(ins), list(out_shapes), list(sem_shapes)
        self.start, self.finish = start, finish


def _position():
    return lax.axis_index("x"), lax.axis_index("y"), lax.axis_index("c")


def _index(px, py, pc):
    return 4 * px + 2 * py + pc


def _dma_sems(n, per):
    return [pltpu.SemaphoreType.DMA((n, per)), pltpu.SemaphoreType.DMA((n, per)), pltpu.SemaphoreType.DMA((n,))]


def _gather_rider(shards):
    n = len(shards)

    def copies(ins, outs, sems):
        send_sems, recv_sems, local_sems = sems
        x, y, c = _position()
        me, sibling = (x, y, c), (x, y, 1 - c)
        chips = [(1 - x, y), (x, 1 - y), (1 - x, 1 - y)]

        def copy(a, k, block, to, src=None):
            rows = outs[a].at[_index(*block)]
            return pltpu.make_async_remote_copy(
                src_ref=rows if src is None else src, dst_ref=rows,
                send_sem=send_sems.at[a, k], recv_sem=recv_sems.at[a, k], device_id=to, device_id_type=MESH)

        mine = [pltpu.make_async_copy(ins[a], outs[a].at[_index(*me)], local_sems.at[a]) for a in range(n)]
        first = []
        for a in range(n):
            first.append(copy(a, 0, me, sibling, src=ins[a]))
            first += [copy(a, 1 + j, me, (*chip, c), src=ins[a]) for j, chip in enumerate(chips)]
        return copy, mine, first, me, sibling, chips, c

    def start(ins, outs, sems):
        _, mine, first, *_ = copies(ins, outs, sems)
        for cp in mine + first:
            cp.start()

    def finish(ins, outs, sems):
        copy, mine, first, me, sibling, chips, c = copies(ins, outs, sems)
        passed = []
        for a in range(n):
            for j, chip in enumerate(chips):
                copy(a, 1 + j, (*chip, c), me).wait_recv()
                fwd = copy(a, 4 + j, (*chip, c), sibling)
                fwd.start()
                passed.append(fwd)
        for a in range(n):
            copy(a, 0, sibling, me).wait_recv()
            for j, chip in enumerate(chips):
                copy(a, 4 + j, (*chip, 1 - c), me).wait_recv()
        for cp in first + passed:
            cp.wait_send()
        for cp in mine:
            cp.wait()

    out_shapes = [jax.ShapeDtypeStruct((N_DEV,) + s.shape, s.dtype) for s in shards]
    return _Rider(shards, out_shapes, _dma_sems(n, 7), start, finish)


def _pair_rider(grads):
    n = len(grads)

    def copies(ins, outs, sems):
        send_sems, recv_sems, _ = sems
        x, y, c = _position()
        return [pltpu.make_async_remote_copy(
            src_ref=ins[a].at[2 * q + (1 - c)], dst_ref=outs[a].at[q],
            send_sem=send_sems.at[a, q], recv_sem=recv_sems.at[a, q], device_id=(x, y, 1 - c), device_id_type=MESH)
            for a in range(n) for q in range(N_CHIPS)]

    def start(ins, outs, sems):
        for cp in copies(ins, outs, sems):
            cp.start()

    def finish(ins, outs, sems):
        cps = copies(ins, outs, sems)
        for cp in cps:
            cp.wait_recv()
        for cp in cps:
            cp.wait_send()

    out_shapes = [jax.ShapeDtypeStruct((N_CHIPS,) + g.shape[1:], g.dtype) for g in grads]
    return _Rider(grads, out_shapes, _dma_sems(n, N_CHIPS), start, finish)


def _chip_rider(parts):
    n = len(parts)

    def copies(ins, outs, sems):
        send_sems, recv_sems, local_sems = sems
        x, y, c = _position()
        q_me = 2 * x + y
        chips = [(1 - x, y), (x, 1 - y), (1 - x, 1 - y)]
        mine = [pltpu.make_async_copy(ins[a].at[q_me], outs[a].at[q_me], local_sems.at[a]) for a in range(n)]
        sends, arrivals = [], []
        for a in range(n):
            for j, (tx, ty) in enumerate(chips):
                q_t = 2 * tx + ty
                sends.append(pltpu.make_async_remote_copy(
                    src_ref=ins[a].at[q_t], dst_ref=outs[a].at[q_me],
                    send_sem=send_sems.at[a, j], recv_sem=recv_sems.at[a, j], device_id=(tx, ty, c), device_id_type=MESH))
                arrivals.append(pltpu.make_async_remote_copy(
                    src_ref=ins[a].at[q_t], dst_ref=outs[a].at[q_t],
                    send_sem=send_sems.at[a, j], recv_sem=recv_sems.at[a, j], device_id=(tx, ty, c), device_id_type=MESH))
        return mine, sends, arrivals

    def start(ins, outs, sems):
        mine, sends, _ = copies(ins, outs, sems)
        for cp in mine + sends:
            cp.start()

    def finish(ins, outs, sems):
        mine, sends, arrivals = copies(ins, outs, sems)
        for cp in arrivals:
            cp.wait_recv()
        for cp in sends:
            cp.wait_send()
        for cp in mine:
            cp.wait()

    out_shapes = [jax.ShapeDtypeStruct(p.shape, p.dtype) for p in parts]
    return _Rider(parts, out_shapes, _dma_sems(n, 3), start, finish)


def _broadcast_rider(values):
    n = len(values)

    def copies(ins, outs, sems):
        send_sems, recv_sems, local_sems = sems
        x, y, c = _position()
        me = _index(x, y, c)

        def peer(k):
            return (1 - x if k & 4 else x, 1 - y if k & 2 else y, 1 - c if k & 1 else c)

        mine = [pltpu.make_async_copy(ins[a], outs[a].at[me], local_sems.at[a]) for a in range(n)]
        sends, arrivals = [], []
        for a in range(n):
            for k in range(1, N_DEV):
                common = dict(send_sem=send_sems.at[a, k - 1], recv_sem=recv_sems.at[a, k - 1], device_id=peer(k), device_id_type=MESH)
                sends.append(pltpu.make_async_remote_copy(src_ref=ins[a], dst_ref=outs[a].at[me], **common))
                arrivals.append(pltpu.make_async_remote_copy(src_ref=ins[a], dst_ref=outs[a].at[_index(*peer(k))], **common))
        return mine, sends, arrivals

    def start(ins, outs, sems):
        mine, sends, _ = copies(ins, outs, sems)
        for cp in mine + sends:
            cp.start()

    def finish(ins, outs, sems):
        mine, sends, arrivals = copies(ins, outs, sems)
        for cp in arrivals:
            cp.wait_recv()
        for cp in sends:
            cp.wait_send()
        for cp in mine:
            cp.wait()

    out_shapes = [jax.ShapeDtypeStruct((N_DEV,) + v.shape, v.dtype) for v in values]
    return _Rider(values, out_shapes, _dma_sems(n, 7), start, finish)


def _run_rider(name, rider):
    n_in, n_out = len(rider.ins), len(rider.out_shapes)

    def body(*refs):
        ins, outs, sems = refs[:n_in], refs[n_in:n_in + n_out], refs[n_in + n_out:]
        rider.start(ins, outs, sems)
        rider.finish(ins, outs, sems)

    return pl.pallas_call(
        body, name=name, in_specs=[ANY] * n_in, out_specs=[ANY] * n_out, out_shape=rider.out_shapes,
        scratch_shapes=rider.sem_shapes)(*rider.ins)


class _Host:
    def __init__(self, rider):
        self.rider = rider
        self.n_in = len(rider.ins) if rider else 0
        self.n_out = len(rider.out_shapes) if rider else 0
        self.n_sem = len(rider.sem_shapes) if rider else 0
        self.ins = rider.ins if rider else []
        self.in_specs = [ANY] * self.n_in
        self.out_specs = [ANY] * self.n_out
        self.out_shapes = rider.out_shapes if rider else []
        self.scratch = rider.sem_shapes if rider else []

    def run(self, first, last, ins, outs, sems, compute):
        if self.rider is None:
            compute()
            return

        @pl.when(first)
        def _():
            self.rider.start(ins, outs, sems)

        compute()

        @pl.when(last)
        def _():
            self.rider.finish(ins, outs, sems)


def _matmul(name, kind, a, b, *, tm, tn, tk, outs, epilogue=None, extras=(), out_blocks=False, rider=None, j_outer=False):
    if kind == "nn":
        (m, kdim), n = a.shape, b.shape[1]
    elif kind == "nt":
        (m, kdim), n = a.shape, b.shape[0]
    else:
        (kdim, m), n = a.shape, b.shape[1]
    if out_blocks:
        tn = min(tn, n // N_DEV)
    tm, tn, tk = _tile(m, tm), _tile(n, tn), _tile(kdim, tk)
    ni, nj, nk = m // tm, n // tn, kdim // tk

    def spec(shape, fn):
        return pl.BlockSpec(shape, (lambda g0, g1, k: fn(g1, g0, k)) if j_outer else fn)

    a_spec = spec((tk, tm), lambda i, j, k: (k, i)) if kind == "tn" else spec((tm, tk), lambda i, j, k: (i, k))
    b_spec = spec((tn, tk), lambda i, j, k: (j, k)) if kind == "nt" else spec((tk, tn), lambda i, j, k: (k, j))
    dn = {"nn": NN, "nt": NT, "tn": TN}[kind]

    tile_spec = spec((tm, tn), lambda i, j, k: (i, j))
    if out_blocks:
        width = n // N_DEV
        r_out = width // tn
        out_shape = [jax.ShapeDtypeStruct((N_DEV, m, width), dt) for dt in outs]
        out_specs = [spec((None, tm, tn), lambda i, j, k: (j // r_out, i, j % r_out)) for _ in outs]
    else:
        out_shape = [jax.ShapeDtypeStruct((m, n), dt) for dt in outs]
        out_specs = [tile_spec for _ in outs]
    n_ex, n_out = len(extras), len(outs)
    host = _Host(rider)
    n_acc = 1 if nk > 1 else 0

    def body(*refs):
        a_ref, b_ref = refs[0], refs[1]
        pos = 2
        ex_refs = refs[pos:pos + n_ex]; pos += n_ex
        r_ins = refs[pos:pos + host.n_in]; pos += host.n_in
        out_refs = refs[pos:pos + n_out]; pos += n_out
        r_outs = refs[pos:pos + host.n_out]; pos += host.n_out
        acc_ref = refs[pos] if n_acc else None
        sems = refs[pos + n_acc:]
        i, j, k = pl.program_id(1 if j_outer else 0), pl.program_id(0 if j_outer else 1), pl.program_id(2)

        def finish_tile(acc):
            vals = (acc,) if epilogue is None else epilogue(acc, *[e[...] for e in ex_refs])
            for o_ref, v in zip(out_refs, vals):
                o_ref[...] = v.astype(o_ref.dtype)

        def compute():
            part = _dot(a_ref[...], b_ref[...], dn)
            if nk == 1:
                finish_tile(part)
                return

            @pl.when(k == 0)
            def _():
                acc_ref[...] = part

            @pl.when(jnp.logical_and(k > 0, k < nk - 1))
            def _():
                acc_ref[...] += part

            @pl.when(k == nk - 1)
            def _():
                finish_tile(acc_ref[...] + part)

        first = jnp.logical_and(jnp.logical_and(i == 0, j == 0), k == 0)
        last = jnp.logical_and(jnp.logical_and(i == ni - 1, j == nj - 1), k == nk - 1)
        host.run(first, last, r_ins, r_outs, sems, compute)

    sem = ("arbitrary",) * 3 if rider else ("parallel", "parallel", "arbitrary")
    res = pl.pallas_call(
        body,
        name=name,
        grid=(nj, ni, nk) if j_outer else (ni, nj, nk),
        in_specs=[a_spec, b_spec] + [tile_spec for _ in extras] + host.in_specs,
        out_specs=out_specs + host.out_specs,
        out_shape=out_shape + host.out_shapes,
        scratch_shapes=([pltpu.VMEM((tm, tn), F32)] if n_acc else []) + host.scratch,
        compiler_params=_params(sem),
    )(a, b, *extras, *host.ins)
    return res[0] if len(res) == 1 else res


def _rms_fwd(name, x, g, tm=512):
    t, d = x.shape
    tm = _tile(t, tm)

    def body(x_ref, g_ref, h_ref):
        xf = x_ref[...]
        r = lax.rsqrt(jnp.mean(xf * xf, axis=-1, keepdims=True) + EPS)
        h_ref[...] = (xf * r * g_ref[...]).astype(h_ref.dtype)

    return pl.pallas_call(
        body,
        name=name,
        grid=(t // tm,),
        in_specs=[pl.BlockSpec((tm, d), lambda i: (i, 0)), pl.BlockSpec((1, d), lambda i: (0, 0))],
        out_specs=pl.BlockSpec((tm, d), lambda i: (i, 0)),
        out_shape=jax.ShapeDtypeStruct((t, d), BF16),
        compiler_params=_params(("parallel",)),
    )(x, g.reshape(1, d))


def _rms_bwd(name, dh, x, g, res=None, tm=256):
    t, d = x.shape
    tm = _tile(t, tm)
    has_res = res is not None

    def body(*refs):
        if has_res:
            dh_ref, x_ref, g_ref, res_ref, dx_ref, dxb_ref, gg_ref, ss_ref = refs
        else:
            dh_ref, x_ref, g_ref, dx_ref, dxb_ref, gg_ref, ss_ref = refs
        i = pl.program_id(0)
        xf = x_ref[...]
        r = lax.rsqrt(jnp.mean(xf * xf, axis=-1, keepdims=True) + EPS)
        xh = xf * r
        dhf = dh_ref[...].astype(F32)
        dxh = dhf * g_ref[...]
        dx = r * (dxh - xh * jnp.mean(dxh * xh, axis=-1, keepdims=True))

        @pl.when(i == 0)
        def _():
            gg_ref[...] = jnp.zeros_like(gg_ref)
            ss_ref[...] = jnp.zeros_like(ss_ref)

        if has_res:
            resf = res_ref[...]
            dx = dx + resf
            ss_ref[...] += jnp.sum(jnp.sum(resf * resf, axis=0, keepdims=True), axis=1, keepdims=True)
        dx_ref[...] = dx
        dxb_ref[...] = dx.astype(BF16)
        gg_ref[...] += jnp.sum(dhf * xh, axis=0, keepdims=True)

    row = pl.BlockSpec((tm, d), lambda i: (i, 0))
    vec = pl.BlockSpec((1, d), lambda i: (0, 0))
    one = pl.BlockSpec((1, 1), lambda i: (0, 0))
    ins = [dh, x, g.reshape(1, d)] + ([res] if has_res else [])
    dx, dxb, gg, ss = pl.pallas_call(
        body,
        name=name,
        grid=(t // tm,),
        in_specs=[row, row, vec] + ([row] if has_res else []),
        out_specs=[row, row, vec, one],
        out_shape=[jax.ShapeDtypeStruct((t, d), F32), jax.ShapeDtypeStruct((t, d), BF16), jax.ShapeDtypeStruct((1, d), F32),
                   jax.ShapeDtypeStruct((1, 1), F32)],
        compiler_params=_params(("arbitrary",)),
    )(*ins)
    return dx, dxb, gg.reshape(d), ss[0, 0]


def _head_rms(xf):
    r = lax.rsqrt(jnp.mean(xf * xf, axis=-1, keepdims=True) + EPS)
    return xf * r, r


def _head_rms_bwd(dy, xn, r, g):
    dxh = dy * g
    dx = r * (dxh - xn * jnp.mean(dxh * xn, axis=-1, keepdims=True))
    return dx, jnp.sum(dy * xn, axis=0, keepdims=True)


def _col_to_row(col):
    n = col.shape[0]
    eye = lax.broadcasted_iota(jnp.int32, (n, n), 0) == lax.broadcasted_iota(jnp.int32, (n, n), 1)
    return jnp.sum(jnp.where(eye, col, 0.0), axis=0, keepdims=True)


def _row_to_col(row):
    n = row.shape[1]
    eye = lax.broadcasted_iota(jnp.int32, (n, n), 0) == lax.broadcasted_iota(jnp.int32, (n, n), 1)
    return jnp.sum(jnp.where(eye, row, 0.0), axis=1, keepdims=True)


def _dproj_args(dproj, n_in):
    if dproj is None:
        return [], [], {}
    return [dproj], [ANY], {n_in: 0}


def _shift_down(u, s, rows):
    return jnp.where(rows >= s, pltpu.roll(u, s, axis=0), 0.0)


def _shift_up(u, s, rows, t):
    return jnp.where(rows < t - s, pltpu.roll(u, t - s, axis=0), 0.0)


def _conv_fwd(proj, off, conv_w, cb):
    t = proj.shape[0]
    c = conv_w.shape[1]
    blk0 = off // (3 * cb)

    def body(p_ref, w_ref, y_ref):
        rows = lax.broadcasted_iota(jnp.int32, (t, cb), 0)
        bg = p_ref[:, 0:cb].astype(F32)
        u = p_ref[:, cb:2 * cb].astype(F32) * p_ref[:, 2 * cb:3 * cb].astype(F32)
        w = w_ref[...]
        conv = w[2:3] * u + w[1:2] * _shift_down(u, 1, rows) + w[0:1] * _shift_down(u, 2, rows)
        y_ref[...] = (bg * conv).astype(y_ref.dtype)

    return pl.pallas_call(
        body,
        name="conv_fwd",
        grid=(c // cb,),
        in_specs=[pl.BlockSpec((t, 3 * cb), lambda j: (0, blk0 + j)), pl.BlockSpec((CONV_TAPS, cb), lambda j: (0, j))],
        out_specs=pl.BlockSpec((t, cb), lambda j: (0, j)),
        out_shape=jax.ShapeDtypeStruct((t, c), BF16),
        compiler_params=_params(("parallel",)),
    )(proj, conv_w)


def _conv_bwd(proj, off, conv_w, dy, cb, dproj, rider=None):
    t = proj.shape[0]
    c = conv_w.shape[1]
    blk0 = off // (3 * cb)
    nj = c // cb
    host = _Host(rider)

    def body(*refs):
        p_ref, w_ref, dy_ref = refs[:3]
        r_ins = refs[4:4 + host.n_in]
        dp_ref, gw_ref = refs[4 + host.n_in:6 + host.n_in]
        r_outs = refs[6 + host.n_in:6 + host.n_in + host.n_out]
        sems = refs[6 + host.n_in + host.n_out:]
        j = pl.program_id(0)

        def compute():
            rows = lax.broadcasted_iota(jnp.int32, (t, cb), 0)
            bg = p_ref[:, 0:cb].astype(F32)
            cg = p_ref[:, cb:2 * cb].astype(F32)
            v = p_ref[:, 2 * cb:3 * cb].astype(F32)
            u = cg * v
            w = w_ref[...]
            u1 = _shift_down(u, 1, rows)
            u2 = _shift_down(u, 2, rows)
            conv = w[2:3] * u + w[1:2] * u1 + w[0:1] * u2
            dyf = dy_ref[...].astype(F32)
            dconv = dyf * bg
            du = w[2:3] * dconv + w[1:2] * _shift_up(dconv, 1, rows, t) + w[0:1] * _shift_up(dconv, 2, rows, t)
            dp_ref[:, 0:cb] = (dyf * conv).astype(dp_ref.dtype)
            dp_ref[:, cb:2 * cb] = (du * v).astype(dp_ref.dtype)
            dp_ref[:, 2 * cb:3 * cb] = (du * cg).astype(dp_ref.dtype)
            gw_ref[0:1, :] = jnp.sum(dconv * u2, axis=0, keepdims=True)
            gw_ref[1:2, :] = jnp.sum(dconv * u1, axis=0, keepdims=True)
            gw_ref[2:3, :] = jnp.sum(dconv * u, axis=0, keepdims=True)

        host.run(j == 0, j == nj - 1, r_ins, r_outs, sems, compute)

    res = pl.pallas_call(
        body,
        name="conv_bwd",
        grid=(nj,),
        in_specs=[
            pl.BlockSpec((t, 3 * cb), lambda j: (0, blk0 + j)),
            pl.BlockSpec((CONV_TAPS, cb), lambda j: (0, j)),
            pl.BlockSpec((t, cb), lambda j: (0, j)),
            ANY,
        ] + host.in_specs,
        out_specs=[pl.BlockSpec((t, 3 * cb), lambda j: (0, blk0 + j)), pl.BlockSpec((CONV_TAPS, cb), lambda j: (0, j))] + host.out_specs,
        out_shape=[jax.ShapeDtypeStruct(dproj.shape, dproj.dtype), jax.ShapeDtypeStruct((CONV_TAPS, c), F32)] + host.out_shapes,
        input_output_aliases={3: 0},
        scratch_shapes=host.scratch,
        compiler_params=_params(("arbitrary",)),
    )(proj, conv_w, dy, dproj, *host.ins)
    return res


def _lane_scan(x, reverse):
    lane = lax.broadcasted_iota(jnp.int32, x.shape, 1)
    s = 1
    while s < LANES:
        if reverse:
            x = x + jnp.where(lane < LANES - s, pltpu.roll(x, LANES - s, axis=1), 0.0)
        else:
            x = x + jnp.where(lane >= s, pltpu.roll(x, s, axis=1), 0.0)
        s *= 2
    return x


def _scan_rows(src_ref, dst_ref, t, reverse, fn=None):
    groups = list(range(t // LANES))
    if reverse:
        groups = groups[::-1]
    carry = None
    for gi in groups:
        sl = slice(gi * LANES, (gi + 1) * LANES)
        blk = src_ref[:, sl]
        if fn is not None:
            blk = fn(blk)
        blk = _lane_scan(blk, reverse)
        if carry is not None:
            blk = blk + carry
        dst_ref[:, sl] = blk
        carry = blk[:, 0:1] if reverse else blk[:, LANES - 1:LANES]


def _forget_fwd(z_row, b_col):
    rows, t = z_row.shape

    def body(z_ref, b_ref, c_ref):
        def logf(z):
            zz = z + b_ref[...]
            return jnp.minimum(zz, 0.0) - jnp.log(1.0 + jnp.exp(-jnp.abs(zz)))

        _scan_rows(z_ref, c_ref, t, False, logf)

    return pl.pallas_call(
        body,
        name="forget_fwd",
        out_shape=jax.ShapeDtypeStruct((rows, t), F32),
        compiler_params=pltpu.CompilerParams(vmem_limit_bytes=VMEM_LIMIT),
    )(z_row, b_col)


def _rows_to_colb(c_row3, tq):
    heads, _, t = c_row3.shape

    def body(r_ref, o_ref):
        o_ref[...] = jnp.broadcast_to(_row_to_col(r_ref[...]), (tq, LANES))

    return pl.pallas_call(
        body,
        name="rows_to_colb",
        grid=(heads, t // tq),
        in_specs=[pl.BlockSpec((None, 1, tq), lambda h, i: (h, 0, i))],
        out_specs=pl.BlockSpec((None, tq, LANES), lambda h, i: (h, i, 0)),
        out_shape=jax.ShapeDtypeStruct((heads, t, LANES), F32),
        compiler_params=_params(("parallel", "parallel")),
    )(c_row3)


def _forget_bwd(z_row, b_col, dc_row):
    rows, t = z_row.shape

    def body(z_ref, b_ref, dc_ref, dz_ref, db_ref, tmp_ref):
        _scan_rows(dc_ref, tmp_ref, t, True)
        zz = z_ref[...] + b_ref[...]
        dz = tmp_ref[...] * (1.0 / (1.0 + jnp.exp(zz)))
        dz_ref[...] = dz.astype(dz_ref.dtype)
        db_ref[...] = jnp.sum(dz, axis=1, keepdims=True)

    return pl.pallas_call(
        body,
        name="forget_bwd",
        out_shape=[jax.ShapeDtypeStruct((rows, t), BF16), jax.ShapeDtypeStruct((rows, 1), F32)],
        scratch_shapes=[pltpu.VMEM((rows, t), F32)],
        compiler_params=pltpu.CompilerParams(vmem_limit_bytes=VMEM_LIMIT),
    )(z_row, b_col, dc_row)


def _fox_fwd(proj, off, gq, gk, c_row3, c_colb, heads, tq, rider=None):
    t = proj.shape[0]
    hd = FOX_HEAD_DIM
    tq = _tile(t, tq)
    nq = t // tq
    blk0 = off // hd
    scale = 1.0 / math.sqrt(hd)
    host = _Host(rider)

    def body(*refs):
        q_ref, k_ref, v_ref, gq_ref, gk_ref, crow_ref, ccol_ref = refs[:7]
        r_ins = refs[7:7 + host.n_in]
        o_ref, lse_ref = refs[7 + host.n_in:9 + host.n_in]
        r_outs = refs[9 + host.n_in:9 + host.n_in + host.n_out]
        khat_ref, v_t_ref = refs[9 + host.n_in + host.n_out:11 + host.n_in + host.n_out]
        sems = refs[11 + host.n_in + host.n_out:]
        h, qi = pl.program_id(0), pl.program_id(1)

        def compute():
            eye = (lax.broadcasted_iota(jnp.int32, (hd, hd), 0) == lax.broadcasted_iota(jnp.int32, (hd, hd), 1)).astype(BF16)

            @pl.when(qi == 0)
            def _():
                kn, _ = _head_rms(k_ref[...].astype(F32))
                khat_ref[...] = (kn * gk_ref[...]).astype(BF16)
                v_t_ref[...] = _dot(eye, v_ref[...], NT).astype(BF16)

            qn, _ = _head_rms(q_ref[...].astype(F32))
            qhat = (qn * (gq_ref[...] * scale)).astype(BF16)
            crow = crow_ref[:, pl.ds(pl.multiple_of(qi * tq, tq), tq)]
            above = lax.broadcasted_iota(jnp.int32, (tq, tq), 1) >= lax.broadcasted_iota(jnp.int32, (tq, tq), 0)

            def tile(j, carry, diagonal):
                m, l, acc_t = carry
                ks = pl.multiple_of(j * tq, tq)
                s_t = _dot(khat_ref[pl.ds(ks, tq), :], qhat, NT) - ccol_ref[pl.ds(ks, tq), 0:1]
                if diagonal:
                    s_t = jnp.where(above, s_t, NEG)
                m_new = jnp.maximum(m, jnp.max(s_t, axis=0, keepdims=True) + crow)
                alpha = jnp.exp(m - m_new)
                p_t = jnp.exp(s_t + (crow - m_new))
                l = alpha * l + jnp.sum(p_t, axis=0, keepdims=True)
                acc_t = alpha * acc_t + _dot(v_t_ref[:, pl.ds(ks, tq)], p_t.astype(BF16), NN)
                return m_new, l, acc_t

            init = (jnp.full((1, tq), NEG, F32), jnp.zeros((1, tq), F32), jnp.zeros((hd, tq), F32))
            carry = lax.fori_loop(0, qi, lambda j, c: tile(j, c, False), init)
            m, l, acc_t = tile(qi, carry, True)
            o_ref[...] = _dot((acc_t / l).astype(BF16), eye, TN).astype(o_ref.dtype)
            lse_ref[...] = m + jnp.log(l)

        first = jnp.logical_and(h == 0, qi == 0)
        last = jnp.logical_and(h == heads - 1, qi == nq - 1)
        host.run(first, last, r_ins, r_outs, sems, compute)

    res = pl.pallas_call(
        body,
        name="fox_fwd",
        grid=(heads, nq),
        in_specs=[
            pl.BlockSpec((tq, hd), lambda h, i: (i, blk0 + 3 * h)),
            pl.BlockSpec((t, hd), lambda h, i: (0, blk0 + 3 * h + 1)),
            pl.BlockSpec((t, hd), lambda h, i: (0, blk0 + 3 * h + 2)),
            pl.BlockSpec((1, hd), lambda h, i: (0, 0)),
            pl.BlockSpec((1, hd), lambda h, i: (0, 0)),
            pl.BlockSpec((None, 1, t), lambda h, i: (h, 0, 0)),
            pl.BlockSpec((None, t, LANES), lambda h, i: (h, 0, 0)),
        ] + host.in_specs,
        out_specs=[pl.BlockSpec((tq, hd), lambda h, i: (i, h)), pl.BlockSpec((None, 1, tq), lambda h, i: (h, 0, i))] + host.out_specs,
        out_shape=[jax.ShapeDtypeStruct((t, heads * hd), BF16), jax.ShapeDtypeStruct((heads, 1, t), F32)] + host.out_shapes,
        scratch_shapes=[pltpu.VMEM((t, hd), BF16), pltpu.VMEM((hd, t), BF16)] + host.scratch,
        compiler_params=_params(("arbitrary", "arbitrary")),
    )(proj, proj, proj, gq.reshape(1, hd), gk.reshape(1, hd), c_row3, c_colb, *host.ins)
    return res


def _fox_bwd(proj, off, o, do, gq, gk, c_row3, c_colb, lse, heads, tq, dproj, rider=None):
    t = proj.shape[0]
    hd = FOX_HEAD_DIM
    tq = _tile(t, tq)
    nb = t // tq
    blk0 = off // hd
    scale = 1.0 / math.sqrt(hd)
    host = _Host(rider)
    n_fixed_in = 11

    def body(*refs):
        q_ref, k_ref, v_ref, o_ref, do_ref, gq_ref, gk_ref, crow_ref, ccol_ref, lse_ref = refs[:10]
        pos = n_fixed_in
        r_ins = refs[pos:pos + host.n_in]; pos += host.n_in
        dp_ref, dc_ref, ggq_ref, ggk_ref = refs[pos:pos + 4]; pos += 4
        r_outs = refs[pos:pos + host.n_out]; pos += host.n_out
        qhat_ref, khat_ref, khat_t_ref, dq_t_ref, dk_ref, dcq_ref, dck_ref, delta_ref = refs[pos:pos + 8]; pos += 8
        sems = refs[pos:]
        h = pl.program_id(0)

        def compute():
            qn, rq = _head_rms(q_ref[...].astype(F32))
            qhat_ref[...] = (qn * (gq_ref[...] * scale)).astype(BF16)
            kn, rk = _head_rms(k_ref[...].astype(F32))
            khat_ref[...] = (kn * gk_ref[...]).astype(BF16)
            eye = (lax.broadcasted_iota(jnp.int32, (hd, hd), 0) == lax.broadcasted_iota(jnp.int32, (hd, hd), 1)).astype(BF16)
            khat_t_ref[...] = _dot(eye, khat_ref[...], NT).astype(BF16)
            delta = jnp.sum(do_ref[...].astype(F32) * o_ref[...].astype(F32), axis=-1, keepdims=True)
            for b in range(nb):
                sl = slice(b * tq, (b + 1) * tq)
                delta_ref[:, sl] = _col_to_row(delta[sl, :])
            dq_t_ref[...] = jnp.zeros_like(dq_t_ref)
            dcq_ref[...] = jnp.zeros_like(dcq_ref)
            above = lax.broadcasted_iota(jnp.int32, (tq, tq), 1) >= lax.broadcasted_iota(jnp.int32, (tq, tq), 0)

            def kv_block(j, _):
                ks = pl.multiple_of(j * tq, tq)
                kh = khat_ref[pl.ds(ks, tq), :]
                kh_t = khat_t_ref[:, pl.ds(ks, tq)]
                vv = v_ref[pl.ds(ks, tq), :]
                ccol = ccol_ref[pl.ds(ks, tq), 0:1]

                def q_block(i, carry, diagonal):
                    dk, dv, dck = carry
                    qs = pl.multiple_of(i * tq, tq)
                    qh = qhat_ref[pl.ds(qs, tq), :]
                    dob = do_ref[pl.ds(qs, tq), :]
                    s_t = _dot(kh, qh, NT) + ((crow_ref[:, pl.ds(qs, tq)] - lse_ref[:, pl.ds(qs, tq)]) - ccol)
                    p_t = jnp.exp(s_t)
                    if diagonal:
                        p_t = jnp.where(above, p_t, 0.0)
                    ds_t = p_t * (_dot(vv, dob, NT) - delta_ref[:, pl.ds(qs, tq)])
                    dsb = ds_t.astype(BF16)
                    dv = dv + _dot(p_t.astype(BF16), dob, NN)
                    dk = dk + _dot(dsb, qh, NN)
                    dq_t_ref[:, pl.ds(qs, tq)] += _dot(kh_t, dsb, NN)
                    dcq_ref[:, pl.ds(qs, tq)] += jnp.sum(ds_t, axis=0, keepdims=True)
                    dck = dck + jnp.sum(ds_t, axis=-1, keepdims=True)
                    return dk, dv, dck

                zero = jnp.zeros((tq, hd), F32)
                carry = q_block(j, (zero, zero, jnp.zeros((tq, 1), F32)), True)
                dk, dv, dck = lax.fori_loop(j + 1, nb, lambda i, c: q_block(i, c, False), carry)
                dk_ref[pl.ds(ks, tq), :] = dk
                dp_ref[pl.ds(ks, tq), 2 * hd:3 * hd] = dv.astype(dp_ref.dtype)
                dck_ref[pl.ds(ks, tq), :] = dck
                return 0

            lax.fori_loop(0, nb, kv_block, 0)

            dq, ggq = _head_rms_bwd(dq_t_ref[...].T * scale, qn, rq, gq_ref[...])
            dk, ggk = _head_rms_bwd(dk_ref[...], kn, rk, gk_ref[...])
            dp_ref[:, 0:hd] = dq.astype(dp_ref.dtype)
            dp_ref[:, hd:2 * hd] = dk.astype(dp_ref.dtype)
            for b in range(nb):
                sl = slice(b * tq, (b + 1) * tq)
                dc_ref[:, sl] = dcq_ref[:, sl] - _col_to_row(dck_ref[sl, :])

            @pl.when(h == 0)
            def _():
                ggq_ref[...] = jnp.zeros_like(ggq_ref)
                ggk_ref[...] = jnp.zeros_like(ggk_ref)

            ggq_ref[...] += ggq
            ggk_ref[...] += ggk

        host.run(h == 0, h == heads - 1, r_ins, r_outs, sems, compute)

    head_in = lambda part: pl.BlockSpec((t, hd), lambda h: (0, blk0 + 3 * h + part))
    vec = pl.BlockSpec((1, hd), lambda h: (0, 0))
    colb = pl.BlockSpec((None, t, LANES), lambda h: (h, 0, 0))
    res = pl.pallas_call(
        body,
        name="fox_bwd",
        grid=(heads,),
        in_specs=[
            head_in(0), head_in(1), head_in(2),
            pl.BlockSpec((t, hd), lambda h: (0, h)),
            pl.BlockSpec((t, hd), lambda h: (0, h)),
            vec, vec,
            pl.BlockSpec((None, 1, t), lambda h: (h, 0, 0)),
            colb,
            pl.BlockSpec((None, 1, t), lambda h: (h, 0, 0)),
            ANY,
        ] + host.in_specs,
        out_specs=[
            pl.BlockSpec((t, 3 * hd), lambda h: (0, blk0 // 3 + h)),
            pl.BlockSpec((None, 1, t), lambda h: (h, 0, 0)),
            vec, vec,
        ] + host.out_specs,
        out_shape=[
            jax.ShapeDtypeStruct(dproj.shape, dproj.dtype),
            jax.ShapeDtypeStruct((heads, 1, t), F32),
            jax.ShapeDtypeStruct((1, hd), F32),
            jax.ShapeDtypeStruct((1, hd), F32),
        ] + host.out_shapes,
        input_output_aliases={10: 0},
        scratch_shapes=[
            pltpu.VMEM((t, hd), BF16), pltpu.VMEM((t, hd), BF16), pltpu.VMEM((hd, t), BF16),
            pltpu.VMEM((hd, t), F32), pltpu.VMEM((t, hd), F32),
            pltpu.VMEM((1, t), F32), pltpu.VMEM((t, 1), F32), pltpu.VMEM((1, t), F32),
        ] + host.scratch,
        compiler_params=_params(("arbitrary",)),
    )(proj, proj, proj, o, do, gq.reshape(1, hd), gk.reshape(1, hd), c_row3, c_colb, lse, dproj, *host.ins)
    return res


def _mem_fwd(proj, off, kv, gq, gk, tq):
    t = proj.shape[0]
    m, width = kv.shape[0], kv.shape[1] // 2
    hd = width // MEM_HEADS
    tq = _tile(t, tq)
    blk0 = off // hd
    scale = 1.0 / math.sqrt(hd)

    def body(q_ref, k_ref, v_ref, gq_ref, gk_ref, o_ref):
        qn, _ = _head_rms(q_ref[...].astype(F32))
        kn, _ = _head_rms(k_ref[...])
        s = _dot((qn * gq_ref[...]).astype(BF16), (kn * gk_ref[...]).astype(BF16), NT) * scale
        p = jnp.exp(s - jnp.max(s, axis=-1, keepdims=True))
        p = p / jnp.sum(p, axis=-1, keepdims=True)
        o_ref[...] = _dot(p.astype(BF16), v_ref[...].astype(BF16), NN).astype(o_ref.dtype)

    vec = pl.BlockSpec((1, hd), lambda h, i: (0, 0))
    return pl.pallas_call(
        body,
        name="mem_fwd",
        grid=(MEM_HEADS, t // tq),
        in_specs=[
            pl.BlockSpec((tq, hd), lambda h, i: (i, blk0 + h)),
            pl.BlockSpec((m, hd), lambda h, i: (0, h)),
            pl.BlockSpec((m, hd), lambda h, i: (0, MEM_HEADS + h)),
            vec, vec,
        ],
        out_specs=pl.BlockSpec((tq, hd), lambda h, i: (i, h)),
        out_shape=jax.ShapeDtypeStruct((t, width), BF16),
        compiler_params=_params(("parallel", "parallel")),
    )(proj, kv, kv, gq.reshape(1, hd), gk.reshape(1, hd))


def _mem_bwd(proj, off, kv, do, gq, gk, tq, dproj, rider=None):
    t = proj.shape[0]
    m, width = kv.shape[0], kv.shape[1] // 2
    hd = width // MEM_HEADS
    tq = _tile(t, tq)
    nq = t // tq
    blk0 = off // hd
    scale = 1.0 / math.sqrt(hd)
    host = _Host(rider)

    def body(*refs):
        q_ref, k_ref, v_ref, do_ref, gq_ref, gk_ref = refs[:6]
        pos = 7
        r_ins = refs[pos:pos + host.n_in]; pos += host.n_in
        dq_ref, dk_ref, dv_ref, ggq_ref, ggk_ref = refs[pos:pos + 5]; pos += 5
        r_outs = refs[pos:pos + host.n_out]; pos += host.n_out
        dkh_ref, dvh_ref = refs[pos:pos + 2]; pos += 2
        sems = refs[pos:]
        h, i = pl.program_id(0), pl.program_id(1)

        def compute():
            qn, rq = _head_rms(q_ref[...].astype(F32))
            kn, rk = _head_rms(k_ref[...])
            qhat = (qn * gq_ref[...]).astype(BF16)
            khat = (kn * gk_ref[...]).astype(BF16)
            vb = v_ref[...].astype(BF16)
            dob = do_ref[...]
            s = _dot(qhat, khat, NT) * scale
            p = jnp.exp(s - jnp.max(s, axis=-1, keepdims=True))
            p = p / jnp.sum(p, axis=-1, keepdims=True)
            dp = _dot(dob, vb, NT)
            ds = p * (dp - jnp.sum(dp * p, axis=-1, keepdims=True))
            dsb = ds.astype(BF16)
            dq, ggq = _head_rms_bwd(_dot(dsb, khat, NN) * scale, qn, rq, gq_ref[...])
            dq_ref[...] = dq.astype(dq_ref.dtype)

            @pl.when(i == 0)
            def _():
                dkh_ref[...] = jnp.zeros_like(dkh_ref)
                dvh_ref[...] = jnp.zeros_like(dvh_ref)

            @pl.when(jnp.logical_and(h == 0, i == 0))
            def _():
                ggq_ref[...] = jnp.zeros_like(ggq_ref)
                ggk_ref[...] = jnp.zeros_like(ggk_ref)

            dkh_ref[...] += _dot(dsb, qhat, TN)
            dvh_ref[...] += _dot(p.astype(BF16), dob, TN)
            ggq_ref[...] += ggq

            @pl.when(i == nq - 1)
            def _():
                dk, ggk = _head_rms_bwd(dkh_ref[...] * scale, kn, rk, gk_ref[...])
                dk_ref[...] = dk.astype(dk_ref.dtype)
                dv_ref[...] = dvh_ref[...].astype(dv_ref.dtype)
                ggk_ref[...] += ggk

        first = jnp.logical_and(h == 0, i == 0)
        last = jnp.logical_and(h == MEM_HEADS - 1, i == nq - 1)
        host.run(first, last, r_ins, r_outs, sems, compute)

    vec = pl.BlockSpec((1, hd), lambda h, i: (0, 0))
    kblk = pl.BlockSpec((m, hd), lambda h, i: (0, h))
    res = pl.pallas_call(
        body,
        name="mem_bwd",
        grid=(MEM_HEADS, nq),
        in_specs=[
            pl.BlockSpec((tq, hd), lambda h, i: (i, blk0 + h)), kblk,
            pl.BlockSpec((m, hd), lambda h, i: (0, MEM_HEADS + h)),
            pl.BlockSpec((tq, hd), lambda h, i: (i, h)), vec, vec, ANY,
        ] + host.in_specs,
        out_specs=[pl.BlockSpec((tq, hd), lambda h, i: (i, blk0 + h)), kblk, kblk, vec, vec] + host.out_specs,
        out_shape=[
            jax.ShapeDtypeStruct(dproj.shape, dproj.dtype),
            jax.ShapeDtypeStruct((m, width), BF16),
            jax.ShapeDtypeStruct((m, width), BF16),
            jax.ShapeDtypeStruct((1, hd), F32),
            jax.ShapeDtypeStruct((1, hd), F32),
        ] + host.out_shapes,
        input_output_aliases={6: 0},
        scratch_shapes=[pltpu.VMEM((m, hd), F32), pltpu.VMEM((m, hd), F32)] + host.scratch,
        compiler_params=_params(("arbitrary", "arbitrary")),
    )(proj, kv, kv, do, gq.reshape(1, hd), gk.reshape(1, hd), dproj, *host.ins)
    dproj, dk, dv, ggq, ggk = res[:5]
    return (dproj, jnp.concatenate([dk, dv], axis=1), ggq.reshape(hd), ggk.reshape(hd), *res[5:])


def _sigmoid(z):
    return 1.0 / (1.0 + jnp.exp(-z))


def _merge_fwd(proj, o3, tm, tc):
    t, d = o3[0].shape
    tm = _tile(t, tm)

    def body(g_ref, oa_ref, ob_ref, oc_ref, out_ref):
        acc = jnp.zeros((tm, tc), F32)
        for s, o_ref in enumerate((oa_ref, ob_ref, oc_ref)):
            acc = acc + _sigmoid(g_ref[:, s * tc:(s + 1) * tc].astype(F32)) * o_ref[...].astype(F32)
        out_ref[...] = acc.astype(out_ref.dtype)

    blk = pl.BlockSpec((tm, tc), lambda i, j: (i, j))
    return pl.pallas_call(
        body,
        name="merge_fwd",
        grid=(t // tm, d // tc),
        in_specs=[pl.BlockSpec((tm, 3 * tc), lambda i, j: (i, j)), blk, blk, blk],
        out_specs=blk,
        out_shape=jax.ShapeDtypeStruct((t, d), BF16),
        compiler_params=_params(("parallel", "parallel")),
    )(proj, *o3)


def _merge_bwd(proj, o3, dm, tm, tc):
    t, d = dm.shape
    tm = _tile(t, tm)

    def body(g_ref, oa_ref, ob_ref, oc_ref, dm_ref, dg_ref, da_ref, db_ref, dc_ref):
        dmf = dm_ref[...].astype(F32)
        for s, (o_ref, do_ref) in enumerate(((oa_ref, da_ref), (ob_ref, db_ref), (oc_ref, dc_ref))):
            g = _sigmoid(g_ref[:, s * tc:(s + 1) * tc].astype(F32))
            do_ref[...] = (dmf * g).astype(do_ref.dtype)
            dg_ref[:, s * tc:(s + 1) * tc] = (dmf * o_ref[...].astype(F32) * g * (1.0 - g)).astype(dg_ref.dtype)

    blk = pl.BlockSpec((tm, tc), lambda i, j: (i, j))
    wide = pl.BlockSpec((tm, 3 * tc), lambda i, j: (i, j))
    return pl.pallas_call(
        body,
        name="merge_bwd",
        grid=(t // tm, d // tc),
        in_specs=[wide, blk, blk, blk, blk],
        out_specs=[wide, blk, blk, blk],
        out_shape=[jax.ShapeDtypeStruct(proj.shape, BF16)] + [jax.ShapeDtypeStruct((t, d), BF16)] * 3,
        compiler_params=_params(("parallel", "parallel")),
    )(proj, *o3, dm)


def _w_in_chunks(d, tc):
    cw = d // 2
    heads = cw // FOX_HEAD_DIM
    conv0, fox0, f0, mq0, gate0 = 0, 3 * cw, 6 * cw, 6 * cw + heads, 7 * cw + heads
    chunks = [(gate0 + s * d + j * tc, gate0 + s * d + (j + 1) * tc) for j in range(d // tc) for s in range(N_BRANCHES)]
    chunks += [(conv0 + s * cw + j * LANES, conv0 + s * cw + (j + 1) * LANES) for j in range(cw // LANES) for s in range(3)]
    chunks += [(fox0 + s * cw + j * FOX_HEAD_DIM, fox0 + s * cw + (j + 1) * FOX_HEAD_DIM) for j in range(heads) for s in range(3)]
    chunks.append((mq0, mq0 + cw))
    return chunks, (f0, f0 + heads)


def _pack_w_in(w_in_t, d, tc):
    cw = d // 2
    heads = cw // FOX_HEAD_DIM
    k = w_in_t.shape[1]
    o = 0
    conv = w_in_t[o:o + 3 * cw]; o += 3 * cw
    fox = w_in_t[o:o + 3 * cw]; o += 3 * cw
    f = w_in_t[o:o + heads]; o += heads
    mq = w_in_t[o:o + cw]; o += cw
    gate = w_in_t[o:o + N_BRANCHES * d]
    conv = conv.reshape(3, cw // LANES, LANES, k).transpose(1, 0, 2, 3).reshape(3 * cw, k)
    fox = fox.reshape(3, heads, FOX_HEAD_DIM, k).transpose(1, 0, 2, 3).reshape(3 * cw, k)
    gate = gate.reshape(N_BRANCHES, d // tc, tc, k).transpose(1, 0, 2, 3).reshape(N_BRANCHES * d, k)
    return jnp.concatenate([gate, conv, fox, mq], axis=0), jnp.pad(f, ((0, F_ROWS - heads), (0, 0)))


def _unpack_g_in(g_all, g_f, d, tc):
    cw = d // 2
    heads = cw // FOX_HEAD_DIM
    k = g_all.shape[1]
    o = 0
    gate = g_all[o:o + N_BRANCHES * d]; o += N_BRANCHES * d
    conv = g_all[o:o + 3 * cw]; o += 3 * cw
    fox = g_all[o:o + 3 * cw]; o += 3 * cw
    mq = g_all[o:o + cw]
    conv = conv.reshape(cw // LANES, 3, LANES, k).transpose(1, 0, 2, 3).reshape(3 * cw, k)
    fox = fox.reshape(heads, 3, FOX_HEAD_DIM, k).transpose(1, 0, 2, 3).reshape(3 * cw, k)
    gate = gate.reshape(d // tc, N_BRANCHES, tc, k).transpose(1, 0, 2, 3).reshape(N_BRANCHES * d, k)
    return jnp.concatenate([conv, fox, g_f[:heads], mq, gate], axis=0)


def _unblock(w8):
    return w8.transpose(1, 0, 2).reshape(w8.shape[1], -1)


def _tile2(r, cols, tr, tcols):
    if r % 8 == 0:
        return _tile(r, tr), cols
    return r, _tile(cols, tcols)


def _pair_sum(name, g8, got, c):
    def body(c_ref, g_ref, s_ref, o_ref):
        o_ref[...] = (g_ref[...].astype(F32) + s_ref[...].astype(F32)).astype(o_ref.dtype)

    if g8.ndim == 4:
        _, r, k1, k2 = g8.shape
        tr = max(cand for cand in range(1, 385) if r % cand == 0)
        grid = (N_CHIPS, r // tr)
        shape = (None, tr, k1, k2)
        own = pl.BlockSpec(shape, lambda q, i, c_ref: (2 * q + c_ref[0], i, 0, 0))
        blk = pl.BlockSpec(shape, lambda q, i, c_ref: (q, i, 0, 0))
    else:
        _, r, cols = g8.shape
        tr, tcols = _tile2(r, cols, 256, 256)
        grid = (N_CHIPS, r // tr, cols // tcols)
        own = pl.BlockSpec((None, tr, tcols), lambda q, i, j, c_ref: (2 * q + c_ref[0], i, j))
        blk = pl.BlockSpec((None, tr, tcols), lambda q, i, j, c_ref: (q, i, j))
    return pl.pallas_call(
        body,
        name=name,
        grid_spec=pltpu.PrefetchScalarGridSpec(num_scalar_prefetch=1, grid=grid, in_specs=[own, blk], out_specs=blk),
        out_shape=jax.ShapeDtypeStruct((N_CHIPS,) + g8.shape[1:], BF16),
        compiler_params=_params(("parallel",) * len(grid)),
    )(c, g8, got)


def _local_step(x, mem, target, w, small, comm=None):
    t, d = x.shape
    cw = d // 2
    heads = cw // FOX_HEAD_DIM
    tc = min(512, d)
    tq = min(512, t)
    off_conv, off_fox, off_mq = 3 * d, 3 * d + 3 * cw, 3 * d + 6 * cw
    w = dict(w)
    w_all, w_f = _pack_w_in(w["w_in"], d, tc)
    big = dict(tm=1024, tn=512, tk=2048)
    wide_k = dict(tm=512, tn=1024, tk=4096)

    h = _rms_fwd("rms1_fwd", x, small["norm1_g"])
    if comm:
        early = ("w_conv_out", "w_fox_out", "w_mem_out", "w_out", "w_mem_kv")
        proj, *got = _matmul("proj", "nt", h, w_all, outs=[BF16], rider=_gather_rider([comm["shards"][n] for n in early]), **big)
        for n, val in zip(early, got):
            w[n] = _unblock(val) if n in COLUMN_SPLIT else val.reshape(-1, val.shape[-1])
    else:
        proj = _matmul("proj", "nt", h, w_all, outs=[BF16], **big)
    z_row = _matmul("proj_f", "nt", w_f, h, outs=[F32], tm=F_ROWS, tn=512, tk=2048)

    y_conv = _conv_fwd(proj, off_conv, small["conv_w"], LANES)

    b_col = jnp.pad(small["b_f"], (0, F_ROWS - heads)).reshape(F_ROWS, 1)
    c_row3 = _forget_fwd(z_row, b_col)[:heads].reshape(heads, 1, t)
    c_colb = _rows_to_colb(c_row3, tq)
    if comm:
        y_fox, lse, got = _fox_fwd(proj, off_fox, small["fox_q_g"], small["fox_k_g"], c_row3, c_colb, heads, tq,
                                   rider=_gather_rider([comm["shards"]["w_up"]]))
        w["w_up"] = _unblock(got)
    else:
        y_fox, lse = _fox_fwd(proj, off_fox, small["fox_q_g"], small["fox_k_g"], c_row3, c_colb, heads, tq)

    nm = _rms_fwd("mem_rms_fwd", mem, small["mem_norm_g"])
    kv = _matmul("mem_kv", "nn", nm, w["w_mem_kv"], outs=[F32], tm=256, tn=512, tk=2048)
    y_mem = _mem_fwd(proj, off_mq, kv, small["mem_q_g"], small["mem_k_g"], tq)

    ys = (y_conv, y_fox, y_mem)
    w_outs = (w["w_conv_out"], w["w_fox_out"], w["w_mem_out"])
    o3 = [_matmul(f"branch_out{s}", "nn", ys[s], w_outs[s], outs=[BF16], **big) for s in range(3)]
    merged = _merge_fwd(proj, o3, 512, tc)
    x1 = _matmul("out_proj", "nn", merged, w["w_out"], outs=[F32], extras=[x],
                 epilogue=lambda acc, xr: (acc + xr,), **big)
    h2 = _rms_fwd("rms2_fwd", x1, small["norm2_g"])

    def up_epilogue(acc):
        return acc, jnp.square(jnp.maximum(acc, 0.0))

    if comm:
        up, act, got = _matmul("mlp_up", "nn", h2, w["w_up"], outs=[BF16, BF16], epilogue=up_epilogue,
                               rider=_gather_rider([comm["shards"]["w_down"]]), **big)
        w["w_down"] = got.reshape(-1, got.shape[-1])
    else:
        up, act = _matmul("mlp_up", "nn", h2, w["w_up"], outs=[BF16, BF16], epilogue=up_epilogue, **big)

    def loss_epilogue(acc, x1r, tr):
        dy = (acc + x1r - tr) * (1.0 / d)
        return dy, dy

    dy, dyb = _matmul("mlp_down", "nn", act, w["w_down"], outs=[F32, BF16], extras=[x1, target],
                      epilogue=loss_epilogue, **big)

    def dup_epilogue(acc, upr):
        return (acc * 2.0 * jnp.maximum(upr.astype(F32), 0.0),)

    def by_owner(g):
        return g.reshape(N_DEV, -1, g.shape[-1])

    g, parts = {}, {}
    g["w_down"] = _matmul("d_w_down", "tn", act, dyb, outs=[BF16], **wide_k)
    if comm:
        dup, got = _matmul("d_act", "nt", dyb, w["w_down"], outs=[BF16], extras=[up], epilogue=dup_epilogue,
                           rider=_pair_rider([by_owner(g["w_down"])]), **big)
        pair = _pair_sum("pair_w_down", by_owner(g["w_down"]), got, comm["c"])
        g["w_up"], parts["w_down"] = _matmul("d_w_up", "tn", h2, dup, outs=[BF16], out_blocks=True,
                                             rider=_chip_rider([pair]), **wide_k)
        dh2, got = _matmul("d_h2", "nt", dup, w["w_up"], outs=[F32], rider=_pair_rider([g["w_up"]]), **big)
        pair_up = _pair_sum("pair_w_up", g["w_up"], got, comm["c"])
    else:
        dup = _matmul("d_act", "nt", dyb, w["w_down"], outs=[BF16], extras=[up], epilogue=dup_epilogue, **big)
        g["w_up"] = _matmul("d_w_up", "tn", h2, dup, outs=[BF16], out_blocks=True, **wide_k)
        dh2 = _matmul("d_h2", "nt", dup, w["w_up"], outs=[F32], **big)
    dx1, dx1b, g_norm2, dy_sq = _rms_bwd("rms2_bwd", dh2, x1, small["norm2_g"], res=dy)
    loss = dy_sq * (0.5 * d)

    g["w_out"] = _matmul("d_w_out", "tn", merged, dx1b, outs=[BF16], **wide_k)
    dmerged = _matmul("d_merged", "nt", dx1b, w["w_out"], outs=[BF16], **big)
    dproj, *do3 = _merge_bwd(proj, o3, dmerged, 512, tc)
    names = ("w_conv_out", "w_fox_out", "w_mem_out")
    dys = []
    for s in range(3):
        g[names[s]] = _matmul(f"d_w_branch{s}", "tn", ys[s], do3[s], outs=[BF16], out_blocks=True, **wide_k)
        dys.append(_matmul(f"d_branch{s}", "nt", do3[s], w_outs[s], outs=[BF16], **big))

    dproj, dkv, g_mq, g_mk = _mem_bwd(proj, off_mq, kv, dys[2], small["mem_q_g"], small["mem_k_g"], tq, dproj)
    g["w_mem_kv"] = _matmul("d_w_mem_kv", "tn", nm, dkv, outs=[BF16], **wide_k)
    dnm = _matmul("d_mem_norm", "nt", dkv, w["w_mem_kv"], outs=[F32], tm=256, tn=512, tk=2048)
    _, _, g_mem_norm, _ = _rms_bwd("mem_rms_bwd", dnm, mem, small["mem_norm_g"])

    mid = ("w_out", "w_conv_out", "w_fox_out", "w_mem_out", "w_mem_kv")
    if comm:
        mid8 = [g[n] if n in names else by_owner(g[n]) for n in mid]
        dproj, g_conv_w, *got = _conv_bwd(proj, off_conv, small["conv_w"], dys[0], LANES, dproj, rider=_pair_rider(mid8))
        pairs = [pair_up] + [_pair_sum("pair_" + n, g8, s4, comm["c"]) for n, g8, s4 in zip(mid, mid8, got)]
        dproj, dc, g_fq, g_fk, *got = _fox_bwd(proj, off_fox, y_fox, dys[1], small["fox_q_g"], small["fox_k_g"], c_row3, c_colb,
                                               lse, heads, tq, dproj, rider=_chip_rider(pairs))
        parts.update(zip(("w_up",) + mid, got))
    else:
        dproj, g_conv_w = _conv_bwd(proj, off_conv, small["conv_w"], dys[0], LANES, dproj)
        dproj, dc, g_fq, g_fk = _fox_bwd(proj, off_fox, y_fox, dys[1], small["fox_q_g"], small["fox_k_g"], c_row3, c_colb,
                                         lse, heads, tq, dproj)
    dc_row = jnp.pad(dc.reshape(heads, t), ((0, F_ROWS - heads), (0, 0)))
    dz_row, db = _forget_bwd(z_row, b_col, dc_row)

    g_all = _matmul("d_w_in", "tn", dproj, h, outs=[BF16], j_outer=True, **wide_k)
    g_wf = _matmul("d_w_f", "nn", dz_row, h, outs=[BF16], tm=F_ROWS, tn=512, tk=4096)
    g["w_in"] = _unpack_g_in(g_all, g_wf, d, tc)
    dh = _matmul("d_h_f", "tn", dz_row, w_f, outs=[F32], tm=1024, tn=512, tk=F_ROWS)
    add_prev = lambda acc, prev: (acc + prev,)
    if comm:
        g_in8 = by_owner(g["w_in"])
        got = _run_rider("pair_exchange_w_in", _pair_rider([g_in8]))[0]
        pair = _pair_sum("pair_w_in", g_in8, got, comm["c"])
        dh, parts["w_in"] = _matmul("d_h", "nn", dproj, w_all, outs=[F32], extras=[dh], epilogue=add_prev,
                                    rider=_chip_rider([pair]), tm=1024, tn=512, tk=3328)
    else:
        dh = _matmul("d_h", "nn", dproj, w_all, outs=[F32], extras=[dh], epilogue=add_prev, tm=1024, tn=512, tk=3328)
    grad_x, _, g_norm1, _ = _rms_bwd("rms1_bwd", dh, x, small["norm1_g"], res=dx1)

    gs = dict(norm1_g=g_norm1, b_f=db[:heads, 0], conv_w=g_conv_w, fox_q_g=g_fq.reshape(-1), fox_k_g=g_fk.reshape(-1),
              mem_norm_g=g_mem_norm, mem_q_g=g_mq, mem_k_g=g_mk, norm2_g=g_norm2)
    return loss, grad_x, (parts if comm else g), gs


def _adamw_math(w, g, m, v):
    m = ADAM_B1 * m + (1.0 - ADAM_B1) * g
    v = ADAM_B2 * v + (1.0 - ADAM_B2) * jnp.square(g)
    m_hat = m / (1.0 - ADAM_B1 ** ADAM_STEP)
    v_hat = v / (1.0 - ADAM_B2 ** ADAM_STEP)
    delta = -ADAM_LR * (m_hat / (jnp.sqrt(v_hat) + ADAM_EPS) + ADAM_WD * w)
    return delta, m, v


def _adamw(name, parts, w, m, v):
    r, c = w.shape
    tr, tc = _tile2(r, c, 128, 256)
    n_parts = parts.shape[0]

    def body(p_ref, w_ref, m_ref, v_ref, g_ref, d_ref, nm_ref, nv_ref):
        g = p_ref[0].astype(F32)
        for s in range(1, n_parts):
            g = g + p_ref[s].astype(F32)
        delta, nm, nv = _adamw_math(w_ref[...], g, m_ref[...], v_ref[...])
        g_ref[...] = g
        d_ref[...] = delta
        nm_ref[...] = nm
        nv_ref[...] = nv

    blk = pl.BlockSpec((tr, tc), lambda i, j: (i, j))
    return pl.pallas_call(
        body,
        name=name,
        grid=(r // tr, c // tc),
        in_specs=[pl.BlockSpec((n_parts, tr, tc), lambda i, j: (0, i, j)), blk, blk, blk],
        out_specs=[blk] * 4,
        out_shape=[jax.ShapeDtypeStruct((r, c), F32)] * 4,
        compiler_params=_params(("parallel", "parallel")),
    )(parts, w, m, v)


def _sum_parts(name, parts):
    n_parts, r, c = parts.shape

    def body(p_ref, o_ref):
        acc = p_ref[0]
        for s in range(1, n_parts):
            acc = acc + p_ref[s]
        o_ref[...] = acc

    return pl.pallas_call(body, name=name, out_shape=jax.ShapeDtypeStruct((r, c), F32))(parts)


BIG = ("w_in", "w_mem_kv", "w_conv_out", "w_fox_out", "w_mem_out", "w_out", "w_up", "w_down")
COLUMN_SPLIT = ("w_in", "w_conv_out", "w_fox_out", "w_mem_out", "w_up")
SMALL = ("norm1_g", "b_f", "conv_w", "fox_q_g", "fox_k_g", "mem_norm_g", "mem_q_g", "mem_k_g", "norm2_g")
WEIGHTS = ("norm1_g", "w_in", "b_f", "conv_w", "fox_q_g", "fox_k_g", "mem_norm_g", "w_mem_kv", "mem_q_g", "mem_k_g",
           "w_conv_out", "w_fox_out", "w_mem_out", "w_out", "norm2_g", "w_up", "w_down")


def _pack(vectors):
    rows = []
    for vec in vectors:
        n = vec.shape[0]
        rows.append(jnp.pad(vec, (0, -n % LANES)).reshape(-1, LANES))
    out = jnp.concatenate(rows, axis=0)
    return jnp.pad(out, ((0, -out.shape[0] % 8), (0, 0)))


def _unpack(packed, sizes):
    out, row = [], 0
    for n in sizes:
        nr = -(-n // LANES)
        out.append(packed[row:row + nr].reshape(-1)[:n])
        row += nr
    return out


def kernel(x, mem, norm1_g, w_in, b_f, conv_w, fox_q_g, fox_k_g, mem_norm_g, w_mem_kv, mem_q_g, mem_k_g, w_conv_out, w_fox_out, w_mem_out, w_out, norm2_g, w_up, w_down, loss_target, m_norm1_g, m_w_in, m_b_f, m_conv_w, m_fox_q_g, m_fox_k_g, m_mem_norm_g, m_w_mem_kv, m_mem_q_g, m_mem_k_g, m_w_conv_out, m_w_fox_out, m_w_mem_out, m_w_out, m_norm2_g, m_w_up, m_w_down, v_norm1_g, v_w_in, v_b_f, v_conv_w, v_fox_q_g, v_fox_k_g, v_mem_norm_g, v_w_mem_kv, v_mem_q_g, v_mem_k_g, v_w_conv_out, v_w_fox_out, v_w_mem_out, v_w_out, v_norm2_g, v_w_up, v_w_down):
    args = dict(locals())
    wts = {n: args[n] for n in WEIGHTS}
    ms = {n: args["m_" + n] for n in WEIGHTS}
    vs = {n: args["v_" + n] for n in WEIGHTS}
    x_pos, y_pos, c_pos = _position()
    me = _index(x_pos, y_pos, c_pos)

    shards = {n: (wts[n].T if n == "w_in" else wts[n]).astype(BF16) for n in BIG}
    wi, cw8 = _run_rider("all_gather_first", _gather_rider([shards["w_in"], conv_w]))
    full = {"w_in": wi.reshape(-1, wi.shape[-1])}
    small = {n: wts[n] for n in SMALL}
    small["conv_w"] = _unblock(cw8)
    comm = {"shards": shards, "c": c_pos.astype(jnp.int32).reshape(1)}

    loss, grad_x, parts, gs = _local_step(x[0], mem[0], loss_target[0], full, small, comm)

    out_g, out_d, out_m, out_v = {}, {}, {}, {}
    for n in BIG:
        if n == "w_in":
            res = _adamw("adamw_" + n, parts[n], wts[n].T, ms[n].T, vs[n].T)
            out_g[n], out_d[n], out_m[n], out_v[n] = (r.T for r in res)
        else:
            out_g[n], out_d[n], out_m[n], out_v[n] = _adamw("adamw_" + n, parts[n], wts[n], ms[n], vs[n])

    small_sizes = [int(math.prod(gs[n].shape)) for n in SMALL]
    packed = _pack([gs[n].reshape(-1) for n in SMALL])
    gsum = _sum_parts("sum_small", _run_rider("exchange_small", _broadcast_rider([packed]))[0])
    gsmall = dict(zip(SMALL, _unpack(gsum, small_sizes)))
    cols = conv_w.shape[1]
    gsmall["conv_w"] = lax.dynamic_slice(gsmall["conv_w"].reshape(CONV_TAPS, -1), (0, me * cols), (CONV_TAPS, cols)).reshape(-1)
    pg, pw, pm, pv = (_pack([src[n].reshape(-1) for n in SMALL]) for src in (gsmall, wts, ms, vs))
    _, sd, sm, sv = _adamw("adamw_small", pg[None], pw, pm, pv)
    local_sizes = [int(math.prod(wts[n].shape)) for n in SMALL]
    for dst, src in ((out_d, sd), (out_m, sm), (out_v, sv)):
        for n, val in zip(SMALL, _unpack(src, local_sizes)):
            dst[n] = val.reshape(wts[n].shape)
    for n in SMALL:
        out_g[n] = gsmall[n].reshape(wts[n].shape)

    loss = lax.psum(loss, MESH_AXES)
    return (loss, grad_x[None], *[out_g[n] for n in WEIGHTS], *[out_d[n] for n in WEIGHTS],
            *[out_m[n] for n in WEIGHTS], *[out_v[n] for n in WEIGHTS])
```

```python
import math

import jax
import jax.numpy as jnp
from jax import lax
from jax.experimental import pallas as pl
from jax.experimental.pallas import tpu as pltpu

F32 = jnp.float32
BF16 = jnp.bfloat16

EPS = 1e-6
N_DEV = 8
N_CHIPS = 4
FOX_HEAD_DIM = 128
MEM_HEADS = 4
CONV_TAPS = 3
N_BRANCHES = 3
F_ROWS = 16

ADAM_LR = 0.001
ADAM_B1 = 0.9
ADAM_B2 = 0.999
ADAM_EPS = 1e-08
ADAM_WD = 0.01
ADAM_STEP = 10

V7X_VMEM_BYTES = 64 * 1024 * 1024
VMEM_LIMIT = V7X_VMEM_BYTES * 3 // 4
LANES = 128
NEG = -1e30

MESH_AXES = ("x", "y", "c")
MESH = pl.DeviceIdType.MESH
ANY = pl.BlockSpec(memory_space=pl.ANY)

NN = (((1,), (0,)), ((), ()))
NT = (((1,), (1,)), ((), ()))
TN = (((0,), (0,)), ((), ()))


def _params(sem):
    return pltpu.CompilerParams(dimension_semantics=sem, vmem_limit_bytes=VMEM_LIMIT)


def _dot(a, b, dn):
    return lax.dot_general(a, b, dn, preferred_element_type=F32)


def _tile(n, t):
    if n <= t:
        return n
    for cand in range(t - t % LANES, 0, -LANES):
        if n % cand == 0:
            return cand
    raise ValueError((n, t))


class _Rider:
    def __init__(self, ins, out_shapes, sem_shapes, start, finish):
        self.ins, self.out_shapes, self.sem_shapes = list(ins), list(out_shapes), list(sem_shapes)
        self.start, self.finish = start, finish


def _position():
    return lax.axis_index("x"), lax.axis_index("y"), lax.axis_index("c")


def _index(px, py, pc):
    return 4 * px + 2 * py + pc


def _dma_sems(n, per):
    return [pltpu.SemaphoreType.DMA((n, per)), pltpu.SemaphoreType.DMA((n, per)), pltpu.SemaphoreType.DMA((n,))]


def _gather_rider(shards):
    n = len(shards)

    def copies(ins, outs, sems):
        send_sems, recv_sems, local_sems = sems
        x, y, c = _position()
        me, sibling = (x, y, c), (x, y, 1 - c)
        chips = [(1 - x, y), (x, 1 - y), (1 - x, 1 - y)]

        def copy(a, k, block, to, src=None):
            rows = outs[a].at[_index(*block)]
            return pltpu.make_async_remote_copy(
                src_ref=rows if src is None else src, dst_ref=rows,
                send_sem=send_sems.at[a, k], recv_sem=recv_sems.at[a, k], device_id=to, device_id_type=MESH)

        mine = [pltpu.make_async_copy(ins[a], outs[a].at[_index(*me)], local_sems.at[a]) for a in range(n)]
        first = []
        for a in range(n):
            first.append(copy(a, 0, me, sibling, src=ins[a]))
            first += [copy(a, 1 + j, me, (*chip, c), src=ins[a]) for j, chip in enumerate(chips)]
        return copy, mine, first, me, sibling, chips, c

    def start(ins, outs, sems):
        _, mine, first, *_ = copies(ins, outs, sems)
        for cp in mine + first:
            cp.start()

    def finish(ins, outs, sems):
        copy, mine, first, me, sibling, chips, c = copies(ins, outs, sems)
        passed = []
        for a in range(n):
            for j, chip in enumerate(chips):
                copy(a, 1 + j, (*chip, c), me).wait_recv()
                fwd = copy(a, 4 + j, (*chip, c), sibling)
                fwd.start()
                passed.append(fwd)
        for a in range(n):
            copy(a, 0, sibling, me).wait_recv()
            for j, chip in enumerate(chips):
                copy(a, 4 + j, (*chip, 1 - c), me).wait_recv()
        for cp in first + passed:
            cp.wait_send()
        for cp in mine:
            cp.wait()

    out_shapes = [jax.ShapeDtypeStruct((N_DEV,) + s.shape, s.dtype) for s in shards]
    return _Rider(shards, out_shapes, _dma_sems(n, 7), start, finish)


def _pair_rider(grads):
    n = len(grads)

    def copies(ins, outs, sems):
        send_sems, recv_sems, _ = sems
        x, y, c = _position()
        return [pltpu.make_async_remote_copy(
            src_ref=ins[a].at[2 * q + (1 - c)], dst_ref=outs[a].at[q],
            send_sem=send_sems.at[a, q], recv_sem=recv_sems.at[a, q], device_id=(x, y, 1 - c), device_id_type=MESH)
            for a in range(n) for q in range(N_CHIPS)]

    def start(ins, outs, sems):
        for cp in copies(ins, outs, sems):
            cp.start()

    def finish(ins, outs, sems):
        cps = copies(ins, outs, sems)
        for cp in cps:
            cp.wait_recv()
        for cp in cps:
            cp.wait_send()

    out_shapes = [jax.ShapeDtypeStruct((N_CHIPS,) + g.shape[1:], g.dtype) for g in grads]
    return _Rider(grads, out_shapes, _dma_sems(n, N_CHIPS), start, finish)


def _chip_rider(parts):
    n = len(parts)

    def copies(ins, outs, sems):
        send_sems, recv_sems, local_sems = sems
        x, y, c = _position()
        q_me = 2 * x + y
        chips = [(1 - x, y), (x, 1 - y), (1 - x, 1 - y)]
        mine = [pltpu.make_async_copy(ins[a].at[q_me], outs[a].at[q_me], local_sems.at[a]) for a in range(n)]
        sends, arrivals = [], []
        for a in range(n):
            for j, (tx, ty) in enumerate(chips):
                q_t = 2 * tx + ty
                sends.append(pltpu.make_async_remote_copy(
                    src_ref=ins[a].at[q_t], dst_ref=outs[a].at[q_me],
                    send_sem=send_sems.at[a, j], recv_sem=recv_sems.at[a, j], device_id=(tx, ty, c), device_id_type=MESH))
                arrivals.append(pltpu.make_async_remote_copy(
                    src_ref=ins[a].at[q_t], dst_ref=outs[a].at[q_t],
                    send_sem=send_sems.at[a, j], recv_sem=recv_sems.at[a, j], device_id=(tx, ty, c), device_id_type=MESH))
        return mine, sends, arrivals

    def start(ins, outs, sems):
        mine, sends, _ = copies(ins, outs, sems)
        for cp in mine + sends:
            cp.start()

    def finish(ins, outs, sems):
        mine, sends, arrivals = copies(ins, outs, sems)
        for cp in arrivals:
            cp.wait_recv()
        for cp in sends:
            cp.wait_send()
        for cp in mine:
            cp.wait()

    out_shapes = [jax.ShapeDtypeStruct(p.shape, p.dtype) for p in parts]
    return _Rider(parts, out_shapes, _dma_sems(n, 3), start, finish)


def _broadcast_rider(values):
    n = len(values)

    def copies(ins, outs, sems):
        send_sems, recv_sems, local_sems = sems
        x, y, c = _position()
        me = _index(x, y, c)

        def peer(k):
            return (1 - x if k & 4 else x, 1 - y if k & 2 else y, 1 - c if k & 1 else c)

        mine = [pltpu.make_async_copy(ins[a], outs[a].at[me], local_sems.at[a]) for a in range(n)]
        sends, arrivals = [], []
        for a in range(n):
            for k in range(1, N_DEV):
                common = dict(send_sem=send_sems.at[a, k - 1], recv_sem=recv_sems.at[a, k - 1], device_id=peer(k), device_id_type=MESH)
                sends.append(pltpu.make_async_remote_copy(src_ref=ins[a], dst_ref=outs[a].at[me], **common))
                arrivals.append(pltpu.make_async_remote_copy(src_ref=ins[a], dst_ref=outs[a].at[_index(*peer(k))], **common))
        return mine, sends, arrivals

    def start(ins, outs, sems):
        mine, sends, _ = copies(ins, outs, sems)
        for cp in mine + sends:
            cp.start()

    def finish(ins, outs, sems):
        mine, sends, arrivals = copies(ins, outs, sems)
        for cp in arrivals:
            cp.wait_recv()
        for cp in sends:
            cp.wait_send()
        for cp in mine:
            cp.wait()

    out_shapes = [jax.ShapeDtypeStruct((N_DEV,) + v.shape, v.dtype) for v in values]
    return _Rider(values, out_shapes, _dma_sems(n, 7), start, finish)


def _run_rider(name, rider):
    n_in, n_out = len(rider.ins), len(rider.out_shapes)

    def body(*refs):
        ins, outs, sems = refs[:n_in], refs[n_in:n_in + n_out], refs[n_in + n_out:]
        rider.start(ins, outs, sems)
        rider.finish(ins, outs, sems)

    return pl.pallas_call(
        body, name=name, in_specs=[ANY] * n_in, out_specs=[ANY] * n_out, out_shape=rider.out_shapes,
        scratch_shapes=rider.sem_shapes)(*rider.ins)


class _Host:
    def __init__(self, rider):
        self.rider = rider
        self.n_in = len(rider.ins) if rider else 0
        self.n_out = len(rider.out_shapes) if rider else 0
        self.n_sem = len(rider.sem_shapes) if rider else 0
        self.ins = rider.ins if rider else []
        self.in_specs = [ANY] * self.n_in
        self.out_specs = [ANY] * self.n_out
        self.out_shapes = rider.out_shapes if rider else []
        self.scratch = rider.sem_shapes if rider else []

    def run(self, first, last, ins, outs, sems, compute):
        if self.rider is None:
            compute()
            return

        @pl.when(first)
        def _():
            self.rider.start(ins, outs, sems)

        compute()

        @pl.when(last)
        def _():
            self.rider.finish(ins, outs, sems)


def _matmul(name, kind, a, b, *, tm, tn, tk, outs, epilogue=None, extras=(), out_blocks=False, rider=None, j_outer=False):
    if kind == "nn":
        (m, kdim), n = a.shape, b.shape[1]
    elif kind == "nt":
        (m, kdim), n = a.shape, b.shape[0]
    else:
        (kdim, m), n = a.shape, b.shape[1]
    if out_blocks:
        tn = min(tn, n // N_DEV)
    tm, tn, tk = _tile(m, tm), _tile(n, tn), _tile(kdim, tk)
    ni, nj, nk = m // tm, n // tn, kdim // tk

    def spec(shape, fn):
        return pl.BlockSpec(shape, (lambda g0, g1, k: fn(g1, g0, k)) if j_outer else fn)

    a_spec = spec((tk, tm), lambda i, j, k: (k, i)) if kind == "tn" else spec((tm, tk), lambda i, j, k: (i, k))
    b_spec = spec((tn, tk), lambda i, j, k: (j, k)) if kind == "nt" else spec((tk, tn), lambda i, j, k: (k, j))
    dn = {"nn": NN, "nt": NT, "tn": TN}[kind]

    tile_spec = spec((tm, tn), lambda i, j, k: (i, j))
    if out_blocks:
        width = n // N_DEV
        r_out = width // tn
        out_shape = [jax.ShapeDtypeStruct((N_DEV, m, width), dt) for dt in outs]
        out_specs = [spec((None, tm, tn), lambda i, j, k: (j // r_out, i, j % r_out)) for _ in outs]
    else:
        out_shape = [jax.ShapeDtypeStruct((m, n), dt) for dt in outs]
        out_specs = [tile_spec for _ in outs]
    n_ex, n_out = len(extras), len(outs)
    host = _Host(rider)
    n_acc = 1 if nk > 1 else 0

    def body(*refs):
        a_ref, b_ref = refs[0], refs[1]
        pos = 2
        ex_refs = refs[pos:pos + n_ex]; pos += n_ex
        r_ins = refs[pos:pos + host.n_in]; pos += host.n_in
        out_refs = refs[pos:pos + n_out]; pos += n_out
        r_outs = refs[pos:pos + host.n_out]; pos += host.n_out
        acc_ref = refs[pos] if n_acc else None
        sems = refs[pos + n_acc:]
        i, j, k = pl.program_id(1 if j_outer else 0), pl.program_id(0 if j_outer else 1), pl.program_id(2)

        def finish_tile(acc):
            vals = (acc,) if epilogue is None else epilogue(acc, *[e[...] for e in ex_refs])
            for o_ref, v in zip(out_refs, vals):
                o_ref[...] = v.astype(o_ref.dtype)

        def compute():
            part = _dot(a_ref[...], b_ref[...], dn)
            if nk == 1:
                finish_tile(part)
                return

            @pl.when(k == 0)
            def _():
                acc_ref[...] = part

            @pl.when(jnp.logical_and(k > 0, k < nk - 1))
            def _():
                acc_ref[...] += part

            @pl.when(k == nk - 1)
            def _():
                finish_tile(acc_ref[...] + part)

        first = jnp.logical_and(jnp.logical_and(i == 0, j == 0), k == 0)
        last = jnp.logical_and(jnp.logical_and(i == ni - 1, j == nj - 1), k == nk - 1)
        host.run(first, last, r_ins, r_outs, sems, compute)

    sem = ("arbitrary",) * 3 if rider else ("parallel", "parallel", "arbitrary")
    res = pl.pallas_call(
        body,
        name=name,
        grid=(nj, ni, nk) if j_outer else (ni, nj, nk),
        in_specs=[a_spec, b_spec] + [tile_spec for _ in extras] + host.in_specs,
        out_specs=out_specs + host.out_specs,
        out_shape=out_shape + host.out_shapes,
        scratch_shapes=([pltpu.VMEM((tm, tn), F32)] if n_acc else []) + host.scratch,
        compiler_params=_params(sem),
    )(a, b, *extras, *host.ins)
    return res[0] if len(res) == 1 else res


def _rms_fwd(name, x, g, tm=512):
    t, d = x.shape
    tm = _tile(t, tm)

    def body(x_ref, g_ref, h_ref):
        xf = x_ref[...]
        r = lax.rsqrt(jnp.mean(xf * xf, axis=-1, keepdims=True) + EPS)
        h_ref[...] = (xf * r * g_ref[...]).astype(h_ref.dtype)

    return pl.pallas_call(
        body,
        name=name,
        grid=(t // tm,),
        in_specs=[pl.BlockSpec((tm, d), lambda i: (i, 0)), pl.BlockSpec((1, d), lambda i: (0, 0))],
        out_specs=pl.BlockSpec((tm, d), lambda i: (i, 0)),
        out_shape=jax.ShapeDtypeStruct((t, d), BF16),
        compiler_params=_params(("parallel",)),
    )(x, g.reshape(1, d))


def _rms_bwd(name, dh, x, g, res=None, tm=256):
    t, d = x.shape
    tm = _tile(t, tm)
    has_res = res is not None

    def body(*refs):
        if has_res:
            dh_ref, x_ref, g_ref, res_ref, dx_ref, dxb_ref, gg_ref, ss_ref = refs
        else:
            dh_ref, x_ref, g_ref, dx_ref, dxb_ref, gg_ref, ss_ref = refs
        i = pl.program_id(0)
        xf = x_ref[...]
        r = lax.rsqrt(jnp.mean(xf * xf, axis=-1, keepdims=True) + EPS)
        xh = xf * r
        dhf = dh_ref[...].astype(F32)
        dxh = dhf * g_ref[...]
        dx = r * (dxh - xh * jnp.mean(dxh * xh, axis=-1, keepdims=True))

        @pl.when(i == 0)
        def _():
            gg_ref[...] = jnp.zeros_like(gg_ref)
            ss_ref[...] = jnp.zeros_like(ss_ref)

        if has_res:
            resf = res_ref[...]
            dx = dx + resf
            ss_ref[...] += jnp.sum(jnp.sum(resf * resf, axis=0, keepdims=True), axis=1, keepdims=True)
        dx_ref[...] = dx
        dxb_ref[...] = dx.astype(BF16)
        gg_ref[...] += jnp.sum(dhf * xh, axis=0, keepdims=True)

    row = pl.BlockSpec((tm, d), lambda i: (i, 0))
    vec = pl.BlockSpec((1, d), lambda i: (0, 0))
    one = pl.BlockSpec((1, 1), lambda i: (0, 0))
    ins = [dh, x, g.reshape(1, d)] + ([res] if has_res else [])
    dx, dxb, gg, ss = pl.pallas_call(
        body,
        name=name,
        grid=(t // tm,),
        in_specs=[row, row, vec] + ([row] if has_res else []),
        out_specs=[row, row, vec, one],
        out_shape=[jax.ShapeDtypeStruct((t, d), F32), jax.ShapeDtypeStruct((t, d), BF16), jax.ShapeDtypeStruct((1, d), F32),
                   jax.ShapeDtypeStruct((1, 1), F32)],
        compiler_params=_params(("arbitrary",)),
    )(*ins)
    return dx, dxb, gg.reshape(d), ss[0, 0]


def _head_rms(xf):
    r = lax.rsqrt(jnp.mean(xf * xf, axis=-1, keepdims=True) + EPS)
    return xf * r, r


def _head_rms_bwd(dy, xn, r, g):
    dxh = dy * g
    dx = r * (dxh - xn * jnp.mean(dxh * xn, axis=-1, keepdims=True))
    return dx, jnp.sum(dy * xn, axis=0, keepdims=True)


def _col_to_row(col):
    n = col.shape[0]
    eye = lax.broadcasted_iota(jnp.int32, (n, n), 0) == lax.broadcasted_iota(jnp.int32, (n, n), 1)
    return jnp.sum(jnp.where(eye, col, 0.0), axis=0, keepdims=True)


def _row_to_col(row):
    n = row.shape[1]
    eye = lax.broadcasted_iota(jnp.int32, (n, n), 0) == lax.broadcasted_iota(jnp.int32, (n, n), 1)
    return jnp.sum(jnp.where(eye, row, 0.0), axis=1, keepdims=True)


def _dproj_args(dproj, n_in):
    if dproj is None:
        return [], [], {}
    return [dproj], [ANY], {n_in: 0}


def _shift_down(u, s, rows):
    return jnp.where(rows >= s, pltpu.roll(u, s, axis=0), 0.0)


def _shift_up(u, s, rows, t):
    return jnp.where(rows < t - s, pltpu.roll(u, t - s, axis=0), 0.0)


def _conv_fwd(proj, off, conv_w, cb):
    t = proj.shape[0]
    c = conv_w.shape[1]
    blk0 = off // (3 * cb)

    def body(p_ref, w_ref, y_ref):
        rows = lax.broadcasted_iota(jnp.int32, (t, cb), 0)
        bg = p_ref[:, 0:cb].astype(F32)
        u = p_ref[:, cb:2 * cb].astype(F32) * p_ref[:, 2 * cb:3 * cb].astype(F32)
        w = w_ref[...]
        conv = w[2:3] * u + w[1:2] * _shift_down(u, 1, rows) + w[0:1] * _shift_down(u, 2, rows)
        y_ref[...] = (bg * conv).astype(y_ref.dtype)

    return pl.pallas_call(
        body,
        name="conv_fwd",
        grid=(c // cb,),
        in_specs=[pl.BlockSpec((t, 3 * cb), lambda j: (0, blk0 + j)), pl.BlockSpec((CONV_TAPS, cb), lambda j: (0, j))],
        out_specs=pl.BlockSpec((t, cb), lambda j: (0, j)),
        out_shape=jax.ShapeDtypeStruct((t, c), BF16),
        compiler_params=_params(("parallel",)),
    )(proj, conv_w)


def _conv_bwd(proj, off, conv_w, dy, cb, dproj, rider=None):
    t = proj.shape[0]
    c = conv_w.shape[1]
    blk0 = off // (3 * cb)
    nj = c // cb
    host = _Host(rider)

    def body(*refs):
        p_ref, w_ref, dy_ref = refs[:3]
        r_ins = refs[4:4 + host.n_in]
        dp_ref, gw_ref = refs[4 + host.n_in:6 + host.n_in]
        r_outs = refs[6 + host.n_in:6 + host.n_in + host.n_out]
        sems = refs[6 + host.n_in + host.n_out:]
        j = pl.program_id(0)

        def compute():
            rows = lax.broadcasted_iota(jnp.int32, (t, cb), 0)
            bg = p_ref[:, 0:cb].astype(F32)
            cg = p_ref[:, cb:2 * cb].astype(F32)
            v = p_ref[:, 2 * cb:3 * cb].astype(F32)
            u = cg * v
            w = w_ref[...]
            u1 = _shift_down(u, 1, rows)
            u2 = _shift_down(u, 2, rows)
            conv = w[2:3] * u + w[1:2] * u1 + w[0:1] * u2
            dyf = dy_ref[...].astype(F32)
            dconv = dyf * bg
            du = w[2:3] * dconv + w[1:2] * _shift_up(dconv, 1, rows, t) + w[0:1] * _shift_up(dconv, 2, rows, t)
            dp_ref[:, 0:cb] = (dyf * conv).astype(dp_ref.dtype)
            dp_ref[:, cb:2 * cb] = (du * v).astype(dp_ref.dtype)
            dp_ref[:, 2 * cb:3 * cb] = (du * cg).astype(dp_ref.dtype)
            gw_ref[0:1, :] = jnp.sum(dconv * u2, axis=0, keepdims=True)
            gw_ref[1:2, :] = jnp.sum(dconv * u1, axis=0, keepdims=True)
            gw_ref[2:3, :] = jnp.sum(dconv * u, axis=0, keepdims=True)

        host.run(j == 0, j == nj - 1, r_ins, r_outs, sems, compute)

    res = pl.pallas_call(
        body,
        name="conv_bwd",
        grid=(nj,),
        in_specs=[
            pl.BlockSpec((t, 3 * cb), lambda j: (0, blk0 + j)),
            pl.BlockSpec((CONV_TAPS, cb), lambda j: (0, j)),
            pl.BlockSpec((t, cb), lambda j: (0, j)),
            ANY,
        ] + host.in_specs,
        out_specs=[pl.BlockSpec((t, 3 * cb), lambda j: (0, blk0 + j)), pl.BlockSpec((CONV_TAPS, cb), lambda j: (0, j))] + host.out_specs,
        out_shape=[jax.ShapeDtypeStruct(dproj.shape, dproj.dtype), jax.ShapeDtypeStruct((CONV_TAPS, c), F32)] + host.out_shapes,
        input_output_aliases={3: 0},
        scratch_shapes=host.scratch,
        compiler_params=_params(("arbitrary",)),
    )(proj, conv_w, dy, dproj, *host.ins)
    return res


def _lane_scan(x, reverse):
    lane = lax.broadcasted_iota(jnp.int32, x.shape, 1)
    s = 1
    while s < LANES:
        if reverse:
            x = x + jnp.where(lane < LANES - s, pltpu.roll(x, LANES - s, axis=1), 0.0)
        else:
            x = x + jnp.where(lane >= s, pltpu.roll(x, s, axis=1), 0.0)
        s *= 2
    return x


def _scan_rows(src_ref, dst_ref, t, reverse, fn=None):
    groups = list(range(t // LANES))
    if reverse:
        groups = groups[::-1]
    carry = None
    for gi in groups:
        sl = slice(gi * LANES, (gi + 1) * LANES)
        blk = src_ref[:, sl]
        if fn is not None:
            blk = fn(blk)
        blk = _lane_scan(blk, reverse)
        if carry is not None:
            blk = blk + carry
        dst_ref[:, sl] = blk
        carry = blk[:, 0:1] if reverse else blk[:, LANES - 1:LANES]


def _forget_fwd(z_row, b_col):
    rows, t = z_row.shape

    def body(z_ref, b_ref, c_ref):
        def logf(z):
            zz = z + b_ref[...]
            return jnp.minimum(zz, 0.0) - jnp.log(1.0 + jnp.exp(-jnp.abs(zz)))

        _scan_rows(z_ref, c_ref, t, False, logf)

    return pl.pallas_call(
        body,
        name="forget_fwd",
        out_shape=jax.ShapeDtypeStruct((rows, t), F32),
        compiler_params=pltpu.CompilerParams(vmem_limit_bytes=VMEM_LIMIT),
    )(z_row, b_col)


def _rows_to_colb(c_row3, tq):
    heads, _, t = c_row3.shape

    def body(r_ref, o_ref):
        o_ref[...] = jnp.broadcast_to(_row_to_col(r_ref[...]), (tq, LANES))

    return pl.pallas_call(
        body,
        name="rows_to_colb",
        grid=(heads, t // tq),
        in_specs=[pl.BlockSpec((None, 1, tq), lambda h, i: (h, 0, i))],
        out_specs=pl.BlockSpec((None, tq, LANES), lambda h, i: (h, i, 0)),
        out_shape=jax.ShapeDtypeStruct((heads, t, LANES), F32),
        compiler_params=_params(("parallel", "parallel")),
    )(c_row3)


def _forget_bwd(z_row, b_col, dc_row):
    rows, t = z_row.shape

    def body(z_ref, b_ref, dc_ref, dz_ref, db_ref, tmp_ref):
        _scan_rows(dc_ref, tmp_ref, t, True)
        zz = z_ref[...] + b_ref[...]
        dz = tmp_ref[...] * (1.0 / (1.0 + jnp.exp(zz)))
        dz_ref[...] = dz.astype(dz_ref.dtype)
        db_ref[...] = jnp.sum(dz, axis=1, keepdims=True)

    return pl.pallas_call(
        body,
        name="forget_bwd",
        out_shape=[jax.ShapeDtypeStruct((rows, t), BF16), jax.ShapeDtypeStruct((rows, 1), F32)],
        scratch_shapes=[pltpu.VMEM((rows, t), F32)],
        compiler_params=pltpu.CompilerParams(vmem_limit_bytes=VMEM_LIMIT),
    )(z_row, b_col, dc_row)


def _fox_fwd(proj, off, gq, gk, c_row3, c_colb, heads, tq, rider=None):
    t = proj.shape[0]
    hd = FOX_HEAD_DIM
    tq = _tile(t, tq)
    nq = t // tq
    blk0 = off // hd
    scale = 1.0 / math.sqrt(hd)
    host = _Host(rider)

    def body(*refs):
        q_ref, k_ref, v_ref, gq_ref, gk_ref, crow_ref, ccol_ref = refs[:7]
        r_ins = refs[7:7 + host.n_in]
        o_ref, lse_ref = refs[7 + host.n_in:9 + host.n_in]
        r_outs = refs[9 + host.n_in:9 + host.n_in + host.n_out]
        khat_ref, v_t_ref = refs[9 + host.n_in + host.n_out:11 + host.n_in + host.n_out]
        sems = refs[11 + host.n_in + host.n_out:]
        h, qi = pl.program_id(0), pl.program_id(1)

        def compute():
            eye = (lax.broadcasted_iota(jnp.int32, (hd, hd), 0) == lax.broadcasted_iota(jnp.int32, (hd, hd), 1)).astype(BF16)

            @pl.when(qi == 0)
            def _():
                kn, _ = _head_rms(k_ref[...].astype(F32))
                khat_ref[...] = (kn * gk_ref[...]).astype(BF16)
                v_t_ref[...] = _dot(eye, v_ref[...], NT).astype(BF16)

            qn, _ = _head_rms(q_ref[...].astype(F32))
            qhat = (qn * (gq_ref[...] * scale)).astype(BF16)
            crow = crow_ref[:, pl.ds(pl.multiple_of(qi * tq, tq), tq)]
            above = lax.broadcasted_iota(jnp.int32, (tq, tq), 1) >= lax.broadcasted_iota(jnp.int32, (tq, tq), 0)

            def tile(j, carry, diagonal):
                m, l, acc_t = carry
                ks = pl.multiple_of(j * tq, tq)
                s_t = _dot(khat_ref[pl.ds(ks, tq), :], qhat, NT) - ccol_ref[pl.ds(ks, tq), 0:1]
                if diagonal:
                    s_t = jnp.where(above, s_t, NEG)
                m_new = jnp.maximum(m, jnp.max(s_t, axis=0, keepdims=True) + crow)
                alpha = jnp.exp(m - m_new)
                p_t = jnp.exp(s_t + (crow - m_new))
                l = alpha * l + jnp.sum(p_t, axis=0, keepdims=True)
                acc_t = alpha * acc_t + _dot(v_t_ref[:, pl.ds(ks, tq)], p_t.astype(BF16), NN)
                return m_new, l, acc_t

            init = (jnp.full((1, tq), NEG, F32), jnp.zeros((1, tq), F32), jnp.zeros((hd, tq), F32))
            carry = lax.fori_loop(0, qi, lambda j, c: tile(j, c, False), init)
            m, l, acc_t = tile(qi, carry, True)
            o_ref[...] = _dot((acc_t / l).astype(BF16), eye, TN).astype(o_ref.dtype)
            lse_ref[...] = m + jnp.log(l)

        first = jnp.logical_and(h == 0, qi == 0)
        last = jnp.logical_and(h == heads - 1, qi == nq - 1)
        host.run(first, last, r_ins, r_outs, sems, compute)

    res = pl.pallas_call(
        body,
        name="fox_fwd",
        grid=(heads, nq),
        in_specs=[
            pl.BlockSpec((tq, hd), lambda h, i: (i, blk0 + 3 * h)),
            pl.BlockSpec((t, hd), lambda h, i: (0, blk0 + 3 * h + 1)),
            pl.BlockSpec((t, hd), lambda h, i: (0, blk0 + 3 * h + 2)),
            pl.BlockSpec((1, hd), lambda h, i: (0, 0)),
            pl.BlockSpec((1, hd), lambda h, i: (0, 0)),
            pl.BlockSpec((None, 1, t), lambda h, i: (h, 0, 0)),
            pl.BlockSpec((None, t, LANES), lambda h, i: (h, 0, 0)),
        ] + host.in_specs,
        out_specs=[pl.BlockSpec((tq, hd), lambda h, i: (i, h)), pl.BlockSpec((None, 1, tq), lambda h, i: (h, 0, i))] + host.out_specs,
        out_shape=[jax.ShapeDtypeStruct((t, heads * hd), BF16), jax.ShapeDtypeStruct((heads, 1, t), F32)] + host.out_shapes,
        scratch_shapes=[pltpu.VMEM((t, hd), BF16), pltpu.VMEM((hd, t), BF16)] + host.scratch,
        compiler_params=_params(("arbitrary", "arbitrary")),
    )(proj, proj, proj, gq.reshape(1, hd), gk.reshape(1, hd), c_row3, c_colb, *host.ins)
    return res


def _fox_bwd(proj, off, o, do, gq, gk, c_row3, c_colb, lse, heads, tq, dproj, rider=None):
    t = proj.shape[0]
    hd = FOX_HEAD_DIM
    tq = _tile(t, tq)
    nb = t // tq
    blk0 = off // hd
    scale = 1.0 / math.sqrt(hd)
    host = _Host(rider)
    n_fixed_in = 11

    def body(*refs):
        q_ref, k_ref, v_ref, o_ref, do_ref, gq_ref, gk_ref, crow_ref, ccol_ref, lse_ref = refs[:10]
        pos = n_fixed_in
        r_ins = refs[pos:pos + host.n_in]; pos += host.n_in
        dp_ref, dc_ref, ggq_ref, ggk_ref = refs[pos:pos + 4]; pos += 4
        r_outs = refs[pos:pos + host.n_out]; pos += host.n_out
        qhat_ref, khat_ref, khat_t_ref, dq_t_ref, dk_ref, dcq_ref, dck_ref, delta_ref = refs[pos:pos + 8]; pos += 8
        sems = refs[pos:]
        h = pl.program_id(0)

        def compute():
            qn, rq = _head_rms(q_ref[...].astype(F32))
            qhat_ref[...] = (qn * (gq_ref[...] * scale)).astype(BF16)
            kn, rk = _head_rms(k_ref[...].astype(F32))
            khat_ref[...] = (kn * gk_ref[...]).astype(BF16)
            eye = (lax.broadcasted_iota(jnp.int32, (hd, hd), 0) == lax.broadcasted_iota(jnp.int32, (hd, hd), 1)).astype(BF16)
            khat_t_ref[...] = _dot(eye, khat_ref[...], NT).astype(BF16)
            delta = jnp.sum(do_ref[...].astype(F32) * o_ref[...].astype(F32), axis=-1, keepdims=True)
            for b in range(nb):
                sl = slice(b * tq, (b + 1) * tq)
                delta_ref[:, sl] = _col_to_row(delta[sl, :])
            dq_t_ref[...] = jnp.zeros_like(dq_t_ref)
            dcq_ref[...] = jnp.zeros_like(dcq_ref)
            above = lax.broadcasted_iota(jnp.int32, (tq, tq), 1) >= lax.broadcasted_iota(jnp.int32, (tq, tq), 0)

            def kv_block(j, _):
                ks = pl.multiple_of(j * tq, tq)
                kh = khat_ref[pl.ds(ks, tq), :]
                kh_t = khat_t_ref[:, pl.ds(ks, tq)]
                vv = v_ref[pl.ds(ks, tq), :]
                ccol = ccol_ref[pl.ds(ks, tq), 0:1]

                def q_block(i, carry, diagonal):
                    dk, dv, dck = carry
                    qs = pl.multiple_of(i * tq, tq)
                    qh = qhat_ref[pl.ds(qs, tq), :]
                    dob = do_ref[pl.ds(qs, tq), :]
                    s_t = _dot(kh, qh, NT) + ((crow_ref[:, pl.ds(qs, tq)] - lse_ref[:, pl.ds(qs, tq)]) - ccol)
                    p_t = jnp.exp(s_t)
                    if diagonal:
                        p_t = jnp.where(above, p_t, 0.0)
                    ds_t = p_t * (_dot(vv, dob, NT) - delta_ref[:, pl.ds(qs, tq)])
                    dsb = ds_t.astype(BF16)
                    dv = dv + _dot(p_t.astype(BF16), dob, NN)
                    dk = dk + _dot(dsb, qh, NN)
                    dq_t_ref[:, pl.ds(qs, tq)] += _dot(kh_t, dsb, NN)
                    dcq_ref[:, pl.ds(qs, tq)] += jnp.sum(ds_t, axis=0, keepdims=True)
                    dck = dck + jnp.sum(ds_t, axis=-1, keepdims=True)
                    return dk, dv, dck

                zero = jnp.zeros((tq, hd), F32)
                carry = q_block(j, (zero, zero, jnp.zeros((tq, 1), F32)), True)
                dk, dv, dck = lax.fori_loop(j + 1, nb, lambda i, c: q_block(i, c, False), carry)
                dk_ref[pl.ds(ks, tq), :] = dk
                dp_ref[pl.ds(ks, tq), 2 * hd:3 * hd] = dv.astype(dp_ref.dtype)
                dck_ref[pl.ds(ks, tq), :] = dck
                return 0

            lax.fori_loop(0, nb, kv_block, 0)

            dq, ggq = _head_rms_bwd(dq_t_ref[...].T * scale, qn, rq, gq_ref[...])
            dk, ggk = _head_rms_bwd(dk_ref[...], kn, rk, gk_ref[...])
            dp_ref[:, 0:hd] = dq.astype(dp_ref.dtype)
            dp_ref[:, hd:2 * hd] = dk.astype(dp_ref.dtype)
            for b in range(nb):
                sl = slice(b * tq, (b + 1) * tq)
                dc_ref[:, sl] = dcq_ref[:, sl] - _col_to_row(dck_ref[sl, :])

            @pl.when(h == 0)
            def _():
                ggq_ref[...] = jnp.zeros_like(ggq_ref)
                ggk_ref[...] = jnp.zeros_like(ggk_ref)

            ggq_ref[...] += ggq
            ggk_ref[...] += ggk

        host.run(h == 0, h == heads - 1, r_ins, r_outs, sems, compute)

    head_in = lambda part: pl.BlockSpec((t, hd), lambda h: (0, blk0 + 3 * h + part))
    vec = pl.BlockSpec((1, hd), lambda h: (0, 0))
    colb = pl.BlockSpec((None, t, LANES), lambda h: (h, 0, 0))
    res = pl.pallas_call(
        body,
        name="fox_bwd",
        grid=(heads,),
        in_specs=[
            head_in(0), head_in(1), head_in(2),
            pl.BlockSpec((t, hd), lambda h: (0, h)),
            pl.BlockSpec((t, hd), lambda h: (0, h)),
            vec, vec,
            pl.BlockSpec((None, 1, t), lambda h: (h, 0, 0)),
            colb,
            pl.BlockSpec((None, 1, t), lambda h: (h, 0, 0)),
            ANY,
        ] + host.in_specs,
        out_specs=[
            pl.BlockSpec((t, 3 * hd), lambda h: (0, blk0 // 3 + h)),
            pl.BlockSpec((None, 1, t), lambda h: (h, 0, 0)),
            vec, vec,
        ] + host.out_specs,
        out_shape=[
            jax.ShapeDtypeStruct(dproj.shape, dproj.dtype),
            jax.ShapeDtypeStruct((heads, 1, t), F32),
            jax.ShapeDtypeStruct((1, hd), F32),
            jax.ShapeDtypeStruct((1, hd), F32),
        ] + host.out_shapes,
        input_output_aliases={10: 0},
        scratch_shapes=[
            pltpu.VMEM((t, hd), BF16), pltpu.VMEM((t, hd), BF16), pltpu.VMEM((hd, t), BF16),
            pltpu.VMEM((hd, t), F32), pltpu.VMEM((t, hd), F32),
            pltpu.VMEM((1, t), F32), pltpu.VMEM((t, 1), F32), pltpu.VMEM((1, t), F32),
        ] + host.scratch,
        compiler_params=_params(("arbitrary",)),
    )(proj, proj, proj, o, do, gq.reshape(1, hd), gk.reshape(1, hd), c_row3, c_colb, lse, dproj, *host.ins)
    return res


def _mem_fwd(proj, off, kv, gq, gk, tq):
    t = proj.shape[0]
    m, width = kv.shape[0], kv.shape[1] // 2
    hd = width // MEM_HEADS
    tq = _tile(t, tq)
    blk0 = off // hd
    scale = 1.0 / math.sqrt(hd)

    def body(q_ref, k_ref, v_ref, gq_ref, gk_ref, o_ref):
        qn, _ = _head_rms(q_ref[...].astype(F32))
        kn, _ = _head_rms(k_ref[...])
        s = _dot((qn * gq_ref[...]).astype(BF16), (kn * gk_ref[...]).astype(BF16), NT) * scale
        p = jnp.exp(s - jnp.max(s, axis=-1, keepdims=True))
        p = p / jnp.sum(p, axis=-1, keepdims=True)
        o_ref[...] = _dot(p.astype(BF16), v_ref[...].astype(BF16), NN).astype(o_ref.dtype)

    vec = pl.BlockSpec((1, hd), lambda h, i: (0, 0))
    return pl.pallas_call(
        body,
        name="mem_fwd",
        grid=(MEM_HEADS, t // tq),
        in_specs=[
            pl.BlockSpec((tq, hd), lambda h, i: (i, blk0 + h)),
            pl.BlockSpec((m, hd), lambda h, i: (0, h)),
            pl.BlockSpec((m, hd), lambda h, i: (0, MEM_HEADS + h)),
            vec, vec,
        ],
        out_specs=pl.BlockSpec((tq, hd), lambda h, i: (i, h)),
        out_shape=jax.ShapeDtypeStruct((t, width), BF16),
        compiler_params=_params(("parallel", "parallel")),
    )(proj, kv, kv, gq.reshape(1, hd), gk.reshape(1, hd))


def _mem_bwd(proj, off, kv, do, gq, gk, tq, dproj, rider=None):
    t = proj.shape[0]
    m, width = kv.shape[0], kv.shape[1] // 2
    hd = width // MEM_HEADS
    tq = _tile(t, tq)
    nq = t // tq
    blk0 = off // hd
    scale = 1.0 / math.sqrt(hd)
    host = _Host(rider)

    def body(*refs):
        q_ref, k_ref, v_ref, do_ref, gq_ref, gk_ref = refs[:6]
        pos = 7
        r_ins = refs[pos:pos + host.n_in]; pos += host.n_in
        dq_ref, dk_ref, dv_ref, ggq_ref, ggk_ref = refs[pos:pos + 5]; pos += 5
        r_outs = refs[pos:pos + host.n_out]; pos += host.n_out
        dkh_ref, dvh_ref = refs[pos:pos + 2]; pos += 2
        sems = refs[pos:]
        h, i = pl.program_id(0), pl.program_id(1)

        def compute():
            qn, rq = _head_rms(q_ref[...].astype(F32))
            kn, rk = _head_rms(k_ref[...])
            qhat = (qn * gq_ref[...]).astype(BF16)
            khat = (kn * gk_ref[...]).astype(BF16)
            vb = v_ref[...].astype(BF16)
            dob = do_ref[...]
            s = _dot(qhat, khat, NT) * scale
            p = jnp.exp(s - jnp.max(s, axis=-1, keepdims=True))
            p = p / jnp.sum(p, axis=-1, keepdims=True)
            dp = _dot(dob, vb, NT)
            ds = p * (dp - jnp.sum(dp * p, axis=-1, keepdims=True))
            dsb = ds.astype(BF16)
            dq, ggq = _head_rms_bwd(_dot(dsb, khat, NN) * scale, qn, rq, gq_ref[...])
            dq_ref[...] = dq.astype(dq_ref.dtype)

            @pl.when(i == 0)
            def _():
                dkh_ref[...] = jnp.zeros_like(dkh_ref)
                dvh_ref[...] = jnp.zeros_like(dvh_ref)

            @pl.when(jnp.logical_and(h == 0, i == 0))
            def _():
                ggq_ref[...] = jnp.zeros_like(ggq_ref)
                ggk_ref[...] = jnp.zeros_like(ggk_ref)

            dkh_ref[...] += _dot(dsb, qhat, TN)
            dvh_ref[...] += _dot(p.astype(BF16), dob, TN)
            ggq_ref[...] += ggq

            @pl.when(i == nq - 1)
            def _():
                dk, ggk = _head_rms_bwd(dkh_ref[...] * scale, kn, rk, gk_ref[...])
                dk_ref[...] = dk.astype(dk_ref.dtype)
                dv_ref[...] = dvh_ref[...].astype(dv_ref.dtype)
                ggk_ref[...] += ggk

        first = jnp.logical_and(h == 0, i == 0)
        last = jnp.logical_and(h == MEM_HEADS - 1, i == nq - 1)
        host.run(first, last, r_ins, r_outs, sems, compute)

    vec = pl.BlockSpec((1, hd), lambda h, i: (0, 0))
    kblk = pl.BlockSpec((m, hd), lambda h, i: (0, h))
    res = pl.pallas_call(
        body,
        name="mem_bwd",
        grid=(MEM_HEADS, nq),
        in_specs=[
            pl.BlockSpec((tq, hd), lambda h, i: (i, blk0 + h)), kblk,
            pl.BlockSpec((m, hd), lambda h, i: (0, MEM_HEADS + h)),
            pl.BlockSpec((tq, hd), lambda h, i: (i, h)), vec, vec, ANY,
        ] + host.in_specs,
        out_specs=[pl.BlockSpec((tq, hd), lambda h, i: (i, blk0 + h)), kblk, kblk, vec, vec] + host.out_specs,
        out_shape=[
            jax.ShapeDtypeStruct(dproj.shape, dproj.dtype),
            jax.ShapeDtypeStruct((m, width), BF16),
            jax.ShapeDtypeStruct((m, width), BF16),
            jax.ShapeDtypeStruct((1, hd), F32),
            jax.ShapeDtypeStruct((1, hd), F32),
        ] + host.out_shapes,
        input_output_aliases={6: 0},
        scratch_shapes=[pltpu.VMEM((m, hd), F32), pltpu.VMEM((m, hd), F32)] + host.scratch,
        compiler_params=_params(("arbitrary", "arbitrary")),
    )(proj, kv, kv, do, gq.reshape(1, hd), gk.reshape(1, hd), dproj, *host.ins)
    dproj, dk, dv, ggq, ggk = res[:5]
    return (dproj, jnp.concatenate([dk, dv], axis=1), ggq.reshape(hd), ggk.reshape(hd), *res[5:])


def _sigmoid(z):
    return 1.0 / (1.0 + jnp.exp(-z))


def _merge_fwd(proj, o3, tm, tc):
    t, d = o3[0].shape
    tm = _tile(t, tm)

    def body(g_ref, oa_ref, ob_ref, oc_ref, out_ref):
        acc = jnp.zeros((tm, tc), F32)
        for s, o_ref in enumerate((oa_ref, ob_ref, oc_ref)):
            acc = acc + _sigmoid(g_ref[:, s * tc:(s + 1) * tc].astype(F32)) * o_ref[...].astype(F32)
        out_ref[...] = acc.astype(out_ref.dtype)

    blk = pl.BlockSpec((tm, tc), lambda i, j: (i, j))
    return pl.pallas_call(
        body,
        name="merge_fwd",
        grid=(t // tm, d // tc),
        in_specs=[pl.BlockSpec((tm, 3 * tc), lambda i, j: (i, j)), blk, blk, blk],
        out_specs=blk,
        out_shape=jax.ShapeDtypeStruct((t, d), BF16),
        compiler_params=_params(("parallel", "parallel")),
    )(proj, *o3)


def _merge_bwd(proj, o3, dm, tm, tc):
    t, d = dm.shape
    tm = _tile(t, tm)

    def body(g_ref, oa_ref, ob_ref, oc_ref, dm_ref, dg_ref, da_ref, db_ref, dc_ref):
        dmf = dm_ref[...].astype(F32)
        for s, (o_ref, do_ref) in enumerate(((oa_ref, da_ref), (ob_ref, db_ref), (oc_ref, dc_ref))):
            g = _sigmoid(g_ref[:, s * tc:(s + 1) * tc].astype(F32))
            do_ref[...] = (dmf * g).astype(do_ref.dtype)
            dg_ref[:, s * tc:(s + 1) * tc] = (dmf * o_ref[...].astype(F32) * g * (1.0 - g)).astype(dg_ref.dtype)

    blk = pl.BlockSpec((tm, tc), lambda i, j: (i, j))
    wide = pl.BlockSpec((tm, 3 * tc), lambda i, j: (i, j))
    return pl.pallas_call(
        body,
        name="merge_bwd",
        grid=(t // tm, d // tc),
        in_specs=[wide, blk, blk, blk, blk],
        out_specs=[wide, blk, blk, blk],
        out_shape=[jax.ShapeDtypeStruct(proj.shape, BF16)] + [jax.ShapeDtypeStruct((t, d), BF16)] * 3,
        compiler_params=_params(("parallel", "parallel")),
    )(proj, *o3, dm)


def _w_in_chunks(d, tc):
    cw = d // 2
    heads = cw // FOX_HEAD_DIM
    conv0, fox0, f0, mq0, gate0 = 0, 3 * cw, 6 * cw, 6 * cw + heads, 7 * cw + heads
    chunks = [(gate0 + s * d + j * tc, gate0 + s * d + (j + 1) * tc) for j in range(d // tc) for s in range(N_BRANCHES)]
    chunks += [(conv0 + s * cw + j * LANES, conv0 + s * cw + (j + 1) * LANES) for j in range(cw // LANES) for s in range(3)]
    chunks += [(fox0 + s * cw + j * FOX_HEAD_DIM, fox0 + s * cw + (j + 1) * FOX_HEAD_DIM) for j in range(heads) for s in range(3)]
    chunks.append((mq0, mq0 + cw))
    return chunks, (f0, f0 + heads)


def _pack_w_in(w_in_t, d, tc):
    cw = d // 2
    heads = cw // FOX_HEAD_DIM
    k = w_in_t.shape[1]
    o = 0
    conv = w_in_t[o:o + 3 * cw]; o += 3 * cw
    fox = w_in_t[o:o + 3 * cw]; o += 3 * cw
    f = w_in_t[o:o + heads]; o += heads
    mq = w_in_t[o:o + cw]; o += cw
    gate = w_in_t[o:o + N_BRANCHES * d]
    conv = conv.reshape(3, cw // LANES, LANES, k).transpose(1, 0, 2, 3).reshape(3 * cw, k)
    fox = fox.reshape(3, heads, FOX_HEAD_DIM, k).transpose(1, 0, 2, 3).reshape(3 * cw, k)
    gate = gate.reshape(N_BRANCHES, d // tc, tc, k).transpose(1, 0, 2, 3).reshape(N_BRANCHES * d, k)
    return jnp.concatenate([gate, conv, fox, mq], axis=0), jnp.pad(f, ((0, F_ROWS - heads), (0, 0)))


def _unpack_g_in(g_all, g_f, d, tc):
    cw = d // 2
    heads = cw // FOX_HEAD_DIM
    k = g_all.shape[1]
    o = 0
    gate = g_all[o:o + N_BRANCHES * d]; o += N_BRANCHES * d
    conv = g_all[o:o + 3 * cw]; o += 3 * cw
    fox = g_all[o:o + 3 * cw]; o += 3 * cw
    mq = g_all[o:o + cw]
    conv = conv.reshape(cw // LANES, 3, LANES, k).transpose(1, 0, 2, 3).reshape(3 * cw, k)
    fox = fox.reshape(heads, 3, FOX_HEAD_DIM, k).transpose(1, 0, 2, 3).reshape(3 * cw, k)
    gate = gate.reshape(d // tc, N_BRANCHES, tc, k).transpose(1, 0, 2, 3).reshape(N_BRANCHES * d, k)
    return jnp.concatenate([conv, fox, g_f[:heads], mq, gate], axis=0)


def _unblock(w8):
    return w8.transpose(1, 0, 2).reshape(w8.shape[1], -1)


def _tile2(r, cols, tr, tcols):
    if r % 8 == 0:
        return _tile(r, tr), cols
    return r, _tile(cols, tcols)


def _pair_sum(name, g8, got, c):
    def body(c_ref, g_ref, s_ref, o_ref):
        o_ref[...] = (g_ref[...].astype(F32) + s_ref[...].astype(F32)).astype(o_ref.dtype)

    if g8.ndim == 4:
        _, r, k1, k2 = g8.shape
        tr = max(cand for cand in range(1, 385) if r % cand == 0)
        grid = (N_CHIPS, r // tr)
        shape = (None, tr, k1, k2)
        own = pl.BlockSpec(shape, lambda q, i, c_ref: (2 * q + c_ref[0], i, 0, 0))
        blk = pl.BlockSpec(shape, lambda q, i, c_ref: (q, i, 0, 0))
    else:
        _, r, cols = g8.shape
        tr, tcols = _tile2(r, cols, 256, 256)
        grid = (N_CHIPS, r // tr, cols // tcols)
        own = pl.BlockSpec((None, tr, tcols), lambda q, i, j, c_ref: (2 * q + c_ref[0], i, j))
        blk = pl.BlockSpec((None, tr, tcols), lambda q, i, j, c_ref: (q, i, j))
    return pl.pallas_call(
        body,
        name=name,
        grid_spec=pltpu.PrefetchScalarGridSpec(num_scalar_prefetch=1, grid=grid, in_specs=[own, blk], out_specs=blk),
        out_shape=jax.ShapeDtypeStruct((N_CHIPS,) + g8.shape[1:], BF16),
        compiler_params=_params(("parallel",) * len(grid)),
    )(c, g8, got)


def _local_step(x, mem, target, w, small, comm=None):
    t, d = x.shape
    cw = d // 2
    heads = cw // FOX_HEAD_DIM
    tc = min(512, d)
    tq = min(512, t)
    off_conv, off_fox, off_mq = 3 * d, 3 * d + 3 * cw, 3 * d + 6 * cw
    w = dict(w)
    w_all, w_f = _pack_w_in(w["w_in"], d, tc)
    big = dict(tm=1024, tn=512, tk=2048)
    wide_k = dict(tm=512, tn=1024, tk=4096)
    tall = dict(tm=2048, tn=512, tk=2048)

    h = _rms_fwd("rms1_fwd", x, small["norm1_g"])
    if comm:
        early = ("w_conv_out", "w_fox_out", "w_mem_out", "w_out", "w_mem_kv")
        proj, *got = _matmul("proj", "nt", h, w_all, outs=[BF16], rider=_gather_rider([comm["shards"][n] for n in early]), **tall)
        for n, val in zip(early, got):
            w[n] = _unblock(val) if n in COLUMN_SPLIT else val.reshape(-1, val.shape[-1])
    else:
        proj = _matmul("proj", "nt", h, w_all, outs=[BF16], **tall)
    z_row = _matmul("proj_f", "nt", w_f, h, outs=[F32], tm=F_ROWS, tn=512, tk=2048)

    y_conv = _conv_fwd(proj, off_conv, small["conv_w"], LANES)

    b_col = jnp.pad(small["b_f"], (0, F_ROWS - heads)).reshape(F_ROWS, 1)
    c_row3 = _forget_fwd(z_row, b_col)[:heads].reshape(heads, 1, t)
    c_colb = _rows_to_colb(c_row3, tq)
    if comm:
        y_fox, lse, got = _fox_fwd(proj, off_fox, small["fox_q_g"], small["fox_k_g"], c_row3, c_colb, heads, tq,
                                   rider=_gather_rider([comm["shards"]["w_up"]]))
        w["w_up"] = _unblock(got)
    else:
        y_fox, lse = _fox_fwd(proj, off_fox, small["fox_q_g"], small["fox_k_g"], c_row3, c_colb, heads, tq)

    nm = _rms_fwd("mem_rms_fwd", mem, small["mem_norm_g"])
    kv = _matmul("mem_kv", "nn", nm, w["w_mem_kv"], outs=[F32], tm=256, tn=512, tk=2048)
    y_mem = _mem_fwd(proj, off_mq, kv, small["mem_q_g"], small["mem_k_g"], tq)

    ys = (y_conv, y_fox, y_mem)
    w_outs = (w["w_conv_out"], w["w_fox_out"], w["w_mem_out"])
    o3 = [_matmul(f"branch_out{s}", "nn", ys[s], w_outs[s], outs=[BF16], **big) for s in range(3)]
    merged = _merge_fwd(proj, o3, 512, tc)
    x1 = _matmul("out_proj", "nn", merged, w["w_out"], outs=[F32], extras=[x],
                 epilogue=lambda acc, xr: (acc + xr,), **big)
    h2 = _rms_fwd("rms2_fwd", x1, small["norm2_g"])

    def up_epilogue(acc):
        return acc, jnp.square(jnp.maximum(acc, 0.0))

    if comm:
        up, act, got = _matmul("mlp_up", "nn", h2, w["w_up"], outs=[BF16, BF16], epilogue=up_epilogue,
                               rider=_gather_rider([comm["shards"]["w_down"]]), **big)
        w["w_down"] = got.reshape(-1, got.shape[-1])
    else:
        up, act = _matmul("mlp_up", "nn", h2, w["w_up"], outs=[BF16, BF16], epilogue=up_epilogue, **big)

    def loss_epilogue(acc, x1r, tr):
        dy = (acc + x1r - tr) * (1.0 / d)
        return dy, dy

    dy, dyb = _matmul("mlp_down", "nn", act, w["w_down"], outs=[F32, BF16], extras=[x1, target],
                      epilogue=loss_epilogue, **big)

    def dup_epilogue(acc, upr):
        return (acc * 2.0 * jnp.maximum(upr.astype(F32), 0.0),)

    def by_owner(g):
        return g.reshape(N_DEV, -1, g.shape[-1])

    g, parts = {}, {}
    g["w_down"] = _matmul("d_w_down", "tn", act, dyb, outs=[BF16], **wide_k)
    if comm:
        dup, got = _matmul("d_act", "nt", dyb, w["w_down"], outs=[BF16], extras=[up], epilogue=dup_epilogue,
                           rider=_pair_rider([by_owner(g["w_down"])]), **tall)
        pair = _pair_sum("pair_w_down", by_owner(g["w_down"]), got, comm["c"])
        g["w_up"], parts["w_down"] = _matmul("d_w_up", "tn", h2, dup, outs=[BF16], out_blocks=True,
                                             rider=_chip_rider([pair]), **wide_k)
        dh2, got = _matmul("d_h2", "nt", dup, w["w_up"], outs=[F32], rider=_pair_rider([g["w_up"]]), **tall)
        pair_up = _pair_sum("pair_w_up", g["w_up"], got, comm["c"])
    else:
        dup = _matmul("d_act", "nt", dyb, w["w_down"], outs=[BF16], extras=[up], epilogue=dup_epilogue, **tall)
        g["w_up"] = _matmul("d_w_up", "tn", h2, dup, outs=[BF16], out_blocks=True, **wide_k)
        dh2 = _matmul("d_h2", "nt", dup, w["w_up"], outs=[F32], **tall)
    dx1, dx1b, g_norm2, dy_sq = _rms_bwd("rms2_bwd", dh2, x1, small["norm2_g"], res=dy)
    loss = dy_sq * (0.5 * d)

    g["w_out"] = _matmul("d_w_out", "tn", merged, dx1b, outs=[BF16], **wide_k)
    dmerged = _matmul("d_merged", "nt", dx1b, w["w_out"], outs=[BF16], **tall)
    dproj, *do3 = _merge_bwd(proj, o3, dmerged, 512, tc)
    names = ("w_conv_out", "w_fox_out", "w_mem_out")
    dys = []
    for s in range(3):
        g[names[s]] = _matmul(f"d_w_branch{s}", "tn", ys[s], do3[s], outs=[BF16], out_blocks=True, **wide_k)
        dys.append(_matmul(f"d_branch{s}", "nt", do3[s], w_outs[s], outs=[BF16], **tall))

    dproj, dkv, g_mq, g_mk = _mem_bwd(proj, off_mq, kv, dys[2], small["mem_q_g"], small["mem_k_g"], tq, dproj)
    g["w_mem_kv"] = _matmul("d_w_mem_kv", "tn", nm, dkv, outs=[BF16], **wide_k)
    dnm = _matmul("d_mem_norm", "nt", dkv, w["w_mem_kv"], outs=[F32], tm=256, tn=512, tk=2048)
    _, _, g_mem_norm, _ = _rms_bwd("mem_rms_bwd", dnm, mem, small["mem_norm_g"])

    mid = ("w_out", "w_conv_out", "w_fox_out", "w_mem_out", "w_mem_kv")
    if comm:
        mid8 = [g[n] if n in names else by_owner(g[n]) for n in mid]
        dproj, g_conv_w, *got = _conv_bwd(proj, off_conv, small["conv_w"], dys[0], LANES, dproj, rider=_pair_rider(mid8))
        pairs = [pair_up] + [_pair_sum("pair_" + n, g8, s4, comm["c"]) for n, g8, s4 in zip(mid, mid8, got)]
        dproj, dc, g_fq, g_fk, *got = _fox_bwd(proj, off_fox, y_fox, dys[1], small["fox_q_g"], small["fox_k_g"], c_row3, c_colb,
                                               lse, heads, tq, dproj, rider=_chip_rider(pairs))
        parts.update(zip(("w_up",) + mid, got))
    else:
        dproj, g_conv_w = _conv_bwd(proj, off_conv, small["conv_w"], dys[0], LANES, dproj)
        dproj, dc, g_fq, g_fk = _fox_bwd(proj, off_fox, y_fox, dys[1], small["fox_q_g"], small["fox_k_g"], c_row3, c_colb,
                                         lse, heads, tq, dproj)
    dc_row = jnp.pad(dc.reshape(heads, t), ((0, F_ROWS - heads), (0, 0)))
    dz_row, db = _forget_bwd(z_row, b_col, dc_row)

    g_all = _matmul("d_w_in", "tn", dproj, h, outs=[BF16], j_outer=True, **wide_k)
    g_wf = _matmul("d_w_f", "nn", dz_row, h, outs=[BF16], tm=F_ROWS, tn=512, tk=4096)
    g["w_in"] = _unpack_g_in(g_all, g_wf, d, tc)
    dh = _matmul("d_h_f", "tn", dz_row, w_f, outs=[F32], tm=1024, tn=512, tk=F_ROWS)
    add_prev = lambda acc, prev: (acc + prev,)
    if comm:
        g_in8 = by_owner(g["w_in"])
        got = _run_rider("pair_exchange_w_in", _pair_rider([g_in8]))[0]
        pair = _pair_sum("pair_w_in", g_in8, got, comm["c"])
        dh, parts["w_in"] = _matmul("d_h", "nn", dproj, w_all, outs=[F32], extras=[dh], epilogue=add_prev,
                                    rider=_chip_rider([pair]), tm=1024, tn=512, tk=3328)
    else:
        dh = _matmul("d_h", "nn", dproj, w_all, outs=[F32], extras=[dh], epilogue=add_prev, tm=1024, tn=512, tk=3328)
    grad_x, _, g_norm1, _ = _rms_bwd("rms1_bwd", dh, x, small["norm1_g"], res=dx1)

    gs = dict(norm1_g=g_norm1, b_f=db[:heads, 0], conv_w=g_conv_w, fox_q_g=g_fq.reshape(-1), fox_k_g=g_fk.reshape(-1),
              mem_norm_g=g_mem_norm, mem_q_g=g_mq, mem_k_g=g_mk, norm2_g=g_norm2)
    return loss, grad_x, (parts if comm else g), gs


def _adamw_math(w, g, m, v):
    m = ADAM_B1 * m + (1.0 - ADAM_B1) * g
    v = ADAM_B2 * v + (1.0 - ADAM_B2) * jnp.square(g)
    m_hat = m / (1.0 - ADAM_B1 ** ADAM_STEP)
    v_hat = v / (1.0 - ADAM_B2 ** ADAM_STEP)
    delta = -ADAM_LR * (m_hat / (jnp.sqrt(v_hat) + ADAM_EPS) + ADAM_WD * w)
    return delta, m, v


def _adamw(name, parts, w, m, v):
    r, c = w.shape
    tr, tc = _tile2(r, c, 128, 256)
    n_parts = parts.shape[0]

    def body(p_ref, w_ref, m_ref, v_ref, g_ref, d_ref, nm_ref, nv_ref):
        g = p_ref[0].astype(F32)
        for s in range(1, n_parts):
            g = g + p_ref[s].astype(F32)
        delta, nm, nv = _adamw_math(w_ref[...], g, m_ref[...], v_ref[...])
        g_ref[...] = g
        d_ref[...] = delta
        nm_ref[...] = nm
        nv_ref[...] = nv

    blk = pl.BlockSpec((tr, tc), lambda i, j: (i, j))
    return pl.pallas_call(
        body,
        name=name,
        grid=(r // tr, c // tc),
        in_specs=[pl.BlockSpec((n_parts, tr, tc), lambda i, j: (0, i, j)), blk, blk, blk],
        out_specs=[blk] * 4,
        out_shape=[jax.ShapeDtypeStruct((r, c), F32)] * 4,
        compiler_params=_params(("parallel", "parallel")),
    )(parts, w, m, v)


def _sum_parts(name, parts):
    n_parts, r, c = parts.shape

    def body(p_ref, o_ref):
        acc = p_ref[0]
        for s in range(1, n_parts):
            acc = acc + p_ref[s]
        o_ref[...] = acc

    return pl.pallas_call(body, name=name, out_shape=jax.ShapeDtypeStruct((r, c), F32))(parts)


BIG = ("w_in", "w_mem_kv", "w_conv_out", "w_fox_out", "w_mem_out", "w_out", "w_up", "w_down")
COLUMN_SPLIT = ("w_in", "w_conv_out", "w_fox_out", "w_mem_out", "w_up")
SMALL = ("norm1_g", "b_f", "conv_w", "fox_q_g", "fox_k_g", "mem_norm_g", "mem_q_g", "mem_k_g", "norm2_g")
WEIGHTS = ("norm1_g", "w_in", "b_f", "conv_w", "fox_q_g", "fox_k_g", "mem_norm_g", "w_mem_kv", "mem_q_g", "mem_k_g",
           "w_conv_out", "w_fox_out", "w_mem_out", "w_out", "norm2_g", "w_up", "w_down")


def _pack(vectors):
    rows = []
    for vec in vectors:
        n = vec.shape[0]
        rows.append(jnp.pad(vec, (0, -n % LANES)).reshape(-1, LANES))
    out = jnp.concatenate(rows, axis=0)
    return jnp.pad(out, ((0, -out.shape[0] % 8), (0, 0)))


def _unpack(packed, sizes):
    out, row = [], 0
    for n in sizes:
        nr = -(-n // LANES)
        out.append(packed[row:row + nr].reshape(-1)[:n])
        row += nr
    return out


def kernel(x, mem, norm1_g, w_in, b_f, conv_w, fox_q_g, fox_k_g, mem_norm_g, w_mem_kv, mem_q_g, mem_k_g, w_conv_out, w_fox_out, w_mem_out, w_out, norm2_g, w_up, w_down, loss_target, m_norm1_g, m_w_in, m_b_f, m_conv_w, m_fox_q_g, m_fox_k_g, m_mem_norm_g, m_w_mem_kv, m_mem_q_g, m_mem_k_g, m_w_conv_out, m_w_fox_out, m_w_mem_out, m_w_out, m_norm2_g, m_w_up, m_w_down, v_norm1_g, v_w_in, v_b_f, v_conv_w, v_fox_q_g, v_fox_k_g, v_mem_norm_g, v_w_mem_kv, v_mem_q_g, v_mem_k_g, v_w_conv_out, v_w_fox_out, v_w_mem_out, v_w_out, v_norm2_g, v_w_up, v_w_down):
    args = dict(locals())
    wts = {n: args[n] for n in WEIGHTS}
    ms = {n: args["m_" + n] for n in WEIGHTS}
    vs = {n: args["v_" + n] for n in WEIGHTS}
    x_pos, y_pos, c_pos = _position()
    me = _index(x_pos, y_pos, c_pos)

    shards = {n: (wts[n].T if n == "w_in" else wts[n]).astype(BF16) for n in BIG}
    wi, cw8 = _run_rider("all_gather_first", _gather_rider([shards["w_in"], conv_w]))
    full = {"w_in": wi.reshape(-1, wi.shape[-1])}
    small = {n: wts[n] for n in SMALL}
    small["conv_w"] = _unblock(cw8)
    comm = {"shards": shards, "c": c_pos.astype(jnp.int32).reshape(1)}

    loss, grad_x, parts, gs = _local_step(x[0], mem[0], loss_target[0], full, small, comm)

    out_g, out_d, out_m, out_v = {}, {}, {}, {}
    for n in BIG:
        if n == "w_in":
            res = _adamw("adamw_" + n, parts[n], wts[n].T, ms[n].T, vs[n].T)
            out_g[n], out_d[n], out_m[n], out_v[n] = (r.T for r in res)
        else:
            out_g[n], out_d[n], out_m[n], out_v[n] = _adamw("adamw_" + n, parts[n], wts[n], ms[n], vs[n])

    small_sizes = [int(math.prod(gs[n].shape)) for n in SMALL]
    packed = _pack([gs[n].reshape(-1) for n in SMALL])
    gsum = _sum_parts("sum_small", _run_rider("exchange_small", _broadcast_rider([packed]))[0])
    gsmall = dict(zip(SMALL, _unpack(gsum, small_sizes)))
    cols = conv_w.shape[1]
    gsmall["conv_w"] = lax.dynamic_slice(gsmall["conv_w"].reshape(CONV_TAPS, -1), (0, me * cols), (CONV_TAPS, cols)).reshape(-1)
    pg, pw, pm, pv = (_pack([src[n].reshape(-1) for n in SMALL]) for src in (gsmall, wts, ms, vs))
    _, sd, sm, sv = _adamw("adamw_small", pg[None], pw, pm, pv)
    local_sizes = [int(math.prod(wts[n].shape)) for n in SMALL]
    for dst, src in ((out_d, sd), (out_m, sm), (out_v, sv)):
        for n, val in zip(SMALL, _unpack(src, local_sizes)):
            dst[n] = val.reshape(wts[n].shape)
    for n in SMALL:
        out_g[n] = gsmall[n].reshape(wts[n].shape)

    loss = lax.psum(loss, MESH_AXES)
    return (loss, grad_x[None], *[out_g[n] for n in WEIGHTS], *[out_d[n] for n in WEIGHTS],
            *[out_m[n] for n in WEIGHTS], *[out_v[n] for n in WEIGHTS])
```

```python
import math

import numpy as np
import jax
import jax.numpy as jnp
from jax import lax
from jax.experimental import pallas as pl
from jax.experimental.pallas import tpu as pltpu

F32 = jnp.float32
BF16 = jnp.bfloat16

EPS = 1e-6
N_DEV = 8
N_CHIPS = 4
FOX_HEAD_DIM = 128
MEM_HEADS = 4
CONV_TAPS = 3
N_BRANCHES = 3
F_ROWS = 16

ADAM_LR = 0.001
ADAM_B1 = 0.9
ADAM_B2 = 0.999
ADAM_EPS = 1e-08
ADAM_WD = 0.01
ADAM_STEP = 10

V7X_VMEM_BYTES = 64 * 1024 * 1024
VMEM_LIMIT = V7X_VMEM_BYTES * 3 // 4
LANES = 128
NEG = -1e30

MESH_AXES = ("x", "y", "c")
MESH = pl.DeviceIdType.MESH
ANY = pl.BlockSpec(memory_space=pl.ANY)

NN = (((1,), (0,)), ((), ()))
NT = (((1,), (1,)), ((), ()))
TN = (((0,), (0,)), ((), ()))


def _params(sem):
    return pltpu.CompilerParams(dimension_semantics=sem, vmem_limit_bytes=VMEM_LIMIT)


def _dot(a, b, dn):
    return lax.dot_general(a, b, dn, preferred_element_type=F32)


def _tile(n, t):
    if n <= t:
        return n
    for step in (LANES, 16):
        for cand in range(t - t % step, 0, -step):
            if n % cand == 0:
                return cand
    raise ValueError((n, t))


class _Rider:
    def __init__(self, ins, out_shapes, sem_shapes, start, finish):
        self.ins, self.out_shapes, self.sem_shapes = list(ins), list(out_shapes), list(sem_shapes)
        self.start, self.finish = start, finish


def _position():
    return lax.axis_index("x"), lax.axis_index("y"), lax.axis_index("c")


def _index(px, py, pc):
    return 4 * px + 2 * py + pc


def _dma_sems(n, per):
    return [pltpu.SemaphoreType.DMA((n, per)), pltpu.SemaphoreType.DMA((n, per)), pltpu.SemaphoreType.DMA((n,))]


def _gather_rider(shards):
    n = len(shards)

    def copies(ins, outs, sems):
        send_sems, recv_sems, local_sems = sems
        x, y, c = _position()
        me, sibling = (x, y, c), (x, y, 1 - c)
        chips = [(1 - x, y), (x, 1 - y), (1 - x, 1 - y)]

        def copy(a, k, block, to, src=None):
            rows = outs[a].at[_index(*block)]
            return pltpu.make_async_remote_copy(
                src_ref=rows if src is None else src, dst_ref=rows,
                send_sem=send_sems.at[a, k], recv_sem=recv_sems.at[a, k], device_id=to, device_id_type=MESH)

        mine = [pltpu.make_async_copy(ins[a], outs[a].at[_index(*me)], local_sems.at[a]) for a in range(n)]
        first = []
        for a in range(n):
            first.append(copy(a, 0, me, sibling, src=ins[a]))
            first += [copy(a, 1 + j, me, (*chip, c), src=ins[a]) for j, chip in enumerate(chips)]
        return copy, mine, first, me, sibling, chips, c

    def start(ins, outs, sems):
        _, mine, first, *_ = copies(ins, outs, sems)
        for cp in mine + first:
            cp.start()

    def finish(ins, outs, sems):
        copy, mine, first, me, sibling, chips, c = copies(ins, outs, sems)
        passed = []
        for a in range(n):
            for j, chip in enumerate(chips):
                copy(a, 1 + j, (*chip, c), me).wait_recv()
                fwd = copy(a, 4 + j, (*chip, c), sibling)
                fwd.start()
                passed.append(fwd)
        for a in range(n):
            copy(a, 0, sibling, me).wait_recv()
            for j, chip in enumerate(chips):
                copy(a, 4 + j, (*chip, 1 - c), me).wait_recv()
        for cp in first + passed:
            cp.wait_send()
        for cp in mine:
            cp.wait()

    out_shapes = [jax.ShapeDtypeStruct((N_DEV,) + s.shape, s.dtype) for s in shards]
    return _Rider(shards, out_shapes, _dma_sems(n, 7), start, finish)


def _pair_rider(grads):
    n = len(grads)

    def copies(ins, outs, sems):
        send_sems, recv_sems, _ = sems
        x, y, c = _position()
        return [pltpu.make_async_remote_copy(
            src_ref=ins[a].at[2 * q + (1 - c)], dst_ref=outs[a].at[q],
            send_sem=send_sems.at[a, q], recv_sem=recv_sems.at[a, q], device_id=(x, y, 1 - c), device_id_type=MESH)
            for a in range(n) for q in range(N_CHIPS)]

    def start(ins, outs, sems):
        for cp in copies(ins, outs, sems):
            cp.start()

    def finish(ins, outs, sems):
        cps = copies(ins, outs, sems)
        for cp in cps:
            cp.wait_recv()
        for cp in cps:
            cp.wait_send()

    out_shapes = [jax.ShapeDtypeStruct((N_CHIPS,) + g.shape[1:], g.dtype) for g in grads]
    return _Rider(grads, out_shapes, _dma_sems(n, N_CHIPS), start, finish)


def _chip_rider(parts):
    n = len(parts)

    def copies(ins, outs, sems):
        send_sems, recv_sems, local_sems = sems
        x, y, c = _position()
        q_me = 2 * x + y
        chips = [(1 - x, y), (x, 1 - y), (1 - x, 1 - y)]
        mine = [pltpu.make_async_copy(ins[a].at[q_me], outs[a].at[q_me], local_sems.at[a]) for a in range(n)]
        sends, arrivals = [], []
        for a in range(n):
            for j, (tx, ty) in enumerate(chips):
                q_t = 2 * tx + ty
                sends.append(pltpu.make_async_remote_copy(
                    src_ref=ins[a].at[q_t], dst_ref=outs[a].at[q_me],
                    send_sem=send_sems.at[a, j], recv_sem=recv_sems.at[a, j], device_id=(tx, ty, c), device_id_type=MESH))
                arrivals.append(pltpu.make_async_remote_copy(
                    src_ref=ins[a].at[q_t], dst_ref=outs[a].at[q_t],
                    send_sem=send_sems.at[a, j], recv_sem=recv_sems.at[a, j], device_id=(tx, ty, c), device_id_type=MESH))
        return mine, sends, arrivals

    def start(ins, outs, sems):
        mine, sends, _ = copies(ins, outs, sems)
        for cp in mine + sends:
            cp.start()

    def finish(ins, outs, sems):
        mine, sends, arrivals = copies(ins, outs, sems)
        for cp in arrivals:
            cp.wait_recv()
        for cp in sends:
            cp.wait_send()
        for cp in mine:
            cp.wait()

    out_shapes = [jax.ShapeDtypeStruct(p.shape, p.dtype) for p in parts]
    return _Rider(parts, out_shapes, _dma_sems(n, 3), start, finish)


def _broadcast_rider(values):
    n = len(values)

    def copies(ins, outs, sems):
        send_sems, recv_sems, local_sems = sems
        x, y, c = _position()
        me = _index(x, y, c)

        def peer(k):
            return (1 - x if k & 4 else x, 1 - y if k & 2 else y, 1 - c if k & 1 else c)

        mine = [pltpu.make_async_copy(ins[a], outs[a].at[me], local_sems.at[a]) for a in range(n)]
        sends, arrivals = [], []
        for a in range(n):
            for k in range(1, N_DEV):
                common = dict(send_sem=send_sems.at[a, k - 1], recv_sem=recv_sems.at[a, k - 1], device_id=peer(k), device_id_type=MESH)
                sends.append(pltpu.make_async_remote_copy(src_ref=ins[a], dst_ref=outs[a].at[me], **common))
                arrivals.append(pltpu.make_async_remote_copy(src_ref=ins[a], dst_ref=outs[a].at[_index(*peer(k))], **common))
        return mine, sends, arrivals

    def start(ins, outs, sems):
        mine, sends, _ = copies(ins, outs, sems)
        for cp in mine + sends:
            cp.start()

    def finish(ins, outs, sems):
        mine, sends, arrivals = copies(ins, outs, sems)
        for cp in arrivals:
            cp.wait_recv()
        for cp in sends:
            cp.wait_send()
        for cp in mine:
            cp.wait()

    out_shapes = [jax.ShapeDtypeStruct((N_DEV,) + v.shape, v.dtype) for v in values]
    return _Rider(values, out_shapes, _dma_sems(n, 7), start, finish)


def _run_rider(name, rider):
    n_in, n_out = len(rider.ins), len(rider.out_shapes)

    def body(*refs):
        ins, outs, sems = refs[:n_in], refs[n_in:n_in + n_out], refs[n_in + n_out:]
        rider.start(ins, outs, sems)
        rider.finish(ins, outs, sems)

    return pl.pallas_call(
        body, name=name, in_specs=[ANY] * n_in, out_specs=[ANY] * n_out, out_shape=rider.out_shapes,
        scratch_shapes=rider.sem_shapes)(*rider.ins)


class _Host:
    def __init__(self, rider):
        self.rider = rider
        self.n_in = len(rider.ins) if rider else 0
        self.n_out = len(rider.out_shapes) if rider else 0
        self.n_sem = len(rider.sem_shapes) if rider else 0
        self.ins = rider.ins if rider else []
        self.in_specs = [ANY] * self.n_in
        self.out_specs = [ANY] * self.n_out
        self.out_shapes = rider.out_shapes if rider else []
        self.scratch = rider.sem_shapes if rider else []

    def run(self, first, last, ins, outs, sems, compute):
        if self.rider is None:
            compute()
            return

        @pl.when(first)
        def _():
            self.rider.start(ins, outs, sems)

        compute()

        @pl.when(last)
        def _():
            self.rider.finish(ins, outs, sems)


def _matmul(name, kind, a, b, *, tm, tn, tk, outs, epilogue=None, extras=(), out_blocks=False, rider=None, j_outer=False):
    if kind == "nn":
        (m, kdim), n = a.shape, b.shape[1]
    elif kind == "nt":
        (m, kdim), n = a.shape, b.shape[0]
    else:
        (kdim, m), n = a.shape, b.shape[1]
    if out_blocks:
        tn = min(tn, n // N_DEV)
    tm, tn, tk = _tile(m, tm), _tile(n, tn), _tile(kdim, tk)
    ni, nj, nk = m // tm, n // tn, kdim // tk

    def spec(shape, fn):
        return pl.BlockSpec(shape, (lambda g0, g1, k: fn(g1, g0, k)) if j_outer else fn)

    a_spec = spec((tk, tm), lambda i, j, k: (k, i)) if kind == "tn" else spec((tm, tk), lambda i, j, k: (i, k))
    b_spec = spec((tn, tk), lambda i, j, k: (j, k)) if kind == "nt" else spec((tk, tn), lambda i, j, k: (k, j))
    dn = {"nn": NN, "nt": NT, "tn": TN}[kind]

    tile_spec = spec((tm, tn), lambda i, j, k: (i, j))
    if out_blocks:
        width = n // N_DEV
        r_out = width // tn
        out_shape = [jax.ShapeDtypeStruct((N_DEV, m, width), dt) for dt in outs]
        out_specs = [spec((None, tm, tn), lambda i, j, k: (j // r_out, i, j % r_out)) for _ in outs]
    else:
        out_shape = [jax.ShapeDtypeStruct((m, n), dt) for dt in outs]
        out_specs = [tile_spec for _ in outs]
    n_ex, n_out = len(extras), len(outs)
    host = _Host(rider)
    n_acc = 1 if nk > 1 else 0

    def body(*refs):
        a_ref, b_ref = refs[0], refs[1]
        pos = 2
        ex_refs = refs[pos:pos + n_ex]; pos += n_ex
        r_ins = refs[pos:pos + host.n_in]; pos += host.n_in
        out_refs = refs[pos:pos + n_out]; pos += n_out
        r_outs = refs[pos:pos + host.n_out]; pos += host.n_out
        acc_ref = refs[pos] if n_acc else None
        sems = refs[pos + n_acc:]
        i, j, k = pl.program_id(1 if j_outer else 0), pl.program_id(0 if j_outer else 1), pl.program_id(2)

        def finish_tile(acc):
            vals = (acc,) if epilogue is None else epilogue(acc, *[e[...] for e in ex_refs])
            for o_ref, v in zip(out_refs, vals):
                o_ref[...] = v.astype(o_ref.dtype)

        def compute():
            part = _dot(a_ref[...], b_ref[...], dn)
            if nk == 1:
                finish_tile(part)
                return

            @pl.when(k == 0)
            def _():
                acc_ref[...] = part

            @pl.when(jnp.logical_and(k > 0, k < nk - 1))
            def _():
                acc_ref[...] += part

            @pl.when(k == nk - 1)
            def _():
                finish_tile(acc_ref[...] + part)

        first = jnp.logical_and(jnp.logical_and(i == 0, j == 0), k == 0)
        last = jnp.logical_and(jnp.logical_and(i == ni - 1, j == nj - 1), k == nk - 1)
        host.run(first, last, r_ins, r_outs, sems, compute)

    sem = ("arbitrary",) * 3 if rider else ("parallel", "parallel", "arbitrary")
    res = pl.pallas_call(
        body,
        name=name,
        grid=(nj, ni, nk) if j_outer else (ni, nj, nk),
        in_specs=[a_spec, b_spec] + [tile_spec for _ in extras] + host.in_specs,
        out_specs=out_specs + host.out_specs,
        out_shape=out_shape + host.out_shapes,
        scratch_shapes=([pltpu.VMEM((tm, tn), F32)] if n_acc else []) + host.scratch,
        compiler_params=_params(sem),
    )(a, b, *extras, *host.ins)
    return res[0] if len(res) == 1 else res


def _rms_fwd(name, x, g, tm=512):
    t, d = x.shape
    tm = _tile(t, tm)

    def body(x_ref, g_ref, h_ref):
        xf = x_ref[...]
        r = lax.rsqrt(jnp.mean(xf * xf, axis=-1, keepdims=True) + EPS)
        h_ref[...] = (xf * r * g_ref[...]).astype(h_ref.dtype)

    return pl.pallas_call(
        body,
        name=name,
        grid=(t // tm,),
        in_specs=[pl.BlockSpec((tm, d), lambda i: (i, 0)), pl.BlockSpec((1, d), lambda i: (0, 0))],
        out_specs=pl.BlockSpec((tm, d), lambda i: (i, 0)),
        out_shape=jax.ShapeDtypeStruct((t, d), BF16),
        compiler_params=_params(("parallel",)),
    )(x, g.reshape(1, d))


def _rms_bwd(name, dh, x, g, res=None, tm=256):
    t, d = x.shape
    tm = _tile(t, tm)
    has_res = res is not None

    def body(*refs):
        if has_res:
            dh_ref, x_ref, g_ref, res_ref, dx_ref, dxb_ref, gg_ref, ss_ref = refs
        else:
            dh_ref, x_ref, g_ref, dx_ref, dxb_ref, gg_ref, ss_ref = refs
        i = pl.program_id(0)
        xf = x_ref[...]
        r = lax.rsqrt(jnp.mean(xf * xf, axis=-1, keepdims=True) + EPS)
        xh = xf * r
        dhf = dh_ref[...].astype(F32)
        dxh = dhf * g_ref[...]
        dx = r * (dxh - xh * jnp.mean(dxh * xh, axis=-1, keepdims=True))

        @pl.when(i == 0)
        def _():
            gg_ref[...] = jnp.zeros_like(gg_ref)
            ss_ref[...] = jnp.zeros_like(ss_ref)

        if has_res:
            resf = res_ref[...]
            dx = dx + resf
            ss_ref[...] += jnp.sum(jnp.sum(resf * resf, axis=0, keepdims=True), axis=1, keepdims=True)
        dx_ref[...] = dx
        dxb_ref[...] = dx.astype(BF16)
        gg_ref[...] += jnp.sum(dhf * xh, axis=0, keepdims=True)

    row = pl.BlockSpec((tm, d), lambda i: (i, 0))
    vec = pl.BlockSpec((1, d), lambda i: (0, 0))
    one = pl.BlockSpec((1, 1), lambda i: (0, 0))
    ins = [dh, x, g.reshape(1, d)] + ([res] if has_res else [])
    dx, dxb, gg, ss = pl.pallas_call(
        body,
        name=name,
        grid=(t // tm,),
        in_specs=[row, row, vec] + ([row] if has_res else []),
        out_specs=[row, row, vec, one],
        out_shape=[jax.ShapeDtypeStruct((t, d), F32), jax.ShapeDtypeStruct((t, d), BF16), jax.ShapeDtypeStruct((1, d), F32),
                   jax.ShapeDtypeStruct((1, 1), F32)],
        compiler_params=_params(("arbitrary",)),
    )(*ins)
    return dx, dxb, gg.reshape(d), ss[0, 0]


def _head_rms(xf):
    r = lax.rsqrt(jnp.mean(xf * xf, axis=-1, keepdims=True) + EPS)
    return xf * r, r


def _head_rms_bwd(dy, xn, r, g):
    dxh = dy * g
    dx = r * (dxh - xn * jnp.mean(dxh * xn, axis=-1, keepdims=True))
    return dx, jnp.sum(dy * xn, axis=0, keepdims=True)


def _col_to_row(col):
    n = col.shape[0]
    eye = lax.broadcasted_iota(jnp.int32, (n, n), 0) == lax.broadcasted_iota(jnp.int32, (n, n), 1)
    return jnp.sum(jnp.where(eye, col, 0.0), axis=0, keepdims=True)


def _row_to_col(row):
    n = row.shape[1]
    eye = lax.broadcasted_iota(jnp.int32, (n, n), 0) == lax.broadcasted_iota(jnp.int32, (n, n), 1)
    return jnp.sum(jnp.where(eye, row, 0.0), axis=1, keepdims=True)


def _dproj_args(dproj, n_in):
    if dproj is None:
        return [], [], {}
    return [dproj], [ANY], {n_in: 0}


def _shift_down(u, s, rows):
    return jnp.where(rows >= s, pltpu.roll(u, s, axis=0), 0.0)


def _shift_up(u, s, rows, t):
    return jnp.where(rows < t - s, pltpu.roll(u, t - s, axis=0), 0.0)


def _conv_fwd(proj, off, conv_w, cb):
    t = proj.shape[0]
    c = conv_w.shape[1]
    blk0 = off // (3 * cb)

    def body(p_ref, w_ref, y_ref):
        rows = lax.broadcasted_iota(jnp.int32, (t, cb), 0)
        bg = p_ref[:, 0:cb].astype(F32)
        u = p_ref[:, cb:2 * cb].astype(F32) * p_ref[:, 2 * cb:3 * cb].astype(F32)
        w = w_ref[...]
        conv = w[2:3] * u + w[1:2] * _shift_down(u, 1, rows) + w[0:1] * _shift_down(u, 2, rows)
        y_ref[...] = (bg * conv).astype(y_ref.dtype)

    return pl.pallas_call(
        body,
        name="conv_fwd",
        grid=(c // cb,),
        in_specs=[pl.BlockSpec((t, 3 * cb), lambda j: (0, blk0 + j)), pl.BlockSpec((CONV_TAPS, cb), lambda j: (0, j))],
        out_specs=pl.BlockSpec((t, cb), lambda j: (0, j)),
        out_shape=jax.ShapeDtypeStruct((t, c), BF16),
        compiler_params=_params(("parallel",)),
    )(proj, conv_w)


def _conv_bwd(proj, off, conv_w, dy, cb, dproj, rider=None):
    t = proj.shape[0]
    c = conv_w.shape[1]
    blk0 = off // (3 * cb)
    nj = c // cb
    host = _Host(rider)

    def body(*refs):
        p_ref, w_ref, dy_ref = refs[:3]
        r_ins = refs[4:4 + host.n_in]
        dp_ref, gw_ref = refs[4 + host.n_in:6 + host.n_in]
        r_outs = refs[6 + host.n_in:6 + host.n_in + host.n_out]
        sems = refs[6 + host.n_in + host.n_out:]
        j = pl.program_id(0)

        def compute():
            rows = lax.broadcasted_iota(jnp.int32, (t, cb), 0)
            bg = p_ref[:, 0:cb].astype(F32)
            cg = p_ref[:, cb:2 * cb].astype(F32)
            v = p_ref[:, 2 * cb:3 * cb].astype(F32)
            u = cg * v
            w = w_ref[...]
            u1 = _shift_down(u, 1, rows)
            u2 = _shift_down(u, 2, rows)
            conv = w[2:3] * u + w[1:2] * u1 + w[0:1] * u2
            dyf = dy_ref[...].astype(F32)
            dconv = dyf * bg
            du = w[2:3] * dconv + w[1:2] * _shift_up(dconv, 1, rows, t) + w[0:1] * _shift_up(dconv, 2, rows, t)
            dp_ref[:, 0:cb] = (dyf * conv).astype(dp_ref.dtype)
            dp_ref[:, cb:2 * cb] = (du * v).astype(dp_ref.dtype)
            dp_ref[:, 2 * cb:3 * cb] = (du * cg).astype(dp_ref.dtype)
            gw_ref[0:1, :] = jnp.sum(dconv * u2, axis=0, keepdims=True)
            gw_ref[1:2, :] = jnp.sum(dconv * u1, axis=0, keepdims=True)
            gw_ref[2:3, :] = jnp.sum(dconv * u, axis=0, keepdims=True)

        host.run(j == 0, j == nj - 1, r_ins, r_outs, sems, compute)

    res = pl.pallas_call(
        body,
        name="conv_bwd",
        grid=(nj,),
        in_specs=[
            pl.BlockSpec((t, 3 * cb), lambda j: (0, blk0 + j)),
            pl.BlockSpec((CONV_TAPS, cb), lambda j: (0, j)),
            pl.BlockSpec((t, cb), lambda j: (0, j)),
            ANY,
        ] + host.in_specs,
        out_specs=[pl.BlockSpec((t, 3 * cb), lambda j: (0, blk0 + j)), pl.BlockSpec((CONV_TAPS, cb), lambda j: (0, j))] + host.out_specs,
        out_shape=[jax.ShapeDtypeStruct(dproj.shape, dproj.dtype), jax.ShapeDtypeStruct((CONV_TAPS, c), F32)] + host.out_shapes,
        input_output_aliases={3: 0},
        scratch_shapes=host.scratch,
        compiler_params=_params(("arbitrary",)),
    )(proj, conv_w, dy, dproj, *host.ins)
    return res


def _lane_scan(x, reverse):
    lane = lax.broadcasted_iota(jnp.int32, x.shape, 1)
    s = 1
    while s < LANES:
        if reverse:
            x = x + jnp.where(lane < LANES - s, pltpu.roll(x, LANES - s, axis=1), 0.0)
        else:
            x = x + jnp.where(lane >= s, pltpu.roll(x, s, axis=1), 0.0)
        s *= 2
    return x


def _scan_rows(src_ref, dst_ref, t, reverse, fn=None):
    groups = list(range(t // LANES))
    if reverse:
        groups = groups[::-1]
    carry = None
    for gi in groups:
        sl = slice(gi * LANES, (gi + 1) * LANES)
        blk = src_ref[:, sl]
        if fn is not None:
            blk = fn(blk)
        blk = _lane_scan(blk, reverse)
        if carry is not None:
            blk = blk + carry
        dst_ref[:, sl] = blk
        carry = blk[:, 0:1] if reverse else blk[:, LANES - 1:LANES]


def _forget_fwd(z_row, b_col):
    rows, t = z_row.shape

    def body(z_ref, b_ref, c_ref):
        def logf(z):
            zz = z + b_ref[...]
            return jnp.minimum(zz, 0.0) - jnp.log(1.0 + jnp.exp(-jnp.abs(zz)))

        _scan_rows(z_ref, c_ref, t, False, logf)

    return pl.pallas_call(
        body,
        name="forget_fwd",
        out_shape=jax.ShapeDtypeStruct((rows, t), F32),
        compiler_params=pltpu.CompilerParams(vmem_limit_bytes=VMEM_LIMIT),
    )(z_row, b_col)


def _rows_to_colb(c_row3, tq):
    heads, _, t = c_row3.shape

    def body(r_ref, o_ref):
        o_ref[...] = jnp.broadcast_to(_row_to_col(r_ref[...]), (tq, LANES))

    return pl.pallas_call(
        body,
        name="rows_to_colb",
        grid=(heads, t // tq),
        in_specs=[pl.BlockSpec((None, 1, tq), lambda h, i: (h, 0, i))],
        out_specs=pl.BlockSpec((None, tq, LANES), lambda h, i: (h, i, 0)),
        out_shape=jax.ShapeDtypeStruct((heads, t, LANES), F32),
        compiler_params=_params(("parallel", "parallel")),
    )(c_row3)


def _forget_bwd(z_row, b_col, dc_row):
    rows, t = z_row.shape

    def body(z_ref, b_ref, dc_ref, dz_ref, db_ref, tmp_ref):
        _scan_rows(dc_ref, tmp_ref, t, True)
        zz = z_ref[...] + b_ref[...]
        dz = tmp_ref[...] * (1.0 / (1.0 + jnp.exp(zz)))
        dz_ref[...] = dz.astype(dz_ref.dtype)
        db_ref[...] = jnp.sum(dz, axis=1, keepdims=True)

    return pl.pallas_call(
        body,
        name="forget_bwd",
        out_shape=[jax.ShapeDtypeStruct((rows, t), BF16), jax.ShapeDtypeStruct((rows, 1), F32)],
        scratch_shapes=[pltpu.VMEM((rows, t), F32)],
        compiler_params=pltpu.CompilerParams(vmem_limit_bytes=VMEM_LIMIT),
    )(z_row, b_col, dc_row)


def _fox_fwd(proj, off, gq, gk, c_row3, c_colb, heads, tq, rider=None):
    t = proj.shape[0]
    hd = FOX_HEAD_DIM
    tq = _tile(t, tq)
    nq = t // tq
    blk0 = off // hd
    scale = 1.0 / math.sqrt(hd)
    host = _Host(rider)

    def body(*refs):
        q_ref, k_ref, v_ref, gq_ref, gk_ref, crow_ref, ccol_ref = refs[:7]
        r_ins = refs[7:7 + host.n_in]
        o_ref, lse_ref = refs[7 + host.n_in:9 + host.n_in]
        r_outs = refs[9 + host.n_in:9 + host.n_in + host.n_out]
        khat_ref, v_t_ref = refs[9 + host.n_in + host.n_out:11 + host.n_in + host.n_out]
        sems = refs[11 + host.n_in + host.n_out:]
        h, qi = pl.program_id(0), pl.program_id(1)

        def compute():
            eye = (lax.broadcasted_iota(jnp.int32, (hd, hd), 0) == lax.broadcasted_iota(jnp.int32, (hd, hd), 1)).astype(BF16)

            @pl.when(qi == 0)
            def _():
                kn, _ = _head_rms(k_ref[...].astype(F32))
                khat_ref[...] = (kn * gk_ref[...]).astype(BF16)
                v_t_ref[...] = _dot(eye, v_ref[...], NT).astype(BF16)

            qn, _ = _head_rms(q_ref[...].astype(F32))
            qhat = (qn * (gq_ref[...] * scale)).astype(BF16)
            crow = crow_ref[:, pl.ds(pl.multiple_of(qi * tq, tq), tq)]
            above = lax.broadcasted_iota(jnp.int32, (tq, tq), 1) >= lax.broadcasted_iota(jnp.int32, (tq, tq), 0)

            def tile(j, carry, diagonal):
                m, l, acc_t = carry
                ks = pl.multiple_of(j * tq, tq)
                s_t = _dot(khat_ref[pl.ds(ks, tq), :], qhat, NT) - ccol_ref[pl.ds(ks, tq), 0:1]
                if diagonal:
                    s_t = jnp.where(above, s_t, NEG)
                m_new = jnp.maximum(m, jnp.max(s_t, axis=0, keepdims=True) + crow)
                alpha = jnp.exp(m - m_new)
                p_t = jnp.exp(s_t + (crow - m_new))
                l = alpha * l + jnp.sum(p_t, axis=0, keepdims=True)
                acc_t = alpha * acc_t + _dot(v_t_ref[:, pl.ds(ks, tq)], p_t.astype(BF16), NN)
                return m_new, l, acc_t

            init = (jnp.full((1, tq), NEG, F32), jnp.zeros((1, tq), F32), jnp.zeros((hd, tq), F32))
            carry = lax.fori_loop(0, qi, lambda j, c: tile(j, c, False), init)
            m, l, acc_t = tile(qi, carry, True)
            o_ref[...] = _dot((acc_t / l).astype(BF16), eye, TN).astype(o_ref.dtype)
            lse_ref[...] = m + jnp.log(l)

        first = jnp.logical_and(h == 0, qi == 0)
        last = jnp.logical_and(h == heads - 1, qi == nq - 1)
        host.run(first, last, r_ins, r_outs, sems, compute)

    res = pl.pallas_call(
        body,
        name="fox_fwd",
        grid=(heads, nq),
        in_specs=[
            pl.BlockSpec((tq, hd), lambda h, i: (i, blk0 + 3 * h)),
            pl.BlockSpec((t, hd), lambda h, i: (0, blk0 + 3 * h + 1)),
            pl.BlockSpec((t, hd), lambda h, i: (0, blk0 + 3 * h + 2)),
            pl.BlockSpec((1, hd), lambda h, i: (0, 0)),
            pl.BlockSpec((1, hd), lambda h, i: (0, 0)),
            pl.BlockSpec((None, 1, t), lambda h, i: (h, 0, 0)),
            pl.BlockSpec((None, t, LANES), lambda h, i: (h, 0, 0)),
        ] + host.in_specs,
        out_specs=[pl.BlockSpec((tq, hd), lambda h, i: (i, h)), pl.BlockSpec((None, 1, tq), lambda h, i: (h, 0, i))] + host.out_specs,
        out_shape=[jax.ShapeDtypeStruct((t, heads * hd), BF16), jax.ShapeDtypeStruct((heads, 1, t), F32)] + host.out_shapes,
        scratch_shapes=[pltpu.VMEM((t, hd), BF16), pltpu.VMEM((hd, t), BF16)] + host.scratch,
        compiler_params=_params(("arbitrary", "arbitrary")),
    )(proj, proj, proj, gq.reshape(1, hd), gk.reshape(1, hd), c_row3, c_colb, *host.ins)
    return res


def _fox_bwd(proj, off, o, do, gq, gk, c_row3, c_colb, lse, heads, tq, dproj, rider=None):
    t = proj.shape[0]
    hd = FOX_HEAD_DIM
    tq = _tile(t, tq)
    nb = t // tq
    blk0 = off // hd
    scale = 1.0 / math.sqrt(hd)
    host = _Host(rider)
    n_fixed_in = 11

    def body(*refs):
        q_ref, k_ref, v_ref, o_ref, do_ref, gq_ref, gk_ref, crow_ref, ccol_ref, lse_ref = refs[:10]
        pos = n_fixed_in
        r_ins = refs[pos:pos + host.n_in]; pos += host.n_in
        dp_ref, dc_ref, ggq_ref, ggk_ref = refs[pos:pos + 4]; pos += 4
        r_outs = refs[pos:pos + host.n_out]; pos += host.n_out
        qhat_ref, khat_ref, khat_t_ref, dq_t_ref, dk_ref, dcq_ref, dck_ref, delta_ref = refs[pos:pos + 8]; pos += 8
        sems = refs[pos:]
        h = pl.program_id(0)

        def compute():
            qn, rq = _head_rms(q_ref[...].astype(F32))
            qhat_ref[...] = (qn * (gq_ref[...] * scale)).astype(BF16)
            kn, rk = _head_rms(k_ref[...].astype(F32))
            khat_ref[...] = (kn * gk_ref[...]).astype(BF16)
            eye = (lax.broadcasted_iota(jnp.int32, (hd, hd), 0) == lax.broadcasted_iota(jnp.int32, (hd, hd), 1)).astype(BF16)
            khat_t_ref[...] = _dot(eye, khat_ref[...], NT).astype(BF16)
            delta = jnp.sum(do_ref[...].astype(F32) * o_ref[...].astype(F32), axis=-1, keepdims=True)
            for b in range(nb):
                sl = slice(b * tq, (b + 1) * tq)
                delta_ref[:, sl] = _col_to_row(delta[sl, :])
            dq_t_ref[...] = jnp.zeros_like(dq_t_ref)
            dcq_ref[...] = jnp.zeros_like(dcq_ref)
            above = lax.broadcasted_iota(jnp.int32, (tq, tq), 1) >= lax.broadcasted_iota(jnp.int32, (tq, tq), 0)

            def kv_block(j, _):
                ks = pl.multiple_of(j * tq, tq)
                kh = khat_ref[pl.ds(ks, tq), :]
                kh_t = khat_t_ref[:, pl.ds(ks, tq)]
                vv = v_ref[pl.ds(ks, tq), :]
                ccol = ccol_ref[pl.ds(ks, tq), 0:1]

                def q_block(i, carry, diagonal):
                    dk, dv, dck = carry
                    qs = pl.multiple_of(i * tq, tq)
                    qh = qhat_ref[pl.ds(qs, tq), :]
                    dob = do_ref[pl.ds(qs, tq), :]
                    s_t = _dot(kh, qh, NT) + ((crow_ref[:, pl.ds(qs, tq)] - lse_ref[:, pl.ds(qs, tq)]) - ccol)
                    p_t = jnp.exp(s_t)
                    if diagonal:
                        p_t = jnp.where(above, p_t, 0.0)
                    ds_t = p_t * (_dot(vv, dob, NT) - delta_ref[:, pl.ds(qs, tq)])
                    dsb = ds_t.astype(BF16)
                    dv = dv + _dot(p_t.astype(BF16), dob, NN)
                    dk = dk + _dot(dsb, qh, NN)
                    dq_t_ref[:, pl.ds(qs, tq)] += _dot(kh_t, dsb, NN)
                    dcq_ref[:, pl.ds(qs, tq)] += jnp.sum(ds_t, axis=0, keepdims=True)
                    dck = dck + jnp.sum(ds_t, axis=-1, keepdims=True)
                    return dk, dv, dck

                zero = jnp.zeros((tq, hd), F32)
                carry = q_block(j, (zero, zero, jnp.zeros((tq, 1), F32)), True)
                dk, dv, dck = lax.fori_loop(j + 1, nb, lambda i, c: q_block(i, c, False), carry)
                dk_ref[pl.ds(ks, tq), :] = dk
                dp_ref[pl.ds(ks, tq), 2 * hd:3 * hd] = dv.astype(dp_ref.dtype)
                dck_ref[pl.ds(ks, tq), :] = dck
                return 0

            lax.fori_loop(0, nb, kv_block, 0)

            dq, ggq = _head_rms_bwd(dq_t_ref[...].T * scale, qn, rq, gq_ref[...])
            dk, ggk = _head_rms_bwd(dk_ref[...], kn, rk, gk_ref[...])
            dp_ref[:, 0:hd] = dq.astype(dp_ref.dtype)
            dp_ref[:, hd:2 * hd] = dk.astype(dp_ref.dtype)
            for b in range(nb):
                sl = slice(b * tq, (b + 1) * tq)
                dc_ref[:, sl] = dcq_ref[:, sl] - _col_to_row(dck_ref[sl, :])

            @pl.when(h == 0)
            def _():
                ggq_ref[...] = jnp.zeros_like(ggq_ref)
                ggk_ref[...] = jnp.zeros_like(ggk_ref)

            ggq_ref[...] += ggq
            ggk_ref[...] += ggk

        host.run(h == 0, h == heads - 1, r_ins, r_outs, sems, compute)

    head_in = lambda part: pl.BlockSpec((t, hd), lambda h: (0, blk0 + 3 * h + part))
    vec = pl.BlockSpec((1, hd), lambda h: (0, 0))
    colb = pl.BlockSpec((None, t, LANES), lambda h: (h, 0, 0))
    res = pl.pallas_call(
        body,
        name="fox_bwd",
        grid=(heads,),
        in_specs=[
            head_in(0), head_in(1), head_in(2),
            pl.BlockSpec((t, hd), lambda h: (0, h)),
            pl.BlockSpec((t, hd), lambda h: (0, h)),
            vec, vec,
            pl.BlockSpec((None, 1, t), lambda h: (h, 0, 0)),
            colb,
            pl.BlockSpec((None, 1, t), lambda h: (h, 0, 0)),
            ANY,
        ] + host.in_specs,
        out_specs=[
            pl.BlockSpec((t, 3 * hd), lambda h: (0, blk0 // 3 + h)),
            pl.BlockSpec((None, 1, t), lambda h: (h, 0, 0)),
            vec, vec,
        ] + host.out_specs,
        out_shape=[
            jax.ShapeDtypeStruct(dproj.shape, dproj.dtype),
            jax.ShapeDtypeStruct((heads, 1, t), F32),
            jax.ShapeDtypeStruct((1, hd), F32),
            jax.ShapeDtypeStruct((1, hd), F32),
        ] + host.out_shapes,
        input_output_aliases={10: 0},
        scratch_shapes=[
            pltpu.VMEM((t, hd), BF16), pltpu.VMEM((t, hd), BF16), pltpu.VMEM((hd, t), BF16),
            pltpu.VMEM((hd, t), F32), pltpu.VMEM((t, hd), F32),
            pltpu.VMEM((1, t), F32), pltpu.VMEM((t, 1), F32), pltpu.VMEM((1, t), F32),
        ] + host.scratch,
        compiler_params=_params(("arbitrary",)),
    )(proj, proj, proj, o, do, gq.reshape(1, hd), gk.reshape(1, hd), c_row3, c_colb, lse, dproj, *host.ins)
    return res


def _mem_fwd(proj, off, kv, gq, gk, tq):
    t = proj.shape[0]
    m, width = kv.shape[0], kv.shape[1] // 2
    hd = width // MEM_HEADS
    tq = _tile(t, tq)
    blk0 = off // hd
    scale = 1.0 / math.sqrt(hd)

    def body(q_ref, k_ref, v_ref, gq_ref, gk_ref, o_ref):
        qn, _ = _head_rms(q_ref[...].astype(F32))
        kn, _ = _head_rms(k_ref[...])
        s = _dot((qn * gq_ref[...]).astype(BF16), (kn * gk_ref[...]).astype(BF16), NT) * scale
        p = jnp.exp(s - jnp.max(s, axis=-1, keepdims=True))
        p = p / jnp.sum(p, axis=-1, keepdims=True)
        o_ref[...] = _dot(p.astype(BF16), v_ref[...].astype(BF16), NN).astype(o_ref.dtype)

    vec = pl.BlockSpec((1, hd), lambda h, i: (0, 0))
    return pl.pallas_call(
        body,
        name="mem_fwd",
        grid=(MEM_HEADS, t // tq),
        in_specs=[
            pl.BlockSpec((tq, hd), lambda h, i: (i, blk0 + h)),
            pl.BlockSpec((m, hd), lambda h, i: (0, h)),
            pl.BlockSpec((m, hd), lambda h, i: (0, MEM_HEADS + h)),
            vec, vec,
        ],
        out_specs=pl.BlockSpec((tq, hd), lambda h, i: (i, h)),
        out_shape=jax.ShapeDtypeStruct((t, width), BF16),
        compiler_params=_params(("parallel", "parallel")),
    )(proj, kv, kv, gq.reshape(1, hd), gk.reshape(1, hd))


def _mem_bwd(proj, off, kv, do, gq, gk, tq, dproj, rider=None):
    t = proj.shape[0]
    m, width = kv.shape[0], kv.shape[1] // 2
    hd = width // MEM_HEADS
    tq = _tile(t, tq)
    nq = t // tq
    blk0 = off // hd
    scale = 1.0 / math.sqrt(hd)
    host = _Host(rider)

    def body(*refs):
        q_ref, k_ref, v_ref, do_ref, gq_ref, gk_ref = refs[:6]
        pos = 7
        r_ins = refs[pos:pos + host.n_in]; pos += host.n_in
        dq_ref, dk_ref, dv_ref, ggq_ref, ggk_ref = refs[pos:pos + 5]; pos += 5
        r_outs = refs[pos:pos + host.n_out]; pos += host.n_out
        dkh_ref, dvh_ref = refs[pos:pos + 2]; pos += 2
        sems = refs[pos:]
        h, i = pl.program_id(0), pl.program_id(1)

        def compute():
            qn, rq = _head_rms(q_ref[...].astype(F32))
            kn, rk = _head_rms(k_ref[...])
            qhat = (qn * gq_ref[...]).astype(BF16)
            khat = (kn * gk_ref[...]).astype(BF16)
            vb = v_ref[...].astype(BF16)
            dob = do_ref[...]
            s = _dot(qhat, khat, NT) * scale
            p = jnp.exp(s - jnp.max(s, axis=-1, keepdims=True))
            p = p / jnp.sum(p, axis=-1, keepdims=True)
            dp = _dot(dob, vb, NT)
            ds = p * (dp - jnp.sum(dp * p, axis=-1, keepdims=True))
            dsb = ds.astype(BF16)
            dq, ggq = _head_rms_bwd(_dot(dsb, khat, NN) * scale, qn, rq, gq_ref[...])
            dq_ref[...] = dq.astype(dq_ref.dtype)

            @pl.when(i == 0)
            def _():
                dkh_ref[...] = jnp.zeros_like(dkh_ref)
                dvh_ref[...] = jnp.zeros_like(dvh_ref)

            @pl.when(jnp.logical_and(h == 0, i == 0))
            def _():
                ggq_ref[...] = jnp.zeros_like(ggq_ref)
                ggk_ref[...] = jnp.zeros_like(ggk_ref)

            dkh_ref[...] += _dot(dsb, qhat, TN)
            dvh_ref[...] += _dot(p.astype(BF16), dob, TN)
            ggq_ref[...] += ggq

            @pl.when(i == nq - 1)
            def _():
                dk, ggk = _head_rms_bwd(dkh_ref[...] * scale, kn, rk, gk_ref[...])
                dk_ref[...] = dk.astype(dk_ref.dtype)
                dv_ref[...] = dvh_ref[...].astype(dv_ref.dtype)
                ggk_ref[...] += ggk

        first = jnp.logical_and(h == 0, i == 0)
        last = jnp.logical_and(h == MEM_HEADS - 1, i == nq - 1)
        host.run(first, last, r_ins, r_outs, sems, compute)

    vec = pl.BlockSpec((1, hd), lambda h, i: (0, 0))
    kblk = pl.BlockSpec((m, hd), lambda h, i: (0, h))
    res = pl.pallas_call(
        body,
        name="mem_bwd",
        grid=(MEM_HEADS, nq),
        in_specs=[
            pl.BlockSpec((tq, hd), lambda h, i: (i, blk0 + h)), kblk,
            pl.BlockSpec((m, hd), lambda h, i: (0, MEM_HEADS + h)),
            pl.BlockSpec((tq, hd), lambda h, i: (i, h)), vec, vec, ANY,
        ] + host.in_specs,
        out_specs=[pl.BlockSpec((tq, hd), lambda h, i: (i, blk0 + h)), kblk, kblk, vec, vec] + host.out_specs,
        out_shape=[
            jax.ShapeDtypeStruct(dproj.shape, dproj.dtype),
            jax.ShapeDtypeStruct((m, width), BF16),
            jax.ShapeDtypeStruct((m, width), BF16),
            jax.ShapeDtypeStruct((1, hd), F32),
            jax.ShapeDtypeStruct((1, hd), F32),
        ] + host.out_shapes,
        input_output_aliases={6: 0},
        scratch_shapes=[pltpu.VMEM((m, hd), F32), pltpu.VMEM((m, hd), F32)] + host.scratch,
        compiler_params=_params(("arbitrary", "arbitrary")),
    )(proj, kv, kv, do, gq.reshape(1, hd), gk.reshape(1, hd), dproj, *host.ins)
    dproj, dk, dv, ggq, ggk = res[:5]
    return (dproj, jnp.concatenate([dk, dv], axis=1), ggq.reshape(hd), ggk.reshape(hd), *res[5:])


def _sigmoid(z):
    return 1.0 / (1.0 + jnp.exp(-z))


def _merge_fwd(proj, o3, tm, tc):
    t, d = o3[0].shape
    tm = _tile(t, tm)

    def body(g_ref, oa_ref, ob_ref, oc_ref, out_ref):
        acc = jnp.zeros((tm, tc), F32)
        for s, o_ref in enumerate((oa_ref, ob_ref, oc_ref)):
            acc = acc + _sigmoid(g_ref[:, s * tc:(s + 1) * tc].astype(F32)) * o_ref[...].astype(F32)
        out_ref[...] = acc.astype(out_ref.dtype)

    blk = pl.BlockSpec((tm, tc), lambda i, j: (i, j))
    return pl.pallas_call(
        body,
        name="merge_fwd",
        grid=(t // tm, d // tc),
        in_specs=[pl.BlockSpec((tm, 3 * tc), lambda i, j: (i, j)), blk, blk, blk],
        out_specs=blk,
        out_shape=jax.ShapeDtypeStruct((t, d), BF16),
        compiler_params=_params(("parallel", "parallel")),
    )(proj, *o3)


def _merge_bwd(proj, o3, dm, tm, tc):
    t, d = dm.shape
    tm = _tile(t, tm)

    def body(g_ref, oa_ref, ob_ref, oc_ref, dm_ref, dg_ref, da_ref, db_ref, dc_ref):
        dmf = dm_ref[...].astype(F32)
        for s, (o_ref, do_ref) in enumerate(((oa_ref, da_ref), (ob_ref, db_ref), (oc_ref, dc_ref))):
            g = _sigmoid(g_ref[:, s * tc:(s + 1) * tc].astype(F32))
            do_ref[...] = (dmf * g).astype(do_ref.dtype)
            dg_ref[:, s * tc:(s + 1) * tc] = (dmf * o_ref[...].astype(F32) * g * (1.0 - g)).astype(dg_ref.dtype)

    blk = pl.BlockSpec((tm, tc), lambda i, j: (i, j))
    wide = pl.BlockSpec((tm, 3 * tc), lambda i, j: (i, j))
    return pl.pallas_call(
        body,
        name="merge_bwd",
        grid=(t // tm, d // tc),
        in_specs=[wide, blk, blk, blk, blk],
        out_specs=[wide, blk, blk, blk],
        out_shape=[jax.ShapeDtypeStruct(proj.shape, BF16)] + [jax.ShapeDtypeStruct((t, d), BF16)] * 3,
        compiler_params=_params(("parallel", "parallel")),
    )(proj, *o3, dm)


def _w_in_chunks(d, tc):
    cw = d // 2
    heads = cw // FOX_HEAD_DIM
    conv0, fox0, f0, mq0, gate0 = 0, 3 * cw, 6 * cw, 6 * cw + heads, 7 * cw + heads
    chunks = [(gate0 + s * d + j * tc, gate0 + s * d + (j + 1) * tc) for j in range(d // tc) for s in range(N_BRANCHES)]
    chunks += [(conv0 + s * cw + j * LANES, conv0 + s * cw + (j + 1) * LANES) for j in range(cw // LANES) for s in range(3)]
    chunks += [(fox0 + s * cw + j * FOX_HEAD_DIM, fox0 + s * cw + (j + 1) * FOX_HEAD_DIM) for j in range(heads) for s in range(3)]
    chunks.append((mq0, mq0 + cw))
    return chunks, (f0, f0 + heads)


ROW_TILE = 16
GROUP = 128
GROUP_BACK = 112
SCRATCH_ROWS = 2 * GROUP + 32


def _padded_rows(r):
    return -(-r // GROUP_BACK) * GROUP_BACK


def _rows_from(scr_ref, y_ref, q8, fine, g):
    x = scr_ref[pl.ds(pl.multiple_of(q8 * 8, 8), g + 8), :]
    for s in range(8):
        @pl.when(fine == s)
        def _(s=s):
            y_ref[...] = (x if s == 0 else pltpu.roll(x, g + 8 - s, axis=0))[0:g]


def _assemble(name, tbl, grid, step, in_specs, out_spec, out_shape, operands, g, w1, cols_of):
    has_f = len(in_specs) == 3
    k = out_shape.shape[-1]
    c = cols_of

    def body(*refs):
        t_ref, s1_ref, s2_ref = refs[:3]
        f_ref = refs[3] if has_f else None
        out_ref = refs[3 + has_f]
        scr1, scr2, scrf, y_ref = refs[4 + has_f:]
        t = step()

        @pl.when(t == 0)
        def _():
            scr1[...] = jnp.zeros_like(scr1)
            scr2[...] = jnp.zeros_like(scr2)
            scrf[...] = jnp.zeros_like(scrf)

        rows = lax.broadcasted_iota(jnp.int32, (g, k), 0)
        n1, a2 = t_ref[c["n1"], t], t_ref[c["a2"], t]
        scr1[0:w1, :] = (s1_ref[0] if len(s1_ref.shape) == 3 else s1_ref[...]).astype(F32)
        _rows_from(scr1, y_ref, t_ref[c["q1"], t], t_ref[c["s1"], t], g)
        out_ref[...] = y_ref[...].astype(out_ref.dtype)

        @pl.when(a2 < g)
        def _():
            scr2[g:g + s2_ref.shape[0], :] = s2_ref[...].astype(F32)
            _rows_from(scr2, y_ref, t_ref[c["q2"], t], t_ref[c["s2"], t], g)
            out_ref[...] = jnp.where(rows < n1, out_ref[...].astype(F32), y_ref[...]).astype(out_ref.dtype)

        if has_f:
            fa, fb = t_ref[c["fa"], t], t_ref[c["fb"], t]

            @pl.when(fb > fa)
            def _():
                scrf[g:g + f_ref.shape[0], :] = f_ref[...].astype(F32)
                _rows_from(scrf, y_ref, t_ref[c["qf"], t], t_ref[c["sf"], t], g)
                inside = jnp.logical_and(rows >= fa, rows < fb)
                out_ref[...] = jnp.where(inside, y_ref[...], out_ref[...].astype(F32)).astype(out_ref.dtype)

            valid = t_ref[c["valid"], t]

            @pl.when(valid < g)
            def _():
                out_ref[...] = jnp.where(rows < valid, out_ref[...].astype(F32), 0.0).astype(out_ref.dtype)

    return pl.pallas_call(
        body,
        name=name,
        grid_spec=pltpu.PrefetchScalarGridSpec(
            num_scalar_prefetch=1, grid=grid, in_specs=in_specs, out_specs=out_spec,
            scratch_shapes=[pltpu.VMEM((SCRATCH_ROWS, k), F32)] * 3 + [pltpu.VMEM((g, k), F32)]),
        out_shape=out_shape,
        compiler_params=_params(("arbitrary",) * len(grid)),
    )(jnp.asarray(tbl), *operands)


def _pack_w_in(w8, d, tc):
    blocks, rp, k = w8.shape
    chunks, (f_lo, f_hi) = _w_in_chunks(d, tc)
    r = max(hi for _, hi in chunks) // blocks
    g, w1 = GROUP, GROUP + ROW_TILE
    table = []
    for lo, hi in chunks:
        for g0 in range(lo, hi, g):
            b1, r1 = divmod(g0, r)
            n1 = min(g, r - r1)
            st1 = min(r1 // ROW_TILE * ROW_TILE, rp - w1)
            o1, o2 = r1 - st1, g - n1
            table.append((b1, st1, o1 // 8, o1 % 8, n1, n1, min(b1 + 1, blocks - 1), o2 // 8, o2 % 8))
    names = ("b1", "st1", "q1", "s1", "n1", "a2", "b2", "q2", "s2")
    cols_of = {n: i for i, n in enumerate(names)}
    tbl = np.array(table, np.int32).T
    c = cols_of
    w_all = _assemble(
        "pack_w_in", tbl, (len(table),), lambda: pl.program_id(0),
        [pl.BlockSpec((pl.Element(1), pl.Element(w1), pl.Element(k)), lambda i, t: (t[c["b1"], i], pl.multiple_of(t[c["st1"], i], ROW_TILE), 0)),
         pl.BlockSpec((None, g, k), lambda i, t: (t[c["b2"], i], 0, 0))],
        pl.BlockSpec((g, k), lambda i, t: (i, 0)),
        jax.ShapeDtypeStruct((len(table) * g, k), w8.dtype), [w8, w8], g, w1, cols_of)
    fb, fr = divmod(f_lo, r)
    return w_all, jnp.pad(w8[fb, fr:fr + f_hi - f_lo], ((0, F_ROWS - (f_hi - f_lo)), (0, 0)))


def _unpack_g_in(g_all, g_f, d, tc, blocks):
    n_all, k = g_all.shape
    chunks, (f_lo, f_hi) = _w_in_chunks(d, tc)
    r = max(hi for _, hi in chunks) // blocks
    rp = _padded_rows(r)
    g, w1 = GROUP_BACK, GROUP_BACK + ROW_TILE
    pos, spans = 0, [(f_lo, f_hi, None)]
    for lo, hi in chunks:
        spans.append((lo, hi, pos))
        pos += hi - lo
    spans.sort()
    table = []
    for b in range(blocks):
        for l0 in range(0, rp, g):
            valid = max(0, min(g, r - l0))
            g0, segs, fa, fb, of = b * r + l0, [], 0, 0, 0
            for lo, hi, p in spans:
                a, e = max(lo, g0), min(hi, g0 + valid)
                if a < e and p is None:
                    fa, fb, of = a - g0, e - g0, g + (a - lo) - (a - g0)
                elif a < e:
                    segs.append((a - g0, p + a - lo, e - a))
            assert len(segs) <= 2 and (not segs or segs[0][0] == 0 or len(segs) == 1)
            first = segs[0] if segs and segs[0][0] == 0 else (0, 0, 0)
            second = segs[-1] if segs and segs[-1][0] > 0 else (g, 0, 0)
            st1 = min(first[1] // ROW_TILE * ROW_TILE, n_all - w1)
            o1, o2 = first[1] - st1, g - second[0]
            assert second[1] % GROUP == 0
            table.append((st1, o1 // 8, o1 % 8, first[2], second[0], second[1] // GROUP, o2 // 8, o2 % 8,
                          fa, fb, of // 8, of % 8, valid))
    names = ("st1", "q1", "s1", "n1", "a2", "j2", "q2", "s2", "fa", "fb", "qf", "sf", "valid")
    cols_of = {n: i for i, n in enumerate(names)}
    tbl = np.array(table, np.int32).T
    c, per = cols_of, rp // g
    return _assemble(
        "unpack_g_in", tbl, (blocks, per), lambda: pl.program_id(0) * per + pl.program_id(1),
        [pl.BlockSpec((pl.Element(w1), pl.Element(k)), lambda b, u, t: (pl.multiple_of(t[c["st1"], b * per + u], ROW_TILE), 0)),
         pl.BlockSpec((GROUP, k), lambda b, u, t: (t[c["j2"], b * per + u], 0)),
         pl.BlockSpec((F_ROWS, k), lambda b, u, t: (0, 0))],
        pl.BlockSpec((None, g, k), lambda b, u, t: (b, u, 0)),
        jax.ShapeDtypeStruct((blocks, rp, k), g_all.dtype), [g_all, g_all, g_f], g, w1, cols_of)


def _unblock(w8):
    return w8.transpose(1, 0, 2).reshape(w8.shape[1], -1)


def _tile2(r, cols, tr, tcols):
    if r % 8 == 0:
        return _tile(r, tr), cols
    return r, _tile(cols, tcols)


def _pair_sum(name, g8, got, c):
    def body(c_ref, g_ref, s_ref, o_ref):
        o_ref[...] = (g_ref[...].astype(F32) + s_ref[...].astype(F32)).astype(o_ref.dtype)

    if g8.ndim == 4:
        _, r, k1, k2 = g8.shape
        tr = max(cand for cand in range(1, 385) if r % cand == 0)
        grid = (N_CHIPS, r // tr)
        shape = (None, tr, k1, k2)
        own = pl.BlockSpec(shape, lambda q, i, c_ref: (2 * q + c_ref[0], i, 0, 0))
        blk = pl.BlockSpec(shape, lambda q, i, c_ref: (q, i, 0, 0))
    else:
        _, r, cols = g8.shape
        tr, tcols = _tile2(r, cols, 256, 256)
        grid = (N_CHIPS, r // tr, cols // tcols)
        own = pl.BlockSpec((None, tr, tcols), lambda q, i, j, c_ref: (2 * q + c_ref[0], i, j))
        blk = pl.BlockSpec((None, tr, tcols), lambda q, i, j, c_ref: (q, i, j))
    return pl.pallas_call(
        body,
        name=name,
        grid_spec=pltpu.PrefetchScalarGridSpec(num_scalar_prefetch=1, grid=grid, in_specs=[own, blk], out_specs=blk),
        out_shape=jax.ShapeDtypeStruct((N_CHIPS,) + g8.shape[1:], BF16),
        compiler_params=_params(("parallel",) * len(grid)),
    )(c, g8, got)


def _local_step(x, mem, target, w, small, comm=None):
    t, d = x.shape
    cw = d // 2
    heads = cw // FOX_HEAD_DIM
    tc = min(512, d)
    tq = min(512, t)
    off_conv, off_fox, off_mq = 3 * d, 3 * d + 3 * cw, 3 * d + 6 * cw
    w = dict(w)
    w_all, w_f = _pack_w_in(w["w_in"], d, tc)
    big = dict(tm=1024, tn=512, tk=2048)
    wide_k = dict(tm=512, tn=1024, tk=4096)
    tall = dict(tm=2048, tn=512, tk=2048)

    h = _rms_fwd("rms1_fwd", x, small["norm1_g"])
    if comm:
        early = ("w_conv_out", "w_fox_out", "w_mem_out", "w_out", "w_mem_kv")
        proj, *got = _matmul("proj", "nt", h, w_all, outs=[BF16], rider=_gather_rider([comm["shards"][n] for n in early]), **tall)
        for n, val in zip(early, got):
            w[n] = _unblock(val) if n in COLUMN_SPLIT else val.reshape(-1, val.shape[-1])
    else:
        proj = _matmul("proj", "nt", h, w_all, outs=[BF16], **tall)
    z_row = _matmul("proj_f", "nt", w_f, h, outs=[F32], tm=F_ROWS, tn=512, tk=2048)

    y_conv = _conv_fwd(proj, off_conv, small["conv_w"], LANES)

    b_col = jnp.pad(small["b_f"], (0, F_ROWS - heads)).reshape(F_ROWS, 1)
    c_row3 = _forget_fwd(z_row, b_col)[:heads].reshape(heads, 1, t)
    c_colb = _rows_to_colb(c_row3, tq)
    if comm:
        y_fox, lse, got = _fox_fwd(proj, off_fox, small["fox_q_g"], small["fox_k_g"], c_row3, c_colb, heads, tq,
                                   rider=_gather_rider([comm["shards"]["w_up"]]))
        w["w_up"] = _unblock(got)
    else:
        y_fox, lse = _fox_fwd(proj, off_fox, small["fox_q_g"], small["fox_k_g"], c_row3, c_colb, heads, tq)

    nm = _rms_fwd("mem_rms_fwd", mem, small["mem_norm_g"])
    kv = _matmul("mem_kv", "nn", nm, w["w_mem_kv"], outs=[F32], tm=256, tn=512, tk=2048)
    y_mem = _mem_fwd(proj, off_mq, kv, small["mem_q_g"], small["mem_k_g"], tq)

    ys = (y_conv, y_fox, y_mem)
    w_outs = (w["w_conv_out"], w["w_fox_out"], w["w_mem_out"])
    o3 = [_matmul(f"branch_out{s}", "nn", ys[s], w_outs[s], outs=[BF16], **big) for s in range(3)]
    merged = _merge_fwd(proj, o3, 512, tc)
    x1 = _matmul("out_proj", "nn", merged, w["w_out"], outs=[F32], extras=[x],
                 epilogue=lambda acc, xr: (acc + xr,), **big)
    h2 = _rms_fwd("rms2_fwd", x1, small["norm2_g"])

    def up_epilogue(acc):
        return acc, jnp.square(jnp.maximum(acc, 0.0))

    if comm:
        up, act, got = _matmul("mlp_up", "nn", h2, w["w_up"], outs=[BF16, BF16], epilogue=up_epilogue,
                               rider=_gather_rider([comm["shards"]["w_down"]]), **big)
        w["w_down"] = got.reshape(-1, got.shape[-1])
    else:
        up, act = _matmul("mlp_up", "nn", h2, w["w_up"], outs=[BF16, BF16], epilogue=up_epilogue, **big)

    def loss_epilogue(acc, x1r, tr):
        dy = (acc + x1r - tr) * (1.0 / d)
        return dy, dy

    dy, dyb = _matmul("mlp_down", "nn", act, w["w_down"], outs=[F32, BF16], extras=[x1, target],
                      epilogue=loss_epilogue, **big)

    def dup_epilogue(acc, upr):
        return (acc * 2.0 * jnp.maximum(upr.astype(F32), 0.0),)

    def by_owner(g):
        return g.reshape(N_DEV, -1, g.shape[-1])

    g, parts = {}, {}
    g["w_down"] = _matmul("d_w_down", "tn", act, dyb, outs=[BF16], **wide_k)
    if comm:
        dup, got = _matmul("d_act", "nt", dyb, w["w_down"], outs=[BF16], extras=[up], epilogue=dup_epilogue,
                           rider=_pair_rider([by_owner(g["w_down"])]), **tall)
        pair = _pair_sum("pair_w_down", by_owner(g["w_down"]), got, comm["c"])
        g["w_up"], parts["w_down"] = _matmul("d_w_up", "tn", h2, dup, outs=[BF16], out_blocks=True,
                                             rider=_chip_rider([pair]), **wide_k)
        dh2, got = _matmul("d_h2", "nt", dup, w["w_up"], outs=[F32], rider=_pair_rider([g["w_up"]]), **tall)
        pair_up = _pair_sum("pair_w_up", g["w_up"], got, comm["c"])
    else:
        dup = _matmul("d_act", "nt", dyb, w["w_down"], outs=[BF16], extras=[up], epilogue=dup_epilogue, **tall)
        g["w_up"] = _matmul("d_w_up", "tn", h2, dup, outs=[BF16], out_blocks=True, **wide_k)
        dh2 = _matmul("d_h2", "nt", dup, w["w_up"], outs=[F32], **tall)
    dx1, dx1b, g_norm2, dy_sq = _rms_bwd("rms2_bwd", dh2, x1, small["norm2_g"], res=dy)
    loss = dy_sq * (0.5 * d)

    g["w_out"] = _matmul("d_w_out", "tn", merged, dx1b, outs=[BF16], **wide_k)
    dmerged = _matmul("d_merged", "nt", dx1b, w["w_out"], outs=[BF16], **tall)
    dproj, *do3 = _merge_bwd(proj, o3, dmerged, 512, tc)
    names = ("w_conv_out", "w_fox_out", "w_mem_out")
    dys = []
    for s in range(3):
        g[names[s]] = _matmul(f"d_w_branch{s}", "tn", ys[s], do3[s], outs=[BF16], out_blocks=True, **wide_k)
        dys.append(_matmul(f"d_branch{s}", "nt", do3[s], w_outs[s], outs=[BF16], **tall))

    dproj, dkv, g_mq, g_mk = _mem_bwd(proj, off_mq, kv, dys[2], small["mem_q_g"], small["mem_k_g"], tq, dproj)
    g["w_mem_kv"] = _matmul("d_w_mem_kv", "tn", nm, dkv, outs=[BF16], **wide_k)
    dnm = _matmul("d_mem_norm", "nt", dkv, w["w_mem_kv"], outs=[F32], tm=256, tn=512, tk=2048)
    _, _, g_mem_norm, _ = _rms_bwd("mem_rms_bwd", dnm, mem, small["mem_norm_g"])

    mid = ("w_out", "w_conv_out", "w_fox_out", "w_mem_out", "w_mem_kv")
    if comm:
        mid8 = [g[n] if n in names else by_owner(g[n]) for n in mid]
        dproj, g_conv_w, *got = _conv_bwd(proj, off_conv, small["conv_w"], dys[0], LANES, dproj, rider=_pair_rider(mid8))
        pairs = [pair_up] + [_pair_sum("pair_" + n, g8, s4, comm["c"]) for n, g8, s4 in zip(mid, mid8, got)]
        dproj, dc, g_fq, g_fk, *got = _fox_bwd(proj, off_fox, y_fox, dys[1], small["fox_q_g"], small["fox_k_g"], c_row3, c_colb,
                                               lse, heads, tq, dproj, rider=_chip_rider(pairs))
        parts.update(zip(("w_up",) + mid, got))
    else:
        dproj, g_conv_w = _conv_bwd(proj, off_conv, small["conv_w"], dys[0], LANES, dproj)
        dproj, dc, g_fq, g_fk = _fox_bwd(proj, off_fox, y_fox, dys[1], small["fox_q_g"], small["fox_k_g"], c_row3, c_colb,
                                         lse, heads, tq, dproj)
    dc_row = jnp.pad(dc.reshape(heads, t), ((0, F_ROWS - heads), (0, 0)))
    dz_row, db = _forget_bwd(z_row, b_col, dc_row)

    g_all = _matmul("d_w_in", "tn", dproj, h, outs=[BF16], j_outer=True, **wide_k)
    g_wf = _matmul("d_w_f", "nn", dz_row, h, outs=[BF16], tm=F_ROWS, tn=512, tk=4096)
    g["w_in"] = _unpack_g_in(g_all, g_wf, d, tc, w["w_in"].shape[0])
    dh = _matmul("d_h_f", "tn", dz_row, w_f, outs=[F32], tm=1024, tn=512, tk=F_ROWS)
    add_prev = lambda acc, prev: (acc + prev,)
    if comm:
        g_in8 = g["w_in"]
        got = _run_rider("pair_exchange_w_in", _pair_rider([g_in8]))[0]
        pair = _pair_sum("pair_w_in", g_in8, got, comm["c"])
        dh, parts["w_in"] = _matmul("d_h", "nn", dproj, w_all, outs=[F32], extras=[dh], epilogue=add_prev,
                                    rider=_chip_rider([pair]), tm=1024, tn=512, tk=3328)
    else:
        dh = _matmul("d_h", "nn", dproj, w_all, outs=[F32], extras=[dh], epilogue=add_prev, tm=1024, tn=512, tk=3328)
    grad_x, _, g_norm1, _ = _rms_bwd("rms1_bwd", dh, x, small["norm1_g"], res=dx1)

    gs = dict(norm1_g=g_norm1, b_f=db[:heads, 0], conv_w=g_conv_w, fox_q_g=g_fq.reshape(-1), fox_k_g=g_fk.reshape(-1),
              mem_norm_g=g_mem_norm, mem_q_g=g_mq, mem_k_g=g_mk, norm2_g=g_norm2)
    return loss, grad_x, (parts if comm else g), gs


def _adamw_math(w, g, m, v):
    m = ADAM_B1 * m + (1.0 - ADAM_B1) * g
    v = ADAM_B2 * v + (1.0 - ADAM_B2) * jnp.square(g)
    m_hat = m / (1.0 - ADAM_B1 ** ADAM_STEP)
    v_hat = v / (1.0 - ADAM_B2 ** ADAM_STEP)
    delta = -ADAM_LR * (m_hat / (jnp.sqrt(v_hat) + ADAM_EPS) + ADAM_WD * w)
    return delta, m, v


def _adamw(name, parts, w, m, v):
    r, c = w.shape
    n_parts, rp = parts.shape[:2]
    if rp == r:
        tr, tc = _tile2(r, c, 128, 256)
    else:
        tr, tc = _tile(rp, 256), _tile(c, 1024)

    def body(p_ref, w_ref, m_ref, v_ref, g_ref, d_ref, nm_ref, nv_ref):
        g = p_ref[0].astype(F32)
        for s in range(1, n_parts):
            g = g + p_ref[s].astype(F32)
        delta, nm, nv = _adamw_math(w_ref[...], g, m_ref[...], v_ref[...])
        g_ref[...] = g
        d_ref[...] = delta
        nm_ref[...] = nm
        nv_ref[...] = nv

    blk = pl.BlockSpec((tr, tc), lambda i, j: (i, j))
    return pl.pallas_call(
        body,
        name=name,
        grid=(rp // tr, c // tc),
        in_specs=[pl.BlockSpec((n_parts, tr, tc), lambda i, j: (0, i, j)), blk, blk, blk],
        out_specs=[blk] * 4,
        out_shape=[jax.ShapeDtypeStruct((r, c), F32)] * 4,
        compiler_params=_params(("parallel", "parallel")),
    )(parts, w, m, v)


def _sum_parts(name, parts):
    n_parts, r, c = parts.shape

    def body(p_ref, o_ref):
        acc = p_ref[0]
        for s in range(1, n_parts):
            acc = acc + p_ref[s]
        o_ref[...] = acc

    return pl.pallas_call(body, name=name, out_shape=jax.ShapeDtypeStruct((r, c), F32))(parts)


BIG = ("w_in", "w_mem_kv", "w_conv_out", "w_fox_out", "w_mem_out", "w_out", "w_up", "w_down")
COLUMN_SPLIT = ("w_in", "w_conv_out", "w_fox_out", "w_mem_out", "w_up")
SMALL = ("norm1_g", "b_f", "conv_w", "fox_q_g", "fox_k_g", "mem_norm_g", "mem_q_g", "mem_k_g", "norm2_g")
WEIGHTS = ("norm1_g", "w_in", "b_f", "conv_w", "fox_q_g", "fox_k_g", "mem_norm_g", "w_mem_kv", "mem_q_g", "mem_k_g",
           "w_conv_out", "w_fox_out", "w_mem_out", "w_out", "norm2_g", "w_up", "w_down")


def _pack(vectors):
    rows = []
    for vec in vectors:
        n = vec.shape[0]
        rows.append(jnp.pad(vec, (0, -n % LANES)).reshape(-1, LANES))
    out = jnp.concatenate(rows, axis=0)
    return jnp.pad(out, ((0, -out.shape[0] % 8), (0, 0)))


def _unpack(packed, sizes):
    out, row = [], 0
    for n in sizes:
        nr = -(-n // LANES)
        out.append(packed[row:row + nr].reshape(-1)[:n])
        row += nr
    return out


def kernel(x, mem, norm1_g, w_in, b_f, conv_w, fox_q_g, fox_k_g, mem_norm_g, w_mem_kv, mem_q_g, mem_k_g, w_conv_out, w_fox_out, w_mem_out, w_out, norm2_g, w_up, w_down, loss_target, m_norm1_g, m_w_in, m_b_f, m_conv_w, m_fox_q_g, m_fox_k_g, m_mem_norm_g, m_w_mem_kv, m_mem_q_g, m_mem_k_g, m_w_conv_out, m_w_fox_out, m_w_mem_out, m_w_out, m_norm2_g, m_w_up, m_w_down, v_norm1_g, v_w_in, v_b_f, v_conv_w, v_fox_q_g, v_fox_k_g, v_mem_norm_g, v_w_mem_kv, v_mem_q_g, v_mem_k_g, v_w_conv_out, v_w_fox_out, v_w_mem_out, v_w_out, v_norm2_g, v_w_up, v_w_down):
    args = dict(locals())
    wts = {n: args[n] for n in WEIGHTS}
    ms = {n: args["m_" + n] for n in WEIGHTS}
    vs = {n: args["v_" + n] for n in WEIGHTS}
    x_pos, y_pos, c_pos = _position()
    me = _index(x_pos, y_pos, c_pos)

    shards = {n: wts[n].astype(BF16) for n in BIG if n != "w_in"}
    rows_in = w_in.shape[1]
    shards["w_in"] = jnp.pad(w_in.T.astype(BF16), ((0, _padded_rows(rows_in) - rows_in), (0, 0)))
    wi, cw8 = _run_rider("all_gather_first", _gather_rider([shards["w_in"], conv_w]))
    full = {"w_in": wi}
    small = {n: wts[n] for n in SMALL}
    small["conv_w"] = _unblock(cw8)
    comm = {"shards": shards, "c": c_pos.astype(jnp.int32).reshape(1)}

    loss, grad_x, parts, gs = _local_step(x[0], mem[0], loss_target[0], full, small, comm)

    out_g, out_d, out_m, out_v = {}, {}, {}, {}
    for n in BIG:
        if n == "w_in":
            res = _adamw("adamw_" + n, parts[n], wts[n].T, ms[n].T, vs[n].T)
            out_g[n], out_d[n], out_m[n], out_v[n] = (r.T for r in res)
        else:
            out_g[n], out_d[n], out_m[n], out_v[n] = _adamw("adamw_" + n, parts[n], wts[n], ms[n], vs[n])

    small_sizes = [int(math.prod(gs[n].shape)) for n in SMALL]
    packed = _pack([gs[n].reshape(-1) for n in SMALL])
    gsum = _sum_parts("sum_small", _run_rider("exchange_small", _broadcast_rider([packed]))[0])
    gsmall = dict(zip(SMALL, _unpack(gsum, small_sizes)))
    cols = conv_w.shape[1]
    gsmall["conv_w"] = lax.dynamic_slice(gsmall["conv_w"].reshape(CONV_TAPS, -1), (0, me * cols), (CONV_TAPS, cols)).reshape(-1)
    pg, pw, pm, pv = (_pack([src[n].reshape(-1) for n in SMALL]) for src in (gsmall, wts, ms, vs))
    _, sd, sm, sv = _adamw("adamw_small", pg[None], pw, pm, pv)
    local_sizes = [int(math.prod(wts[n].shape)) for n in SMALL]
    for dst, src in ((out_d, sd), (out_m, sm), (out_v, sv)):
        for n, val in zip(SMALL, _unpack(src, local_sizes)):
            dst[n] = val.reshape(wts[n].shape)
    for n in SMALL:
        out_g[n] = gsmall[n].reshape(wts[n].shape)

    loss = lax.psum(loss, MESH_AXES)
    return (loss, grad_x[None], *[out_g[n] for n in WEIGHTS], *[out_d[n] for n in WEIGHTS],
            *[out_m[n] for n in WEIGHTS], *[out_v[n] for n in WEIGHTS])
```

```python
import math

import numpy as np
import jax
import jax.numpy as jnp
from jax import lax
from jax.experimental import pallas as pl
from jax.experimental.pallas import tpu as pltpu

F32 = jnp.float32
BF16 = jnp.bfloat16

EPS = 1e-6
N_DEV = 8
N_CHIPS = 4
FOX_HEAD_DIM = 128
MEM_HEADS = 4
CONV_TAPS = 3
N_BRANCHES = 3
F_ROWS = 16

ADAM_LR = 0.001
ADAM_B1 = 0.9
ADAM_B2 = 0.999
ADAM_EPS = 1e-08
ADAM_WD = 0.01
ADAM_STEP = 10

V7X_VMEM_BYTES = 64 * 1024 * 1024
VMEM_LIMIT = V7X_VMEM_BYTES * 3 // 4
LANES = 128
NEG = -1e30

MESH_AXES = ("x", "y", "c")
MESH = pl.DeviceIdType.MESH
ANY = pl.BlockSpec(memory_space=pl.ANY)

NN = (((1,), (0,)), ((), ()))
NT = (((1,), (1,)), ((), ()))
TN = (((0,), (0,)), ((), ()))


def _params(sem):
    return pltpu.CompilerParams(dimension_semantics=sem, vmem_limit_bytes=VMEM_LIMIT)


def _dot(a, b, dn):
    return lax.dot_general(a, b, dn, preferred_element_type=F32)


def _tile(n, t):
    if n <= t:
        return n
    for step in (LANES, 16):
        for cand in range(t - t % step, 0, -step):
            if n % cand == 0:
                return cand
    raise ValueError((n, t))


class _Rider:
    def __init__(self, ins, out_shapes, sem_shapes, start, finish):
        self.ins, self.out_shapes, self.sem_shapes = list(ins), list(out_shapes), list(sem_shapes)
        self.start, self.finish = start, finish


def _position():
    return lax.axis_index("x"), lax.axis_index("y"), lax.axis_index("c")


def _index(px, py, pc):
    return 4 * px + 2 * py + pc


def _dma_sems(n, per):
    return [pltpu.SemaphoreType.DMA((n, per)), pltpu.SemaphoreType.DMA((n, per)), pltpu.SemaphoreType.DMA((n,))]


def _gather_rider(shards):
    n = len(shards)

    def copies(ins, outs, sems):
        send_sems, recv_sems, local_sems = sems
        x, y, c = _position()
        me, sibling = (x, y, c), (x, y, 1 - c)
        chips = [(1 - x, y), (x, 1 - y), (1 - x, 1 - y)]

        def copy(a, k, block, to, src=None, k_send=None):
            rows = outs[a].at[_index(*block)]
            return pltpu.make_async_remote_copy(
                src_ref=rows if src is None else src, dst_ref=rows,
                send_sem=send_sems.at[a, k if k_send is None else k_send], recv_sem=recv_sems.at[a, k],
                device_id=to, device_id_type=MESH)

        mine = [pltpu.make_async_copy(ins[a], outs[a].at[_index(*me)], local_sems.at[a]) for a in range(n)]
        first = []
        for a in range(n):
            first.append(copy(a, 0, me, sibling, src=ins[a]))
            first += [copy(a, 1 + j, me, (*chips[j], c), src=ins[a]) for j in range(2)]
        return copy, mine, first, me, sibling, chips, c

    def start(ins, outs, sems):
        _, mine, first, *_ = copies(ins, outs, sems)
        for cp in mine + first:
            cp.start()

    def finish(ins, outs, sems):
        copy, mine, first, me, sibling, chips, c = copies(ins, outs, sems)

        def finish_as(kind):
            j_on, j_to = (0, 1) if kind == 1 else (1, 0)
            passed = []
            for a in range(n):
                copy(a, 1 + j_on, (*chips[j_on], c), me).wait_recv()
                passed.append(copy(a, 3, (*chips[j_on], c), (*chips[j_to], c), k_send=7))
                passed.append(copy(a, 4 + j_on, (*chips[j_on], c), sibling))
                for cp in passed[-2:]:
                    cp.start()
            for a in range(n):
                for j in (j_to, 2):
                    copy(a, 1 + j, (*chips[j], c), me).wait_recv()
                    passed.append(copy(a, 4 + j, (*chips[j], c), sibling))
                    passed[-1].start()
            for a in range(n):
                copy(a, 0, sibling, me).wait_recv()
                for j, chip in enumerate(chips):
                    copy(a, 4 + j, (*chip, 1 - c), me).wait_recv()
            for cp in first + passed:
                cp.wait_send()
            for cp in mine:
                cp.wait()

        for kind in (0, 1):
            pl.when(c == kind)(lambda kind=kind: finish_as(kind))

    out_shapes = [jax.ShapeDtypeStruct((N_DEV,) + s.shape, s.dtype) for s in shards]
    return _Rider(shards, out_shapes, _dma_sems(n, 8), start, finish)


def _pair_rider(grads):
    n = len(grads)

    def copies(ins, outs, sems):
        send_sems, recv_sems, _ = sems
        x, y, c = _position()
        return [pltpu.make_async_remote_copy(
            src_ref=ins[a].at[2 * q + (1 - c)], dst_ref=outs[a].at[q],
            send_sem=send_sems.at[a, q], recv_sem=recv_sems.at[a, q], device_id=(x, y, 1 - c), device_id_type=MESH)
            for a in range(n) for q in range(N_CHIPS)]

    def start(ins, outs, sems):
        for cp in copies(ins, outs, sems):
            cp.start()

    def finish(ins, outs, sems):
        cps = copies(ins, outs, sems)
        for cp in cps:
            cp.wait_recv()
        for cp in cps:
            cp.wait_send()

    out_shapes = [jax.ShapeDtypeStruct((N_CHIPS,) + g.shape[1:], g.dtype) for g in grads]
    return _Rider(grads, out_shapes, _dma_sems(n, N_CHIPS), start, finish)


def _chip_rider(parts):
    n = len(parts)

    def copies(ins, outs, sems):
        send_sems, recv_sems, local_sems = sems
        x, y, c = _position()
        q_me = 2 * x + y
        chips = [(1 - x, y), (x, 1 - y), (1 - x, 1 - y)]
        mine = [pltpu.make_async_copy(ins[a].at[q_me], outs[a].at[q_me], local_sems.at[a]) for a in range(n)]
        sends, arrivals = [], []
        for a in range(n):
            for j, (tx, ty) in enumerate(chips):
                q_t = 2 * tx + ty
                sends.append(pltpu.make_async_remote_copy(
                    src_ref=ins[a].at[q_t], dst_ref=outs[a].at[q_me],
                    send_sem=send_sems.at[a, j], recv_sem=recv_sems.at[a, j], device_id=(tx, ty, c), device_id_type=MESH))
                arrivals.append(pltpu.make_async_remote_copy(
                    src_ref=ins[a].at[q_t], dst_ref=outs[a].at[q_t],
                    send_sem=send_sems.at[a, j], recv_sem=recv_sems.at[a, j], device_id=(tx, ty, c), device_id_type=MESH))
        return mine, sends, arrivals

    def start(ins, outs, sems):
        mine, sends, _ = copies(ins, outs, sems)
        for cp in mine + sends:
            cp.start()

    def finish(ins, outs, sems):
        mine, sends, arrivals = copies(ins, outs, sems)
        for cp in arrivals:
            cp.wait_recv()
        for cp in sends:
            cp.wait_send()
        for cp in mine:
            cp.wait()

    out_shapes = [jax.ShapeDtypeStruct(p.shape, p.dtype) for p in parts]
    return _Rider(parts, out_shapes, _dma_sems(n, 3), start, finish)


def _broadcast_rider(values):
    n = len(values)

    def copies(ins, outs, sems):
        send_sems, recv_sems, local_sems = sems
        x, y, c = _position()
        me = _index(x, y, c)

        def peer(k):
            return (1 - x if k & 4 else x, 1 - y if k & 2 else y, 1 - c if k & 1 else c)

        mine = [pltpu.make_async_copy(ins[a], outs[a].at[me], local_sems.at[a]) for a in range(n)]
        sends, arrivals = [], []
        for a in range(n):
            for k in range(1, N_DEV):
                common = dict(send_sem=send_sems.at[a, k - 1], recv_sem=recv_sems.at[a, k - 1], device_id=peer(k), device_id_type=MESH)
                sends.append(pltpu.make_async_remote_copy(src_ref=ins[a], dst_ref=outs[a].at[me], **common))
                arrivals.append(pltpu.make_async_remote_copy(src_ref=ins[a], dst_ref=outs[a].at[_index(*peer(k))], **common))
        return mine, sends, arrivals

    def start(ins, outs, sems):
        mine, sends, _ = copies(ins, outs, sems)
        for cp in mine + sends:
            cp.start()

    def finish(ins, outs, sems):
        mine, sends, arrivals = copies(ins, outs, sems)
        for cp in arrivals:
            cp.wait_recv()
        for cp in sends:
            cp.wait_send()
        for cp in mine:
            cp.wait()

    out_shapes = [jax.ShapeDtypeStruct((N_DEV,) + v.shape, v.dtype) for v in values]
    return _Rider(values, out_shapes, _dma_sems(n, 7), start, finish)


def _run_rider(name, rider):
    n_in, n_out = len(rider.ins), len(rider.out_shapes)

    def body(*refs):
        ins, outs, sems = refs[:n_in], refs[n_in:n_in + n_out], refs[n_in + n_out:]
        rider.start(ins, outs, sems)
        rider.finish(ins, outs, sems)

    return pl.pallas_call(
        body, name=name, in_specs=[ANY] * n_in, out_specs=[ANY] * n_out, out_shape=rider.out_shapes,
        scratch_shapes=rider.sem_shapes)(*rider.ins)


class _Host:
    def __init__(self, rider):
        self.rider = rider
        self.n_in = len(rider.ins) if rider else 0
        self.n_out = len(rider.out_shapes) if rider else 0
        self.n_sem = len(rider.sem_shapes) if rider else 0
        self.ins = rider.ins if rider else []
        self.in_specs = [ANY] * self.n_in
        self.out_specs = [ANY] * self.n_out
        self.out_shapes = rider.out_shapes if rider else []
        self.scratch = rider.sem_shapes if rider else []

    def run(self, first, last, ins, outs, sems, compute):
        if self.rider is None:
            compute()
            return

        @pl.when(first)
        def _():
            self.rider.start(ins, outs, sems)

        compute()

        @pl.when(last)
        def _():
            self.rider.finish(ins, outs, sems)


def _matmul(name, kind, a, b, *, tm, tn, tk, outs, epilogue=None, extras=(), out_blocks=False, rider=None, j_outer=False):
    if kind == "nn":
        (m, kdim), n = a.shape, b.shape[1]
    elif kind == "nt":
        (m, kdim), n = a.shape, b.shape[0]
    else:
        (kdim, m), n = a.shape, b.shape[1]
    if out_blocks:
        tn = min(tn, n // N_DEV)
    tm, tn, tk = _tile(m, tm), _tile(n, tn), _tile(kdim, tk)
    ni, nj, nk = m // tm, n // tn, kdim // tk

    def spec(shape, fn):
        return pl.BlockSpec(shape, (lambda g0, g1, k: fn(g1, g0, k)) if j_outer else fn)

    a_spec = spec((tk, tm), lambda i, j, k: (k, i)) if kind == "tn" else spec((tm, tk), lambda i, j, k: (i, k))
    b_spec = spec((tn, tk), lambda i, j, k: (j, k)) if kind == "nt" else spec((tk, tn), lambda i, j, k: (k, j))
    dn = {"nn": NN, "nt": NT, "tn": TN}[kind]

    tile_spec = spec((tm, tn), lambda i, j, k: (i, j))
    if out_blocks:
        width = n // N_DEV
        r_out = width // tn
        out_shape = [jax.ShapeDtypeStruct((N_DEV, m, width), dt) for dt in outs]
        out_specs = [spec((None, tm, tn), lambda i, j, k: (j // r_out, i, j % r_out)) for _ in outs]
    else:
        out_shape = [jax.ShapeDtypeStruct((m, n), dt) for dt in outs]
        out_specs = [tile_spec for _ in outs]
    n_ex, n_out = len(extras), len(outs)
    host = _Host(rider)
    n_acc = 1 if nk > 1 else 0

    def body(*refs):
        a_ref, b_ref = refs[0], refs[1]
        pos = 2
        ex_refs = refs[pos:pos + n_ex]; pos += n_ex
        r_ins = refs[pos:pos + host.n_in]; pos += host.n_in
        out_refs = refs[pos:pos + n_out]; pos += n_out
        r_outs = refs[pos:pos + host.n_out]; pos += host.n_out
        acc_ref = refs[pos] if n_acc else None
        sems = refs[pos + n_acc:]
        i, j, k = pl.program_id(1 if j_outer else 0), pl.program_id(0 if j_outer else 1), pl.program_id(2)

        def finish_tile(acc):
            vals = (acc,) if epilogue is None else epilogue(acc, *[e[...] for e in ex_refs])
            for o_ref, v in zip(out_refs, vals):
                o_ref[...] = v.astype(o_ref.dtype)

        def compute():
            part = _dot(a_ref[...], b_ref[...], dn)
            if nk == 1:
                finish_tile(part)
                return

            @pl.when(k == 0)
            def _():
                acc_ref[...] = part

            @pl.when(jnp.logical_and(k > 0, k < nk - 1))
            def _():
                acc_ref[...] += part

            @pl.when(k == nk - 1)
            def _():
                finish_tile(acc_ref[...] + part)

        first = jnp.logical_and(jnp.logical_and(i == 0, j == 0), k == 0)
        last = jnp.logical_and(jnp.logical_and(i == ni - 1, j == nj - 1), k == nk - 1)
        host.run(first, last, r_ins, r_outs, sems, compute)

    sem = ("arbitrary",) * 3 if rider else ("parallel", "parallel", "arbitrary")
    res = pl.pallas_call(
        body,
        name=name,
        grid=(nj, ni, nk) if j_outer else (ni, nj, nk),
        in_specs=[a_spec, b_spec] + [tile_spec for _ in extras] + host.in_specs,
        out_specs=out_specs + host.out_specs,
        out_shape=out_shape + host.out_shapes,
        scratch_shapes=([pltpu.VMEM((tm, tn), F32)] if n_acc else []) + host.scratch,
        compiler_params=_params(sem),
    )(a, b, *extras, *host.ins)
    return res[0] if len(res) == 1 else res


def _rms_fwd(name, x, g, tm=512):
    t, d = x.shape
    tm = _tile(t, tm)

    def body(x_ref, g_ref, h_ref):
        xf = x_ref[...]
        r = lax.rsqrt(jnp.mean(xf * xf, axis=-1, keepdims=True) + EPS)
        h_ref[...] = (xf * r * g_ref[...]).astype(h_ref.dtype)

    return pl.pallas_call(
        body,
        name=name,
        grid=(t // tm,),
        in_specs=[pl.BlockSpec((tm, d), lambda i: (i, 0)), pl.BlockSpec((1, d), lambda i: (0, 0))],
        out_specs=pl.BlockSpec((tm, d), lambda i: (i, 0)),
        out_shape=jax.ShapeDtypeStruct((t, d), BF16),
        compiler_params=_params(("parallel",)),
    )(x, g.reshape(1, d))


def _rms_bwd(name, dh, x, g, res=None, tm=256):
    t, d = x.shape
    tm = _tile(t, tm)
    has_res = res is not None

    def body(*refs):
        if has_res:
            dh_ref, x_ref, g_ref, res_ref, dx_ref, dxb_ref, gg_ref, ss_ref = refs
        else:
            dh_ref, x_ref, g_ref, dx_ref, dxb_ref, gg_ref, ss_ref = refs
        i = pl.program_id(0)
        xf = x_ref[...]
        r = lax.rsqrt(jnp.mean(xf * xf, axis=-1, keepdims=True) + EPS)
        xh = xf * r
        dhf = dh_ref[...].astype(F32)
        dxh = dhf * g_ref[...]
        dx = r * (dxh - xh * jnp.mean(dxh * xh, axis=-1, keepdims=True))

        @pl.when(i == 0)
        def _():
            gg_ref[...] = jnp.zeros_like(gg_ref)
            ss_ref[...] = jnp.zeros_like(ss_ref)

        if has_res:
            resf = res_ref[...]
            dx = dx + resf
            ss_ref[...] += jnp.sum(jnp.sum(resf * resf, axis=0, keepdims=True), axis=1, keepdims=True)
        dx_ref[...] = dx
        dxb_ref[...] = dx.astype(BF16)
        gg_ref[...] += jnp.sum(dhf * xh, axis=0, keepdims=True)

    row = pl.BlockSpec((tm, d), lambda i: (i, 0))
    vec = pl.BlockSpec((1, d), lambda i: (0, 0))
    one = pl.BlockSpec((1, 1), lambda i: (0, 0))
    ins = [dh, x, g.reshape(1, d)] + ([res] if has_res else [])
    dx, dxb, gg, ss = pl.pallas_call(
        body,
        name=name,
        grid=(t // tm,),
        in_specs=[row, row, vec] + ([row] if has_res else []),
        out_specs=[row, row, vec, one],
        out_shape=[jax.ShapeDtypeStruct((t, d), F32), jax.ShapeDtypeStruct((t, d), BF16), jax.ShapeDtypeStruct((1, d), F32),
                   jax.ShapeDtypeStruct((1, 1), F32)],
        compiler_params=_params(("arbitrary",)),
    )(*ins)
    return dx, dxb, gg.reshape(d), ss[0, 0]


def _head_rms(xf):
    r = lax.rsqrt(jnp.mean(xf * xf, axis=-1, keepdims=True) + EPS)
    return xf * r, r


def _head_rms_bwd(dy, xn, r, g):
    dxh = dy * g
    dx = r * (dxh - xn * jnp.mean(dxh * xn, axis=-1, keepdims=True))
    return dx, jnp.sum(dy * xn, axis=0, keepdims=True)


def _col_to_row(col):
    n = col.shape[0]
    eye = lax.broadcasted_iota(jnp.int32, (n, n), 0) == lax.broadcasted_iota(jnp.int32, (n, n), 1)
    return jnp.sum(jnp.where(eye, col, 0.0), axis=0, keepdims=True)


def _row_to_col(row):
    n = row.shape[1]
    eye = lax.broadcasted_iota(jnp.int32, (n, n), 0) == lax.broadcasted_iota(jnp.int32, (n, n), 1)
    return jnp.sum(jnp.where(eye, row, 0.0), axis=1, keepdims=True)


def _dproj_args(dproj, n_in):
    if dproj is None:
        return [], [], {}
    return [dproj], [ANY], {n_in: 0}


def _shift_down(u, s, rows):
    return jnp.where(rows >= s, pltpu.roll(u, s, axis=0), 0.0)


def _shift_up(u, s, rows, t):
    return jnp.where(rows < t - s, pltpu.roll(u, t - s, axis=0), 0.0)


def _conv_fwd(proj, off, conv_w, cb):
    t = proj.shape[0]
    c = conv_w.shape[1]
    blk0 = off // (3 * cb)

    def body(p_ref, w_ref, y_ref):
        rows = lax.broadcasted_iota(jnp.int32, (t, cb), 0)
        bg = p_ref[:, 0:cb].astype(F32)
        u = p_ref[:, cb:2 * cb].astype(F32) * p_ref[:, 2 * cb:3 * cb].astype(F32)
        w = w_ref[...]
        conv = w[2:3] * u + w[1:2] * _shift_down(u, 1, rows) + w[0:1] * _shift_down(u, 2, rows)
        y_ref[...] = (bg * conv).astype(y_ref.dtype)

    return pl.pallas_call(
        body,
        name="conv_fwd",
        grid=(c // cb,),
        in_specs=[pl.BlockSpec((t, 3 * cb), lambda j: (0, blk0 + j)), pl.BlockSpec((CONV_TAPS, cb), lambda j: (0, j))],
        out_specs=pl.BlockSpec((t, cb), lambda j: (0, j)),
        out_shape=jax.ShapeDtypeStruct((t, c), BF16),
        compiler_params=_params(("parallel",)),
    )(proj, conv_w)


def _conv_bwd(proj, off, conv_w, dy, cb, dproj, rider=None):
    t = proj.shape[0]
    c = conv_w.shape[1]
    blk0 = off // (3 * cb)
    nj = c // cb
    host = _Host(rider)

    def body(*refs):
        p_ref, w_ref, dy_ref = refs[:3]
        r_ins = refs[4:4 + host.n_in]
        dp_ref, gw_ref = refs[4 + host.n_in:6 + host.n_in]
        r_outs = refs[6 + host.n_in:6 + host.n_in + host.n_out]
        sems = refs[6 + host.n_in + host.n_out:]
        j = pl.program_id(0)

        def compute():
            rows = lax.broadcasted_iota(jnp.int32, (t, cb), 0)
            bg = p_ref[:, 0:cb].astype(F32)
            cg = p_ref[:, cb:2 * cb].astype(F32)
            v = p_ref[:, 2 * cb:3 * cb].astype(F32)
            u = cg * v
            w = w_ref[...]
            u1 = _shift_down(u, 1, rows)
            u2 = _shift_down(u, 2, rows)
            conv = w[2:3] * u + w[1:2] * u1 + w[0:1] * u2
            dyf = dy_ref[...].astype(F32)
            dconv = dyf * bg
            du = w[2:3] * dconv + w[1:2] * _shift_up(dconv, 1, rows, t) + w[0:1] * _shift_up(dconv, 2, rows, t)
            dp_ref[:, 0:cb] = (dyf * conv).astype(dp_ref.dtype)
            dp_ref[:, cb:2 * cb] = (du * v).astype(dp_ref.dtype)
            dp_ref[:, 2 * cb:3 * cb] = (du * cg).astype(dp_ref.dtype)
            gw_ref[0:1, :] = jnp.sum(dconv * u2, axis=0, keepdims=True)
            gw_ref[1:2, :] = jnp.sum(dconv * u1, axis=0, keepdims=True)
            gw_ref[2:3, :] = jnp.sum(dconv * u, axis=0, keepdims=True)

        host.run(j == 0, j == nj - 1, r_ins, r_outs, sems, compute)

    res = pl.pallas_call(
        body,
        name="conv_bwd",
        grid=(nj,),
        in_specs=[
            pl.BlockSpec((t, 3 * cb), lambda j: (0, blk0 + j)),
            pl.BlockSpec((CONV_TAPS, cb), lambda j: (0, j)),
            pl.BlockSpec((t, cb), lambda j: (0, j)),
            ANY,
        ] + host.in_specs,
        out_specs=[pl.BlockSpec((t, 3 * cb), lambda j: (0, blk0 + j)), pl.BlockSpec((CONV_TAPS, cb), lambda j: (0, j))] + host.out_specs,
        out_shape=[jax.ShapeDtypeStruct(dproj.shape, dproj.dtype), jax.ShapeDtypeStruct((CONV_TAPS, c), F32)] + host.out_shapes,
        input_output_aliases={3: 0},
        scratch_shapes=host.scratch,
        compiler_params=_params(("arbitrary",)),
    )(proj, conv_w, dy, dproj, *host.ins)
    return res


def _lane_scan(x, reverse):
    lane = lax.broadcasted_iota(jnp.int32, x.shape, 1)
    s = 1
    while s < LANES:
        if reverse:
            x = x + jnp.where(lane < LANES - s, pltpu.roll(x, LANES - s, axis=1), 0.0)
        else:
            x = x + jnp.where(lane >= s, pltpu.roll(x, s, axis=1), 0.0)
        s *= 2
    return x


def _scan_rows(src_ref, dst_ref, t, reverse, fn=None):
    groups = list(range(t // LANES))
    if reverse:
        groups = groups[::-1]
    carry = None
    for gi in groups:
        sl = slice(gi * LANES, (gi + 1) * LANES)
        blk = src_ref[:, sl]
        if fn is not None:
            blk = fn(blk)
        blk = _lane_scan(blk, reverse)
        if carry is not None:
            blk = blk + carry
        dst_ref[:, sl] = blk
        carry = blk[:, 0:1] if reverse else blk[:, LANES - 1:LANES]


def _forget_fwd(z_row, b_col):
    rows, t = z_row.shape

    def body(z_ref, b_ref, c_ref):
        def logf(z):
            zz = z + b_ref[...]
            return jnp.minimum(zz, 0.0) - jnp.log(1.0 + jnp.exp(-jnp.abs(zz)))

        _scan_rows(z_ref, c_ref, t, False, logf)

    return pl.pallas_call(
        body,
        name="forget_fwd",
        out_shape=jax.ShapeDtypeStruct((rows, t), F32),
        compiler_params=pltpu.CompilerParams(vmem_limit_bytes=VMEM_LIMIT),
    )(z_row, b_col)


def _rows_to_colb(c_row3, tq):
    heads, _, t = c_row3.shape

    def body(r_ref, o_ref):
        o_ref[...] = jnp.broadcast_to(_row_to_col(r_ref[...]), (tq, LANES))

    return pl.pallas_call(
        body,
        name="rows_to_colb",
        grid=(heads, t // tq),
        in_specs=[pl.BlockSpec((None, 1, tq), lambda h, i: (h, 0, i))],
        out_specs=pl.BlockSpec((None, tq, LANES), lambda h, i: (h, i, 0)),
        out_shape=jax.ShapeDtypeStruct((heads, t, LANES), F32),
        compiler_params=_params(("parallel", "parallel")),
    )(c_row3)


def _forget_bwd(z_row, b_col, dc_row):
    rows, t = z_row.shape

    def body(z_ref, b_ref, dc_ref, dz_ref, db_ref, tmp_ref):
        _scan_rows(dc_ref, tmp_ref, t, True)
        zz = z_ref[...] + b_ref[...]
        dz = tmp_ref[...] * (1.0 / (1.0 + jnp.exp(zz)))
        dz_ref[...] = dz.astype(dz_ref.dtype)
        db_ref[...] = jnp.sum(dz, axis=1, keepdims=True)

    return pl.pallas_call(
        body,
        name="forget_bwd",
        out_shape=[jax.ShapeDtypeStruct((rows, t), BF16), jax.ShapeDtypeStruct((rows, 1), F32)],
        scratch_shapes=[pltpu.VMEM((rows, t), F32)],
        compiler_params=pltpu.CompilerParams(vmem_limit_bytes=VMEM_LIMIT),
    )(z_row, b_col, dc_row)


def _fox_fwd(proj, off, gq, gk, c_row3, c_colb, heads, tq, rider=None):
    t = proj.shape[0]
    hd = FOX_HEAD_DIM
    tq = _tile(t, tq)
    nq = t // tq
    blk0 = off // hd
    scale = 1.0 / math.sqrt(hd)
    host = _Host(rider)

    def body(*refs):
        q_ref, k_ref, v_ref, gq_ref, gk_ref, crow_ref, ccol_ref = refs[:7]
        r_ins = refs[7:7 + host.n_in]
        o_ref, lse_ref = refs[7 + host.n_in:9 + host.n_in]
        r_outs = refs[9 + host.n_in:9 + host.n_in + host.n_out]
        khat_ref, v_t_ref = refs[9 + host.n_in + host.n_out:11 + host.n_in + host.n_out]
        sems = refs[11 + host.n_in + host.n_out:]
        h, qi = pl.program_id(0), pl.program_id(1)

        def compute():
            eye = (lax.broadcasted_iota(jnp.int32, (hd, hd), 0) == lax.broadcasted_iota(jnp.int32, (hd, hd), 1)).astype(BF16)

            @pl.when(qi == 0)
            def _():
                kn, _ = _head_rms(k_ref[...].astype(F32))
                khat_ref[...] = (kn * gk_ref[...]).astype(BF16)
                v_t_ref[...] = _dot(eye, v_ref[...], NT).astype(BF16)

            qn, _ = _head_rms(q_ref[...].astype(F32))
            qhat = (qn * (gq_ref[...] * scale)).astype(BF16)
            crow = crow_ref[:, pl.ds(pl.multiple_of(qi * tq, tq), tq)]
            above = lax.broadcasted_iota(jnp.int32, (tq, tq), 1) >= lax.broadcasted_iota(jnp.int32, (tq, tq), 0)

            def tile(j, carry, diagonal):
                m, l, acc_t = carry
                ks = pl.multiple_of(j * tq, tq)
                s_t = _dot(khat_ref[pl.ds(ks, tq), :], qhat, NT) - ccol_ref[pl.ds(ks, tq), 0:1]
                if diagonal:
                    s_t = jnp.where(above, s_t, NEG)
                m_new = jnp.maximum(m, jnp.max(s_t, axis=0, keepdims=True) + crow)
                alpha = jnp.exp(m - m_new)
                p_t = jnp.exp(s_t + (crow - m_new))
                l = alpha * l + jnp.sum(p_t, axis=0, keepdims=True)
                acc_t = alpha * acc_t + _dot(v_t_ref[:, pl.ds(ks, tq)], p_t.astype(BF16), NN)
                return m_new, l, acc_t

            init = (jnp.full((1, tq), NEG, F32), jnp.zeros((1, tq), F32), jnp.zeros((hd, tq), F32))
            carry = lax.fori_loop(0, qi, lambda j, c: tile(j, c, False), init)
            m, l, acc_t = tile(qi, carry, True)
            o_ref[...] = _dot((acc_t / l).astype(BF16), eye, TN).astype(o_ref.dtype)
            lse_ref[...] = m + jnp.log(l)

        first = jnp.logical_and(h == 0, qi == 0)
        last = jnp.logical_and(h == heads - 1, qi == nq - 1)
        host.run(first, last, r_ins, r_outs, sems, compute)

    res = pl.pallas_call(
        body,
        name="fox_fwd",
        grid=(heads, nq),
        in_specs=[
            pl.BlockSpec((tq, hd), lambda h, i: (i, blk0 + 3 * h)),
            pl.BlockSpec((t, hd), lambda h, i: (0, blk0 + 3 * h + 1)),
            pl.BlockSpec((t, hd), lambda h, i: (0, blk0 + 3 * h + 2)),
            pl.BlockSpec((1, hd), lambda h, i: (0, 0)),
            pl.BlockSpec((1, hd), lambda h, i: (0, 0)),
            pl.BlockSpec((None, 1, t), lambda h, i: (h, 0, 0)),
            pl.BlockSpec((None, t, LANES), lambda h, i: (h, 0, 0)),
        ] + host.in_specs,
        out_specs=[pl.BlockSpec((tq, hd), lambda h, i: (i, h)), pl.BlockSpec((None, 1, tq), lambda h, i: (h, 0, i))] + host.out_specs,
        out_shape=[jax.ShapeDtypeStruct((t, heads * hd), BF16), jax.ShapeDtypeStruct((heads, 1, t), F32)] + host.out_shapes,
        scratch_shapes=[pltpu.VMEM((t, hd), BF16), pltpu.VMEM((hd, t), BF16)] + host.scratch,
        compiler_params=_params(("arbitrary", "arbitrary")),
    )(proj, proj, proj, gq.reshape(1, hd), gk.reshape(1, hd), c_row3, c_colb, *host.ins)
    return res


def _fox_bwd(proj, off, o, do, gq, gk, c_row3, c_colb, lse, heads, tq, dproj, rider=None):
    t = proj.shape[0]
    hd = FOX_HEAD_DIM
    tq = _tile(t, tq)
    nb = t // tq
    blk0 = off // hd
    scale = 1.0 / math.sqrt(hd)
    host = _Host(rider)
    n_fixed_in = 11

    def body(*refs):
        q_ref, k_ref, v_ref, o_ref, do_ref, gq_ref, gk_ref, crow_ref, ccol_ref, lse_ref = refs[:10]
        pos = n_fixed_in
        r_ins = refs[pos:pos + host.n_in]; pos += host.n_in
        dp_ref, dc_ref, ggq_ref, ggk_ref = refs[pos:pos + 4]; pos += 4
        r_outs = refs[pos:pos + host.n_out]; pos += host.n_out
        qhat_ref, khat_ref, khat_t_ref, dq_t_ref, dk_ref, dcq_ref, dck_ref, delta_ref = refs[pos:pos + 8]; pos += 8
        sems = refs[pos:]
        h = pl.program_id(0)

        def compute():
            qn, rq = _head_rms(q_ref[...].astype(F32))
            qhat_ref[...] = (qn * (gq_ref[...] * scale)).astype(BF16)
            kn, rk = _head_rms(k_ref[...].astype(F32))
            khat_ref[...] = (kn * gk_ref[...]).astype(BF16)
            eye = (lax.broadcasted_iota(jnp.int32, (hd, hd), 0) == lax.broadcasted_iota(jnp.int32, (hd, hd), 1)).astype(BF16)
            khat_t_ref[...] = _dot(eye, khat_ref[...], NT).astype(BF16)
            delta = jnp.sum(do_ref[...].astype(F32) * o_ref[...].astype(F32), axis=-1, keepdims=True)
            for b in range(nb):
                sl = slice(b * tq, (b + 1) * tq)
                delta_ref[:, sl] = _col_to_row(delta[sl, :])
            dq_t_ref[...] = jnp.zeros_like(dq_t_ref)
            dcq_ref[...] = jnp.zeros_like(dcq_ref)
            above = lax.broadcasted_iota(jnp.int32, (tq, tq), 1) >= lax.broadcasted_iota(jnp.int32, (tq, tq), 0)

            def kv_block(j, _):
                ks = pl.multiple_of(j * tq, tq)
                kh = khat_ref[pl.ds(ks, tq), :]
                kh_t = khat_t_ref[:, pl.ds(ks, tq)]
                vv = v_ref[pl.ds(ks, tq), :]
                ccol = ccol_ref[pl.ds(ks, tq), 0:1]

                def q_block(i, carry, diagonal):
                    dk, dv, dck = carry
                    qs = pl.multiple_of(i * tq, tq)
                    qh = qhat_ref[pl.ds(qs, tq), :]
                    dob = do_ref[pl.ds(qs, tq), :]
                    s_t = _dot(kh, qh, NT) + ((crow_ref[:, pl.ds(qs, tq)] - lse_ref[:, pl.ds(qs, tq)]) - ccol)
                    p_t = jnp.exp(s_t)
                    if diagonal:
                        p_t = jnp.where(above, p_t, 0.0)
                    ds_t = p_t * (_dot(vv, dob, NT) - delta_ref[:, pl.ds(qs, tq)])
                    dsb = ds_t.astype(BF16)
                    dv = dv + _dot(p_t.astype(BF16), dob, NN)
                    dk = dk + _dot(dsb, qh, NN)
                    dq_t_ref[:, pl.ds(qs, tq)] += _dot(kh_t, dsb, NN)
                    dcq_ref[:, pl.ds(qs, tq)] += jnp.sum(ds_t, axis=0, keepdims=True)
                    dck = dck + jnp.sum(ds_t, axis=-1, keepdims=True)
                    return dk, dv, dck

                zero = jnp.zeros((tq, hd), F32)
                carry = q_block(j, (zero, zero, jnp.zeros((tq, 1), F32)), True)
                dk, dv, dck = lax.fori_loop(j + 1, nb, lambda i, c: q_block(i, c, False), carry)
                dk_ref[pl.ds(ks, tq), :] = dk
                dp_ref[pl.ds(ks, tq), 2 * hd:3 * hd] = dv.astype(dp_ref.dtype)
                dck_ref[pl.ds(ks, tq), :] = dck
                return 0

            lax.fori_loop(0, nb, kv_block, 0)

            dq, ggq = _head_rms_bwd(dq_t_ref[...].T * scale, qn, rq, gq_ref[...])
            dk, ggk = _head_rms_bwd(dk_ref[...], kn, rk, gk_ref[...])
            dp_ref[:, 0:hd] = dq.astype(dp_ref.dtype)
            dp_ref[:, hd:2 * hd] = dk.astype(dp_ref.dtype)
            for b in range(nb):
                sl = slice(b * tq, (b + 1) * tq)
                dc_ref[:, sl] = dcq_ref[:, sl] - _col_to_row(dck_ref[sl, :])

            @pl.when(h == 0)
            def _():
                ggq_ref[...] = jnp.zeros_like(ggq_ref)
                ggk_ref[...] = jnp.zeros_like(ggk_ref)

            ggq_ref[...] += ggq
            ggk_ref[...] += ggk

        host.run(h == 0, h == heads - 1, r_ins, r_outs, sems, compute)

    head_in = lambda part: pl.BlockSpec((t, hd), lambda h: (0, blk0 + 3 * h + part))
    vec = pl.BlockSpec((1, hd), lambda h: (0, 0))
    colb = pl.BlockSpec((None, t, LANES), lambda h: (h, 0, 0))
    res = pl.pallas_call(
        body,
        name="fox_bwd",
        grid=(heads,),
        in_specs=[
            head_in(0), head_in(1), head_in(2),
            pl.BlockSpec((t, hd), lambda h: (0, h)),
            pl.BlockSpec((t, hd), lambda h: (0, h)),
            vec, vec,
            pl.BlockSpec((None, 1, t), lambda h: (h, 0, 0)),
            colb,
            pl.BlockSpec((None, 1, t), lambda h: (h, 0, 0)),
            ANY,
        ] + host.in_specs,
        out_specs=[
            pl.BlockSpec((t, 3 * hd), lambda h: (0, blk0 // 3 + h)),
            pl.BlockSpec((None, 1, t), lambda h: (h, 0, 0)),
            vec, vec,
        ] + host.out_specs,
        out_shape=[
            jax.ShapeDtypeStruct(dproj.shape, dproj.dtype),
            jax.ShapeDtypeStruct((heads, 1, t), F32),
            jax.ShapeDtypeStruct((1, hd), F32),
            jax.ShapeDtypeStruct((1, hd), F32),
        ] + host.out_shapes,
        input_output_aliases={10: 0},
        scratch_shapes=[
            pltpu.VMEM((t, hd), BF16), pltpu.VMEM((t, hd), BF16), pltpu.VMEM((hd, t), BF16),
            pltpu.VMEM((hd, t), F32), pltpu.VMEM((t, hd), F32),
            pltpu.VMEM((1, t), F32), pltpu.VMEM((t, 1), F32), pltpu.VMEM((1, t), F32),
        ] + host.scratch,
        compiler_params=_params(("arbitrary",)),
    )(proj, proj, proj, o, do, gq.reshape(1, hd), gk.reshape(1, hd), c_row3, c_colb, lse, dproj, *host.ins)
    return res


def _mem_fwd(proj, off, kv, gq, gk, tq):
    t = proj.shape[0]
    m, width = kv.shape[0], kv.shape[1] // 2
    hd = width // MEM_HEADS
    tq = _tile(t, tq)
    blk0 = off // hd
    scale = 1.0 / math.sqrt(hd)

    def body(q_ref, k_ref, v_ref, gq_ref, gk_ref, o_ref):
        qn, _ = _head_rms(q_ref[...].astype(F32))
        kn, _ = _head_rms(k_ref[...])
        s = _dot((qn * gq_ref[...]).astype(BF16), (kn * gk_ref[...]).astype(BF16), NT) * scale
        p = jnp.exp(s - jnp.max(s, axis=-1, keepdims=True))
        p = p / jnp.sum(p, axis=-1, keepdims=True)
        o_ref[...] = _dot(p.astype(BF16), v_ref[...].astype(BF16), NN).astype(o_ref.dtype)

    vec = pl.BlockSpec((1, hd), lambda h, i: (0, 0))
    return pl.pallas_call(
        body,
        name="mem_fwd",
        grid=(MEM_HEADS, t // tq),
        in_specs=[
            pl.BlockSpec((tq, hd), lambda h, i: (i, blk0 + h)),
            pl.BlockSpec((m, hd), lambda h, i: (0, h)),
            pl.BlockSpec((m, hd), lambda h, i: (0, MEM_HEADS + h)),
            vec, vec,
        ],
        out_specs=pl.BlockSpec((tq, hd), lambda h, i: (i, h)),
        out_shape=jax.ShapeDtypeStruct((t, width), BF16),
        compiler_params=_params(("parallel", "parallel")),
    )(proj, kv, kv, gq.reshape(1, hd), gk.reshape(1, hd))


def _mem_bwd(proj, off, kv, do, gq, gk, tq, dproj, rider=None):
    t = proj.shape[0]
    m, width = kv.shape[0], kv.shape[1] // 2
    hd = width // MEM_HEADS
    tq = _tile(t, tq)
    nq = t // tq
    blk0 = off // hd
    scale = 1.0 / math.sqrt(hd)
    host = _Host(rider)

    def body(*refs):
        q_ref, k_ref, v_ref, do_ref, gq_ref, gk_ref = refs[:6]
        pos = 7
        r_ins = refs[pos:pos + host.n_in]; pos += host.n_in
        dq_ref, dk_ref, dv_ref, ggq_ref, ggk_ref = refs[pos:pos + 5]; pos += 5
        r_outs = refs[pos:pos + host.n_out]; pos += host.n_out
        dkh_ref, dvh_ref = refs[pos:pos + 2]; pos += 2
        sems = refs[pos:]
        h, i = pl.program_id(0), pl.program_id(1)

        def compute():
            qn, rq = _head_rms(q_ref[...].astype(F32))
            kn, rk = _head_rms(k_ref[...])
            qhat = (qn * gq_ref[...]).astype(BF16)
            khat = (kn * gk_ref[...]).astype(BF16)
            vb = v_ref[...].astype(BF16)
            dob = do_ref[...]
            s = _dot(qhat, khat, NT) * scale
            p = jnp.exp(s - jnp.max(s, axis=-1, keepdims=True))
            p = p / jnp.sum(p, axis=-1, keepdims=True)
            dp = _dot(dob, vb, NT)
            ds = p * (dp - jnp.sum(dp * p, axis=-1, keepdims=True))
            dsb = ds.astype(BF16)
            dq, ggq = _head_rms_bwd(_dot(dsb, khat, NN) * scale, qn, rq, gq_ref[...])
            dq_ref[...] = dq.astype(dq_ref.dtype)

            @pl.when(i == 0)
            def _():
                dkh_ref[...] = jnp.zeros_like(dkh_ref)
                dvh_ref[...] = jnp.zeros_like(dvh_ref)

            @pl.when(jnp.logical_and(h == 0, i == 0))
            def _():
                ggq_ref[...] = jnp.zeros_like(ggq_ref)
                ggk_ref[...] = jnp.zeros_like(ggk_ref)

            dkh_ref[...] += _dot(dsb, qhat, TN)
            dvh_ref[...] += _dot(p.astype(BF16), dob, TN)
            ggq_ref[...] += ggq

            @pl.when(i == nq - 1)
            def _():
                dk, ggk = _head_rms_bwd(dkh_ref[...] * scale, kn, rk, gk_ref[...])
                dk_ref[...] = dk.astype(dk_ref.dtype)
                dv_ref[...] = dvh_ref[...].astype(dv_ref.dtype)
                ggk_ref[...] += ggk

        first = jnp.logical_and(h == 0, i == 0)
        last = jnp.logical_and(h == MEM_HEADS - 1, i == nq - 1)
        host.run(first, last, r_ins, r_outs, sems, compute)

    vec = pl.BlockSpec((1, hd), lambda h, i: (0, 0))
    kblk = pl.BlockSpec((m, hd), lambda h, i: (0, h))
    res = pl.pallas_call(
        body,
        name="mem_bwd",
        grid=(MEM_HEADS, nq),
        in_specs=[
            pl.BlockSpec((tq, hd), lambda h, i: (i, blk0 + h)), kblk,
            pl.BlockSpec((m, hd), lambda h, i: (0, MEM_HEADS + h)),
            pl.BlockSpec((tq, hd), lambda h, i: (i, h)), vec, vec, ANY,
        ] + host.in_specs,
        out_specs=[pl.BlockSpec((tq, hd), lambda h, i: (i, blk0 + h)), kblk, kblk, vec, vec] + host.out_specs,
        out_shape=[
            jax.ShapeDtypeStruct(dproj.shape, dproj.dtype),
            jax.ShapeDtypeStruct((m, width), BF16),
            jax.ShapeDtypeStruct((m, width), BF16),
            jax.ShapeDtypeStruct((1, hd), F32),
            jax.ShapeDtypeStruct((1, hd), F32),
        ] + host.out_shapes,
        input_output_aliases={6: 0},
        scratch_shapes=[pltpu.VMEM((m, hd), F32), pltpu.VMEM((m, hd), F32)] + host.scratch,
        compiler_params=_params(("arbitrary", "arbitrary")),
    )(proj, kv, kv, do, gq.reshape(1, hd), gk.reshape(1, hd), dproj, *host.ins)
    dproj, dk, dv, ggq, ggk = res[:5]
    return (dproj, jnp.concatenate([dk, dv], axis=1), ggq.reshape(hd), ggk.reshape(hd), *res[5:])


def _sigmoid(z):
    return 1.0 / (1.0 + jnp.exp(-z))


def _merge_fwd(proj, o3, tm, tc):
    t, d = o3[0].shape
    tm = _tile(t, tm)

    def body(g_ref, oa_ref, ob_ref, oc_ref, out_ref):
        acc = jnp.zeros((tm, tc), F32)
        for s, o_ref in enumerate((oa_ref, ob_ref, oc_ref)):
            acc = acc + _sigmoid(g_ref[:, s * tc:(s + 1) * tc].astype(F32)) * o_ref[...].astype(F32)
        out_ref[...] = acc.astype(out_ref.dtype)

    blk = pl.BlockSpec((tm, tc), lambda i, j: (i, j))
    return pl.pallas_call(
        body,
        name="merge_fwd",
        grid=(t // tm, d // tc),
        in_specs=[pl.BlockSpec((tm, 3 * tc), lambda i, j: (i, j)), blk, blk, blk],
        out_specs=blk,
        out_shape=jax.ShapeDtypeStruct((t, d), BF16),
        compiler_params=_params(("parallel", "parallel")),
    )(proj, *o3)


def _merge_bwd(proj, o3, dm, tm, tc):
    t, d = dm.shape
    tm = _tile(t, tm)

    def body(g_ref, oa_ref, ob_ref, oc_ref, dm_ref, dg_ref, da_ref, db_ref, dc_ref):
        dmf = dm_ref[...].astype(F32)
        for s, (o_ref, do_ref) in enumerate(((oa_ref, da_ref), (ob_ref, db_ref), (oc_ref, dc_ref))):
            g = _sigmoid(g_ref[:, s * tc:(s + 1) * tc].astype(F32))
            do_ref[...] = (dmf * g).astype(do_ref.dtype)
            dg_ref[:, s * tc:(s + 1) * tc] = (dmf * o_ref[...].astype(F32) * g * (1.0 - g)).astype(dg_ref.dtype)

    blk = pl.BlockSpec((tm, tc), lambda i, j: (i, j))
    wide = pl.BlockSpec((tm, 3 * tc), lambda i, j: (i, j))
    return pl.pallas_call(
        body,
        name="merge_bwd",
        grid=(t // tm, d // tc),
        in_specs=[wide, blk, blk, blk, blk],
        out_specs=[wide, blk, blk, blk],
        out_shape=[jax.ShapeDtypeStruct(proj.shape, BF16)] + [jax.ShapeDtypeStruct((t, d), BF16)] * 3,
        compiler_params=_params(("parallel", "parallel")),
    )(proj, *o3, dm)


def _w_in_chunks(d, tc):
    cw = d // 2
    heads = cw // FOX_HEAD_DIM
    conv0, fox0, f0, mq0, gate0 = 0, 3 * cw, 6 * cw, 6 * cw + heads, 7 * cw + heads
    chunks = [(gate0 + s * d + j * tc, gate0 + s * d + (j + 1) * tc) for j in range(d // tc) for s in range(N_BRANCHES)]
    chunks += [(conv0 + s * cw + j * LANES, conv0 + s * cw + (j + 1) * LANES) for j in range(cw // LANES) for s in range(3)]
    chunks += [(fox0 + s * cw + j * FOX_HEAD_DIM, fox0 + s * cw + (j + 1) * FOX_HEAD_DIM) for j in range(heads) for s in range(3)]
    chunks.append((mq0, mq0 + cw))
    return chunks, (f0, f0 + heads)


ROW_TILE = 16
GROUP = 128
GROUP_BACK = 112
SCRATCH_ROWS = 2 * GROUP + 32


def _padded_rows(r):
    return -(-r // GROUP_BACK) * GROUP_BACK


def _rows_from(scr_ref, y_ref, q8, fine, g):
    x = scr_ref[pl.ds(pl.multiple_of(q8 * 8, 8), g + 8), :]
    for s in range(8):
        @pl.when(fine == s)
        def _(s=s):
            y_ref[...] = (x if s == 0 else pltpu.roll(x, g + 8 - s, axis=0))[0:g]


def _assemble(name, tbl, grid, step, in_specs, out_spec, out_shape, operands, g, w1, cols_of):
    has_f = len(in_specs) == 3
    k = out_shape.shape[-1]
    c = cols_of

    def body(*refs):
        t_ref, s1_ref, s2_ref = refs[:3]
        f_ref = refs[3] if has_f else None
        out_ref = refs[3 + has_f]
        scr1, scr2, scrf, y_ref = refs[4 + has_f:]
        t = step()

        @pl.when(t == 0)
        def _():
            scr1[...] = jnp.zeros_like(scr1)
            scr2[...] = jnp.zeros_like(scr2)
            scrf[...] = jnp.zeros_like(scrf)

        rows = lax.broadcasted_iota(jnp.int32, (g, k), 0)
        n1, a2 = t_ref[c["n1"], t], t_ref[c["a2"], t]
        scr1[0:w1, :] = (s1_ref[0] if len(s1_ref.shape) == 3 else s1_ref[...]).astype(F32)
        _rows_from(scr1, y_ref, t_ref[c["q1"], t], t_ref[c["s1"], t], g)
        out_ref[...] = y_ref[...].astype(out_ref.dtype)

        @pl.when(a2 < g)
        def _():
            scr2[g:g + s2_ref.shape[0], :] = s2_ref[...].astype(F32)
            _rows_from(scr2, y_ref, t_ref[c["q2"], t], t_ref[c["s2"], t], g)
            out_ref[...] = jnp.where(rows < n1, out_ref[...].astype(F32), y_ref[...]).astype(out_ref.dtype)

        if has_f:
            fa, fb = t_ref[c["fa"], t], t_ref[c["fb"], t]

            @pl.when(fb > fa)
            def _():
                scrf[g:g + f_ref.shape[0], :] = f_ref[...].astype(F32)
                _rows_from(scrf, y_ref, t_ref[c["qf"], t], t_ref[c["sf"], t], g)
                inside = jnp.logical_and(rows >= fa, rows < fb)
                out_ref[...] = jnp.where(inside, y_ref[...], out_ref[...].astype(F32)).astype(out_ref.dtype)

            valid = t_ref[c["valid"], t]

            @pl.when(valid < g)
            def _():
                out_ref[...] = jnp.where(rows < valid, out_ref[...].astype(F32), 0.0).astype(out_ref.dtype)

    return pl.pallas_call(
        body,
        name=name,
        grid_spec=pltpu.PrefetchScalarGridSpec(
            num_scalar_prefetch=1, grid=grid, in_specs=in_specs, out_specs=out_spec,
            scratch_shapes=[pltpu.VMEM((SCRATCH_ROWS, k), F32)] * 3 + [pltpu.VMEM((g, k), F32)]),
        out_shape=out_shape,
        compiler_params=_params(("arbitrary",) * len(grid)),
    )(jnp.asarray(tbl), *operands)


def _pack_w_in(w8, d, tc):
    blocks, rp, k = w8.shape
    chunks, (f_lo, f_hi) = _w_in_chunks(d, tc)
    r = max(hi for _, hi in chunks) // blocks
    g, w1 = GROUP, GROUP + ROW_TILE
    table = []
    for lo, hi in chunks:
        for g0 in range(lo, hi, g):
            b1, r1 = divmod(g0, r)
            n1 = min(g, r - r1)
            st1 = min(r1 // ROW_TILE * ROW_TILE, rp - w1)
            o1, o2 = r1 - st1, g - n1
            b2 = b1 + 1 if n1 < g else 0
            table.append((b1, st1, o1 // 8, o1 % 8, n1, n1, b2, o2 // 8, o2 % 8))
    names = ("b1", "st1", "q1", "s1", "n1", "a2", "b2", "q2", "s2")
    cols_of = {n: i for i, n in enumerate(names)}
    tbl = np.array(table, np.int32).T
    c = cols_of
    w_all = _assemble(
        "pack_w_in", tbl, (len(table),), lambda: pl.program_id(0),
        [pl.BlockSpec((pl.Element(1), pl.Element(w1), pl.Element(k)), lambda i, t: (t[c["b1"], i], pl.multiple_of(t[c["st1"], i], ROW_TILE), 0)),
         pl.BlockSpec((None, g, k), lambda i, t: (t[c["b2"], i], 0, 0))],
        pl.BlockSpec((g, k), lambda i, t: (i, 0)),
        jax.ShapeDtypeStruct((len(table) * g, k), w8.dtype), [w8, w8], g, w1, cols_of)
    fb, fr = divmod(f_lo, r)
    return w_all, jnp.pad(w8[fb, fr:fr + f_hi - f_lo], ((0, F_ROWS - (f_hi - f_lo)), (0, 0)))


def _unpack_g_in(g_all, g_f, d, tc, blocks):
    n_all, k = g_all.shape
    chunks, (f_lo, f_hi) = _w_in_chunks(d, tc)
    r = max(hi for _, hi in chunks) // blocks
    rp = _padded_rows(r)
    g, w1 = GROUP_BACK, GROUP_BACK + ROW_TILE
    pos, spans = 0, [(f_lo, f_hi, None)]
    for lo, hi in chunks:
        spans.append((lo, hi, pos))
        pos += hi - lo
    spans.sort()
    table = []
    for b in range(blocks):
        for l0 in range(0, rp, g):
            valid = max(0, min(g, r - l0))
            g0, segs, fa, fb, of = b * r + l0, [], 0, 0, 0
            for lo, hi, p in spans:
                a, e = max(lo, g0), min(hi, g0 + valid)
                if a < e and p is None:
                    fa, fb, of = a - g0, e - g0, g + (a - lo) - (a - g0)
                elif a < e:
                    segs.append((a - g0, p + a - lo, e - a))
            assert len(segs) <= 2 and (not segs or segs[0][0] == 0 or len(segs) == 1)
            first = segs[0] if segs and segs[0][0] == 0 else (0, 0, 0)
            second = segs[-1] if segs and segs[-1][0] > 0 else (g, 0, 0)
            st1 = min(first[1] // ROW_TILE * ROW_TILE, n_all - w1)
            o1, o2 = first[1] - st1, g - second[0]
            assert second[1] % GROUP == 0
            table.append((st1, o1 // 8, o1 % 8, first[2], second[0], second[1] // GROUP, o2 // 8, o2 % 8,
                          fa, fb, of // 8, of % 8, valid))
    names = ("st1", "q1", "s1", "n1", "a2", "j2", "q2", "s2", "fa", "fb", "qf", "sf", "valid")
    cols_of = {n: i for i, n in enumerate(names)}
    tbl = np.array(table, np.int32).T
    c, per = cols_of, rp // g
    return _assemble(
        "unpack_g_in", tbl, (blocks, per), lambda: pl.program_id(0) * per + pl.program_id(1),
        [pl.BlockSpec((pl.Element(w1), pl.Element(k)), lambda b, u, t: (pl.multiple_of(t[c["st1"], b * per + u], ROW_TILE), 0)),
         pl.BlockSpec((GROUP, k), lambda b, u, t: (t[c["j2"], b * per + u], 0)),
         pl.BlockSpec((F_ROWS, k), lambda b, u, t: (0, 0))],
        pl.BlockSpec((None, g, k), lambda b, u, t: (b, u, 0)),
        jax.ShapeDtypeStruct((blocks, rp, k), g_all.dtype), [g_all, g_all, g_f], g, w1, cols_of)


def _unblock(w8):
    return w8.transpose(1, 0, 2).reshape(w8.shape[1], -1)


def _tile2(r, cols, tr, tcols):
    if r % 8 == 0:
        return _tile(r, tr), cols
    return r, _tile(cols, tcols)


def _pair_sum(name, g8, got, c):
    def body(c_ref, g_ref, s_ref, o_ref):
        o_ref[...] = (g_ref[...].astype(F32) + s_ref[...].astype(F32)).astype(o_ref.dtype)

    if g8.ndim == 4:
        _, r, k1, k2 = g8.shape
        tr = max(cand for cand in range(1, 385) if r % cand == 0)
        grid = (N_CHIPS, r // tr)
        shape = (None, tr, k1, k2)
        own = pl.BlockSpec(shape, lambda q, i, c_ref: (2 * q + c_ref[0], i, 0, 0))
        blk = pl.BlockSpec(shape, lambda q, i, c_ref: (q, i, 0, 0))
    else:
        _, r, cols = g8.shape
        tr, tcols = _tile2(r, cols, 256, 256)
        grid = (N_CHIPS, r // tr, cols // tcols)
        own = pl.BlockSpec((None, tr, tcols), lambda q, i, j, c_ref: (2 * q + c_ref[0], i, j))
        blk = pl.BlockSpec((None, tr, tcols), lambda q, i, j, c_ref: (q, i, j))
    return pl.pallas_call(
        body,
        name=name,
        grid_spec=pltpu.PrefetchScalarGridSpec(num_scalar_prefetch=1, grid=grid, in_specs=[own, blk], out_specs=blk),
        out_shape=jax.ShapeDtypeStruct((N_CHIPS,) + g8.shape[1:], BF16),
        compiler_params=_params(("parallel",) * len(grid)),
    )(c, g8, got)


def _local_step(x, mem, target, w, small, comm=None):
    t, d = x.shape
    cw = d // 2
    heads = cw // FOX_HEAD_DIM
    tc = min(512, d)
    tq = min(512, t)
    off_conv, off_fox, off_mq = 3 * d, 3 * d + 3 * cw, 3 * d + 6 * cw
    w = dict(w)
    w_all, w_f = _pack_w_in(w["w_in"], d, tc)
    big = dict(tm=1024, tn=512, tk=2048)
    wide_k = dict(tm=512, tn=1024, tk=4096)
    tall = dict(tm=2048, tn=512, tk=2048)

    h = _rms_fwd("rms1_fwd", x, small["norm1_g"])
    if comm:
        early = ("w_conv_out", "w_fox_out", "w_mem_out", "w_out", "w_mem_kv")
        proj, *got = _matmul("proj", "nt", h, w_all, outs=[BF16], rider=_gather_rider([comm["shards"][n] for n in early]), **tall)
        for n, val in zip(early, got):
            w[n] = _unblock(val) if n in COLUMN_SPLIT else val.reshape(-1, val.shape[-1])
    else:
        proj = _matmul("proj", "nt", h, w_all, outs=[BF16], **tall)
    z_row = _matmul("proj_f", "nt", w_f, h, outs=[F32], tm=F_ROWS, tn=512, tk=2048)

    y_conv = _conv_fwd(proj, off_conv, small["conv_w"], LANES)

    b_col = jnp.pad(small["b_f"], (0, F_ROWS - heads)).reshape(F_ROWS, 1)
    c_row3 = _forget_fwd(z_row, b_col)[:heads].reshape(heads, 1, t)
    c_colb = _rows_to_colb(c_row3, tq)
    if comm:
        y_fox, lse, got = _fox_fwd(proj, off_fox, small["fox_q_g"], small["fox_k_g"], c_row3, c_colb, heads, tq,
                                   rider=_gather_rider([comm["shards"]["w_up"]]))
        w["w_up"] = _unblock(got)
    else:
        y_fox, lse = _fox_fwd(proj, off_fox, small["fox_q_g"], small["fox_k_g"], c_row3, c_colb, heads, tq)

    nm = _rms_fwd("mem_rms_fwd", mem, small["mem_norm_g"])
    kv = _matmul("mem_kv", "nn", nm, w["w_mem_kv"], outs=[F32], tm=256, tn=512, tk=2048)
    y_mem = _mem_fwd(proj, off_mq, kv, small["mem_q_g"], small["mem_k_g"], tq)

    ys = (y_conv, y_fox, y_mem)
    w_outs = (w["w_conv_out"], w["w_fox_out"], w["w_mem_out"])
    o3 = [_matmul(f"branch_out{s}", "nn", ys[s], w_outs[s], outs=[BF16], **big) for s in range(3)]
    merged = _merge_fwd(proj, o3, 512, tc)
    x1 = _matmul("out_proj", "nn", merged, w["w_out"], outs=[F32], extras=[x],
                 epilogue=lambda acc, xr: (acc + xr,), **big)
    h2 = _rms_fwd("rms2_fwd", x1, small["norm2_g"])

    def up_epilogue(acc):
        return acc, jnp.square(jnp.maximum(acc, 0.0))

    if comm:
        up, act, got = _matmul("mlp_up", "nn", h2, w["w_up"], outs=[BF16, BF16], epilogue=up_epilogue,
                               rider=_gather_rider([comm["shards"]["w_down"]]), **big)
        w["w_down"] = got.reshape(-1, got.shape[-1])
    else:
        up, act = _matmul("mlp_up", "nn", h2, w["w_up"], outs=[BF16, BF16], epilogue=up_epilogue, **big)

    def loss_epilogue(acc, x1r, tr):
        dy = (acc + x1r - tr) * (1.0 / d)
        return dy, dy

    dy, dyb = _matmul("mlp_down", "nn", act, w["w_down"], outs=[F32, BF16], extras=[x1, target],
                      epilogue=loss_epilogue, tm=1024, tn=512, tk=4096)

    def dup_epilogue(acc, upr):
        return (acc * 2.0 * jnp.maximum(upr.astype(F32), 0.0),)

    def by_owner(g):
        return g.reshape(N_DEV, -1, g.shape[-1])

    g, parts = {}, {}
    g["w_down"] = _matmul("d_w_down", "tn", act, dyb, outs=[BF16], **wide_k)
    if comm:
        dup, got = _matmul("d_act", "nt", dyb, w["w_down"], outs=[BF16], extras=[up], epilogue=dup_epilogue,
                           rider=_pair_rider([by_owner(g["w_down"])]), **tall)
        pair = _pair_sum("pair_w_down", by_owner(g["w_down"]), got, comm["c"])
        g["w_up"], parts["w_down"] = _matmul("d_w_up", "tn", h2, dup, outs=[BF16], out_blocks=True,
                                             rider=_chip_rider([pair]), **wide_k)
        dh2, got = _matmul("d_h2", "nt", dup, w["w_up"], outs=[F32], rider=_pair_rider([g["w_up"]]), **tall)
        pair_up = _pair_sum("pair_w_up", g["w_up"], got, comm["c"])
    else:
        dup = _matmul("d_act", "nt", dyb, w["w_down"], outs=[BF16], extras=[up], epilogue=dup_epilogue, **tall)
        g["w_up"] = _matmul("d_w_up", "tn", h2, dup, outs=[BF16], out_blocks=True, **wide_k)
        dh2 = _matmul("d_h2", "nt", dup, w["w_up"], outs=[F32], **tall)
    dx1, dx1b, g_norm2, dy_sq = _rms_bwd("rms2_bwd", dh2, x1, small["norm2_g"], res=dy)
    loss = dy_sq * (0.5 * d)

    g["w_out"] = _matmul("d_w_out", "tn", merged, dx1b, outs=[BF16], **wide_k)
    dmerged = _matmul("d_merged", "nt", dx1b, w["w_out"], outs=[BF16], **tall)
    dproj, *do3 = _merge_bwd(proj, o3, dmerged, 512, tc)
    names = ("w_conv_out", "w_fox_out", "w_mem_out")
    dys = []
    for s in range(3):
        g[names[s]] = _matmul(f"d_w_branch{s}", "tn", ys[s], do3[s], outs=[BF16], out_blocks=True, **wide_k)
        dys.append(_matmul(f"d_branch{s}", "nt", do3[s], w_outs[s], outs=[BF16], **tall))

    dproj, dkv, g_mq, g_mk = _mem_bwd(proj, off_mq, kv, dys[2], small["mem_q_g"], small["mem_k_g"], tq, dproj)
    g["w_mem_kv"] = _matmul("d_w_mem_kv", "tn", nm, dkv, outs=[BF16], **wide_k)
    dnm = _matmul("d_mem_norm", "nt", dkv, w["w_mem_kv"], outs=[F32], tm=256, tn=512, tk=2048)
    _, _, g_mem_norm, _ = _rms_bwd("mem_rms_bwd", dnm, mem, small["mem_norm_g"])

    mid = ("w_out", "w_conv_out", "w_fox_out", "w_mem_out", "w_mem_kv")
    if comm:
        mid8 = [g[n] if n in names else by_owner(g[n]) for n in mid]
        dproj, g_conv_w, *got = _conv_bwd(proj, off_conv, small["conv_w"], dys[0], LANES, dproj, rider=_pair_rider(mid8))
        pairs = [pair_up] + [_pair_sum("pair_" + n, g8, s4, comm["c"]) for n, g8, s4 in zip(mid, mid8, got)]
        dproj, dc, g_fq, g_fk, *got = _fox_bwd(proj, off_fox, y_fox, dys[1], small["fox_q_g"], small["fox_k_g"], c_row3, c_colb,
                                               lse, heads, tq, dproj, rider=_chip_rider(pairs))
        parts.update(zip(("w_up",) + mid, got))
    else:
        dproj, g_conv_w = _conv_bwd(proj, off_conv, small["conv_w"], dys[0], LANES, dproj)
        dproj, dc, g_fq, g_fk = _fox_bwd(proj, off_fox, y_fox, dys[1], small["fox_q_g"], small["fox_k_g"], c_row3, c_colb,
                                         lse, heads, tq, dproj)
    dc_row = jnp.pad(dc.reshape(heads, t), ((0, F_ROWS - heads), (0, 0)))
    dz_row, db = _forget_bwd(z_row, b_col, dc_row)

    g_all = _matmul("d_w_in", "tn", dproj, h, outs=[BF16], j_outer=True, **wide_k)
    g_wf = _matmul("d_w_f", "nn", dz_row, h, outs=[BF16], tm=F_ROWS, tn=512, tk=4096)
    g["w_in"] = _unpack_g_in(g_all, g_wf, d, tc, w["w_in"].shape[0])
    dh = _matmul("d_h_f", "tn", dz_row, w_f, outs=[F32], tm=1024, tn=512, tk=F_ROWS)
    add_prev = lambda acc, prev: (acc + prev,)
    if comm:
        g_in8 = g["w_in"]
        got = _run_rider("pair_exchange_w_in", _pair_rider([g_in8]))[0]
        pair = _pair_sum("pair_w_in", g_in8, got, comm["c"])
        dh, parts["w_in"] = _matmul("d_h", "nn", dproj, w_all, outs=[F32], extras=[dh], epilogue=add_prev,
                                    rider=_chip_rider([pair]), tm=1024, tn=512, tk=3328)
    else:
        dh = _matmul("d_h", "nn", dproj, w_all, outs=[F32], extras=[dh], epilogue=add_prev, tm=1024, tn=512, tk=3328)
    grad_x, _, g_norm1, _ = _rms_bwd("rms1_bwd", dh, x, small["norm1_g"], res=dx1)

    gs = dict(norm1_g=g_norm1, b_f=db[:heads, 0], conv_w=g_conv_w, fox_q_g=g_fq.reshape(-1), fox_k_g=g_fk.reshape(-1),
              mem_norm_g=g_mem_norm, mem_q_g=g_mq, mem_k_g=g_mk, norm2_g=g_norm2)
    return loss, grad_x, (parts if comm else g), gs


def _adamw_math(w, g, m, v):
    m = ADAM_B1 * m + (1.0 - ADAM_B1) * g
    v = ADAM_B2 * v + (1.0 - ADAM_B2) * jnp.square(g)
    m_hat = m / (1.0 - ADAM_B1 ** ADAM_STEP)
    v_hat = v / (1.0 - ADAM_B2 ** ADAM_STEP)
    delta = -ADAM_LR * (m_hat / (jnp.sqrt(v_hat) + ADAM_EPS) + ADAM_WD * w)
    return delta, m, v


def _adamw(name, parts, w, m, v):
    r, c = w.shape
    n_parts, rp = parts.shape[:2]
    if rp == r:
        tr, tc = _tile2(r, c, 128, 256)
    else:
        tr, tc = _tile(rp, 256), _tile(c, 1024)

    def body(p_ref, w_ref, m_ref, v_ref, g_ref, d_ref, nm_ref, nv_ref):
        g = p_ref[0].astype(F32)
        for s in range(1, n_parts):
            g = g + p_ref[s].astype(F32)
        delta, nm, nv = _adamw_math(w_ref[...], g, m_ref[...], v_ref[...])
        g_ref[...] = g
        d_ref[...] = delta
        nm_ref[...] = nm
        nv_ref[...] = nv

    blk = pl.BlockSpec((tr, tc), lambda i, j: (i, j))
    return pl.pallas_call(
        body,
        name=name,
        grid=(rp // tr, c // tc),
        in_specs=[pl.BlockSpec((n_parts, tr, tc), lambda i, j: (0, i, j)), blk, blk, blk],
        out_specs=[blk] * 4,
        out_shape=[jax.ShapeDtypeStruct((r, c), F32)] * 4,
        compiler_params=_params(("parallel", "parallel")),
    )(parts, w, m, v)


def _sum_parts(name, parts):
    n_parts, r, c = parts.shape

    def body(p_ref, o_ref):
        acc = p_ref[0]
        for s in range(1, n_parts):
            acc = acc + p_ref[s]
        o_ref[...] = acc

    return pl.pallas_call(body, name=name, out_shape=jax.ShapeDtypeStruct((r, c), F32))(parts)


BIG = ("w_in", "w_mem_kv", "w_conv_out", "w_fox_out", "w_mem_out", "w_out", "w_up", "w_down")
COLUMN_SPLIT = ("w_in", "w_conv_out", "w_fox_out", "w_mem_out", "w_up")
SMALL = ("norm1_g", "b_f", "conv_w", "fox_q_g", "fox_k_g", "mem_norm_g", "mem_q_g", "mem_k_g", "norm2_g")
WEIGHTS = ("norm1_g", "w_in", "b_f", "conv_w", "fox_q_g", "fox_k_g", "mem_norm_g", "w_mem_kv", "mem_q_g", "mem_k_g",
           "w_conv_out", "w_fox_out", "w_mem_out", "w_out", "norm2_g", "w_up", "w_down")


def _pack(vectors):
    rows = []
    for vec in vectors:
        n = vec.shape[0]
        rows.append(jnp.pad(vec, (0, -n % LANES)).reshape(-1, LANES))
    out = jnp.concatenate(rows, axis=0)
    return jnp.pad(out, ((0, -out.shape[0] % 8), (0, 0)))


def _unpack(packed, sizes):
    out, row = [], 0
    for n in sizes:
        nr = -(-n // LANES)
        out.append(packed[row:row + nr].reshape(-1)[:n])
        row += nr
    return out


def kernel(x, mem, norm1_g, w_in, b_f, conv_w, fox_q_g, fox_k_g, mem_norm_g, w_mem_kv, mem_q_g, mem_k_g, w_conv_out, w_fox_out, w_mem_out, w_out, norm2_g, w_up, w_down, loss_target, m_norm1_g, m_w_in, m_b_f, m_conv_w, m_fox_q_g, m_fox_k_g, m_mem_norm_g, m_w_mem_kv, m_mem_q_g, m_mem_k_g, m_w_conv_out, m_w_fox_out, m_w_mem_out, m_w_out, m_norm2_g, m_w_up, m_w_down, v_norm1_g, v_w_in, v_b_f, v_conv_w, v_fox_q_g, v_fox_k_g, v_mem_norm_g, v_w_mem_kv, v_mem_q_g, v_mem_k_g, v_w_conv_out, v_w_fox_out, v_w_mem_out, v_w_out, v_norm2_g, v_w_up, v_w_down):
    args = dict(locals())
    wts = {n: args[n] for n in WEIGHTS}
    ms = {n: args["m_" + n] for n in WEIGHTS}
    vs = {n: args["v_" + n] for n in WEIGHTS}
    x_pos, y_pos, c_pos = _position()
    me = _index(x_pos, y_pos, c_pos)

    shards = {n: wts[n].astype(BF16) for n in BIG if n != "w_in"}
    rows_in = w_in.shape[1]
    shards["w_in"] = jnp.pad(w_in.T.astype(BF16), ((0, _padded_rows(rows_in) - rows_in), (0, 0)))
    wi, cw8 = _run_rider("all_gather_first", _gather_rider([shards["w_in"], conv_w]))
    full = {"w_in": wi}
    small = {n: wts[n] for n in SMALL}
    small["conv_w"] = _unblock(cw8)
    comm = {"shards": shards, "c": c_pos.astype(jnp.int32).reshape(1)}

    loss, grad_x, parts, gs = _local_step(x[0], mem[0], loss_target[0], full, small, comm)

    out_g, out_d, out_m, out_v = {}, {}, {}, {}
    for n in BIG:
        if n == "w_in":
            res = _adamw("adamw_" + n, parts[n], wts[n].T, ms[n].T, vs[n].T)
            out_g[n], out_d[n], out_m[n], out_v[n] = (r.T for r in res)
        else:
            out_g[n], out_d[n], out_m[n], out_v[n] = _adamw("adamw_" + n, parts[n], wts[n], ms[n], vs[n])

    small_sizes = [int(math.prod(gs[n].shape)) for n in SMALL]
    packed = _pack([gs[n].reshape(-1) for n in SMALL])
    gsum = _sum_parts("sum_small", _run_rider("exchange_small", _broadcast_rider([packed]))[0])
    gsmall = dict(zip(SMALL, _unpack(gsum, small_sizes)))
    cols = conv_w.shape[1]
    gsmall["conv_w"] = lax.dynamic_slice(gsmall["conv_w"].reshape(CONV_TAPS, -1), (0, me * cols), (CONV_TAPS, cols)).reshape(-1)
    pg, pw, pm, pv = (_pack([src[n].reshape(-1) for n in SMALL]) for src in (gsmall, wts, ms, vs))
    _, sd, sm, sv = _adamw("adamw_small", pg[None], pw, pm, pv)
    local_sizes = [int(math.prod(wts[n].shape)) for n in SMALL]
    for dst, src in ((out_d, sd), (out_m, sm), (out_v, sv)):
        for n, val in zip(SMALL, _unpack(src, local_sizes)):
            dst[n] = val.reshape(wts[n].shape)
    for n in SMALL:
        out_g[n] = gsmall[n].reshape(wts[n].shape)

    loss = lax.psum(loss, MESH_AXES)
    return (loss, grad_x[None], *[out_g[n] for n in WEIGHTS], *[out_d[n] for n in WEIGHTS],
            *[out_m[n] for n in WEIGHTS], *[out_v[n] for n in WEIGHTS])
```

```python
import math

import numpy as np
import jax
import jax.numpy as jnp
from jax import lax
from jax.experimental import pallas as pl
from jax.experimental.pallas import tpu as pltpu

F32 = jnp.float32
BF16 = jnp.bfloat16

EPS = 1e-6
N_DEV = 8
N_CHIPS = 4
FOX_HEAD_DIM = 128
MEM_HEADS = 4
CONV_TAPS = 3
N_BRANCHES = 3
F_ROWS = 16

ADAM_LR = 0.001
ADAM_B1 = 0.9
ADAM_B2 = 0.999
ADAM_EPS = 1e-08
ADAM_WD = 0.01
ADAM_STEP = 10

V7X_VMEM_BYTES = 64 * 1024 * 1024
VMEM_LIMIT = V7X_VMEM_BYTES * 3 // 4
LANES = 128
NEG = -1e30

MESH_AXES = ("x", "y", "c")
MESH = pl.DeviceIdType.MESH
ANY = pl.BlockSpec(memory_space=pl.ANY)

NN = (((1,), (0,)), ((), ()))
NT = (((1,), (1,)), ((), ()))
TN = (((0,), (0,)), ((), ()))


def _params(sem):
    return pltpu.CompilerParams(dimension_semantics=sem, vmem_limit_bytes=VMEM_LIMIT)


def _dot(a, b, dn):
    return lax.dot_general(a, b, dn, preferred_element_type=F32)


def _tile(n, t):
    if n <= t:
        return n
    for step in (LANES, 16):
        for cand in range(t - t % step, 0, -step):
            if n % cand == 0:
                return cand
    raise ValueError((n, t))


class _Rider:
    def __init__(self, ins, out_shapes, sem_shapes, start, finish):
        self.ins, self.out_shapes, self.sem_shapes = list(ins), list(out_shapes), list(sem_shapes)
        self.start, self.finish = start, finish


def _position():
    return lax.axis_index("x"), lax.axis_index("y"), lax.axis_index("c")


def _index(px, py, pc):
    return 4 * px + 2 * py + pc


def _dma_sems(n, per):
    return [pltpu.SemaphoreType.DMA((n, per)), pltpu.SemaphoreType.DMA((n, per)), pltpu.SemaphoreType.DMA((n,))]


def _gather_rider(shards, pass_on):
    n = len(shards)

    def copies(ins, outs, sems):
        send_sems, recv_sems, local_sems = sems
        x, y, c = _position()
        me, sibling = (x, y, c), (x, y, 1 - c)
        chips = [(1 - x, y), (x, 1 - y), (1 - x, 1 - y)]

        def copy(a, k, block, to, src=None, k_send=None):
            rows = outs[a].at[_index(*block)]
            return pltpu.make_async_remote_copy(
                src_ref=rows if src is None else src, dst_ref=rows,
                send_sem=send_sems.at[a, k if k_send is None else k_send], recv_sem=recv_sems.at[a, k],
                device_id=to, device_id_type=MESH)

        mine = [pltpu.make_async_copy(ins[a], outs[a].at[_index(*me)], local_sems.at[a]) for a in range(n)]
        first = []
        for a in range(n):
            first.append(copy(a, 0, me, sibling, src=ins[a]))
            first += [copy(a, 1 + j, me, (*chips[j], c), src=ins[a]) for j in range(2 if pass_on else 3)]
        return copy, mine, first, me, sibling, chips, c

    def start(ins, outs, sems):
        _, mine, first, *_ = copies(ins, outs, sems)
        for cp in mine + first:
            cp.start()

    def finish(ins, outs, sems):
        copy, mine, first, me, sibling, chips, c = copies(ins, outs, sems)

        def finish_as(kind):
            j_on, j_to = (0, 1) if kind == 1 else (1, 0)
            passed = []
            for a in range(n):
                copy(a, 1 + j_on, (*chips[j_on], c), me).wait_recv()
                if pass_on:
                    passed.append(copy(a, 3, (*chips[j_on], c), (*chips[j_to], c), k_send=7))
                    passed[-1].start()
                passed.append(copy(a, 4 + j_on, (*chips[j_on], c), sibling))
                passed[-1].start()
            for a in range(n):
                for j in (j_to, 2):
                    copy(a, 1 + j, (*chips[j], c), me).wait_recv()
                    passed.append(copy(a, 4 + j, (*chips[j], c), sibling))
                    passed[-1].start()
            for a in range(n):
                copy(a, 0, sibling, me).wait_recv()
                for j, chip in enumerate(chips):
                    copy(a, 4 + j, (*chip, 1 - c), me).wait_recv()
            for cp in first + passed:
                cp.wait_send()
            for cp in mine:
                cp.wait()

        if pass_on:
            for kind in (0, 1):
                pl.when(c == kind)(lambda kind=kind: finish_as(kind))
        else:
            finish_as(1)

    out_shapes = [jax.ShapeDtypeStruct((N_DEV,) + s.shape, s.dtype) for s in shards]
    return _Rider(shards, out_shapes, _dma_sems(n, 8), start, finish)


def _pair_rider(grads):
    n = len(grads)

    def copies(ins, outs, sems):
        send_sems, recv_sems, _ = sems
        x, y, c = _position()
        return [pltpu.make_async_remote_copy(
            src_ref=ins[a].at[2 * q + (1 - c)], dst_ref=outs[a].at[q],
            send_sem=send_sems.at[a, q], recv_sem=recv_sems.at[a, q], device_id=(x, y, 1 - c), device_id_type=MESH)
            for a in range(n) for q in range(N_CHIPS)]

    def start(ins, outs, sems):
        for cp in copies(ins, outs, sems):
            cp.start()

    def finish(ins, outs, sems):
        cps = copies(ins, outs, sems)
        for cp in cps:
            cp.wait_recv()
        for cp in cps:
            cp.wait_send()

    out_shapes = [jax.ShapeDtypeStruct((N_CHIPS,) + g.shape[1:], g.dtype) for g in grads]
    return _Rider(grads, out_shapes, _dma_sems(n, N_CHIPS), start, finish)


def _chip_rider(parts):
    n = len(parts)

    def copies(ins, outs, sems):
        send_sems, recv_sems, local_sems = sems
        x, y, c = _position()
        q_me = 2 * x + y
        chips = [(1 - x, y), (x, 1 - y), (1 - x, 1 - y)]
        mine = [pltpu.make_async_copy(ins[a].at[q_me], outs[a].at[q_me], local_sems.at[a]) for a in range(n)]
        sends, arrivals = [], []
        for a in range(n):
            for j, (tx, ty) in enumerate(chips):
                q_t = 2 * tx + ty
                sends.append(pltpu.make_async_remote_copy(
                    src_ref=ins[a].at[q_t], dst_ref=outs[a].at[q_me],
                    send_sem=send_sems.at[a, j], recv_sem=recv_sems.at[a, j], device_id=(tx, ty, c), device_id_type=MESH))
                arrivals.append(pltpu.make_async_remote_copy(
                    src_ref=ins[a].at[q_t], dst_ref=outs[a].at[q_t],
                    send_sem=send_sems.at[a, j], recv_sem=recv_sems.at[a, j], device_id=(tx, ty, c), device_id_type=MESH))
        return mine, sends, arrivals

    def start(ins, outs, sems):
        mine, sends, _ = copies(ins, outs, sems)
        for cp in mine + sends:
            cp.start()

    def finish(ins, outs, sems):
        mine, sends, arrivals = copies(ins, outs, sems)
        for cp in arrivals:
            cp.wait_recv()
        for cp in sends:
            cp.wait_send()
        for cp in mine:
            cp.wait()

    out_shapes = [jax.ShapeDtypeStruct(p.shape, p.dtype) for p in parts]
    return _Rider(parts, out_shapes, _dma_sems(n, 3), start, finish)


def _broadcast_rider(values):
    n = len(values)

    def copies(ins, outs, sems):
        send_sems, recv_sems, local_sems = sems
        x, y, c = _position()
        me = _index(x, y, c)

        def peer(k):
            return (1 - x if k & 4 else x, 1 - y if k & 2 else y, 1 - c if k & 1 else c)

        mine = [pltpu.make_async_copy(ins[a], outs[a].at[me], local_sems.at[a]) for a in range(n)]
        sends, arrivals = [], []
        for a in range(n):
            for k in range(1, N_DEV):
                common = dict(send_sem=send_sems.at[a, k - 1], recv_sem=recv_sems.at[a, k - 1], device_id=peer(k), device_id_type=MESH)
                sends.append(pltpu.make_async_remote_copy(src_ref=ins[a], dst_ref=outs[a].at[me], **common))
                arrivals.append(pltpu.make_async_remote_copy(src_ref=ins[a], dst_ref=outs[a].at[_index(*peer(k))], **common))
        return mine, sends, arrivals

    def start(ins, outs, sems):
        mine, sends, _ = copies(ins, outs, sems)
        for cp in mine + sends:
            cp.start()

    def finish(ins, outs, sems):
        mine, sends, arrivals = copies(ins, outs, sems)
        for cp in arrivals:
            cp.wait_recv()
        for cp in sends:
            cp.wait_send()
        for cp in mine:
            cp.wait()

    out_shapes = [jax.ShapeDtypeStruct((N_DEV,) + v.shape, v.dtype) for v in values]
    return _Rider(values, out_shapes, _dma_sems(n, 7), start, finish)


def _run_rider(name, rider):
    n_in, n_out = len(rider.ins), len(rider.out_shapes)

    def body(*refs):
        ins, outs, sems = refs[:n_in], refs[n_in:n_in + n_out], refs[n_in + n_out:]
        rider.start(ins, outs, sems)
        rider.finish(ins, outs, sems)

    return pl.pallas_call(
        body, name=name, in_specs=[ANY] * n_in, out_specs=[ANY] * n_out, out_shape=rider.out_shapes,
        scratch_shapes=rider.sem_shapes)(*rider.ins)


class _Host:
    def __init__(self, rider):
        self.rider = rider
        self.n_in = len(rider.ins) if rider else 0
        self.n_out = len(rider.out_shapes) if rider else 0
        self.n_sem = len(rider.sem_shapes) if rider else 0
        self.ins = rider.ins if rider else []
        self.in_specs = [ANY] * self.n_in
        self.out_specs = [ANY] * self.n_out
        self.out_shapes = rider.out_shapes if rider else []
        self.scratch = rider.sem_shapes if rider else []

    def run(self, first, last, ins, outs, sems, compute):
        if self.rider is None:
            compute()
            return

        @pl.when(first)
        def _():
            self.rider.start(ins, outs, sems)

        compute()

        @pl.when(last)
        def _():
            self.rider.finish(ins, outs, sems)


def _matmul(name, kind, a, b, *, tm, tn, tk, outs, epilogue=None, extras=(), out_blocks=False, rider=None, j_outer=False):
    if kind == "nn":
        (m, kdim), n = a.shape, b.shape[1]
    elif kind == "nt":
        (m, kdim), n = a.shape, b.shape[0]
    else:
        (kdim, m), n = a.shape, b.shape[1]
    if out_blocks:
        tn = min(tn, n // N_DEV)
    tm, tn, tk = _tile(m, tm), _tile(n, tn), _tile(kdim, tk)
    ni, nj, nk = m // tm, n // tn, kdim // tk

    def spec(shape, fn):
        return pl.BlockSpec(shape, (lambda g0, g1, k: fn(g1, g0, k)) if j_outer else fn)

    a_spec = spec((tk, tm), lambda i, j, k: (k, i)) if kind == "tn" else spec((tm, tk), lambda i, j, k: (i, k))
    b_spec = spec((tn, tk), lambda i, j, k: (j, k)) if kind == "nt" else spec((tk, tn), lambda i, j, k: (k, j))
    dn = {"nn": NN, "nt": NT, "tn": TN}[kind]

    tile_spec = spec((tm, tn), lambda i, j, k: (i, j))
    if out_blocks:
        width = n // N_DEV
        r_out = width // tn
        out_shape = [jax.ShapeDtypeStruct((N_DEV, m, width), dt) for dt in outs]
        out_specs = [spec((None, tm, tn), lambda i, j, k: (j // r_out, i, j % r_out)) for _ in outs]
    else:
        out_shape = [jax.ShapeDtypeStruct((m, n), dt) for dt in outs]
        out_specs = [tile_spec for _ in outs]
    n_ex, n_out = len(extras), len(outs)
    host = _Host(rider)
    n_acc = 1 if nk > 1 else 0

    def body(*refs):
        a_ref, b_ref = refs[0], refs[1]
        pos = 2
        ex_refs = refs[pos:pos + n_ex]; pos += n_ex
        r_ins = refs[pos:pos + host.n_in]; pos += host.n_in
        out_refs = refs[pos:pos + n_out]; pos += n_out
        r_outs = refs[pos:pos + host.n_out]; pos += host.n_out
        acc_ref = refs[pos] if n_acc else None
        sems = refs[pos + n_acc:]
        i, j, k = pl.program_id(1 if j_outer else 0), pl.program_id(0 if j_outer else 1), pl.program_id(2)

        def finish_tile(acc):
            vals = (acc,) if epilogue is None else epilogue(acc, *[e[...] for e in ex_refs])
            for o_ref, v in zip(out_refs, vals):
                o_ref[...] = v.astype(o_ref.dtype)

        def compute():
            part = _dot(a_ref[...], b_ref[...], dn)
            if nk == 1:
                finish_tile(part)
                return

            @pl.when(k == 0)
            def _():
                acc_ref[...] = part

            @pl.when(jnp.logical_and(k > 0, k < nk - 1))
            def _():
                acc_ref[...] += part

            @pl.when(k == nk - 1)
            def _():
                finish_tile(acc_ref[...] + part)

        first = jnp.logical_and(jnp.logical_and(i == 0, j == 0), k == 0)
        last = jnp.logical_and(jnp.logical_and(i == ni - 1, j == nj - 1), k == nk - 1)
        host.run(first, last, r_ins, r_outs, sems, compute)

    sem = ("arbitrary",) * 3 if rider else ("parallel", "parallel", "arbitrary")
    res = pl.pallas_call(
        body,
        name=name,
        grid=(nj, ni, nk) if j_outer else (ni, nj, nk),
        in_specs=[a_spec, b_spec] + [tile_spec for _ in extras] + host.in_specs,
        out_specs=out_specs + host.out_specs,
        out_shape=out_shape + host.out_shapes,
        scratch_shapes=([pltpu.VMEM((tm, tn), F32)] if n_acc else []) + host.scratch,
        compiler_params=_params(sem),
    )(a, b, *extras, *host.ins)
    return res[0] if len(res) == 1 else res


def _rms_fwd(name, x, g, tm=512):
    t, d = x.shape
    tm = _tile(t, tm)

    def body(x_ref, g_ref, h_ref):
        xf = x_ref[...]
        r = lax.rsqrt(jnp.mean(xf * xf, axis=-1, keepdims=True) + EPS)
        h_ref[...] = (xf * r * g_ref[...]).astype(h_ref.dtype)

    return pl.pallas_call(
        body,
        name=name,
        grid=(t // tm,),
        in_specs=[pl.BlockSpec((tm, d), lambda i: (i, 0)), pl.BlockSpec((1, d), lambda i: (0, 0))],
        out_specs=pl.BlockSpec((tm, d), lambda i: (i, 0)),
        out_shape=jax.ShapeDtypeStruct((t, d), BF16),
        compiler_params=_params(("parallel",)),
    )(x, g.reshape(1, d))


def _rms_bwd(name, dh, x, g, res=None, tm=256):
    t, d = x.shape
    tm = _tile(t, tm)
    has_res = res is not None

    def body(*refs):
        if has_res:
            dh_ref, x_ref, g_ref, res_ref, dx_ref, dxb_ref, gg_ref, ss_ref = refs
        else:
            dh_ref, x_ref, g_ref, dx_ref, dxb_ref, gg_ref, ss_ref = refs
        i = pl.program_id(0)
        xf = x_ref[...]
        r = lax.rsqrt(jnp.mean(xf * xf, axis=-1, keepdims=True) + EPS)
        xh = xf * r
        dhf = dh_ref[...].astype(F32)
        dxh = dhf * g_ref[...]
        dx = r * (dxh - xh * jnp.mean(dxh * xh, axis=-1, keepdims=True))

        @pl.when(i == 0)
        def _():
            gg_ref[...] = jnp.zeros_like(gg_ref)
            ss_ref[...] = jnp.zeros_like(ss_ref)

        if has_res:
            resf = res_ref[...]
            dx = dx + resf
            ss_ref[...] += jnp.sum(jnp.sum(resf * resf, axis=0, keepdims=True), axis=1, keepdims=True)
        dx_ref[...] = dx
        dxb_ref[...] = dx.astype(BF16)
        gg_ref[...] += jnp.sum(dhf * xh, axis=0, keepdims=True)

    row = pl.BlockSpec((tm, d), lambda i: (i, 0))
    vec = pl.BlockSpec((1, d), lambda i: (0, 0))
    one = pl.BlockSpec((1, 1), lambda i: (0, 0))
    ins = [dh, x, g.reshape(1, d)] + ([res] if has_res else [])
    dx, dxb, gg, ss = pl.pallas_call(
        body,
        name=name,
        grid=(t // tm,),
        in_specs=[row, row, vec] + ([row] if has_res else []),
        out_specs=[row, row, vec, one],
        out_shape=[jax.ShapeDtypeStruct((t, d), F32), jax.ShapeDtypeStruct((t, d), BF16), jax.ShapeDtypeStruct((1, d), F32),
                   jax.ShapeDtypeStruct((1, 1), F32)],
        compiler_params=_params(("arbitrary",)),
    )(*ins)
    return dx, dxb, gg.reshape(d), ss[0, 0]


def _head_rms(xf):
    r = lax.rsqrt(jnp.mean(xf * xf, axis=-1, keepdims=True) + EPS)
    return xf * r, r


def _head_rms_bwd(dy, xn, r, g):
    dxh = dy * g
    dx = r * (dxh - xn * jnp.mean(dxh * xn, axis=-1, keepdims=True))
    return dx, jnp.sum(dy * xn, axis=0, keepdims=True)


def _col_to_row(col):
    n = col.shape[0]
    eye = lax.broadcasted_iota(jnp.int32, (n, n), 0) == lax.broadcasted_iota(jnp.int32, (n, n), 1)
    return jnp.sum(jnp.where(eye, col, 0.0), axis=0, keepdims=True)


def _row_to_col(row):
    n = row.shape[1]
    eye = lax.broadcasted_iota(jnp.int32, (n, n), 0) == lax.broadcasted_iota(jnp.int32, (n, n), 1)
    return jnp.sum(jnp.where(eye, row, 0.0), axis=1, keepdims=True)


def _dproj_args(dproj, n_in):
    if dproj is None:
        return [], [], {}
    return [dproj], [ANY], {n_in: 0}


def _shift_down(u, s, rows):
    return jnp.where(rows >= s, pltpu.roll(u, s, axis=0), 0.0)


def _shift_up(u, s, rows, t):
    return jnp.where(rows < t - s, pltpu.roll(u, t - s, axis=0), 0.0)


def _conv_fwd(proj, off, conv_w, cb):
    t = proj.shape[0]
    c = conv_w.shape[1]
    blk0 = off // (3 * cb)

    def body(p_ref, w_ref, y_ref):
        rows = lax.broadcasted_iota(jnp.int32, (t, cb), 0)
        bg = p_ref[:, 0:cb].astype(F32)
        u = p_ref[:, cb:2 * cb].astype(F32) * p_ref[:, 2 * cb:3 * cb].astype(F32)
        w = w_ref[...]
        conv = w[2:3] * u + w[1:2] * _shift_down(u, 1, rows) + w[0:1] * _shift_down(u, 2, rows)
        y_ref[...] = (bg * conv).astype(y_ref.dtype)

    return pl.pallas_call(
        body,
        name="conv_fwd",
        grid=(c // cb,),
        in_specs=[pl.BlockSpec((t, 3 * cb), lambda j: (0, blk0 + j)), pl.BlockSpec((CONV_TAPS, cb), lambda j: (0, j))],
        out_specs=pl.BlockSpec((t, cb), lambda j: (0, j)),
        out_shape=jax.ShapeDtypeStruct((t, c), BF16),
        compiler_params=_params(("parallel",)),
    )(proj, conv_w)


def _conv_bwd(proj, off, conv_w, dy, cb, dproj, rider=None):
    t = proj.shape[0]
    c = conv_w.shape[1]
    blk0 = off // (3 * cb)
    nj = c // cb
    host = _Host(rider)

    def body(*refs):
        p_ref, w_ref, dy_ref = refs[:3]
        r_ins = refs[4:4 + host.n_in]
        dp_ref, gw_ref = refs[4 + host.n_in:6 + host.n_in]
        r_outs = refs[6 + host.n_in:6 + host.n_in + host.n_out]
        sems = refs[6 + host.n_in + host.n_out:]
        j = pl.program_id(0)

        def compute():
            rows = lax.broadcasted_iota(jnp.int32, (t, cb), 0)
            bg = p_ref[:, 0:cb].astype(F32)
            cg = p_ref[:, cb:2 * cb].astype(F32)
            v = p_ref[:, 2 * cb:3 * cb].astype(F32)
            u = cg * v
            w = w_ref[...]
            u1 = _shift_down(u, 1, rows)
            u2 = _shift_down(u, 2, rows)
            conv = w[2:3] * u + w[1:2] * u1 + w[0:1] * u2
            dyf = dy_ref[...].astype(F32)
            dconv = dyf * bg
            du = w[2:3] * dconv + w[1:2] * _shift_up(dconv, 1, rows, t) + w[0:1] * _shift_up(dconv, 2, rows, t)
            dp_ref[:, 0:cb] = (dyf * conv).astype(dp_ref.dtype)
            dp_ref[:, cb:2 * cb] = (du * v).astype(dp_ref.dtype)
            dp_ref[:, 2 * cb:3 * cb] = (du * cg).astype(dp_ref.dtype)
            gw_ref[0:1, :] = jnp.sum(dconv * u2, axis=0, keepdims=True)
            gw_ref[1:2, :] = jnp.sum(dconv * u1, axis=0, keepdims=True)
            gw_ref[2:3, :] = jnp.sum(dconv * u, axis=0, keepdims=True)

        host.run(j == 0, j == nj - 1, r_ins, r_outs, sems, compute)

    res = pl.pallas_call(
        body,
        name="conv_bwd",
        grid=(nj,),
        in_specs=[
            pl.BlockSpec((t, 3 * cb), lambda j: (0, blk0 + j)),
            pl.BlockSpec((CONV_TAPS, cb), lambda j: (0, j)),
            pl.BlockSpec((t, cb), lambda j: (0, j)),
            ANY,
        ] + host.in_specs,
        out_specs=[pl.BlockSpec((t, 3 * cb), lambda j: (0, blk0 + j)), pl.BlockSpec((CONV_TAPS, cb), lambda j: (0, j))] + host.out_specs,
        out_shape=[jax.ShapeDtypeStruct(dproj.shape, dproj.dtype), jax.ShapeDtypeStruct((CONV_TAPS, c), F32)] + host.out_shapes,
        input_output_aliases={3: 0},
        scratch_shapes=host.scratch,
        compiler_params=_params(("arbitrary",)),
    )(proj, conv_w, dy, dproj, *host.ins)
    return res


def _lane_scan(x, reverse):
    lane = lax.broadcasted_iota(jnp.int32, x.shape, 1)
    s = 1
    while s < LANES:
        if reverse:
            x = x + jnp.where(lane < LANES - s, pltpu.roll(x, LANES - s, axis=1), 0.0)
        else:
            x = x + jnp.where(lane >= s, pltpu.roll(x, s, axis=1), 0.0)
        s *= 2
    return x


def _scan_rows(src_ref, dst_ref, t, reverse, fn=None):
    groups = list(range(t // LANES))
    if reverse:
        groups = groups[::-1]
    carry = None
    for gi in groups:
        sl = slice(gi * LANES, (gi + 1) * LANES)
        blk = src_ref[:, sl]
        if fn is not None:
            blk = fn(blk)
        blk = _lane_scan(blk, reverse)
        if carry is not None:
            blk = blk + carry
        dst_ref[:, sl] = blk
        carry = blk[:, 0:1] if reverse else blk[:, LANES - 1:LANES]


def _forget_fwd(z_row, b_col):
    rows, t = z_row.shape

    def body(z_ref, b_ref, c_ref):
        def logf(z):
            zz = z + b_ref[...]
            return jnp.minimum(zz, 0.0) - jnp.log(1.0 + jnp.exp(-jnp.abs(zz)))

        _scan_rows(z_ref, c_ref, t, False, logf)

    return pl.pallas_call(
        body,
        name="forget_fwd",
        out_shape=jax.ShapeDtypeStruct((rows, t), F32),
        compiler_params=pltpu.CompilerParams(vmem_limit_bytes=VMEM_LIMIT),
    )(z_row, b_col)


def _rows_to_colb(c_row3, tq):
    heads, _, t = c_row3.shape

    def body(r_ref, o_ref):
        o_ref[...] = jnp.broadcast_to(_row_to_col(r_ref[...]), (tq, LANES))

    return pl.pallas_call(
        body,
        name="rows_to_colb",
        grid=(heads, t // tq),
        in_specs=[pl.BlockSpec((None, 1, tq), lambda h, i: (h, 0, i))],
        out_specs=pl.BlockSpec((None, tq, LANES), lambda h, i: (h, i, 0)),
        out_shape=jax.ShapeDtypeStruct((heads, t, LANES), F32),
        compiler_params=_params(("parallel", "parallel")),
    )(c_row3)


def _forget_bwd(z_row, b_col, dc_row):
    rows, t = z_row.shape

    def body(z_ref, b_ref, dc_ref, dz_ref, db_ref, tmp_ref):
        _scan_rows(dc_ref, tmp_ref, t, True)
        zz = z_ref[...] + b_ref[...]
        dz = tmp_ref[...] * (1.0 / (1.0 + jnp.exp(zz)))
        dz_ref[...] = dz.astype(dz_ref.dtype)
        db_ref[...] = jnp.sum(dz, axis=1, keepdims=True)

    return pl.pallas_call(
        body,
        name="forget_bwd",
        out_shape=[jax.ShapeDtypeStruct((rows, t), BF16), jax.ShapeDtypeStruct((rows, 1), F32)],
        scratch_shapes=[pltpu.VMEM((rows, t), F32)],
        compiler_params=pltpu.CompilerParams(vmem_limit_bytes=VMEM_LIMIT),
    )(z_row, b_col, dc_row)


def _fox_fwd(proj, off, gq, gk, c_row3, c_colb, heads, tq, rider=None):
    t = proj.shape[0]
    hd = FOX_HEAD_DIM
    tq = _tile(t, tq)
    nq = t // tq
    blk0 = off // hd
    scale = 1.0 / math.sqrt(hd)
    host = _Host(rider)

    def body(*refs):
        q_ref, k_ref, v_ref, gq_ref, gk_ref, crow_ref, ccol_ref = refs[:7]
        r_ins = refs[7:7 + host.n_in]
        o_ref, lse_ref = refs[7 + host.n_in:9 + host.n_in]
        r_outs = refs[9 + host.n_in:9 + host.n_in + host.n_out]
        khat_ref, v_t_ref = refs[9 + host.n_in + host.n_out:11 + host.n_in + host.n_out]
        sems = refs[11 + host.n_in + host.n_out:]
        h, qi = pl.program_id(0), pl.program_id(1)

        def compute():
            eye = (lax.broadcasted_iota(jnp.int32, (hd, hd), 0) == lax.broadcasted_iota(jnp.int32, (hd, hd), 1)).astype(BF16)

            @pl.when(qi == 0)
            def _():
                kn, _ = _head_rms(k_ref[...].astype(F32))
                khat_ref[...] = (kn * gk_ref[...]).astype(BF16)
                v_t_ref[...] = _dot(eye, v_ref[...], NT).astype(BF16)

            qn, _ = _head_rms(q_ref[...].astype(F32))
            qhat = (qn * (gq_ref[...] * scale)).astype(BF16)
            crow = crow_ref[:, pl.ds(pl.multiple_of(qi * tq, tq), tq)]
            above = lax.broadcasted_iota(jnp.int32, (tq, tq), 1) >= lax.broadcasted_iota(jnp.int32, (tq, tq), 0)

            def tile(j, carry, diagonal):
                m, l, acc_t = carry
                ks = pl.multiple_of(j * tq, tq)
                s_t = _dot(khat_ref[pl.ds(ks, tq), :], qhat, NT) - ccol_ref[pl.ds(ks, tq), 0:1]
                if diagonal:
                    s_t = jnp.where(above, s_t, NEG)
                m_new = jnp.maximum(m, jnp.max(s_t, axis=0, keepdims=True) + crow)
                alpha = jnp.exp(m - m_new)
                p_t = jnp.exp(s_t + (crow - m_new))
                l = alpha * l + jnp.sum(p_t, axis=0, keepdims=True)
                acc_t = alpha * acc_t + _dot(v_t_ref[:, pl.ds(ks, tq)], p_t.astype(BF16), NN)
                return m_new, l, acc_t

            init = (jnp.full((1, tq), NEG, F32), jnp.zeros((1, tq), F32), jnp.zeros((hd, tq), F32))
            carry = lax.fori_loop(0, qi, lambda j, c: tile(j, c, False), init)
            m, l, acc_t = tile(qi, carry, True)
            o_ref[...] = _dot((acc_t / l).astype(BF16), eye, TN).astype(o_ref.dtype)
            lse_ref[...] = m + jnp.log(l)

        first = jnp.logical_and(h == 0, qi == 0)
        last = jnp.logical_and(h == heads - 1, qi == nq - 1)
        host.run(first, last, r_ins, r_outs, sems, compute)

    res = pl.pallas_call(
        body,
        name="fox_fwd",
        grid=(heads, nq),
        in_specs=[
            pl.BlockSpec((tq, hd), lambda h, i: (i, blk0 + 3 * h)),
            pl.BlockSpec((t, hd), lambda h, i: (0, blk0 + 3 * h + 1)),
            pl.BlockSpec((t, hd), lambda h, i: (0, blk0 + 3 * h + 2)),
            pl.BlockSpec((1, hd), lambda h, i: (0, 0)),
            pl.BlockSpec((1, hd), lambda h, i: (0, 0)),
            pl.BlockSpec((None, 1, t), lambda h, i: (h, 0, 0)),
            pl.BlockSpec((None, t, LANES), lambda h, i: (h, 0, 0)),
        ] + host.in_specs,
        out_specs=[pl.BlockSpec((tq, hd), lambda h, i: (i, h)), pl.BlockSpec((None, 1, tq), lambda h, i: (h, 0, i))] + host.out_specs,
        out_shape=[jax.ShapeDtypeStruct((t, heads * hd), BF16), jax.ShapeDtypeStruct((heads, 1, t), F32)] + host.out_shapes,
        scratch_shapes=[pltpu.VMEM((t, hd), BF16), pltpu.VMEM((hd, t), BF16)] + host.scratch,
        compiler_params=_params(("arbitrary", "arbitrary")),
    )(proj, proj, proj, gq.reshape(1, hd), gk.reshape(1, hd), c_row3, c_colb, *host.ins)
    return res


def _fox_bwd(proj, off, o, do, gq, gk, c_row3, c_colb, lse, heads, tq, dproj, rider=None):
    t = proj.shape[0]
    hd = FOX_HEAD_DIM
    tq = _tile(t, tq)
    nb = t // tq
    blk0 = off // hd
    scale = 1.0 / math.sqrt(hd)
    host = _Host(rider)
    n_fixed_in = 11

    def body(*refs):
        q_ref, k_ref, v_ref, o_ref, do_ref, gq_ref, gk_ref, crow_ref, ccol_ref, lse_ref = refs[:10]
        pos = n_fixed_in
        r_ins = refs[pos:pos + host.n_in]; pos += host.n_in
        dp_ref, dc_ref, ggq_ref, ggk_ref = refs[pos:pos + 4]; pos += 4
        r_outs = refs[pos:pos + host.n_out]; pos += host.n_out
        qhat_ref, khat_ref, khat_t_ref, dq_t_ref, dk_ref, dcq_ref, dck_ref, delta_ref = refs[pos:pos + 8]; pos += 8
        sems = refs[pos:]
        h = pl.program_id(0)

        def compute():
            qn, rq = _head_rms(q_ref[...].astype(F32))
            qhat_ref[...] = (qn * (gq_ref[...] * scale)).astype(BF16)
            kn, rk = _head_rms(k_ref[...].astype(F32))
            khat_ref[...] = (kn * gk_ref[...]).astype(BF16)
            eye = (lax.broadcasted_iota(jnp.int32, (hd, hd), 0) == lax.broadcasted_iota(jnp.int32, (hd, hd), 1)).astype(BF16)
            khat_t_ref[...] = _dot(eye, khat_ref[...], NT).astype(BF16)
            delta = jnp.sum(do_ref[...].astype(F32) * o_ref[...].astype(F32), axis=-1, keepdims=True)
            for b in range(nb):
                sl = slice(b * tq, (b + 1) * tq)
                delta_ref[:, sl] = _col_to_row(delta[sl, :])
            dq_t_ref[...] = jnp.zeros_like(dq_t_ref)
            dcq_ref[...] = jnp.zeros_like(dcq_ref)
            above = lax.broadcasted_iota(jnp.int32, (tq, tq), 1) >= lax.broadcasted_iota(jnp.int32, (tq, tq), 0)

            def kv_block(j, _):
                ks = pl.multiple_of(j * tq, tq)
                kh = khat_ref[pl.ds(ks, tq), :]
                kh_t = khat_t_ref[:, pl.ds(ks, tq)]
                vv = v_ref[pl.ds(ks, tq), :]
                ccol = ccol_ref[pl.ds(ks, tq), 0:1]

                def q_block(i, carry, diagonal):
                    dk, dv, dck = carry
                    qs = pl.multiple_of(i * tq, tq)
                    qh = qhat_ref[pl.ds(qs, tq), :]
                    dob = do_ref[pl.ds(qs, tq), :]
                    s_t = _dot(kh, qh, NT) + ((crow_ref[:, pl.ds(qs, tq)] - lse_ref[:, pl.ds(qs, tq)]) - ccol)
                    p_t = jnp.exp(s_t)
                    if diagonal:
                        p_t = jnp.where(above, p_t, 0.0)
                    ds_t = p_t * (_dot(vv, dob, NT) - delta_ref[:, pl.ds(qs, tq)])
                    dsb = ds_t.astype(BF16)
                    dv = dv + _dot(p_t.astype(BF16), dob, NN)
                    dk = dk + _dot(dsb, qh, NN)
                    dq_t_ref[:, pl.ds(qs, tq)] += _dot(kh_t, dsb, NN)
                    dcq_ref[:, pl.ds(qs, tq)] += jnp.sum(ds_t, axis=0, keepdims=True)
                    dck = dck + jnp.sum(ds_t, axis=-1, keepdims=True)
                    return dk, dv, dck

                zero = jnp.zeros((tq, hd), F32)
                carry = q_block(j, (zero, zero, jnp.zeros((tq, 1), F32)), True)
                dk, dv, dck = lax.fori_loop(j + 1, nb, lambda i, c: q_block(i, c, False), carry)
                dk_ref[pl.ds(ks, tq), :] = dk
                dp_ref[pl.ds(ks, tq), 2 * hd:3 * hd] = dv.astype(dp_ref.dtype)
                dck_ref[pl.ds(ks, tq), :] = dck
                return 0

            lax.fori_loop(0, nb, kv_block, 0)

            dq, ggq = _head_rms_bwd(dq_t_ref[...].T * scale, qn, rq, gq_ref[...])
            dk, ggk = _head_rms_bwd(dk_ref[...], kn, rk, gk_ref[...])
            dp_ref[:, 0:hd] = dq.astype(dp_ref.dtype)
            dp_ref[:, hd:2 * hd] = dk.astype(dp_ref.dtype)
            for b in range(nb):
                sl = slice(b * tq, (b + 1) * tq)
                dc_ref[:, sl] = dcq_ref[:, sl] - _col_to_row(dck_ref[sl, :])

            @pl.when(h == 0)
            def _():
                ggq_ref[...] = jnp.zeros_like(ggq_ref)
                ggk_ref[...] = jnp.zeros_like(ggk_ref)

            ggq_ref[...] += ggq
            ggk_ref[...] += ggk

        host.run(h == 0, h == heads - 1, r_ins, r_outs, sems, compute)

    head_in = lambda part: pl.BlockSpec((t, hd), lambda h: (0, blk0 + 3 * h + part))
    vec = pl.BlockSpec((1, hd), lambda h: (0, 0))
    colb = pl.BlockSpec((None, t, LANES), lambda h: (h, 0, 0))
    res = pl.pallas_call(
        body,
        name="fox_bwd",
        grid=(heads,),
        in_specs=[
            head_in(0), head_in(1), head_in(2),
            pl.BlockSpec((t, hd), lambda h: (0, h)),
            pl.BlockSpec((t, hd), lambda h: (0, h)),
            vec, vec,
            pl.BlockSpec((None, 1, t), lambda h: (h, 0, 0)),
            colb,
            pl.BlockSpec((None, 1, t), lambda h: (h, 0, 0)),
            ANY,
        ] + host.in_specs,
        out_specs=[
            pl.BlockSpec((t, 3 * hd), lambda h: (0, blk0 // 3 + h)),
            pl.BlockSpec((None, 1, t), lambda h: (h, 0, 0)),
            vec, vec,
        ] + host.out_specs,
        out_shape=[
            jax.ShapeDtypeStruct(dproj.shape, dproj.dtype),
            jax.ShapeDtypeStruct((heads, 1, t), F32),
            jax.ShapeDtypeStruct((1, hd), F32),
            jax.ShapeDtypeStruct((1, hd), F32),
        ] + host.out_shapes,
        input_output_aliases={10: 0},
        scratch_shapes=[
            pltpu.VMEM((t, hd), BF16), pltpu.VMEM((t, hd), BF16), pltpu.VMEM((hd, t), BF16),
            pltpu.VMEM((hd, t), F32), pltpu.VMEM((t, hd), F32),
            pltpu.VMEM((1, t), F32), pltpu.VMEM((t, 1), F32), pltpu.VMEM((1, t), F32),
        ] + host.scratch,
        compiler_params=_params(("arbitrary",)),
    )(proj, proj, proj, o, do, gq.reshape(1, hd), gk.reshape(1, hd), c_row3, c_colb, lse, dproj, *host.ins)
    return res


def _mem_fwd(proj, off, kv, gq, gk, tq):
    t = proj.shape[0]
    m, width = kv.shape[0], kv.shape[1] // 2
    hd = width // MEM_HEADS
    tq = _tile(t, tq)
    blk0 = off // hd
    scale = 1.0 / math.sqrt(hd)

    def body(q_ref, k_ref, v_ref, gq_ref, gk_ref, o_ref):
        qn, _ = _head_rms(q_ref[...].astype(F32))
        kn, _ = _head_rms(k_ref[...])
        s = _dot((qn * gq_ref[...]).astype(BF16), (kn * gk_ref[...]).astype(BF16), NT) * scale
        p = jnp.exp(s - jnp.max(s, axis=-1, keepdims=True))
        p = p / jnp.sum(p, axis=-1, keepdims=True)
        o_ref[...] = _dot(p.astype(BF16), v_ref[...].astype(BF16), NN).astype(o_ref.dtype)

    vec = pl.BlockSpec((1, hd), lambda h, i: (0, 0))
    return pl.pallas_call(
        body,
        name="mem_fwd",
        grid=(MEM_HEADS, t // tq),
        in_specs=[
            pl.BlockSpec((tq, hd), lambda h, i: (i, blk0 + h)),
            pl.BlockSpec((m, hd), lambda h, i: (0, h)),
            pl.BlockSpec((m, hd), lambda h, i: (0, MEM_HEADS + h)),
            vec, vec,
        ],
        out_specs=pl.BlockSpec((tq, hd), lambda h, i: (i, h)),
        out_shape=jax.ShapeDtypeStruct((t, width), BF16),
        compiler_params=_params(("parallel", "parallel")),
    )(proj, kv, kv, gq.reshape(1, hd), gk.reshape(1, hd))


def _mem_bwd(proj, off, kv, do, gq, gk, tq, dproj, rider=None):
    t = proj.shape[0]
    m, width = kv.shape[0], kv.shape[1] // 2
    hd = width // MEM_HEADS
    tq = _tile(t, tq)
    nq = t // tq
    blk0 = off // hd
    scale = 1.0 / math.sqrt(hd)
    host = _Host(rider)

    def body(*refs):
        q_ref, k_ref, v_ref, do_ref, gq_ref, gk_ref = refs[:6]
        pos = 7
        r_ins = refs[pos:pos + host.n_in]; pos += host.n_in
        dq_ref, dk_ref, dv_ref, ggq_ref, ggk_ref = refs[pos:pos + 5]; pos += 5
        r_outs = refs[pos:pos + host.n_out]; pos += host.n_out
        dkh_ref, dvh_ref = refs[pos:pos + 2]; pos += 2
        sems = refs[pos:]
        h, i = pl.program_id(0), pl.program_id(1)

        def compute():
            qn, rq = _head_rms(q_ref[...].astype(F32))
            kn, rk = _head_rms(k_ref[...])
            qhat = (qn * gq_ref[...]).astype(BF16)
            khat = (kn * gk_ref[...]).astype(BF16)
            vb = v_ref[...].astype(BF16)
            dob = do_ref[...]
            s = _dot(qhat, khat, NT) * scale
            p = jnp.exp(s - jnp.max(s, axis=-1, keepdims=True))
            p = p / jnp.sum(p, axis=-1, keepdims=True)
            dp = _dot(dob, vb, NT)
            ds = p * (dp - jnp.sum(dp * p, axis=-1, keepdims=True))
            dsb = ds.astype(BF16)
            dq, ggq = _head_rms_bwd(_dot(dsb, khat, NN) * scale, qn, rq, gq_ref[...])
            dq_ref[...] = dq.astype(dq_ref.dtype)

            @pl.when(i == 0)
            def _():
                dkh_ref[...] = jnp.zeros_like(dkh_ref)
                dvh_ref[...] = jnp.zeros_like(dvh_ref)

            @pl.when(jnp.logical_and(h == 0, i == 0))
            def _():
                ggq_ref[...] = jnp.zeros_like(ggq_ref)
                ggk_ref[...] = jnp.zeros_like(ggk_ref)

            dkh_ref[...] += _dot(dsb, qhat, TN)
            dvh_ref[...] += _dot(p.astype(BF16), dob, TN)
            ggq_ref[...] += ggq

            @pl.when(i == nq - 1)
            def _():
                dk, ggk = _head_rms_bwd(dkh_ref[...] * scale, kn, rk, gk_ref[...])
                dk_ref[...] = dk.astype(dk_ref.dtype)
                dv_ref[...] = dvh_ref[...].astype(dv_ref.dtype)
                ggk_ref[...] += ggk

        first = jnp.logical_and(h == 0, i == 0)
        last = jnp.logical_and(h == MEM_HEADS - 1, i == nq - 1)
        host.run(first, last, r_ins, r_outs, sems, compute)

    vec = pl.BlockSpec((1, hd), lambda h, i: (0, 0))
    kblk = pl.BlockSpec((m, hd), lambda h, i: (0, h))
    res = pl.pallas_call(
        body,
        name="mem_bwd",
        grid=(MEM_HEADS, nq),
        in_specs=[
            pl.BlockSpec((tq, hd), lambda h, i: (i, blk0 + h)), kblk,
            pl.BlockSpec((m, hd), lambda h, i: (0, MEM_HEADS + h)),
            pl.BlockSpec((tq, hd), lambda h, i: (i, h)), vec, vec, ANY,
        ] + host.in_specs,
        out_specs=[pl.BlockSpec((tq, hd), lambda h, i: (i, blk0 + h)), kblk, kblk, vec, vec] + host.out_specs,
        out_shape=[
            jax.ShapeDtypeStruct(dproj.shape, dproj.dtype),
            jax.ShapeDtypeStruct((m, width), BF16),
            jax.ShapeDtypeStruct((m, width), BF16),
            jax.ShapeDtypeStruct((1, hd), F32),
            jax.ShapeDtypeStruct((1, hd), F32),
        ] + host.out_shapes,
        input_output_aliases={6: 0},
        scratch_shapes=[pltpu.VMEM((m, hd), F32), pltpu.VMEM((m, hd), F32)] + host.scratch,
        compiler_params=_params(("arbitrary", "arbitrary")),
    )(proj, kv, kv, do, gq.reshape(1, hd), gk.reshape(1, hd), dproj, *host.ins)
    dproj, dk, dv, ggq, ggk = res[:5]
    return (dproj, jnp.concatenate([dk, dv], axis=1), ggq.reshape(hd), ggk.reshape(hd), *res[5:])


def _sigmoid(z):
    return 1.0 / (1.0 + jnp.exp(-z))


def _merge_fwd(proj, o3, tm, tc):
    t, d = o3[0].shape
    tm = _tile(t, tm)

    def body(g_ref, oa_ref, ob_ref, oc_ref, out_ref):
        acc = jnp.zeros((tm, tc), F32)
        for s, o_ref in enumerate((oa_ref, ob_ref, oc_ref)):
            acc = acc + _sigmoid(g_ref[:, s * tc:(s + 1) * tc].astype(F32)) * o_ref[...].astype(F32)
        out_ref[...] = acc.astype(out_ref.dtype)

    blk = pl.BlockSpec((tm, tc), lambda i, j: (i, j))
    return pl.pallas_call(
        body,
        name="merge_fwd",
        grid=(t // tm, d // tc),
        in_specs=[pl.BlockSpec((tm, 3 * tc), lambda i, j: (i, j)), blk, blk, blk],
        out_specs=blk,
        out_shape=jax.ShapeDtypeStruct((t, d), BF16),
        compiler_params=_params(("parallel", "parallel")),
    )(proj, *o3)


def _merge_bwd(proj, o3, dm, tm, tc):
    t, d = dm.shape
    tm = _tile(t, tm)

    def body(g_ref, oa_ref, ob_ref, oc_ref, dm_ref, dg_ref, da_ref, db_ref, dc_ref):
        dmf = dm_ref[...].astype(F32)
        for s, (o_ref, do_ref) in enumerate(((oa_ref, da_ref), (ob_ref, db_ref), (oc_ref, dc_ref))):
            g = _sigmoid(g_ref[:, s * tc:(s + 1) * tc].astype(F32))
            do_ref[...] = (dmf * g).astype(do_ref.dtype)
            dg_ref[:, s * tc:(s + 1) * tc] = (dmf * o_ref[...].astype(F32) * g * (1.0 - g)).astype(dg_ref.dtype)

    blk = pl.BlockSpec((tm, tc), lambda i, j: (i, j))
    wide = pl.BlockSpec((tm, 3 * tc), lambda i, j: (i, j))
    return pl.pallas_call(
        body,
        name="merge_bwd",
        grid=(t // tm, d // tc),
        in_specs=[wide, blk, blk, blk, blk],
        out_specs=[wide, blk, blk, blk],
        out_shape=[jax.ShapeDtypeStruct(proj.shape, BF16)] + [jax.ShapeDtypeStruct((t, d), BF16)] * 3,
        compiler_params=_params(("parallel", "parallel")),
    )(proj, *o3, dm)


def _w_in_chunks(d, tc):
    cw = d // 2
    heads = cw // FOX_HEAD_DIM
    conv0, fox0, f0, mq0, gate0 = 0, 3 * cw, 6 * cw, 6 * cw + heads, 7 * cw + heads
    chunks = [(gate0 + s * d + j * tc, gate0 + s * d + (j + 1) * tc) for j in range(d // tc) for s in range(N_BRANCHES)]
    chunks += [(conv0 + s * cw + j * LANES, conv0 + s * cw + (j + 1) * LANES) for j in range(cw // LANES) for s in range(3)]
    chunks += [(fox0 + s * cw + j * FOX_HEAD_DIM, fox0 + s * cw + (j + 1) * FOX_HEAD_DIM) for j in range(heads) for s in range(3)]
    chunks.append((mq0, mq0 + cw))
    return chunks, (f0, f0 + heads)


ROW_TILE = 16
GROUP = 128
GROUP_BACK = 112
SCRATCH_ROWS = 2 * GROUP + 32


def _padded_rows(r):
    return -(-r // GROUP_BACK) * GROUP_BACK


def _rows_from(scr_ref, y_ref, q8, fine, g):
    x = scr_ref[pl.ds(pl.multiple_of(q8 * 8, 8), g + 8), :]
    for s in range(8):
        @pl.when(fine == s)
        def _(s=s):
            y_ref[...] = (x if s == 0 else pltpu.roll(x, g + 8 - s, axis=0))[0:g]


def _assemble(name, tbl, grid, step, in_specs, out_spec, out_shape, operands, g, w1, cols_of):
    has_f = len(in_specs) == 3
    k = out_shape.shape[-1]
    c = cols_of

    def body(*refs):
        t_ref, s1_ref, s2_ref = refs[:3]
        f_ref = refs[3] if has_f else None
        out_ref = refs[3 + has_f]
        scr1, scr2, scrf, y_ref = refs[4 + has_f:]
        t = step()

        @pl.when(t == 0)
        def _():
            scr1[...] = jnp.zeros_like(scr1)
            scr2[...] = jnp.zeros_like(scr2)
            scrf[...] = jnp.zeros_like(scrf)

        rows = lax.broadcasted_iota(jnp.int32, (g, k), 0)
        n1, a2 = t_ref[c["n1"], t], t_ref[c["a2"], t]
        scr1[0:w1, :] = (s1_ref[0] if len(s1_ref.shape) == 3 else s1_ref[...]).astype(F32)
        _rows_from(scr1, y_ref, t_ref[c["q1"], t], t_ref[c["s1"], t], g)
        out_ref[...] = y_ref[...].astype(out_ref.dtype)

        @pl.when(a2 < g)
        def _():
            scr2[g:g + s2_ref.shape[0], :] = s2_ref[...].astype(F32)
            _rows_from(scr2, y_ref, t_ref[c["q2"], t], t_ref[c["s2"], t], g)
            out_ref[...] = jnp.where(rows < n1, out_ref[...].astype(F32), y_ref[...]).astype(out_ref.dtype)

        if has_f:
            fa, fb = t_ref[c["fa"], t], t_ref[c["fb"], t]

            @pl.when(fb > fa)
            def _():
                scrf[g:g + f_ref.shape[0], :] = f_ref[...].astype(F32)
                _rows_from(scrf, y_ref, t_ref[c["qf"], t], t_ref[c["sf"], t], g)
                inside = jnp.logical_and(rows >= fa, rows < fb)
                out_ref[...] = jnp.where(inside, y_ref[...], out_ref[...].astype(F32)).astype(out_ref.dtype)

            valid = t_ref[c["valid"], t]

            @pl.when(valid < g)
            def _():
                out_ref[...] = jnp.where(rows < valid, out_ref[...].astype(F32), 0.0).astype(out_ref.dtype)

    return pl.pallas_call(
        body,
        name=name,
        grid_spec=pltpu.PrefetchScalarGridSpec(
            num_scalar_prefetch=1, grid=grid, in_specs=in_specs, out_specs=out_spec,
            scratch_shapes=[pltpu.VMEM((SCRATCH_ROWS, k), F32)] * 3 + [pltpu.VMEM((g, k), F32)]),
        out_shape=out_shape,
        compiler_params=_params(("arbitrary",) * len(grid)),
    )(jnp.asarray(tbl), *operands)


def _pack_w_in(w8, d, tc):
    blocks, rp, k = w8.shape
    chunks, (f_lo, f_hi) = _w_in_chunks(d, tc)
    r = max(hi for _, hi in chunks) // blocks
    g, w1 = GROUP, GROUP + ROW_TILE
    table = []
    for lo, hi in chunks:
        for g0 in range(lo, hi, g):
            b1, r1 = divmod(g0, r)
            n1 = min(g, r - r1)
            st1 = min(r1 // ROW_TILE * ROW_TILE, rp - w1)
            o1, o2 = r1 - st1, g - n1
            b2 = b1 + 1 if n1 < g else 0
            table.append((b1, st1, o1 // 8, o1 % 8, n1, n1, b2, o2 // 8, o2 % 8))
    names = ("b1", "st1", "q1", "s1", "n1", "a2", "b2", "q2", "s2")
    cols_of = {n: i for i, n in enumerate(names)}
    tbl = np.array(table, np.int32).T
    c = cols_of
    w_all = _assemble(
        "pack_w_in", tbl, (len(table),), lambda: pl.program_id(0),
        [pl.BlockSpec((pl.Element(1), pl.Element(w1), pl.Element(k)), lambda i, t: (t[c["b1"], i], pl.multiple_of(t[c["st1"], i], ROW_TILE), 0)),
         pl.BlockSpec((None, g, k), lambda i, t: (t[c["b2"], i], 0, 0))],
        pl.BlockSpec((g, k), lambda i, t: (i, 0)),
        jax.ShapeDtypeStruct((len(table) * g, k), w8.dtype), [w8, w8], g, w1, cols_of)
    fb, fr = divmod(f_lo, r)
    return w_all, jnp.pad(w8[fb, fr:fr + f_hi - f_lo], ((0, F_ROWS - (f_hi - f_lo)), (0, 0)))


def _unpack_g_in(g_all, g_f, d, tc, blocks):
    n_all, k = g_all.shape
    chunks, (f_lo, f_hi) = _w_in_chunks(d, tc)
    r = max(hi for _, hi in chunks) // blocks
    rp = _padded_rows(r)
    g, w1 = GROUP_BACK, GROUP_BACK + ROW_TILE
    pos, spans = 0, [(f_lo, f_hi, None)]
    for lo, hi in chunks:
        spans.append((lo, hi, pos))
        pos += hi - lo
    spans.sort()
    table = []
    for b in range(blocks):
        for l0 in range(0, rp, g):
            valid = max(0, min(g, r - l0))
            g0, segs, fa, fb, of = b * r + l0, [], 0, 0, 0
            for lo, hi, p in spans:
                a, e = max(lo, g0), min(hi, g0 + valid)
                if a < e and p is None:
                    fa, fb, of = a - g0, e - g0, g + (a - lo) - (a - g0)
                elif a < e:
                    segs.append((a - g0, p + a - lo, e - a))
            assert len(segs) <= 2 and (not segs or segs[0][0] == 0 or len(segs) == 1)
            first = segs[0] if segs and segs[0][0] == 0 else (0, 0, 0)
            second = segs[-1] if segs and segs[-1][0] > 0 else (g, 0, 0)
            st1 = min(first[1] // ROW_TILE * ROW_TILE, n_all - w1)
            o1, o2 = first[1] - st1, g - second[0]
            assert second[1] % GROUP == 0
            table.append((st1, o1 // 8, o1 % 8, first[2], second[0], second[1] // GROUP, o2 // 8, o2 % 8,
                          fa, fb, of // 8, of % 8, valid))
    names = ("st1", "q1", "s1", "n1", "a2", "j2", "q2", "s2", "fa", "fb", "qf", "sf", "valid")
    cols_of = {n: i for i, n in enumerate(names)}
    tbl = np.array(table, np.int32).T
    c, per = cols_of, rp // g
    return _assemble(
        "unpack_g_in", tbl, (blocks, per), lambda: pl.program_id(0) * per + pl.program_id(1),
        [pl.BlockSpec((pl.Element(w1), pl.Element(k)), lambda b, u, t: (pl.multiple_of(t[c["st1"], b * per + u], ROW_TILE), 0)),
         pl.BlockSpec((GROUP, k), lambda b, u, t: (t[c["j2"], b * per + u], 0)),
         pl.BlockSpec((F_ROWS, k), lambda b, u, t: (0, 0))],
        pl.BlockSpec((None, g, k), lambda b, u, t: (b, u, 0)),
        jax.ShapeDtypeStruct((blocks, rp, k), g_all.dtype), [g_all, g_all, g_f], g, w1, cols_of)


def _unblock(w8):
    return w8.transpose(1, 0, 2).reshape(w8.shape[1], -1)


def _tile2(r, cols, tr, tcols):
    if r % 8 == 0:
        return _tile(r, tr), cols
    return r, _tile(cols, tcols)


def _pair_sum(name, g8, got, c):
    def body(c_ref, g_ref, s_ref, o_ref):
        o_ref[...] = (g_ref[...].astype(F32) + s_ref[...].astype(F32)).astype(o_ref.dtype)

    if g8.ndim == 4:
        _, r, k1, k2 = g8.shape
        tr = max(cand for cand in range(1, 385) if r % cand == 0)
        grid = (N_CHIPS, r // tr)
        shape = (None, tr, k1, k2)
        own = pl.BlockSpec(shape, lambda q, i, c_ref: (2 * q + c_ref[0], i, 0, 0))
        blk = pl.BlockSpec(shape, lambda q, i, c_ref: (q, i, 0, 0))
    else:
        _, r, cols = g8.shape
        tr, tcols = _tile2(r, cols, 256, 256)
        grid = (N_CHIPS, r // tr, cols // tcols)
        own = pl.BlockSpec((None, tr, tcols), lambda q, i, j, c_ref: (2 * q + c_ref[0], i, j))
        blk = pl.BlockSpec((None, tr, tcols), lambda q, i, j, c_ref: (q, i, j))
    return pl.pallas_call(
        body,
        name=name,
        grid_spec=pltpu.PrefetchScalarGridSpec(num_scalar_prefetch=1, grid=grid, in_specs=[own, blk], out_specs=blk),
        out_shape=jax.ShapeDtypeStruct((N_CHIPS,) + g8.shape[1:], BF16),
        compiler_params=_params(("parallel",) * len(grid)),
    )(c, g8, got)


def _local_step(x, mem, target, w, small, comm=None):
    t, d = x.shape
    cw = d // 2
    heads = cw // FOX_HEAD_DIM
    tc = min(512, d)
    tq = min(512, t)
    off_conv, off_fox, off_mq = 3 * d, 3 * d + 3 * cw, 3 * d + 6 * cw
    w = dict(w)
    w_all, w_f = _pack_w_in(w["w_in"], d, tc)
    big = dict(tm=1024, tn=512, tk=2048)
    wide_k = dict(tm=512, tn=1024, tk=4096)
    tall = dict(tm=2048, tn=512, tk=2048)

    h = _rms_fwd("rms1_fwd", x, small["norm1_g"])
    if comm:
        early = ("w_conv_out", "w_fox_out", "w_mem_out", "w_out", "w_mem_kv")
        proj, *got = _matmul("proj", "nt", h, w_all, outs=[BF16], rider=_gather_rider([comm["shards"][n] for n in early], False), **tall)
        for n, val in zip(early, got):
            w[n] = _unblock(val) if n in COLUMN_SPLIT else val.reshape(-1, val.shape[-1])
    else:
        proj = _matmul("proj", "nt", h, w_all, outs=[BF16], **tall)
    z_row = _matmul("proj_f", "nt", w_f, h, outs=[F32], tm=F_ROWS, tn=512, tk=2048)

    y_conv = _conv_fwd(proj, off_conv, small["conv_w"], LANES)

    b_col = jnp.pad(small["b_f"], (0, F_ROWS - heads)).reshape(F_ROWS, 1)
    c_row3 = _forget_fwd(z_row, b_col)[:heads].reshape(heads, 1, t)
    c_colb = _rows_to_colb(c_row3, tq)
    if comm:
        y_fox, lse, got = _fox_fwd(proj, off_fox, small["fox_q_g"], small["fox_k_g"], c_row3, c_colb, heads, tq,
                                   rider=_gather_rider([comm["shards"]["w_up"]], False))
        w["w_up"] = _unblock(got)
    else:
        y_fox, lse = _fox_fwd(proj, off_fox, small["fox_q_g"], small["fox_k_g"], c_row3, c_colb, heads, tq)

    nm = _rms_fwd("mem_rms_fwd", mem, small["mem_norm_g"])
    kv = _matmul("mem_kv", "nn", nm, w["w_mem_kv"], outs=[F32], tm=256, tn=512, tk=2048)
    y_mem = _mem_fwd(proj, off_mq, kv, small["mem_q_g"], small["mem_k_g"], tq)

    ys = (y_conv, y_fox, y_mem)
    w_outs = (w["w_conv_out"], w["w_fox_out"], w["w_mem_out"])
    o3 = [_matmul(f"branch_out{s}", "nn", ys[s], w_outs[s], outs=[BF16], **big) for s in range(3)]
    merged = _merge_fwd(proj, o3, 512, tc)
    x1 = _matmul("out_proj", "nn", merged, w["w_out"], outs=[F32], extras=[x],
                 epilogue=lambda acc, xr: (acc + xr,), **big)
    h2 = _rms_fwd("rms2_fwd", x1, small["norm2_g"])

    def up_epilogue(acc):
        return acc, jnp.square(jnp.maximum(acc, 0.0))

    if comm:
        up, act, got = _matmul("mlp_up", "nn", h2, w["w_up"], outs=[BF16, BF16], epilogue=up_epilogue,
                               rider=_gather_rider([comm["shards"]["w_down"]], False), **big)
        w["w_down"] = got.reshape(-1, got.shape[-1])
    else:
        up, act = _matmul("mlp_up", "nn", h2, w["w_up"], outs=[BF16, BF16], epilogue=up_epilogue, **big)

    def loss_epilogue(acc, x1r, tr):
        dy = (acc + x1r - tr) * (1.0 / d)
        return dy, dy

    dy, dyb = _matmul("mlp_down", "nn", act, w["w_down"], outs=[F32, BF16], extras=[x1, target],
                      epilogue=loss_epilogue, tm=1024, tn=512, tk=4096)

    def dup_epilogue(acc, upr):
        return (acc * 2.0 * jnp.maximum(upr.astype(F32), 0.0),)

    def by_owner(g):
        return g.reshape(N_DEV, -1, g.shape[-1])

    g, parts = {}, {}
    g["w_down"] = _matmul("d_w_down", "tn", act, dyb, outs=[BF16], **wide_k)
    if comm:
        dup, got = _matmul("d_act", "nt", dyb, w["w_down"], outs=[BF16], extras=[up], epilogue=dup_epilogue,
                           rider=_pair_rider([by_owner(g["w_down"])]), **tall)
        pair = _pair_sum("pair_w_down", by_owner(g["w_down"]), got, comm["c"])
        g["w_up"], parts["w_down"] = _matmul("d_w_up", "tn", h2, dup, outs=[BF16], out_blocks=True,
                                             rider=_chip_rider([pair]), **wide_k)
        dh2, got = _matmul("d_h2", "nt", dup, w["w_up"], outs=[F32], rider=_pair_rider([g["w_up"]]), **tall)
        pair_up = _pair_sum("pair_w_up", g["w_up"], got, comm["c"])
    else:
        dup = _matmul("d_act", "nt", dyb, w["w_down"], outs=[BF16], extras=[up], epilogue=dup_epilogue, **tall)
        g["w_up"] = _matmul("d_w_up", "tn", h2, dup, outs=[BF16], out_blocks=True, **wide_k)
        dh2 = _matmul("d_h2", "nt", dup, w["w_up"], outs=[F32], **tall)
    dx1, dx1b, g_norm2, dy_sq = _rms_bwd("rms2_bwd", dh2, x1, small["norm2_g"], res=dy)
    loss = dy_sq * (0.5 * d)

    g["w_out"] = _matmul("d_w_out", "tn", merged, dx1b, outs=[BF16], **wide_k)
    dmerged = _matmul("d_merged", "nt", dx1b, w["w_out"], outs=[BF16], **tall)
    dproj, *do3 = _merge_bwd(proj, o3, dmerged, 512, tc)
    names = ("w_conv_out", "w_fox_out", "w_mem_out")
    dys = []
    for s in range(3):
        g[names[s]] = _matmul(f"d_w_branch{s}", "tn", ys[s], do3[s], outs=[BF16], out_blocks=True, **wide_k)
        dys.append(_matmul(f"d_branch{s}", "nt", do3[s], w_outs[s], outs=[BF16], **tall))

    dproj, dkv, g_mq, g_mk = _mem_bwd(proj, off_mq, kv, dys[2], small["mem_q_g"], small["mem_k_g"], tq, dproj)
    g["w_mem_kv"] = _matmul("d_w_mem_kv", "tn", nm, dkv, outs=[BF16], **wide_k)
    dnm = _matmul("d_mem_norm", "nt", dkv, w["w_mem_kv"], outs=[F32], tm=256, tn=512, tk=2048)
    _, _, g_mem_norm, _ = _rms_bwd("mem_rms_bwd", dnm, mem, small["mem_norm_g"])

    mid = ("w_out", "w_conv_out", "w_fox_out", "w_mem_out", "w_mem_kv")
    if comm:
        mid8 = [g[n] if n in names else by_owner(g[n]) for n in mid]
        dproj, g_conv_w, *got = _conv_bwd(proj, off_conv, small["conv_w"], dys[0], LANES, dproj, rider=_pair_rider(mid8))
        pairs = [pair_up] + [_pair_sum("pair_" + n, g8, s4, comm["c"]) for n, g8, s4 in zip(mid, mid8, got)]
        dproj, dc, g_fq, g_fk, *got = _fox_bwd(proj, off_fox, y_fox, dys[1], small["fox_q_g"], small["fox_k_g"], c_row3, c_colb,
                                               lse, heads, tq, dproj, rider=_chip_rider(pairs))
        parts.update(zip(("w_up",) + mid, got))
    else:
        dproj, g_conv_w = _conv_bwd(proj, off_conv, small["conv_w"], dys[0], LANES, dproj)
        dproj, dc, g_fq, g_fk = _fox_bwd(proj, off_fox, y_fox, dys[1], small["fox_q_g"], small["fox_k_g"], c_row3, c_colb,
                                         lse, heads, tq, dproj)
    dc_row = jnp.pad(dc.reshape(heads, t), ((0, F_ROWS - heads), (0, 0)))
    dz_row, db = _forget_bwd(z_row, b_col, dc_row)

    g_all = _matmul("d_w_in", "tn", dproj, h, outs=[BF16], j_outer=True, **wide_k)
    g_wf = _matmul("d_w_f", "nn", dz_row, h, outs=[BF16], tm=F_ROWS, tn=512, tk=4096)
    g["w_in"] = _unpack_g_in(g_all, g_wf, d, tc, w["w_in"].shape[0])
    dh = _matmul("d_h_f", "tn", dz_row, w_f, outs=[F32], tm=1024, tn=512, tk=F_ROWS)
    add_prev = lambda acc, prev: (acc + prev,)
    if comm:
        g_in8 = g["w_in"]
        got = _run_rider("pair_exchange_w_in", _pair_rider([g_in8]))[0]
        pair = _pair_sum("pair_w_in", g_in8, got, comm["c"])
        dh, parts["w_in"] = _matmul("d_h", "nn", dproj, w_all, outs=[F32], extras=[dh], epilogue=add_prev,
                                    rider=_chip_rider([pair]), tm=1024, tn=512, tk=3328)
    else:
        dh = _matmul("d_h", "nn", dproj, w_all, outs=[F32], extras=[dh], epilogue=add_prev, tm=1024, tn=512, tk=3328)
    grad_x, _, g_norm1, _ = _rms_bwd("rms1_bwd", dh, x, small["norm1_g"], res=dx1)

    gs = dict(norm1_g=g_norm1, b_f=db[:heads, 0], conv_w=g_conv_w, fox_q_g=g_fq.reshape(-1), fox_k_g=g_fk.reshape(-1),
              mem_norm_g=g_mem_norm, mem_q_g=g_mq, mem_k_g=g_mk, norm2_g=g_norm2)
    return loss, grad_x, (parts if comm else g), gs


def _adamw_math(w, g, m, v):
    m = ADAM_B1 * m + (1.0 - ADAM_B1) * g
    v = ADAM_B2 * v + (1.0 - ADAM_B2) * jnp.square(g)
    m_hat = m / (1.0 - ADAM_B1 ** ADAM_STEP)
    v_hat = v / (1.0 - ADAM_B2 ** ADAM_STEP)
    delta = -ADAM_LR * (m_hat / (jnp.sqrt(v_hat) + ADAM_EPS) + ADAM_WD * w)
    return delta, m, v


def _adamw(name, parts, w, m, v):
    r, c = w.shape
    n_parts, rp = parts.shape[:2]
    if rp == r:
        tr, tc = _tile2(r, c, 128, 256)
    else:
        tr, tc = _tile(rp, 256), _tile(c, 1024)

    def body(p_ref, w_ref, m_ref, v_ref, g_ref, d_ref, nm_ref, nv_ref):
        g = p_ref[0].astype(F32)
        for s in range(1, n_parts):
            g = g + p_ref[s].astype(F32)
        delta, nm, nv = _adamw_math(w_ref[...], g, m_ref[...], v_ref[...])
        g_ref[...] = g
        d_ref[...] = delta
        nm_ref[...] = nm
        nv_ref[...] = nv

    blk = pl.BlockSpec((tr, tc), lambda i, j: (i, j))
    return pl.pallas_call(
        body,
        name=name,
        grid=(rp // tr, c // tc),
        in_specs=[pl.BlockSpec((n_parts, tr, tc), lambda i, j: (0, i, j)), blk, blk, blk],
        out_specs=[blk] * 4,
        out_shape=[jax.ShapeDtypeStruct((r, c), F32)] * 4,
        compiler_params=_params(("parallel", "parallel")),
    )(parts, w, m, v)


def _sum_parts(name, parts):
    n_parts, r, c = parts.shape

    def body(p_ref, o_ref):
        acc = p_ref[0]
        for s in range(1, n_parts):
            acc = acc + p_ref[s]
        o_ref[...] = acc

    return pl.pallas_call(body, name=name, out_shape=jax.ShapeDtypeStruct((r, c), F32))(parts)


BIG = ("w_in", "w_mem_kv", "w_conv_out", "w_fox_out", "w_mem_out", "w_out", "w_up", "w_down")
COLUMN_SPLIT = ("w_in", "w_conv_out", "w_fox_out", "w_mem_out", "w_up")
SMALL = ("norm1_g", "b_f", "conv_w", "fox_q_g", "fox_k_g", "mem_norm_g", "mem_q_g", "mem_k_g", "norm2_g")
WEIGHTS = ("norm1_g", "w_in", "b_f", "conv_w", "fox_q_g", "fox_k_g", "mem_norm_g", "w_mem_kv", "mem_q_g", "mem_k_g",
           "w_conv_out", "w_fox_out", "w_mem_out", "w_out", "norm2_g", "w_up", "w_down")


def _pack(vectors):
    rows = []
    for vec in vectors:
        n = vec.shape[0]
        rows.append(jnp.pad(vec, (0, -n % LANES)).reshape(-1, LANES))
    out = jnp.concatenate(rows, axis=0)
    return jnp.pad(out, ((0, -out.shape[0] % 8), (0, 0)))


def _unpack(packed, sizes):
    out, row = [], 0
    for n in sizes:
        nr = -(-n // LANES)
        out.append(packed[row:row + nr].reshape(-1)[:n])
        row += nr
    return out


def kernel(x, mem, norm1_g, w_in, b_f, conv_w, fox_q_g, fox_k_g, mem_norm_g, w_mem_kv, mem_q_g, mem_k_g, w_conv_out, w_fox_out, w_mem_out, w_out, norm2_g, w_up, w_down, loss_target, m_norm1_g, m_w_in, m_b_f, m_conv_w, m_fox_q_g, m_fox_k_g, m_mem_norm_g, m_w_mem_kv, m_mem_q_g, m_mem_k_g, m_w_conv_out, m_w_fox_out, m_w_mem_out, m_w_out, m_norm2_g, m_w_up, m_w_down, v_norm1_g, v_w_in, v_b_f, v_conv_w, v_fox_q_g, v_fox_k_g, v_mem_norm_g, v_w_mem_kv, v_mem_q_g, v_mem_k_g, v_w_conv_out, v_w_fox_out, v_w_mem_out, v_w_out, v_norm2_g, v_w_up, v_w_down):
    args = dict(locals())
    wts = {n: args[n] for n in WEIGHTS}
    ms = {n: args["m_" + n] for n in WEIGHTS}
    vs = {n: args["v_" + n] for n in WEIGHTS}
    x_pos, y_pos, c_pos = _position()
    me = _index(x_pos, y_pos, c_pos)

    shards = {n: wts[n].astype(BF16) for n in BIG if n != "w_in"}
    rows_in = w_in.shape[1]
    shards["w_in"] = jnp.pad(w_in.T.astype(BF16), ((0, _padded_rows(rows_in) - rows_in), (0, 0)))
    wi, cw8 = _run_rider("all_gather_first", _gather_rider([shards["w_in"], conv_w], True))
    full = {"w_in": wi}
    small = {n: wts[n] for n in SMALL}
    small["conv_w"] = _unblock(cw8)
    comm = {"shards": shards, "c": c_pos.astype(jnp.int32).reshape(1)}

    loss, grad_x, parts, gs = _local_step(x[0], mem[0], loss_target[0], full, small, comm)

    out_g, out_d, out_m, out_v = {}, {}, {}, {}
    for n in BIG:
        if n == "w_in":
            res = _adamw("adamw_" + n, parts[n], wts[n].T, ms[n].T, vs[n].T)
            out_g[n], out_d[n], out_m[n], out_v[n] = (r.T for r in res)
        else:
            out_g[n], out_d[n], out_m[n], out_v[n] = _adamw("adamw_" + n, parts[n], wts[n], ms[n], vs[n])

    small_sizes = [int(math.prod(gs[n].shape)) for n in SMALL]
    packed = _pack([gs[n].reshape(-1) for n in SMALL])
    gsum = _sum_parts("sum_small", _run_rider("exchange_small", _broadcast_rider([packed]))[0])
    gsmall = dict(zip(SMALL, _unpack(gsum, small_sizes)))
    cols = conv_w.shape[1]
    gsmall["conv_w"] = lax.dynamic_slice(gsmall["conv_w"].reshape(CONV_TAPS, -1), (0, me * cols), (CONV_TAPS, cols)).reshape(-1)
    pg, pw, pm, pv = (_pack([src[n].reshape(-1) for n in SMALL]) for src in (gsmall, wts, ms, vs))
    _, sd, sm, sv = _adamw("adamw_small", pg[None], pw, pm, pv)
    local_sizes = [int(math.prod(wts[n].shape)) for n in SMALL]
    for dst, src in ((out_d, sd), (out_m, sm), (out_v, sv)):
        for n, val in zip(SMALL, _unpack(src, local_sizes)):
            dst[n] = val.reshape(wts[n].shape)
    for n in SMALL:
        out_g[n] = gsmall[n].reshape(wts[n].shape)

    loss = lax.psum(loss, MESH_AXES)
    return (loss, grad_x[None], *[out_g[n] for n in WEIGHTS], *[out_d[n] for n in WEIGHTS],
            *[out_m[n] for n in WEIGHTS], *[out_v[n] for n in WEIGHTS])
```

```python
import math

import numpy as np
import jax
import jax.numpy as jnp
from jax import lax
from jax.experimental import pallas as pl
from jax.experimental.pallas import tpu as pltpu

F32 = jnp.float32
BF16 = jnp.bfloat16

EPS = 1e-6
N_DEV = 8
N_CHIPS = 4
FOX_HEAD_DIM = 128
MEM_HEADS = 4
CONV_TAPS = 3
N_BRANCHES = 3
F_ROWS = 16

ADAM_LR = 0.001
ADAM_B1 = 0.9
ADAM_B2 = 0.999
ADAM_EPS = 1e-08
ADAM_WD = 0.01
ADAM_STEP = 10

V7X_VMEM_BYTES = 64 * 1024 * 1024
VMEM_LIMIT = V7X_VMEM_BYTES * 3 // 4
LANES = 128
NEG = -1e30

MESH_AXES = ("x", "y", "c")
MESH = pl.DeviceIdType.MESH
ANY = pl.BlockSpec(memory_space=pl.ANY)

NN = (((1,), (0,)), ((), ()))
NT = (((1,), (1,)), ((), ()))
TN = (((0,), (0,)), ((), ()))


def _params(sem):
    return pltpu.CompilerParams(dimension_semantics=sem, vmem_limit_bytes=VMEM_LIMIT)


def _dot(a, b, dn):
    return lax.dot_general(a, b, dn, preferred_element_type=F32)


def _tile(n, t):
    if n <= t:
        return n
    for step in (LANES, 16):
        for cand in range(t - t % step, 0, -step):
            if n % cand == 0:
                return cand
    raise ValueError((n, t))


class _Rider:
    def __init__(self, ins, out_shapes, sem_shapes, start, finish, middle=None):
        self.ins, self.out_shapes, self.sem_shapes = list(ins), list(out_shapes), list(sem_shapes)
        self.start, self.finish, self.middle = start, finish, middle


def _position():
    return lax.axis_index("x"), lax.axis_index("y"), lax.axis_index("c")


def _index(px, py, pc):
    return 4 * px + 2 * py + pc


def _dma_sems(n, per):
    return [pltpu.SemaphoreType.DMA((n, per)), pltpu.SemaphoreType.DMA((n, per)), pltpu.SemaphoreType.DMA((n,))]


def _gather_rider(shards, pass_on):
    n = len(shards)

    def copies(ins, outs, sems):
        send_sems, recv_sems, local_sems = sems
        x, y, c = _position()
        me, sibling = (x, y, c), (x, y, 1 - c)
        chips = [(1 - x, y), (x, 1 - y), (1 - x, 1 - y)]

        def copy(a, k, block, to, src=None, k_send=None):
            rows = outs[a].at[_index(*block)]
            return pltpu.make_async_remote_copy(
                src_ref=rows if src is None else src, dst_ref=rows,
                send_sem=send_sems.at[a, k if k_send is None else k_send], recv_sem=recv_sems.at[a, k],
                device_id=to, device_id_type=MESH)

        mine = [pltpu.make_async_copy(ins[a], outs[a].at[_index(*me)], local_sems.at[a]) for a in range(n)]
        first = []
        for a in range(n):
            first.append(copy(a, 0, me, sibling, src=ins[a]))
            first += [copy(a, 1 + j, me, (*chips[j], c), src=ins[a]) for j in range(2 if pass_on else 3)]
        return copy, mine, first, me, sibling, chips, c

    def start(ins, outs, sems):
        _, mine, first, *_ = copies(ins, outs, sems)
        for cp in mine + first:
            cp.start()

    def by_kind(c, fn):
        if pass_on:
            pl.when(c == 1)(lambda: fn(0, 1))
            pl.when(c == 0)(lambda: fn(1, 0))
        else:
            fn(0, 1)

    def onward(copy, a, j_on, j_to, chips, c, sibling):
        third = [copy(a, 3, (*chips[j_on], c), (*chips[j_to], c), k_send=7)] if pass_on else []
        return third + [copy(a, 4 + j_on, (*chips[j_on], c), sibling)]

    def middle(ins, outs, sems):
        copy, _, _, me, sibling, chips, c = copies(ins, outs, sems)

        def fn(j_on, j_to):
            for a in range(n):
                copy(a, 1 + j_on, (*chips[j_on], c), me).wait_recv()
                for cp in onward(copy, a, j_on, j_to, chips, c, sibling):
                    cp.start()

        by_kind(c, fn)

    def finish(ins, outs, sems):
        copy, mine, first, me, sibling, chips, c = copies(ins, outs, sems)

        def fn(j_on, j_to):
            passed = [cp for a in range(n) for cp in onward(copy, a, j_on, j_to, chips, c, sibling)]
            for a in range(n):
                for j in (j_to, 2):
                    copy(a, 1 + j, (*chips[j], c), me).wait_recv()
                    passed.append(copy(a, 4 + j, (*chips[j], c), sibling))
                    passed[-1].start()
            for a in range(n):
                copy(a, 0, sibling, me).wait_recv()
                for j, chip in enumerate(chips):
                    copy(a, 4 + j, (*chip, 1 - c), me).wait_recv()
            for cp in first + passed:
                cp.wait_send()
            for cp in mine:
                cp.wait()

        by_kind(c, fn)

    out_shapes = [jax.ShapeDtypeStruct((N_DEV,) + s.shape, s.dtype) for s in shards]
    return _Rider(shards, out_shapes, _dma_sems(n, 8), start, finish, middle)


def _pair_rider(grads):
    n = len(grads)

    def copies(ins, outs, sems):
        send_sems, recv_sems, _ = sems
        x, y, c = _position()
        return [pltpu.make_async_remote_copy(
            src_ref=ins[a].at[2 * q + (1 - c)], dst_ref=outs[a].at[q],
            send_sem=send_sems.at[a, q], recv_sem=recv_sems.at[a, q], device_id=(x, y, 1 - c), device_id_type=MESH)
            for a in range(n) for q in range(N_CHIPS)]

    def start(ins, outs, sems):
        for cp in copies(ins, outs, sems):
            cp.start()

    def finish(ins, outs, sems):
        cps = copies(ins, outs, sems)
        for cp in cps:
            cp.wait_recv()
        for cp in cps:
            cp.wait_send()

    out_shapes = [jax.ShapeDtypeStruct((N_CHIPS,) + g.shape[1:], g.dtype) for g in grads]
    return _Rider(grads, out_shapes, _dma_sems(n, N_CHIPS), start, finish)


def _chip_rider(parts):
    n = len(parts)

    def copies(ins, outs, sems):
        send_sems, recv_sems, local_sems = sems
        x, y, c = _position()
        q_me = 2 * x + y
        chips = [(1 - x, y), (x, 1 - y), (1 - x, 1 - y)]
        mine = [pltpu.make_async_copy(ins[a].at[q_me], outs[a].at[q_me], local_sems.at[a]) for a in range(n)]
        sends, arrivals = [], []
        for a in range(n):
            for j, (tx, ty) in enumerate(chips):
                q_t = 2 * tx + ty
                sends.append(pltpu.make_async_remote_copy(
                    src_ref=ins[a].at[q_t], dst_ref=outs[a].at[q_me],
                    send_sem=send_sems.at[a, j], recv_sem=recv_sems.at[a, j], device_id=(tx, ty, c), device_id_type=MESH))
                arrivals.append(pltpu.make_async_remote_copy(
                    src_ref=ins[a].at[q_t], dst_ref=outs[a].at[q_t],
                    send_sem=send_sems.at[a, j], recv_sem=recv_sems.at[a, j], device_id=(tx, ty, c), device_id_type=MESH))
        return mine, sends, arrivals

    def start(ins, outs, sems):
        mine, sends, _ = copies(ins, outs, sems)
        for cp in mine + sends:
            cp.start()

    def finish(ins, outs, sems):
        mine, sends, arrivals = copies(ins, outs, sems)
        for cp in arrivals:
            cp.wait_recv()
        for cp in sends:
            cp.wait_send()
        for cp in mine:
            cp.wait()

    out_shapes = [jax.ShapeDtypeStruct(p.shape, p.dtype) for p in parts]
    return _Rider(parts, out_shapes, _dma_sems(n, 3), start, finish)


def _broadcast_rider(values):
    n = len(values)

    def copies(ins, outs, sems):
        send_sems, recv_sems, local_sems = sems
        x, y, c = _position()
        me = _index(x, y, c)

        def peer(k):
            return (1 - x if k & 4 else x, 1 - y if k & 2 else y, 1 - c if k & 1 else c)

        mine = [pltpu.make_async_copy(ins[a], outs[a].at[me], local_sems.at[a]) for a in range(n)]
        sends, arrivals = [], []
        for a in range(n):
            for k in range(1, N_DEV):
                common = dict(send_sem=send_sems.at[a, k - 1], recv_sem=recv_sems.at[a, k - 1], device_id=peer(k), device_id_type=MESH)
                sends.append(pltpu.make_async_remote_copy(src_ref=ins[a], dst_ref=outs[a].at[me], **common))
                arrivals.append(pltpu.make_async_remote_copy(src_ref=ins[a], dst_ref=outs[a].at[_index(*peer(k))], **common))
        return mine, sends, arrivals

    def start(ins, outs, sems):
        mine, sends, _ = copies(ins, outs, sems)
        for cp in mine + sends:
            cp.start()

    def finish(ins, outs, sems):
        mine, sends, arrivals = copies(ins, outs, sems)
        for cp in arrivals:
            cp.wait_recv()
        for cp in sends:
            cp.wait_send()
        for cp in mine:
            cp.wait()

    out_shapes = [jax.ShapeDtypeStruct((N_DEV,) + v.shape, v.dtype) for v in values]
    return _Rider(values, out_shapes, _dma_sems(n, 7), start, finish)


def _join_riders(*riders):
    def each(fn_name, ins, outs, sems):
        i = o = s = 0
        for r in riders:
            n_i, n_o, n_s = len(r.ins), len(r.out_shapes), len(r.sem_shapes)
            if getattr(r, fn_name) is not None:
                getattr(r, fn_name)(ins[i:i + n_i], outs[o:o + n_o], sems[s:s + n_s])
            i, o, s = i + n_i, o + n_o, s + n_s

    middle = (lambda ins, outs, sems: each("middle", ins, outs, sems)) if any(r.middle for r in riders) else None
    return _Rider([a for r in riders for a in r.ins], [a for r in riders for a in r.out_shapes],
                  [a for r in riders for a in r.sem_shapes],
                  lambda ins, outs, sems: each("start", ins, outs, sems),
                  lambda ins, outs, sems: each("finish", ins, outs, sems), middle)


def _run_rider(name, rider):
    n_in, n_out = len(rider.ins), len(rider.out_shapes)

    def body(*refs):
        ins, outs, sems = refs[:n_in], refs[n_in:n_in + n_out], refs[n_in + n_out:]
        rider.start(ins, outs, sems)
        if rider.middle is not None:
            rider.middle(ins, outs, sems)
        rider.finish(ins, outs, sems)

    return pl.pallas_call(
        body, name=name, in_specs=[ANY] * n_in, out_specs=[ANY] * n_out, out_shape=rider.out_shapes,
        scratch_shapes=rider.sem_shapes)(*rider.ins)


class _Host:
    def __init__(self, rider):
        self.rider = rider
        self.n_in = len(rider.ins) if rider else 0
        self.n_out = len(rider.out_shapes) if rider else 0
        self.n_sem = len(rider.sem_shapes) if rider else 0
        self.ins = rider.ins if rider else []
        self.in_specs = [ANY] * self.n_in
        self.out_specs = [ANY] * self.n_out
        self.out_shapes = rider.out_shapes if rider else []
        self.scratch = rider.sem_shapes if rider else []

    def run(self, first, last, ins, outs, sems, compute, midway=None):
        if self.rider is None:
            compute()
            return

        @pl.when(first)
        def _():
            self.rider.start(ins, outs, sems)

        compute()
        if self.rider.middle is not None and midway is not None:
            pl.when(midway)(lambda: self.rider.middle(ins, outs, sems))

        @pl.when(last)
        def _():
            if self.rider.middle is not None and midway is None:
                self.rider.middle(ins, outs, sems)
            self.rider.finish(ins, outs, sems)


def _matmul(name, kind, a, b, *, tm, tn, tk, outs, epilogue=None, extras=(), out_blocks=False, rider=None, j_outer=False):
    if kind == "nn":
        (m, kdim), n = a.shape, b.shape[1]
    elif kind == "nt":
        (m, kdim), n = a.shape, b.shape[0]
    else:
        (kdim, m), n = a.shape, b.shape[1]
    if out_blocks:
        tn = min(tn, n // N_DEV)
    tm, tn, tk = _tile(m, tm), _tile(n, tn), _tile(kdim, tk)
    ni, nj, nk = m // tm, n // tn, kdim // tk

    def spec(shape, fn):
        return pl.BlockSpec(shape, (lambda g0, g1, k: fn(g1, g0, k)) if j_outer else fn)

    a_spec = spec((tk, tm), lambda i, j, k: (k, i)) if kind == "tn" else spec((tm, tk), lambda i, j, k: (i, k))
    b_spec = spec((tn, tk), lambda i, j, k: (j, k)) if kind == "nt" else spec((tk, tn), lambda i, j, k: (k, j))
    dn = {"nn": NN, "nt": NT, "tn": TN}[kind]

    tile_spec = spec((tm, tn), lambda i, j, k: (i, j))
    if out_blocks:
        width = n // N_DEV
        r_out = width // tn
        out_shape = [jax.ShapeDtypeStruct((N_DEV, m, width), dt) for dt in outs]
        out_specs = [spec((None, tm, tn), lambda i, j, k: (j // r_out, i, j % r_out)) for _ in outs]
    else:
        out_shape = [jax.ShapeDtypeStruct((m, n), dt) for dt in outs]
        out_specs = [tile_spec for _ in outs]
    n_ex, n_out = len(extras), len(outs)
    host = _Host(rider)
    n_acc = 1 if nk > 1 else 0

    def body(*refs):
        a_ref, b_ref = refs[0], refs[1]
        pos = 2
        ex_refs = refs[pos:pos + n_ex]; pos += n_ex
        r_ins = refs[pos:pos + host.n_in]; pos += host.n_in
        out_refs = refs[pos:pos + n_out]; pos += n_out
        r_outs = refs[pos:pos + host.n_out]; pos += host.n_out
        acc_ref = refs[pos] if n_acc else None
        sems = refs[pos + n_acc:]
        i, j, k = pl.program_id(1 if j_outer else 0), pl.program_id(0 if j_outer else 1), pl.program_id(2)

        def finish_tile(acc):
            vals = (acc,) if epilogue is None else epilogue(acc, *[e[...] for e in ex_refs])
            for o_ref, v in zip(out_refs, vals):
                o_ref[...] = v.astype(o_ref.dtype)

        def compute():
            part = _dot(a_ref[...], b_ref[...], dn)
            if nk == 1:
                finish_tile(part)
                return

            @pl.when(k == 0)
            def _():
                acc_ref[...] = part

            @pl.when(jnp.logical_and(k > 0, k < nk - 1))
            def _():
                acc_ref[...] += part

            @pl.when(k == nk - 1)
            def _():
                finish_tile(acc_ref[...] + part)

        first = jnp.logical_and(jnp.logical_and(i == 0, j == 0), k == 0)
        last = jnp.logical_and(jnp.logical_and(i == ni - 1, j == nj - 1), k == nk - 1)
        step = (pl.program_id(0) * (ni if j_outer else nj) + pl.program_id(1)) * nk + k
        host.run(first, last, r_ins, r_outs, sems, compute, midway=step == (ni * nj * nk * 3) // 5)

    sem = ("arbitrary",) * 3 if rider else ("parallel", "parallel", "arbitrary")
    res = pl.pallas_call(
        body,
        name=name,
        grid=(nj, ni, nk) if j_outer else (ni, nj, nk),
        in_specs=[a_spec, b_spec] + [tile_spec for _ in extras] + host.in_specs,
        out_specs=out_specs + host.out_specs,
        out_shape=out_shape + host.out_shapes,
        scratch_shapes=([pltpu.VMEM((tm, tn), F32)] if n_acc else []) + host.scratch,
        compiler_params=_params(sem),
    )(a, b, *extras, *host.ins)
    return res[0] if len(res) == 1 else res


def _rms_fwd(name, x, g, tm=512):
    t, d = x.shape
    tm = _tile(t, tm)

    def body(x_ref, g_ref, h_ref):
        xf = x_ref[...]
        r = lax.rsqrt(jnp.mean(xf * xf, axis=-1, keepdims=True) + EPS)
        h_ref[...] = (xf * r * g_ref[...]).astype(h_ref.dtype)

    return pl.pallas_call(
        body,
        name=name,
        grid=(t // tm,),
        in_specs=[pl.BlockSpec((tm, d), lambda i: (i, 0)), pl.BlockSpec((1, d), lambda i: (0, 0))],
        out_specs=pl.BlockSpec((tm, d), lambda i: (i, 0)),
        out_shape=jax.ShapeDtypeStruct((t, d), BF16),
        compiler_params=_params(("parallel",)),
    )(x, g.reshape(1, d))


def _rms_bwd(name, dh, x, g, res=None, tm=256):
    t, d = x.shape
    tm = _tile(t, tm)
    has_res = res is not None

    def body(*refs):
        if has_res:
            dh_ref, x_ref, g_ref, res_ref, dx_ref, dxb_ref, gg_ref, ss_ref = refs
        else:
            dh_ref, x_ref, g_ref, dx_ref, dxb_ref, gg_ref, ss_ref = refs
        i = pl.program_id(0)
        xf = x_ref[...]
        r = lax.rsqrt(jnp.mean(xf * xf, axis=-1, keepdims=True) + EPS)
        xh = xf * r
        dhf = dh_ref[...].astype(F32)
        dxh = dhf * g_ref[...]
        dx = r * (dxh - xh * jnp.mean(dxh * xh, axis=-1, keepdims=True))

        @pl.when(i == 0)
        def _():
            gg_ref[...] = jnp.zeros_like(gg_ref)
            ss_ref[...] = jnp.zeros_like(ss_ref)

        if has_res:
            resf = res_ref[...]
            dx = dx + resf
            ss_ref[...] += jnp.sum(jnp.sum(resf * resf, axis=0, keepdims=True), axis=1, keepdims=True)
        dx_ref[...] = dx
        dxb_ref[...] = dx.astype(BF16)
        gg_ref[...] += jnp.sum(dhf * xh, axis=0, keepdims=True)

    row = pl.BlockSpec((tm, d), lambda i: (i, 0))
    vec = pl.BlockSpec((1, d), lambda i: (0, 0))
    one = pl.BlockSpec((1, 1), lambda i: (0, 0))
    ins = [dh, x, g.reshape(1, d)] + ([res] if has_res else [])
    dx, dxb, gg, ss = pl.pallas_call(
        body,
        name=name,
        grid=(t // tm,),
        in_specs=[row, row, vec] + ([row] if has_res else []),
        out_specs=[row, row, vec, one],
        out_shape=[jax.ShapeDtypeStruct((t, d), F32), jax.ShapeDtypeStruct((t, d), BF16), jax.ShapeDtypeStruct((1, d), F32),
                   jax.ShapeDtypeStruct((1, 1), F32)],
        compiler_params=_params(("arbitrary",)),
    )(*ins)
    return dx, dxb, gg.reshape(d), ss[0, 0]


def _head_rms(xf):
    r = lax.rsqrt(jnp.mean(xf * xf, axis=-1, keepdims=True) + EPS)
    return xf * r, r


def _head_rms_bwd(dy, xn, r, g):
    dxh = dy * g
    dx = r * (dxh - xn * jnp.mean(dxh * xn, axis=-1, keepdims=True))
    return dx, jnp.sum(dy * xn, axis=0, keepdims=True)


def _col_to_row(col):
    n = col.shape[0]
    eye = lax.broadcasted_iota(jnp.int32, (n, n), 0) == lax.broadcasted_iota(jnp.int32, (n, n), 1)
    return jnp.sum(jnp.where(eye, col, 0.0), axis=0, keepdims=True)


def _row_to_col(row):
    n = row.shape[1]
    eye = lax.broadcasted_iota(jnp.int32, (n, n), 0) == lax.broadcasted_iota(jnp.int32, (n, n), 1)
    return jnp.sum(jnp.where(eye, row, 0.0), axis=1, keepdims=True)


def _dproj_args(dproj, n_in):
    if dproj is None:
        return [], [], {}
    return [dproj], [ANY], {n_in: 0}


def _shift_down(u, s, rows):
    return jnp.where(rows >= s, pltpu.roll(u, s, axis=0), 0.0)


def _shift_up(u, s, rows, t):
    return jnp.where(rows < t - s, pltpu.roll(u, t - s, axis=0), 0.0)


def _conv_fwd(proj, off, conv_w, cb):
    t = proj.shape[0]
    c = conv_w.shape[1]
    blk0 = off // (3 * cb)

    def body(p_ref, w_ref, y_ref):
        rows = lax.broadcasted_iota(jnp.int32, (t, cb), 0)
        bg = p_ref[:, 0:cb].astype(F32)
        u = p_ref[:, cb:2 * cb].astype(F32) * p_ref[:, 2 * cb:3 * cb].astype(F32)
        w = w_ref[...]
        conv = w[2:3] * u + w[1:2] * _shift_down(u, 1, rows) + w[0:1] * _shift_down(u, 2, rows)
        y_ref[...] = (bg * conv).astype(y_ref.dtype)

    return pl.pallas_call(
        body,
        name="conv_fwd",
        grid=(c // cb,),
        in_specs=[pl.BlockSpec((t, 3 * cb), lambda j: (0, blk0 + j)), pl.BlockSpec((CONV_TAPS, cb), lambda j: (0, j))],
        out_specs=pl.BlockSpec((t, cb), lambda j: (0, j)),
        out_shape=jax.ShapeDtypeStruct((t, c), BF16),
        compiler_params=_params(("parallel",)),
    )(proj, conv_w)


def _conv_bwd(proj, off, conv_w, dy, cb, dproj, rider=None):
    t = proj.shape[0]
    c = conv_w.shape[1]
    blk0 = off // (3 * cb)
    nj = c // cb
    host = _Host(rider)

    def body(*refs):
        p_ref, w_ref, dy_ref = refs[:3]
        r_ins = refs[4:4 + host.n_in]
        dp_ref, gw_ref = refs[4 + host.n_in:6 + host.n_in]
        r_outs = refs[6 + host.n_in:6 + host.n_in + host.n_out]
        sems = refs[6 + host.n_in + host.n_out:]
        j = pl.program_id(0)

        def compute():
            rows = lax.broadcasted_iota(jnp.int32, (t, cb), 0)
            bg = p_ref[:, 0:cb].astype(F32)
            cg = p_ref[:, cb:2 * cb].astype(F32)
            v = p_ref[:, 2 * cb:3 * cb].astype(F32)
            u = cg * v
            w = w_ref[...]
            u1 = _shift_down(u, 1, rows)
            u2 = _shift_down(u, 2, rows)
            conv = w[2:3] * u + w[1:2] * u1 + w[0:1] * u2
            dyf = dy_ref[...].astype(F32)
            dconv = dyf * bg
            du = w[2:3] * dconv + w[1:2] * _shift_up(dconv, 1, rows, t) + w[0:1] * _shift_up(dconv, 2, rows, t)
            dp_ref[:, 0:cb] = (dyf * conv).astype(dp_ref.dtype)
            dp_ref[:, cb:2 * cb] = (du * v).astype(dp_ref.dtype)
            dp_ref[:, 2 * cb:3 * cb] = (du * cg).astype(dp_ref.dtype)
            gw_ref[0:1, :] = jnp.sum(dconv * u2, axis=0, keepdims=True)
            gw_ref[1:2, :] = jnp.sum(dconv * u1, axis=0, keepdims=True)
            gw_ref[2:3, :] = jnp.sum(dconv * u, axis=0, keepdims=True)

        host.run(j == 0, j == nj - 1, r_ins, r_outs, sems, compute)

    res = pl.pallas_call(
        body,
        name="conv_bwd",
        grid=(nj,),
        in_specs=[
            pl.BlockSpec((t, 3 * cb), lambda j: (0, blk0 + j)),
            pl.BlockSpec((CONV_TAPS, cb), lambda j: (0, j)),
            pl.BlockSpec((t, cb), lambda j: (0, j)),
            ANY,
        ] + host.in_specs,
        out_specs=[pl.BlockSpec((t, 3 * cb), lambda j: (0, blk0 + j)), pl.BlockSpec((CONV_TAPS, cb), lambda j: (0, j))] + host.out_specs,
        out_shape=[jax.ShapeDtypeStruct(dproj.shape, dproj.dtype), jax.ShapeDtypeStruct((CONV_TAPS, c), F32)] + host.out_shapes,
        input_output_aliases={3: 0},
        scratch_shapes=host.scratch,
        compiler_params=_params(("arbitrary",)),
    )(proj, conv_w, dy, dproj, *host.ins)
    return res


def _lane_scan(x, reverse):
    lane = lax.broadcasted_iota(jnp.int32, x.shape, 1)
    s = 1
    while s < LANES:
        if reverse:
            x = x + jnp.where(lane < LANES - s, pltpu.roll(x, LANES - s, axis=1), 0.0)
        else:
            x = x + jnp.where(lane >= s, pltpu.roll(x, s, axis=1), 0.0)
        s *= 2
    return x


def _scan_rows(src_ref, dst_ref, t, reverse, fn=None):
    groups = list(range(t // LANES))
    if reverse:
        groups = groups[::-1]
    carry = None
    for gi in groups:
        sl = slice(gi * LANES, (gi + 1) * LANES)
        blk = src_ref[:, sl]
        if fn is not None:
            blk = fn(blk)
        blk = _lane_scan(blk, reverse)
        if carry is not None:
            blk = blk + carry
        dst_ref[:, sl] = blk
        carry = blk[:, 0:1] if reverse else blk[:, LANES - 1:LANES]


def _forget_fwd(z_row, b_col):
    rows, t = z_row.shape

    def body(z_ref, b_ref, c_ref):
        def logf(z):
            zz = z + b_ref[...]
            return jnp.minimum(zz, 0.0) - jnp.log(1.0 + jnp.exp(-jnp.abs(zz)))

        _scan_rows(z_ref, c_ref, t, False, logf)

    return pl.pallas_call(
        body,
        name="forget_fwd",
        out_shape=jax.ShapeDtypeStruct((rows, t), F32),
        compiler_params=pltpu.CompilerParams(vmem_limit_bytes=VMEM_LIMIT),
    )(z_row, b_col)


def _rows_to_colb(c_row3, tq):
    heads, _, t = c_row3.shape

    def body(r_ref, o_ref):
        o_ref[...] = jnp.broadcast_to(_row_to_col(r_ref[...]), (tq, LANES))

    return pl.pallas_call(
        body,
        name="rows_to_colb",
        grid=(heads, t // tq),
        in_specs=[pl.BlockSpec((None, 1, tq), lambda h, i: (h, 0, i))],
        out_specs=pl.BlockSpec((None, tq, LANES), lambda h, i: (h, i, 0)),
        out_shape=jax.ShapeDtypeStruct((heads, t, LANES), F32),
        compiler_params=_params(("parallel", "parallel")),
    )(c_row3)


def _forget_bwd(z_row, b_col, dc_row):
    rows, t = z_row.shape

    def body(z_ref, b_ref, dc_ref, dz_ref, db_ref, tmp_ref):
        _scan_rows(dc_ref, tmp_ref, t, True)
        zz = z_ref[...] + b_ref[...]
        dz = tmp_ref[...] * (1.0 / (1.0 + jnp.exp(zz)))
        dz_ref[...] = dz.astype(dz_ref.dtype)
        db_ref[...] = jnp.sum(dz, axis=1, keepdims=True)

    return pl.pallas_call(
        body,
        name="forget_bwd",
        out_shape=[jax.ShapeDtypeStruct((rows, t), BF16), jax.ShapeDtypeStruct((rows, 1), F32)],
        scratch_shapes=[pltpu.VMEM((rows, t), F32)],
        compiler_params=pltpu.CompilerParams(vmem_limit_bytes=VMEM_LIMIT),
    )(z_row, b_col, dc_row)


def _fox_fwd(proj, off, gq, gk, c_row3, c_colb, heads, tq, rider=None):
    t = proj.shape[0]
    hd = FOX_HEAD_DIM
    tq = _tile(t, tq)
    nq = t // tq
    blk0 = off // hd
    scale = 1.0 / math.sqrt(hd)
    host = _Host(rider)

    def body(*refs):
        q_ref, k_ref, v_ref, gq_ref, gk_ref, crow_ref, ccol_ref = refs[:7]
        r_ins = refs[7:7 + host.n_in]
        o_ref, lse_ref = refs[7 + host.n_in:9 + host.n_in]
        r_outs = refs[9 + host.n_in:9 + host.n_in + host.n_out]
        khat_ref, v_t_ref = refs[9 + host.n_in + host.n_out:11 + host.n_in + host.n_out]
        sems = refs[11 + host.n_in + host.n_out:]
        h, qi = pl.program_id(0), pl.program_id(1)

        def compute():
            eye = (lax.broadcasted_iota(jnp.int32, (hd, hd), 0) == lax.broadcasted_iota(jnp.int32, (hd, hd), 1)).astype(BF16)

            @pl.when(qi == 0)
            def _():
                kn, _ = _head_rms(k_ref[...].astype(F32))
                khat_ref[...] = (kn * gk_ref[...]).astype(BF16)
                v_t_ref[...] = _dot(eye, v_ref[...], NT).astype(BF16)

            qn, _ = _head_rms(q_ref[...].astype(F32))
            qhat = (qn * (gq_ref[...] * scale)).astype(BF16)
            crow = crow_ref[:, pl.ds(pl.multiple_of(qi * tq, tq), tq)]
            above = lax.broadcasted_iota(jnp.int32, (tq, tq), 1) >= lax.broadcasted_iota(jnp.int32, (tq, tq), 0)

            def tile(j, carry, diagonal):
                m, l, acc_t = carry
                ks = pl.multiple_of(j * tq, tq)
                s_t = _dot(khat_ref[pl.ds(ks, tq), :], qhat, NT) - ccol_ref[pl.ds(ks, tq), 0:1]
                if diagonal:
                    s_t = jnp.where(above, s_t, NEG)
                m_new = jnp.maximum(m, jnp.max(s_t, axis=0, keepdims=True) + crow)
                alpha = jnp.exp(m - m_new)
                p_t = jnp.exp(s_t + (crow - m_new))
                l = alpha * l + jnp.sum(p_t, axis=0, keepdims=True)
                acc_t = alpha * acc_t + _dot(v_t_ref[:, pl.ds(ks, tq)], p_t.astype(BF16), NN)
                return m_new, l, acc_t

            init = (jnp.full((1, tq), NEG, F32), jnp.zeros((1, tq), F32), jnp.zeros((hd, tq), F32))
            carry = lax.fori_loop(0, qi, lambda j, c: tile(j, c, False), init)
            m, l, acc_t = tile(qi, carry, True)
            o_ref[...] = _dot((acc_t / l).astype(BF16), eye, TN).astype(o_ref.dtype)
            lse_ref[...] = m + jnp.log(l)

        first = jnp.logical_and(h == 0, qi == 0)
        last = jnp.logical_and(h == heads - 1, qi == nq - 1)
        host.run(first, last, r_ins, r_outs, sems, compute)

    res = pl.pallas_call(
        body,
        name="fox_fwd",
        grid=(heads, nq),
        in_specs=[
            pl.BlockSpec((tq, hd), lambda h, i: (i, blk0 + 3 * h)),
            pl.BlockSpec((t, hd), lambda h, i: (0, blk0 + 3 * h + 1)),
            pl.BlockSpec((t, hd), lambda h, i: (0, blk0 + 3 * h + 2)),
            pl.BlockSpec((1, hd), lambda h, i: (0, 0)),
            pl.BlockSpec((1, hd), lambda h, i: (0, 0)),
            pl.BlockSpec((None, 1, t), lambda h, i: (h, 0, 0)),
            pl.BlockSpec((None, t, LANES), lambda h, i: (h, 0, 0)),
        ] + host.in_specs,
        out_specs=[pl.BlockSpec((tq, hd), lambda h, i: (i, h)), pl.BlockSpec((None, 1, tq), lambda h, i: (h, 0, i))] + host.out_specs,
        out_shape=[jax.ShapeDtypeStruct((t, heads * hd), BF16), jax.ShapeDtypeStruct((heads, 1, t), F32)] + host.out_shapes,
        scratch_shapes=[pltpu.VMEM((t, hd), BF16), pltpu.VMEM((hd, t), BF16)] + host.scratch,
        compiler_params=_params(("arbitrary", "arbitrary")),
    )(proj, proj, proj, gq.reshape(1, hd), gk.reshape(1, hd), c_row3, c_colb, *host.ins)
    return res


def _fox_bwd(proj, off, o, do, gq, gk, c_row3, c_colb, lse, heads, tq, dproj, rider=None):
    t = proj.shape[0]
    hd = FOX_HEAD_DIM
    tq = _tile(t, tq)
    nb = t // tq
    blk0 = off // hd
    scale = 1.0 / math.sqrt(hd)
    host = _Host(rider)
    n_fixed_in = 11

    def body(*refs):
        q_ref, k_ref, v_ref, o_ref, do_ref, gq_ref, gk_ref, crow_ref, ccol_ref, lse_ref = refs[:10]
        pos = n_fixed_in
        r_ins = refs[pos:pos + host.n_in]; pos += host.n_in
        dp_ref, dc_ref, ggq_ref, ggk_ref = refs[pos:pos + 4]; pos += 4
        r_outs = refs[pos:pos + host.n_out]; pos += host.n_out
        qhat_ref, khat_ref, khat_t_ref, dq_t_ref, dk_ref, dcq_ref, dck_ref, delta_ref = refs[pos:pos + 8]; pos += 8
        sems = refs[pos:]
        h = pl.program_id(0)

        def compute():
            qn, rq = _head_rms(q_ref[...].astype(F32))
            qhat_ref[...] = (qn * (gq_ref[...] * scale)).astype(BF16)
            kn, rk = _head_rms(k_ref[...].astype(F32))
            khat_ref[...] = (kn * gk_ref[...]).astype(BF16)
            eye = (lax.broadcasted_iota(jnp.int32, (hd, hd), 0) == lax.broadcasted_iota(jnp.int32, (hd, hd), 1)).astype(BF16)
            khat_t_ref[...] = _dot(eye, khat_ref[...], NT).astype(BF16)
            delta = jnp.sum(do_ref[...].astype(F32) * o_ref[...].astype(F32), axis=-1, keepdims=True)
            for b in range(nb):
                sl = slice(b * tq, (b + 1) * tq)
                delta_ref[:, sl] = _col_to_row(delta[sl, :])
            dq_t_ref[...] = jnp.zeros_like(dq_t_ref)
            dcq_ref[...] = jnp.zeros_like(dcq_ref)
            above = lax.broadcasted_iota(jnp.int32, (tq, tq), 1) >= lax.broadcasted_iota(jnp.int32, (tq, tq), 0)

            def kv_block(j, _):
                ks = pl.multiple_of(j * tq, tq)
                kh = khat_ref[pl.ds(ks, tq), :]
                kh_t = khat_t_ref[:, pl.ds(ks, tq)]
                vv = v_ref[pl.ds(ks, tq), :]
                ccol = ccol_ref[pl.ds(ks, tq), 0:1]

                def q_block(i, carry, diagonal):
                    dk, dv, dck = carry
                    qs = pl.multiple_of(i * tq, tq)
                    qh = qhat_ref[pl.ds(qs, tq), :]
                    dob = do_ref[pl.ds(qs, tq), :]
                    s_t = _dot(kh, qh, NT) + ((crow_ref[:, pl.ds(qs, tq)] - lse_ref[:, pl.ds(qs, tq)]) - ccol)
                    p_t = jnp.exp(s_t)
                    if diagonal:
                        p_t = jnp.where(above, p_t, 0.0)
                    ds_t = p_t * (_dot(vv, dob, NT) - delta_ref[:, pl.ds(qs, tq)])
                    dsb = ds_t.astype(BF16)
                    dv = dv + _dot(p_t.astype(BF16), dob, NN)
                    dk = dk + _dot(dsb, qh, NN)
                    dq_t_ref[:, pl.ds(qs, tq)] += _dot(kh_t, dsb, NN)
                    dcq_ref[:, pl.ds(qs, tq)] += jnp.sum(ds_t, axis=0, keepdims=True)
                    dck = dck + jnp.sum(ds_t, axis=-1, keepdims=True)
                    return dk, dv, dck

                zero = jnp.zeros((tq, hd), F32)
                carry = q_block(j, (zero, zero, jnp.zeros((tq, 1), F32)), True)
                dk, dv, dck = lax.fori_loop(j + 1, nb, lambda i, c: q_block(i, c, False), carry)
                dk_ref[pl.ds(ks, tq), :] = dk
                dp_ref[pl.ds(ks, tq), 2 * hd:3 * hd] = dv.astype(dp_ref.dtype)
                dck_ref[pl.ds(ks, tq), :] = dck
                return 0

            lax.fori_loop(0, nb, kv_block, 0)

            dq, ggq = _head_rms_bwd(dq_t_ref[...].T * scale, qn, rq, gq_ref[...])
            dk, ggk = _head_rms_bwd(dk_ref[...], kn, rk, gk_ref[...])
            dp_ref[:, 0:hd] = dq.astype(dp_ref.dtype)
            dp_ref[:, hd:2 * hd] = dk.astype(dp_ref.dtype)
            for b in range(nb):
                sl = slice(b * tq, (b + 1) * tq)
                dc_ref[:, sl] = dcq_ref[:, sl] - _col_to_row(dck_ref[sl, :])

            @pl.when(h == 0)
            def _():
                ggq_ref[...] = jnp.zeros_like(ggq_ref)
                ggk_ref[...] = jnp.zeros_like(ggk_ref)

            ggq_ref[...] += ggq
            ggk_ref[...] += ggk

        host.run(h == 0, h == heads - 1, r_ins, r_outs, sems, compute)

    head_in = lambda part: pl.BlockSpec((t, hd), lambda h: (0, blk0 + 3 * h + part))
    vec = pl.BlockSpec((1, hd), lambda h: (0, 0))
    colb = pl.BlockSpec((None, t, LANES), lambda h: (h, 0, 0))
    res = pl.pallas_call(
        body,
        name="fox_bwd",
        grid=(heads,),
        in_specs=[
            head_in(0), head_in(1), head_in(2),
            pl.BlockSpec((t, hd), lambda h: (0, h)),
            pl.BlockSpec((t, hd), lambda h: (0, h)),
            vec, vec,
            pl.BlockSpec((None, 1, t), lambda h: (h, 0, 0)),
            colb,
            pl.BlockSpec((None, 1, t), lambda h: (h, 0, 0)),
            ANY,
        ] + host.in_specs,
        out_specs=[
            pl.BlockSpec((t, 3 * hd), lambda h: (0, blk0 // 3 + h)),
            pl.BlockSpec((None, 1, t), lambda h: (h, 0, 0)),
            vec, vec,
        ] + host.out_specs,
        out_shape=[
            jax.ShapeDtypeStruct(dproj.shape, dproj.dtype),
            jax.ShapeDtypeStruct((heads, 1, t), F32),
            jax.ShapeDtypeStruct((1, hd), F32),
            jax.ShapeDtypeStruct((1, hd), F32),
        ] + host.out_shapes,
        input_output_aliases={10: 0},
        scratch_shapes=[
            pltpu.VMEM((t, hd), BF16), pltpu.VMEM((t, hd), BF16), pltpu.VMEM((hd, t), BF16),
            pltpu.VMEM((hd, t), F32), pltpu.VMEM((t, hd), F32),
            pltpu.VMEM((1, t), F32), pltpu.VMEM((t, 1), F32), pltpu.VMEM((1, t), F32),
        ] + host.scratch,
        compiler_params=_params(("arbitrary",)),
    )(proj, proj, proj, o, do, gq.reshape(1, hd), gk.reshape(1, hd), c_row3, c_colb, lse, dproj, *host.ins)
    return res


def _mem_fwd(proj, off, kv, gq, gk, tq):
    t = proj.shape[0]
    m, width = kv.shape[0], kv.shape[1] // 2
    hd = width // MEM_HEADS
    tq = _tile(t, tq)
    blk0 = off // hd
    scale = 1.0 / math.sqrt(hd)

    def body(q_ref, k_ref, v_ref, gq_ref, gk_ref, o_ref):
        qn, _ = _head_rms(q_ref[...].astype(F32))
        kn, _ = _head_rms(k_ref[...])
        s = _dot((qn * gq_ref[...]).astype(BF16), (kn * gk_ref[...]).astype(BF16), NT) * scale
        p = jnp.exp(s - jnp.max(s, axis=-1, keepdims=True))
        p = p / jnp.sum(p, axis=-1, keepdims=True)
        o_ref[...] = _dot(p.astype(BF16), v_ref[...].astype(BF16), NN).astype(o_ref.dtype)

    vec = pl.BlockSpec((1, hd), lambda h, i: (0, 0))
    return pl.pallas_call(
        body,
        name="mem_fwd",
        grid=(MEM_HEADS, t // tq),
        in_specs=[
            pl.BlockSpec((tq, hd), lambda h, i: (i, blk0 + h)),
            pl.BlockSpec((m, hd), lambda h, i: (0, h)),
            pl.BlockSpec((m, hd), lambda h, i: (0, MEM_HEADS + h)),
            vec, vec,
        ],
        out_specs=pl.BlockSpec((tq, hd), lambda h, i: (i, h)),
        out_shape=jax.ShapeDtypeStruct((t, width), BF16),
        compiler_params=_params(("parallel", "parallel")),
    )(proj, kv, kv, gq.reshape(1, hd), gk.reshape(1, hd))


def _mem_bwd(proj, off, kv, do, gq, gk, tq, dproj, rider=None):
    t = proj.shape[0]
    m, width = kv.shape[0], kv.shape[1] // 2
    hd = width // MEM_HEADS
    tq = _tile(t, tq)
    nq = t // tq
    blk0 = off // hd
    scale = 1.0 / math.sqrt(hd)
    host = _Host(rider)

    def body(*refs):
        q_ref, k_ref, v_ref, do_ref, gq_ref, gk_ref = refs[:6]
        pos = 7
        r_ins = refs[pos:pos + host.n_in]; pos += host.n_in
        dq_ref, dk_ref, dv_ref, ggq_ref, ggk_ref = refs[pos:pos + 5]; pos += 5
        r_outs = refs[pos:pos + host.n_out]; pos += host.n_out
        dkh_ref, dvh_ref = refs[pos:pos + 2]; pos += 2
        sems = refs[pos:]
        h, i = pl.program_id(0), pl.program_id(1)

        def compute():
            qn, rq = _head_rms(q_ref[...].astype(F32))
            kn, rk = _head_rms(k_ref[...])
            qhat = (qn * gq_ref[...]).astype(BF16)
            khat = (kn * gk_ref[...]).astype(BF16)
            vb = v_ref[...].astype(BF16)
            dob = do_ref[...]
            s = _dot(qhat, khat, NT) * scale
            p = jnp.exp(s - jnp.max(s, axis=-1, keepdims=True))
            p = p / jnp.sum(p, axis=-1, keepdims=True)
            dp = _dot(dob, vb, NT)
            ds = p * (dp - jnp.sum(dp * p, axis=-1, keepdims=True))
            dsb = ds.astype(BF16)
            dq, ggq = _head_rms_bwd(_dot(dsb, khat, NN) * scale, qn, rq, gq_ref[...])
            dq_ref[...] = dq.astype(dq_ref.dtype)

            @pl.when(i == 0)
            def _():
                dkh_ref[...] = jnp.zeros_like(dkh_ref)
                dvh_ref[...] = jnp.zeros_like(dvh_ref)

            @pl.when(jnp.logical_and(h == 0, i == 0))
            def _():
                ggq_ref[...] = jnp.zeros_like(ggq_ref)
                ggk_ref[...] = jnp.zeros_like(ggk_ref)

            dkh_ref[...] += _dot(dsb, qhat, TN)
            dvh_ref[...] += _dot(p.astype(BF16), dob, TN)
            ggq_ref[...] += ggq

            @pl.when(i == nq - 1)
            def _():
                dk, ggk = _head_rms_bwd(dkh_ref[...] * scale, kn, rk, gk_ref[...])
                dk_ref[...] = dk.astype(dk_ref.dtype)
                dv_ref[...] = dvh_ref[...].astype(dv_ref.dtype)
                ggk_ref[...] += ggk

        first = jnp.logical_and(h == 0, i == 0)
        last = jnp.logical_and(h == MEM_HEADS - 1, i == nq - 1)
        host.run(first, last, r_ins, r_outs, sems, compute)

    vec = pl.BlockSpec((1, hd), lambda h, i: (0, 0))
    kblk = pl.BlockSpec((m, hd), lambda h, i: (0, h))
    res = pl.pallas_call(
        body,
        name="mem_bwd",
        grid=(MEM_HEADS, nq),
        in_specs=[
            pl.BlockSpec((tq, hd), lambda h, i: (i, blk0 + h)), kblk,
            pl.BlockSpec((m, hd), lambda h, i: (0, MEM_HEADS + h)),
            pl.BlockSpec((tq, hd), lambda h, i: (i, h)), vec, vec, ANY,
        ] + host.in_specs,
        out_specs=[pl.BlockSpec((tq, hd), lambda h, i: (i, blk0 + h)), kblk, kblk, vec, vec] + host.out_specs,
        out_shape=[
            jax.ShapeDtypeStruct(dproj.shape, dproj.dtype),
            jax.ShapeDtypeStruct((m, width), BF16),
            jax.ShapeDtypeStruct((m, width), BF16),
            jax.ShapeDtypeStruct((1, hd), F32),
            jax.ShapeDtypeStruct((1, hd), F32),
        ] + host.out_shapes,
        input_output_aliases={6: 0},
        scratch_shapes=[pltpu.VMEM((m, hd), F32), pltpu.VMEM((m, hd), F32)] + host.scratch,
        compiler_params=_params(("arbitrary", "arbitrary")),
    )(proj, kv, kv, do, gq.reshape(1, hd), gk.reshape(1, hd), dproj, *host.ins)
    dproj, dk, dv, ggq, ggk = res[:5]
    return (dproj, jnp.concatenate([dk, dv], axis=1), ggq.reshape(hd), ggk.reshape(hd), *res[5:])


def _sigmoid(z):
    return 1.0 / (1.0 + jnp.exp(-z))


def _merge_fwd(proj, o3, tm, tc):
    t, d = o3[0].shape
    tm = _tile(t, tm)

    def body(g_ref, oa_ref, ob_ref, oc_ref, out_ref):
        acc = jnp.zeros((tm, tc), F32)
        for s, o_ref in enumerate((oa_ref, ob_ref, oc_ref)):
            acc = acc + _sigmoid(g_ref[:, s * tc:(s + 1) * tc].astype(F32)) * o_ref[...].astype(F32)
        out_ref[...] = acc.astype(out_ref.dtype)

    blk = pl.BlockSpec((tm, tc), lambda i, j: (i, j))
    return pl.pallas_call(
        body,
        name="merge_fwd",
        grid=(t // tm, d // tc),
        in_specs=[pl.BlockSpec((tm, 3 * tc), lambda i, j: (i, j)), blk, blk, blk],
        out_specs=blk,
        out_shape=jax.ShapeDtypeStruct((t, d), BF16),
        compiler_params=_params(("parallel", "parallel")),
    )(proj, *o3)


def _merge_bwd(proj, o3, dm, tm, tc):
    t, d = dm.shape
    tm = _tile(t, tm)

    def body(g_ref, oa_ref, ob_ref, oc_ref, dm_ref, dg_ref, da_ref, db_ref, dc_ref):
        dmf = dm_ref[...].astype(F32)
        for s, (o_ref, do_ref) in enumerate(((oa_ref, da_ref), (ob_ref, db_ref), (oc_ref, dc_ref))):
            g = _sigmoid(g_ref[:, s * tc:(s + 1) * tc].astype(F32))
            do_ref[...] = (dmf * g).astype(do_ref.dtype)
            dg_ref[:, s * tc:(s + 1) * tc] = (dmf * o_ref[...].astype(F32) * g * (1.0 - g)).astype(dg_ref.dtype)

    blk = pl.BlockSpec((tm, tc), lambda i, j: (i, j))
    wide = pl.BlockSpec((tm, 3 * tc), lambda i, j: (i, j))
    return pl.pallas_call(
        body,
        name="merge_bwd",
        grid=(t // tm, d // tc),
        in_specs=[wide, blk, blk, blk, blk],
        out_specs=[wide, blk, blk, blk],
        out_shape=[jax.ShapeDtypeStruct(proj.shape, BF16)] + [jax.ShapeDtypeStruct((t, d), BF16)] * 3,
        compiler_params=_params(("parallel", "parallel")),
    )(proj, *o3, dm)


def _w_in_chunks(d, tc):
    cw = d // 2
    heads = cw // FOX_HEAD_DIM
    conv0, fox0, f0, mq0, gate0 = 0, 3 * cw, 6 * cw, 6 * cw + heads, 7 * cw + heads
    chunks = [(gate0 + s * d + j * tc, gate0 + s * d + (j + 1) * tc) for j in range(d // tc) for s in range(N_BRANCHES)]
    chunks += [(conv0 + s * cw + j * LANES, conv0 + s * cw + (j + 1) * LANES) for j in range(cw // LANES) for s in range(3)]
    chunks += [(fox0 + s * cw + j * FOX_HEAD_DIM, fox0 + s * cw + (j + 1) * FOX_HEAD_DIM) for j in range(heads) for s in range(3)]
    chunks.append((mq0, mq0 + cw))
    return chunks, (f0, f0 + heads)


ROW_TILE = 16
GROUP = 128
GROUP_BACK = 112
SCRATCH_ROWS = 2 * GROUP + 32


def _padded_rows(r):
    return -(-r // GROUP_BACK) * GROUP_BACK


def _rows_from(scr_ref, y_ref, q8, fine, g):
    x = scr_ref[pl.ds(pl.multiple_of(q8 * 8, 8), g + 8), :]
    for s in range(8):
        @pl.when(fine == s)
        def _(s=s):
            y_ref[...] = (x if s == 0 else pltpu.roll(x, g + 8 - s, axis=0))[0:g]


def _assemble(name, tbl, grid, step, in_specs, out_spec, out_shape, operands, g, w1, cols_of):
    has_f = len(in_specs) == 3
    k = out_shape.shape[-1]
    c = cols_of

    def body(*refs):
        t_ref, s1_ref, s2_ref = refs[:3]
        f_ref = refs[3] if has_f else None
        out_ref = refs[3 + has_f]
        scr1, scr2, scrf, y_ref = refs[4 + has_f:]
        t = step()

        @pl.when(t == 0)
        def _():
            scr1[...] = jnp.zeros_like(scr1)
            scr2[...] = jnp.zeros_like(scr2)
            scrf[...] = jnp.zeros_like(scrf)

        rows = lax.broadcasted_iota(jnp.int32, (g, k), 0)
        n1, a2 = t_ref[c["n1"], t], t_ref[c["a2"], t]
        scr1[0:w1, :] = (s1_ref[0] if len(s1_ref.shape) == 3 else s1_ref[...]).astype(F32)
        _rows_from(scr1, y_ref, t_ref[c["q1"], t], t_ref[c["s1"], t], g)
        out_ref[...] = y_ref[...].astype(out_ref.dtype)

        @pl.when(a2 < g)
        def _():
            scr2[g:g + s2_ref.shape[0], :] = s2_ref[...].astype(F32)
            _rows_from(scr2, y_ref, t_ref[c["q2"], t], t_ref[c["s2"], t], g)
            out_ref[...] = jnp.where(rows < n1, out_ref[...].astype(F32), y_ref[...]).astype(out_ref.dtype)

        if has_f:
            fa, fb = t_ref[c["fa"], t], t_ref[c["fb"], t]

            @pl.when(fb > fa)
            def _():
                scrf[g:g + f_ref.shape[0], :] = f_ref[...].astype(F32)
                _rows_from(scrf, y_ref, t_ref[c["qf"], t], t_ref[c["sf"], t], g)
                inside = jnp.logical_and(rows >= fa, rows < fb)
                out_ref[...] = jnp.where(inside, y_ref[...], out_ref[...].astype(F32)).astype(out_ref.dtype)

            valid = t_ref[c["valid"], t]

            @pl.when(valid < g)
            def _():
                out_ref[...] = jnp.where(rows < valid, out_ref[...].astype(F32), 0.0).astype(out_ref.dtype)

    return pl.pallas_call(
        body,
        name=name,
        grid_spec=pltpu.PrefetchScalarGridSpec(
            num_scalar_prefetch=1, grid=grid, in_specs=in_specs, out_specs=out_spec,
            scratch_shapes=[pltpu.VMEM((SCRATCH_ROWS, k), F32)] * 3 + [pltpu.VMEM((g, k), F32)]),
        out_shape=out_shape,
        compiler_params=_params(("arbitrary",) * len(grid)),
    )(jnp.asarray(tbl), *operands)


def _pack_w_in(w8, d, tc):
    blocks, rp, k = w8.shape
    chunks, (f_lo, f_hi) = _w_in_chunks(d, tc)
    r = max(hi for _, hi in chunks) // blocks
    g, w1 = GROUP, GROUP + ROW_TILE
    table = []
    for lo, hi in chunks:
        for g0 in range(lo, hi, g):
            b1, r1 = divmod(g0, r)
            n1 = min(g, r - r1)
            st1 = min(r1 // ROW_TILE * ROW_TILE, rp - w1)
            o1, o2 = r1 - st1, g - n1
            b2 = b1 + 1 if n1 < g else 0
            table.append((b1, st1, o1 // 8, o1 % 8, n1, n1, b2, o2 // 8, o2 % 8))
    names = ("b1", "st1", "q1", "s1", "n1", "a2", "b2", "q2", "s2")
    cols_of = {n: i for i, n in enumerate(names)}
    tbl = np.array(table, np.int32).T
    c = cols_of
    w_all = _assemble(
        "pack_w_in", tbl, (len(table),), lambda: pl.program_id(0),
        [pl.BlockSpec((pl.Element(1), pl.Element(w1), pl.Element(k)), lambda i, t: (t[c["b1"], i], pl.multiple_of(t[c["st1"], i], ROW_TILE), 0)),
         pl.BlockSpec((None, g, k), lambda i, t: (t[c["b2"], i], 0, 0))],
        pl.BlockSpec((g, k), lambda i, t: (i, 0)),
        jax.ShapeDtypeStruct((len(table) * g, k), w8.dtype), [w8, w8], g, w1, cols_of)
    fb, fr = divmod(f_lo, r)
    return w_all, jnp.pad(w8[fb, fr:fr + f_hi - f_lo], ((0, F_ROWS - (f_hi - f_lo)), (0, 0)))


def _unpack_g_in(g_all, g_f, d, tc, blocks):
    n_all, k = g_all.shape
    chunks, (f_lo, f_hi) = _w_in_chunks(d, tc)
    r = max(hi for _, hi in chunks) // blocks
    rp = _padded_rows(r)
    g, w1 = GROUP_BACK, GROUP_BACK + ROW_TILE
    pos, spans = 0, [(f_lo, f_hi, None)]
    for lo, hi in chunks:
        spans.append((lo, hi, pos))
        pos += hi - lo
    spans.sort()
    table = []
    for b in range(blocks):
        for l0 in range(0, rp, g):
            valid = max(0, min(g, r - l0))
            g0, segs, fa, fb, of = b * r + l0, [], 0, 0, 0
            for lo, hi, p in spans:
                a, e = max(lo, g0), min(hi, g0 + valid)
                if a < e and p is None:
                    fa, fb, of = a - g0, e - g0, g + (a - lo) - (a - g0)
                elif a < e:
                    segs.append((a - g0, p + a - lo, e - a))
            assert len(segs) <= 2 and (not segs or segs[0][0] == 0 or len(segs) == 1)
            first = segs[0] if segs and segs[0][0] == 0 else (0, 0, 0)
            second = segs[-1] if segs and segs[-1][0] > 0 else (g, 0, 0)
            st1 = min(first[1] // ROW_TILE * ROW_TILE, n_all - w1)
            o1, o2 = first[1] - st1, g - second[0]
            assert second[1] % GROUP == 0
            table.append((st1, o1 // 8, o1 % 8, first[2], second[0], second[1] // GROUP, o2 // 8, o2 % 8,
                          fa, fb, of // 8, of % 8, valid))
    names = ("st1", "q1", "s1", "n1", "a2", "j2", "q2", "s2", "fa", "fb", "qf", "sf", "valid")
    cols_of = {n: i for i, n in enumerate(names)}
    tbl = np.array(table, np.int32).T
    c, per = cols_of, rp // g
    return _assemble(
        "unpack_g_in", tbl, (blocks, per), lambda: pl.program_id(0) * per + pl.program_id(1),
        [pl.BlockSpec((pl.Element(w1), pl.Element(k)), lambda b, u, t: (pl.multiple_of(t[c["st1"], b * per + u], ROW_TILE), 0)),
         pl.BlockSpec((GROUP, k), lambda b, u, t: (t[c["j2"], b * per + u], 0)),
         pl.BlockSpec((F_ROWS, k), lambda b, u, t: (0, 0))],
        pl.BlockSpec((None, g, k), lambda b, u, t: (b, u, 0)),
        jax.ShapeDtypeStruct((blocks, rp, k), g_all.dtype), [g_all, g_all, g_f], g, w1, cols_of)


def _unblock(w8):
    return w8.transpose(1, 0, 2).reshape(w8.shape[1], -1)


def _tile2(r, cols, tr, tcols):
    if r % 8 == 0:
        return _tile(r, tr), cols
    return r, _tile(cols, tcols)


def _pair_sum(name, g8, got, c):
    def body(c_ref, g_ref, s_ref, o_ref):
        o_ref[...] = (g_ref[...].astype(F32) + s_ref[...].astype(F32)).astype(o_ref.dtype)

    if g8.ndim == 4:
        _, r, k1, k2 = g8.shape
        tr = max(cand for cand in range(1, 385) if r % cand == 0)
        grid = (N_CHIPS, r // tr)
        shape = (None, tr, k1, k2)
        own = pl.BlockSpec(shape, lambda q, i, c_ref: (2 * q + c_ref[0], i, 0, 0))
        blk = pl.BlockSpec(shape, lambda q, i, c_ref: (q, i, 0, 0))
    else:
        _, r, cols = g8.shape
        tr, tcols = _tile2(r, cols, 256, 256)
        grid = (N_CHIPS, r // tr, cols // tcols)
        own = pl.BlockSpec((None, tr, tcols), lambda q, i, j, c_ref: (2 * q + c_ref[0], i, j))
        blk = pl.BlockSpec((None, tr, tcols), lambda q, i, j, c_ref: (q, i, j))
    return pl.pallas_call(
        body,
        name=name,
        grid_spec=pltpu.PrefetchScalarGridSpec(num_scalar_prefetch=1, grid=grid, in_specs=[own, blk], out_specs=blk),
        out_shape=jax.ShapeDtypeStruct((N_CHIPS,) + g8.shape[1:], BF16),
        compiler_params=_params(("parallel",) * len(grid)),
    )(c, g8, got)


def _local_step(x, mem, target, w, small, comm=None):
    t, d = x.shape
    cw = d // 2
    heads = cw // FOX_HEAD_DIM
    tc = min(512, d)
    tq = min(512, t)
    off_conv, off_fox, off_mq = 3 * d, 3 * d + 3 * cw, 3 * d + 6 * cw
    w = dict(w)
    w_all, w_f = _pack_w_in(w["w_in"], d, tc)
    big = dict(tm=1024, tn=512, tk=2048)
    wide_k = dict(tm=512, tn=1024, tk=4096)
    tall = dict(tm=2048, tn=512, tk=2048)

    h = _rms_fwd("rms1_fwd", x, small["norm1_g"])
    if comm:
        early = ("w_conv_out", "w_fox_out", "w_mem_out", "w_out", "w_mem_kv")
        proj, *got = _matmul("proj", "nt", h, w_all, outs=[BF16], rider=_gather_rider([comm["shards"][n] for n in early], False), **tall)
        for n, val in zip(early, got):
            w[n] = _unblock(val) if n in COLUMN_SPLIT else val.reshape(-1, val.shape[-1])
    else:
        proj = _matmul("proj", "nt", h, w_all, outs=[BF16], **tall)
    z_row = _matmul("proj_f", "nt", w_f, h, outs=[F32], tm=F_ROWS, tn=512, tk=2048)

    y_conv = _conv_fwd(proj, off_conv, small["conv_w"], LANES)

    b_col = jnp.pad(small["b_f"], (0, F_ROWS - heads)).reshape(F_ROWS, 1)
    c_row3 = _forget_fwd(z_row, b_col)[:heads].reshape(heads, 1, t)
    c_colb = _rows_to_colb(c_row3, tq)
    if comm:
        y_fox, lse, got = _fox_fwd(proj, off_fox, small["fox_q_g"], small["fox_k_g"], c_row3, c_colb, heads, tq,
                                   rider=_gather_rider([comm["shards"]["w_up"]], False))
        w["w_up"] = _unblock(got)
    else:
        y_fox, lse = _fox_fwd(proj, off_fox, small["fox_q_g"], small["fox_k_g"], c_row3, c_colb, heads, tq)

    nm = _rms_fwd("mem_rms_fwd", mem, small["mem_norm_g"])
    kv = _matmul("mem_kv", "nn", nm, w["w_mem_kv"], outs=[F32], tm=256, tn=512, tk=2048)
    y_mem = _mem_fwd(proj, off_mq, kv, small["mem_q_g"], small["mem_k_g"], tq)

    ys = (y_conv, y_fox, y_mem)
    w_outs = (w["w_conv_out"], w["w_fox_out"], w["w_mem_out"])
    o3 = [_matmul(f"branch_out{s}", "nn", ys[s], w_outs[s], outs=[BF16], **big) for s in range(3)]
    merged = _merge_fwd(proj, o3, 512, tc)
    x1 = _matmul("out_proj", "nn", merged, w["w_out"], outs=[F32], extras=[x],
                 epilogue=lambda acc, xr: (acc + xr,), **big)
    h2 = _rms_fwd("rms2_fwd", x1, small["norm2_g"])

    def up_epilogue(acc):
        return acc, jnp.square(jnp.maximum(acc, 0.0))

    if comm:
        up, act, got = _matmul("mlp_up", "nn", h2, w["w_up"], outs=[BF16, BF16], epilogue=up_epilogue,
                               rider=_gather_rider([comm["shards"]["w_down"]], True), **big)
        w["w_down"] = got.reshape(-1, got.shape[-1])
    else:
        up, act = _matmul("mlp_up", "nn", h2, w["w_up"], outs=[BF16, BF16], epilogue=up_epilogue, **big)

    def loss_epilogue(acc, x1r, tr):
        dy = (acc + x1r - tr) * (1.0 / d)
        return dy, dy

    dy, dyb = _matmul("mlp_down", "nn", act, w["w_down"], outs=[F32, BF16], extras=[x1, target],
                      epilogue=loss_epilogue, tm=1024, tn=512, tk=4096)

    def dup_epilogue(acc, upr):
        return (acc * 2.0 * jnp.maximum(upr.astype(F32), 0.0),)

    def by_owner(g):
        return g.reshape(N_DEV, -1, g.shape[-1])

    g, parts = {}, {}
    g["w_down"] = _matmul("d_w_down", "tn", act, dyb, outs=[BF16], **wide_k)
    if comm:
        dup = _matmul("d_act", "nt", dyb, w["w_down"], outs=[BF16], extras=[up], epilogue=dup_epilogue, **tall)
        g["w_up"], got = _matmul("d_w_up", "tn", h2, dup, outs=[BF16], out_blocks=True,
                                 rider=_pair_rider([by_owner(g["w_down"])]), **wide_k)
        pair = _pair_sum("pair_w_down", by_owner(g["w_down"]), got, comm["c"])
        dh2, parts["w_down"], got = _matmul("d_h2", "nt", dup, w["w_up"], outs=[F32],
                                            rider=_join_riders(_chip_rider([pair]), _pair_rider([g["w_up"]])), **tall)
        pair_up = _pair_sum("pair_w_up", g["w_up"], got, comm["c"])
    else:
        dup = _matmul("d_act", "nt", dyb, w["w_down"], outs=[BF16], extras=[up], epilogue=dup_epilogue, **tall)
        g["w_up"] = _matmul("d_w_up", "tn", h2, dup, outs=[BF16], out_blocks=True, **wide_k)
        dh2 = _matmul("d_h2", "nt", dup, w["w_up"], outs=[F32], **tall)
    dx1, dx1b, g_norm2, dy_sq = _rms_bwd("rms2_bwd", dh2, x1, small["norm2_g"], res=dy)
    loss = dy_sq * (0.5 * d)

    g["w_out"] = _matmul("d_w_out", "tn", merged, dx1b, outs=[BF16], **wide_k)
    dmerged = _matmul("d_merged", "nt", dx1b, w["w_out"], outs=[BF16], **tall)
    dproj, *do3 = _merge_bwd(proj, o3, dmerged, 512, tc)
    names = ("w_conv_out", "w_fox_out", "w_mem_out")
    dys = []
    for s in range(3):
        g[names[s]] = _matmul(f"d_w_branch{s}", "tn", ys[s], do3[s], outs=[BF16], out_blocks=True, **wide_k)
        dys.append(_matmul(f"d_branch{s}", "nt", do3[s], w_outs[s], outs=[BF16], **tall))

    dproj, dkv, g_mq, g_mk = _mem_bwd(proj, off_mq, kv, dys[2], small["mem_q_g"], small["mem_k_g"], tq, dproj)
    g["w_mem_kv"] = _matmul("d_w_mem_kv", "tn", nm, dkv, outs=[BF16], **wide_k)
    dnm = _matmul("d_mem_norm", "nt", dkv, w["w_mem_kv"], outs=[F32], tm=256, tn=512, tk=2048)
    _, _, g_mem_norm, _ = _rms_bwd("mem_rms_bwd", dnm, mem, small["mem_norm_g"])

    mid = ("w_out", "w_conv_out", "w_fox_out", "w_mem_out", "w_mem_kv")
    if comm:
        mid8 = [g[n] if n in names else by_owner(g[n]) for n in mid]
        dproj, g_conv_w, *got = _conv_bwd(proj, off_conv, small["conv_w"], dys[0], LANES, dproj, rider=_pair_rider(mid8))
        pairs_mid = [_pair_sum("pair_" + n, g8, s4, comm["c"]) for n, g8, s4 in zip(mid, mid8, got)]
        dproj, dc, g_fq, g_fk, parts["w_up"] = _fox_bwd(proj, off_fox, y_fox, dys[1], small["fox_q_g"], small["fox_k_g"], c_row3,
                                                        c_colb, lse, heads, tq, dproj, rider=_chip_rider([pair_up]))
    else:
        dproj, g_conv_w = _conv_bwd(proj, off_conv, small["conv_w"], dys[0], LANES, dproj)
        dproj, dc, g_fq, g_fk = _fox_bwd(proj, off_fox, y_fox, dys[1], small["fox_q_g"], small["fox_k_g"], c_row3, c_colb,
                                         lse, heads, tq, dproj)
    dc_row = jnp.pad(dc.reshape(heads, t), ((0, F_ROWS - heads), (0, 0)))
    dz_row, db = _forget_bwd(z_row, b_col, dc_row)

    if comm:
        g_all, *got = _matmul("d_w_in", "tn", dproj, h, outs=[BF16], j_outer=True, rider=_chip_rider(pairs_mid), **wide_k)
        parts.update(zip(mid, got))
    else:
        g_all = _matmul("d_w_in", "tn", dproj, h, outs=[BF16], j_outer=True, **wide_k)
    g_wf = _matmul("d_w_f", "nn", dz_row, h, outs=[BF16], tm=F_ROWS, tn=512, tk=4096)
    g["w_in"] = _unpack_g_in(g_all, g_wf, d, tc, w["w_in"].shape[0])
    dh = _matmul("d_h_f", "tn", dz_row, w_f, outs=[F32], tm=1024, tn=512, tk=F_ROWS)
    add_prev = lambda acc, prev: (acc + prev,)
    if comm:
        g_in8 = g["w_in"]
        got = _run_rider("pair_exchange_w_in", _pair_rider([g_in8]))[0]
        pair = _pair_sum("pair_w_in", g_in8, got, comm["c"])
        dh, parts["w_in"] = _matmul("d_h", "nn", dproj, w_all, outs=[F32], extras=[dh], epilogue=add_prev,
                                    rider=_chip_rider([pair]), tm=1024, tn=512, tk=3328)
    else:
        dh = _matmul("d_h", "nn", dproj, w_all, outs=[F32], extras=[dh], epilogue=add_prev, tm=1024, tn=512, tk=3328)
    grad_x, _, g_norm1, _ = _rms_bwd("rms1_bwd", dh, x, small["norm1_g"], res=dx1)

    gs = dict(norm1_g=g_norm1, b_f=db[:heads, 0], conv_w=g_conv_w, fox_q_g=g_fq.reshape(-1), fox_k_g=g_fk.reshape(-1),
              mem_norm_g=g_mem_norm, mem_q_g=g_mq, mem_k_g=g_mk, norm2_g=g_norm2)
    return loss, grad_x, (parts if comm else g), gs


def _adamw_math(w, g, m, v):
    m = ADAM_B1 * m + (1.0 - ADAM_B1) * g
    v = ADAM_B2 * v + (1.0 - ADAM_B2) * jnp.square(g)
    m_hat = m / (1.0 - ADAM_B1 ** ADAM_STEP)
    v_hat = v / (1.0 - ADAM_B2 ** ADAM_STEP)
    delta = -ADAM_LR * (m_hat / (jnp.sqrt(v_hat) + ADAM_EPS) + ADAM_WD * w)
    return delta, m, v


def _adamw(name, parts, w, m, v):
    r, c = w.shape
    n_parts, rp = parts.shape[:2]
    if rp == r:
        tr, tc = _tile2(r, c, 128, 256)
    else:
        tr, tc = _tile(rp, 256), _tile(c, 1024)

    def body(p_ref, w_ref, m_ref, v_ref, g_ref, d_ref, nm_ref, nv_ref):
        g = p_ref[0].astype(F32)
        for s in range(1, n_parts):
            g = g + p_ref[s].astype(F32)
        delta, nm, nv = _adamw_math(w_ref[...], g, m_ref[...], v_ref[...])
        g_ref[...] = g
        d_ref[...] = delta
        nm_ref[...] = nm
        nv_ref[...] = nv

    blk = pl.BlockSpec((tr, tc), lambda i, j: (i, j))
    return pl.pallas_call(
        body,
        name=name,
        grid=(rp // tr, c // tc),
        in_specs=[pl.BlockSpec((n_parts, tr, tc), lambda i, j: (0, i, j)), blk, blk, blk],
        out_specs=[blk] * 4,
        out_shape=[jax.ShapeDtypeStruct((r, c), F32)] * 4,
        compiler_params=_params(("parallel", "parallel")),
    )(parts, w, m, v)


def _sum_parts(name, parts):
    n_parts, r, c = parts.shape

    def body(p_ref, o_ref):
        acc = p_ref[0]
        for s in range(1, n_parts):
            acc = acc + p_ref[s]
        o_ref[...] = acc

    return pl.pallas_call(body, name=name, out_shape=jax.ShapeDtypeStruct((r, c), F32))(parts)


BIG = ("w_in", "w_mem_kv", "w_conv_out", "w_fox_out", "w_mem_out", "w_out", "w_up", "w_down")
COLUMN_SPLIT = ("w_in", "w_conv_out", "w_fox_out", "w_mem_out", "w_up")
SMALL = ("norm1_g", "b_f", "conv_w", "fox_q_g", "fox_k_g", "mem_norm_g", "mem_q_g", "mem_k_g", "norm2_g")
WEIGHTS = ("norm1_g", "w_in", "b_f", "conv_w", "fox_q_g", "fox_k_g", "mem_norm_g", "w_mem_kv", "mem_q_g", "mem_k_g",
           "w_conv_out", "w_fox_out", "w_mem_out", "w_out", "norm2_g", "w_up", "w_down")


def _pack(vectors):
    rows = []
    for vec in vectors:
        n = vec.shape[0]
        rows.append(jnp.pad(vec, (0, -n % LANES)).reshape(-1, LANES))
    out = jnp.concatenate(rows, axis=0)
    return jnp.pad(out, ((0, -out.shape[0] % 8), (0, 0)))


def _unpack(packed, sizes):
    out, row = [], 0
    for n in sizes:
        nr = -(-n // LANES)
        out.append(packed[row:row + nr].reshape(-1)[:n])
        row += nr
    return out


def kernel(x, mem, norm1_g, w_in, b_f, conv_w, fox_q_g, fox_k_g, mem_norm_g, w_mem_kv, mem_q_g, mem_k_g, w_conv_out, w_fox_out, w_mem_out, w_out, norm2_g, w_up, w_down, loss_target, m_norm1_g, m_w_in, m_b_f, m_conv_w, m_fox_q_g, m_fox_k_g, m_mem_norm_g, m_w_mem_kv, m_mem_q_g, m_mem_k_g, m_w_conv_out, m_w_fox_out, m_w_mem_out, m_w_out, m_norm2_g, m_w_up, m_w_down, v_norm1_g, v_w_in, v_b_f, v_conv_w, v_fox_q_g, v_fox_k_g, v_mem_norm_g, v_w_mem_kv, v_mem_q_g, v_mem_k_g, v_w_conv_out, v_w_fox_out, v_w_mem_out, v_w_out, v_norm2_g, v_w_up, v_w_down):
    args = dict(locals())
    wts = {n: args[n] for n in WEIGHTS}
    ms = {n: args["m_" + n] for n in WEIGHTS}
    vs = {n: args["v_" + n] for n in WEIGHTS}
    x_pos, y_pos, c_pos = _position()
    me = _index(x_pos, y_pos, c_pos)

    shards = {n: wts[n].astype(BF16) for n in BIG if n != "w_in"}
    rows_in = w_in.shape[1]
    shards["w_in"] = jnp.pad(w_in.T.astype(BF16), ((0, _padded_rows(rows_in) - rows_in), (0, 0)))
    wi, cw8 = _run_rider("all_gather_first", _gather_rider([shards["w_in"], conv_w], True))
    full = {"w_in": wi}
    small = {n: wts[n] for n in SMALL}
    small["conv_w"] = _unblock(cw8)
    comm = {"shards": shards, "c": c_pos.astype(jnp.int32).reshape(1)}

    loss, grad_x, parts, gs = _local_step(x[0], mem[0], loss_target[0], full, small, comm)

    out_g, out_d, out_m, out_v = {}, {}, {}, {}
    for n in BIG:
        if n == "w_in":
            res = _adamw("adamw_" + n, parts[n], wts[n].T, ms[n].T, vs[n].T)
            out_g[n], out_d[n], out_m[n], out_v[n] = (r.T for r in res)
        else:
            out_g[n], out_d[n], out_m[n], out_v[n] = _adamw("adamw_" + n, parts[n], wts[n], ms[n], vs[n])

    small_sizes = [int(math.prod(gs[n].shape)) for n in SMALL]
    packed = _pack([gs[n].reshape(-1) for n in SMALL])
    gsum = _sum_parts("sum_small", _run_rider("exchange_small", _broadcast_rider([packed]))[0])
    gsmall = dict(zip(SMALL, _unpack(gsum, small_sizes)))
    cols = conv_w.shape[1]
    gsmall["conv_w"] = lax.dynamic_slice(gsmall["conv_w"].reshape(CONV_TAPS, -1), (0, me * cols), (CONV_TAPS, cols)).reshape(-1)
    pg, pw, pm, pv = (_pack([src[n].reshape(-1) for n in SMALL]) for src in (gsmall, wts, ms, vs))
    _, sd, sm, sv = _adamw("adamw_small", pg[None], pw, pm, pv)
    local_sizes = [int(math.prod(wts[n].shape)) for n in SMALL]
    for dst, src in ((out_d, sd), (out_m, sm), (out_v, sv)):
        for n, val in zip(SMALL, _unpack(src, local_sizes)):
            dst[n] = val.reshape(wts[n].shape)
    for n in SMALL:
        out_g[n] = gsmall[n].reshape(wts[n].shape)

    loss = lax.psum(loss, MESH_AXES)
    return (loss, grad_x[None], *[out_g[n] for n in WEIGHTS], *[out_d[n] for n in WEIGHTS],
            *[out_m[n] for n in WEIGHTS], *[out_v[n] for n in WEIGHTS])
```

```python
import math

import numpy as np
import jax
import jax.numpy as jnp
from jax import lax
from jax.experimental import pallas as pl
from jax.experimental.pallas import tpu as pltpu

F32 = jnp.float32
BF16 = jnp.bfloat16

EPS = 1e-6
N_DEV = 8
N_CHIPS = 4
FOX_HEAD_DIM = 128
MEM_HEADS = 4
CONV_TAPS = 3
N_BRANCHES = 3
F_ROWS = 16

ADAM_LR = 0.001
ADAM_B1 = 0.9
ADAM_B2 = 0.999
ADAM_EPS = 1e-08
ADAM_WD = 0.01
ADAM_STEP = 10

V7X_VMEM_BYTES = 64 * 1024 * 1024
VMEM_LIMIT = V7X_VMEM_BYTES * 3 // 4
LANES = 128
NEG = -1e30

MESH_AXES = ("x", "y", "c")
MESH = pl.DeviceIdType.MESH
ANY = pl.BlockSpec(memory_space=pl.ANY)

NN = (((1,), (0,)), ((), ()))
NT = (((1,), (1,)), ((), ()))
TN = (((0,), (0,)), ((), ()))


def _params(sem):
    return pltpu.CompilerParams(dimension_semantics=sem, vmem_limit_bytes=VMEM_LIMIT)


def _dot(a, b, dn):
    return lax.dot_general(a, b, dn, preferred_element_type=F32)


def _tile(n, t):
    if n <= t:
        return n
    for step in (LANES, 16):
        for cand in range(t - t % step, 0, -step):
            if n % cand == 0:
                return cand
    raise ValueError((n, t))


class _Rider:
    def __init__(self, ins, out_shapes, sem_shapes, start, finish, middle=None):
        self.ins, self.out_shapes, self.sem_shapes = list(ins), list(out_shapes), list(sem_shapes)
        self.start, self.finish, self.middle = start, finish, middle


def _position():
    return lax.axis_index("x"), lax.axis_index("y"), lax.axis_index("c")


def _index(px, py, pc):
    return 4 * px + 2 * py + pc


def _dma_sems(n, per):
    return [pltpu.SemaphoreType.DMA((n, per)), pltpu.SemaphoreType.DMA((n, per)), pltpu.SemaphoreType.DMA((n,))]


def _gather_rider(shards, pass_on):
    n = len(shards)

    def copies(ins, outs, sems):
        send_sems, recv_sems, local_sems = sems
        x, y, c = _position()
        me, sibling = (x, y, c), (x, y, 1 - c)
        chips = [(1 - x, y), (x, 1 - y), (1 - x, 1 - y)]

        def copy(a, k, block, to, src=None, k_send=None):
            rows = outs[a].at[_index(*block)]
            return pltpu.make_async_remote_copy(
                src_ref=rows if src is None else src, dst_ref=rows,
                send_sem=send_sems.at[a, k if k_send is None else k_send], recv_sem=recv_sems.at[a, k],
                device_id=to, device_id_type=MESH)

        mine = [pltpu.make_async_copy(ins[a], outs[a].at[_index(*me)], local_sems.at[a]) for a in range(n)]
        first = []
        for a in range(n):
            first.append(copy(a, 0, me, sibling, src=ins[a]))
            first += [copy(a, 1 + j, me, (*chips[j], c), src=ins[a]) for j in range(2 if pass_on else 3)]
        return copy, mine, first, me, sibling, chips, c

    def start(ins, outs, sems):
        _, mine, first, *_ = copies(ins, outs, sems)
        for cp in mine + first:
            cp.start()

    def by_kind(c, fn):
        if pass_on:
            pl.when(c == 1)(lambda: fn(0, 1))
            pl.when(c == 0)(lambda: fn(1, 0))
        else:
            fn(0, 1)

    def onward(copy, a, j_on, j_to, chips, c, sibling):
        third = [copy(a, 3, (*chips[j_on], c), (*chips[j_to], c), k_send=7)] if pass_on else []
        return third + [copy(a, 4 + j_on, (*chips[j_on], c), sibling)]

    def middle(ins, outs, sems):
        copy, _, _, me, sibling, chips, c = copies(ins, outs, sems)

        def fn(j_on, j_to):
            for a in range(n):
                copy(a, 1 + j_on, (*chips[j_on], c), me).wait_recv()
                for cp in onward(copy, a, j_on, j_to, chips, c, sibling):
                    cp.start()

        by_kind(c, fn)

    def finish(ins, outs, sems):
        copy, mine, first, me, sibling, chips, c = copies(ins, outs, sems)

        def fn(j_on, j_to):
            passed = [cp for a in range(n) for cp in onward(copy, a, j_on, j_to, chips, c, sibling)]
            for a in range(n):
                for j in (j_to, 2):
                    copy(a, 1 + j, (*chips[j], c), me).wait_recv()
                    passed.append(copy(a, 4 + j, (*chips[j], c), sibling))
                    passed[-1].start()
            for a in range(n):
                copy(a, 0, sibling, me).wait_recv()
                for j, chip in enumerate(chips):
                    copy(a, 4 + j, (*chip, 1 - c), me).wait_recv()
            for cp in first + passed:
                cp.wait_send()
            for cp in mine:
                cp.wait()

        by_kind(c, fn)

    out_shapes = [jax.ShapeDtypeStruct((N_DEV,) + s.shape, s.dtype) for s in shards]
    return _Rider(shards, out_shapes, _dma_sems(n, 8), start, finish, middle)


def _pair_rider(grads):
    n = len(grads)

    def copies(ins, outs, sems):
        send_sems, recv_sems, _ = sems
        x, y, c = _position()
        return [pltpu.make_async_remote_copy(
            src_ref=ins[a].at[2 * q + (1 - c)], dst_ref=outs[a].at[q],
            send_sem=send_sems.at[a, q], recv_sem=recv_sems.at[a, q], device_id=(x, y, 1 - c), device_id_type=MESH)
            for a in range(n) for q in range(N_CHIPS)]

    def start(ins, outs, sems):
        for cp in copies(ins, outs, sems):
            cp.start()

    def finish(ins, outs, sems):
        cps = copies(ins, outs, sems)
        for cp in cps:
            cp.wait_recv()
        for cp in cps:
            cp.wait_send()

    out_shapes = [jax.ShapeDtypeStruct((N_CHIPS,) + g.shape[1:], g.dtype) for g in grads]
    return _Rider(grads, out_shapes, _dma_sems(n, N_CHIPS), start, finish)


def _chip_rider(parts):
    n = len(parts)

    def copies(ins, outs, sems):
        send_sems, recv_sems, local_sems = sems
        x, y, c = _position()
        q_me = 2 * x + y
        chips = [(1 - x, y), (x, 1 - y), (1 - x, 1 - y)]
        mine = [pltpu.make_async_copy(ins[a].at[q_me], outs[a].at[q_me], local_sems.at[a]) for a in range(n)]
        sends, arrivals = [], []
        for a in range(n):
            for j, (tx, ty) in enumerate(chips):
                q_t = 2 * tx + ty
                sends.append(pltpu.make_async_remote_copy(
                    src_ref=ins[a].at[q_t], dst_ref=outs[a].at[q_me],
                    send_sem=send_sems.at[a, j], recv_sem=recv_sems.at[a, j], device_id=(tx, ty, c), device_id_type=MESH))
                arrivals.append(pltpu.make_async_remote_copy(
                    src_ref=ins[a].at[q_t], dst_ref=outs[a].at[q_t],
                    send_sem=send_sems.at[a, j], recv_sem=recv_sems.at[a, j], device_id=(tx, ty, c), device_id_type=MESH))
        return mine, sends, arrivals

    def start(ins, outs, sems):
        mine, sends, _ = copies(ins, outs, sems)
        for cp in mine + sends:
            cp.start()

    def finish(ins, outs, sems):
        mine, sends, arrivals = copies(ins, outs, sems)
        for cp in arrivals:
            cp.wait_recv()
        for cp in sends:
            cp.wait_send()
        for cp in mine:
            cp.wait()

    out_shapes = [jax.ShapeDtypeStruct(p.shape, p.dtype) for p in parts]
    return _Rider(parts, out_shapes, _dma_sems(n, 3), start, finish)


def _broadcast_rider(values):
    n = len(values)

    def copies(ins, outs, sems):
        send_sems, recv_sems, local_sems = sems
        x, y, c = _position()
        me = _index(x, y, c)

        def peer(k):
            return (1 - x if k & 4 else x, 1 - y if k & 2 else y, 1 - c if k & 1 else c)

        mine = [pltpu.make_async_copy(ins[a], outs[a].at[me], local_sems.at[a]) for a in range(n)]
        sends, arrivals = [], []
        for a in range(n):
            for k in range(1, N_DEV):
                common = dict(send_sem=send_sems.at[a, k - 1], recv_sem=recv_sems.at[a, k - 1], device_id=peer(k), device_id_type=MESH)
                sends.append(pltpu.make_async_remote_copy(src_ref=ins[a], dst_ref=outs[a].at[me], **common))
                arrivals.append(pltpu.make_async_remote_copy(src_ref=ins[a], dst_ref=outs[a].at[_index(*peer(k))], **common))
        return mine, sends, arrivals

    def start(ins, outs, sems):
        mine, sends, _ = copies(ins, outs, sems)
        for cp in mine + sends:
            cp.start()

    def finish(ins, outs, sems):
        mine, sends, arrivals = copies(ins, outs, sems)
        for cp in arrivals:
            cp.wait_recv()
        for cp in sends:
            cp.wait_send()
        for cp in mine:
            cp.wait()

    out_shapes = [jax.ShapeDtypeStruct((N_DEV,) + v.shape, v.dtype) for v in values]
    return _Rider(values, out_shapes, _dma_sems(n, 7), start, finish)


def _join_riders(*riders):
    def each(fn_name, ins, outs, sems):
        i = o = s = 0
        for r in riders:
            n_i, n_o, n_s = len(r.ins), len(r.out_shapes), len(r.sem_shapes)
            if getattr(r, fn_name) is not None:
                getattr(r, fn_name)(ins[i:i + n_i], outs[o:o + n_o], sems[s:s + n_s])
            i, o, s = i + n_i, o + n_o, s + n_s

    middle = (lambda ins, outs, sems: each("middle", ins, outs, sems)) if any(r.middle for r in riders) else None
    return _Rider([a for r in riders for a in r.ins], [a for r in riders for a in r.out_shapes],
                  [a for r in riders for a in r.sem_shapes],
                  lambda ins, outs, sems: each("start", ins, outs, sems),
                  lambda ins, outs, sems: each("finish", ins, outs, sems), middle)


def _run_rider(name, rider):
    n_in, n_out = len(rider.ins), len(rider.out_shapes)

    def body(*refs):
        ins, outs, sems = refs[:n_in], refs[n_in:n_in + n_out], refs[n_in + n_out:]
        rider.start(ins, outs, sems)
        if rider.middle is not None:
            rider.middle(ins, outs, sems)
        rider.finish(ins, outs, sems)

    return pl.pallas_call(
        body, name=name, in_specs=[ANY] * n_in, out_specs=[ANY] * n_out, out_shape=rider.out_shapes,
        scratch_shapes=rider.sem_shapes)(*rider.ins)


class _Host:
    def __init__(self, rider):
        self.rider = rider
        self.n_in = len(rider.ins) if rider else 0
        self.n_out = len(rider.out_shapes) if rider else 0
        self.n_sem = len(rider.sem_shapes) if rider else 0
        self.ins = rider.ins if rider else []
        self.in_specs = [ANY] * self.n_in
        self.out_specs = [ANY] * self.n_out
        self.out_shapes = rider.out_shapes if rider else []
        self.scratch = rider.sem_shapes if rider else []

    def run(self, first, last, ins, outs, sems, compute, midway=None):
        if self.rider is None:
            compute()
            return

        @pl.when(first)
        def _():
            self.rider.start(ins, outs, sems)

        compute()
        if self.rider.middle is not None and midway is not None:
            pl.when(midway)(lambda: self.rider.middle(ins, outs, sems))

        @pl.when(last)
        def _():
            if self.rider.middle is not None and midway is None:
                self.rider.middle(ins, outs, sems)
            self.rider.finish(ins, outs, sems)


def _matmul(name, kind, a, b, *, tm, tn, tk, outs, epilogue=None, extras=(), out_blocks=False, rider=None, j_outer=False):
    if kind == "nn":
        (m, kdim), n = a.shape, b.shape[1]
    elif kind == "nt":
        (m, kdim), n = a.shape, b.shape[0]
    else:
        (kdim, m), n = a.shape, b.shape[1]
    if out_blocks:
        tn = min(tn, n // N_DEV)
    tm, tn, tk = _tile(m, tm), _tile(n, tn), _tile(kdim, tk)
    ni, nj, nk = m // tm, n // tn, kdim // tk

    def spec(shape, fn):
        return pl.BlockSpec(shape, (lambda g0, g1, k: fn(g1, g0, k)) if j_outer else fn)

    a_spec = spec((tk, tm), lambda i, j, k: (k, i)) if kind == "tn" else spec((tm, tk), lambda i, j, k: (i, k))
    b_spec = spec((tn, tk), lambda i, j, k: (j, k)) if kind == "nt" else spec((tk, tn), lambda i, j, k: (k, j))
    dn = {"nn": NN, "nt": NT, "tn": TN}[kind]

    tile_spec = spec((tm, tn), lambda i, j, k: (i, j))
    if out_blocks:
        width = n // N_DEV
        r_out = width // tn
        out_shape = [jax.ShapeDtypeStruct((N_DEV, m, width), dt) for dt in outs]
        out_specs = [spec((None, tm, tn), lambda i, j, k: (j // r_out, i, j % r_out)) for _ in outs]
    else:
        out_shape = [jax.ShapeDtypeStruct((m, n), dt) for dt in outs]
        out_specs = [tile_spec for _ in outs]
    n_ex, n_out = len(extras), len(outs)
    host = _Host(rider)
    n_acc = 1 if nk > 1 else 0

    def body(*refs):
        a_ref, b_ref = refs[0], refs[1]
        pos = 2
        ex_refs = refs[pos:pos + n_ex]; pos += n_ex
        r_ins = refs[pos:pos + host.n_in]; pos += host.n_in
        out_refs = refs[pos:pos + n_out]; pos += n_out
        r_outs = refs[pos:pos + host.n_out]; pos += host.n_out
        acc_ref = refs[pos] if n_acc else None
        sems = refs[pos + n_acc:]
        i, j, k = pl.program_id(1 if j_outer else 0), pl.program_id(0 if j_outer else 1), pl.program_id(2)

        def finish_tile(acc):
            vals = (acc,) if epilogue is None else epilogue(acc, *[e[...] for e in ex_refs])
            for o_ref, v in zip(out_refs, vals):
                o_ref[...] = v.astype(o_ref.dtype)

        def compute():
            part = _dot(a_ref[...], b_ref[...], dn)
            if nk == 1:
                finish_tile(part)
                return

            @pl.when(k == 0)
            def _():
                acc_ref[...] = part

            @pl.when(jnp.logical_and(k > 0, k < nk - 1))
            def _():
                acc_ref[...] += part

            @pl.when(k == nk - 1)
            def _():
                finish_tile(acc_ref[...] + part)

        first = jnp.logical_and(jnp.logical_and(i == 0, j == 0), k == 0)
        last = jnp.logical_and(jnp.logical_and(i == ni - 1, j == nj - 1), k == nk - 1)
        step = (pl.program_id(0) * (ni if j_outer else nj) + pl.program_id(1)) * nk + k
        host.run(first, last, r_ins, r_outs, sems, compute, midway=step == (ni * nj * nk * 3) // 5)

    sem = ("arbitrary",) * 3 if rider else ("parallel", "parallel", "arbitrary")
    res = pl.pallas_call(
        body,
        name=name,
        grid=(nj, ni, nk) if j_outer else (ni, nj, nk),
        in_specs=[a_spec, b_spec] + [tile_spec for _ in extras] + host.in_specs,
        out_specs=out_specs + host.out_specs,
        out_shape=out_shape + host.out_shapes,
        scratch_shapes=([pltpu.VMEM((tm, tn), F32)] if n_acc else []) + host.scratch,
        compiler_params=_params(sem),
    )(a, b, *extras, *host.ins)
    return res[0] if len(res) == 1 else res


def _rms_fwd(name, x, g, tm=512):
    t, d = x.shape
    tm = _tile(t, tm)

    def body(x_ref, g_ref, h_ref):
        xf = x_ref[...]
        r = lax.rsqrt(jnp.mean(xf * xf, axis=-1, keepdims=True) + EPS)
        h_ref[...] = (xf * r * g_ref[...]).astype(h_ref.dtype)

    return pl.pallas_call(
        body,
        name=name,
        grid=(t // tm,),
        in_specs=[pl.BlockSpec((tm, d), lambda i: (i, 0)), pl.BlockSpec((1, d), lambda i: (0, 0))],
        out_specs=pl.BlockSpec((tm, d), lambda i: (i, 0)),
        out_shape=jax.ShapeDtypeStruct((t, d), BF16),
        compiler_params=_params(("parallel",)),
    )(x, g.reshape(1, d))


def _rms_bwd(name, dh, x, g, res=None, tm=256):
    t, d = x.shape
    tm = _tile(t, tm)
    has_res = res is not None

    def body(*refs):
        if has_res:
            dh_ref, x_ref, g_ref, res_ref, dx_ref, dxb_ref, gg_ref, ss_ref = refs
        else:
            dh_ref, x_ref, g_ref, dx_ref, dxb_ref, gg_ref, ss_ref = refs
        i = pl.program_id(0)
        xf = x_ref[...]
        r = lax.rsqrt(jnp.mean(xf * xf, axis=-1, keepdims=True) + EPS)
        xh = xf * r
        dhf = dh_ref[...].astype(F32)
        dxh = dhf * g_ref[...]
        dx = r * (dxh - xh * jnp.mean(dxh * xh, axis=-1, keepdims=True))

        @pl.when(i == 0)
        def _():
            gg_ref[...] = jnp.zeros_like(gg_ref)
            ss_ref[...] = jnp.zeros_like(ss_ref)

        if has_res:
            resf = res_ref[...]
            dx = dx + resf
            ss_ref[...] += jnp.sum(jnp.sum(resf * resf, axis=0, keepdims=True), axis=1, keepdims=True)
        dx_ref[...] = dx
        dxb_ref[...] = dx.astype(BF16)
        gg_ref[...] += jnp.sum(dhf * xh, axis=0, keepdims=True)

    row = pl.BlockSpec((tm, d), lambda i: (i, 0))
    vec = pl.BlockSpec((1, d), lambda i: (0, 0))
    one = pl.BlockSpec((1, 1), lambda i: (0, 0))
    ins = [dh, x, g.reshape(1, d)] + ([res] if has_res else [])
    dx, dxb, gg, ss = pl.pallas_call(
        body,
        name=name,
        grid=(t // tm,),
        in_specs=[row, row, vec] + ([row] if has_res else []),
        out_specs=[row, row, vec, one],
        out_shape=[jax.ShapeDtypeStruct((t, d), F32), jax.ShapeDtypeStruct((t, d), BF16), jax.ShapeDtypeStruct((1, d), F32),
                   jax.ShapeDtypeStruct((1, 1), F32)],
        compiler_params=_params(("arbitrary",)),
    )(*ins)
    return dx, dxb, gg.reshape(d), ss[0, 0]


def _head_rms(xf):
    r = lax.rsqrt(jnp.mean(xf * xf, axis=-1, keepdims=True) + EPS)
    return xf * r, r


def _head_rms_bwd(dy, xn, r, g):
    dxh = dy * g
    dx = r * (dxh - xn * jnp.mean(dxh * xn, axis=-1, keepdims=True))
    return dx, jnp.sum(dy * xn, axis=0, keepdims=True)


def _col_to_row(col):
    n = col.shape[0]
    eye = lax.broadcasted_iota(jnp.int32, (n, n), 0) == lax.broadcasted_iota(jnp.int32, (n, n), 1)
    return jnp.sum(jnp.where(eye, col, 0.0), axis=0, keepdims=True)


def _row_to_col(row):
    n = row.shape[1]
    eye = lax.broadcasted_iota(jnp.int32, (n, n), 0) == lax.broadcasted_iota(jnp.int32, (n, n), 1)
    return jnp.sum(jnp.where(eye, row, 0.0), axis=1, keepdims=True)


def _dproj_args(dproj, n_in):
    if dproj is None:
        return [], [], {}
    return [dproj], [ANY], {n_in: 0}


def _shift_down(u, s, rows):
    return jnp.where(rows >= s, pltpu.roll(u, s, axis=0), 0.0)


def _shift_up(u, s, rows, t):
    return jnp.where(rows < t - s, pltpu.roll(u, t - s, axis=0), 0.0)


def _conv_fwd(proj, off, conv_w, cb):
    t = proj.shape[0]
    c = conv_w.shape[1]
    blk0 = off // (3 * cb)

    def body(p_ref, w_ref, y_ref):
        rows = lax.broadcasted_iota(jnp.int32, (t, cb), 0)
        bg = p_ref[:, 0:cb].astype(F32)
        u = p_ref[:, cb:2 * cb].astype(F32) * p_ref[:, 2 * cb:3 * cb].astype(F32)
        w = w_ref[...]
        conv = w[2:3] * u + w[1:2] * _shift_down(u, 1, rows) + w[0:1] * _shift_down(u, 2, rows)
        y_ref[...] = (bg * conv).astype(y_ref.dtype)

    return pl.pallas_call(
        body,
        name="conv_fwd",
        grid=(c // cb,),
        in_specs=[pl.BlockSpec((t, 3 * cb), lambda j: (0, blk0 + j)), pl.BlockSpec((CONV_TAPS, cb), lambda j: (0, j))],
        out_specs=pl.BlockSpec((t, cb), lambda j: (0, j)),
        out_shape=jax.ShapeDtypeStruct((t, c), BF16),
        compiler_params=_params(("parallel",)),
    )(proj, conv_w)


def _conv_bwd(proj, off, conv_w, dy, cb, dproj, rider=None):
    t = proj.shape[0]
    c = conv_w.shape[1]
    blk0 = off // (3 * cb)
    nj = c // cb
    host = _Host(rider)

    def body(*refs):
        p_ref, w_ref, dy_ref = refs[:3]
        r_ins = refs[4:4 + host.n_in]
        dp_ref, gw_ref = refs[4 + host.n_in:6 + host.n_in]
        r_outs = refs[6 + host.n_in:6 + host.n_in + host.n_out]
        sems = refs[6 + host.n_in + host.n_out:]
        j = pl.program_id(0)

        def compute():
            rows = lax.broadcasted_iota(jnp.int32, (t, cb), 0)
            bg = p_ref[:, 0:cb].astype(F32)
            cg = p_ref[:, cb:2 * cb].astype(F32)
            v = p_ref[:, 2 * cb:3 * cb].astype(F32)
            u = cg * v
            w = w_ref[...]
            u1 = _shift_down(u, 1, rows)
            u2 = _shift_down(u, 2, rows)
            conv = w[2:3] * u + w[1:2] * u1 + w[0:1] * u2
            dyf = dy_ref[...].astype(F32)
            dconv = dyf * bg
            du = w[2:3] * dconv + w[1:2] * _shift_up(dconv, 1, rows, t) + w[0:1] * _shift_up(dconv, 2, rows, t)
            dp_ref[:, 0:cb] = (dyf * conv).astype(dp_ref.dtype)
            dp_ref[:, cb:2 * cb] = (du * v).astype(dp_ref.dtype)
            dp_ref[:, 2 * cb:3 * cb] = (du * cg).astype(dp_ref.dtype)
            gw_ref[0:1, :] = jnp.sum(dconv * u2, axis=0, keepdims=True)
            gw_ref[1:2, :] = jnp.sum(dconv * u1, axis=0, keepdims=True)
            gw_ref[2:3, :] = jnp.sum(dconv * u, axis=0, keepdims=True)

        host.run(j == 0, j == nj - 1, r_ins, r_outs, sems, compute)

    res = pl.pallas_call(
        body,
        name="conv_bwd",
        grid=(nj,),
        in_specs=[
            pl.BlockSpec((t, 3 * cb), lambda j: (0, blk0 + j)),
            pl.BlockSpec((CONV_TAPS, cb), lambda j: (0, j)),
            pl.BlockSpec((t, cb), lambda j: (0, j)),
            ANY,
        ] + host.in_specs,
        out_specs=[pl.BlockSpec((t, 3 * cb), lambda j: (0, blk0 + j)), pl.BlockSpec((CONV_TAPS, cb), lambda j: (0, j))] + host.out_specs,
        out_shape=[jax.ShapeDtypeStruct(dproj.shape, dproj.dtype), jax.ShapeDtypeStruct((CONV_TAPS, c), F32)] + host.out_shapes,
        input_output_aliases={3: 0},
        scratch_shapes=host.scratch,
        compiler_params=_params(("arbitrary",)),
    )(proj, conv_w, dy, dproj, *host.ins)
    return res


def _lane_scan(x, reverse):
    lane = lax.broadcasted_iota(jnp.int32, x.shape, 1)
    s = 1
    while s < LANES:
        if reverse:
            x = x + jnp.where(lane < LANES - s, pltpu.roll(x, LANES - s, axis=1), 0.0)
        else:
            x = x + jnp.where(lane >= s, pltpu.roll(x, s, axis=1), 0.0)
        s *= 2
    return x


def _scan_rows(src_ref, dst_ref, t, reverse, fn=None):
    groups = list(range(t // LANES))
    if reverse:
        groups = groups[::-1]
    carry = None
    for gi in groups:
        sl = slice(gi * LANES, (gi + 1) * LANES)
        blk = src_ref[:, sl]
        if fn is not None:
            blk = fn(blk)
        blk = _lane_scan(blk, reverse)
        if carry is not None:
            blk = blk + carry
        dst_ref[:, sl] = blk
        carry = blk[:, 0:1] if reverse else blk[:, LANES - 1:LANES]


def _forget_fwd(z_row, b_col):
    rows, t = z_row.shape

    def body(z_ref, b_ref, c_ref):
        def logf(z):
            zz = z + b_ref[...]
            return jnp.minimum(zz, 0.0) - jnp.log(1.0 + jnp.exp(-jnp.abs(zz)))

        _scan_rows(z_ref, c_ref, t, False, logf)

    return pl.pallas_call(
        body,
        name="forget_fwd",
        out_shape=jax.ShapeDtypeStruct((rows, t), F32),
        compiler_params=pltpu.CompilerParams(vmem_limit_bytes=VMEM_LIMIT),
    )(z_row, b_col)


def _rows_to_colb(c_row3, tq):
    heads, _, t = c_row3.shape

    def body(r_ref, o_ref):
        o_ref[...] = jnp.broadcast_to(_row_to_col(r_ref[...]), (tq, LANES))

    return pl.pallas_call(
        body,
        name="rows_to_colb",
        grid=(heads, t // tq),
        in_specs=[pl.BlockSpec((None, 1, tq), lambda h, i: (h, 0, i))],
        out_specs=pl.BlockSpec((None, tq, LANES), lambda h, i: (h, i, 0)),
        out_shape=jax.ShapeDtypeStruct((heads, t, LANES), F32),
        compiler_params=_params(("parallel", "parallel")),
    )(c_row3)


def _forget_bwd(z_row, b_col, dc_row):
    rows, t = z_row.shape

    def body(z_ref, b_ref, dc_ref, dz_ref, db_ref, tmp_ref):
        _scan_rows(dc_ref, tmp_ref, t, True)
        zz = z_ref[...] + b_ref[...]
        dz = tmp_ref[...] * (1.0 / (1.0 + jnp.exp(zz)))
        dz_ref[...] = dz.astype(dz_ref.dtype)
        db_ref[...] = jnp.sum(dz, axis=1, keepdims=True)

    return pl.pallas_call(
        body,
        name="forget_bwd",
        out_shape=[jax.ShapeDtypeStruct((rows, t), BF16), jax.ShapeDtypeStruct((rows, 1), F32)],
        scratch_shapes=[pltpu.VMEM((rows, t), F32)],
        compiler_params=pltpu.CompilerParams(vmem_limit_bytes=VMEM_LIMIT),
    )(z_row, b_col, dc_row)


def _fox_fwd(proj, off, gq, gk, c_row3, c_colb, heads, tq, rider=None):
    t = proj.shape[0]
    hd = FOX_HEAD_DIM
    tq = _tile(t, tq)
    nq = t // tq
    blk0 = off // hd
    scale = 1.0 / math.sqrt(hd)
    host = _Host(rider)

    def body(*refs):
        q_ref, k_ref, v_ref, gq_ref, gk_ref, crow_ref, ccol_ref = refs[:7]
        r_ins = refs[7:7 + host.n_in]
        o_ref, lse_ref = refs[7 + host.n_in:9 + host.n_in]
        r_outs = refs[9 + host.n_in:9 + host.n_in + host.n_out]
        khat_ref, v_t_ref = refs[9 + host.n_in + host.n_out:11 + host.n_in + host.n_out]
        sems = refs[11 + host.n_in + host.n_out:]
        h, qi = pl.program_id(0), pl.program_id(1)

        def compute():
            eye = (lax.broadcasted_iota(jnp.int32, (hd, hd), 0) == lax.broadcasted_iota(jnp.int32, (hd, hd), 1)).astype(BF16)

            @pl.when(qi == 0)
            def _():
                kn, _ = _head_rms(k_ref[...].astype(F32))
                khat_ref[...] = (kn * gk_ref[...]).astype(BF16)
                v_t_ref[...] = _dot(eye, v_ref[...], NT).astype(BF16)

            qn, _ = _head_rms(q_ref[...].astype(F32))
            qhat = (qn * (gq_ref[...] * scale)).astype(BF16)
            crow = crow_ref[:, pl.ds(pl.multiple_of(qi * tq, tq), tq)]
            above = lax.broadcasted_iota(jnp.int32, (tq, tq), 1) >= lax.broadcasted_iota(jnp.int32, (tq, tq), 0)

            def tile(j, keys, carry, diagonal):
                m, l, acc_t = carry
                ks = pl.multiple_of(j * keys, keys)
                s_t = _dot(khat_ref[pl.ds(ks, keys), :], qhat, NT) - ccol_ref[pl.ds(ks, keys), 0:1]
                if diagonal:
                    s_t = jnp.where(above, s_t, NEG)
                m_new = jnp.maximum(m, jnp.max(s_t, axis=0, keepdims=True) + crow)
                alpha = jnp.exp(m - m_new)
                p_t = jnp.exp(s_t + (crow - m_new))
                l = alpha * l + jnp.sum(p_t, axis=0, keepdims=True)
                acc_t = alpha * acc_t + _dot(v_t_ref[:, pl.ds(ks, keys)], p_t.astype(BF16), NN)
                return m_new, l, acc_t

            init = (jnp.full((1, tq), NEG, F32), jnp.zeros((1, tq), F32), jnp.zeros((hd, tq), F32))
            pairs = qi // 2
            carry = lax.fori_loop(0, pairs, lambda j, c: tile(j, 2 * tq, c, False), init)
            carry = lax.fori_loop(2 * pairs, qi, lambda j, c: tile(j, tq, c, False), carry)
            m, l, acc_t = tile(qi, tq, carry, True)
            o_ref[...] = _dot((acc_t / l).astype(BF16), eye, TN).astype(o_ref.dtype)
            lse_ref[...] = m + jnp.log(l)

        first = jnp.logical_and(h == 0, qi == 0)
        last = jnp.logical_and(h == heads - 1, qi == nq - 1)
        host.run(first, last, r_ins, r_outs, sems, compute)

    res = pl.pallas_call(
        body,
        name="fox_fwd",
        grid=(heads, nq),
        in_specs=[
            pl.BlockSpec((tq, hd), lambda h, i: (i, blk0 + 3 * h)),
            pl.BlockSpec((t, hd), lambda h, i: (0, blk0 + 3 * h + 1)),
            pl.BlockSpec((t, hd), lambda h, i: (0, blk0 + 3 * h + 2)),
            pl.BlockSpec((1, hd), lambda h, i: (0, 0)),
            pl.BlockSpec((1, hd), lambda h, i: (0, 0)),
            pl.BlockSpec((None, 1, t), lambda h, i: (h, 0, 0)),
            pl.BlockSpec((None, t, LANES), lambda h, i: (h, 0, 0)),
        ] + host.in_specs,
        out_specs=[pl.BlockSpec((tq, hd), lambda h, i: (i, h)), pl.BlockSpec((None, 1, tq), lambda h, i: (h, 0, i))] + host.out_specs,
        out_shape=[jax.ShapeDtypeStruct((t, heads * hd), BF16), jax.ShapeDtypeStruct((heads, 1, t), F32)] + host.out_shapes,
        scratch_shapes=[pltpu.VMEM((t, hd), BF16), pltpu.VMEM((hd, t), BF16)] + host.scratch,
        compiler_params=_params(("arbitrary", "arbitrary")),
    )(proj, proj, proj, gq.reshape(1, hd), gk.reshape(1, hd), c_row3, c_colb, *host.ins)
    return res


def _fox_bwd(proj, off, o, do, gq, gk, c_row3, c_colb, lse, heads, tq, dproj, rider=None):
    t = proj.shape[0]
    hd = FOX_HEAD_DIM
    tq = _tile(t, tq)
    nb = t // tq
    blk0 = off // hd
    scale = 1.0 / math.sqrt(hd)
    host = _Host(rider)
    n_fixed_in = 11

    def body(*refs):
        q_ref, k_ref, v_ref, o_ref, do_ref, gq_ref, gk_ref, crow_ref, ccol_ref, lse_ref = refs[:10]
        pos = n_fixed_in
        r_ins = refs[pos:pos + host.n_in]; pos += host.n_in
        dp_ref, dc_ref, ggq_ref, ggk_ref = refs[pos:pos + 4]; pos += 4
        r_outs = refs[pos:pos + host.n_out]; pos += host.n_out
        qhat_ref, khat_ref, khat_t_ref, dq_t_ref, dk_ref, dcq_ref, dck_ref, delta_ref = refs[pos:pos + 8]; pos += 8
        sems = refs[pos:]
        h = pl.program_id(0)

        def compute():
            qn, rq = _head_rms(q_ref[...].astype(F32))
            qhat_ref[...] = (qn * (gq_ref[...] * scale)).astype(BF16)
            kn, rk = _head_rms(k_ref[...].astype(F32))
            khat_ref[...] = (kn * gk_ref[...]).astype(BF16)
            eye = (lax.broadcasted_iota(jnp.int32, (hd, hd), 0) == lax.broadcasted_iota(jnp.int32, (hd, hd), 1)).astype(BF16)
            khat_t_ref[...] = _dot(eye, khat_ref[...], NT).astype(BF16)
            delta = jnp.sum(do_ref[...].astype(F32) * o_ref[...].astype(F32), axis=-1, keepdims=True)
            for b in range(nb):
                sl = slice(b * tq, (b + 1) * tq)
                delta_ref[:, sl] = _col_to_row(delta[sl, :])
            dq_t_ref[...] = jnp.zeros_like(dq_t_ref)
            dcq_ref[...] = jnp.zeros_like(dcq_ref)
            above = lax.broadcasted_iota(jnp.int32, (tq, tq), 1) >= lax.broadcasted_iota(jnp.int32, (tq, tq), 0)

            def kv_block(j, _):
                ks = pl.multiple_of(j * tq, tq)
                kh = khat_ref[pl.ds(ks, tq), :]
                kh_t = khat_t_ref[:, pl.ds(ks, tq)]
                vv = v_ref[pl.ds(ks, tq), :]
                ccol = ccol_ref[pl.ds(ks, tq), 0:1]

                def q_block(i, n, carry, diagonal):
                    dk, dv, dck = carry
                    qs = pl.multiple_of(i * tq, tq)
                    qh = qhat_ref[pl.ds(qs, n), :]
                    dob = do_ref[pl.ds(qs, n), :]
                    s_t = _dot(kh, qh, NT) + ((crow_ref[:, pl.ds(qs, n)] - lse_ref[:, pl.ds(qs, n)]) - ccol)
                    p_t = jnp.exp(s_t)
                    if diagonal:
                        p_t = jnp.where(above, p_t, 0.0)
                    ds_t = p_t * (_dot(vv, dob, NT) - delta_ref[:, pl.ds(qs, n)])
                    dsb = ds_t.astype(BF16)
                    dv = dv + _dot(p_t.astype(BF16), dob, NN)
                    dk = dk + _dot(dsb, qh, NN)
                    dq_t_ref[:, pl.ds(qs, n)] += _dot(kh_t, dsb, NN)
                    dcq_ref[:, pl.ds(qs, n)] += jnp.sum(ds_t, axis=0, keepdims=True)
                    dck = dck + jnp.sum(ds_t, axis=-1, keepdims=True)
                    return dk, dv, dck

                zero = jnp.zeros((tq, hd), F32)
                carry = q_block(j, tq, (zero, zero, jnp.zeros((tq, 1), F32)), True)
                pairs = (nb - 1 - j) // 2
                carry = lax.fori_loop(0, pairs, lambda p, c: q_block(j + 1 + 2 * p, 2 * tq, c, False), carry)
                dk, dv, dck = lax.fori_loop(j + 1 + 2 * pairs, nb, lambda i, c: q_block(i, tq, c, False), carry)
                dk_ref[pl.ds(ks, tq), :] = dk
                dp_ref[pl.ds(ks, tq), 2 * hd:3 * hd] = dv.astype(dp_ref.dtype)
                dck_ref[pl.ds(ks, tq), :] = dck
                return 0

            lax.fori_loop(0, nb, kv_block, 0)

            dq, ggq = _head_rms_bwd(dq_t_ref[...].T * scale, qn, rq, gq_ref[...])
            dk, ggk = _head_rms_bwd(dk_ref[...], kn, rk, gk_ref[...])
            dp_ref[:, 0:hd] = dq.astype(dp_ref.dtype)
            dp_ref[:, hd:2 * hd] = dk.astype(dp_ref.dtype)
            for b in range(nb):
                sl = slice(b * tq, (b + 1) * tq)
                dc_ref[:, sl] = dcq_ref[:, sl] - _col_to_row(dck_ref[sl, :])

            @pl.when(h == 0)
            def _():
                ggq_ref[...] = jnp.zeros_like(ggq_ref)
                ggk_ref[...] = jnp.zeros_like(ggk_ref)

            ggq_ref[...] += ggq
            ggk_ref[...] += ggk

        host.run(h == 0, h == heads - 1, r_ins, r_outs, sems, compute)

    head_in = lambda part: pl.BlockSpec((t, hd), lambda h: (0, blk0 + 3 * h + part))
    vec = pl.BlockSpec((1, hd), lambda h: (0, 0))
    colb = pl.BlockSpec((None, t, LANES), lambda h: (h, 0, 0))
    res = pl.pallas_call(
        body,
        name="fox_bwd",
        grid=(heads,),
        in_specs=[
            head_in(0), head_in(1), head_in(2),
            pl.BlockSpec((t, hd), lambda h: (0, h)),
            pl.BlockSpec((t, hd), lambda h: (0, h)),
            vec, vec,
            pl.BlockSpec((None, 1, t), lambda h: (h, 0, 0)),
            colb,
            pl.BlockSpec((None, 1, t), lambda h: (h, 0, 0)),
            ANY,
        ] + host.in_specs,
        out_specs=[
            pl.BlockSpec((t, 3 * hd), lambda h: (0, blk0 // 3 + h)),
            pl.BlockSpec((None, 1, t), lambda h: (h, 0, 0)),
            vec, vec,
        ] + host.out_specs,
        out_shape=[
            jax.ShapeDtypeStruct(dproj.shape, dproj.dtype),
            jax.ShapeDtypeStruct((heads, 1, t), F32),
            jax.ShapeDtypeStruct((1, hd), F32),
            jax.ShapeDtypeStruct((1, hd), F32),
        ] + host.out_shapes,
        input_output_aliases={10: 0},
        scratch_shapes=[
            pltpu.VMEM((t, hd), BF16), pltpu.VMEM((t, hd), BF16), pltpu.VMEM((hd, t), BF16),
            pltpu.VMEM((hd, t), F32), pltpu.VMEM((t, hd), F32),
            pltpu.VMEM((1, t), F32), pltpu.VMEM((t, 1), F32), pltpu.VMEM((1, t), F32),
        ] + host.scratch,
        compiler_params=_params(("arbitrary",)),
    )(proj, proj, proj, o, do, gq.reshape(1, hd), gk.reshape(1, hd), c_row3, c_colb, lse, dproj, *host.ins)
    return res


def _mem_fwd(proj, off, kv, gq, gk, tq):
    t = proj.shape[0]
    m, width = kv.shape[0], kv.shape[1] // 2
    hd = width // MEM_HEADS
    tq = _tile(t, tq)
    blk0 = off // hd
    scale = 1.0 / math.sqrt(hd)

    def body(q_ref, k_ref, v_ref, gq_ref, gk_ref, o_ref):
        qn, _ = _head_rms(q_ref[...].astype(F32))
        kn, _ = _head_rms(k_ref[...])
        s = _dot((qn * gq_ref[...]).astype(BF16), (kn * gk_ref[...]).astype(BF16), NT) * scale
        p = jnp.exp(s - jnp.max(s, axis=-1, keepdims=True))
        p = p / jnp.sum(p, axis=-1, keepdims=True)
        o_ref[...] = _dot(p.astype(BF16), v_ref[...].astype(BF16), NN).astype(o_ref.dtype)

    vec = pl.BlockSpec((1, hd), lambda h, i: (0, 0))
    return pl.pallas_call(
        body,
        name="mem_fwd",
        grid=(MEM_HEADS, t // tq),
        in_specs=[
            pl.BlockSpec((tq, hd), lambda h, i: (i, blk0 + h)),
            pl.BlockSpec((m, hd), lambda h, i: (0, h)),
            pl.BlockSpec((m, hd), lambda h, i: (0, MEM_HEADS + h)),
            vec, vec,
        ],
        out_specs=pl.BlockSpec((tq, hd), lambda h, i: (i, h)),
        out_shape=jax.ShapeDtypeStruct((t, width), BF16),
        compiler_params=_params(("parallel", "parallel")),
    )(proj, kv, kv, gq.reshape(1, hd), gk.reshape(1, hd))


def _mem_bwd(proj, off, kv, do, gq, gk, tq, dproj, rider=None):
    t = proj.shape[0]
    m, width = kv.shape[0], kv.shape[1] // 2
    hd = width // MEM_HEADS
    tq = _tile(t, tq)
    nq = t // tq
    blk0 = off // hd
    scale = 1.0 / math.sqrt(hd)
    host = _Host(rider)

    def body(*refs):
        q_ref, k_ref, v_ref, do_ref, gq_ref, gk_ref = refs[:6]
        pos = 7
        r_ins = refs[pos:pos + host.n_in]; pos += host.n_in
        dq_ref, dk_ref, dv_ref, ggq_ref, ggk_ref = refs[pos:pos + 5]; pos += 5
        r_outs = refs[pos:pos + host.n_out]; pos += host.n_out
        dkh_ref, dvh_ref = refs[pos:pos + 2]; pos += 2
        sems = refs[pos:]
        h, i = pl.program_id(0), pl.program_id(1)

        def compute():
            qn, rq = _head_rms(q_ref[...].astype(F32))
            kn, rk = _head_rms(k_ref[...])
            qhat = (qn * gq_ref[...]).astype(BF16)
            khat = (kn * gk_ref[...]).astype(BF16)
            vb = v_ref[...].astype(BF16)
            dob = do_ref[...]
            s = _dot(qhat, khat, NT) * scale
            p = jnp.exp(s - jnp.max(s, axis=-1, keepdims=True))
            p = p / jnp.sum(p, axis=-1, keepdims=True)
            dp = _dot(dob, vb, NT)
            ds = p * (dp - jnp.sum(dp * p, axis=-1, keepdims=True))
            dsb = ds.astype(BF16)
            dq, ggq = _head_rms_bwd(_dot(dsb, khat, NN) * scale, qn, rq, gq_ref[...])
            dq_ref[...] = dq.astype(dq_ref.dtype)

            @pl.when(i == 0)
            def _():
                dkh_ref[...] = jnp.zeros_like(dkh_ref)
                dvh_ref[...] = jnp.zeros_like(dvh_ref)

            @pl.when(jnp.logical_and(h == 0, i == 0))
            def _():
                ggq_ref[...] = jnp.zeros_like(ggq_ref)
                ggk_ref[...] = jnp.zeros_like(ggk_ref)

            dkh_ref[...] += _dot(dsb, qhat, TN)
            dvh_ref[...] += _dot(p.astype(BF16), dob, TN)
            ggq_ref[...] += ggq

            @pl.when(i == nq - 1)
            def _():
                dk, ggk = _head_rms_bwd(dkh_ref[...] * scale, kn, rk, gk_ref[...])
                dk_ref[...] = dk.astype(dk_ref.dtype)
                dv_ref[...] = dvh_ref[...].astype(dv_ref.dtype)
                ggk_ref[...] += ggk

        first = jnp.logical_and(h == 0, i == 0)
        last = jnp.logical_and(h == MEM_HEADS - 1, i == nq - 1)
        host.run(first, last, r_ins, r_outs, sems, compute)

    vec = pl.BlockSpec((1, hd), lambda h, i: (0, 0))
    kblk = pl.BlockSpec((m, hd), lambda h, i: (0, h))
    res = pl.pallas_call(
        body,
        name="mem_bwd",
        grid=(MEM_HEADS, nq),
        in_specs=[
            pl.BlockSpec((tq, hd), lambda h, i: (i, blk0 + h)), kblk,
            pl.BlockSpec((m, hd), lambda h, i: (0, MEM_HEADS + h)),
            pl.BlockSpec((tq, hd), lambda h, i: (i, h)), vec, vec, ANY,
        ] + host.in_specs,
        out_specs=[pl.BlockSpec((tq, hd), lambda h, i: (i, blk0 + h)), kblk, kblk, vec, vec] + host.out_specs,
        out_shape=[
            jax.ShapeDtypeStruct(dproj.shape, dproj.dtype),
            jax.ShapeDtypeStruct((m, width), BF16),
            jax.ShapeDtypeStruct((m, width), BF16),
            jax.ShapeDtypeStruct((1, hd), F32),
            jax.ShapeDtypeStruct((1, hd), F32),
        ] + host.out_shapes,
        input_output_aliases={6: 0},
        scratch_shapes=[pltpu.VMEM((m, hd), F32), pltpu.VMEM((m, hd), F32)] + host.scratch,
        compiler_params=_params(("arbitrary", "arbitrary")),
    )(proj, kv, kv, do, gq.reshape(1, hd), gk.reshape(1, hd), dproj, *host.ins)
    dproj, dk, dv, ggq, ggk = res[:5]
    return (dproj, jnp.concatenate([dk, dv], axis=1), ggq.reshape(hd), ggk.reshape(hd), *res[5:])


def _sigmoid(z):
    return 1.0 / (1.0 + jnp.exp(-z))


def _merge_fwd(proj, o3, tm, tc):
    t, d = o3[0].shape
    tm = _tile(t, tm)

    def body(g_ref, oa_ref, ob_ref, oc_ref, out_ref):
        acc = jnp.zeros((tm, tc), F32)
        for s, o_ref in enumerate((oa_ref, ob_ref, oc_ref)):
            acc = acc + _sigmoid(g_ref[:, s * tc:(s + 1) * tc].astype(F32)) * o_ref[...].astype(F32)
        out_ref[...] = acc.astype(out_ref.dtype)

    blk = pl.BlockSpec((tm, tc), lambda i, j: (i, j))
    return pl.pallas_call(
        body,
        name="merge_fwd",
        grid=(t // tm, d // tc),
        in_specs=[pl.BlockSpec((tm, 3 * tc), lambda i, j: (i, j)), blk, blk, blk],
        out_specs=blk,
        out_shape=jax.ShapeDtypeStruct((t, d), BF16),
        compiler_params=_params(("parallel", "parallel")),
    )(proj, *o3)


def _merge_bwd(proj, o3, dx1, w_out, tm, tc):
    t, d = o3[0].shape
    k = dx1.shape[1]
    tm = _tile(t, tm)

    def body(dx_ref, w_ref, g_ref, oa_ref, ob_ref, oc_ref, dg_ref, da_ref, db_ref, dc_ref):
        dmf = _dot(dx_ref[...], w_ref[...], NT)
        for s, (o_ref, do_ref) in enumerate(((oa_ref, da_ref), (ob_ref, db_ref), (oc_ref, dc_ref))):
            g = _sigmoid(g_ref[:, s * tc:(s + 1) * tc].astype(F32))
            do_ref[...] = (dmf * g).astype(do_ref.dtype)
            dg_ref[:, s * tc:(s + 1) * tc] = (dmf * o_ref[...].astype(F32) * g * (1.0 - g)).astype(dg_ref.dtype)

    blk = pl.BlockSpec((tm, tc), lambda i, j: (i, j))
    wide = pl.BlockSpec((tm, 3 * tc), lambda i, j: (i, j))
    return pl.pallas_call(
        body,
        name="merge_bwd",
        grid=(t // tm, d // tc),
        in_specs=[pl.BlockSpec((tm, k), lambda i, j: (i, 0)), pl.BlockSpec((tc, k), lambda i, j: (j, 0)), wide, blk, blk, blk],
        out_specs=[wide, blk, blk, blk],
        out_shape=[jax.ShapeDtypeStruct(proj.shape, BF16)] + [jax.ShapeDtypeStruct((t, d), BF16)] * 3,
        compiler_params=_params(("parallel", "parallel")),
    )(dx1, w_out, proj, *o3)


def _w_in_chunks(d, tc):
    cw = d // 2
    heads = cw // FOX_HEAD_DIM
    conv0, fox0, f0, mq0, gate0 = 0, 3 * cw, 6 * cw, 6 * cw + heads, 7 * cw + heads
    chunks = [(gate0 + s * d + j * tc, gate0 + s * d + (j + 1) * tc) for j in range(d // tc) for s in range(N_BRANCHES)]
    chunks += [(conv0 + s * cw + j * LANES, conv0 + s * cw + (j + 1) * LANES) for j in range(cw // LANES) for s in range(3)]
    chunks += [(fox0 + s * cw + j * FOX_HEAD_DIM, fox0 + s * cw + (j + 1) * FOX_HEAD_DIM) for j in range(heads) for s in range(3)]
    chunks.append((mq0, mq0 + cw))
    return chunks, (f0, f0 + heads)


ROW_TILE = 16
GROUP = 128
GROUP_BACK = 112
SCRATCH_ROWS = 2 * GROUP + 32


def _padded_rows(r):
    return -(-r // GROUP_BACK) * GROUP_BACK


def _rows_from(scr_ref, y_ref, q8, fine, g):
    x = scr_ref[pl.ds(pl.multiple_of(q8 * 8, 8), g + 8), :]
    for s in range(8):
        @pl.when(fine == s)
        def _(s=s):
            y_ref[...] = (x if s == 0 else pltpu.roll(x, g + 8 - s, axis=0))[0:g]


def _assemble(name, tbl, grid, step, in_specs, out_spec, out_shape, operands, g, w1, cols_of):
    has_f = len(in_specs) == 3
    k = out_shape.shape[-1]
    c = cols_of

    def body(*refs):
        t_ref, s1_ref, s2_ref = refs[:3]
        f_ref = refs[3] if has_f else None
        out_ref = refs[3 + has_f]
        scr1, scr2, scrf, y_ref = refs[4 + has_f:]
        t = step()

        @pl.when(t == 0)
        def _():
            scr1[...] = jnp.zeros_like(scr1)
            scr2[...] = jnp.zeros_like(scr2)
            scrf[...] = jnp.zeros_like(scrf)

        rows = lax.broadcasted_iota(jnp.int32, (g, k), 0)
        n1, a2 = t_ref[c["n1"], t], t_ref[c["a2"], t]
        scr1[0:w1, :] = (s1_ref[0] if len(s1_ref.shape) == 3 else s1_ref[...]).astype(F32)
        _rows_from(scr1, y_ref, t_ref[c["q1"], t], t_ref[c["s1"], t], g)
        out_ref[...] = y_ref[...].astype(out_ref.dtype)

        @pl.when(a2 < g)
        def _():
            scr2[g:g + s2_ref.shape[0], :] = s2_ref[...].astype(F32)
            _rows_from(scr2, y_ref, t_ref[c["q2"], t], t_ref[c["s2"], t], g)
            out_ref[...] = jnp.where(rows < n1, out_ref[...].astype(F32), y_ref[...]).astype(out_ref.dtype)

        if has_f:
            fa, fb = t_ref[c["fa"], t], t_ref[c["fb"], t]

            @pl.when(fb > fa)
            def _():
                scrf[g:g + f_ref.shape[0], :] = f_ref[...].astype(F32)
                _rows_from(scrf, y_ref, t_ref[c["qf"], t], t_ref[c["sf"], t], g)
                inside = jnp.logical_and(rows >= fa, rows < fb)
                out_ref[...] = jnp.where(inside, y_ref[...], out_ref[...].astype(F32)).astype(out_ref.dtype)

            valid = t_ref[c["valid"], t]

            @pl.when(valid < g)
            def _():
                out_ref[...] = jnp.where(rows < valid, out_ref[...].astype(F32), 0.0).astype(out_ref.dtype)

    return pl.pallas_call(
        body,
        name=name,
        grid_spec=pltpu.PrefetchScalarGridSpec(
            num_scalar_prefetch=1, grid=grid, in_specs=in_specs, out_specs=out_spec,
            scratch_shapes=[pltpu.VMEM((SCRATCH_ROWS, k), F32)] * 3 + [pltpu.VMEM((g, k), F32)]),
        out_shape=out_shape,
        compiler_params=_params(("arbitrary",) * len(grid)),
    )(jnp.asarray(tbl), *operands)


def _pack_w_in(w8, d, tc):
    blocks, rp, k = w8.shape
    chunks, (f_lo, f_hi) = _w_in_chunks(d, tc)
    r = max(hi for _, hi in chunks) // blocks
    g, w1 = GROUP, GROUP + ROW_TILE
    table = []
    for lo, hi in chunks:
        for g0 in range(lo, hi, g):
            b1, r1 = divmod(g0, r)
            n1 = min(g, r - r1)
            st1 = min(r1 // ROW_TILE * ROW_TILE, rp - w1)
            o1, o2 = r1 - st1, g - n1
            b2 = b1 + 1 if n1 < g else 0
            table.append((b1, st1, o1 // 8, o1 % 8, n1, n1, b2, o2 // 8, o2 % 8))
    names = ("b1", "st1", "q1", "s1", "n1", "a2", "b2", "q2", "s2")
    cols_of = {n: i for i, n in enumerate(names)}
    tbl = np.array(table, np.int32).T
    c = cols_of
    w_all = _assemble(
        "pack_w_in", tbl, (len(table),), lambda: pl.program_id(0),
        [pl.BlockSpec((pl.Element(1), pl.Element(w1), pl.Element(k)), lambda i, t: (t[c["b1"], i], pl.multiple_of(t[c["st1"], i], ROW_TILE), 0)),
         pl.BlockSpec((None, g, k), lambda i, t: (t[c["b2"], i], 0, 0))],
        pl.BlockSpec((g, k), lambda i, t: (i, 0)),
        jax.ShapeDtypeStruct((len(table) * g, k), w8.dtype), [w8, w8], g, w1, cols_of)
    fb, fr = divmod(f_lo, r)
    return w_all, jnp.pad(w8[fb, fr:fr + f_hi - f_lo], ((0, F_ROWS - (f_hi - f_lo)), (0, 0)))


def _unpack_g_in(g_all, g_f, d, tc, blocks):
    n_all, k = g_all.shape
    chunks, (f_lo, f_hi) = _w_in_chunks(d, tc)
    r = max(hi for _, hi in chunks) // blocks
    rp = _padded_rows(r)
    g, w1 = GROUP_BACK, GROUP_BACK + ROW_TILE
    pos, spans = 0, [(f_lo, f_hi, None)]
    for lo, hi in chunks:
        spans.append((lo, hi, pos))
        pos += hi - lo
    spans.sort()
    table = []
    for b in range(blocks):
        for l0 in range(0, rp, g):
            valid = max(0, min(g, r - l0))
            g0, segs, fa, fb, of = b * r + l0, [], 0, 0, 0
            for lo, hi, p in spans:
                a, e = max(lo, g0), min(hi, g0 + valid)
                if a < e and p is None:
                    fa, fb, of = a - g0, e - g0, g + (a - lo) - (a - g0)
                elif a < e:
                    segs.append((a - g0, p + a - lo, e - a))
            assert len(segs) <= 2 and (not segs or segs[0][0] == 0 or len(segs) == 1)
            first = segs[0] if segs and segs[0][0] == 0 else (0, 0, 0)
            second = segs[-1] if segs and segs[-1][0] > 0 else (g, 0, 0)
            st1 = min(first[1] // ROW_TILE * ROW_TILE, n_all - w1)
            o1, o2 = first[1] - st1, g - second[0]
            assert second[1] % GROUP == 0
            table.append((st1, o1 // 8, o1 % 8, first[2], second[0], second[1] // GROUP, o2 // 8, o2 % 8,
                          fa, fb, of // 8, of % 8, valid))
    names = ("st1", "q1", "s1", "n1", "a2", "j2", "q2", "s2", "fa", "fb", "qf", "sf", "valid")
    cols_of = {n: i for i, n in enumerate(names)}
    tbl = np.array(table, np.int32).T
    c, per = cols_of, rp // g
    return _assemble(
        "unpack_g_in", tbl, (blocks, per), lambda: pl.program_id(0) * per + pl.program_id(1),
        [pl.BlockSpec((pl.Element(w1), pl.Element(k)), lambda b, u, t: (pl.multiple_of(t[c["st1"], b * per + u], ROW_TILE), 0)),
         pl.BlockSpec((GROUP, k), lambda b, u, t: (t[c["j2"], b * per + u], 0)),
         pl.BlockSpec((F_ROWS, k), lambda b, u, t: (0, 0))],
        pl.BlockSpec((None, g, k), lambda b, u, t: (b, u, 0)),
        jax.ShapeDtypeStruct((blocks, rp, k), g_all.dtype), [g_all, g_all, g_f], g, w1, cols_of)


def _unblock(w8):
    return w8.transpose(1, 0, 2).reshape(w8.shape[1], -1)


def _tile2(r, cols, tr, tcols):
    if r % 8 == 0:
        return _tile(r, tr), cols
    return r, _tile(cols, tcols)


def _pair_sum(name, g8, got, c):
    def body(c_ref, g_ref, s_ref, o_ref):
        o_ref[...] = (g_ref[...].astype(F32) + s_ref[...].astype(F32)).astype(o_ref.dtype)

    if g8.ndim == 4:
        _, r, k1, k2 = g8.shape
        tr = max(cand for cand in range(1, 385) if r % cand == 0)
        grid = (N_CHIPS, r // tr)
        shape = (None, tr, k1, k2)
        own = pl.BlockSpec(shape, lambda q, i, c_ref: (2 * q + c_ref[0], i, 0, 0))
        blk = pl.BlockSpec(shape, lambda q, i, c_ref: (q, i, 0, 0))
    else:
        _, r, cols = g8.shape
        tr, tcols = _tile2(r, cols, 256, 256)
        grid = (N_CHIPS, r // tr, cols // tcols)
        own = pl.BlockSpec((None, tr, tcols), lambda q, i, j, c_ref: (2 * q + c_ref[0], i, j))
        blk = pl.BlockSpec((None, tr, tcols), lambda q, i, j, c_ref: (q, i, j))
    return pl.pallas_call(
        body,
        name=name,
        grid_spec=pltpu.PrefetchScalarGridSpec(num_scalar_prefetch=1, grid=grid, in_specs=[own, blk], out_specs=blk),
        out_shape=jax.ShapeDtypeStruct((N_CHIPS,) + g8.shape[1:], BF16),
        compiler_params=_params(("parallel",) * len(grid)),
    )(c, g8, got)


def _local_step(x, mem, target, w, small, comm=None):
    t, d = x.shape
    cw = d // 2
    heads = cw // FOX_HEAD_DIM
    tc = min(512, d)
    tq = min(512, t)
    off_conv, off_fox, off_mq = 3 * d, 3 * d + 3 * cw, 3 * d + 6 * cw
    w = dict(w)
    w_all, w_f = _pack_w_in(w["w_in"], d, tc)
    big = dict(tm=1024, tn=512, tk=2048)
    wide_k = dict(tm=512, tn=1024, tk=4096)
    tall = dict(tm=2048, tn=512, tk=2048)

    h = _rms_fwd("rms1_fwd", x, small["norm1_g"])
    if comm:
        early = ("w_conv_out", "w_fox_out", "w_mem_out", "w_out", "w_mem_kv")
        proj, *got = _matmul("proj", "nt", h, w_all, outs=[BF16], rider=_gather_rider([comm["shards"][n] for n in early], False), **tall)
        for n, val in zip(early, got):
            w[n] = _unblock(val) if n in COLUMN_SPLIT else val.reshape(-1, val.shape[-1])
    else:
        proj = _matmul("proj", "nt", h, w_all, outs=[BF16], **tall)
    z_row = _matmul("proj_f", "nt", w_f, h, outs=[F32], tm=F_ROWS, tn=512, tk=2048)

    y_conv = _conv_fwd(proj, off_conv, small["conv_w"], LANES)

    b_col = jnp.pad(small["b_f"], (0, F_ROWS - heads)).reshape(F_ROWS, 1)
    c_row3 = _forget_fwd(z_row, b_col)[:heads].reshape(heads, 1, t)
    c_colb = _rows_to_colb(c_row3, tq)
    if comm:
        y_fox, lse, got = _fox_fwd(proj, off_fox, small["fox_q_g"], small["fox_k_g"], c_row3, c_colb, heads, tq,
                                   rider=_gather_rider([comm["shards"]["w_up"]], False))
        w["w_up"] = _unblock(got)
    else:
        y_fox, lse = _fox_fwd(proj, off_fox, small["fox_q_g"], small["fox_k_g"], c_row3, c_colb, heads, tq)

    nm = _rms_fwd("mem_rms_fwd", mem, small["mem_norm_g"])
    kv = _matmul("mem_kv", "nn", nm, w["w_mem_kv"], outs=[F32], tm=256, tn=512, tk=2048)
    y_mem = _mem_fwd(proj, off_mq, kv, small["mem_q_g"], small["mem_k_g"], tq)

    ys = (y_conv, y_fox, y_mem)
    w_outs = (w["w_conv_out"], w["w_fox_out"], w["w_mem_out"])
    o3 = [_matmul(f"branch_out{s}", "nn", ys[s], w_outs[s], outs=[BF16], **big) for s in range(3)]
    merged = _merge_fwd(proj, o3, 512, tc)
    x1 = _matmul("out_proj", "nn", merged, w["w_out"], outs=[F32], extras=[x],
                 epilogue=lambda acc, xr: (acc + xr,), **big)
    h2 = _rms_fwd("rms2_fwd", x1, small["norm2_g"])

    def up_epilogue(acc):
        return acc, jnp.square(jnp.maximum(acc, 0.0))

    if comm:
        up, act, got = _matmul("mlp_up", "nn", h2, w["w_up"], outs=[BF16, BF16], epilogue=up_epilogue,
                               rider=_gather_rider([comm["shards"]["w_down"]], True), **big)
        w["w_down"] = got.reshape(-1, got.shape[-1])
    else:
        up, act = _matmul("mlp_up", "nn", h2, w["w_up"], outs=[BF16, BF16], epilogue=up_epilogue, **big)

    def loss_epilogue(acc, x1r, tr):
        dy = (acc + x1r - tr) * (1.0 / d)
        return dy, dy

    dy, dyb = _matmul("mlp_down", "nn", act, w["w_down"], outs=[F32, BF16], extras=[x1, target],
                      epilogue=loss_epilogue, tm=1024, tn=512, tk=4096)

    def dup_epilogue(acc, upr):
        return (acc * 2.0 * jnp.maximum(upr.astype(F32), 0.0),)

    def by_owner(g):
        return g.reshape(N_DEV, -1, g.shape[-1])

    g, parts = {}, {}
    g["w_down"] = _matmul("d_w_down", "tn", act, dyb, outs=[BF16], **wide_k)
    if comm:
        dup = _matmul("d_act", "nt", dyb, w["w_down"], outs=[BF16], extras=[up], epilogue=dup_epilogue, **tall)
        g["w_up"], got = _matmul("d_w_up", "tn", h2, dup, outs=[BF16], out_blocks=True,
                                 rider=_pair_rider([by_owner(g["w_down"])]), **wide_k)
        pair = _pair_sum("pair_w_down", by_owner(g["w_down"]), got, comm["c"])
        dh2, parts["w_down"], got = _matmul("d_h2", "nt", dup, w["w_up"], outs=[F32],
                                            rider=_join_riders(_chip_rider([pair]), _pair_rider([g["w_up"]])), **tall)
        pair_up = _pair_sum("pair_w_up", g["w_up"], got, comm["c"])
    else:
        dup = _matmul("d_act", "nt", dyb, w["w_down"], outs=[BF16], extras=[up], epilogue=dup_epilogue, **tall)
        g["w_up"] = _matmul("d_w_up", "tn", h2, dup, outs=[BF16], out_blocks=True, **wide_k)
        dh2 = _matmul("d_h2", "nt", dup, w["w_up"], outs=[F32], **tall)
    dx1, dx1b, g_norm2, dy_sq = _rms_bwd("rms2_bwd", dh2, x1, small["norm2_g"], res=dy)
    loss = dy_sq * (0.5 * d)

    g["w_out"] = _matmul("d_w_out", "tn", merged, dx1b, outs=[BF16], **wide_k)
    dproj, *do3 = _merge_bwd(proj, o3, dx1b, w["w_out"], 1024, tc)
    names = ("w_conv_out", "w_fox_out", "w_mem_out")
    dys = []
    for s in range(3):
        g[names[s]] = _matmul(f"d_w_branch{s}", "tn", ys[s], do3[s], outs=[BF16], out_blocks=True, **wide_k)
        dys.append(_matmul(f"d_branch{s}", "nt", do3[s], w_outs[s], outs=[BF16], **tall))

    dproj, dkv, g_mq, g_mk = _mem_bwd(proj, off_mq, kv, dys[2], small["mem_q_g"], small["mem_k_g"], tq, dproj)
    g["w_mem_kv"] = _matmul("d_w_mem_kv", "tn", nm, dkv, outs=[BF16], **wide_k)
    dnm = _matmul("d_mem_norm", "nt", dkv, w["w_mem_kv"], outs=[F32], tm=256, tn=512, tk=2048)
    _, _, g_mem_norm, _ = _rms_bwd("mem_rms_bwd", dnm, mem, small["mem_norm_g"])

    mid = ("w_out", "w_conv_out", "w_fox_out", "w_mem_out", "w_mem_kv")
    if comm:
        mid8 = [g[n] if n in names else by_owner(g[n]) for n in mid]
        dproj, g_conv_w, *got = _conv_bwd(proj, off_conv, small["conv_w"], dys[0], LANES, dproj, rider=_pair_rider(mid8))
        pairs_mid = [_pair_sum("pair_" + n, g8, s4, comm["c"]) for n, g8, s4 in zip(mid, mid8, got)]
        dproj, dc, g_fq, g_fk, parts["w_up"] = _fox_bwd(proj, off_fox, y_fox, dys[1], small["fox_q_g"], small["fox_k_g"], c_row3,
                                                        c_colb, lse, heads, tq, dproj, rider=_chip_rider([pair_up]))
    else:
        dproj, g_conv_w = _conv_bwd(proj, off_conv, small["conv_w"], dys[0], LANES, dproj)
        dproj, dc, g_fq, g_fk = _fox_bwd(proj, off_fox, y_fox, dys[1], small["fox_q_g"], small["fox_k_g"], c_row3, c_colb,
                                         lse, heads, tq, dproj)
    dc_row = jnp.pad(dc.reshape(heads, t), ((0, F_ROWS - heads), (0, 0)))
    dz_row, db = _forget_bwd(z_row, b_col, dc_row)

    if comm:
        g_all, *got = _matmul("d_w_in", "tn", dproj, h, outs=[BF16], j_outer=True, rider=_chip_rider(pairs_mid), **wide_k)
        parts.update(zip(mid, got))
    else:
        g_all = _matmul("d_w_in", "tn", dproj, h, outs=[BF16], j_outer=True, **wide_k)
    g_wf = _matmul("d_w_f", "nn", dz_row, h, outs=[BF16], tm=F_ROWS, tn=512, tk=4096)
    g["w_in"] = _unpack_g_in(g_all, g_wf, d, tc, w["w_in"].shape[0])
    dh = _matmul("d_h_f", "tn", dz_row, w_f, outs=[F32], tm=1024, tn=512, tk=F_ROWS)
    add_prev = lambda acc, prev: (acc + prev,)
    if comm:
        g_in8 = g["w_in"]
        got = _run_rider("pair_exchange_w_in", _pair_rider([g_in8]))[0]
        pair = _pair_sum("pair_w_in", g_in8, got, comm["c"])
        dh, parts["w_in"] = _matmul("d_h", "nn", dproj, w_all, outs=[F32], extras=[dh], epilogue=add_prev,
                                    rider=_chip_rider([pair]), tm=1024, tn=512, tk=3328)
    else:
        dh = _matmul("d_h", "nn", dproj, w_all, outs=[F32], extras=[dh], epilogue=add_prev, tm=1024, tn=512, tk=3328)
    grad_x, _, g_norm1, _ = _rms_bwd("rms1_bwd", dh, x, small["norm1_g"], res=dx1)

    gs = dict(norm1_g=g_norm1, b_f=db[:heads, 0], conv_w=g_conv_w, fox_q_g=g_fq.reshape(-1), fox_k_g=g_fk.reshape(-1),
              mem_norm_g=g_mem_norm, mem_q_g=g_mq, mem_k_g=g_mk, norm2_g=g_norm2)
    return loss, grad_x, (parts if comm else g), gs


def _adamw_math(w, g, m, v):
    m = ADAM_B1 * m + (1.0 - ADAM_B1) * g
    v = ADAM_B2 * v + (1.0 - ADAM_B2) * jnp.square(g)
    m_hat = m / (1.0 - ADAM_B1 ** ADAM_STEP)
    v_hat = v / (1.0 - ADAM_B2 ** ADAM_STEP)
    delta = -ADAM_LR * (m_hat / (jnp.sqrt(v_hat) + ADAM_EPS) + ADAM_WD * w)
    return delta, m, v


def _adamw(name, parts, w, m, v):
    r, c = w.shape
    n_parts, rp = parts.shape[:2]
    if rp == r:
        tr, tc = _tile2(r, c, 128, 256)
    else:
        tr, tc = _tile(rp, 256), _tile(c, 1024)

    def body(p_ref, w_ref, m_ref, v_ref, g_ref, d_ref, nm_ref, nv_ref):
        g = p_ref[0].astype(F32)
        for s in range(1, n_parts):
            g = g + p_ref[s].astype(F32)
        delta, nm, nv = _adamw_math(w_ref[...], g, m_ref[...], v_ref[...])
        g_ref[...] = g
        d_ref[...] = delta
        nm_ref[...] = nm
        nv_ref[...] = nv

    blk = pl.BlockSpec((tr, tc), lambda i, j: (i, j))
    return pl.pallas_call(
        body,
        name=name,
        grid=(rp // tr, c // tc),
        in_specs=[pl.BlockSpec((n_parts, tr, tc), lambda i, j: (0, i, j)), blk, blk, blk],
        out_specs=[blk] * 4,
        out_shape=[jax.ShapeDtypeStruct((r, c), F32)] * 4,
        compiler_params=_params(("parallel", "parallel")),
    )(parts, w, m, v)


def _sum_parts(name, parts):
    n_parts, r, c = parts.shape

    def body(p_ref, o_ref):
        acc = p_ref[0]
        for s in range(1, n_parts):
            acc = acc + p_ref[s]
        o_ref[...] = acc

    return pl.pallas_call(body, name=name, out_shape=jax.ShapeDtypeStruct((r, c), F32))(parts)


BIG = ("w_in", "w_mem_kv", "w_conv_out", "w_fox_out", "w_mem_out", "w_out", "w_up", "w_down")
COLUMN_SPLIT = ("w_in", "w_conv_out", "w_fox_out", "w_mem_out", "w_up")
SMALL = ("norm1_g", "b_f", "conv_w", "fox_q_g", "fox_k_g", "mem_norm_g", "mem_q_g", "mem_k_g", "norm2_g")
WEIGHTS = ("norm1_g", "w_in", "b_f", "conv_w", "fox_q_g", "fox_k_g", "mem_norm_g", "w_mem_kv", "mem_q_g", "mem_k_g",
           "w_conv_out", "w_fox_out", "w_mem_out", "w_out", "norm2_g", "w_up", "w_down")


def _pack(vectors):
    rows = []
    for vec in vectors:
        n = vec.shape[0]
        rows.append(jnp.pad(vec, (0, -n % LANES)).reshape(-1, LANES))
    out = jnp.concatenate(rows, axis=0)
    return jnp.pad(out, ((0, -out.shape[0] % 8), (0, 0)))


def _unpack(packed, sizes):
    out, row = [], 0
    for n in sizes:
        nr = -(-n // LANES)
        out.append(packed[row:row + nr].reshape(-1)[:n])
        row += nr
    return out


def kernel(x, mem, norm1_g, w_in, b_f, conv_w, fox_q_g, fox_k_g, mem_norm_g, w_mem_kv, mem_q_g, mem_k_g, w_conv_out, w_fox_out, w_mem_out, w_out, norm2_g, w_up, w_down, loss_target, m_norm1_g, m_w_in, m_b_f, m_conv_w, m_fox_q_g, m_fox_k_g, m_mem_norm_g, m_w_mem_kv, m_mem_q_g, m_mem_k_g, m_w_conv_out, m_w_fox_out, m_w_mem_out, m_w_out, m_norm2_g, m_w_up, m_w_down, v_norm1_g, v_w_in, v_b_f, v_conv_w, v_fox_q_g, v_fox_k_g, v_mem_norm_g, v_w_mem_kv, v_mem_q_g, v_mem_k_g, v_w_conv_out, v_w_fox_out, v_w_mem_out, v_w_out, v_norm2_g, v_w_up, v_w_down):
    args = dict(locals())
    wts = {n: args[n] for n in WEIGHTS}
    ms = {n: args["m_" + n] for n in WEIGHTS}
    vs = {n: args["v_" + n] for n in WEIGHTS}
    x_pos, y_pos, c_pos = _position()
    me = _index(x_pos, y_pos, c_pos)

    shards = {n: wts[n].astype(BF16) for n in BIG if n != "w_in"}
    rows_in = w_in.shape[1]
    shards["w_in"] = jnp.pad(w_in.T.astype(BF16), ((0, _padded_rows(rows_in) - rows_in), (0, 0)))
    wi, cw8 = _run_rider("all_gather_first", _gather_rider([shards["w_in"], conv_w], True))
    full = {"w_in": wi}
    small = {n: wts[n] for n in SMALL}
    small["conv_w"] = _unblock(cw8)
    comm = {"shards": shards, "c": c_pos.astype(jnp.int32).reshape(1)}

    loss, grad_x, parts, gs = _local_step(x[0], mem[0], loss_target[0], full, small, comm)

    out_g, out_d, out_m, out_v = {}, {}, {}, {}
    for n in BIG:
        if n == "w_in":
            res = _adamw("adamw_" + n, parts[n], wts[n].T, ms[n].T, vs[n].T)
            out_g[n], out_d[n], out_m[n], out_v[n] = (r.T for r in res)
        else:
            out_g[n], out_d[n], out_m[n], out_v[n] = _adamw("adamw_" + n, parts[n], wts[n], ms[n], vs[n])

    small_sizes = [int(math.prod(gs[n].shape)) for n in SMALL]
    packed = _pack([gs[n].reshape(-1) for n in SMALL])
    gsum = _sum_parts("sum_small", _run_rider("exchange_small", _broadcast_rider([packed]))[0])
    gsmall = dict(zip(SMALL, _unpack(gsum, small_sizes)))
    cols = conv_w.shape[1]
    gsmall["conv_w"] = lax.dynamic_slice(gsmall["conv_w"].reshape(CONV_TAPS, -1), (0, me * cols), (CONV_TAPS, cols)).reshape(-1)
    pg, pw, pm, pv = (_pack([src[n].reshape(-1) for n in SMALL]) for src in (gsmall, wts, ms, vs))
    _, sd, sm, sv = _adamw("adamw_small", pg[None], pw, pm, pv)
    local_sizes = [int(math.prod(wts[n].shape)) for n in SMALL]
    for dst, src in ((out_d, sd), (out_m, sm), (out_v, sv)):
        for n, val in zip(SMALL, _unpack(src, local_sizes)):
            dst[n] = val.reshape(wts[n].shape)
    for n in SMALL:
        out_g[n] = gsmall[n].reshape(wts[n].shape)

    loss = lax.psum(loss, MESH_AXES)
    return (loss, grad_x[None], *[out_g[n] for n in WEIGHTS], *[out_d[n] for n in WEIGHTS],
            *[out_m[n] for n in WEIGHTS], *[out_v[n] for n in WEIGHTS])
```

```python
import math

import numpy as np
import jax
import jax.numpy as jnp
from jax import lax
from jax.experimental import pallas as pl
from jax.experimental.pallas import tpu as pltpu

F32 = jnp.float32
BF16 = jnp.bfloat16

EPS = 1e-6
N_DEV = 8
N_CHIPS = 4
FOX_HEAD_DIM = 128
MEM_HEADS = 4
CONV_TAPS = 3
N_BRANCHES = 3
F_ROWS = 16

ADAM_LR = 0.001
ADAM_B1 = 0.9
ADAM_B2 = 0.999
ADAM_EPS = 1e-08
ADAM_WD = 0.01
ADAM_STEP = 10

V7X_VMEM_BYTES = 64 * 1024 * 1024
VMEM_LIMIT = V7X_VMEM_BYTES * 3 // 4
LANES = 128
NEG = -1e30
MAX_KEYS = 1024

MESH_AXES = ("x", "y", "c")
MESH = pl.DeviceIdType.MESH
ANY = pl.BlockSpec(memory_space=pl.ANY)

NN = (((1,), (0,)), ((), ()))
NT = (((1,), (1,)), ((), ()))
TN = (((0,), (0,)), ((), ()))


def _params(sem):
    return pltpu.CompilerParams(dimension_semantics=sem, vmem_limit_bytes=VMEM_LIMIT)


def _dot(a, b, dn):
    return lax.dot_general(a, b, dn, preferred_element_type=F32)


def _tile(n, t):
    if n <= t:
        return n
    for step in (LANES, 16):
        for cand in range(t - t % step, 0, -step):
            if n % cand == 0:
                return cand
    raise ValueError((n, t))


class _Rider:
    def __init__(self, ins, out_shapes, sem_shapes, start, finish, middle=None):
        self.ins, self.out_shapes, self.sem_shapes = list(ins), list(out_shapes), list(sem_shapes)
        self.start, self.finish, self.middle = start, finish, middle


def _position():
    return lax.axis_index("x"), lax.axis_index("y"), lax.axis_index("c")


def _index(px, py, pc):
    return 4 * px + 2 * py + pc


def _dma_sems(n, per):
    return [pltpu.SemaphoreType.DMA((n, per)), pltpu.SemaphoreType.DMA((n, per)), pltpu.SemaphoreType.DMA((n,))]


def _gather_rider(shards, pass_on):
    n = len(shards)

    def copies(ins, outs, sems):
        send_sems, recv_sems, local_sems = sems
        x, y, c = _position()
        me, sibling = (x, y, c), (x, y, 1 - c)
        chips = [(1 - x, y), (x, 1 - y), (1 - x, 1 - y)]

        def copy(a, k, block, to, src=None, k_send=None):
            rows = outs[a].at[_index(*block)]
            return pltpu.make_async_remote_copy(
                src_ref=rows if src is None else src, dst_ref=rows,
                send_sem=send_sems.at[a, k if k_send is None else k_send], recv_sem=recv_sems.at[a, k],
                device_id=to, device_id_type=MESH)

        mine = [pltpu.make_async_copy(ins[a], outs[a].at[_index(*me)], local_sems.at[a]) for a in range(n)]
        first = []
        for a in range(n):
            first.append(copy(a, 0, me, sibling, src=ins[a]))
            first += [copy(a, 1 + j, me, (*chips[j], c), src=ins[a]) for j in range(2 if pass_on else 3)]
        return copy, mine, first, me, sibling, chips, c

    def start(ins, outs, sems):
        _, mine, first, *_ = copies(ins, outs, sems)
        for cp in mine + first:
            cp.start()

    def by_kind(c, fn):
        if pass_on:
            pl.when(c == 1)(lambda: fn(0, 1))
            pl.when(c == 0)(lambda: fn(1, 0))
        else:
            fn(0, 1)

    def onward(copy, a, j_on, j_to, chips, c, sibling):
        third = [copy(a, 3, (*chips[j_on], c), (*chips[j_to], c), k_send=7)] if pass_on else []
        return third + [copy(a, 4 + j_on, (*chips[j_on], c), sibling)]

    def middle(ins, outs, sems):
        copy, _, _, me, sibling, chips, c = copies(ins, outs, sems)

        def fn(j_on, j_to):
            for a in range(n):
                copy(a, 1 + j_on, (*chips[j_on], c), me).wait_recv()
                for cp in onward(copy, a, j_on, j_to, chips, c, sibling):
                    cp.start()

        by_kind(c, fn)

    def finish(ins, outs, sems):
        copy, mine, first, me, sibling, chips, c = copies(ins, outs, sems)

        def fn(j_on, j_to):
            passed = [cp for a in range(n) for cp in onward(copy, a, j_on, j_to, chips, c, sibling)]
            for a in range(n):
                for j in (j_to, 2):
                    copy(a, 1 + j, (*chips[j], c), me).wait_recv()
                    passed.append(copy(a, 4 + j, (*chips[j], c), sibling))
                    passed[-1].start()
            for a in range(n):
                copy(a, 0, sibling, me).wait_recv()
                for j, chip in enumerate(chips):
                    copy(a, 4 + j, (*chip, 1 - c), me).wait_recv()
            for cp in first + passed:
                cp.wait_send()
            for cp in mine:
                cp.wait()

        by_kind(c, fn)

    out_shapes = [jax.ShapeDtypeStruct((N_DEV,) + s.shape, s.dtype) for s in shards]
    return _Rider(shards, out_shapes, _dma_sems(n, 8), start, finish, middle)


def _pair_rider(grads):
    n = len(grads)

    def copies(ins, outs, sems):
        send_sems, recv_sems, _ = sems
        x, y, c = _position()
        return [pltpu.make_async_remote_copy(
            src_ref=ins[a].at[2 * q + (1 - c)], dst_ref=outs[a].at[q],
            send_sem=send_sems.at[a, q], recv_sem=recv_sems.at[a, q], device_id=(x, y, 1 - c), device_id_type=MESH)
            for a in range(n) for q in range(N_CHIPS)]

    def start(ins, outs, sems):
        for cp in copies(ins, outs, sems):
            cp.start()

    def finish(ins, outs, sems):
        cps = copies(ins, outs, sems)
        for cp in cps:
            cp.wait_recv()
        for cp in cps:
            cp.wait_send()

    out_shapes = [jax.ShapeDtypeStruct((N_CHIPS,) + g.shape[1:], g.dtype) for g in grads]
    return _Rider(grads, out_shapes, _dma_sems(n, N_CHIPS), start, finish)


def _chip_rider(parts):
    n = len(parts)

    def copies(ins, outs, sems):
        send_sems, recv_sems, local_sems = sems
        x, y, c = _position()
        q_me = 2 * x + y
        chips = [(1 - x, y), (x, 1 - y), (1 - x, 1 - y)]
        mine = [pltpu.make_async_copy(ins[a].at[q_me], outs[a].at[q_me], local_sems.at[a]) for a in range(n)]
        sends, arrivals = [], []
        for a in range(n):
            for j, (tx, ty) in enumerate(chips):
                q_t = 2 * tx + ty
                sends.append(pltpu.make_async_remote_copy(
                    src_ref=ins[a].at[q_t], dst_ref=outs[a].at[q_me],
                    send_sem=send_sems.at[a, j], recv_sem=recv_sems.at[a, j], device_id=(tx, ty, c), device_id_type=MESH))
                arrivals.append(pltpu.make_async_remote_copy(
                    src_ref=ins[a].at[q_t], dst_ref=outs[a].at[q_t],
                    send_sem=send_sems.at[a, j], recv_sem=recv_sems.at[a, j], device_id=(tx, ty, c), device_id_type=MESH))
        return mine, sends, arrivals

    def start(ins, outs, sems):
        mine, sends, _ = copies(ins, outs, sems)
        for cp in mine + sends:
            cp.start()

    def finish(ins, outs, sems):
        mine, sends, arrivals = copies(ins, outs, sems)
        for cp in arrivals:
            cp.wait_recv()
        for cp in sends:
            cp.wait_send()
        for cp in mine:
            cp.wait()

    out_shapes = [jax.ShapeDtypeStruct(p.shape, p.dtype) for p in parts]
    return _Rider(parts, out_shapes, _dma_sems(n, 3), start, finish)


def _broadcast_rider(values):
    n = len(values)

    def copies(ins, outs, sems):
        send_sems, recv_sems, local_sems = sems
        x, y, c = _position()
        me = _index(x, y, c)

        def peer(k):
            return (1 - x if k & 4 else x, 1 - y if k & 2 else y, 1 - c if k & 1 else c)

        mine = [pltpu.make_async_copy(ins[a], outs[a].at[me], local_sems.at[a]) for a in range(n)]
        sends, arrivals = [], []
        for a in range(n):
            for k in range(1, N_DEV):
                common = dict(send_sem=send_sems.at[a, k - 1], recv_sem=recv_sems.at[a, k - 1], device_id=peer(k), device_id_type=MESH)
                sends.append(pltpu.make_async_remote_copy(src_ref=ins[a], dst_ref=outs[a].at[me], **common))
                arrivals.append(pltpu.make_async_remote_copy(src_ref=ins[a], dst_ref=outs[a].at[_index(*peer(k))], **common))
        return mine, sends, arrivals

    def start(ins, outs, sems):
        mine, sends, _ = copies(ins, outs, sems)
        for cp in mine + sends:
            cp.start()

    def finish(ins, outs, sems):
        mine, sends, arrivals = copies(ins, outs, sems)
        for cp in arrivals:
            cp.wait_recv()
        for cp in sends:
            cp.wait_send()
        for cp in mine:
            cp.wait()

    out_shapes = [jax.ShapeDtypeStruct((N_DEV,) + v.shape, v.dtype) for v in values]
    return _Rider(values, out_shapes, _dma_sems(n, 7), start, finish)


def _join_riders(*riders):
    def each(fn_name, ins, outs, sems):
        i = o = s = 0
        for r in riders:
            n_i, n_o, n_s = len(r.ins), len(r.out_shapes), len(r.sem_shapes)
            if getattr(r, fn_name) is not None:
                getattr(r, fn_name)(ins[i:i + n_i], outs[o:o + n_o], sems[s:s + n_s])
            i, o, s = i + n_i, o + n_o, s + n_s

    middle = (lambda ins, outs, sems: each("middle", ins, outs, sems)) if any(r.middle for r in riders) else None
    return _Rider([a for r in riders for a in r.ins], [a for r in riders for a in r.out_shapes],
                  [a for r in riders for a in r.sem_shapes],
                  lambda ins, outs, sems: each("start", ins, outs, sems),
                  lambda ins, outs, sems: each("finish", ins, outs, sems), middle)


def _run_rider(name, rider):
    n_in, n_out = len(rider.ins), len(rider.out_shapes)

    def body(*refs):
        ins, outs, sems = refs[:n_in], refs[n_in:n_in + n_out], refs[n_in + n_out:]
        rider.start(ins, outs, sems)
        if rider.middle is not None:
            rider.middle(ins, outs, sems)
        rider.finish(ins, outs, sems)

    return pl.pallas_call(
        body, name=name, in_specs=[ANY] * n_in, out_specs=[ANY] * n_out, out_shape=rider.out_shapes,
        scratch_shapes=rider.sem_shapes)(*rider.ins)


class _Host:
    def __init__(self, rider):
        self.rider = rider
        self.n_in = len(rider.ins) if rider else 0
        self.n_out = len(rider.out_shapes) if rider else 0
        self.n_sem = len(rider.sem_shapes) if rider else 0
        self.ins = rider.ins if rider else []
        self.in_specs = [ANY] * self.n_in
        self.out_specs = [ANY] * self.n_out
        self.out_shapes = rider.out_shapes if rider else []
        self.scratch = rider.sem_shapes if rider else []

    def run(self, first, last, ins, outs, sems, compute, midway=None):
        if self.rider is None:
            compute()
            return

        @pl.when(first)
        def _():
            self.rider.start(ins, outs, sems)

        compute()
        if self.rider.middle is not None and midway is not None:
            pl.when(midway)(lambda: self.rider.middle(ins, outs, sems))

        @pl.when(last)
        def _():
            if self.rider.middle is not None and midway is None:
                self.rider.middle(ins, outs, sems)
            self.rider.finish(ins, outs, sems)


def _matmul(name, kind, a, b, *, tm, tn, tk, outs, epilogue=None, extras=(), out_blocks=False, rider=None, j_outer=False):
    if kind == "nn":
        (m, kdim), n = a.shape, b.shape[1]
    elif kind == "nt":
        (m, kdim), n = a.shape, b.shape[0]
    else:
        (kdim, m), n = a.shape, b.shape[1]
    if out_blocks:
        tn = min(tn, n // N_DEV)
    tm, tn, tk = _tile(m, tm), _tile(n, tn), _tile(kdim, tk)
    ni, nj, nk = m // tm, n // tn, kdim // tk

    def spec(shape, fn):
        return pl.BlockSpec(shape, (lambda g0, g1, k: fn(g1, g0, k)) if j_outer else fn)

    a_spec = spec((tk, tm), lambda i, j, k: (k, i)) if kind == "tn" else spec((tm, tk), lambda i, j, k: (i, k))
    b_spec = spec((tn, tk), lambda i, j, k: (j, k)) if kind == "nt" else spec((tk, tn), lambda i, j, k: (k, j))
    dn = {"nn": NN, "nt": NT, "tn": TN}[kind]

    tile_spec = spec((tm, tn), lambda i, j, k: (i, j))
    if out_blocks:
        width = n // N_DEV
        r_out = width // tn
        out_shape = [jax.ShapeDtypeStruct((N_DEV, m, width), dt) for dt in outs]
        out_specs = [spec((None, tm, tn), lambda i, j, k: (j // r_out, i, j % r_out)) for _ in outs]
    else:
        out_shape = [jax.ShapeDtypeStruct((m, n), dt) for dt in outs]
        out_specs = [tile_spec for _ in outs]
    n_ex, n_out = len(extras), len(outs)
    host = _Host(rider)
    n_acc = 1 if nk > 1 else 0

    def body(*refs):
        a_ref, b_ref = refs[0], refs[1]
        pos = 2
        ex_refs = refs[pos:pos + n_ex]; pos += n_ex
        r_ins = refs[pos:pos + host.n_in]; pos += host.n_in
        out_refs = refs[pos:pos + n_out]; pos += n_out
        r_outs = refs[pos:pos + host.n_out]; pos += host.n_out
        acc_ref = refs[pos] if n_acc else None
        sems = refs[pos + n_acc:]
        i, j, k = pl.program_id(1 if j_outer else 0), pl.program_id(0 if j_outer else 1), pl.program_id(2)

        def finish_tile(acc):
            vals = (acc,) if epilogue is None else epilogue(acc, *[e[...] for e in ex_refs])
            for o_ref, v in zip(out_refs, vals):
                o_ref[...] = v.astype(o_ref.dtype)

        def compute():
            part = _dot(a_ref[...], b_ref[...], dn)
            if nk == 1:
                finish_tile(part)
                return

            @pl.when(k == 0)
            def _():
                acc_ref[...] = part

            @pl.when(jnp.logical_and(k > 0, k < nk - 1))
            def _():
                acc_ref[...] += part

            @pl.when(k == nk - 1)
            def _():
                finish_tile(acc_ref[...] + part)

        first = jnp.logical_and(jnp.logical_and(i == 0, j == 0), k == 0)
        last = jnp.logical_and(jnp.logical_and(i == ni - 1, j == nj - 1), k == nk - 1)
        step = (pl.program_id(0) * (ni if j_outer else nj) + pl.program_id(1)) * nk + k
        host.run(first, last, r_ins, r_outs, sems, compute, midway=step == (ni * nj * nk * 3) // 5)

    sem = ("arbitrary",) * 3 if rider else ("parallel", "parallel", "arbitrary")
    res = pl.pallas_call(
        body,
        name=name,
        grid=(nj, ni, nk) if j_outer else (ni, nj, nk),
        in_specs=[a_spec, b_spec] + [tile_spec for _ in extras] + host.in_specs,
        out_specs=out_specs + host.out_specs,
        out_shape=out_shape + host.out_shapes,
        scratch_shapes=([pltpu.VMEM((tm, tn), F32)] if n_acc else []) + host.scratch,
        compiler_params=_params(sem),
    )(a, b, *extras, *host.ins)
    return res[0] if len(res) == 1 else res


def _rms_fwd(name, x, g, tm=512):
    t, d = x.shape
    tm = _tile(t, tm)

    def body(x_ref, g_ref, h_ref):
        xf = x_ref[...]
        r = lax.rsqrt(jnp.mean(xf * xf, axis=-1, keepdims=True) + EPS)
        h_ref[...] = (xf * r * g_ref[...]).astype(h_ref.dtype)

    return pl.pallas_call(
        body,
        name=name,
        grid=(t // tm,),
        in_specs=[pl.BlockSpec((tm, d), lambda i: (i, 0)), pl.BlockSpec((1, d), lambda i: (0, 0))],
        out_specs=pl.BlockSpec((tm, d), lambda i: (i, 0)),
        out_shape=jax.ShapeDtypeStruct((t, d), BF16),
        compiler_params=_params(("parallel",)),
    )(x, g.reshape(1, d))


def _rms_bwd(name, dh, x, g, res=None, tm=256):
    t, d = x.shape
    tm = _tile(t, tm)
    has_res = res is not None

    def body(*refs):
        if has_res:
            dh_ref, x_ref, g_ref, res_ref, dx_ref, dxb_ref, gg_ref, ss_ref = refs
        else:
            dh_ref, x_ref, g_ref, dx_ref, dxb_ref, gg_ref, ss_ref = refs
        i = pl.program_id(0)
        xf = x_ref[...]
        r = lax.rsqrt(jnp.mean(xf * xf, axis=-1, keepdims=True) + EPS)
        xh = xf * r
        dhf = dh_ref[...].astype(F32)
        dxh = dhf * g_ref[...]
        dx = r * (dxh - xh * jnp.mean(dxh * xh, axis=-1, keepdims=True))

        @pl.when(i == 0)
        def _():
            gg_ref[...] = jnp.zeros_like(gg_ref)
            ss_ref[...] = jnp.zeros_like(ss_ref)

        if has_res:
            resf = res_ref[...]
            dx = dx + resf
            ss_ref[...] += jnp.sum(jnp.sum(resf * resf, axis=0, keepdims=True), axis=1, keepdims=True)
        dx_ref[...] = dx
        dxb_ref[...] = dx.astype(BF16)
        gg_ref[...] += jnp.sum(dhf * xh, axis=0, keepdims=True)

    row = pl.BlockSpec((tm, d), lambda i: (i, 0))
    vec = pl.BlockSpec((1, d), lambda i: (0, 0))
    one = pl.BlockSpec((1, 1), lambda i: (0, 0))
    ins = [dh, x, g.reshape(1, d)] + ([res] if has_res else [])
    dx, dxb, gg, ss = pl.pallas_call(
        body,
        name=name,
        grid=(t // tm,),
        in_specs=[row, row, vec] + ([row] if has_res else []),
        out_specs=[row, row, vec, one],
        out_shape=[jax.ShapeDtypeStruct((t, d), F32), jax.ShapeDtypeStruct((t, d), BF16), jax.ShapeDtypeStruct((1, d), F32),
                   jax.ShapeDtypeStruct((1, 1), F32)],
        compiler_params=_params(("arbitrary",)),
    )(*ins)
    return dx, dxb, gg.reshape(d), ss[0, 0]


def _head_rms(xf):
    r = lax.rsqrt(jnp.mean(xf * xf, axis=-1, keepdims=True) + EPS)
    return xf * r, r


def _head_rms_bwd(dy, xn, r, g):
    dxh = dy * g
    dx = r * (dxh - xn * jnp.mean(dxh * xn, axis=-1, keepdims=True))
    return dx, jnp.sum(dy * xn, axis=0, keepdims=True)


def _col_to_row(col):
    n = col.shape[0]
    eye = lax.broadcasted_iota(jnp.int32, (n, n), 0) == lax.broadcasted_iota(jnp.int32, (n, n), 1)
    return jnp.sum(jnp.where(eye, col, 0.0), axis=0, keepdims=True)


def _row_to_col(row):
    n = row.shape[1]
    eye = lax.broadcasted_iota(jnp.int32, (n, n), 0) == lax.broadcasted_iota(jnp.int32, (n, n), 1)
    return jnp.sum(jnp.where(eye, row, 0.0), axis=1, keepdims=True)


def _dproj_args(dproj, n_in):
    if dproj is None:
        return [], [], {}
    return [dproj], [ANY], {n_in: 0}


def _shift_down(u, s, rows):
    return jnp.where(rows >= s, pltpu.roll(u, s, axis=0), 0.0)


def _shift_up(u, s, rows, t):
    return jnp.where(rows < t - s, pltpu.roll(u, t - s, axis=0), 0.0)


def _conv_fwd(proj, off, conv_w, cb):
    t = proj.shape[0]
    c = conv_w.shape[1]
    blk0 = off // (3 * cb)

    def body(p_ref, w_ref, y_ref):
        rows = lax.broadcasted_iota(jnp.int32, (t, cb), 0)
        bg = p_ref[:, 0:cb].astype(F32)
        u = p_ref[:, cb:2 * cb].astype(F32) * p_ref[:, 2 * cb:3 * cb].astype(F32)
        w = w_ref[...]
        conv = w[2:3] * u + w[1:2] * _shift_down(u, 1, rows) + w[0:1] * _shift_down(u, 2, rows)
        y_ref[...] = (bg * conv).astype(y_ref.dtype)

    return pl.pallas_call(
        body,
        name="conv_fwd",
        grid=(c // cb,),
        in_specs=[pl.BlockSpec((t, 3 * cb), lambda j: (0, blk0 + j)), pl.BlockSpec((CONV_TAPS, cb), lambda j: (0, j))],
        out_specs=pl.BlockSpec((t, cb), lambda j: (0, j)),
        out_shape=jax.ShapeDtypeStruct((t, c), BF16),
        compiler_params=_params(("parallel",)),
    )(proj, conv_w)


def _conv_bwd(proj, off, conv_w, dy, cb, dproj, rider=None):
    t = proj.shape[0]
    c = conv_w.shape[1]
    blk0 = off // (3 * cb)
    nj = c // cb
    host = _Host(rider)

    def body(*refs):
        p_ref, w_ref, dy_ref = refs[:3]
        r_ins = refs[4:4 + host.n_in]
        dp_ref, gw_ref = refs[4 + host.n_in:6 + host.n_in]
        r_outs = refs[6 + host.n_in:6 + host.n_in + host.n_out]
        sems = refs[6 + host.n_in + host.n_out:]
        j = pl.program_id(0)

        def compute():
            rows = lax.broadcasted_iota(jnp.int32, (t, cb), 0)
            bg = p_ref[:, 0:cb].astype(F32)
            cg = p_ref[:, cb:2 * cb].astype(F32)
            v = p_ref[:, 2 * cb:3 * cb].astype(F32)
            u = cg * v
            w = w_ref[...]
            u1 = _shift_down(u, 1, rows)
            u2 = _shift_down(u, 2, rows)
            conv = w[2:3] * u + w[1:2] * u1 + w[0:1] * u2
            dyf = dy_ref[...].astype(F32)
            dconv = dyf * bg
            du = w[2:3] * dconv + w[1:2] * _shift_up(dconv, 1, rows, t) + w[0:1] * _shift_up(dconv, 2, rows, t)
            dp_ref[:, 0:cb] = (dyf * conv).astype(dp_ref.dtype)
            dp_ref[:, cb:2 * cb] = (du * v).astype(dp_ref.dtype)
            dp_ref[:, 2 * cb:3 * cb] = (du * cg).astype(dp_ref.dtype)
            gw_ref[0:1, :] = jnp.sum(dconv * u2, axis=0, keepdims=True)
            gw_ref[1:2, :] = jnp.sum(dconv * u1, axis=0, keepdims=True)
            gw_ref[2:3, :] = jnp.sum(dconv * u, axis=0, keepdims=True)

        host.run(j == 0, j == nj - 1, r_ins, r_outs, sems, compute)

    res = pl.pallas_call(
        body,
        name="conv_bwd",
        grid=(nj,),
        in_specs=[
            pl.BlockSpec((t, 3 * cb), lambda j: (0, blk0 + j)),
            pl.BlockSpec((CONV_TAPS, cb), lambda j: (0, j)),
            pl.BlockSpec((t, cb), lambda j: (0, j)),
            ANY,
        ] + host.in_specs,
        out_specs=[pl.BlockSpec((t, 3 * cb), lambda j: (0, blk0 + j)), pl.BlockSpec((CONV_TAPS, cb), lambda j: (0, j))] + host.out_specs,
        out_shape=[jax.ShapeDtypeStruct(dproj.shape, dproj.dtype), jax.ShapeDtypeStruct((CONV_TAPS, c), F32)] + host.out_shapes,
        input_output_aliases={3: 0},
        scratch_shapes=host.scratch,
        compiler_params=_params(("arbitrary",)),
    )(proj, conv_w, dy, dproj, *host.ins)
    return res


def _lane_scan(x, reverse):
    lane = lax.broadcasted_iota(jnp.int32, x.shape, 1)
    s = 1
    while s < LANES:
        if reverse:
            x = x + jnp.where(lane < LANES - s, pltpu.roll(x, LANES - s, axis=1), 0.0)
        else:
            x = x + jnp.where(lane >= s, pltpu.roll(x, s, axis=1), 0.0)
        s *= 2
    return x


def _scan_rows(src_ref, dst_ref, t, reverse, fn=None):
    groups = list(range(t // LANES))
    if reverse:
        groups = groups[::-1]
    carry = None
    for gi in groups:
        sl = slice(gi * LANES, (gi + 1) * LANES)
        blk = src_ref[:, sl]
        if fn is not None:
            blk = fn(blk)
        blk = _lane_scan(blk, reverse)
        if carry is not None:
            blk = blk + carry
        dst_ref[:, sl] = blk
        carry = blk[:, 0:1] if reverse else blk[:, LANES - 1:LANES]


def _forget_fwd(z_row, b_col):
    rows, t = z_row.shape

    def body(z_ref, b_ref, c_ref):
        def logf(z):
            zz = z + b_ref[...]
            return jnp.minimum(zz, 0.0) - jnp.log(1.0 + jnp.exp(-jnp.abs(zz)))

        _scan_rows(z_ref, c_ref, t, False, logf)

    return pl.pallas_call(
        body,
        name="forget_fwd",
        out_shape=jax.ShapeDtypeStruct((rows, t), F32),
        compiler_params=pltpu.CompilerParams(vmem_limit_bytes=VMEM_LIMIT),
    )(z_row, b_col)


def _rows_to_colb(c_row3, tq):
    heads, _, t = c_row3.shape

    def body(r_ref, o_ref):
        o_ref[...] = jnp.broadcast_to(_row_to_col(r_ref[...]), (tq, LANES))

    return pl.pallas_call(
        body,
        name="rows_to_colb",
        grid=(heads, t // tq),
        in_specs=[pl.BlockSpec((None, 1, tq), lambda h, i: (h, 0, i))],
        out_specs=pl.BlockSpec((None, tq, LANES), lambda h, i: (h, i, 0)),
        out_shape=jax.ShapeDtypeStruct((heads, t, LANES), F32),
        compiler_params=_params(("parallel", "parallel")),
    )(c_row3)


def _forget_bwd(z_row, b_col, dc_row):
    rows, t = z_row.shape

    def body(z_ref, b_ref, dc_ref, dz_ref, db_ref, tmp_ref):
        _scan_rows(dc_ref, tmp_ref, t, True)
        zz = z_ref[...] + b_ref[...]
        dz = tmp_ref[...] * (1.0 / (1.0 + jnp.exp(zz)))
        dz_ref[...] = dz.astype(dz_ref.dtype)
        db_ref[...] = jnp.sum(dz, axis=1, keepdims=True)

    return pl.pallas_call(
        body,
        name="forget_bwd",
        out_shape=[jax.ShapeDtypeStruct((rows, t), BF16), jax.ShapeDtypeStruct((rows, 1), F32)],
        scratch_shapes=[pltpu.VMEM((rows, t), F32)],
        compiler_params=pltpu.CompilerParams(vmem_limit_bytes=VMEM_LIMIT),
    )(z_row, b_col, dc_row)


def _fox_fwd(proj, off, gq, gk, c_row3, c_colb, heads, tq, rider=None):
    t = proj.shape[0]
    hd = FOX_HEAD_DIM
    tq = _tile(t, tq)
    nq = t // tq
    blk0 = off // hd
    scale = 1.0 / math.sqrt(hd)
    host = _Host(rider)

    def body(*refs):
        q_ref, k_ref, v_ref, gq_ref, gk_ref, crow_ref, ccol_ref = refs[:7]
        r_ins = refs[7:7 + host.n_in]
        o_ref, lse_ref = refs[7 + host.n_in:9 + host.n_in]
        r_outs = refs[9 + host.n_in:9 + host.n_in + host.n_out]
        khat_ref, v_t_ref = refs[9 + host.n_in + host.n_out:11 + host.n_in + host.n_out]
        sems = refs[11 + host.n_in + host.n_out:]
        h, qi = pl.program_id(0), pl.program_id(1)

        def compute():
            eye = (lax.broadcasted_iota(jnp.int32, (hd, hd), 0) == lax.broadcasted_iota(jnp.int32, (hd, hd), 1)).astype(BF16)

            @pl.when(qi == 0)
            def _():
                kn, _ = _head_rms(k_ref[...].astype(F32))
                khat_ref[...] = (kn * gk_ref[...]).astype(BF16)
                v_t_ref[...] = _dot(eye, v_ref[...], NT).astype(BF16)

            qn, _ = _head_rms(q_ref[...].astype(F32))
            qhat = (qn * (gq_ref[...] * scale)).astype(BF16)
            crow = crow_ref[:, pl.ds(pl.multiple_of(qi * tq, tq), tq)]
            above = lax.broadcasted_iota(jnp.int32, (tq, tq), 1) >= lax.broadcasted_iota(jnp.int32, (tq, tq), 0)

            def tile(j, keys, carry, diagonal):
                m, l, acc_t = carry
                ks = pl.multiple_of(j * keys, keys)
                s_t = _dot(khat_ref[pl.ds(ks, keys), :], qhat, NT) - ccol_ref[pl.ds(ks, keys), 0:1]
                if diagonal:
                    s_t = jnp.where(above, s_t, NEG)
                m_new = jnp.maximum(m, jnp.max(s_t, axis=0, keepdims=True) + crow)
                alpha = jnp.exp(m - m_new)
                p_t = jnp.exp(s_t + (crow - m_new))
                l = alpha * l + jnp.sum(p_t, axis=0, keepdims=True)
                acc_t = alpha * acc_t + _dot(v_t_ref[:, pl.ds(ks, keys)], p_t.astype(BF16), NN)
                return m_new, l, acc_t

            init = (jnp.full((1, tq), NEG, F32), jnp.zeros((1, tq), F32), jnp.zeros((hd, tq), F32))
            pairs = qi // 2 if 2 * tq <= MAX_KEYS else 0
            carry = lax.fori_loop(0, pairs, lambda j, c: tile(j, 2 * tq, c, False), init)
            carry = lax.fori_loop(2 * pairs, qi, lambda j, c: tile(j, tq, c, False), carry)
            m, l, acc_t = tile(qi, tq, carry, True)
            o_ref[...] = _dot((acc_t / l).astype(BF16), eye, TN).astype(o_ref.dtype)
            lse_ref[...] = m + jnp.log(l)

        first = jnp.logical_and(h == 0, qi == 0)
        last = jnp.logical_and(h == heads - 1, qi == nq - 1)
        host.run(first, last, r_ins, r_outs, sems, compute)

    res = pl.pallas_call(
        body,
        name="fox_fwd",
        grid=(heads, nq),
        in_specs=[
            pl.BlockSpec((tq, hd), lambda h, i: (i, blk0 + 3 * h)),
            pl.BlockSpec((t, hd), lambda h, i: (0, blk0 + 3 * h + 1)),
            pl.BlockSpec((t, hd), lambda h, i: (0, blk0 + 3 * h + 2)),
            pl.BlockSpec((1, hd), lambda h, i: (0, 0)),
            pl.BlockSpec((1, hd), lambda h, i: (0, 0)),
            pl.BlockSpec((None, 1, t), lambda h, i: (h, 0, 0)),
            pl.BlockSpec((None, t, LANES), lambda h, i: (h, 0, 0)),
        ] + host.in_specs,
        out_specs=[pl.BlockSpec((tq, hd), lambda h, i: (i, h)), pl.BlockSpec((None, 1, tq), lambda h, i: (h, 0, i))] + host.out_specs,
        out_shape=[jax.ShapeDtypeStruct((t, heads * hd), BF16), jax.ShapeDtypeStruct((heads, 1, t), F32)] + host.out_shapes,
        scratch_shapes=[pltpu.VMEM((t, hd), BF16), pltpu.VMEM((hd, t), BF16)] + host.scratch,
        compiler_params=_params(("arbitrary", "arbitrary")),
    )(proj, proj, proj, gq.reshape(1, hd), gk.reshape(1, hd), c_row3, c_colb, *host.ins)
    return res


def _fox_bwd(proj, off, o, do, gq, gk, c_row3, c_colb, lse, heads, tq, dproj, rider=None):
    t = proj.shape[0]
    hd = FOX_HEAD_DIM
    tq = _tile(t, tq)
    nb = t // tq
    blk0 = off // hd
    scale = 1.0 / math.sqrt(hd)
    host = _Host(rider)
    n_fixed_in = 11

    def body(*refs):
        q_ref, k_ref, v_ref, o_ref, do_ref, gq_ref, gk_ref, crow_ref, ccol_ref, lse_ref = refs[:10]
        pos = n_fixed_in
        r_ins = refs[pos:pos + host.n_in]; pos += host.n_in
        dp_ref, dc_ref, ggq_ref, ggk_ref = refs[pos:pos + 4]; pos += 4
        r_outs = refs[pos:pos + host.n_out]; pos += host.n_out
        qhat_ref, khat_ref, khat_t_ref, dq_t_ref, dk_ref, dcq_ref, dck_ref, delta_ref = refs[pos:pos + 8]; pos += 8
        sems = refs[pos:]
        h = pl.program_id(0)

        def compute():
            qn, rq = _head_rms(q_ref[...].astype(F32))
            qhat_ref[...] = (qn * (gq_ref[...] * scale)).astype(BF16)
            kn, rk = _head_rms(k_ref[...].astype(F32))
            khat_ref[...] = (kn * gk_ref[...]).astype(BF16)
            eye = (lax.broadcasted_iota(jnp.int32, (hd, hd), 0) == lax.broadcasted_iota(jnp.int32, (hd, hd), 1)).astype(BF16)
            khat_t_ref[...] = _dot(eye, khat_ref[...], NT).astype(BF16)
            delta = jnp.sum(do_ref[...].astype(F32) * o_ref[...].astype(F32), axis=-1, keepdims=True)
            for b in range(nb):
                sl = slice(b * tq, (b + 1) * tq)
                delta_ref[:, sl] = _col_to_row(delta[sl, :])
            dq_t_ref[...] = jnp.zeros_like(dq_t_ref)
            dcq_ref[...] = jnp.zeros_like(dcq_ref)
            above = lax.broadcasted_iota(jnp.int32, (tq, tq), 1) >= lax.broadcasted_iota(jnp.int32, (tq, tq), 0)

            def kv_block(j, _):
                ks = pl.multiple_of(j * tq, tq)
                kh = khat_ref[pl.ds(ks, tq), :]
                kh_t = khat_t_ref[:, pl.ds(ks, tq)]
                vv = v_ref[pl.ds(ks, tq), :]
                ccol = ccol_ref[pl.ds(ks, tq), 0:1]

                def q_block(i, n, carry, diagonal):
                    dk, dv, dck = carry
                    qs = pl.multiple_of(i * tq, tq)
                    qh = qhat_ref[pl.ds(qs, n), :]
                    dob = do_ref[pl.ds(qs, n), :]
                    s_t = _dot(kh, qh, NT) + ((crow_ref[:, pl.ds(qs, n)] - lse_ref[:, pl.ds(qs, n)]) - ccol)
                    p_t = jnp.exp(s_t)
                    if diagonal:
                        p_t = jnp.where(above, p_t, 0.0)
                    ds_t = p_t * (_dot(vv, dob, NT) - delta_ref[:, pl.ds(qs, n)])
                    dsb = ds_t.astype(BF16)
                    dv = dv + _dot(p_t.astype(BF16), dob, NN)
                    dk = dk + _dot(dsb, qh, NN)
                    dq_t_ref[:, pl.ds(qs, n)] += _dot(kh_t, dsb, NN)
                    dcq_ref[:, pl.ds(qs, n)] += jnp.sum(ds_t, axis=0, keepdims=True)
                    dck = dck + jnp.sum(ds_t, axis=-1, keepdims=True)
                    return dk, dv, dck

                zero = jnp.zeros((tq, hd), F32)
                carry = q_block(j, tq, (zero, zero, jnp.zeros((tq, 1), F32)), True)
                pairs = (nb - 1 - j) // 2
                carry = lax.fori_loop(0, pairs, lambda p, c: q_block(j + 1 + 2 * p, 2 * tq, c, False), carry)
                dk, dv, dck = lax.fori_loop(j + 1 + 2 * pairs, nb, lambda i, c: q_block(i, tq, c, False), carry)
                dk_ref[pl.ds(ks, tq), :] = dk
                dp_ref[pl.ds(ks, tq), 2 * hd:3 * hd] = dv.astype(dp_ref.dtype)
                dck_ref[pl.ds(ks, tq), :] = dck
                return 0

            lax.fori_loop(0, nb, kv_block, 0)

            dq, ggq = _head_rms_bwd(dq_t_ref[...].T * scale, qn, rq, gq_ref[...])
            dk, ggk = _head_rms_bwd(dk_ref[...], kn, rk, gk_ref[...])
            dp_ref[:, 0:hd] = dq.astype(dp_ref.dtype)
            dp_ref[:, hd:2 * hd] = dk.astype(dp_ref.dtype)
            for b in range(nb):
                sl = slice(b * tq, (b + 1) * tq)
                dc_ref[:, sl] = dcq_ref[:, sl] - _col_to_row(dck_ref[sl, :])

            @pl.when(h == 0)
            def _():
                ggq_ref[...] = jnp.zeros_like(ggq_ref)
                ggk_ref[...] = jnp.zeros_like(ggk_ref)

            ggq_ref[...] += ggq
            ggk_ref[...] += ggk

        host.run(h == 0, h == heads - 1, r_ins, r_outs, sems, compute)

    head_in = lambda part: pl.BlockSpec((t, hd), lambda h: (0, blk0 + 3 * h + part))
    vec = pl.BlockSpec((1, hd), lambda h: (0, 0))
    colb = pl.BlockSpec((None, t, LANES), lambda h: (h, 0, 0))
    res = pl.pallas_call(
        body,
        name="fox_bwd",
        grid=(heads,),
        in_specs=[
            head_in(0), head_in(1), head_in(2),
            pl.BlockSpec((t, hd), lambda h: (0, h)),
            pl.BlockSpec((t, hd), lambda h: (0, h)),
            vec, vec,
            pl.BlockSpec((None, 1, t), lambda h: (h, 0, 0)),
            colb,
            pl.BlockSpec((None, 1, t), lambda h: (h, 0, 0)),
            ANY,
        ] + host.in_specs,
        out_specs=[
            pl.BlockSpec((t, 3 * hd), lambda h: (0, blk0 // 3 + h)),
            pl.BlockSpec((None, 1, t), lambda h: (h, 0, 0)),
            vec, vec,
        ] + host.out_specs,
        out_shape=[
            jax.ShapeDtypeStruct(dproj.shape, dproj.dtype),
            jax.ShapeDtypeStruct((heads, 1, t), F32),
            jax.ShapeDtypeStruct((1, hd), F32),
            jax.ShapeDtypeStruct((1, hd), F32),
        ] + host.out_shapes,
        input_output_aliases={10: 0},
        scratch_shapes=[
            pltpu.VMEM((t, hd), BF16), pltpu.VMEM((t, hd), BF16), pltpu.VMEM((hd, t), BF16),
            pltpu.VMEM((hd, t), F32), pltpu.VMEM((t, hd), F32),
            pltpu.VMEM((1, t), F32), pltpu.VMEM((t, 1), F32), pltpu.VMEM((1, t), F32),
        ] + host.scratch,
        compiler_params=_params(("arbitrary",)),
    )(proj, proj, proj, o, do, gq.reshape(1, hd), gk.reshape(1, hd), c_row3, c_colb, lse, dproj, *host.ins)
    return res


def _mem_fwd(proj, off, kv, gq, gk, tq):
    t = proj.shape[0]
    m, width = kv.shape[0], kv.shape[1] // 2
    hd = width // MEM_HEADS
    tq = _tile(t, tq)
    blk0 = off // hd
    scale = 1.0 / math.sqrt(hd)

    def body(q_ref, k_ref, v_ref, gq_ref, gk_ref, o_ref):
        qn, _ = _head_rms(q_ref[...].astype(F32))
        kn, _ = _head_rms(k_ref[...])
        s = _dot((qn * gq_ref[...]).astype(BF16), (kn * gk_ref[...]).astype(BF16), NT) * scale
        p = jnp.exp(s - jnp.max(s, axis=-1, keepdims=True))
        p = p / jnp.sum(p, axis=-1, keepdims=True)
        o_ref[...] = _dot(p.astype(BF16), v_ref[...].astype(BF16), NN).astype(o_ref.dtype)

    vec = pl.BlockSpec((1, hd), lambda h, i: (0, 0))
    return pl.pallas_call(
        body,
        name="mem_fwd",
        grid=(MEM_HEADS, t // tq),
        in_specs=[
            pl.BlockSpec((tq, hd), lambda h, i: (i, blk0 + h)),
            pl.BlockSpec((m, hd), lambda h, i: (0, h)),
            pl.BlockSpec((m, hd), lambda h, i: (0, MEM_HEADS + h)),
            vec, vec,
        ],
        out_specs=pl.BlockSpec((tq, hd), lambda h, i: (i, h)),
        out_shape=jax.ShapeDtypeStruct((t, width), BF16),
        compiler_params=_params(("parallel", "parallel")),
    )(proj, kv, kv, gq.reshape(1, hd), gk.reshape(1, hd))


def _mem_bwd(proj, off, kv, do, gq, gk, tq, dproj, rider=None):
    t = proj.shape[0]
    m, width = kv.shape[0], kv.shape[1] // 2
    hd = width // MEM_HEADS
    tq = _tile(t, tq)
    nq = t // tq
    blk0 = off // hd
    scale = 1.0 / math.sqrt(hd)
    host = _Host(rider)

    def body(*refs):
        q_ref, k_ref, v_ref, do_ref, gq_ref, gk_ref = refs[:6]
        pos = 7
        r_ins = refs[pos:pos + host.n_in]; pos += host.n_in
        dq_ref, dk_ref, dv_ref, ggq_ref, ggk_ref = refs[pos:pos + 5]; pos += 5
        r_outs = refs[pos:pos + host.n_out]; pos += host.n_out
        dkh_ref, dvh_ref = refs[pos:pos + 2]; pos += 2
        sems = refs[pos:]
        h, i = pl.program_id(0), pl.program_id(1)

        def compute():
            qn, rq = _head_rms(q_ref[...].astype(F32))
            kn, rk = _head_rms(k_ref[...])
            qhat = (qn * gq_ref[...]).astype(BF16)
            khat = (kn * gk_ref[...]).astype(BF16)
            vb = v_ref[...].astype(BF16)
            dob = do_ref[...]
            s = _dot(qhat, khat, NT) * scale
            p = jnp.exp(s - jnp.max(s, axis=-1, keepdims=True))
            p = p / jnp.sum(p, axis=-1, keepdims=True)
            dp = _dot(dob, vb, NT)
            ds = p * (dp - jnp.sum(dp * p, axis=-1, keepdims=True))
            dsb = ds.astype(BF16)
            dq, ggq = _head_rms_bwd(_dot(dsb, khat, NN) * scale, qn, rq, gq_ref[...])
            dq_ref[...] = dq.astype(dq_ref.dtype)

            @pl.when(i == 0)
            def _():
                dkh_ref[...] = jnp.zeros_like(dkh_ref)
                dvh_ref[...] = jnp.zeros_like(dvh_ref)

            @pl.when(jnp.logical_and(h == 0, i == 0))
            def _():
                ggq_ref[...] = jnp.zeros_like(ggq_ref)
                ggk_ref[...] = jnp.zeros_like(ggk_ref)

            dkh_ref[...] += _dot(dsb, qhat, TN)
            dvh_ref[...] += _dot(p.astype(BF16), dob, TN)
            ggq_ref[...] += ggq

            @pl.when(i == nq - 1)
            def _():
                dk, ggk = _head_rms_bwd(dkh_ref[...] * scale, kn, rk, gk_ref[...])
                dk_ref[...] = dk.astype(dk_ref.dtype)
                dv_ref[...] = dvh_ref[...].astype(dv_ref.dtype)
                ggk_ref[...] += ggk

        first = jnp.logical_and(h == 0, i == 0)
        last = jnp.logical_and(h == MEM_HEADS - 1, i == nq - 1)
        host.run(first, last, r_ins, r_outs, sems, compute)

    vec = pl.BlockSpec((1, hd), lambda h, i: (0, 0))
    kblk = pl.BlockSpec((m, hd), lambda h, i: (0, h))
    res = pl.pallas_call(
        body,
        name="mem_bwd",
        grid=(MEM_HEADS, nq),
        in_specs=[
            pl.BlockSpec((tq, hd), lambda h, i: (i, blk0 + h)), kblk,
            pl.BlockSpec((m, hd), lambda h, i: (0, MEM_HEADS + h)),
            pl.BlockSpec((tq, hd), lambda h, i: (i, h)), vec, vec, ANY,
        ] + host.in_specs,
        out_specs=[pl.BlockSpec((tq, hd), lambda h, i: (i, blk0 + h)), kblk, kblk, vec, vec] + host.out_specs,
        out_shape=[
            jax.ShapeDtypeStruct(dproj.shape, dproj.dtype),
            jax.ShapeDtypeStruct((m, width), BF16),
            jax.ShapeDtypeStruct((m, width), BF16),
            jax.ShapeDtypeStruct((1, hd), F32),
            jax.ShapeDtypeStruct((1, hd), F32),
        ] + host.out_shapes,
        input_output_aliases={6: 0},
        scratch_shapes=[pltpu.VMEM((m, hd), F32), pltpu.VMEM((m, hd), F32)] + host.scratch,
        compiler_params=_params(("arbitrary", "arbitrary")),
    )(proj, kv, kv, do, gq.reshape(1, hd), gk.reshape(1, hd), dproj, *host.ins)
    dproj, dk, dv, ggq, ggk = res[:5]
    return (dproj, jnp.concatenate([dk, dv], axis=1), ggq.reshape(hd), ggk.reshape(hd), *res[5:])


def _sigmoid(z):
    return 1.0 / (1.0 + jnp.exp(-z))


def _merge_fwd(proj, o3, tm, tc):
    t, d = o3[0].shape
    tm = _tile(t, tm)

    def body(g_ref, oa_ref, ob_ref, oc_ref, out_ref):
        acc = jnp.zeros((tm, tc), F32)
        for s, o_ref in enumerate((oa_ref, ob_ref, oc_ref)):
            acc = acc + _sigmoid(g_ref[:, s * tc:(s + 1) * tc].astype(F32)) * o_ref[...].astype(F32)
        out_ref[...] = acc.astype(out_ref.dtype)

    blk = pl.BlockSpec((tm, tc), lambda i, j: (i, j))
    return pl.pallas_call(
        body,
        name="merge_fwd",
        grid=(t // tm, d // tc),
        in_specs=[pl.BlockSpec((tm, 3 * tc), lambda i, j: (i, j)), blk, blk, blk],
        out_specs=blk,
        out_shape=jax.ShapeDtypeStruct((t, d), BF16),
        compiler_params=_params(("parallel", "parallel")),
    )(proj, *o3)


def _merge_bwd(proj, o3, dx1, w_out, tm, tc):
    t, d = o3[0].shape
    k = dx1.shape[1]
    tm = _tile(t, tm)

    def body(dx_ref, w_ref, g_ref, oa_ref, ob_ref, oc_ref, dg_ref, da_ref, db_ref, dc_ref):
        dmf = _dot(dx_ref[...], w_ref[...], NT)
        for s, (o_ref, do_ref) in enumerate(((oa_ref, da_ref), (ob_ref, db_ref), (oc_ref, dc_ref))):
            g = _sigmoid(g_ref[:, s * tc:(s + 1) * tc].astype(F32))
            do_ref[...] = (dmf * g).astype(do_ref.dtype)
            dg_ref[:, s * tc:(s + 1) * tc] = (dmf * o_ref[...].astype(F32) * g * (1.0 - g)).astype(dg_ref.dtype)

    blk = pl.BlockSpec((tm, tc), lambda i, j: (i, j))
    wide = pl.BlockSpec((tm, 3 * tc), lambda i, j: (i, j))
    return pl.pallas_call(
        body,
        name="merge_bwd",
        grid=(t // tm, d // tc),
        in_specs=[pl.BlockSpec((tm, k), lambda i, j: (i, 0)), pl.BlockSpec((tc, k), lambda i, j: (j, 0)), wide, blk, blk, blk],
        out_specs=[wide, blk, blk, blk],
        out_shape=[jax.ShapeDtypeStruct(proj.shape, BF16)] + [jax.ShapeDtypeStruct((t, d), BF16)] * 3,
        compiler_params=_params(("parallel", "parallel")),
    )(dx1, w_out, proj, *o3)


def _w_in_chunks(d, tc):
    cw = d // 2
    heads = cw // FOX_HEAD_DIM
    conv0, fox0, f0, mq0, gate0 = 0, 3 * cw, 6 * cw, 6 * cw + heads, 7 * cw + heads
    chunks = [(gate0 + s * d + j * tc, gate0 + s * d + (j + 1) * tc) for j in range(d // tc) for s in range(N_BRANCHES)]
    chunks += [(conv0 + s * cw + j * LANES, conv0 + s * cw + (j + 1) * LANES) for j in range(cw // LANES) for s in range(3)]
    chunks += [(fox0 + s * cw + j * FOX_HEAD_DIM, fox0 + s * cw + (j + 1) * FOX_HEAD_DIM) for j in range(heads) for s in range(3)]
    chunks.append((mq0, mq0 + cw))
    return chunks, (f0, f0 + heads)


ROW_TILE = 16
GROUP = 128
GROUP_BACK = 112
SCRATCH_ROWS = 2 * GROUP + 32


def _padded_rows(r):
    return -(-r // GROUP_BACK) * GROUP_BACK


def _rows_from(scr_ref, use, q8, fine, g):
    x = scr_ref[pl.ds(pl.multiple_of(q8 * 8, 8), g + 8), :]
    for s in range(8):
        @pl.when(fine == s)
        def _(s=s):
            use((x if s == 0 else pltpu.roll(x, g + 8 - s, axis=0))[0:g])


def _assemble(name, tbl, grid, step, in_specs, out_spec, out_shape, operands, g, w1, cols_of):
    has_f = len(in_specs) == 3
    k = out_shape.shape[-1]
    c = cols_of

    def body(*refs):
        t_ref, s1_ref, s2_ref = refs[:3]
        f_ref = refs[3] if has_f else None
        out_ref = refs[3 + has_f]
        scr1, scr2, scrf = refs[4 + has_f:]
        t = step()

        def put(y):
            out_ref[...] = y.astype(out_ref.dtype)

        @pl.when(t == 0)
        def _():
            scr1[...] = jnp.zeros_like(scr1)
            scr2[...] = jnp.zeros_like(scr2)
            scrf[...] = jnp.zeros_like(scrf)

        rows = lax.broadcasted_iota(jnp.int32, (g, k), 0)
        n1, a2 = t_ref[c["n1"], t], t_ref[c["a2"], t]
        scr1[0:w1, :] = (s1_ref[0] if len(s1_ref.shape) == 3 else s1_ref[...]).astype(F32)
        _rows_from(scr1, put, t_ref[c["q1"], t], t_ref[c["s1"], t], g)

        @pl.when(a2 < g)
        def _():
            scr2[g:g + s2_ref.shape[0], :] = s2_ref[...].astype(F32)
            _rows_from(scr2, lambda y: put(jnp.where(rows < n1, out_ref[...].astype(F32), y)),
                       t_ref[c["q2"], t], t_ref[c["s2"], t], g)

        if has_f:
            fa, fb = t_ref[c["fa"], t], t_ref[c["fb"], t]

            @pl.when(fb > fa)
            def _():
                scrf[g:g + f_ref.shape[0], :] = f_ref[...].astype(F32)
                inside = jnp.logical_and(rows >= fa, rows < fb)
                _rows_from(scrf, lambda y: put(jnp.where(inside, y, out_ref[...].astype(F32))),
                           t_ref[c["qf"], t], t_ref[c["sf"], t], g)

            valid = t_ref[c["valid"], t]

            @pl.when(valid < g)
            def _():
                out_ref[...] = jnp.where(rows < valid, out_ref[...].astype(F32), 0.0).astype(out_ref.dtype)

    return pl.pallas_call(
        body,
        name=name,
        grid_spec=pltpu.PrefetchScalarGridSpec(
            num_scalar_prefetch=1, grid=grid, in_specs=in_specs, out_specs=out_spec,
            scratch_shapes=[pltpu.VMEM((SCRATCH_ROWS, k), F32)] * 3),
        out_shape=out_shape,
        compiler_params=_params(("arbitrary",) * len(grid)),
    )(jnp.asarray(tbl), *operands)


def _pack_w_in(w8, d, tc):
    blocks, rp, k = w8.shape
    chunks, (f_lo, f_hi) = _w_in_chunks(d, tc)
    r = max(hi for _, hi in chunks) // blocks
    g, w1 = GROUP, GROUP + ROW_TILE
    table = []
    for lo, hi in chunks:
        for g0 in range(lo, hi, g):
            b1, r1 = divmod(g0, r)
            n1 = min(g, r - r1)
            st1 = min(r1 // ROW_TILE * ROW_TILE, rp - w1)
            o1, o2 = r1 - st1, g - n1
            b2 = b1 + 1 if n1 < g else 0
            table.append((b1, st1, o1 // 8, o1 % 8, n1, n1, b2, o2 // 8, o2 % 8))
    names = ("b1", "st1", "q1", "s1", "n1", "a2", "b2", "q2", "s2")
    cols_of = {n: i for i, n in enumerate(names)}
    tbl = np.array(table, np.int32).T
    c = cols_of
    w_all = _assemble(
        "pack_w_in", tbl, (len(table),), lambda: pl.program_id(0),
        [pl.BlockSpec((pl.Element(1), pl.Element(w1), pl.Element(k)), lambda i, t: (t[c["b1"], i], pl.multiple_of(t[c["st1"], i], ROW_TILE), 0)),
         pl.BlockSpec((None, g, k), lambda i, t: (t[c["b2"], i], 0, 0))],
        pl.BlockSpec((g, k), lambda i, t: (i, 0)),
        jax.ShapeDtypeStruct((len(table) * g, k), w8.dtype), [w8, w8], g, w1, cols_of)
    fb, fr = divmod(f_lo, r)
    return w_all, jnp.pad(w8[fb, fr:fr + f_hi - f_lo], ((0, F_ROWS - (f_hi - f_lo)), (0, 0)))


def _unpack_g_in(g_all, g_f, d, tc, blocks):
    n_all, k = g_all.shape
    chunks, (f_lo, f_hi) = _w_in_chunks(d, tc)
    r = max(hi for _, hi in chunks) // blocks
    rp = _padded_rows(r)
    g, w1 = GROUP_BACK, GROUP_BACK + ROW_TILE
    pos, spans = 0, [(f_lo, f_hi, None)]
    for lo, hi in chunks:
        spans.append((lo, hi, pos))
        pos += hi - lo
    spans.sort()
    table = []
    for b in range(blocks):
        for l0 in range(0, rp, g):
            valid = max(0, min(g, r - l0))
            g0, segs, fa, fb, of = b * r + l0, [], 0, 0, 0
            for lo, hi, p in spans:
                a, e = max(lo, g0), min(hi, g0 + valid)
                if a < e and p is None:
                    fa, fb, of = a - g0, e - g0, g + (a - lo) - (a - g0)
                elif a < e:
                    segs.append((a - g0, p + a - lo, e - a))
            assert len(segs) <= 2 and (not segs or segs[0][0] == 0 or len(segs) == 1)
            first = segs[0] if segs and segs[0][0] == 0 else (0, 0, 0)
            second = segs[-1] if segs and segs[-1][0] > 0 else (g, 0, 0)
            st1 = min(first[1] // ROW_TILE * ROW_TILE, n_all - w1)
            o1, o2 = first[1] - st1, g - second[0]
            assert second[1] % GROUP == 0
            table.append((st1, o1 // 8, o1 % 8, first[2], second[0], second[1] // GROUP, o2 // 8, o2 % 8,
                          fa, fb, of // 8, of % 8, valid))
    names = ("st1", "q1", "s1", "n1", "a2", "j2", "q2", "s2", "fa", "fb", "qf", "sf", "valid")
    cols_of = {n: i for i, n in enumerate(names)}
    tbl = np.array(table, np.int32).T
    c, per = cols_of, rp // g
    return _assemble(
        "unpack_g_in", tbl, (blocks, per), lambda: pl.program_id(0) * per + pl.program_id(1),
        [pl.BlockSpec((pl.Element(w1), pl.Element(k)), lambda b, u, t: (pl.multiple_of(t[c["st1"], b * per + u], ROW_TILE), 0)),
         pl.BlockSpec((GROUP, k), lambda b, u, t: (t[c["j2"], b * per + u], 0)),
         pl.BlockSpec((F_ROWS, k), lambda b, u, t: (0, 0))],
        pl.BlockSpec((None, g, k), lambda b, u, t: (b, u, 0)),
        jax.ShapeDtypeStruct((blocks, rp, k), g_all.dtype), [g_all, g_all, g_f], g, w1, cols_of)


def _unblock(w8):
    return w8.transpose(1, 0, 2).reshape(w8.shape[1], -1)


def _tile2(r, cols, tr, tcols):
    if r % 8 == 0:
        return _tile(r, tr), cols
    return r, _tile(cols, tcols)


def _pair_sum(name, g8, got, c):
    def body(c_ref, g_ref, s_ref, o_ref):
        o_ref[...] = (g_ref[...].astype(F32) + s_ref[...].astype(F32)).astype(o_ref.dtype)

    if g8.ndim == 4:
        _, r, k1, k2 = g8.shape
        tr = max(cand for cand in range(1, 385) if r % cand == 0)
        grid = (N_CHIPS, r // tr)
        shape = (None, tr, k1, k2)
        own = pl.BlockSpec(shape, lambda q, i, c_ref: (2 * q + c_ref[0], i, 0, 0))
        blk = pl.BlockSpec(shape, lambda q, i, c_ref: (q, i, 0, 0))
    else:
        _, r, cols = g8.shape
        tr, tcols = _tile2(r, cols, 256, 256)
        grid = (N_CHIPS, r // tr, cols // tcols)
        own = pl.BlockSpec((None, tr, tcols), lambda q, i, j, c_ref: (2 * q + c_ref[0], i, j))
        blk = pl.BlockSpec((None, tr, tcols), lambda q, i, j, c_ref: (q, i, j))
    return pl.pallas_call(
        body,
        name=name,
        grid_spec=pltpu.PrefetchScalarGridSpec(num_scalar_prefetch=1, grid=grid, in_specs=[own, blk], out_specs=blk),
        out_shape=jax.ShapeDtypeStruct((N_CHIPS,) + g8.shape[1:], BF16),
        compiler_params=_params(("parallel",) * len(grid)),
    )(c, g8, got)


def _local_step(x, mem, target, w, small, comm=None):
    t, d = x.shape
    cw = d // 2
    heads = cw // FOX_HEAD_DIM
    tc = min(512, d)
    tq = min(512, t)
    off_conv, off_fox, off_mq = 3 * d, 3 * d + 3 * cw, 3 * d + 6 * cw
    w = dict(w)
    w_all, w_f = _pack_w_in(w["w_in"], d, tc)
    big = dict(tm=1024, tn=512, tk=2048)
    wide_k = dict(tm=512, tn=1024, tk=4096)
    tall = dict(tm=2048, tn=512, tk=2048)

    h = _rms_fwd("rms1_fwd", x, small["norm1_g"])
    if comm:
        early = ("w_conv_out", "w_fox_out", "w_mem_out", "w_out", "w_mem_kv")
        proj, *got = _matmul("proj", "nt", h, w_all, outs=[BF16], rider=_gather_rider([comm["shards"][n] for n in early], False), **tall)
        for n, val in zip(early, got):
            w[n] = _unblock(val) if n in COLUMN_SPLIT else val.reshape(-1, val.shape[-1])
    else:
        proj = _matmul("proj", "nt", h, w_all, outs=[BF16], **tall)
    z_row = _matmul("proj_f", "nt", w_f, h, outs=[F32], tm=F_ROWS, tn=512, tk=2048)

    y_conv = _conv_fwd(proj, off_conv, small["conv_w"], LANES)

    b_col = jnp.pad(small["b_f"], (0, F_ROWS - heads)).reshape(F_ROWS, 1)
    c_row3 = _forget_fwd(z_row, b_col)[:heads].reshape(heads, 1, t)
    c_colb = _rows_to_colb(c_row3, tq)
    if comm:
        y_fox, lse, got = _fox_fwd(proj, off_fox, small["fox_q_g"], small["fox_k_g"], c_row3, c_colb, heads, 2 * tq,
                                   rider=_gather_rider([comm["shards"]["w_up"]], False))
        w["w_up"] = _unblock(got)
    else:
        y_fox, lse = _fox_fwd(proj, off_fox, small["fox_q_g"], small["fox_k_g"], c_row3, c_colb, heads, 2 * tq)

    nm = _rms_fwd("mem_rms_fwd", mem, small["mem_norm_g"])
    kv = _matmul("mem_kv", "nn", nm, w["w_mem_kv"], outs=[F32], tm=256, tn=512, tk=2048)
    y_mem = _mem_fwd(proj, off_mq, kv, small["mem_q_g"], small["mem_k_g"], tq)

    ys = (y_conv, y_fox, y_mem)
    w_outs = (w["w_conv_out"], w["w_fox_out"], w["w_mem_out"])
    o3 = [_matmul(f"branch_out{s}", "nn", ys[s], w_outs[s], outs=[BF16], **big) for s in range(3)]
    merged = _merge_fwd(proj, o3, 512, tc)
    x1 = _matmul("out_proj", "nn", merged, w["w_out"], outs=[F32], extras=[x],
                 epilogue=lambda acc, xr: (acc + xr,), **big)
    h2 = _rms_fwd("rms2_fwd", x1, small["norm2_g"])

    def up_epilogue(acc):
        return acc, jnp.square(jnp.maximum(acc, 0.0))

    if comm:
        up, act, got = _matmul("mlp_up", "nn", h2, w["w_up"], outs=[BF16, BF16], epilogue=up_epilogue,
                               rider=_gather_rider([comm["shards"]["w_down"]], True), **big)
        w["w_down"] = got.reshape(-1, got.shape[-1])
    else:
        up, act = _matmul("mlp_up", "nn", h2, w["w_up"], outs=[BF16, BF16], epilogue=up_epilogue, **big)

    def loss_epilogue(acc, x1r, tr):
        dy = (acc + x1r - tr) * (1.0 / d)
        return dy, dy

    dy, dyb = _matmul("mlp_down", "nn", act, w["w_down"], outs=[F32, BF16], extras=[x1, target],
                      epilogue=loss_epilogue, tm=1024, tn=512, tk=4096)

    def dup_epilogue(acc, upr):
        return (acc * 2.0 * jnp.maximum(upr.astype(F32), 0.0),)

    def by_owner(g):
        return g.reshape(N_DEV, -1, g.shape[-1])

    g, parts = {}, {}
    g["w_down"] = _matmul("d_w_down", "tn", act, dyb, outs=[BF16], **wide_k)
    if comm:
        dup = _matmul("d_act", "nt", dyb, w["w_down"], outs=[BF16], extras=[up], epilogue=dup_epilogue, **tall)
        g["w_up"], got = _matmul("d_w_up", "tn", h2, dup, outs=[BF16], out_blocks=True,
                                 rider=_pair_rider([by_owner(g["w_down"])]), **wide_k)
        pair = _pair_sum("pair_w_down", by_owner(g["w_down"]), got, comm["c"])
        dh2, parts["w_down"], got = _matmul("d_h2", "nt", dup, w["w_up"], outs=[F32],
                                            rider=_join_riders(_chip_rider([pair]), _pair_rider([g["w_up"]])), **tall)
        pair_up = _pair_sum("pair_w_up", g["w_up"], got, comm["c"])
    else:
        dup = _matmul("d_act", "nt", dyb, w["w_down"], outs=[BF16], extras=[up], epilogue=dup_epilogue, **tall)
        g["w_up"] = _matmul("d_w_up", "tn", h2, dup, outs=[BF16], out_blocks=True, **wide_k)
        dh2 = _matmul("d_h2", "nt", dup, w["w_up"], outs=[F32], **tall)
    dx1, dx1b, g_norm2, dy_sq = _rms_bwd("rms2_bwd", dh2, x1, small["norm2_g"], res=dy)
    loss = dy_sq * (0.5 * d)

    g["w_out"] = _matmul("d_w_out", "tn", merged, dx1b, outs=[BF16], **wide_k)
    dproj, *do3 = _merge_bwd(proj, o3, dx1b, w["w_out"], 1024, tc)
    names = ("w_conv_out", "w_fox_out", "w_mem_out")
    dys = []
    for s in range(3):
        g[names[s]] = _matmul(f"d_w_branch{s}", "tn", ys[s], do3[s], outs=[BF16], out_blocks=True, **wide_k)
        dys.append(_matmul(f"d_branch{s}", "nt", do3[s], w_outs[s], outs=[BF16], **tall))

    dproj, dkv, g_mq, g_mk = _mem_bwd(proj, off_mq, kv, dys[2], small["mem_q_g"], small["mem_k_g"], tq, dproj)
    g["w_mem_kv"] = _matmul("d_w_mem_kv", "tn", nm, dkv, outs=[BF16], **wide_k)
    dnm = _matmul("d_mem_norm", "nt", dkv, w["w_mem_kv"], outs=[F32], tm=256, tn=512, tk=2048)
    _, _, g_mem_norm, _ = _rms_bwd("mem_rms_bwd", dnm, mem, small["mem_norm_g"])

    mid = ("w_out", "w_conv_out", "w_fox_out", "w_mem_out", "w_mem_kv")
    if comm:
        mid8 = [g[n] if n in names else by_owner(g[n]) for n in mid]
        dproj, g_conv_w, *got = _conv_bwd(proj, off_conv, small["conv_w"], dys[0], LANES, dproj, rider=_pair_rider(mid8))
        pairs_mid = [_pair_sum("pair_" + n, g8, s4, comm["c"]) for n, g8, s4 in zip(mid, mid8, got)]
        dproj, dc, g_fq, g_fk, parts["w_up"] = _fox_bwd(proj, off_fox, y_fox, dys[1], small["fox_q_g"], small["fox_k_g"], c_row3,
                                                        c_colb, lse, heads, tq, dproj, rider=_chip_rider([pair_up]))
    else:
        dproj, g_conv_w = _conv_bwd(proj, off_conv, small["conv_w"], dys[0], LANES, dproj)
        dproj, dc, g_fq, g_fk = _fox_bwd(proj, off_fox, y_fox, dys[1], small["fox_q_g"], small["fox_k_g"], c_row3, c_colb,
                                         lse, heads, tq, dproj)
    dc_row = jnp.pad(dc.reshape(heads, t), ((0, F_ROWS - heads), (0, 0)))
    dz_row, db = _forget_bwd(z_row, b_col, dc_row)

    if comm:
        g_all, *got = _matmul("d_w_in", "tn", dproj, h, outs=[BF16], j_outer=True, rider=_chip_rider(pairs_mid), **wide_k)
        parts.update(zip(mid, got))
    else:
        g_all = _matmul("d_w_in", "tn", dproj, h, outs=[BF16], j_outer=True, **wide_k)
    g_wf = _matmul("d_w_f", "nn", dz_row, h, outs=[BF16], tm=F_ROWS, tn=512, tk=4096)
    g["w_in"] = _unpack_g_in(g_all, g_wf, d, tc, w["w_in"].shape[0])
    dh = _matmul("d_h_f", "tn", dz_row, w_f, outs=[F32], tm=1024, tn=512, tk=F_ROWS)
    add_prev = lambda acc, prev: (acc + prev,)
    if comm:
        g_in8 = g["w_in"]
        got = _run_rider("pair_exchange_w_in", _pair_rider([g_in8]))[0]
        pair = _pair_sum("pair_w_in", g_in8, got, comm["c"])
        dh, parts["w_in"] = _matmul("d_h", "nn", dproj, w_all, outs=[F32], extras=[dh], epilogue=add_prev,
                                    rider=_chip_rider([pair]), tm=1024, tn=512, tk=3328)
    else:
        dh = _matmul("d_h", "nn", dproj, w_all, outs=[F32], extras=[dh], epilogue=add_prev, tm=1024, tn=512, tk=3328)
    grad_x, _, g_norm1, _ = _rms_bwd("rms1_bwd", dh, x, small["norm1_g"], res=dx1)

    gs = dict(norm1_g=g_norm1, b_f=db[:heads, 0], conv_w=g_conv_w, fox_q_g=g_fq.reshape(-1), fox_k_g=g_fk.reshape(-1),
              mem_norm_g=g_mem_norm, mem_q_g=g_mq, mem_k_g=g_mk, norm2_g=g_norm2)
    return loss, grad_x, (parts if comm else g), gs


def _adamw_math(w, g, m, v):
    m = ADAM_B1 * m + (1.0 - ADAM_B1) * g
    v = ADAM_B2 * v + (1.0 - ADAM_B2) * jnp.square(g)
    m_hat = m / (1.0 - ADAM_B1 ** ADAM_STEP)
    v_hat = v / (1.0 - ADAM_B2 ** ADAM_STEP)
    delta = -ADAM_LR * (m_hat / (jnp.sqrt(v_hat) + ADAM_EPS) + ADAM_WD * w)
    return delta, m, v


def _adamw(name, parts, w, m, v):
    r, c = w.shape
    n_parts, rp = parts.shape[:2]
    if rp == r:
        tr, tc = _tile2(r, c, 128, 256)
    else:
        tr, tc = _tile(rp, 256), _tile(c, 1024)

    def body(p_ref, w_ref, m_ref, v_ref, g_ref, d_ref, nm_ref, nv_ref):
        g = p_ref[0].astype(F32)
        for s in range(1, n_parts):
            g = g + p_ref[s].astype(F32)
        delta, nm, nv = _adamw_math(w_ref[...], g, m_ref[...], v_ref[...])
        g_ref[...] = g
        d_ref[...] = delta
        nm_ref[...] = nm
        nv_ref[...] = nv

    blk = pl.BlockSpec((tr, tc), lambda i, j: (i, j))
    return pl.pallas_call(
        body,
        name=name,
        grid=(rp // tr, c // tc),
        in_specs=[pl.BlockSpec((n_parts, tr, tc), lambda i, j: (0, i, j)), blk, blk, blk],
        out_specs=[blk] * 4,
        out_shape=[jax.ShapeDtypeStruct((r, c), F32)] * 4,
        compiler_params=_params(("parallel", "parallel")),
    )(parts, w, m, v)


def _sum_parts(name, parts):
    n_parts, r, c = parts.shape

    def body(p_ref, o_ref):
        acc = p_ref[0]
        for s in range(1, n_parts):
            acc = acc + p_ref[s]
        o_ref[...] = acc

    return pl.pallas_call(body, name=name, out_shape=jax.ShapeDtypeStruct((r, c), F32))(parts)


BIG = ("w_in", "w_mem_kv", "w_conv_out", "w_fox_out", "w_mem_out", "w_out", "w_up", "w_down")
COLUMN_SPLIT = ("w_in", "w_conv_out", "w_fox_out", "w_mem_out", "w_up")
SMALL = ("norm1_g", "b_f", "conv_w", "fox_q_g", "fox_k_g", "mem_norm_g", "mem_q_g", "mem_k_g", "norm2_g")
WEIGHTS = ("norm1_g", "w_in", "b_f", "conv_w", "fox_q_g", "fox_k_g", "mem_norm_g", "w_mem_kv", "mem_q_g", "mem_k_g",
           "w_conv_out", "w_fox_out", "w_mem_out", "w_out", "norm2_g", "w_up", "w_down")


def _pack(vectors):
    rows = []
    for vec in vectors:
        n = vec.shape[0]
        rows.append(jnp.pad(vec, (0, -n % LANES)).reshape(-1, LANES))
    out = jnp.concatenate(rows, axis=0)
    return jnp.pad(out, ((0, -out.shape[0] % 8), (0, 0)))


def _unpack(packed, sizes):
    out, row = [], 0
    for n in sizes:
        nr = -(-n // LANES)
        out.append(packed[row:row + nr].reshape(-1)[:n])
        row += nr
    return out


def kernel(x, mem, norm1_g, w_in, b_f, conv_w, fox_q_g, fox_k_g, mem_norm_g, w_mem_kv, mem_q_g, mem_k_g, w_conv_out, w_fox_out, w_mem_out, w_out, norm2_g, w_up, w_down, loss_target, m_norm1_g, m_w_in, m_b_f, m_conv_w, m_fox_q_g, m_fox_k_g, m_mem_norm_g, m_w_mem_kv, m_mem_q_g, m_mem_k_g, m_w_conv_out, m_w_fox_out, m_w_mem_out, m_w_out, m_norm2_g, m_w_up, m_w_down, v_norm1_g, v_w_in, v_b_f, v_conv_w, v_fox_q_g, v_fox_k_g, v_mem_norm_g, v_w_mem_kv, v_mem_q_g, v_mem_k_g, v_w_conv_out, v_w_fox_out, v_w_mem_out, v_w_out, v_norm2_g, v_w_up, v_w_down):
    args = dict(locals())
    wts = {n: args[n] for n in WEIGHTS}
    ms = {n: args["m_" + n] for n in WEIGHTS}
    vs = {n: args["v_" + n] for n in WEIGHTS}
    x_pos, y_pos, c_pos = _position()
    me = _index(x_pos, y_pos, c_pos)

    shards = {n: wts[n].astype(BF16) for n in BIG if n != "w_in"}
    rows_in = w_in.shape[1]
    shards["w_in"] = jnp.pad(w_in.T.astype(BF16), ((0, _padded_rows(rows_in) - rows_in), (0, 0)))
    wi, cw8 = _run_rider("all_gather_first", _gather_rider([shards["w_in"], conv_w], True))
    full = {"w_in": wi}
    small = {n: wts[n] for n in SMALL}
    small["conv_w"] = _unblock(cw8)
    comm = {"shards": shards, "c": c_pos.astype(jnp.int32).reshape(1)}

    loss, grad_x, parts, gs = _local_step(x[0], mem[0], loss_target[0], full, small, comm)

    out_g, out_d, out_m, out_v = {}, {}, {}, {}
    for n in BIG:
        if n == "w_in":
            res = _adamw("adamw_" + n, parts[n], wts[n].T, ms[n].T, vs[n].T)
            out_g[n], out_d[n], out_m[n], out_v[n] = (r.T for r in res)
        else:
            out_g[n], out_d[n], out_m[n], out_v[n] = _adamw("adamw_" + n, parts[n], wts[n], ms[n], vs[n])

    small_sizes = [int(math.prod(gs[n].shape)) for n in SMALL]
    packed = _pack([gs[n].reshape(-1) for n in SMALL])
    gsum = _sum_parts("sum_small", _run_rider("exchange_small", _broadcast_rider([packed]))[0])
    gsmall = dict(zip(SMALL, _unpack(gsum, small_sizes)))
    cols = conv_w.shape[1]
    gsmall["conv_w"] = lax.dynamic_slice(gsmall["conv_w"].reshape(CONV_TAPS, -1), (0, me * cols), (CONV_TAPS, cols)).reshape(-1)
    pg, pw, pm, pv = (_pack([src[n].reshape(-1) for n in SMALL]) for src in (gsmall, wts, ms, vs))
    _, sd, sm, sv = _adamw("adamw_small", pg[None], pw, pm, pv)
    local_sizes = [int(math.prod(wts[n].shape)) for n in SMALL]
    for dst, src in ((out_d, sd), (out_m, sm), (out_v, sv)):
        for n, val in zip(SMALL, _unpack(src, local_sizes)):
            dst[n] = val.reshape(wts[n].shape)
    for n in SMALL:
        out_g[n] = gsmall[n].reshape(wts[n].shape)

    loss = lax.psum(loss, MESH_AXES)
    return (loss, grad_x[None], *[out_g[n] for n in WEIGHTS], *[out_d[n] for n in WEIGHTS],
            *[out_m[n] for n in WEIGHTS], *[out_v[n] for n in WEIGHTS])
```

```python
import math

import numpy as np
import jax
import jax.numpy as jnp
from jax import lax
from jax.experimental import pallas as pl
from jax.experimental.pallas import tpu as pltpu

F32 = jnp.float32
BF16 = jnp.bfloat16

EPS = 1e-6
N_DEV = 8
N_CHIPS = 4
FOX_HEAD_DIM = 128
MEM_HEADS = 4
CONV_TAPS = 3
N_BRANCHES = 3
F_ROWS = 16

ADAM_LR = 0.001
ADAM_B1 = 0.9
ADAM_B2 = 0.999
ADAM_EPS = 1e-08
ADAM_WD = 0.01
ADAM_STEP = 10

V7X_VMEM_BYTES = 64 * 1024 * 1024
VMEM_LIMIT = V7X_VMEM_BYTES * 3 // 4
LANES = 128
NEG = -1e30
MAX_KEYS = 1024

MESH_AXES = ("x", "y", "c")
MESH = pl.DeviceIdType.MESH
ANY = pl.BlockSpec(memory_space=pl.ANY)

NN = (((1,), (0,)), ((), ()))
NT = (((1,), (1,)), ((), ()))
TN = (((0,), (0,)), ((), ()))


def _params(sem):
    return pltpu.CompilerParams(dimension_semantics=sem, vmem_limit_bytes=VMEM_LIMIT)


def _dot(a, b, dn):
    return lax.dot_general(a, b, dn, preferred_element_type=F32)


def _tile(n, t):
    if n <= t:
        return n
    for step in (LANES, 16):
        for cand in range(t - t % step, 0, -step):
            if n % cand == 0:
                return cand
    raise ValueError((n, t))


class _Rider:
    def __init__(self, ins, out_shapes, sem_shapes, start, finish, middle=None):
        self.ins, self.out_shapes, self.sem_shapes = list(ins), list(out_shapes), list(sem_shapes)
        self.start, self.finish, self.middle = start, finish, middle


def _position():
    return lax.axis_index("x"), lax.axis_index("y"), lax.axis_index("c")


def _index(px, py, pc):
    return 4 * px + 2 * py + pc


def _dma_sems(n, per):
    return [pltpu.SemaphoreType.DMA((n, per)), pltpu.SemaphoreType.DMA((n, per)), pltpu.SemaphoreType.DMA((n,))]


def _gather_rider(shards, pass_on):
    n = len(shards)

    def copies(ins, outs, sems):
        send_sems, recv_sems, local_sems = sems
        x, y, c = _position()
        me, sibling = (x, y, c), (x, y, 1 - c)
        chips = [(1 - x, y), (x, 1 - y), (1 - x, 1 - y)]

        def copy(a, k, block, to, src=None, k_send=None):
            rows = outs[a].at[_index(*block)]
            return pltpu.make_async_remote_copy(
                src_ref=rows if src is None else src, dst_ref=rows,
                send_sem=send_sems.at[a, k if k_send is None else k_send], recv_sem=recv_sems.at[a, k],
                device_id=to, device_id_type=MESH)

        mine = [pltpu.make_async_copy(ins[a], outs[a].at[_index(*me)], local_sems.at[a]) for a in range(n)]
        first = []
        for a in range(n):
            first.append(copy(a, 0, me, sibling, src=ins[a]))
            first += [copy(a, 1 + j, me, (*chips[j], c), src=ins[a]) for j in range(2 if pass_on else 3)]
        return copy, mine, first, me, sibling, chips, c

    def start(ins, outs, sems):
        _, mine, first, *_ = copies(ins, outs, sems)
        for cp in mine + first:
            cp.start()

    def by_kind(c, fn):
        if pass_on:
            pl.when(c == 1)(lambda: fn(0, 1))
            pl.when(c == 0)(lambda: fn(1, 0))
        else:
            fn(0, 1)

    def onward(copy, a, j_on, j_to, chips, c, sibling):
        third = [copy(a, 3, (*chips[j_on], c), (*chips[j_to], c), k_send=7)] if pass_on else []
        return third + [copy(a, 4 + j_on, (*chips[j_on], c), sibling)], [copy(a, 4 + j_to, (*chips[j_to], c), sibling)]

    def middle(ins, outs, sems):
        copy, _, _, me, sibling, chips, c = copies(ins, outs, sems)

        def fn(j_on, j_to):
            for a in range(n):
                for j, after in zip((j_on, j_to), onward(copy, a, j_on, j_to, chips, c, sibling)):
                    copy(a, 1 + j, (*chips[j], c), me).wait_recv()
                    for cp in after:
                        cp.start()

        by_kind(c, fn)

    def finish(ins, outs, sems):
        copy, mine, first, me, sibling, chips, c = copies(ins, outs, sems)

        def fn(j_on, j_to):
            passed = [cp for a in range(n) for after in onward(copy, a, j_on, j_to, chips, c, sibling) for cp in after]
            for a in range(n):
                copy(a, 3, (*chips[2], c), me).wait_recv()
                passed.append(copy(a, 6, (*chips[2], c), sibling))
                passed[-1].start()
            for a in range(n):
                copy(a, 0, sibling, me).wait_recv()
                for j, chip in enumerate(chips):
                    copy(a, 4 + j, (*chip, 1 - c), me).wait_recv()
            for cp in first + passed:
                cp.wait_send()
            for cp in mine:
                cp.wait()

        by_kind(c, fn)

    out_shapes = [jax.ShapeDtypeStruct((N_DEV,) + s.shape, s.dtype) for s in shards]
    return _Rider(shards, out_shapes, _dma_sems(n, 8), start, finish, middle)


def _pair_rider(grads):
    n = len(grads)

    def copies(ins, outs, sems):
        send_sems, recv_sems, _ = sems
        x, y, c = _position()
        return [pltpu.make_async_remote_copy(
            src_ref=ins[a].at[2 * q + (1 - c)], dst_ref=outs[a].at[q],
            send_sem=send_sems.at[a, q], recv_sem=recv_sems.at[a, q], device_id=(x, y, 1 - c), device_id_type=MESH)
            for a in range(n) for q in range(N_CHIPS)]

    def start(ins, outs, sems):
        for cp in copies(ins, outs, sems):
            cp.start()

    def finish(ins, outs, sems):
        cps = copies(ins, outs, sems)
        for cp in cps:
            cp.wait_recv()
        for cp in cps:
            cp.wait_send()

    out_shapes = [jax.ShapeDtypeStruct((N_CHIPS,) + g.shape[1:], g.dtype) for g in grads]
    return _Rider(grads, out_shapes, _dma_sems(n, N_CHIPS), start, finish)


def _chip_rider(parts):
    n = len(parts)

    def copies(ins, outs, sems):
        send_sems, recv_sems, local_sems = sems
        x, y, c = _position()
        q_me = 2 * x + y
        chips = [(1 - x, y), (x, 1 - y), (1 - x, 1 - y)]
        mine = [pltpu.make_async_copy(ins[a].at[q_me], outs[a].at[q_me], local_sems.at[a]) for a in range(n)]
        sends, arrivals = [], []
        for a in range(n):
            for j, (tx, ty) in enumerate(chips):
                q_t = 2 * tx + ty
                sends.append(pltpu.make_async_remote_copy(
                    src_ref=ins[a].at[q_t], dst_ref=outs[a].at[q_me],
                    send_sem=send_sems.at[a, j], recv_sem=recv_sems.at[a, j], device_id=(tx, ty, c), device_id_type=MESH))
                arrivals.append(pltpu.make_async_remote_copy(
                    src_ref=ins[a].at[q_t], dst_ref=outs[a].at[q_t],
                    send_sem=send_sems.at[a, j], recv_sem=recv_sems.at[a, j], device_id=(tx, ty, c), device_id_type=MESH))
        return mine, sends, arrivals

    def start(ins, outs, sems):
        mine, sends, _ = copies(ins, outs, sems)
        for cp in mine + sends:
            cp.start()

    def finish(ins, outs, sems):
        mine, sends, arrivals = copies(ins, outs, sems)
        for cp in arrivals:
            cp.wait_recv()
        for cp in sends:
            cp.wait_send()
        for cp in mine:
            cp.wait()

    out_shapes = [jax.ShapeDtypeStruct(p.shape, p.dtype) for p in parts]
    return _Rider(parts, out_shapes, _dma_sems(n, 3), start, finish)


def _broadcast_rider(values):
    n = len(values)

    def copies(ins, outs, sems):
        send_sems, recv_sems, local_sems = sems
        x, y, c = _position()
        me = _index(x, y, c)

        def peer(k):
            return (1 - x if k & 4 else x, 1 - y if k & 2 else y, 1 - c if k & 1 else c)

        mine = [pltpu.make_async_copy(ins[a], outs[a].at[me], local_sems.at[a]) for a in range(n)]
        sends, arrivals = [], []
        for a in range(n):
            for k in range(1, N_DEV):
                common = dict(send_sem=send_sems.at[a, k - 1], recv_sem=recv_sems.at[a, k - 1], device_id=peer(k), device_id_type=MESH)
                sends.append(pltpu.make_async_remote_copy(src_ref=ins[a], dst_ref=outs[a].at[me], **common))
                arrivals.append(pltpu.make_async_remote_copy(src_ref=ins[a], dst_ref=outs[a].at[_index(*peer(k))], **common))
        return mine, sends, arrivals

    def start(ins, outs, sems):
        mine, sends, _ = copies(ins, outs, sems)
        for cp in mine + sends:
            cp.start()

    def finish(ins, outs, sems):
        mine, sends, arrivals = copies(ins, outs, sems)
        for cp in arrivals:
            cp.wait_recv()
        for cp in sends:
            cp.wait_send()
        for cp in mine:
            cp.wait()

    out_shapes = [jax.ShapeDtypeStruct((N_DEV,) + v.shape, v.dtype) for v in values]
    return _Rider(values, out_shapes, _dma_sems(n, 7), start, finish)


def _join_riders(*riders):
    def each(fn_name, ins, outs, sems):
        i = o = s = 0
        for r in riders:
            n_i, n_o, n_s = len(r.ins), len(r.out_shapes), len(r.sem_shapes)
            if getattr(r, fn_name) is not None:
                getattr(r, fn_name)(ins[i:i + n_i], outs[o:o + n_o], sems[s:s + n_s])
            i, o, s = i + n_i, o + n_o, s + n_s

    middle = (lambda ins, outs, sems: each("middle", ins, outs, sems)) if any(r.middle for r in riders) else None
    return _Rider([a for r in riders for a in r.ins], [a for r in riders for a in r.out_shapes],
                  [a for r in riders for a in r.sem_shapes],
                  lambda ins, outs, sems: each("start", ins, outs, sems),
                  lambda ins, outs, sems: each("finish", ins, outs, sems), middle)


def _run_rider(name, rider):
    n_in, n_out = len(rider.ins), len(rider.out_shapes)

    def body(*refs):
        ins, outs, sems = refs[:n_in], refs[n_in:n_in + n_out], refs[n_in + n_out:]
        rider.start(ins, outs, sems)
        if rider.middle is not None:
            rider.middle(ins, outs, sems)
        rider.finish(ins, outs, sems)

    return pl.pallas_call(
        body, name=name, in_specs=[ANY] * n_in, out_specs=[ANY] * n_out, out_shape=rider.out_shapes,
        scratch_shapes=rider.sem_shapes)(*rider.ins)


class _Host:
    def __init__(self, rider):
        self.rider = rider
        self.n_in = len(rider.ins) if rider else 0
        self.n_out = len(rider.out_shapes) if rider else 0
        self.n_sem = len(rider.sem_shapes) if rider else 0
        self.ins = rider.ins if rider else []
        self.in_specs = [ANY] * self.n_in
        self.out_specs = [ANY] * self.n_out
        self.out_shapes = rider.out_shapes if rider else []
        self.scratch = rider.sem_shapes if rider else []

    def run(self, first, last, ins, outs, sems, compute, midway=None):
        if self.rider is None:
            compute()
            return

        @pl.when(first)
        def _():
            self.rider.start(ins, outs, sems)

        compute()
        if self.rider.middle is not None and midway is not None:
            pl.when(midway)(lambda: self.rider.middle(ins, outs, sems))

        @pl.when(last)
        def _():
            if self.rider.middle is not None and midway is None:
                self.rider.middle(ins, outs, sems)
            self.rider.finish(ins, outs, sems)


def _matmul(name, kind, a, b, *, tm, tn, tk, outs, epilogue=None, extras=(), out_blocks=False, rider=None, j_outer=False):
    if kind == "nn":
        (m, kdim), n = a.shape, b.shape[1]
    elif kind == "nt":
        (m, kdim), n = a.shape, b.shape[0]
    else:
        (kdim, m), n = a.shape, b.shape[1]
    if out_blocks:
        tn = min(tn, n // N_DEV)
    tm, tn, tk = _tile(m, tm), _tile(n, tn), _tile(kdim, tk)
    ni, nj, nk = m // tm, n // tn, kdim // tk

    def spec(shape, fn):
        return pl.BlockSpec(shape, (lambda g0, g1, k: fn(g1, g0, k)) if j_outer else fn)

    a_spec = spec((tk, tm), lambda i, j, k: (k, i)) if kind == "tn" else spec((tm, tk), lambda i, j, k: (i, k))
    b_spec = spec((tn, tk), lambda i, j, k: (j, k)) if kind == "nt" else spec((tk, tn), lambda i, j, k: (k, j))
    dn = {"nn": NN, "nt": NT, "tn": TN}[kind]

    tile_spec = spec((tm, tn), lambda i, j, k: (i, j))
    if out_blocks:
        width = n // N_DEV
        r_out = width // tn
        out_shape = [jax.ShapeDtypeStruct((N_DEV, m, width), dt) for dt in outs]
        out_specs = [spec((None, tm, tn), lambda i, j, k: (j // r_out, i, j % r_out)) for _ in outs]
    else:
        out_shape = [jax.ShapeDtypeStruct((m, n), dt) for dt in outs]
        out_specs = [tile_spec for _ in outs]
    n_ex, n_out = len(extras), len(outs)
    host = _Host(rider)
    n_acc = 1 if nk > 1 else 0

    def body(*refs):
        a_ref, b_ref = refs[0], refs[1]
        pos = 2
        ex_refs = refs[pos:pos + n_ex]; pos += n_ex
        r_ins = refs[pos:pos + host.n_in]; pos += host.n_in
        out_refs = refs[pos:pos + n_out]; pos += n_out
        r_outs = refs[pos:pos + host.n_out]; pos += host.n_out
        acc_ref = refs[pos] if n_acc else None
        sems = refs[pos + n_acc:]
        i, j, k = pl.program_id(1 if j_outer else 0), pl.program_id(0 if j_outer else 1), pl.program_id(2)

        def finish_tile(acc):
            vals = (acc,) if epilogue is None else epilogue(acc, *[e[...] for e in ex_refs])
            for o_ref, v in zip(out_refs, vals):
                o_ref[...] = v.astype(o_ref.dtype)

        def compute():
            part = _dot(a_ref[...], b_ref[...], dn)
            if nk == 1:
                finish_tile(part)
                return

            @pl.when(k == 0)
            def _():
                acc_ref[...] = part

            @pl.when(jnp.logical_and(k > 0, k < nk - 1))
            def _():
                acc_ref[...] += part

            @pl.when(k == nk - 1)
            def _():
                finish_tile(acc_ref[...] + part)

        first = jnp.logical_and(jnp.logical_and(i == 0, j == 0), k == 0)
        last = jnp.logical_and(jnp.logical_and(i == ni - 1, j == nj - 1), k == nk - 1)
        step = (pl.program_id(0) * (ni if j_outer else nj) + pl.program_id(1)) * nk + k
        host.run(first, last, r_ins, r_outs, sems, compute, midway=step == (ni * nj * nk * 3) // 5)

    sem = ("arbitrary",) * 3 if rider else ("parallel", "parallel", "arbitrary")
    res = pl.pallas_call(
        body,
        name=name,
        grid=(nj, ni, nk) if j_outer else (ni, nj, nk),
        in_specs=[a_spec, b_spec] + [tile_spec for _ in extras] + host.in_specs,
        out_specs=out_specs + host.out_specs,
        out_shape=out_shape + host.out_shapes,
        scratch_shapes=([pltpu.VMEM((tm, tn), F32)] if n_acc else []) + host.scratch,
        compiler_params=_params(sem),
    )(a, b, *extras, *host.ins)
    return res[0] if len(res) == 1 else res


def _rms_fwd(name, x, g, tm=512):
    t, d = x.shape
    tm = _tile(t, tm)

    def body(x_ref, g_ref, h_ref):
        xf = x_ref[...]
        r = lax.rsqrt(jnp.mean(xf * xf, axis=-1, keepdims=True) + EPS)
        h_ref[...] = (xf * r * g_ref[...]).astype(h_ref.dtype)

    return pl.pallas_call(
        body,
        name=name,
        grid=(t // tm,),
        in_specs=[pl.BlockSpec((tm, d), lambda i: (i, 0)), pl.BlockSpec((1, d), lambda i: (0, 0))],
        out_specs=pl.BlockSpec((tm, d), lambda i: (i, 0)),
        out_shape=jax.ShapeDtypeStruct((t, d), BF16),
        compiler_params=_params(("parallel",)),
    )(x, g.reshape(1, d))


def _rms_bwd(name, dh, x, g, res=None, tm=256):
    t, d = x.shape
    tm = _tile(t, tm)
    has_res = res is not None

    def body(*refs):
        if has_res:
            dh_ref, x_ref, g_ref, res_ref, dx_ref, dxb_ref, gg_ref, ss_ref = refs
        else:
            dh_ref, x_ref, g_ref, dx_ref, dxb_ref, gg_ref, ss_ref = refs
        i = pl.program_id(0)
        xf = x_ref[...]
        r = lax.rsqrt(jnp.mean(xf * xf, axis=-1, keepdims=True) + EPS)
        xh = xf * r
        dhf = dh_ref[...].astype(F32)
        dxh = dhf * g_ref[...]
        dx = r * (dxh - xh * jnp.mean(dxh * xh, axis=-1, keepdims=True))

        @pl.when(i == 0)
        def _():
            gg_ref[...] = jnp.zeros_like(gg_ref)
            ss_ref[...] = jnp.zeros_like(ss_ref)

        if has_res:
            resf = res_ref[...]
            dx = dx + resf
            ss_ref[...] += jnp.sum(jnp.sum(resf * resf, axis=0, keepdims=True), axis=1, keepdims=True)
        dx_ref[...] = dx
        dxb_ref[...] = dx.astype(BF16)
        gg_ref[...] += jnp.sum(dhf * xh, axis=0, keepdims=True)

    row = pl.BlockSpec((tm, d), lambda i: (i, 0))
    vec = pl.BlockSpec((1, d), lambda i: (0, 0))
    one = pl.BlockSpec((1, 1), lambda i: (0, 0))
    ins = [dh, x, g.reshape(1, d)] + ([res] if has_res else [])
    dx, dxb, gg, ss = pl.pallas_call(
        body,
        name=name,
        grid=(t // tm,),
        in_specs=[row, row, vec] + ([row] if has_res else []),
        out_specs=[row, row, vec, one],
        out_shape=[jax.ShapeDtypeStruct((t, d), F32), jax.ShapeDtypeStruct((t, d), BF16), jax.ShapeDtypeStruct((1, d), F32),
                   jax.ShapeDtypeStruct((1, 1), F32)],
        compiler_params=_params(("arbitrary",)),
    )(*ins)
    return dx, dxb, gg.reshape(d), ss[0, 0]


def _head_rms(xf):
    r = lax.rsqrt(jnp.mean(xf * xf, axis=-1, keepdims=True) + EPS)
    return xf * r, r


def _head_rms_bwd(dy, xn, r, g):
    dxh = dy * g
    dx = r * (dxh - xn * jnp.mean(dxh * xn, axis=-1, keepdims=True))
    return dx, jnp.sum(dy * xn, axis=0, keepdims=True)


def _col_to_row(col):
    n = col.shape[0]
    eye = lax.broadcasted_iota(jnp.int32, (n, n), 0) == lax.broadcasted_iota(jnp.int32, (n, n), 1)
    return jnp.sum(jnp.where(eye, col, 0.0), axis=0, keepdims=True)


def _row_to_col(row):
    n = row.shape[1]
    eye = lax.broadcasted_iota(jnp.int32, (n, n), 0) == lax.broadcasted_iota(jnp.int32, (n, n), 1)
    return jnp.sum(jnp.where(eye, row, 0.0), axis=1, keepdims=True)


def _dproj_args(dproj, n_in):
    if dproj is None:
        return [], [], {}
    return [dproj], [ANY], {n_in: 0}


def _shift_down(u, s, rows):
    return jnp.where(rows >= s, pltpu.roll(u, s, axis=0), 0.0)


def _shift_up(u, s, rows, t):
    return jnp.where(rows < t - s, pltpu.roll(u, t - s, axis=0), 0.0)


def _conv_fwd(proj, off, conv_w, cb):
    t = proj.shape[0]
    c = conv_w.shape[1]
    blk0 = off // (3 * cb)

    def body(p_ref, w_ref, y_ref):
        rows = lax.broadcasted_iota(jnp.int32, (t, cb), 0)
        bg = p_ref[:, 0:cb].astype(F32)
        u = p_ref[:, cb:2 * cb].astype(F32) * p_ref[:, 2 * cb:3 * cb].astype(F32)
        w = w_ref[...]
        conv = w[2:3] * u + w[1:2] * _shift_down(u, 1, rows) + w[0:1] * _shift_down(u, 2, rows)
        y_ref[...] = (bg * conv).astype(y_ref.dtype)

    return pl.pallas_call(
        body,
        name="conv_fwd",
        grid=(c // cb,),
        in_specs=[pl.BlockSpec((t, 3 * cb), lambda j: (0, blk0 + j)), pl.BlockSpec((CONV_TAPS, cb), lambda j: (0, j))],
        out_specs=pl.BlockSpec((t, cb), lambda j: (0, j)),
        out_shape=jax.ShapeDtypeStruct((t, c), BF16),
        compiler_params=_params(("parallel",)),
    )(proj, conv_w)


def _conv_bwd(proj, off, conv_w, dy, cb, dproj, rider=None):
    t = proj.shape[0]
    c = conv_w.shape[1]
    blk0 = off // (3 * cb)
    nj = c // cb
    host = _Host(rider)

    def body(*refs):
        p_ref, w_ref, dy_ref = refs[:3]
        r_ins = refs[4:4 + host.n_in]
        dp_ref, gw_ref = refs[4 + host.n_in:6 + host.n_in]
        r_outs = refs[6 + host.n_in:6 + host.n_in + host.n_out]
        sems = refs[6 + host.n_in + host.n_out:]
        j = pl.program_id(0)

        def compute():
            rows = lax.broadcasted_iota(jnp.int32, (t, cb), 0)
            bg = p_ref[:, 0:cb].astype(F32)
            cg = p_ref[:, cb:2 * cb].astype(F32)
            v = p_ref[:, 2 * cb:3 * cb].astype(F32)
            u = cg * v
            w = w_ref[...]
            u1 = _shift_down(u, 1, rows)
            u2 = _shift_down(u, 2, rows)
            conv = w[2:3] * u + w[1:2] * u1 + w[0:1] * u2
            dyf = dy_ref[...].astype(F32)
            dconv = dyf * bg
            du = w[2:3] * dconv + w[1:2] * _shift_up(dconv, 1, rows, t) + w[0:1] * _shift_up(dconv, 2, rows, t)
            dp_ref[:, 0:cb] = (dyf * conv).astype(dp_ref.dtype)
            dp_ref[:, cb:2 * cb] = (du * v).astype(dp_ref.dtype)
            dp_ref[:, 2 * cb:3 * cb] = (du * cg).astype(dp_ref.dtype)
            gw_ref[0:1, :] = jnp.sum(dconv * u2, axis=0, keepdims=True)
            gw_ref[1:2, :] = jnp.sum(dconv * u1, axis=0, keepdims=True)
            gw_ref[2:3, :] = jnp.sum(dconv * u, axis=0, keepdims=True)

        host.run(j == 0, j == nj - 1, r_ins, r_outs, sems, compute)

    res = pl.pallas_call(
        body,
        name="conv_bwd",
        grid=(nj,),
        in_specs=[
            pl.BlockSpec((t, 3 * cb), lambda j: (0, blk0 + j)),
            pl.BlockSpec((CONV_TAPS, cb), lambda j: (0, j)),
            pl.BlockSpec((t, cb), lambda j: (0, j)),
            ANY,
        ] + host.in_specs,
        out_specs=[pl.BlockSpec((t, 3 * cb), lambda j: (0, blk0 + j)), pl.BlockSpec((CONV_TAPS, cb), lambda j: (0, j))] + host.out_specs,
        out_shape=[jax.ShapeDtypeStruct(dproj.shape, dproj.dtype), jax.ShapeDtypeStruct((CONV_TAPS, c), F32)] + host.out_shapes,
        input_output_aliases={3: 0},
        scratch_shapes=host.scratch,
        compiler_params=_params(("arbitrary",)),
    )(proj, conv_w, dy, dproj, *host.ins)
    return res


def _lane_scan(x, reverse):
    lane = lax.broadcasted_iota(jnp.int32, x.shape, 1)
    s = 1
    while s < LANES:
        if reverse:
            x = x + jnp.where(lane < LANES - s, pltpu.roll(x, LANES - s, axis=1), 0.0)
        else:
            x = x + jnp.where(lane >= s, pltpu.roll(x, s, axis=1), 0.0)
        s *= 2
    return x


def _scan_rows(src_ref, dst_ref, t, reverse, fn=None):
    groups = list(range(t // LANES))
    if reverse:
        groups = groups[::-1]
    carry = None
    for gi in groups:
        sl = slice(gi * LANES, (gi + 1) * LANES)
        blk = src_ref[:, sl]
        if fn is not None:
            blk = fn(blk)
        blk = _lane_scan(blk, reverse)
        if carry is not None:
            blk = blk + carry
        dst_ref[:, sl] = blk
        carry = blk[:, 0:1] if reverse else blk[:, LANES - 1:LANES]


def _forget_fwd(z_row, b_col):
    rows, t = z_row.shape

    def body(z_ref, b_ref, c_ref):
        def logf(z):
            zz = z + b_ref[...]
            return jnp.minimum(zz, 0.0) - jnp.log(1.0 + jnp.exp(-jnp.abs(zz)))

        _scan_rows(z_ref, c_ref, t, False, logf)

    return pl.pallas_call(
        body,
        name="forget_fwd",
        out_shape=jax.ShapeDtypeStruct((rows, t), F32),
        compiler_params=pltpu.CompilerParams(vmem_limit_bytes=VMEM_LIMIT),
    )(z_row, b_col)


def _rows_to_colb(c_row3, tq):
    heads, _, t = c_row3.shape

    def body(r_ref, o_ref):
        o_ref[...] = jnp.broadcast_to(_row_to_col(r_ref[...]), (tq, LANES))

    return pl.pallas_call(
        body,
        name="rows_to_colb",
        grid=(heads, t // tq),
        in_specs=[pl.BlockSpec((None, 1, tq), lambda h, i: (h, 0, i))],
        out_specs=pl.BlockSpec((None, tq, LANES), lambda h, i: (h, i, 0)),
        out_shape=jax.ShapeDtypeStruct((heads, t, LANES), F32),
        compiler_params=_params(("parallel", "parallel")),
    )(c_row3)


def _forget_bwd(z_row, b_col, dc_row):
    rows, t = z_row.shape

    def body(z_ref, b_ref, dc_ref, dz_ref, db_ref, tmp_ref):
        _scan_rows(dc_ref, tmp_ref, t, True)
        zz = z_ref[...] + b_ref[...]
        dz = tmp_ref[...] * (1.0 / (1.0 + jnp.exp(zz)))
        dz_ref[...] = dz.astype(dz_ref.dtype)
        db_ref[...] = jnp.sum(dz, axis=1, keepdims=True)

    return pl.pallas_call(
        body,
        name="forget_bwd",
        out_shape=[jax.ShapeDtypeStruct((rows, t), BF16), jax.ShapeDtypeStruct((rows, 1), F32)],
        scratch_shapes=[pltpu.VMEM((rows, t), F32)],
        compiler_params=pltpu.CompilerParams(vmem_limit_bytes=VMEM_LIMIT),
    )(z_row, b_col, dc_row)


def _fox_fwd(proj, off, gq, gk, c_row3, c_colb, heads, tq, rider=None):
    t = proj.shape[0]
    hd = FOX_HEAD_DIM
    tq = _tile(t, tq)
    nq = t // tq
    blk0 = off // hd
    scale = 1.0 / math.sqrt(hd)
    host = _Host(rider)

    def body(*refs):
        q_ref, k_ref, v_ref, gq_ref, gk_ref, crow_ref, ccol_ref = refs[:7]
        r_ins = refs[7:7 + host.n_in]
        o_ref, lse_ref = refs[7 + host.n_in:9 + host.n_in]
        r_outs = refs[9 + host.n_in:9 + host.n_in + host.n_out]
        khat_ref, v_t_ref = refs[9 + host.n_in + host.n_out:11 + host.n_in + host.n_out]
        sems = refs[11 + host.n_in + host.n_out:]
        h, qi = pl.program_id(0), pl.program_id(1)

        def compute():
            eye = (lax.broadcasted_iota(jnp.int32, (hd, hd), 0) == lax.broadcasted_iota(jnp.int32, (hd, hd), 1)).astype(BF16)

            @pl.when(qi == 0)
            def _():
                kn, _ = _head_rms(k_ref[...].astype(F32))
                khat_ref[...] = (kn * gk_ref[...]).astype(BF16)
                v_t_ref[...] = _dot(eye, v_ref[...], NT).astype(BF16)

            qn, _ = _head_rms(q_ref[...].astype(F32))
            qhat = (qn * (gq_ref[...] * scale)).astype(BF16)
            crow = crow_ref[:, pl.ds(pl.multiple_of(qi * tq, tq), tq)]
            above = lax.broadcasted_iota(jnp.int32, (tq, tq), 1) >= lax.broadcasted_iota(jnp.int32, (tq, tq), 0)

            def tile(j, keys, carry, diagonal):
                m, l, acc_t = carry
                ks = pl.multiple_of(j * keys, keys)
                s_t = _dot(khat_ref[pl.ds(ks, keys), :], qhat, NT) - ccol_ref[pl.ds(ks, keys), 0:1]
                if diagonal:
                    s_t = jnp.where(above, s_t, NEG)
                m_new = jnp.maximum(m, jnp.max(s_t, axis=0, keepdims=True) + crow)
                alpha = jnp.exp(m - m_new)
                p_t = jnp.exp(s_t + (crow - m_new))
                l = alpha * l + jnp.sum(p_t, axis=0, keepdims=True)
                acc_t = alpha * acc_t + _dot(v_t_ref[:, pl.ds(ks, keys)], p_t.astype(BF16), NN)
                return m_new, l, acc_t

            init = (jnp.full((1, tq), NEG, F32), jnp.zeros((1, tq), F32), jnp.zeros((hd, tq), F32))
            pairs = qi // 2 if 2 * tq <= MAX_KEYS else 0
            carry = lax.fori_loop(0, pairs, lambda j, c: tile(j, 2 * tq, c, False), init)
            carry = lax.fori_loop(2 * pairs, qi, lambda j, c: tile(j, tq, c, False), carry)
            m, l, acc_t = tile(qi, tq, carry, True)
            o_ref[...] = _dot((acc_t / l).astype(BF16), eye, TN).astype(o_ref.dtype)
            lse_ref[...] = m + jnp.log(l)

        first = jnp.logical_and(h == 0, qi == 0)
        last = jnp.logical_and(h == heads - 1, qi == nq - 1)
        host.run(first, last, r_ins, r_outs, sems, compute)

    res = pl.pallas_call(
        body,
        name="fox_fwd",
        grid=(heads, nq),
        in_specs=[
            pl.BlockSpec((tq, hd), lambda h, i: (i, blk0 + 3 * h)),
            pl.BlockSpec((t, hd), lambda h, i: (0, blk0 + 3 * h + 1)),
            pl.BlockSpec((t, hd), lambda h, i: (0, blk0 + 3 * h + 2)),
            pl.BlockSpec((1, hd), lambda h, i: (0, 0)),
            pl.BlockSpec((1, hd), lambda h, i: (0, 0)),
            pl.BlockSpec((None, 1, t), lambda h, i: (h, 0, 0)),
            pl.BlockSpec((None, t, LANES), lambda h, i: (h, 0, 0)),
        ] + host.in_specs,
        out_specs=[pl.BlockSpec((tq, hd), lambda h, i: (i, h)), pl.BlockSpec((None, 1, tq), lambda h, i: (h, 0, i))] + host.out_specs,
        out_shape=[jax.ShapeDtypeStruct((t, heads * hd), BF16), jax.ShapeDtypeStruct((heads, 1, t), F32)] + host.out_shapes,
        scratch_shapes=[pltpu.VMEM((t, hd), BF16), pltpu.VMEM((hd, t), BF16)] + host.scratch,
        compiler_params=_params(("arbitrary", "arbitrary")),
    )(proj, proj, proj, gq.reshape(1, hd), gk.reshape(1, hd), c_row3, c_colb, *host.ins)
    return res


def _fox_bwd(proj, off, o, do, gq, gk, c_row3, c_colb, lse, heads, tq, dproj, rider=None):
    t = proj.shape[0]
    hd = FOX_HEAD_DIM
    tq = _tile(t, tq)
    nb = t // tq
    blk0 = off // hd
    scale = 1.0 / math.sqrt(hd)
    host = _Host(rider)
    n_fixed_in = 11

    def body(*refs):
        q_ref, k_ref, v_ref, o_ref, do_ref, gq_ref, gk_ref, crow_ref, ccol_ref, lse_ref = refs[:10]
        pos = n_fixed_in
        r_ins = refs[pos:pos + host.n_in]; pos += host.n_in
        dp_ref, dc_ref, ggq_ref, ggk_ref = refs[pos:pos + 4]; pos += 4
        r_outs = refs[pos:pos + host.n_out]; pos += host.n_out
        qhat_ref, khat_ref, khat_t_ref, dq_t_ref, dk_ref, dcq_ref, dck_ref, delta_ref = refs[pos:pos + 8]; pos += 8
        sems = refs[pos:]
        h = pl.program_id(0)

        def compute():
            qn, rq = _head_rms(q_ref[...].astype(F32))
            qhat_ref[...] = (qn * (gq_ref[...] * scale)).astype(BF16)
            kn, rk = _head_rms(k_ref[...].astype(F32))
            khat_ref[...] = (kn * gk_ref[...]).astype(BF16)
            eye = (lax.broadcasted_iota(jnp.int32, (hd, hd), 0) == lax.broadcasted_iota(jnp.int32, (hd, hd), 1)).astype(BF16)
            khat_t_ref[...] = _dot(eye, khat_ref[...], NT).astype(BF16)
            delta = jnp.sum(do_ref[...].astype(F32) * o_ref[...].astype(F32), axis=-1, keepdims=True)
            for b in range(nb):
                sl = slice(b * tq, (b + 1) * tq)
                delta_ref[:, sl] = _col_to_row(delta[sl, :])
            dq_t_ref[...] = jnp.zeros_like(dq_t_ref)
            dcq_ref[...] = jnp.zeros_like(dcq_ref)
            above = lax.broadcasted_iota(jnp.int32, (tq, tq), 1) >= lax.broadcasted_iota(jnp.int32, (tq, tq), 0)

            def kv_block(j, _):
                ks = pl.multiple_of(j * tq, tq)
                kh = khat_ref[pl.ds(ks, tq), :]
                kh_t = khat_t_ref[:, pl.ds(ks, tq)]
                vv = v_ref[pl.ds(ks, tq), :]
                ccol = ccol_ref[pl.ds(ks, tq), 0:1]

                def q_block(i, n, carry, diagonal):
                    dk, dv, dck = carry
                    qs = pl.multiple_of(i * tq, tq)
                    qh = qhat_ref[pl.ds(qs, n), :]
                    dob = do_ref[pl.ds(qs, n), :]
                    s_t = _dot(kh, qh, NT) + ((crow_ref[:, pl.ds(qs, n)] - lse_ref[:, pl.ds(qs, n)]) - ccol)
                    p_t = jnp.exp(s_t)
                    if diagonal:
                        p_t = jnp.where(above, p_t, 0.0)
                    ds_t = p_t * (_dot(vv, dob, NT) - delta_ref[:, pl.ds(qs, n)])
                    dsb = ds_t.astype(BF16)
                    dv = dv + _dot(p_t.astype(BF16), dob, NN)
                    dk = dk + _dot(dsb, qh, NN)
                    dq_t_ref[:, pl.ds(qs, n)] += _dot(kh_t, dsb, NN)
                    dcq_ref[:, pl.ds(qs, n)] += jnp.sum(ds_t, axis=0, keepdims=True)
                    dck = dck + jnp.sum(ds_t, axis=-1, keepdims=True)
                    return dk, dv, dck

                zero = jnp.zeros((tq, hd), F32)
                carry = q_block(j, tq, (zero, zero, jnp.zeros((tq, 1), F32)), True)
                pairs = (nb - 1 - j) // 2 if 2 * tq <= MAX_KEYS else 0
                carry = lax.fori_loop(0, pairs, lambda p, c: q_block(j + 1 + 2 * p, 2 * tq, c, False), carry)
                dk, dv, dck = lax.fori_loop(j + 1 + 2 * pairs, nb, lambda i, c: q_block(i, tq, c, False), carry)
                dk_ref[pl.ds(ks, tq), :] = dk
                dp_ref[pl.ds(ks, tq), 2 * hd:3 * hd] = dv.astype(dp_ref.dtype)
                dck_ref[pl.ds(ks, tq), :] = dck
                return 0

            lax.fori_loop(0, nb, kv_block, 0)

            dq, ggq = _head_rms_bwd(dq_t_ref[...].T * scale, qn, rq, gq_ref[...])
            dk, ggk = _head_rms_bwd(dk_ref[...], kn, rk, gk_ref[...])
            dp_ref[:, 0:hd] = dq.astype(dp_ref.dtype)
            dp_ref[:, hd:2 * hd] = dk.astype(dp_ref.dtype)
            for b in range(nb):
                sl = slice(b * tq, (b + 1) * tq)
                dc_ref[:, sl] = dcq_ref[:, sl] - _col_to_row(dck_ref[sl, :])

            @pl.when(h == 0)
            def _():
                ggq_ref[...] = jnp.zeros_like(ggq_ref)
                ggk_ref[...] = jnp.zeros_like(ggk_ref)

            ggq_ref[...] += ggq
            ggk_ref[...] += ggk

        host.run(h == 0, h == heads - 1, r_ins, r_outs, sems, compute)

    head_in = lambda part: pl.BlockSpec((t, hd), lambda h: (0, blk0 + 3 * h + part))
    vec = pl.BlockSpec((1, hd), lambda h: (0, 0))
    colb = pl.BlockSpec((None, t, LANES), lambda h: (h, 0, 0))
    res = pl.pallas_call(
        body,
        name="fox_bwd",
        grid=(heads,),
        in_specs=[
            head_in(0), head_in(1), head_in(2),
            pl.BlockSpec((t, hd), lambda h: (0, h)),
            pl.BlockSpec((t, hd), lambda h: (0, h)),
            vec, vec,
            pl.BlockSpec((None, 1, t), lambda h: (h, 0, 0)),
            colb,
            pl.BlockSpec((None, 1, t), lambda h: (h, 0, 0)),
            ANY,
        ] + host.in_specs,
        out_specs=[
            pl.BlockSpec((t, 3 * hd), lambda h: (0, blk0 // 3 + h)),
            pl.BlockSpec((None, 1, t), lambda h: (h, 0, 0)),
            vec, vec,
        ] + host.out_specs,
        out_shape=[
            jax.ShapeDtypeStruct(dproj.shape, dproj.dtype),
            jax.ShapeDtypeStruct((heads, 1, t), F32),
            jax.ShapeDtypeStruct((1, hd), F32),
            jax.ShapeDtypeStruct((1, hd), F32),
        ] + host.out_shapes,
        input_output_aliases={10: 0},
        scratch_shapes=[
            pltpu.VMEM((t, hd), BF16), pltpu.VMEM((t, hd), BF16), pltpu.VMEM((hd, t), BF16),
            pltpu.VMEM((hd, t), F32), pltpu.VMEM((t, hd), F32),
            pltpu.VMEM((1, t), F32), pltpu.VMEM((t, 1), F32), pltpu.VMEM((1, t), F32),
        ] + host.scratch,
        compiler_params=_params(("arbitrary",)),
    )(proj, proj, proj, o, do, gq.reshape(1, hd), gk.reshape(1, hd), c_row3, c_colb, lse, dproj, *host.ins)
    return res


def _mem_fwd(proj, off, kv, gq, gk, tq):
    t = proj.shape[0]
    m, width = kv.shape[0], kv.shape[1] // 2
    hd = width // MEM_HEADS
    tq = _tile(t, tq)
    blk0 = off // hd
    scale = 1.0 / math.sqrt(hd)

    def body(q_ref, k_ref, v_ref, gq_ref, gk_ref, o_ref):
        qn, _ = _head_rms(q_ref[...].astype(F32))
        kn, _ = _head_rms(k_ref[...])
        s = _dot((qn * gq_ref[...]).astype(BF16), (kn * gk_ref[...]).astype(BF16), NT) * scale
        p = jnp.exp(s - jnp.max(s, axis=-1, keepdims=True))
        p = p / jnp.sum(p, axis=-1, keepdims=True)
        o_ref[...] = _dot(p.astype(BF16), v_ref[...].astype(BF16), NN).astype(o_ref.dtype)

    vec = pl.BlockSpec((1, hd), lambda h, i: (0, 0))
    return pl.pallas_call(
        body,
        name="mem_fwd",
        grid=(MEM_HEADS, t // tq),
        in_specs=[
            pl.BlockSpec((tq, hd), lambda h, i: (i, blk0 + h)),
            pl.BlockSpec((m, hd), lambda h, i: (0, h)),
            pl.BlockSpec((m, hd), lambda h, i: (0, MEM_HEADS + h)),
            vec, vec,
        ],
        out_specs=pl.BlockSpec((tq, hd), lambda h, i: (i, h)),
        out_shape=jax.ShapeDtypeStruct((t, width), BF16),
        compiler_params=_params(("parallel", "parallel")),
    )(proj, kv, kv, gq.reshape(1, hd), gk.reshape(1, hd))


def _mem_bwd(proj, off, kv, do, gq, gk, tq, dproj, rider=None):
    t = proj.shape[0]
    m, width = kv.shape[0], kv.shape[1] // 2
    hd = width // MEM_HEADS
    tq = _tile(t, tq)
    nq = t // tq
    blk0 = off // hd
    scale = 1.0 / math.sqrt(hd)
    host = _Host(rider)

    def body(*refs):
        q_ref, k_ref, v_ref, do_ref, gq_ref, gk_ref = refs[:6]
        pos = 7
        r_ins = refs[pos:pos + host.n_in]; pos += host.n_in
        dq_ref, dk_ref, dv_ref, ggq_ref, ggk_ref = refs[pos:pos + 5]; pos += 5
        r_outs = refs[pos:pos + host.n_out]; pos += host.n_out
        dkh_ref, dvh_ref = refs[pos:pos + 2]; pos += 2
        sems = refs[pos:]
        h, i = pl.program_id(0), pl.program_id(1)

        def compute():
            qn, rq = _head_rms(q_ref[...].astype(F32))
            kn, rk = _head_rms(k_ref[...])
            qhat = (qn * gq_ref[...]).astype(BF16)
            khat = (kn * gk_ref[...]).astype(BF16)
            vb = v_ref[...].astype(BF16)
            dob = do_ref[...]
            s = _dot(qhat, khat, NT) * scale
            p = jnp.exp(s - jnp.max(s, axis=-1, keepdims=True))
            p = p / jnp.sum(p, axis=-1, keepdims=True)
            dp = _dot(dob, vb, NT)
            ds = p * (dp - jnp.sum(dp * p, axis=-1, keepdims=True))
            dsb = ds.astype(BF16)
            dq, ggq = _head_rms_bwd(_dot(dsb, khat, NN) * scale, qn, rq, gq_ref[...])
            dq_ref[...] = dq.astype(dq_ref.dtype)

            @pl.when(i == 0)
            def _():
                dkh_ref[...] = jnp.zeros_like(dkh_ref)
                dvh_ref[...] = jnp.zeros_like(dvh_ref)

            @pl.when(jnp.logical_and(h == 0, i == 0))
            def _():
                ggq_ref[...] = jnp.zeros_like(ggq_ref)
                ggk_ref[...] = jnp.zeros_like(ggk_ref)

            dkh_ref[...] += _dot(dsb, qhat, TN)
            dvh_ref[...] += _dot(p.astype(BF16), dob, TN)
            ggq_ref[...] += ggq

            @pl.when(i == nq - 1)
            def _():
                dk, ggk = _head_rms_bwd(dkh_ref[...] * scale, kn, rk, gk_ref[...])
                dk_ref[...] = dk.astype(dk_ref.dtype)
                dv_ref[...] = dvh_ref[...].astype(dv_ref.dtype)
                ggk_ref[...] += ggk

        first = jnp.logical_and(h == 0, i == 0)
        last = jnp.logical_and(h == MEM_HEADS - 1, i == nq - 1)
        host.run(first, last, r_ins, r_outs, sems, compute)

    vec = pl.BlockSpec((1, hd), lambda h, i: (0, 0))
    kblk = pl.BlockSpec((m, hd), lambda h, i: (0, h))
    res = pl.pallas_call(
        body,
        name="mem_bwd",
        grid=(MEM_HEADS, nq),
        in_specs=[
            pl.BlockSpec((tq, hd), lambda h, i: (i, blk0 + h)), kblk,
            pl.BlockSpec((m, hd), lambda h, i: (0, MEM_HEADS + h)),
            pl.BlockSpec((tq, hd), lambda h, i: (i, h)), vec, vec, ANY,
        ] + host.in_specs,
        out_specs=[pl.BlockSpec((tq, hd), lambda h, i: (i, blk0 + h)), kblk, kblk, vec, vec] + host.out_specs,
        out_shape=[
            jax.ShapeDtypeStruct(dproj.shape, dproj.dtype),
            jax.ShapeDtypeStruct((m, width), BF16),
            jax.ShapeDtypeStruct((m, width), BF16),
            jax.ShapeDtypeStruct((1, hd), F32),
            jax.ShapeDtypeStruct((1, hd), F32),
        ] + host.out_shapes,
        input_output_aliases={6: 0},
        scratch_shapes=[pltpu.VMEM((m, hd), F32), pltpu.VMEM((m, hd), F32)] + host.scratch,
        compiler_params=_params(("arbitrary", "arbitrary")),
    )(proj, kv, kv, do, gq.reshape(1, hd), gk.reshape(1, hd), dproj, *host.ins)
    dproj, dk, dv, ggq, ggk = res[:5]
    return (dproj, jnp.concatenate([dk, dv], axis=1), ggq.reshape(hd), ggk.reshape(hd), *res[5:])


def _sigmoid(z):
    return 1.0 / (1.0 + jnp.exp(-z))


def _merge_fwd(proj, ys, ws, tm, tc):
    t, cw = ys[0].shape
    d = ws[0].shape[1]
    tm = _tile(t, tm)

    def body(g_ref, ya_ref, yb_ref, yc_ref, wa_ref, wb_ref, wc_ref, oa_ref, ob_ref, oc_ref, out_ref):
        acc = jnp.zeros((tm, tc), F32)
        for s, (y_ref, w_ref, o_ref) in enumerate(((ya_ref, wa_ref, oa_ref), (yb_ref, wb_ref, ob_ref), (yc_ref, wc_ref, oc_ref))):
            o = _dot(y_ref[...], w_ref[...], NN)
            o_ref[...] = o.astype(o_ref.dtype)
            acc = acc + _sigmoid(g_ref[:, s * tc:(s + 1) * tc].astype(F32)) * o
        out_ref[...] = acc.astype(out_ref.dtype)

    blk = pl.BlockSpec((tm, tc), lambda i, j: (i, j))
    y_spec = pl.BlockSpec((tm, cw), lambda i, j: (i, 0))
    w_spec = pl.BlockSpec((cw, tc), lambda i, j: (0, j))
    return pl.pallas_call(
        body,
        name="merge_fwd",
        grid=(t // tm, d // tc),
        in_specs=[pl.BlockSpec((tm, 3 * tc), lambda i, j: (i, j))] + [y_spec] * 3 + [w_spec] * 3,
        out_specs=[blk] * 4,
        out_shape=[jax.ShapeDtypeStruct((t, d), BF16)] * 4,
        compiler_params=_params(("parallel", "parallel")),
    )(proj, *ys, *ws)


def _merge_bwd(proj, o3, dx1, w_out, tm, tc):
    t, d = o3[0].shape
    k = dx1.shape[1]
    tm = _tile(t, tm)

    def body(dx_ref, w_ref, g_ref, oa_ref, ob_ref, oc_ref, dg_ref, da_ref, db_ref, dc_ref):
        dmf = _dot(dx_ref[...], w_ref[...], NT)
        for s, (o_ref, do_ref) in enumerate(((oa_ref, da_ref), (ob_ref, db_ref), (oc_ref, dc_ref))):
            g = _sigmoid(g_ref[:, s * tc:(s + 1) * tc].astype(F32))
            do_ref[...] = (dmf * g).astype(do_ref.dtype)
            dg_ref[:, s * tc:(s + 1) * tc] = (dmf * o_ref[...].astype(F32) * g * (1.0 - g)).astype(dg_ref.dtype)

    blk = pl.BlockSpec((tm, tc), lambda i, j: (i, j))
    wide = pl.BlockSpec((tm, 3 * tc), lambda i, j: (i, j))
    return pl.pallas_call(
        body,
        name="merge_bwd",
        grid=(t // tm, d // tc),
        in_specs=[pl.BlockSpec((tm, k), lambda i, j: (i, 0)), pl.BlockSpec((tc, k), lambda i, j: (j, 0)), wide, blk, blk, blk],
        out_specs=[wide, blk, blk, blk],
        out_shape=[jax.ShapeDtypeStruct(proj.shape, BF16)] + [jax.ShapeDtypeStruct((t, d), BF16)] * 3,
        compiler_params=_params(("parallel", "parallel")),
    )(dx1, w_out, proj, *o3)


def _w_in_chunks(d, tc):
    cw = d // 2
    heads = cw // FOX_HEAD_DIM
    conv0, fox0, f0, mq0, gate0 = 0, 3 * cw, 6 * cw, 6 * cw + heads, 7 * cw + heads
    chunks = [(gate0 + s * d + j * tc, gate0 + s * d + (j + 1) * tc) for j in range(d // tc) for s in range(N_BRANCHES)]
    chunks += [(conv0 + s * cw + j * LANES, conv0 + s * cw + (j + 1) * LANES) for j in range(cw // LANES) for s in range(3)]
    chunks += [(fox0 + s * cw + j * FOX_HEAD_DIM, fox0 + s * cw + (j + 1) * FOX_HEAD_DIM) for j in range(heads) for s in range(3)]
    chunks.append((mq0, mq0 + cw))
    return chunks, (f0, f0 + heads)


ROW_TILE = 16
GROUP = 128
GROUP_BACK = 112
SCRATCH_ROWS = 2 * GROUP + 32


def _padded_rows(r):
    return -(-r // GROUP_BACK) * GROUP_BACK


def _rows_from(scr_ref, use, q8, fine, g):
    x = scr_ref[pl.ds(pl.multiple_of(q8 * 8, 8), g + 8), :]
    for s in range(8):
        @pl.when(fine == s)
        def _(s=s):
            use((x if s == 0 else pltpu.roll(x, g + 8 - s, axis=0))[0:g])


def _assemble(name, tbl, grid, step, in_specs, out_spec, out_shape, operands, g, w1, cols_of):
    has_f = len(in_specs) == 3
    k = out_shape.shape[-1]
    c = cols_of

    def body(*refs):
        t_ref, s1_ref, s2_ref = refs[:3]
        f_ref = refs[3] if has_f else None
        out_ref = refs[3 + has_f]
        scr1, scr2, scrf = refs[4 + has_f:]
        t = step()

        def put(y):
            out_ref[...] = y.astype(out_ref.dtype)

        @pl.when(t == 0)
        def _():
            scr1[...] = jnp.zeros_like(scr1)
            scr2[...] = jnp.zeros_like(scr2)
            scrf[...] = jnp.zeros_like(scrf)

        rows = lax.broadcasted_iota(jnp.int32, (g, k), 0)
        n1, a2 = t_ref[c["n1"], t], t_ref[c["a2"], t]
        scr1[0:w1, :] = (s1_ref[0] if len(s1_ref.shape) == 3 else s1_ref[...]).astype(F32)
        _rows_from(scr1, put, t_ref[c["q1"], t], t_ref[c["s1"], t], g)

        @pl.when(a2 < g)
        def _():
            scr2[g:g + s2_ref.shape[0], :] = s2_ref[...].astype(F32)
            _rows_from(scr2, lambda y: put(jnp.where(rows < n1, out_ref[...].astype(F32), y)),
                       t_ref[c["q2"], t], t_ref[c["s2"], t], g)

        if has_f:
            fa, fb = t_ref[c["fa"], t], t_ref[c["fb"], t]

            @pl.when(fb > fa)
            def _():
                scrf[g:g + f_ref.shape[0], :] = f_ref[...].astype(F32)
                inside = jnp.logical_and(rows >= fa, rows < fb)
                _rows_from(scrf, lambda y: put(jnp.where(inside, y, out_ref[...].astype(F32))),
                           t_ref[c["qf"], t], t_ref[c["sf"], t], g)

            valid = t_ref[c["valid"], t]

            @pl.when(valid < g)
            def _():
                out_ref[...] = jnp.where(rows < valid, out_ref[...].astype(F32), 0.0).astype(out_ref.dtype)

    return pl.pallas_call(
        body,
        name=name,
        grid_spec=pltpu.PrefetchScalarGridSpec(
            num_scalar_prefetch=1, grid=grid, in_specs=in_specs, out_specs=out_spec,
            scratch_shapes=[pltpu.VMEM((SCRATCH_ROWS, k), F32)] * 3),
        out_shape=out_shape,
        compiler_params=_params(("arbitrary",) * len(grid)),
    )(jnp.asarray(tbl), *operands)


def _pack_w_in(w8, d, tc):
    blocks, rp, k = w8.shape
    chunks, (f_lo, f_hi) = _w_in_chunks(d, tc)
    r = max(hi for _, hi in chunks) // blocks
    g, w1 = GROUP, GROUP + ROW_TILE
    table = []
    for lo, hi in chunks:
        for g0 in range(lo, hi, g):
            b1, r1 = divmod(g0, r)
            n1 = min(g, r - r1)
            st1 = min(r1 // ROW_TILE * ROW_TILE, rp - w1)
            o1, o2 = r1 - st1, g - n1
            b2 = b1 + 1 if n1 < g else 0
            table.append((b1, st1, o1 // 8, o1 % 8, n1, n1, b2, o2 // 8, o2 % 8))
    names = ("b1", "st1", "q1", "s1", "n1", "a2", "b2", "q2", "s2")
    cols_of = {n: i for i, n in enumerate(names)}
    tbl = np.array(table, np.int32).T
    c = cols_of
    w_all = _assemble(
        "pack_w_in", tbl, (len(table),), lambda: pl.program_id(0),
        [pl.BlockSpec((pl.Element(1), pl.Element(w1), pl.Element(k)), lambda i, t: (t[c["b1"], i], pl.multiple_of(t[c["st1"], i], ROW_TILE), 0)),
         pl.BlockSpec((None, g, k), lambda i, t: (t[c["b2"], i], 0, 0))],
        pl.BlockSpec((g, k), lambda i, t: (i, 0)),
        jax.ShapeDtypeStruct((len(table) * g, k), w8.dtype), [w8, w8], g, w1, cols_of)
    fb, fr = divmod(f_lo, r)
    return w_all, jnp.pad(w8[fb, fr:fr + f_hi - f_lo], ((0, F_ROWS - (f_hi - f_lo)), (0, 0)))


def _unpack_g_in(g_all, g_f, d, tc, blocks):
    n_all, k = g_all.shape
    chunks, (f_lo, f_hi) = _w_in_chunks(d, tc)
    r = max(hi for _, hi in chunks) // blocks
    rp = _padded_rows(r)
    g, w1 = GROUP_BACK, GROUP_BACK + ROW_TILE
    pos, spans = 0, [(f_lo, f_hi, None)]
    for lo, hi in chunks:
        spans.append((lo, hi, pos))
        pos += hi - lo
    spans.sort()
    table = []
    for b in range(blocks):
        for l0 in range(0, rp, g):
            valid = max(0, min(g, r - l0))
            g0, segs, fa, fb, of = b * r + l0, [], 0, 0, 0
            for lo, hi, p in spans:
                a, e = max(lo, g0), min(hi, g0 + valid)
                if a < e and p is None:
                    fa, fb, of = a - g0, e - g0, g + (a - lo) - (a - g0)
                elif a < e:
                    segs.append((a - g0, p + a - lo, e - a))
            assert len(segs) <= 2 and (not segs or segs[0][0] == 0 or len(segs) == 1)
            first = segs[0] if segs and segs[0][0] == 0 else (0, 0, 0)
            second = segs[-1] if segs and segs[-1][0] > 0 else (g, 0, 0)
            st1 = min(first[1] // ROW_TILE * ROW_TILE, n_all - w1)
            o1, o2 = first[1] - st1, g - second[0]
            assert second[1] % GROUP == 0
            table.append((st1, o1 // 8, o1 % 8, first[2], second[0], second[1] // GROUP, o2 // 8, o2 % 8,
                          fa, fb, of // 8, of % 8, valid))
    names = ("st1", "q1", "s1", "n1", "a2", "j2", "q2", "s2", "fa", "fb", "qf", "sf", "valid")
    cols_of = {n: i for i, n in enumerate(names)}
    tbl = np.array(table, np.int32).T
    c, per = cols_of, rp // g
    return _assemble(
        "unpack_g_in", tbl, (blocks, per), lambda: pl.program_id(0) * per + pl.program_id(1),
        [pl.BlockSpec((pl.Element(w1), pl.Element(k)), lambda b, u, t: (pl.multiple_of(t[c["st1"], b * per + u], ROW_TILE), 0)),
         pl.BlockSpec((GROUP, k), lambda b, u, t: (t[c["j2"], b * per + u], 0)),
         pl.BlockSpec((F_ROWS, k), lambda b, u, t: (0, 0))],
        pl.BlockSpec((None, g, k), lambda b, u, t: (b, u, 0)),
        jax.ShapeDtypeStruct((blocks, rp, k), g_all.dtype), [g_all, g_all, g_f], g, w1, cols_of)


def _unblock(w8):
    return w8.transpose(1, 0, 2).reshape(w8.shape[1], -1)


def _tile2(r, cols, tr, tcols):
    if r % 8 == 0:
        return _tile(r, tr), cols
    return r, _tile(cols, tcols)


def _pair_sum(name, g8, got, c):
    def body(c_ref, g_ref, s_ref, o_ref):
        o_ref[...] = (g_ref[...].astype(F32) + s_ref[...].astype(F32)).astype(o_ref.dtype)

    if g8.ndim == 4:
        _, r, k1, k2 = g8.shape
        tr = max(cand for cand in range(1, 385) if r % cand == 0)
        grid = (N_CHIPS, r // tr)
        shape = (None, tr, k1, k2)
        own = pl.BlockSpec(shape, lambda q, i, c_ref: (2 * q + c_ref[0], i, 0, 0))
        blk = pl.BlockSpec(shape, lambda q, i, c_ref: (q, i, 0, 0))
    else:
        _, r, cols = g8.shape
        tr, tcols = _tile2(r, cols, 256, 256)
        grid = (N_CHIPS, r // tr, cols // tcols)
        own = pl.BlockSpec((None, tr, tcols), lambda q, i, j, c_ref: (2 * q + c_ref[0], i, j))
        blk = pl.BlockSpec((None, tr, tcols), lambda q, i, j, c_ref: (q, i, j))
    return pl.pallas_call(
        body,
        name=name,
        grid_spec=pltpu.PrefetchScalarGridSpec(num_scalar_prefetch=1, grid=grid, in_specs=[own, blk], out_specs=blk),
        out_shape=jax.ShapeDtypeStruct((N_CHIPS,) + g8.shape[1:], BF16),
        compiler_params=_params(("parallel",) * len(grid)),
    )(c, g8, got)


def _local_step(x, mem, target, w, small, comm=None):
    t, d = x.shape
    cw = d // 2
    heads = cw // FOX_HEAD_DIM
    tc = min(512, d)
    tq = min(512, t)
    off_conv, off_fox, off_mq = 3 * d, 3 * d + 3 * cw, 3 * d + 6 * cw
    w = dict(w)
    w_all, w_f = _pack_w_in(w["w_in"], d, tc)
    big = dict(tm=1024, tn=512, tk=2048)
    wide_k = dict(tm=512, tn=1024, tk=4096)
    tall = dict(tm=2048, tn=512, tk=2048)

    h = _rms_fwd("rms1_fwd", x, small["norm1_g"])
    if comm:
        early = ("w_conv_out", "w_fox_out", "w_mem_out", "w_out", "w_mem_kv")
        proj, *got = _matmul("proj", "nt", h, w_all, outs=[BF16], rider=_gather_rider([comm["shards"][n] for n in early], False), **tall)
        for n, val in zip(early, got):
            w[n] = _unblock(val) if n in COLUMN_SPLIT else val.reshape(-1, val.shape[-1])
    else:
        proj = _matmul("proj", "nt", h, w_all, outs=[BF16], **tall)
    z_row = _matmul("proj_f", "nt", w_f, h, outs=[F32], tm=F_ROWS, tn=512, tk=2048)

    y_conv = _conv_fwd(proj, off_conv, small["conv_w"], LANES)

    b_col = jnp.pad(small["b_f"], (0, F_ROWS - heads)).reshape(F_ROWS, 1)
    c_row3 = _forget_fwd(z_row, b_col)[:heads].reshape(heads, 1, t)
    c_colb = _rows_to_colb(c_row3, tq)
    if comm:
        y_fox, lse, got = _fox_fwd(proj, off_fox, small["fox_q_g"], small["fox_k_g"], c_row3, c_colb, heads, 2 * tq,
                                   rider=_gather_rider([comm["shards"]["w_up"]], False))
        w["w_up"] = _unblock(got)
    else:
        y_fox, lse = _fox_fwd(proj, off_fox, small["fox_q_g"], small["fox_k_g"], c_row3, c_colb, heads, 2 * tq)

    nm = _rms_fwd("mem_rms_fwd", mem, small["mem_norm_g"])
    kv = _matmul("mem_kv", "nn", nm, w["w_mem_kv"], outs=[F32], tm=256, tn=512, tk=2048)
    y_mem = _mem_fwd(proj, off_mq, kv, small["mem_q_g"], small["mem_k_g"], tq)

    ys = (y_conv, y_fox, y_mem)
    w_outs = (w["w_conv_out"], w["w_fox_out"], w["w_mem_out"])
    *o3, merged = _merge_fwd(proj, ys, w_outs, 1024, tc)
    x1 = _matmul("out_proj", "nn", merged, w["w_out"], outs=[F32], extras=[x],
                 epilogue=lambda acc, xr: (acc + xr,), **big)
    h2 = _rms_fwd("rms2_fwd", x1, small["norm2_g"])

    def up_epilogue(acc):
        return acc, jnp.square(jnp.maximum(acc, 0.0))

    if comm:
        up, act, got = _matmul("mlp_up", "nn", h2, w["w_up"], outs=[BF16, BF16], epilogue=up_epilogue,
                               rider=_gather_rider([comm["shards"]["w_down"]], True), **big)
        w["w_down"] = got.reshape(-1, got.shape[-1])
    else:
        up, act = _matmul("mlp_up", "nn", h2, w["w_up"], outs=[BF16, BF16], epilogue=up_epilogue, **big)

    def loss_epilogue(acc, x1r, tr):
        dy = (acc + x1r - tr) * (1.0 / d)
        return dy, dy

    dy, dyb = _matmul("mlp_down", "nn", act, w["w_down"], outs=[F32, BF16], extras=[x1, target],
                      epilogue=loss_epilogue, tm=1024, tn=512, tk=4096)

    def dup_epilogue(acc, upr):
        return (acc * 2.0 * jnp.maximum(upr.astype(F32), 0.0),)

    def by_owner(g):
        return g.reshape(N_DEV, -1, g.shape[-1])

    g, parts = {}, {}
    g["w_down"] = _matmul("d_w_down", "tn", act, dyb, outs=[BF16], **wide_k)
    if comm:
        dup = _matmul("d_act", "nt", dyb, w["w_down"], outs=[BF16], extras=[up], epilogue=dup_epilogue, **tall)
        g["w_up"], got = _matmul("d_w_up", "tn", h2, dup, outs=[BF16], out_blocks=True,
                                 rider=_pair_rider([by_owner(g["w_down"])]), **wide_k)
        pair = _pair_sum("pair_w_down", by_owner(g["w_down"]), got, comm["c"])
        dh2, parts["w_down"], got = _matmul("d_h2", "nt", dup, w["w_up"], outs=[F32],
                                            rider=_join_riders(_chip_rider([pair]), _pair_rider([g["w_up"]])), **tall)
        pair_up = _pair_sum("pair_w_up", g["w_up"], got, comm["c"])
    else:
        dup = _matmul("d_act", "nt", dyb, w["w_down"], outs=[BF16], extras=[up], epilogue=dup_epilogue, **tall)
        g["w_up"] = _matmul("d_w_up", "tn", h2, dup, outs=[BF16], out_blocks=True, **wide_k)
        dh2 = _matmul("d_h2", "nt", dup, w["w_up"], outs=[F32], **tall)
    dx1, dx1b, g_norm2, dy_sq = _rms_bwd("rms2_bwd", dh2, x1, small["norm2_g"], res=dy)
    loss = dy_sq * (0.5 * d)

    g["w_out"] = _matmul("d_w_out", "tn", merged, dx1b, outs=[BF16], **wide_k)
    dproj, *do3 = _merge_bwd(proj, o3, dx1b, w["w_out"], 1024, tc)
    names = ("w_conv_out", "w_fox_out", "w_mem_out")
    dys = []
    for s in range(3):
        g[names[s]] = _matmul(f"d_w_branch{s}", "tn", ys[s], do3[s], outs=[BF16], out_blocks=True, **wide_k)
        dys.append(_matmul(f"d_branch{s}", "nt", do3[s], w_outs[s], outs=[BF16], **tall))

    dproj, dkv, g_mq, g_mk = _mem_bwd(proj, off_mq, kv, dys[2], small["mem_q_g"], small["mem_k_g"], tq, dproj)
    g["w_mem_kv"] = _matmul("d_w_mem_kv", "tn", nm, dkv, outs=[BF16], **wide_k)
    dnm = _matmul("d_mem_norm", "nt", dkv, w["w_mem_kv"], outs=[F32], tm=256, tn=512, tk=2048)
    _, _, g_mem_norm, _ = _rms_bwd("mem_rms_bwd", dnm, mem, small["mem_norm_g"])

    mid = ("w_out", "w_conv_out", "w_fox_out", "w_mem_out", "w_mem_kv")
    if comm:
        mid8 = [g[n] if n in names else by_owner(g[n]) for n in mid]
        dproj, g_conv_w, *got = _conv_bwd(proj, off_conv, small["conv_w"], dys[0], LANES, dproj, rider=_pair_rider(mid8))
        pairs_mid = [_pair_sum("pair_" + n, g8, s4, comm["c"]) for n, g8, s4 in zip(mid, mid8, got)]
        dproj, dc, g_fq, g_fk, parts["w_up"] = _fox_bwd(proj, off_fox, y_fox, dys[1], small["fox_q_g"], small["fox_k_g"], c_row3,
                                                        c_colb, lse, heads, tq, dproj, rider=_chip_rider([pair_up]))
    else:
        dproj, g_conv_w = _conv_bwd(proj, off_conv, small["conv_w"], dys[0], LANES, dproj)
        dproj, dc, g_fq, g_fk = _fox_bwd(proj, off_fox, y_fox, dys[1], small["fox_q_g"], small["fox_k_g"], c_row3, c_colb,
                                         lse, heads, tq, dproj)
    dc_row = jnp.pad(dc.reshape(heads, t), ((0, F_ROWS - heads), (0, 0)))
    dz_row, db = _forget_bwd(z_row, b_col, dc_row)

    if comm:
        g_all, *got = _matmul("d_w_in", "tn", dproj, h, outs=[BF16], j_outer=True, rider=_chip_rider(pairs_mid), **wide_k)
        parts.update(zip(mid, got))
    else:
        g_all = _matmul("d_w_in", "tn", dproj, h, outs=[BF16], j_outer=True, **wide_k)
    g_wf = _matmul("d_w_f", "nn", dz_row, h, outs=[BF16], tm=F_ROWS, tn=512, tk=4096)
    g["w_in"] = _unpack_g_in(g_all, g_wf, d, tc, w["w_in"].shape[0])
    dh = _matmul("d_h_f", "tn", dz_row, w_f, outs=[F32], tm=1024, tn=512, tk=F_ROWS)
    add_prev = lambda acc, prev: (acc + prev,)
    if comm:
        g_in8 = g["w_in"]
        got = _run_rider("pair_exchange_w_in", _pair_rider([g_in8]))[0]
        pair = _pair_sum("pair_w_in", g_in8, got, comm["c"])
        dh, parts["w_in"] = _matmul("d_h", "nn", dproj, w_all, outs=[F32], extras=[dh], epilogue=add_prev,
                                    rider=_chip_rider([pair]), tm=1024, tn=512, tk=3328)
    else:
        dh = _matmul("d_h", "nn", dproj, w_all, outs=[F32], extras=[dh], epilogue=add_prev, tm=1024, tn=512, tk=3328)
    grad_x, _, g_norm1, _ = _rms_bwd("rms1_bwd", dh, x, small["norm1_g"], res=dx1)

    gs = dict(norm1_g=g_norm1, b_f=db[:heads, 0], conv_w=g_conv_w, fox_q_g=g_fq.reshape(-1), fox_k_g=g_fk.reshape(-1),
              mem_norm_g=g_mem_norm, mem_q_g=g_mq, mem_k_g=g_mk, norm2_g=g_norm2)
    return loss, grad_x, (parts if comm else g), gs


def _adamw_math(w, g, m, v):
    m = ADAM_B1 * m + (1.0 - ADAM_B1) * g
    v = ADAM_B2 * v + (1.0 - ADAM_B2) * jnp.square(g)
    m_hat = m / (1.0 - ADAM_B1 ** ADAM_STEP)
    v_hat = v / (1.0 - ADAM_B2 ** ADAM_STEP)
    delta = -ADAM_LR * (m_hat / (jnp.sqrt(v_hat) + ADAM_EPS) + ADAM_WD * w)
    return delta, m, v


def _adamw(name, parts, w, m, v):
    r, c = w.shape
    n_parts, rp = parts.shape[:2]
    if rp == r:
        tr, tc = _tile2(r, c, 128, 256)
    else:
        tr, tc = _tile(rp, 256), _tile(c, 1024)

    def body(p_ref, w_ref, m_ref, v_ref, g_ref, d_ref, nm_ref, nv_ref):
        g = p_ref[0].astype(F32)
        for s in range(1, n_parts):
            g = g + p_ref[s].astype(F32)
        delta, nm, nv = _adamw_math(w_ref[...], g, m_ref[...], v_ref[...])
        g_ref[...] = g
        d_ref[...] = delta
        nm_ref[...] = nm
        nv_ref[...] = nv

    blk = pl.BlockSpec((tr, tc), lambda i, j: (i, j))
    return pl.pallas_call(
        body,
        name=name,
        grid=(rp // tr, c // tc),
        in_specs=[pl.BlockSpec((n_parts, tr, tc), lambda i, j: (0, i, j)), blk, blk, blk],
        out_specs=[blk] * 4,
        out_shape=[jax.ShapeDtypeStruct((r, c), F32)] * 4,
        compiler_params=_params(("parallel", "parallel")),
    )(parts, w, m, v)


def _sum_parts(name, parts):
    n_parts, r, c = parts.shape

    def body(p_ref, o_ref):
        acc = p_ref[0]
        for s in range(1, n_parts):
            acc = acc + p_ref[s]
        o_ref[...] = acc

    return pl.pallas_call(body, name=name, out_shape=jax.ShapeDtypeStruct((r, c), F32))(parts)


BIG = ("w_in", "w_mem_kv", "w_conv_out", "w_fox_out", "w_mem_out", "w_out", "w_up", "w_down")
COLUMN_SPLIT = ("w_in", "w_conv_out", "w_fox_out", "w_mem_out", "w_up")
SMALL = ("norm1_g", "b_f", "conv_w", "fox_q_g", "fox_k_g", "mem_norm_g", "mem_q_g", "mem_k_g", "norm2_g")
WEIGHTS = ("norm1_g", "w_in", "b_f", "conv_w", "fox_q_g", "fox_k_g", "mem_norm_g", "w_mem_kv", "mem_q_g", "mem_k_g",
           "w_conv_out", "w_fox_out", "w_mem_out", "w_out", "norm2_g", "w_up", "w_down")


def _pack(vectors):
    rows = []
    for vec in vectors:
        n = vec.shape[0]
        rows.append(jnp.pad(vec, (0, -n % LANES)).reshape(-1, LANES))
    out = jnp.concatenate(rows, axis=0)
    return jnp.pad(out, ((0, -out.shape[0] % 8), (0, 0)))


def _unpack(packed, sizes):
    out, row = [], 0
    for n in sizes:
        nr = -(-n // LANES)
        out.append(packed[row:row + nr].reshape(-1)[:n])
        row += nr
    return out


def kernel(x, mem, norm1_g, w_in, b_f, conv_w, fox_q_g, fox_k_g, mem_norm_g, w_mem_kv, mem_q_g, mem_k_g, w_conv_out, w_fox_out, w_mem_out, w_out, norm2_g, w_up, w_down, loss_target, m_norm1_g, m_w_in, m_b_f, m_conv_w, m_fox_q_g, m_fox_k_g, m_mem_norm_g, m_w_mem_kv, m_mem_q_g, m_mem_k_g, m_w_conv_out, m_w_fox_out, m_w_mem_out, m_w_out, m_norm2_g, m_w_up, m_w_down, v_norm1_g, v_w_in, v_b_f, v_conv_w, v_fox_q_g, v_fox_k_g, v_mem_norm_g, v_w_mem_kv, v_mem_q_g, v_mem_k_g, v_w_conv_out, v_w_fox_out, v_w_mem_out, v_w_out, v_norm2_g, v_w_up, v_w_down):
    args = dict(locals())
    wts = {n: args[n] for n in WEIGHTS}
    ms = {n: args["m_" + n] for n in WEIGHTS}
    vs = {n: args["v_" + n] for n in WEIGHTS}
    x_pos, y_pos, c_pos = _position()
    me = _index(x_pos, y_pos, c_pos)

    shards = {n: wts[n].astype(BF16) for n in BIG if n != "w_in"}
    rows_in = w_in.shape[1]
    shards["w_in"] = jnp.pad(w_in.T.astype(BF16), ((0, _padded_rows(rows_in) - rows_in), (0, 0)))
    wi, cw8 = _run_rider("all_gather_first", _gather_rider([shards["w_in"], conv_w], True))
    full = {"w_in": wi}
    small = {n: wts[n] for n in SMALL}
    small["conv_w"] = _unblock(cw8)
    comm = {"shards": shards, "c": c_pos.astype(jnp.int32).reshape(1)}

    loss, grad_x, parts, gs = _local_step(x[0], mem[0], loss_target[0], full, small, comm)

    out_g, out_d, out_m, out_v = {}, {}, {}, {}
    for n in BIG:
        if n == "w_in":
            res = _adamw("adamw_" + n, parts[n], wts[n].T, ms[n].T, vs[n].T)
            out_g[n], out_d[n], out_m[n], out_v[n] = (r.T for r in res)
        else:
            out_g[n], out_d[n], out_m[n], out_v[n] = _adamw("adamw_" + n, parts[n], wts[n], ms[n], vs[n])

    small_sizes = [int(math.prod(gs[n].shape)) for n in SMALL]
    packed = _pack([gs[n].reshape(-1) for n in SMALL])
    gsum = _sum_parts("sum_small", _run_rider("exchange_small", _broadcast_rider([packed]))[0])
    gsmall = dict(zip(SMALL, _unpack(gsum, small_sizes)))
    cols = conv_w.shape[1]
    gsmall["conv_w"] = lax.dynamic_slice(gsmall["conv_w"].reshape(CONV_TAPS, -1), (0, me * cols), (CONV_TAPS, cols)).reshape(-1)
    pg, pw, pm, pv = (_pack([src[n].reshape(-1) for n in SMALL]) for src in (gsmall, wts, ms, vs))
    _, sd, sm, sv = _adamw("adamw_small", pg[None], pw, pm, pv)
    local_sizes = [int(math.prod(wts[n].shape)) for n in SMALL]
    for dst, src in ((out_d, sd), (out_m, sm), (out_v, sv)):
        for n, val in zip(SMALL, _unpack(src, local_sizes)):
            dst[n] = val.reshape(wts[n].shape)
    for n in SMALL:
        out_g[n] = gsmall[n].reshape(wts[n].shape)

    loss = lax.psum(loss, MESH_AXES)
    return (loss, grad_x[None], *[out_g[n] for n in WEIGHTS], *[out_d[n] for n in WEIGHTS],
            *[out_m[n] for n in WEIGHTS], *[out_v[n] for n in WEIGHTS])
```

```python
import math

import numpy as np
import jax
import jax.numpy as jnp
from jax import lax
from jax.experimental import pallas as pl
from jax.experimental.pallas import tpu as pltpu

F32 = jnp.float32
BF16 = jnp.bfloat16

EPS = 1e-6
N_DEV = 8
N_CHIPS = 4
FOX_HEAD_DIM = 128
MEM_HEADS = 4
CONV_TAPS = 3
N_BRANCHES = 3
F_ROWS = 16

ADAM_LR = 0.001
ADAM_B1 = 0.9
ADAM_B2 = 0.999
ADAM_EPS = 1e-08
ADAM_WD = 0.01
ADAM_STEP = 10

V7X_VMEM_BYTES = 64 * 1024 * 1024
VMEM_LIMIT = V7X_VMEM_BYTES * 3 // 4
LANES = 128
NEG = -1e30
MAX_KEYS = 1024

MESH_AXES = ("x", "y", "c")
MESH = pl.DeviceIdType.MESH
ANY = pl.BlockSpec(memory_space=pl.ANY)

NN = (((1,), (0,)), ((), ()))
NT = (((1,), (1,)), ((), ()))
TN = (((0,), (0,)), ((), ()))


def _params(sem):
    return pltpu.CompilerParams(dimension_semantics=sem, vmem_limit_bytes=VMEM_LIMIT)


def _dot(a, b, dn):
    return lax.dot_general(a, b, dn, preferred_element_type=F32)


def _tile(n, t):
    if n <= t:
        return n
    for step in (LANES, 16):
        for cand in range(t - t % step, 0, -step):
            if n % cand == 0:
                return cand
    raise ValueError((n, t))


class _Rider:
    def __init__(self, ins, out_shapes, sem_shapes, start, finish, middle=None):
        self.ins, self.out_shapes, self.sem_shapes = list(ins), list(out_shapes), list(sem_shapes)
        self.start, self.finish, self.middle = start, finish, middle


def _position():
    return lax.axis_index("x"), lax.axis_index("y"), lax.axis_index("c")


def _index(px, py, pc):
    return 4 * px + 2 * py + pc


def _dma_sems(n, per):
    return [pltpu.SemaphoreType.DMA((n, per)), pltpu.SemaphoreType.DMA((n, per)), pltpu.SemaphoreType.DMA((n,))]


def _gather_rider(shards, pass_on):
    n = len(shards)

    def copies(ins, outs, sems):
        send_sems, recv_sems, local_sems = sems
        x, y, c = _position()
        me, sibling = (x, y, c), (x, y, 1 - c)
        chips = [(1 - x, y), (x, 1 - y), (1 - x, 1 - y)]

        def copy(a, k, block, to, src=None, k_send=None):
            rows = outs[a].at[_index(*block)]
            return pltpu.make_async_remote_copy(
                src_ref=rows if src is None else src, dst_ref=rows,
                send_sem=send_sems.at[a, k if k_send is None else k_send], recv_sem=recv_sems.at[a, k],
                device_id=to, device_id_type=MESH)

        mine = [pltpu.make_async_copy(ins[a], outs[a].at[_index(*me)], local_sems.at[a]) for a in range(n)]
        first = []
        for a in range(n):
            first.append(copy(a, 0, me, sibling, src=ins[a]))
            first += [copy(a, 1 + j, me, (*chips[j], c), src=ins[a]) for j in range(2 if pass_on else 3)]
        return copy, mine, first, me, sibling, chips, c

    def start(ins, outs, sems):
        _, mine, first, *_ = copies(ins, outs, sems)
        for cp in mine + first:
            cp.start()

    def by_kind(c, fn):
        if pass_on:
            pl.when(c == 1)(lambda: fn(0, 1))
            pl.when(c == 0)(lambda: fn(1, 0))
        else:
            fn(0, 1)

    def onward(copy, a, j_on, j_to, chips, c, sibling):
        third = [copy(a, 3, (*chips[j_on], c), (*chips[j_to], c), k_send=7)] if pass_on else []
        return third + [copy(a, 4 + j_on, (*chips[j_on], c), sibling)], [copy(a, 4 + j_to, (*chips[j_to], c), sibling)]

    def middle(ins, outs, sems):
        copy, _, _, me, sibling, chips, c = copies(ins, outs, sems)

        def fn(j_on, j_to):
            for a in range(n):
                for j, after in zip((j_on, j_to), onward(copy, a, j_on, j_to, chips, c, sibling)):
                    copy(a, 1 + j, (*chips[j], c), me).wait_recv()
                    for cp in after:
                        cp.start()

        by_kind(c, fn)

    def finish(ins, outs, sems):
        copy, mine, first, me, sibling, chips, c = copies(ins, outs, sems)

        def fn(j_on, j_to):
            passed = [cp for a in range(n) for after in onward(copy, a, j_on, j_to, chips, c, sibling) for cp in after]
            for a in range(n):
                copy(a, 3, (*chips[2], c), me).wait_recv()
                passed.append(copy(a, 6, (*chips[2], c), sibling))
                passed[-1].start()
            for a in range(n):
                copy(a, 0, sibling, me).wait_recv()
                for j, chip in enumerate(chips):
                    copy(a, 4 + j, (*chip, 1 - c), me).wait_recv()
            for cp in first + passed:
                cp.wait_send()
            for cp in mine:
                cp.wait()

        by_kind(c, fn)

    out_shapes = [jax.ShapeDtypeStruct((N_DEV,) + s.shape, s.dtype) for s in shards]
    return _Rider(shards, out_shapes, _dma_sems(n, 8), start, finish, middle)


def _pair_rider(grads):
    n = len(grads)

    def copies(ins, outs, sems):
        send_sems, recv_sems, _ = sems
        x, y, c = _position()
        return [pltpu.make_async_remote_copy(
            src_ref=ins[a].at[2 * q + (1 - c)], dst_ref=outs[a].at[q],
            send_sem=send_sems.at[a, q], recv_sem=recv_sems.at[a, q], device_id=(x, y, 1 - c), device_id_type=MESH)
            for a in range(n) for q in range(N_CHIPS)]

    def start(ins, outs, sems):
        for cp in copies(ins, outs, sems):
            cp.start()

    def finish(ins, outs, sems):
        cps = copies(ins, outs, sems)
        for cp in cps:
            cp.wait_recv()
        for cp in cps:
            cp.wait_send()

    out_shapes = [jax.ShapeDtypeStruct((N_CHIPS,) + g.shape[1:], g.dtype) for g in grads]
    return _Rider(grads, out_shapes, _dma_sems(n, N_CHIPS), start, finish)


def _chip_rider(parts):
    n = len(parts)

    def copies(ins, outs, sems):
        send_sems, recv_sems, local_sems = sems
        x, y, c = _position()
        q_me = 2 * x + y
        chips = [(1 - x, y), (x, 1 - y), (1 - x, 1 - y)]
        mine = [pltpu.make_async_copy(ins[a].at[q_me], outs[a].at[q_me], local_sems.at[a]) for a in range(n)]
        sends, arrivals = [], []
        for a in range(n):
            for j, (tx, ty) in enumerate(chips):
                q_t = 2 * tx + ty
                sends.append(pltpu.make_async_remote_copy(
                    src_ref=ins[a].at[q_t], dst_ref=outs[a].at[q_me],
                    send_sem=send_sems.at[a, j], recv_sem=recv_sems.at[a, j], device_id=(tx, ty, c), device_id_type=MESH))
                arrivals.append(pltpu.make_async_remote_copy(
                    src_ref=ins[a].at[q_t], dst_ref=outs[a].at[q_t],
                    send_sem=send_sems.at[a, j], recv_sem=recv_sems.at[a, j], device_id=(tx, ty, c), device_id_type=MESH))
        return mine, sends, arrivals

    def start(ins, outs, sems):
        mine, sends, _ = copies(ins, outs, sems)
        for cp in mine + sends:
            cp.start()

    def finish(ins, outs, sems):
        mine, sends, arrivals = copies(ins, outs, sems)
        for cp in arrivals:
            cp.wait_recv()
        for cp in sends:
            cp.wait_send()
        for cp in mine:
            cp.wait()

    out_shapes = [jax.ShapeDtypeStruct(p.shape, p.dtype) for p in parts]
    return _Rider(parts, out_shapes, _dma_sems(n, 3), start, finish)


def _broadcast_rider(values):
    n = len(values)

    def copies(ins, outs, sems):
        send_sems, recv_sems, local_sems = sems
        x, y, c = _position()
        me = _index(x, y, c)

        def peer(k):
            return (1 - x if k & 4 else x, 1 - y if k & 2 else y, 1 - c if k & 1 else c)

        mine = [pltpu.make_async_copy(ins[a], outs[a].at[me], local_sems.at[a]) for a in range(n)]
        sends, arrivals = [], []
        for a in range(n):
            for k in range(1, N_DEV):
                common = dict(send_sem=send_sems.at[a, k - 1], recv_sem=recv_sems.at[a, k - 1], device_id=peer(k), device_id_type=MESH)
                sends.append(pltpu.make_async_remote_copy(src_ref=ins[a], dst_ref=outs[a].at[me], **common))
                arrivals.append(pltpu.make_async_remote_copy(src_ref=ins[a], dst_ref=outs[a].at[_index(*peer(k))], **common))
        return mine, sends, arrivals

    def start(ins, outs, sems):
        mine, sends, _ = copies(ins, outs, sems)
        for cp in mine + sends:
            cp.start()

    def finish(ins, outs, sems):
        mine, sends, arrivals = copies(ins, outs, sems)
        for cp in arrivals:
            cp.wait_recv()
        for cp in sends:
            cp.wait_send()
        for cp in mine:
            cp.wait()

    out_shapes = [jax.ShapeDtypeStruct((N_DEV,) + v.shape, v.dtype) for v in values]
    return _Rider(values, out_shapes, _dma_sems(n, 7), start, finish)


def _join_riders(*riders):
    def each(fn_name, ins, outs, sems):
        i = o = s = 0
        for r in riders:
            n_i, n_o, n_s = len(r.ins), len(r.out_shapes), len(r.sem_shapes)
            if getattr(r, fn_name) is not None:
                getattr(r, fn_name)(ins[i:i + n_i], outs[o:o + n_o], sems[s:s + n_s])
            i, o, s = i + n_i, o + n_o, s + n_s

    middle = (lambda ins, outs, sems: each("middle", ins, outs, sems)) if any(r.middle for r in riders) else None
    return _Rider([a for r in riders for a in r.ins], [a for r in riders for a in r.out_shapes],
                  [a for r in riders for a in r.sem_shapes],
                  lambda ins, outs, sems: each("start", ins, outs, sems),
                  lambda ins, outs, sems: each("finish", ins, outs, sems), middle)


def _run_rider(name, rider):
    n_in, n_out = len(rider.ins), len(rider.out_shapes)

    def body(*refs):
        ins, outs, sems = refs[:n_in], refs[n_in:n_in + n_out], refs[n_in + n_out:]
        rider.start(ins, outs, sems)
        if rider.middle is not None:
            rider.middle(ins, outs, sems)
        rider.finish(ins, outs, sems)

    return pl.pallas_call(
        body, name=name, in_specs=[ANY] * n_in, out_specs=[ANY] * n_out, out_shape=rider.out_shapes,
        scratch_shapes=rider.sem_shapes)(*rider.ins)


class _Host:
    def __init__(self, rider):
        self.rider = rider
        self.n_in = len(rider.ins) if rider else 0
        self.n_out = len(rider.out_shapes) if rider else 0
        self.n_sem = len(rider.sem_shapes) if rider else 0
        self.ins = rider.ins if rider else []
        self.in_specs = [ANY] * self.n_in
        self.out_specs = [ANY] * self.n_out
        self.out_shapes = rider.out_shapes if rider else []
        self.scratch = rider.sem_shapes if rider else []

    def run(self, first, last, ins, outs, sems, compute, midway=None):
        if self.rider is None:
            compute()
            return

        @pl.when(first)
        def _():
            self.rider.start(ins, outs, sems)

        compute()
        if self.rider.middle is not None and midway is not None:
            pl.when(midway)(lambda: self.rider.middle(ins, outs, sems))

        @pl.when(last)
        def _():
            if self.rider.middle is not None and midway is None:
                self.rider.middle(ins, outs, sems)
            self.rider.finish(ins, outs, sems)


def _matmul(name, kind, a, b, *, tm, tn, tk, outs, epilogue=None, extras=(), out_blocks=False, rider=None, j_outer=False):
    if kind == "nn":
        (m, kdim), n = a.shape, b.shape[1]
    elif kind == "nt":
        (m, kdim), n = a.shape, b.shape[0]
    else:
        (kdim, m), n = a.shape, b.shape[1]
    if out_blocks:
        tn = min(tn, n // N_DEV)
    tm, tn, tk = _tile(m, tm), _tile(n, tn), _tile(kdim, tk)
    ni, nj, nk = m // tm, n // tn, kdim // tk

    def spec(shape, fn):
        return pl.BlockSpec(shape, (lambda g0, g1, k: fn(g1, g0, k)) if j_outer else fn)

    a_spec = spec((tk, tm), lambda i, j, k: (k, i)) if kind == "tn" else spec((tm, tk), lambda i, j, k: (i, k))
    b_spec = spec((tn, tk), lambda i, j, k: (j, k)) if kind == "nt" else spec((tk, tn), lambda i, j, k: (k, j))
    dn = {"nn": NN, "nt": NT, "tn": TN}[kind]

    tile_spec = spec((tm, tn), lambda i, j, k: (i, j))
    row_spec = spec((1, tn), lambda i, j, k: (0, j))
    if out_blocks:
        width = n // N_DEV
        r_out = width // tn
        out_shape = [jax.ShapeDtypeStruct((N_DEV, m, width), dt) for dt in outs]
        out_specs = [spec((None, tm, tn), lambda i, j, k: (j // r_out, i, j % r_out)) for _ in outs]
    else:
        out_shape = [jax.ShapeDtypeStruct((m, n), dt) for dt in outs]
        out_specs = [tile_spec for _ in outs]
    n_ex, n_out = len(extras), len(outs)
    host = _Host(rider)
    n_acc = 1 if nk > 1 else 0

    def body(*refs):
        a_ref, b_ref = refs[0], refs[1]
        pos = 2
        ex_refs = refs[pos:pos + n_ex]; pos += n_ex
        r_ins = refs[pos:pos + host.n_in]; pos += host.n_in
        out_refs = refs[pos:pos + n_out]; pos += n_out
        r_outs = refs[pos:pos + host.n_out]; pos += host.n_out
        acc_ref = refs[pos] if n_acc else None
        sems = refs[pos + n_acc:]
        i, j, k = pl.program_id(1 if j_outer else 0), pl.program_id(0 if j_outer else 1), pl.program_id(2)

        def finish_tile(acc):
            vals = (acc,) if epilogue is None else epilogue(acc, *[e[...] for e in ex_refs])
            for o_ref, v in zip(out_refs, vals):
                o_ref[...] = v.astype(o_ref.dtype)

        def compute():
            part = _dot(a_ref[...], b_ref[...], dn)
            if nk == 1:
                finish_tile(part)
                return

            @pl.when(k == 0)
            def _():
                acc_ref[...] = part

            @pl.when(jnp.logical_and(k > 0, k < nk - 1))
            def _():
                acc_ref[...] += part

            @pl.when(k == nk - 1)
            def _():
                finish_tile(acc_ref[...] + part)

        first = jnp.logical_and(jnp.logical_and(i == 0, j == 0), k == 0)
        last = jnp.logical_and(jnp.logical_and(i == ni - 1, j == nj - 1), k == nk - 1)
        step = (pl.program_id(0) * (ni if j_outer else nj) + pl.program_id(1)) * nk + k
        host.run(first, last, r_ins, r_outs, sems, compute, midway=step == (ni * nj * nk * 3) // 5)

    sem = ("arbitrary",) * 3 if rider else ("parallel", "parallel", "arbitrary")
    res = pl.pallas_call(
        body,
        name=name,
        grid=(nj, ni, nk) if j_outer else (ni, nj, nk),
        in_specs=[a_spec, b_spec] + [row_spec if e.shape[0] == 1 else tile_spec for e in extras] + host.in_specs,
        out_specs=out_specs + host.out_specs,
        out_shape=out_shape + host.out_shapes,
        scratch_shapes=([pltpu.VMEM((tm, tn), F32)] if n_acc else []) + host.scratch,
        compiler_params=_params(sem),
    )(a, b, *extras, *host.ins)
    return res[0] if len(res) == 1 else res


def _rms_fwd(name, x, g, tm=512, rider=None):
    t, d = x.shape
    tm = _tile(t, tm)
    n = t // tm
    host = _Host(rider)

    def body(*refs):
        x_ref, g_ref = refs[:2]
        r_ins = refs[2:2 + host.n_in]
        h_ref = refs[2 + host.n_in]
        r_outs = refs[3 + host.n_in:3 + host.n_in + host.n_out]
        sems = refs[3 + host.n_in + host.n_out:]
        i = pl.program_id(0)

        def compute():
            xf = x_ref[...]
            r = lax.rsqrt(jnp.mean(xf * xf, axis=-1, keepdims=True) + EPS)
            h_ref[...] = (xf * r * g_ref[...]).astype(h_ref.dtype)

        host.run(i == 0, i == n - 1, r_ins, r_outs, sems, compute, midway=i == (n * 3) // 5)

    res = pl.pallas_call(
        body,
        name=name,
        grid=(n,),
        in_specs=[pl.BlockSpec((tm, d), lambda i: (i, 0)), pl.BlockSpec((1, d), lambda i: (0, 0))] + host.in_specs,
        out_specs=[pl.BlockSpec((tm, d), lambda i: (i, 0))] + host.out_specs,
        out_shape=[jax.ShapeDtypeStruct((t, d), BF16)] + host.out_shapes,
        scratch_shapes=host.scratch,
        compiler_params=_params(("arbitrary",) if rider else ("parallel",)),
    )(x, g.reshape(1, d), *host.ins)
    return res[0] if len(res) == 1 else res


def _rms_bwd(name, dh, x, g, res=None, tm=256):
    t, d = x.shape
    tm = _tile(t, tm)
    has_res = res is not None

    def body(*refs):
        if has_res:
            dh_ref, x_ref, g_ref, res_ref, dx_ref, dxb_ref, gg_ref, ss_ref = refs
        else:
            dh_ref, x_ref, g_ref, dx_ref, dxb_ref, gg_ref, ss_ref = refs
        i = pl.program_id(0)
        xf = x_ref[...]
        r = lax.rsqrt(jnp.mean(xf * xf, axis=-1, keepdims=True) + EPS)
        xh = xf * r
        dhf = dh_ref[...].astype(F32)
        dxh = dhf * g_ref[...]
        dx = r * (dxh - xh * jnp.mean(dxh * xh, axis=-1, keepdims=True))

        @pl.when(i == 0)
        def _():
            gg_ref[...] = jnp.zeros_like(gg_ref)
            ss_ref[...] = jnp.zeros_like(ss_ref)

        if has_res:
            resf = res_ref[...]
            dx = dx + resf
            ss_ref[...] += jnp.sum(jnp.sum(resf * resf, axis=0, keepdims=True), axis=1, keepdims=True)
        dx_ref[...] = dx
        dxb_ref[...] = dx.astype(BF16)
        gg_ref[...] += jnp.sum(dhf * xh, axis=0, keepdims=True)

    row = pl.BlockSpec((tm, d), lambda i: (i, 0))
    vec = pl.BlockSpec((1, d), lambda i: (0, 0))
    one = pl.BlockSpec((1, 1), lambda i: (0, 0))
    ins = [dh, x, g.reshape(1, d)] + ([res] if has_res else [])
    dx, dxb, gg, ss = pl.pallas_call(
        body,
        name=name,
        grid=(t // tm,),
        in_specs=[row, row, vec] + ([row] if has_res else []),
        out_specs=[row, row, vec, one],
        out_shape=[jax.ShapeDtypeStruct((t, d), F32), jax.ShapeDtypeStruct((t, d), BF16), jax.ShapeDtypeStruct((1, d), F32),
                   jax.ShapeDtypeStruct((1, 1), F32)],
        compiler_params=_params(("arbitrary",)),
    )(*ins)
    return dx, dxb, gg.reshape(d), ss[0, 0]


def _head_rms(xf):
    r = lax.rsqrt(jnp.mean(xf * xf, axis=-1, keepdims=True) + EPS)
    return xf * r, r


def _head_rms_bwd(dy, xn, r, g):
    dxh = dy * g
    dx = r * (dxh - xn * jnp.mean(dxh * xn, axis=-1, keepdims=True))
    return dx, jnp.sum(dy * xn, axis=0, keepdims=True)


def _col_to_row(col):
    n = col.shape[0]
    eye = lax.broadcasted_iota(jnp.int32, (n, n), 0) == lax.broadcasted_iota(jnp.int32, (n, n), 1)
    return jnp.sum(jnp.where(eye, col, 0.0), axis=0, keepdims=True)


def _row_to_col(row):
    n = row.shape[1]
    eye = lax.broadcasted_iota(jnp.int32, (n, n), 0) == lax.broadcasted_iota(jnp.int32, (n, n), 1)
    return jnp.sum(jnp.where(eye, row, 0.0), axis=1, keepdims=True)


def _dproj_args(dproj, n_in):
    if dproj is None:
        return [], [], {}
    return [dproj], [ANY], {n_in: 0}


def _shift_down(u, s, rows):
    return jnp.where(rows >= s, pltpu.roll(u, s, axis=0), 0.0)


def _shift_up(u, s, rows, t):
    return jnp.where(rows < t - s, pltpu.roll(u, t - s, axis=0), 0.0)


def _conv_fwd(proj, off, conv_w, cb):
    t = proj.shape[0]
    c = conv_w.shape[1]
    blk0 = off // (3 * cb)

    def body(p_ref, w_ref, y_ref):
        rows = lax.broadcasted_iota(jnp.int32, (t, cb), 0)
        bg = p_ref[:, 0:cb].astype(F32)
        u = p_ref[:, cb:2 * cb].astype(F32) * p_ref[:, 2 * cb:3 * cb].astype(F32)
        w = w_ref[...]
        conv = w[2:3] * u + w[1:2] * _shift_down(u, 1, rows) + w[0:1] * _shift_down(u, 2, rows)
        y_ref[...] = (bg * conv).astype(y_ref.dtype)

    return pl.pallas_call(
        body,
        name="conv_fwd",
        grid=(c // cb,),
        in_specs=[pl.BlockSpec((t, 3 * cb), lambda j: (0, blk0 + j)), pl.BlockSpec((CONV_TAPS, cb), lambda j: (0, j))],
        out_specs=pl.BlockSpec((t, cb), lambda j: (0, j)),
        out_shape=jax.ShapeDtypeStruct((t, c), BF16),
        compiler_params=_params(("parallel",)),
    )(proj, conv_w)


def _conv_bwd(proj, off, conv_w, dy, cb, dproj, rider=None):
    t = proj.shape[0]
    c = conv_w.shape[1]
    blk0 = off // (3 * cb)
    nj = c // cb
    host = _Host(rider)

    def body(*refs):
        p_ref, w_ref, dy_ref = refs[:3]
        r_ins = refs[4:4 + host.n_in]
        dp_ref, gw_ref = refs[4 + host.n_in:6 + host.n_in]
        r_outs = refs[6 + host.n_in:6 + host.n_in + host.n_out]
        sems = refs[6 + host.n_in + host.n_out:]
        j = pl.program_id(0)

        def compute():
            rows = lax.broadcasted_iota(jnp.int32, (t, cb), 0)
            bg = p_ref[:, 0:cb].astype(F32)
            cg = p_ref[:, cb:2 * cb].astype(F32)
            v = p_ref[:, 2 * cb:3 * cb].astype(F32)
            u = cg * v
            w = w_ref[...]
            u1 = _shift_down(u, 1, rows)
            u2 = _shift_down(u, 2, rows)
            conv = w[2:3] * u + w[1:2] * u1 + w[0:1] * u2
            dyf = dy_ref[...].astype(F32)
            dconv = dyf * bg
            du = w[2:3] * dconv + w[1:2] * _shift_up(dconv, 1, rows, t) + w[0:1] * _shift_up(dconv, 2, rows, t)
            dp_ref[:, 0:cb] = (dyf * conv).astype(dp_ref.dtype)
            dp_ref[:, cb:2 * cb] = (du * v).astype(dp_ref.dtype)
            dp_ref[:, 2 * cb:3 * cb] = (du * cg).astype(dp_ref.dtype)
            gw_ref[0:1, :] = jnp.sum(dconv * u2, axis=0, keepdims=True)
            gw_ref[1:2, :] = jnp.sum(dconv * u1, axis=0, keepdims=True)
            gw_ref[2:3, :] = jnp.sum(dconv * u, axis=0, keepdims=True)

        host.run(j == 0, j == nj - 1, r_ins, r_outs, sems, compute)

    res = pl.pallas_call(
        body,
        name="conv_bwd",
        grid=(nj,),
        in_specs=[
            pl.BlockSpec((t, 3 * cb), lambda j: (0, blk0 + j)),
            pl.BlockSpec((CONV_TAPS, cb), lambda j: (0, j)),
            pl.BlockSpec((t, cb), lambda j: (0, j)),
            ANY,
        ] + host.in_specs,
        out_specs=[pl.BlockSpec((t, 3 * cb), lambda j: (0, blk0 + j)), pl.BlockSpec((CONV_TAPS, cb), lambda j: (0, j))] + host.out_specs,
        out_shape=[jax.ShapeDtypeStruct(dproj.shape, dproj.dtype), jax.ShapeDtypeStruct((CONV_TAPS, c), F32)] + host.out_shapes,
        input_output_aliases={3: 0},
        scratch_shapes=host.scratch,
        compiler_params=_params(("arbitrary",)),
    )(proj, conv_w, dy, dproj, *host.ins)
    return res


def _lane_scan(x, reverse):
    lane = lax.broadcasted_iota(jnp.int32, x.shape, 1)
    s = 1
    while s < LANES:
        if reverse:
            x = x + jnp.where(lane < LANES - s, pltpu.roll(x, LANES - s, axis=1), 0.0)
        else:
            x = x + jnp.where(lane >= s, pltpu.roll(x, s, axis=1), 0.0)
        s *= 2
    return x


def _scan_rows(src_ref, dst_ref, t, reverse, fn=None):
    groups = list(range(t // LANES))
    if reverse:
        groups = groups[::-1]
    carry = None
    for gi in groups:
        sl = slice(gi * LANES, (gi + 1) * LANES)
        blk = src_ref[:, sl]
        if fn is not None:
            blk = fn(blk)
        blk = _lane_scan(blk, reverse)
        if carry is not None:
            blk = blk + carry
        dst_ref[:, sl] = blk
        carry = blk[:, 0:1] if reverse else blk[:, LANES - 1:LANES]


def _forget_fwd(z_row, b_col):
    rows, t = z_row.shape

    def body(z_ref, b_ref, c_ref):
        def logf(z):
            zz = z + b_ref[...]
            return jnp.minimum(zz, 0.0) - jnp.log(1.0 + jnp.exp(-jnp.abs(zz)))

        _scan_rows(z_ref, c_ref, t, False, logf)

    return pl.pallas_call(
        body,
        name="forget_fwd",
        out_shape=jax.ShapeDtypeStruct((rows, t), F32),
        compiler_params=pltpu.CompilerParams(vmem_limit_bytes=VMEM_LIMIT),
    )(z_row, b_col)


def _rows_to_colb(c_row3, tq):
    heads, _, t = c_row3.shape

    def body(r_ref, o_ref):
        o_ref[...] = jnp.broadcast_to(_row_to_col(r_ref[...]), (tq, LANES))

    return pl.pallas_call(
        body,
        name="rows_to_colb",
        grid=(heads, t // tq),
        in_specs=[pl.BlockSpec((None, 1, tq), lambda h, i: (h, 0, i))],
        out_specs=pl.BlockSpec((None, tq, LANES), lambda h, i: (h, i, 0)),
        out_shape=jax.ShapeDtypeStruct((heads, t, LANES), F32),
        compiler_params=_params(("parallel", "parallel")),
    )(c_row3)


def _forget_bwd(z_row, b_col, dc_row):
    rows, t = z_row.shape

    def body(z_ref, b_ref, dc_ref, dz_ref, db_ref, tmp_ref):
        _scan_rows(dc_ref, tmp_ref, t, True)
        zz = z_ref[...] + b_ref[...]
        dz = tmp_ref[...] * (1.0 / (1.0 + jnp.exp(zz)))
        dz_ref[...] = dz.astype(dz_ref.dtype)
        db_ref[...] = jnp.sum(dz, axis=1, keepdims=True)

    return pl.pallas_call(
        body,
        name="forget_bwd",
        out_shape=[jax.ShapeDtypeStruct((rows, t), BF16), jax.ShapeDtypeStruct((rows, 1), F32)],
        scratch_shapes=[pltpu.VMEM((rows, t), F32)],
        compiler_params=pltpu.CompilerParams(vmem_limit_bytes=VMEM_LIMIT),
    )(z_row, b_col, dc_row)


def _fox_fwd(proj, off, gq, gk, c_row3, c_colb, heads, tq, rider=None):
    t = proj.shape[0]
    hd = FOX_HEAD_DIM
    tq = _tile(t, tq)
    nq = t // tq
    blk0 = off // hd
    scale = 1.0 / math.sqrt(hd)
    host = _Host(rider)

    def body(*refs):
        q_ref, k_ref, v_ref, gq_ref, gk_ref, crow_ref, ccol_ref = refs[:7]
        r_ins = refs[7:7 + host.n_in]
        o_ref, lse_ref = refs[7 + host.n_in:9 + host.n_in]
        r_outs = refs[9 + host.n_in:9 + host.n_in + host.n_out]
        khat_ref, v_t_ref = refs[9 + host.n_in + host.n_out:11 + host.n_in + host.n_out]
        sems = refs[11 + host.n_in + host.n_out:]
        h, qi = pl.program_id(0), pl.program_id(1)

        def compute():
            eye = (lax.broadcasted_iota(jnp.int32, (hd, hd), 0) == lax.broadcasted_iota(jnp.int32, (hd, hd), 1)).astype(BF16)

            @pl.when(qi == 0)
            def _():
                kn, _ = _head_rms(k_ref[...].astype(F32))
                khat_ref[...] = (kn * gk_ref[...]).astype(BF16)
                v_t_ref[...] = _dot(eye, v_ref[...], NT).astype(BF16)

            qn, _ = _head_rms(q_ref[...].astype(F32))
            qhat = (qn * (gq_ref[...] * scale)).astype(BF16)
            crow = crow_ref[:, pl.ds(pl.multiple_of(qi * tq, tq), tq)]
            above = lax.broadcasted_iota(jnp.int32, (tq, tq), 1) >= lax.broadcasted_iota(jnp.int32, (tq, tq), 0)

            def tile(j, keys, carry, diagonal):
                m, l, acc_t = carry
                ks = pl.multiple_of(j * keys, keys)
                s_t = _dot(khat_ref[pl.ds(ks, keys), :], qhat, NT) - ccol_ref[pl.ds(ks, keys), 0:1]
                if diagonal:
                    s_t = jnp.where(above, s_t, NEG)
                m_new = jnp.maximum(m, jnp.max(s_t, axis=0, keepdims=True) + crow)
                alpha = jnp.exp(m - m_new)
                p_t = jnp.exp(s_t + (crow - m_new))
                l = alpha * l + jnp.sum(p_t, axis=0, keepdims=True)
                acc_t = alpha * acc_t + _dot(v_t_ref[:, pl.ds(ks, keys)], p_t.astype(BF16), NN)
                return m_new, l, acc_t

            init = (jnp.full((1, tq), NEG, F32), jnp.zeros((1, tq), F32), jnp.zeros((hd, tq), F32))
            pairs = qi // 2 if 2 * tq <= MAX_KEYS else 0
            carry = lax.fori_loop(0, pairs, lambda j, c: tile(j, 2 * tq, c, False), init)
            carry = lax.fori_loop(2 * pairs, qi, lambda j, c: tile(j, tq, c, False), carry)
            m, l, acc_t = tile(qi, tq, carry, True)
            o_ref[...] = _dot((acc_t / l).astype(BF16), eye, TN).astype(o_ref.dtype)
            lse_ref[...] = m + jnp.log(l)

        first = jnp.logical_and(h == 0, qi == 0)
        last = jnp.logical_and(h == heads - 1, qi == nq - 1)
        host.run(first, last, r_ins, r_outs, sems, compute)

    res = pl.pallas_call(
        body,
        name="fox_fwd",
        grid=(heads, nq),
        in_specs=[
            pl.BlockSpec((tq, hd), lambda h, i: (i, blk0 + 3 * h)),
            pl.BlockSpec((t, hd), lambda h, i: (0, blk0 + 3 * h + 1)),
            pl.BlockSpec((t, hd), lambda h, i: (0, blk0 + 3 * h + 2)),
            pl.BlockSpec((1, hd), lambda h, i: (0, 0)),
            pl.BlockSpec((1, hd), lambda h, i: (0, 0)),
            pl.BlockSpec((None, 1, t), lambda h, i: (h, 0, 0)),
            pl.BlockSpec((None, t, LANES), lambda h, i: (h, 0, 0)),
        ] + host.in_specs,
        out_specs=[pl.BlockSpec((tq, hd), lambda h, i: (i, h)), pl.BlockSpec((None, 1, tq), lambda h, i: (h, 0, i))] + host.out_specs,
        out_shape=[jax.ShapeDtypeStruct((t, heads * hd), BF16), jax.ShapeDtypeStruct((heads, 1, t), F32)] + host.out_shapes,
        scratch_shapes=[pltpu.VMEM((t, hd), BF16), pltpu.VMEM((hd, t), BF16)] + host.scratch,
        compiler_params=_params(("arbitrary", "arbitrary")),
    )(proj, proj, proj, gq.reshape(1, hd), gk.reshape(1, hd), c_row3, c_colb, *host.ins)
    return res


def _fox_bwd(proj, off, o, do, gq, gk, c_row3, c_colb, lse, heads, tq, dproj, rider=None):
    t = proj.shape[0]
    hd = FOX_HEAD_DIM
    tq = _tile(t, tq)
    nb = t // tq
    blk0 = off // hd
    scale = 1.0 / math.sqrt(hd)
    host = _Host(rider)
    n_fixed_in = 11

    def body(*refs):
        q_ref, k_ref, v_ref, o_ref, do_ref, gq_ref, gk_ref, crow_ref, ccol_ref, lse_ref = refs[:10]
        pos = n_fixed_in
        r_ins = refs[pos:pos + host.n_in]; pos += host.n_in
        dp_ref, dc_ref, ggq_ref, ggk_ref = refs[pos:pos + 4]; pos += 4
        r_outs = refs[pos:pos + host.n_out]; pos += host.n_out
        qhat_ref, khat_ref, khat_t_ref, dq_t_ref, dk_ref, dcq_ref, dck_ref, delta_ref = refs[pos:pos + 8]; pos += 8
        sems = refs[pos:]
        h = pl.program_id(0)

        def compute():
            qn, rq = _head_rms(q_ref[...].astype(F32))
            qhat_ref[...] = (qn * (gq_ref[...] * scale)).astype(BF16)
            kn, rk = _head_rms(k_ref[...].astype(F32))
            khat_ref[...] = (kn * gk_ref[...]).astype(BF16)
            eye = (lax.broadcasted_iota(jnp.int32, (hd, hd), 0) == lax.broadcasted_iota(jnp.int32, (hd, hd), 1)).astype(BF16)
            khat_t_ref[...] = _dot(eye, khat_ref[...], NT).astype(BF16)
            delta = jnp.sum(do_ref[...].astype(F32) * o_ref[...].astype(F32), axis=-1, keepdims=True)
            for b in range(nb):
                sl = slice(b * tq, (b + 1) * tq)
                delta_ref[:, sl] = _col_to_row(delta[sl, :])
            dq_t_ref[...] = jnp.zeros_like(dq_t_ref)
            dcq_ref[...] = jnp.zeros_like(dcq_ref)
            above = lax.broadcasted_iota(jnp.int32, (tq, tq), 1) >= lax.broadcasted_iota(jnp.int32, (tq, tq), 0)

            def kv_block(j, _):
                ks = pl.multiple_of(j * tq, tq)
                kh = khat_ref[pl.ds(ks, tq), :]
                kh_t = khat_t_ref[:, pl.ds(ks, tq)]
                vv = v_ref[pl.ds(ks, tq), :]
                ccol = ccol_ref[pl.ds(ks, tq), 0:1]

                def q_block(i, n, carry, diagonal):
                    dk, dv, dck = carry
                    qs = pl.multiple_of(i * tq, tq)
                    qh = qhat_ref[pl.ds(qs, n), :]
                    dob = do_ref[pl.ds(qs, n), :]
                    s_t = _dot(kh, qh, NT) + ((crow_ref[:, pl.ds(qs, n)] - lse_ref[:, pl.ds(qs, n)]) - ccol)
                    p_t = jnp.exp(s_t)
                    if diagonal:
                        p_t = jnp.where(above, p_t, 0.0)
                    ds_t = p_t * (_dot(vv, dob, NT) - delta_ref[:, pl.ds(qs, n)])
                    dsb = ds_t.astype(BF16)
                    dv = dv + _dot(p_t.astype(BF16), dob, NN)
                    dk = dk + _dot(dsb, qh, NN)
                    dq_t_ref[:, pl.ds(qs, n)] += _dot(kh_t, dsb, NN)
                    dcq_ref[:, pl.ds(qs, n)] += jnp.sum(ds_t, axis=0, keepdims=True)
                    dck = dck + jnp.sum(ds_t, axis=-1, keepdims=True)
                    return dk, dv, dck

                zero = jnp.zeros((tq, hd), F32)
                carry = q_block(j, tq, (zero, zero, jnp.zeros((tq, 1), F32)), True)
                pairs = (nb - 1 - j) // 2 if 2 * tq <= MAX_KEYS else 0
                carry = lax.fori_loop(0, pairs, lambda p, c: q_block(j + 1 + 2 * p, 2 * tq, c, False), carry)
                dk, dv, dck = lax.fori_loop(j + 1 + 2 * pairs, nb, lambda i, c: q_block(i, tq, c, False), carry)
                dk_ref[pl.ds(ks, tq), :] = dk
                dp_ref[pl.ds(ks, tq), 2 * hd:3 * hd] = dv.astype(dp_ref.dtype)
                dck_ref[pl.ds(ks, tq), :] = dck
                return 0

            lax.fori_loop(0, nb, kv_block, 0)

            dq, ggq = _head_rms_bwd(dq_t_ref[...].T * scale, qn, rq, gq_ref[...])
            dk, ggk = _head_rms_bwd(dk_ref[...], kn, rk, gk_ref[...])
            dp_ref[:, 0:hd] = dq.astype(dp_ref.dtype)
            dp_ref[:, hd:2 * hd] = dk.astype(dp_ref.dtype)
            for b in range(nb):
                sl = slice(b * tq, (b + 1) * tq)
                dc_ref[:, sl] = dcq_ref[:, sl] - _col_to_row(dck_ref[sl, :])

            @pl.when(h == 0)
            def _():
                ggq_ref[...] = jnp.zeros_like(ggq_ref)
                ggk_ref[...] = jnp.zeros_like(ggk_ref)

            ggq_ref[...] += ggq
            ggk_ref[...] += ggk

        host.run(h == 0, h == heads - 1, r_ins, r_outs, sems, compute)

    head_in = lambda part: pl.BlockSpec((t, hd), lambda h: (0, blk0 + 3 * h + part))
    vec = pl.BlockSpec((1, hd), lambda h: (0, 0))
    colb = pl.BlockSpec((None, t, LANES), lambda h: (h, 0, 0))
    res = pl.pallas_call(
        body,
        name="fox_bwd",
        grid=(heads,),
        in_specs=[
            head_in(0), head_in(1), head_in(2),
            pl.BlockSpec((t, hd), lambda h: (0, h)),
            pl.BlockSpec((t, hd), lambda h: (0, h)),
            vec, vec,
            pl.BlockSpec((None, 1, t), lambda h: (h, 0, 0)),
            colb,
            pl.BlockSpec((None, 1, t), lambda h: (h, 0, 0)),
            ANY,
        ] + host.in_specs,
        out_specs=[
            pl.BlockSpec((t, 3 * hd), lambda h: (0, blk0 // 3 + h)),
            pl.BlockSpec((None, 1, t), lambda h: (h, 0, 0)),
            vec, vec,
        ] + host.out_specs,
        out_shape=[
            jax.ShapeDtypeStruct(dproj.shape, dproj.dtype),
            jax.ShapeDtypeStruct((heads, 1, t), F32),
            jax.ShapeDtypeStruct((1, hd), F32),
            jax.ShapeDtypeStruct((1, hd), F32),
        ] + host.out_shapes,
        input_output_aliases={10: 0},
        scratch_shapes=[
            pltpu.VMEM((t, hd), BF16), pltpu.VMEM((t, hd), BF16), pltpu.VMEM((hd, t), BF16),
            pltpu.VMEM((hd, t), F32), pltpu.VMEM((t, hd), F32),
            pltpu.VMEM((1, t), F32), pltpu.VMEM((t, 1), F32), pltpu.VMEM((1, t), F32),
        ] + host.scratch,
        compiler_params=_params(("arbitrary",)),
    )(proj, proj, proj, o, do, gq.reshape(1, hd), gk.reshape(1, hd), c_row3, c_colb, lse, dproj, *host.ins)
    return res


def _mem_fwd(proj, off, kv, gq, gk, tq):
    t = proj.shape[0]
    m, width = kv.shape[0], kv.shape[1] // 2
    hd = width // MEM_HEADS
    tq = _tile(t, tq)
    blk0 = off // hd
    scale = 1.0 / math.sqrt(hd)

    def body(q_ref, k_ref, v_ref, gq_ref, gk_ref, o_ref):
        qn, _ = _head_rms(q_ref[...].astype(F32))
        kn, _ = _head_rms(k_ref[...])
        s = _dot((qn * gq_ref[...]).astype(BF16), (kn * gk_ref[...]).astype(BF16), NT) * scale
        p = jnp.exp(s - jnp.max(s, axis=-1, keepdims=True))
        p = p / jnp.sum(p, axis=-1, keepdims=True)
        o_ref[...] = _dot(p.astype(BF16), v_ref[...].astype(BF16), NN).astype(o_ref.dtype)

    vec = pl.BlockSpec((1, hd), lambda h, i: (0, 0))
    return pl.pallas_call(
        body,
        name="mem_fwd",
        grid=(MEM_HEADS, t // tq),
        in_specs=[
            pl.BlockSpec((tq, hd), lambda h, i: (i, blk0 + h)),
            pl.BlockSpec((m, hd), lambda h, i: (0, h)),
            pl.BlockSpec((m, hd), lambda h, i: (0, MEM_HEADS + h)),
            vec, vec,
        ],
        out_specs=pl.BlockSpec((tq, hd), lambda h, i: (i, h)),
        out_shape=jax.ShapeDtypeStruct((t, width), BF16),
        compiler_params=_params(("parallel", "parallel")),
    )(proj, kv, kv, gq.reshape(1, hd), gk.reshape(1, hd))


def _mem_bwd(proj, off, kv, do, gq, gk, tq, dproj, rider=None):
    t = proj.shape[0]
    m, width = kv.shape[0], kv.shape[1] // 2
    hd = width // MEM_HEADS
    tq = _tile(t, tq)
    nq = t // tq
    blk0 = off // hd
    scale = 1.0 / math.sqrt(hd)
    host = _Host(rider)

    def body(*refs):
        q_ref, k_ref, v_ref, do_ref, gq_ref, gk_ref = refs[:6]
        pos = 7
        r_ins = refs[pos:pos + host.n_in]; pos += host.n_in
        dq_ref, dk_ref, dv_ref, ggq_ref, ggk_ref = refs[pos:pos + 5]; pos += 5
        r_outs = refs[pos:pos + host.n_out]; pos += host.n_out
        dkh_ref, dvh_ref = refs[pos:pos + 2]; pos += 2
        sems = refs[pos:]
        h, i = pl.program_id(0), pl.program_id(1)

        def compute():
            qn, rq = _head_rms(q_ref[...].astype(F32))
            kn, rk = _head_rms(k_ref[...])
            qhat = (qn * gq_ref[...]).astype(BF16)
            khat = (kn * gk_ref[...]).astype(BF16)
            vb = v_ref[...].astype(BF16)
            dob = do_ref[...]
            s = _dot(qhat, khat, NT) * scale
            p = jnp.exp(s - jnp.max(s, axis=-1, keepdims=True))
            p = p / jnp.sum(p, axis=-1, keepdims=True)
            dp = _dot(dob, vb, NT)
            ds = p * (dp - jnp.sum(dp * p, axis=-1, keepdims=True))
            dsb = ds.astype(BF16)
            dq, ggq = _head_rms_bwd(_dot(dsb, khat, NN) * scale, qn, rq, gq_ref[...])
            dq_ref[...] = dq.astype(dq_ref.dtype)

            @pl.when(i == 0)
            def _():
                dkh_ref[...] = jnp.zeros_like(dkh_ref)
                dvh_ref[...] = jnp.zeros_like(dvh_ref)

            @pl.when(jnp.logical_and(h == 0, i == 0))
            def _():
                ggq_ref[...] = jnp.zeros_like(ggq_ref)
                ggk_ref[...] = jnp.zeros_like(ggk_ref)

            dkh_ref[...] += _dot(dsb, qhat, TN)
            dvh_ref[...] += _dot(p.astype(BF16), dob, TN)
            ggq_ref[...] += ggq

            @pl.when(i == nq - 1)
            def _():
                dk, ggk = _head_rms_bwd(dkh_ref[...] * scale, kn, rk, gk_ref[...])
                dk_ref[...] = dk.astype(dk_ref.dtype)
                dv_ref[...] = dvh_ref[...].astype(dv_ref.dtype)
                ggk_ref[...] += ggk

        first = jnp.logical_and(h == 0, i == 0)
        last = jnp.logical_and(h == MEM_HEADS - 1, i == nq - 1)
        host.run(first, last, r_ins, r_outs, sems, compute)

    vec = pl.BlockSpec((1, hd), lambda h, i: (0, 0))
    kblk = pl.BlockSpec((m, hd), lambda h, i: (0, h))
    res = pl.pallas_call(
        body,
        name="mem_bwd",
        grid=(MEM_HEADS, nq),
        in_specs=[
            pl.BlockSpec((tq, hd), lambda h, i: (i, blk0 + h)), kblk,
            pl.BlockSpec((m, hd), lambda h, i: (0, MEM_HEADS + h)),
            pl.BlockSpec((tq, hd), lambda h, i: (i, h)), vec, vec, ANY,
        ] + host.in_specs,
        out_specs=[pl.BlockSpec((tq, hd), lambda h, i: (i, blk0 + h)), kblk, kblk, vec, vec] + host.out_specs,
        out_shape=[
            jax.ShapeDtypeStruct(dproj.shape, dproj.dtype),
            jax.ShapeDtypeStruct((m, width), BF16),
            jax.ShapeDtypeStruct((m, width), BF16),
            jax.ShapeDtypeStruct((1, hd), F32),
            jax.ShapeDtypeStruct((1, hd), F32),
        ] + host.out_shapes,
        input_output_aliases={6: 0},
        scratch_shapes=[pltpu.VMEM((m, hd), F32), pltpu.VMEM((m, hd), F32)] + host.scratch,
        compiler_params=_params(("arbitrary", "arbitrary")),
    )(proj, kv, kv, do, gq.reshape(1, hd), gk.reshape(1, hd), dproj, *host.ins)
    dproj, dk, dv, ggq, ggk = res[:5]
    return (dproj, jnp.concatenate([dk, dv], axis=1), ggq.reshape(hd), ggk.reshape(hd), *res[5:])


def _sigmoid(z):
    return 1.0 / (1.0 + jnp.exp(-z))


def _merge_fwd(proj, ys, ws, tm, tc):
    t, cw = ys[0].shape
    d = ws[0].shape[1]
    tm = _tile(t, tm)

    def body(g_ref, ya_ref, yb_ref, yc_ref, wa_ref, wb_ref, wc_ref, oa_ref, ob_ref, oc_ref, out_ref):
        acc = jnp.zeros((tm, tc), F32)
        for s, (y_ref, w_ref, o_ref) in enumerate(((ya_ref, wa_ref, oa_ref), (yb_ref, wb_ref, ob_ref), (yc_ref, wc_ref, oc_ref))):
            o = _dot(y_ref[...], w_ref[...], NN)
            o_ref[...] = o.astype(o_ref.dtype)
            acc = acc + _sigmoid(g_ref[:, s * tc:(s + 1) * tc].astype(F32)) * o
        out_ref[...] = acc.astype(out_ref.dtype)

    blk = pl.BlockSpec((tm, tc), lambda i, j: (i, j))
    y_spec = pl.BlockSpec((tm, cw), lambda i, j: (i, 0))
    w_spec = pl.BlockSpec((cw, tc), lambda i, j: (0, j))
    return pl.pallas_call(
        body,
        name="merge_fwd",
        grid=(t // tm, d // tc),
        in_specs=[pl.BlockSpec((tm, 3 * tc), lambda i, j: (i, j))] + [y_spec] * 3 + [w_spec] * 3,
        out_specs=[blk] * 4,
        out_shape=[jax.ShapeDtypeStruct((t, d), BF16)] * 4,
        compiler_params=_params(("parallel", "parallel")),
    )(proj, *ys, *ws)


def _merge_bwd(proj, o3, dx1, w_out, tm, tc):
    t, d = o3[0].shape
    k = dx1.shape[1]
    tm = _tile(t, tm)

    def body(dx_ref, w_ref, g_ref, oa_ref, ob_ref, oc_ref, dg_ref, da_ref, db_ref, dc_ref):
        dmf = _dot(dx_ref[...], w_ref[...], NT)
        for s, (o_ref, do_ref) in enumerate(((oa_ref, da_ref), (ob_ref, db_ref), (oc_ref, dc_ref))):
            g = _sigmoid(g_ref[:, s * tc:(s + 1) * tc].astype(F32))
            do_ref[...] = (dmf * g).astype(do_ref.dtype)
            dg_ref[:, s * tc:(s + 1) * tc] = (dmf * o_ref[...].astype(F32) * g * (1.0 - g)).astype(dg_ref.dtype)

    blk = pl.BlockSpec((tm, tc), lambda i, j: (i, j))
    wide = pl.BlockSpec((tm, 3 * tc), lambda i, j: (i, j))
    return pl.pallas_call(
        body,
        name="merge_bwd",
        grid=(t // tm, d // tc),
        in_specs=[pl.BlockSpec((tm, k), lambda i, j: (i, 0)), pl.BlockSpec((tc, k), lambda i, j: (j, 0)), wide, blk, blk, blk],
        out_specs=[wide, blk, blk, blk],
        out_shape=[jax.ShapeDtypeStruct(proj.shape, BF16)] + [jax.ShapeDtypeStruct((t, d), BF16)] * 3,
        compiler_params=_params(("parallel", "parallel")),
    )(dx1, w_out, proj, *o3)


def _w_in_chunks(d, tc):
    cw = d // 2
    heads = cw // FOX_HEAD_DIM
    conv0, fox0, f0, mq0, gate0 = 0, 3 * cw, 6 * cw, 6 * cw + heads, 7 * cw + heads
    chunks = [(gate0 + s * d + j * tc, gate0 + s * d + (j + 1) * tc) for j in range(d // tc) for s in range(N_BRANCHES)]
    chunks += [(conv0 + s * cw + j * LANES, conv0 + s * cw + (j + 1) * LANES) for j in range(cw // LANES) for s in range(3)]
    chunks += [(fox0 + s * cw + j * FOX_HEAD_DIM, fox0 + s * cw + (j + 1) * FOX_HEAD_DIM) for j in range(heads) for s in range(3)]
    chunks.append((mq0, mq0 + cw))
    return chunks, (f0, f0 + heads)


ROW_TILE = 16
GROUP = 128
GROUP_BACK = 112
SCRATCH_ROWS = 2 * GROUP + 32


def _padded_rows(r):
    return -(-r // GROUP_BACK) * GROUP_BACK


def _rows_from(scr_ref, use, q8, fine, g):
    x = scr_ref[pl.ds(pl.multiple_of(q8 * 8, 8), g + 8), :]
    for s in range(8):
        @pl.when(fine == s)
        def _(s=s):
            use((x if s == 0 else pltpu.roll(x, g + 8 - s, axis=0))[0:g])


def _assemble(name, tbl, grid, step, in_specs, out_spec, out_shape, operands, g, w1, cols_of):
    has_f = len(in_specs) == 3
    k = out_shape.shape[-1]
    c = cols_of

    def body(*refs):
        t_ref, s1_ref, s2_ref = refs[:3]
        f_ref = refs[3] if has_f else None
        out_ref = refs[3 + has_f]
        scr1, scr2, scrf = refs[4 + has_f:]
        t = step()

        def put(y):
            out_ref[...] = y.astype(out_ref.dtype)

        @pl.when(t == 0)
        def _():
            scr1[...] = jnp.zeros_like(scr1)
            scr2[...] = jnp.zeros_like(scr2)
            scrf[...] = jnp.zeros_like(scrf)

        rows = lax.broadcasted_iota(jnp.int32, (g, k), 0)
        n1, a2 = t_ref[c["n1"], t], t_ref[c["a2"], t]
        scr1[0:w1, :] = (s1_ref[0] if len(s1_ref.shape) == 3 else s1_ref[...]).astype(F32)
        _rows_from(scr1, put, t_ref[c["q1"], t], t_ref[c["s1"], t], g)

        @pl.when(a2 < g)
        def _():
            scr2[g:g + s2_ref.shape[0], :] = s2_ref[...].astype(F32)
            _rows_from(scr2, lambda y: put(jnp.where(rows < n1, out_ref[...].astype(F32), y)),
                       t_ref[c["q2"], t], t_ref[c["s2"], t], g)

        if has_f:
            fa, fb = t_ref[c["fa"], t], t_ref[c["fb"], t]

            @pl.when(fb > fa)
            def _():
                scrf[g:g + f_ref.shape[0], :] = f_ref[...].astype(F32)
                inside = jnp.logical_and(rows >= fa, rows < fb)
                _rows_from(scrf, lambda y: put(jnp.where(inside, y, out_ref[...].astype(F32))),
                           t_ref[c["qf"], t], t_ref[c["sf"], t], g)

            valid = t_ref[c["valid"], t]

            @pl.when(valid < g)
            def _():
                out_ref[...] = jnp.where(rows < valid, out_ref[...].astype(F32), 0.0).astype(out_ref.dtype)

    return pl.pallas_call(
        body,
        name=name,
        grid_spec=pltpu.PrefetchScalarGridSpec(
            num_scalar_prefetch=1, grid=grid, in_specs=in_specs, out_specs=out_spec,
            scratch_shapes=[pltpu.VMEM((SCRATCH_ROWS, k), F32)] * 3),
        out_shape=out_shape,
        compiler_params=_params(("arbitrary",) * len(grid)),
    )(jnp.asarray(tbl), *operands)


def _pack_w_in(w8, d, tc):
    blocks, rp, k = w8.shape
    chunks, (f_lo, f_hi) = _w_in_chunks(d, tc)
    r = max(hi for _, hi in chunks) // blocks
    g, w1 = GROUP, GROUP + ROW_TILE
    table = []
    for lo, hi in chunks:
        for g0 in range(lo, hi, g):
            b1, r1 = divmod(g0, r)
            n1 = min(g, r - r1)
            st1 = min(r1 // ROW_TILE * ROW_TILE, rp - w1)
            o1, o2 = r1 - st1, g - n1
            b2 = b1 + 1 if n1 < g else 0
            table.append((b1, st1, o1 // 8, o1 % 8, n1, n1, b2, o2 // 8, o2 % 8))
    names = ("b1", "st1", "q1", "s1", "n1", "a2", "b2", "q2", "s2")
    cols_of = {n: i for i, n in enumerate(names)}
    tbl = np.array(table, np.int32).T
    c = cols_of
    w_all = _assemble(
        "pack_w_in", tbl, (len(table),), lambda: pl.program_id(0),
        [pl.BlockSpec((pl.Element(1), pl.Element(w1), pl.Element(k)), lambda i, t: (t[c["b1"], i], pl.multiple_of(t[c["st1"], i], ROW_TILE), 0)),
         pl.BlockSpec((None, g, k), lambda i, t: (t[c["b2"], i], 0, 0))],
        pl.BlockSpec((g, k), lambda i, t: (i, 0)),
        jax.ShapeDtypeStruct((len(table) * g, k), w8.dtype), [w8, w8], g, w1, cols_of)
    fb, fr = divmod(f_lo, r)
    return w_all, jnp.pad(w8[fb, fr:fr + f_hi - f_lo], ((0, F_ROWS - (f_hi - f_lo)), (0, 0)))


def _unpack_g_in(g_all, g_f, d, tc, blocks):
    n_all, k = g_all.shape
    chunks, (f_lo, f_hi) = _w_in_chunks(d, tc)
    r = max(hi for _, hi in chunks) // blocks
    rp = _padded_rows(r)
    g, w1 = GROUP_BACK, GROUP_BACK + ROW_TILE
    pos, spans = 0, [(f_lo, f_hi, None)]
    for lo, hi in chunks:
        spans.append((lo, hi, pos))
        pos += hi - lo
    spans.sort()
    table = []
    for b in range(blocks):
        for l0 in range(0, rp, g):
            valid = max(0, min(g, r - l0))
            g0, segs, fa, fb, of = b * r + l0, [], 0, 0, 0
            for lo, hi, p in spans:
                a, e = max(lo, g0), min(hi, g0 + valid)
                if a < e and p is None:
                    fa, fb, of = a - g0, e - g0, g + (a - lo) - (a - g0)
                elif a < e:
                    segs.append((a - g0, p + a - lo, e - a))
            assert len(segs) <= 2 and (not segs or segs[0][0] == 0 or len(segs) == 1)
            first = segs[0] if segs and segs[0][0] == 0 else (0, 0, 0)
            second = segs[-1] if segs and segs[-1][0] > 0 else (g, 0, 0)
            st1 = min(first[1] // ROW_TILE * ROW_TILE, n_all - w1)
            o1, o2 = first[1] - st1, g - second[0]
            assert second[1] % GROUP == 0
            table.append((st1, o1 // 8, o1 % 8, first[2], second[0], second[1] // GROUP, o2 // 8, o2 % 8,
                          fa, fb, of // 8, of % 8, valid))
    names = ("st1", "q1", "s1", "n1", "a2", "j2", "q2", "s2", "fa", "fb", "qf", "sf", "valid")
    cols_of = {n: i for i, n in enumerate(names)}
    tbl = np.array(table, np.int32).T
    c, per = cols_of, rp // g
    return _assemble(
        "unpack_g_in", tbl, (blocks, per), lambda: pl.program_id(0) * per + pl.program_id(1),
        [pl.BlockSpec((pl.Element(w1), pl.Element(k)), lambda b, u, t: (pl.multiple_of(t[c["st1"], b * per + u], ROW_TILE), 0)),
         pl.BlockSpec((GROUP, k), lambda b, u, t: (t[c["j2"], b * per + u], 0)),
         pl.BlockSpec((F_ROWS, k), lambda b, u, t: (0, 0))],
        pl.BlockSpec((None, g, k), lambda b, u, t: (b, u, 0)),
        jax.ShapeDtypeStruct((blocks, rp, k), g_all.dtype), [g_all, g_all, g_f], g, w1, cols_of)


def _unblock(w8):
    return w8.transpose(1, 0, 2).reshape(w8.shape[1], -1)


def _tile2(r, cols, tr, tcols):
    if r % 8 == 0:
        return _tile(r, tr), cols
    return r, _tile(cols, tcols)


def _pair_sum(name, g8, got, c):
    def body(c_ref, g_ref, s_ref, o_ref):
        o_ref[...] = (g_ref[...].astype(F32) + s_ref[...].astype(F32)).astype(o_ref.dtype)

    if g8.ndim == 4:
        _, r, k1, k2 = g8.shape
        tr = max(cand for cand in range(1, 385) if r % cand == 0)
        grid = (N_CHIPS, r // tr)
        shape = (None, tr, k1, k2)
        own = pl.BlockSpec(shape, lambda q, i, c_ref: (2 * q + c_ref[0], i, 0, 0))
        blk = pl.BlockSpec(shape, lambda q, i, c_ref: (q, i, 0, 0))
    else:
        _, r, cols = g8.shape
        tr, tcols = _tile2(r, cols, 256, 256)
        grid = (N_CHIPS, r // tr, cols // tcols)
        own = pl.BlockSpec((None, tr, tcols), lambda q, i, j, c_ref: (2 * q + c_ref[0], i, j))
        blk = pl.BlockSpec((None, tr, tcols), lambda q, i, j, c_ref: (q, i, j))
    return pl.pallas_call(
        body,
        name=name,
        grid_spec=pltpu.PrefetchScalarGridSpec(num_scalar_prefetch=1, grid=grid, in_specs=[own, blk], out_specs=blk),
        out_shape=jax.ShapeDtypeStruct((N_CHIPS,) + g8.shape[1:], BF16),
        compiler_params=_params(("parallel",) * len(grid)),
    )(c, g8, got)


def _local_step(x, mem, target, w, small, comm=None):
    t, d = x.shape
    cw = d // 2
    heads = cw // FOX_HEAD_DIM
    tc = min(512, d)
    tq = min(512, t)
    off_conv, off_fox, off_mq = 3 * d, 3 * d + 3 * cw, 3 * d + 6 * cw
    w, small = dict(w), dict(small)
    big = dict(tm=1024, tn=512, tk=2048)
    wide_k = dict(tm=512, tn=1024, tk=4096)
    tall = dict(tm=2048, tn=512, tk=2048)

    if comm:
        first = _gather_rider([comm["shards"]["w_in"], comm["conv_w"]], True)
        h, w["w_in"], cw8 = _rms_fwd("rms1_fwd", x, small["norm1_g"], rider=first)
        small["conv_w"] = _unblock(cw8)
        w_all, w_f = _pack_w_in(w["w_in"], d, tc)
        early = ("w_conv_out", "w_fox_out", "w_mem_out", "w_out", "w_mem_kv", "w_down")
        proj, *got = _matmul("proj", "nt", h, w_all, outs=[BF16], rider=_gather_rider([comm["shards"][n] for n in early], True), **tall)
        for n, val in zip(early, got):
            w[n] = _unblock(val) if n in COLUMN_SPLIT else val.reshape(-1, val.shape[-1])
    else:
        h = _rms_fwd("rms1_fwd", x, small["norm1_g"])
        w_all, w_f = _pack_w_in(w["w_in"], d, tc)
        proj = _matmul("proj", "nt", h, w_all, outs=[BF16], **tall)
    z_row = _matmul("proj_f", "nt", w_f, h, outs=[F32], tm=F_ROWS, tn=512, tk=2048)

    y_conv = _conv_fwd(proj, off_conv, small["conv_w"], LANES)

    b_col = jnp.pad(small["b_f"], (0, F_ROWS - heads)).reshape(F_ROWS, 1)
    c_row3 = _forget_fwd(z_row, b_col)[:heads].reshape(heads, 1, t)
    c_colb = _rows_to_colb(c_row3, tq)
    if comm:
        y_fox, lse, got = _fox_fwd(proj, off_fox, small["fox_q_g"], small["fox_k_g"], c_row3, c_colb, heads, 2 * tq,
                                   rider=_gather_rider([comm["shards"]["w_up"]], False))
        w["w_up"] = _unblock(got)
    else:
        y_fox, lse = _fox_fwd(proj, off_fox, small["fox_q_g"], small["fox_k_g"], c_row3, c_colb, heads, 2 * tq)

    nm = _rms_fwd("mem_rms_fwd", mem, small["mem_norm_g"])
    kv = _matmul("mem_kv", "nn", nm, w["w_mem_kv"], outs=[F32], tm=256, tn=512, tk=2048)
    y_mem = _mem_fwd(proj, off_mq, kv, small["mem_q_g"], small["mem_k_g"], tq)

    ys = (y_conv, y_fox, y_mem)
    w_outs = (w["w_conv_out"], w["w_fox_out"], w["w_mem_out"])
    *o3, merged = _merge_fwd(proj, ys, w_outs, 1024, tc)
    def out_epilogue(acc, xr, g2):
        x1r = acc + xr
        r = lax.rsqrt(jnp.mean(x1r * x1r, axis=-1, keepdims=True) + EPS)
        return x1r, x1r * r * g2

    x1, h2 = _matmul("out_proj", "nn", merged, w["w_out"], outs=[F32, BF16], extras=[x, small["norm2_g"].reshape(1, d)],
                     epilogue=out_epilogue, tm=512, tn=d, tk=2048)

    def up_epilogue(acc):
        return acc, jnp.square(jnp.maximum(acc, 0.0))

    up, act = _matmul("mlp_up", "nn", h2, w["w_up"], outs=[BF16, BF16], epilogue=up_epilogue, **big)

    def loss_epilogue(acc, x1r, tr):
        dy = (acc + x1r - tr) * (1.0 / d)
        return dy, dy

    dy, dyb = _matmul("mlp_down", "nn", act, w["w_down"], outs=[F32, BF16], extras=[x1, target],
                      epilogue=loss_epilogue, tm=1024, tn=512, tk=4096)

    def dup_epilogue(acc, upr):
        return (acc * 2.0 * jnp.maximum(upr.astype(F32), 0.0),)

    def by_owner(g):
        return g.reshape(N_DEV, -1, g.shape[-1])

    g, parts = {}, {}
    g["w_down"] = _matmul("d_w_down", "tn", act, dyb, outs=[BF16], **wide_k)
    if comm:
        dup = _matmul("d_act", "nt", dyb, w["w_down"], outs=[BF16], extras=[up], epilogue=dup_epilogue, **tall)
        g["w_up"], got = _matmul("d_w_up", "tn", h2, dup, outs=[BF16], out_blocks=True,
                                 rider=_pair_rider([by_owner(g["w_down"])]), **wide_k)
        pair = _pair_sum("pair_w_down", by_owner(g["w_down"]), got, comm["c"])
        dh2, parts["w_down"], got = _matmul("d_h2", "nt", dup, w["w_up"], outs=[F32],
                                            rider=_join_riders(_chip_rider([pair]), _pair_rider([g["w_up"]])), **tall)
        pair_up = _pair_sum("pair_w_up", g["w_up"], got, comm["c"])
    else:
        dup = _matmul("d_act", "nt", dyb, w["w_down"], outs=[BF16], extras=[up], epilogue=dup_epilogue, **tall)
        g["w_up"] = _matmul("d_w_up", "tn", h2, dup, outs=[BF16], out_blocks=True, **wide_k)
        dh2 = _matmul("d_h2", "nt", dup, w["w_up"], outs=[F32], **tall)
    dx1, dx1b, g_norm2, dy_sq = _rms_bwd("rms2_bwd", dh2, x1, small["norm2_g"], res=dy)
    loss = dy_sq * (0.5 * d)

    g["w_out"] = _matmul("d_w_out", "tn", merged, dx1b, outs=[BF16], **wide_k)
    dproj, *do3 = _merge_bwd(proj, o3, dx1b, w["w_out"], 1024, tc)
    names = ("w_conv_out", "w_fox_out", "w_mem_out")
    dys = []
    for s in range(3):
        g[names[s]] = _matmul(f"d_w_branch{s}", "tn", ys[s], do3[s], outs=[BF16], out_blocks=True, **wide_k)
        dys.append(_matmul(f"d_branch{s}", "nt", do3[s], w_outs[s], outs=[BF16], **tall))

    dproj, dkv, g_mq, g_mk = _mem_bwd(proj, off_mq, kv, dys[2], small["mem_q_g"], small["mem_k_g"], tq, dproj)
    g["w_mem_kv"] = _matmul("d_w_mem_kv", "tn", nm, dkv, outs=[BF16], **wide_k)
    dnm = _matmul("d_mem_norm", "nt", dkv, w["w_mem_kv"], outs=[F32], tm=256, tn=512, tk=2048)
    _, _, g_mem_norm, _ = _rms_bwd("mem_rms_bwd", dnm, mem, small["mem_norm_g"])

    mid = ("w_out", "w_conv_out", "w_fox_out", "w_mem_out", "w_mem_kv")
    if comm:
        mid8 = [g[n] if n in names else by_owner(g[n]) for n in mid]
        dproj, g_conv_w, *got = _conv_bwd(proj, off_conv, small["conv_w"], dys[0], LANES, dproj, rider=_pair_rider(mid8))
        pairs_mid = [_pair_sum("pair_" + n, g8, s4, comm["c"]) for n, g8, s4 in zip(mid, mid8, got)]
        dproj, dc, g_fq, g_fk, parts["w_up"] = _fox_bwd(proj, off_fox, y_fox, dys[1], small["fox_q_g"], small["fox_k_g"], c_row3,
                                                        c_colb, lse, heads, tq, dproj, rider=_chip_rider([pair_up]))
    else:
        dproj, g_conv_w = _conv_bwd(proj, off_conv, small["conv_w"], dys[0], LANES, dproj)
        dproj, dc, g_fq, g_fk = _fox_bwd(proj, off_fox, y_fox, dys[1], small["fox_q_g"], small["fox_k_g"], c_row3, c_colb,
                                         lse, heads, tq, dproj)
    dc_row = jnp.pad(dc.reshape(heads, t), ((0, F_ROWS - heads), (0, 0)))
    dz_row, db = _forget_bwd(z_row, b_col, dc_row)

    if comm:
        g_all, *got = _matmul("d_w_in", "tn", dproj, h, outs=[BF16], j_outer=True, rider=_chip_rider(pairs_mid), **wide_k)
        parts.update(zip(mid, got))
    else:
        g_all = _matmul("d_w_in", "tn", dproj, h, outs=[BF16], j_outer=True, **wide_k)
    g_wf = _matmul("d_w_f", "nn", dz_row, h, outs=[BF16], tm=F_ROWS, tn=512, tk=4096)
    g["w_in"] = _unpack_g_in(g_all, g_wf, d, tc, w["w_in"].shape[0])
    dh = _matmul("d_h_f", "tn", dz_row, w_f, outs=[F32], tm=1024, tn=512, tk=F_ROWS)
    add_prev = lambda acc, prev: (acc + prev,)
    if comm:
        g_in8 = g["w_in"]
        got = _run_rider("pair_exchange_w_in", _pair_rider([g_in8]))[0]
        pair = _pair_sum("pair_w_in", g_in8, got, comm["c"])
        dh, parts["w_in"] = _matmul("d_h", "nn", dproj, w_all, outs=[F32], extras=[dh], epilogue=add_prev,
                                    rider=_chip_rider([pair]), tm=1024, tn=512, tk=3328)
    else:
        dh = _matmul("d_h", "nn", dproj, w_all, outs=[F32], extras=[dh], epilogue=add_prev, tm=1024, tn=512, tk=3328)
    grad_x, _, g_norm1, _ = _rms_bwd("rms1_bwd", dh, x, small["norm1_g"], res=dx1)

    gs = dict(norm1_g=g_norm1, b_f=db[:heads, 0], conv_w=g_conv_w, fox_q_g=g_fq.reshape(-1), fox_k_g=g_fk.reshape(-1),
              mem_norm_g=g_mem_norm, mem_q_g=g_mq, mem_k_g=g_mk, norm2_g=g_norm2)
    return loss, grad_x, (parts if comm else g), gs


def _adamw_math(w, g, m, v):
    m = ADAM_B1 * m + (1.0 - ADAM_B1) * g
    v = ADAM_B2 * v + (1.0 - ADAM_B2) * jnp.square(g)
    m_hat = m / (1.0 - ADAM_B1 ** ADAM_STEP)
    v_hat = v / (1.0 - ADAM_B2 ** ADAM_STEP)
    delta = -ADAM_LR * (m_hat / (jnp.sqrt(v_hat) + ADAM_EPS) + ADAM_WD * w)
    return delta, m, v


def _adamw(name, parts, w, m, v):
    r, c = w.shape
    n_parts, rp = parts.shape[:2]
    if rp == r:
        tr, tc = _tile2(r, c, 128, 256)
    else:
        tr, tc = _tile(rp, 256), _tile(c, 1024)

    def body(p_ref, w_ref, m_ref, v_ref, g_ref, d_ref, nm_ref, nv_ref):
        g = p_ref[0].astype(F32)
        for s in range(1, n_parts):
            g = g + p_ref[s].astype(F32)
        delta, nm, nv = _adamw_math(w_ref[...], g, m_ref[...], v_ref[...])
        g_ref[...] = g
        d_ref[...] = delta
        nm_ref[...] = nm
        nv_ref[...] = nv

    blk = pl.BlockSpec((tr, tc), lambda i, j: (i, j))
    return pl.pallas_call(
        body,
        name=name,
        grid=(rp // tr, c // tc),
        in_specs=[pl.BlockSpec((n_parts, tr, tc), lambda i, j: (0, i, j)), blk, blk, blk],
        out_specs=[blk] * 4,
        out_shape=[jax.ShapeDtypeStruct((r, c), F32)] * 4,
        compiler_params=_params(("parallel", "parallel")),
    )(parts, w, m, v)


def _sum_parts(name, parts):
    n_parts, r, c = parts.shape

    def body(p_ref, o_ref):
        acc = p_ref[0]
        for s in range(1, n_parts):
            acc = acc + p_ref[s]
        o_ref[...] = acc

    return pl.pallas_call(body, name=name, out_shape=jax.ShapeDtypeStruct((r, c), F32))(parts)


BIG = ("w_in", "w_mem_kv", "w_conv_out", "w_fox_out", "w_mem_out", "w_out", "w_up", "w_down")
COLUMN_SPLIT = ("w_in", "w_conv_out", "w_fox_out", "w_mem_out", "w_up")
SMALL = ("norm1_g", "b_f", "conv_w", "fox_q_g", "fox_k_g", "mem_norm_g", "mem_q_g", "mem_k_g", "norm2_g")
WEIGHTS = ("norm1_g", "w_in", "b_f", "conv_w", "fox_q_g", "fox_k_g", "mem_norm_g", "w_mem_kv", "mem_q_g", "mem_k_g",
           "w_conv_out", "w_fox_out", "w_mem_out", "w_out", "norm2_g", "w_up", "w_down")


def _pack(vectors):
    rows = []
    for vec in vectors:
        n = vec.shape[0]
        rows.append(jnp.pad(vec, (0, -n % LANES)).reshape(-1, LANES))
    out = jnp.concatenate(rows, axis=0)
    return jnp.pad(out, ((0, -out.shape[0] % 8), (0, 0)))


def _unpack(packed, sizes):
    out, row = [], 0
    for n in sizes:
        nr = -(-n // LANES)
        out.append(packed[row:row + nr].reshape(-1)[:n])
        row += nr
    return out


def kernel(x, mem, norm1_g, w_in, b_f, conv_w, fox_q_g, fox_k_g, mem_norm_g, w_mem_kv, mem_q_g, mem_k_g, w_conv_out, w_fox_out, w_mem_out, w_out, norm2_g, w_up, w_down, loss_target, m_norm1_g, m_w_in, m_b_f, m_conv_w, m_fox_q_g, m_fox_k_g, m_mem_norm_g, m_w_mem_kv, m_mem_q_g, m_mem_k_g, m_w_conv_out, m_w_fox_out, m_w_mem_out, m_w_out, m_norm2_g, m_w_up, m_w_down, v_norm1_g, v_w_in, v_b_f, v_conv_w, v_fox_q_g, v_fox_k_g, v_mem_norm_g, v_w_mem_kv, v_mem_q_g, v_mem_k_g, v_w_conv_out, v_w_fox_out, v_w_mem_out, v_w_out, v_norm2_g, v_w_up, v_w_down):
    args = dict(locals())
    wts = {n: args[n] for n in WEIGHTS}
    ms = {n: args["m_" + n] for n in WEIGHTS}
    vs = {n: args["v_" + n] for n in WEIGHTS}
    x_pos, y_pos, c_pos = _position()
    me = _index(x_pos, y_pos, c_pos)

    shards = {n: wts[n].astype(BF16) for n in BIG if n != "w_in"}
    rows_in = w_in.shape[1]
    shards["w_in"] = jnp.pad(w_in.T.astype(BF16), ((0, _padded_rows(rows_in) - rows_in), (0, 0)))
    small = {n: wts[n] for n in SMALL if n != "conv_w"}
    comm = {"shards": shards, "conv_w": conv_w, "c": c_pos.astype(jnp.int32).reshape(1)}

    loss, grad_x, parts, gs = _local_step(x[0], mem[0], loss_target[0], {}, small, comm)

    out_g, out_d, out_m, out_v = {}, {}, {}, {}
    for n in BIG:
        if n == "w_in":
            res = _adamw("adamw_" + n, parts[n], wts[n].T, ms[n].T, vs[n].T)
            out_g[n], out_d[n], out_m[n], out_v[n] = (r.T for r in res)
        else:
            out_g[n], out_d[n], out_m[n], out_v[n] = _adamw("adamw_" + n, parts[n], wts[n], ms[n], vs[n])

    small_sizes = [int(math.prod(gs[n].shape)) for n in SMALL]
    packed = _pack([gs[n].reshape(-1) for n in SMALL])
    gsum = _sum_parts("sum_small", _run_rider("exchange_small", _broadcast_rider([packed]))[0])
    gsmall = dict(zip(SMALL, _unpack(gsum, small_sizes)))
    cols = conv_w.shape[1]
    gsmall["conv_w"] = lax.dynamic_slice(gsmall["conv_w"].reshape(CONV_TAPS, -1), (0, me * cols), (CONV_TAPS, cols)).reshape(-1)
    pg, pw, pm, pv = (_pack([src[n].reshape(-1) for n in SMALL]) for src in (gsmall, wts, ms, vs))
    _, sd, sm, sv = _adamw("adamw_small", pg[None], pw, pm, pv)
    local_sizes = [int(math.prod(wts[n].shape)) for n in SMALL]
    for dst, src in ((out_d, sd), (out_m, sm), (out_v, sv)):
        for n, val in zip(SMALL, _unpack(src, local_sizes)):
            dst[n] = val.reshape(wts[n].shape)
    for n in SMALL:
        out_g[n] = gsmall[n].reshape(wts[n].shape)

    loss = lax.psum(loss, MESH_AXES)
    return (loss, grad_x[None], *[out_g[n] for n in WEIGHTS], *[out_d[n] for n in WEIGHTS],
            *[out_m[n] for n in WEIGHTS], *[out_v[n] for n in WEIGHTS])
```

```python
import math

import numpy as np
import jax
import jax.numpy as jnp
from jax import lax
from jax.experimental import pallas as pl
from jax.experimental.pallas import tpu as pltpu

F32 = jnp.float32
BF16 = jnp.bfloat16

EPS = 1e-6
N_DEV = 8
N_CHIPS = 4
FOX_HEAD_DIM = 128
MEM_HEADS = 4
CONV_TAPS = 3
N_BRANCHES = 3
F_ROWS = 16

ADAM_LR = 0.001
ADAM_B1 = 0.9
ADAM_B2 = 0.999
ADAM_EPS = 1e-08
ADAM_WD = 0.01
ADAM_STEP = 10

V7X_VMEM_BYTES = 64 * 1024 * 1024
VMEM_LIMIT = V7X_VMEM_BYTES * 3 // 4
LANES = 128
NEG = -1e30
MAX_KEYS = 1024

MESH_AXES = ("x", "y", "c")
MESH = pl.DeviceIdType.MESH
ANY = pl.BlockSpec(memory_space=pl.ANY)

NN = (((1,), (0,)), ((), ()))
NT = (((1,), (1,)), ((), ()))
TN = (((0,), (0,)), ((), ()))


def _params(sem):
    return pltpu.CompilerParams(dimension_semantics=sem, vmem_limit_bytes=VMEM_LIMIT)


def _dot(a, b, dn):
    return lax.dot_general(a, b, dn, preferred_element_type=F32)


def _tile(n, t):
    if n <= t:
        return n
    for step in (LANES, 16):
        for cand in range(t - t % step, 0, -step):
            if n % cand == 0:
                return cand
    raise ValueError((n, t))


class _Rider:
    def __init__(self, ins, out_shapes, sem_shapes, start, finish, middle=None):
        self.ins, self.out_shapes, self.sem_shapes = list(ins), list(out_shapes), list(sem_shapes)
        self.start, self.finish, self.middle = start, finish, middle


def _position():
    return lax.axis_index("x"), lax.axis_index("y"), lax.axis_index("c")


def _index(px, py, pc):
    return 4 * px + 2 * py + pc


def _dma_sems(n, per):
    return [pltpu.SemaphoreType.DMA((n, per)), pltpu.SemaphoreType.DMA((n, per)), pltpu.SemaphoreType.DMA((n,))]


def _gather_rider(shards, pass_on):
    n = len(shards)

    def copies(ins, outs, sems):
        send_sems, recv_sems, local_sems = sems
        x, y, c = _position()
        me, sibling = (x, y, c), (x, y, 1 - c)
        chips = [(1 - x, y), (x, 1 - y), (1 - x, 1 - y)]

        def copy(a, k, block, to, src=None, k_send=None):
            rows = outs[a].at[_index(*block)]
            return pltpu.make_async_remote_copy(
                src_ref=rows if src is None else src, dst_ref=rows,
                send_sem=send_sems.at[a, k if k_send is None else k_send], recv_sem=recv_sems.at[a, k],
                device_id=to, device_id_type=MESH)

        mine = [pltpu.make_async_copy(ins[a], outs[a].at[_index(*me)], local_sems.at[a]) for a in range(n)]
        first = []
        for a in range(n):
            first.append(copy(a, 0, me, sibling, src=ins[a]))
            first += [copy(a, 1 + j, me, (*chips[j], c), src=ins[a]) for j in range(2 if pass_on else 3)]
        return copy, mine, first, me, sibling, chips, c

    def start(ins, outs, sems):
        _, mine, first, *_ = copies(ins, outs, sems)
        for cp in mine + first:
            cp.start()

    def by_kind(c, fn):
        if pass_on:
            pl.when(c == 1)(lambda: fn(0, 1))
            pl.when(c == 0)(lambda: fn(1, 0))
        else:
            fn(0, 1)

    def onward(copy, a, j_on, j_to, chips, c, sibling):
        third = [copy(a, 3, (*chips[j_on], c), (*chips[j_to], c), k_send=7)] if pass_on else []
        return third + [copy(a, 4 + j_on, (*chips[j_on], c), sibling)], [copy(a, 4 + j_to, (*chips[j_to], c), sibling)]

    def middle(ins, outs, sems):
        copy, _, _, me, sibling, chips, c = copies(ins, outs, sems)

        def fn(j_on, j_to):
            for a in range(n):
                for j, after in zip((j_on, j_to), onward(copy, a, j_on, j_to, chips, c, sibling)):
                    copy(a, 1 + j, (*chips[j], c), me).wait_recv()
                    for cp in after:
                        cp.start()

        by_kind(c, fn)

    def finish(ins, outs, sems):
        copy, mine, first, me, sibling, chips, c = copies(ins, outs, sems)

        def fn(j_on, j_to):
            passed = [cp for a in range(n) for after in onward(copy, a, j_on, j_to, chips, c, sibling) for cp in after]
            for a in range(n):
                copy(a, 3, (*chips[2], c), me).wait_recv()
                passed.append(copy(a, 6, (*chips[2], c), sibling))
                passed[-1].start()
            for a in range(n):
                copy(a, 0, sibling, me).wait_recv()
                for j, chip in enumerate(chips):
                    copy(a, 4 + j, (*chip, 1 - c), me).wait_recv()
            for cp in first + passed:
                cp.wait_send()
            for cp in mine:
                cp.wait()

        by_kind(c, fn)

    out_shapes = [jax.ShapeDtypeStruct((N_DEV,) + s.shape, s.dtype) for s in shards]
    return _Rider(shards, out_shapes, _dma_sems(n, 8), start, finish, middle)


def _pair_rider(grads):
    n = len(grads)

    def copies(ins, outs, sems):
        send_sems, recv_sems, _ = sems
        x, y, c = _position()
        return [pltpu.make_async_remote_copy(
            src_ref=ins[a].at[2 * q + (1 - c)], dst_ref=outs[a].at[q],
            send_sem=send_sems.at[a, q], recv_sem=recv_sems.at[a, q], device_id=(x, y, 1 - c), device_id_type=MESH)
            for a in range(n) for q in range(N_CHIPS)]

    def start(ins, outs, sems):
        for cp in copies(ins, outs, sems):
            cp.start()

    def finish(ins, outs, sems):
        cps = copies(ins, outs, sems)
        for cp in cps:
            cp.wait_recv()
        for cp in cps:
            cp.wait_send()

    out_shapes = [jax.ShapeDtypeStruct((N_CHIPS,) + g.shape[1:], g.dtype) for g in grads]
    return _Rider(grads, out_shapes, _dma_sems(n, N_CHIPS), start, finish)


def _chip_rider(parts):
    n = len(parts)

    def copies(ins, outs, sems):
        send_sems, recv_sems, local_sems = sems
        x, y, c = _position()
        q_me = 2 * x + y
        chips = [(1 - x, y), (x, 1 - y), (1 - x, 1 - y)]
        mine = [pltpu.make_async_copy(ins[a].at[q_me], outs[a].at[q_me], local_sems.at[a]) for a in range(n)]
        sends, arrivals = [], []
        for a in range(n):
            for j, (tx, ty) in enumerate(chips):
                q_t = 2 * tx + ty
                sends.append(pltpu.make_async_remote_copy(
                    src_ref=ins[a].at[q_t], dst_ref=outs[a].at[q_me],
                    send_sem=send_sems.at[a, j], recv_sem=recv_sems.at[a, j], device_id=(tx, ty, c), device_id_type=MESH))
                arrivals.append(pltpu.make_async_remote_copy(
                    src_ref=ins[a].at[q_t], dst_ref=outs[a].at[q_t],
                    send_sem=send_sems.at[a, j], recv_sem=recv_sems.at[a, j], device_id=(tx, ty, c), device_id_type=MESH))
        return mine, sends, arrivals

    def start(ins, outs, sems):
        mine, sends, _ = copies(ins, outs, sems)
        for cp in mine + sends:
            cp.start()

    def finish(ins, outs, sems):
        mine, sends, arrivals = copies(ins, outs, sems)
        for cp in arrivals:
            cp.wait_recv()
        for cp in sends:
            cp.wait_send()
        for cp in mine:
            cp.wait()

    out_shapes = [jax.ShapeDtypeStruct(p.shape, p.dtype) for p in parts]
    return _Rider(parts, out_shapes, _dma_sems(n, 3), start, finish)


def _broadcast_rider(values):
    n = len(values)

    def copies(ins, outs, sems):
        send_sems, recv_sems, local_sems = sems
        x, y, c = _position()
        me = _index(x, y, c)

        def peer(k):
            return (1 - x if k & 4 else x, 1 - y if k & 2 else y, 1 - c if k & 1 else c)

        mine = [pltpu.make_async_copy(ins[a], outs[a].at[me], local_sems.at[a]) for a in range(n)]
        sends, arrivals = [], []
        for a in range(n):
            for k in range(1, N_DEV):
                common = dict(send_sem=send_sems.at[a, k - 1], recv_sem=recv_sems.at[a, k - 1], device_id=peer(k), device_id_type=MESH)
                sends.append(pltpu.make_async_remote_copy(src_ref=ins[a], dst_ref=outs[a].at[me], **common))
                arrivals.append(pltpu.make_async_remote_copy(src_ref=ins[a], dst_ref=outs[a].at[_index(*peer(k))], **common))
        return mine, sends, arrivals

    def start(ins, outs, sems):
        mine, sends, _ = copies(ins, outs, sems)
        for cp in mine + sends:
            cp.start()

    def finish(ins, outs, sems):
        mine, sends, arrivals = copies(ins, outs, sems)
        for cp in arrivals:
            cp.wait_recv()
        for cp in sends:
            cp.wait_send()
        for cp in mine:
            cp.wait()

    out_shapes = [jax.ShapeDtypeStruct((N_DEV,) + v.shape, v.dtype) for v in values]
    return _Rider(values, out_shapes, _dma_sems(n, 7), start, finish)


def _join_riders(*riders):
    def each(fn_name, ins, outs, sems):
        i = o = s = 0
        for r in riders:
            n_i, n_o, n_s = len(r.ins), len(r.out_shapes), len(r.sem_shapes)
            if getattr(r, fn_name) is not None:
                getattr(r, fn_name)(ins[i:i + n_i], outs[o:o + n_o], sems[s:s + n_s])
            i, o, s = i + n_i, o + n_o, s + n_s

    middle = (lambda ins, outs, sems: each("middle", ins, outs, sems)) if any(r.middle for r in riders) else None
    return _Rider([a for r in riders for a in r.ins], [a for r in riders for a in r.out_shapes],
                  [a for r in riders for a in r.sem_shapes],
                  lambda ins, outs, sems: each("start", ins, outs, sems),
                  lambda ins, outs, sems: each("finish", ins, outs, sems), middle)


def _run_rider(name, rider):
    n_in, n_out = len(rider.ins), len(rider.out_shapes)

    def body(*refs):
        ins, outs, sems = refs[:n_in], refs[n_in:n_in + n_out], refs[n_in + n_out:]
        rider.start(ins, outs, sems)
        if rider.middle is not None:
            rider.middle(ins, outs, sems)
        rider.finish(ins, outs, sems)

    return pl.pallas_call(
        body, name=name, in_specs=[ANY] * n_in, out_specs=[ANY] * n_out, out_shape=rider.out_shapes,
        scratch_shapes=rider.sem_shapes)(*rider.ins)


class _Host:
    def __init__(self, rider):
        self.rider = rider
        self.n_in = len(rider.ins) if rider else 0
        self.n_out = len(rider.out_shapes) if rider else 0
        self.n_sem = len(rider.sem_shapes) if rider else 0
        self.ins = rider.ins if rider else []
        self.in_specs = [ANY] * self.n_in
        self.out_specs = [ANY] * self.n_out
        self.out_shapes = rider.out_shapes if rider else []
        self.scratch = rider.sem_shapes if rider else []

    def run(self, first, last, ins, outs, sems, compute, midway=None):
        if self.rider is None:
            compute()
            return

        @pl.when(first)
        def _():
            self.rider.start(ins, outs, sems)

        compute()
        if self.rider.middle is not None and midway is not None:
            pl.when(midway)(lambda: self.rider.middle(ins, outs, sems))

        @pl.when(last)
        def _():
            if self.rider.middle is not None and midway is None:
                self.rider.middle(ins, outs, sems)
            self.rider.finish(ins, outs, sems)


def _matmul(name, kind, a, b, *, tm, tn, tk, outs, epilogue=None, extras=(), out_blocks=False, rider=None, j_outer=False):
    if kind == "nn":
        (m, kdim), n = a.shape, b.shape[1]
    elif kind == "nt":
        (m, kdim), n = a.shape, b.shape[0]
    else:
        (kdim, m), n = a.shape, b.shape[1]
    if out_blocks:
        tn = min(tn, n // N_DEV)
    tm, tn, tk = _tile(m, tm), _tile(n, tn), _tile(kdim, tk)
    ni, nj, nk = m // tm, n // tn, kdim // tk

    def spec(shape, fn):
        return pl.BlockSpec(shape, (lambda g0, g1, k: fn(g1, g0, k)) if j_outer else fn)

    a_spec = spec((tk, tm), lambda i, j, k: (k, i)) if kind == "tn" else spec((tm, tk), lambda i, j, k: (i, k))
    b_spec = spec((tn, tk), lambda i, j, k: (j, k)) if kind == "nt" else spec((tk, tn), lambda i, j, k: (k, j))
    dn = {"nn": NN, "nt": NT, "tn": TN}[kind]

    tile_spec = spec((tm, tn), lambda i, j, k: (i, j))
    row_spec = spec((1, tn), lambda i, j, k: (0, j))
    if out_blocks:
        width = n // N_DEV
        r_out = width // tn
        out_shape = [jax.ShapeDtypeStruct((N_DEV, m, width), dt) for dt in outs]
        out_specs = [spec((None, tm, tn), lambda i, j, k: (j // r_out, i, j % r_out)) for _ in outs]
    else:
        out_shape = [jax.ShapeDtypeStruct((m, n), dt) for dt in outs]
        out_specs = [tile_spec for _ in outs]
    n_ex, n_out = len(extras), len(outs)
    host = _Host(rider)
    n_acc = 1 if nk > 1 else 0

    def body(*refs):
        a_ref, b_ref = refs[0], refs[1]
        pos = 2
        ex_refs = refs[pos:pos + n_ex]; pos += n_ex
        r_ins = refs[pos:pos + host.n_in]; pos += host.n_in
        out_refs = refs[pos:pos + n_out]; pos += n_out
        r_outs = refs[pos:pos + host.n_out]; pos += host.n_out
        acc_ref = refs[pos] if n_acc else None
        sems = refs[pos + n_acc:]
        i, j, k = pl.program_id(1 if j_outer else 0), pl.program_id(0 if j_outer else 1), pl.program_id(2)

        def finish_tile(acc):
            vals = (acc,) if epilogue is None else epilogue(acc, *[e[...] for e in ex_refs])
            for o_ref, v in zip(out_refs, vals):
                o_ref[...] = v.astype(o_ref.dtype)

        def compute():
            part = _dot(a_ref[...], b_ref[...], dn)
            if nk == 1:
                finish_tile(part)
                return

            @pl.when(k == 0)
            def _():
                acc_ref[...] = part

            @pl.when(jnp.logical_and(k > 0, k < nk - 1))
            def _():
                acc_ref[...] += part

            @pl.when(k == nk - 1)
            def _():
                finish_tile(acc_ref[...] + part)

        first = jnp.logical_and(jnp.logical_and(i == 0, j == 0), k == 0)
        last = jnp.logical_and(jnp.logical_and(i == ni - 1, j == nj - 1), k == nk - 1)
        step = (pl.program_id(0) * (ni if j_outer else nj) + pl.program_id(1)) * nk + k
        host.run(first, last, r_ins, r_outs, sems, compute, midway=step == (ni * nj * nk * 3) // 5)

    sem = ("arbitrary",) * 3 if rider else ("parallel", "parallel", "arbitrary")
    res = pl.pallas_call(
        body,
        name=name,
        grid=(nj, ni, nk) if j_outer else (ni, nj, nk),
        in_specs=[a_spec, b_spec] + [row_spec if e.shape[0] == 1 else tile_spec for e in extras] + host.in_specs,
        out_specs=out_specs + host.out_specs,
        out_shape=out_shape + host.out_shapes,
        scratch_shapes=([pltpu.VMEM((tm, tn), F32)] if n_acc else []) + host.scratch,
        compiler_params=_params(sem),
    )(a, b, *extras, *host.ins)
    return res[0] if len(res) == 1 else res


def _rms_fwd(name, x, g, tm=512, rider=None):
    t, d = x.shape
    tm = _tile(t, tm)
    n = t // tm
    host = _Host(rider)

    def body(*refs):
        x_ref, g_ref = refs[:2]
        r_ins = refs[2:2 + host.n_in]
        h_ref = refs[2 + host.n_in]
        r_outs = refs[3 + host.n_in:3 + host.n_in + host.n_out]
        sems = refs[3 + host.n_in + host.n_out:]
        i = pl.program_id(0)

        def compute():
            xf = x_ref[...]
            r = lax.rsqrt(jnp.mean(xf * xf, axis=-1, keepdims=True) + EPS)
            h_ref[...] = (xf * r * g_ref[...]).astype(h_ref.dtype)

        host.run(i == 0, i == n - 1, r_ins, r_outs, sems, compute, midway=i == (n * 3) // 5)

    res = pl.pallas_call(
        body,
        name=name,
        grid=(n,),
        in_specs=[pl.BlockSpec((tm, d), lambda i: (i, 0)), pl.BlockSpec((1, d), lambda i: (0, 0))] + host.in_specs,
        out_specs=[pl.BlockSpec((tm, d), lambda i: (i, 0))] + host.out_specs,
        out_shape=[jax.ShapeDtypeStruct((t, d), BF16)] + host.out_shapes,
        scratch_shapes=host.scratch,
        compiler_params=_params(("arbitrary",) if rider else ("parallel",)),
    )(x, g.reshape(1, d), *host.ins)
    return res[0] if len(res) == 1 else res


def _rms_bwd(name, dh, x, g, res=None, tm=256):
    t, d = x.shape
    tm = _tile(t, tm)
    has_res = res is not None

    def body(*refs):
        if has_res:
            dh_ref, x_ref, g_ref, res_ref, dx_ref, dxb_ref, gg_ref, ss_ref = refs
        else:
            dh_ref, x_ref, g_ref, dx_ref, dxb_ref, gg_ref, ss_ref = refs
        i = pl.program_id(0)
        xf = x_ref[...]
        r = lax.rsqrt(jnp.mean(xf * xf, axis=-1, keepdims=True) + EPS)
        xh = xf * r
        dhf = dh_ref[...].astype(F32)
        dxh = dhf * g_ref[...]
        dx = r * (dxh - xh * jnp.mean(dxh * xh, axis=-1, keepdims=True))

        @pl.when(i == 0)
        def _():
            gg_ref[...] = jnp.zeros_like(gg_ref)
            ss_ref[...] = jnp.zeros_like(ss_ref)

        if has_res:
            resf = res_ref[...]
            dx = dx + resf
            ss_ref[...] += jnp.sum(jnp.sum(resf * resf, axis=0, keepdims=True), axis=1, keepdims=True)
        dx_ref[...] = dx
        dxb_ref[...] = dx.astype(BF16)
        gg_ref[...] += jnp.sum(dhf * xh, axis=0, keepdims=True)

    row = pl.BlockSpec((tm, d), lambda i: (i, 0))
    vec = pl.BlockSpec((1, d), lambda i: (0, 0))
    one = pl.BlockSpec((1, 1), lambda i: (0, 0))
    ins = [dh, x, g.reshape(1, d)] + ([res] if has_res else [])
    dx, dxb, gg, ss = pl.pallas_call(
        body,
        name=name,
        grid=(t // tm,),
        in_specs=[row, row, vec] + ([row] if has_res else []),
        out_specs=[row, row, vec, one],
        out_shape=[jax.ShapeDtypeStruct((t, d), F32), jax.ShapeDtypeStruct((t, d), BF16), jax.ShapeDtypeStruct((1, d), F32),
                   jax.ShapeDtypeStruct((1, 1), F32)],
        compiler_params=_params(("arbitrary",)),
    )(*ins)
    return dx, dxb, gg.reshape(d), ss[0, 0]


def _head_rms(xf):
    r = lax.rsqrt(jnp.mean(xf * xf, axis=-1, keepdims=True) + EPS)
    return xf * r, r


def _head_rms_bwd(dy, xn, r, g):
    dxh = dy * g
    dx = r * (dxh - xn * jnp.mean(dxh * xn, axis=-1, keepdims=True))
    return dx, jnp.sum(dy * xn, axis=0, keepdims=True)


def _col_to_row(col):
    n = col.shape[0]
    eye = lax.broadcasted_iota(jnp.int32, (n, n), 0) == lax.broadcasted_iota(jnp.int32, (n, n), 1)
    return jnp.sum(jnp.where(eye, col, 0.0), axis=0, keepdims=True)


def _row_to_col(row):
    n = row.shape[1]
    eye = lax.broadcasted_iota(jnp.int32, (n, n), 0) == lax.broadcasted_iota(jnp.int32, (n, n), 1)
    return jnp.sum(jnp.where(eye, row, 0.0), axis=1, keepdims=True)


def _dproj_args(dproj, n_in):
    if dproj is None:
        return [], [], {}
    return [dproj], [ANY], {n_in: 0}


def _shift_down(u, s, rows):
    return jnp.where(rows >= s, pltpu.roll(u, s, axis=0), 0.0)


def _shift_up(u, s, rows, t):
    return jnp.where(rows < t - s, pltpu.roll(u, t - s, axis=0), 0.0)


def _conv_fwd(proj, off, conv_w, cb):
    t = proj.shape[0]
    c = conv_w.shape[1]
    blk0 = off // (3 * cb)

    def body(p_ref, w_ref, y_ref):
        rows = lax.broadcasted_iota(jnp.int32, (t, cb), 0)
        bg = p_ref[:, 0:cb].astype(F32)
        u = p_ref[:, cb:2 * cb].astype(F32) * p_ref[:, 2 * cb:3 * cb].astype(F32)
        w = w_ref[...]
        conv = w[2:3] * u + w[1:2] * _shift_down(u, 1, rows) + w[0:1] * _shift_down(u, 2, rows)
        y_ref[...] = (bg * conv).astype(y_ref.dtype)

    return pl.pallas_call(
        body,
        name="conv_fwd",
        grid=(c // cb,),
        in_specs=[pl.BlockSpec((t, 3 * cb), lambda j: (0, blk0 + j)), pl.BlockSpec((CONV_TAPS, cb), lambda j: (0, j))],
        out_specs=pl.BlockSpec((t, cb), lambda j: (0, j)),
        out_shape=jax.ShapeDtypeStruct((t, c), BF16),
        compiler_params=_params(("parallel",)),
    )(proj, conv_w)


def _conv_bwd(proj, off, conv_w, dy, cb, dproj, rider=None):
    t = proj.shape[0]
    c = conv_w.shape[1]
    blk0 = off // (3 * cb)
    nj = c // cb
    host = _Host(rider)

    def body(*refs):
        p_ref, w_ref, dy_ref = refs[:3]
        r_ins = refs[4:4 + host.n_in]
        dp_ref, gw_ref = refs[4 + host.n_in:6 + host.n_in]
        r_outs = refs[6 + host.n_in:6 + host.n_in + host.n_out]
        sems = refs[6 + host.n_in + host.n_out:]
        j = pl.program_id(0)

        def compute():
            rows = lax.broadcasted_iota(jnp.int32, (t, cb), 0)
            bg = p_ref[:, 0:cb].astype(F32)
            cg = p_ref[:, cb:2 * cb].astype(F32)
            v = p_ref[:, 2 * cb:3 * cb].astype(F32)
            u = cg * v
            w = w_ref[...]
            u1 = _shift_down(u, 1, rows)
            u2 = _shift_down(u, 2, rows)
            conv = w[2:3] * u + w[1:2] * u1 + w[0:1] * u2
            dyf = dy_ref[...].astype(F32)
            dconv = dyf * bg
            du = w[2:3] * dconv + w[1:2] * _shift_up(dconv, 1, rows, t) + w[0:1] * _shift_up(dconv, 2, rows, t)
            dp_ref[:, 0:cb] = (dyf * conv).astype(dp_ref.dtype)
            dp_ref[:, cb:2 * cb] = (du * v).astype(dp_ref.dtype)
            dp_ref[:, 2 * cb:3 * cb] = (du * cg).astype(dp_ref.dtype)
            gw_ref[0:1, :] = jnp.sum(dconv * u2, axis=0, keepdims=True)
            gw_ref[1:2, :] = jnp.sum(dconv * u1, axis=0, keepdims=True)
            gw_ref[2:3, :] = jnp.sum(dconv * u, axis=0, keepdims=True)

        host.run(j == 0, j == nj - 1, r_ins, r_outs, sems, compute)

    res = pl.pallas_call(
        body,
        name="conv_bwd",
        grid=(nj,),
        in_specs=[
            pl.BlockSpec((t, 3 * cb), lambda j: (0, blk0 + j)),
            pl.BlockSpec((CONV_TAPS, cb), lambda j: (0, j)),
            pl.BlockSpec((t, cb), lambda j: (0, j)),
            ANY,
        ] + host.in_specs,
        out_specs=[pl.BlockSpec((t, 3 * cb), lambda j: (0, blk0 + j)), pl.BlockSpec((CONV_TAPS, cb), lambda j: (0, j))] + host.out_specs,
        out_shape=[jax.ShapeDtypeStruct(dproj.shape, dproj.dtype), jax.ShapeDtypeStruct((CONV_TAPS, c), F32)] + host.out_shapes,
        input_output_aliases={3: 0},
        scratch_shapes=host.scratch,
        compiler_params=_params(("arbitrary",)),
    )(proj, conv_w, dy, dproj, *host.ins)
    return res


def _lane_scan(x, reverse):
    lane = lax.broadcasted_iota(jnp.int32, x.shape, 1)
    s = 1
    while s < LANES:
        if reverse:
            x = x + jnp.where(lane < LANES - s, pltpu.roll(x, LANES - s, axis=1), 0.0)
        else:
            x = x + jnp.where(lane >= s, pltpu.roll(x, s, axis=1), 0.0)
        s *= 2
    return x


def _scan_rows(src_ref, dst_ref, t, reverse, fn=None):
    groups = list(range(t // LANES))
    if reverse:
        groups = groups[::-1]
    carry = None
    for gi in groups:
        sl = slice(gi * LANES, (gi + 1) * LANES)
        blk = src_ref[:, sl]
        if fn is not None:
            blk = fn(blk)
        blk = _lane_scan(blk, reverse)
        if carry is not None:
            blk = blk + carry
        dst_ref[:, sl] = blk
        carry = blk[:, 0:1] if reverse else blk[:, LANES - 1:LANES]


def _forget_fwd(z_row, b_col):
    rows, t = z_row.shape

    def body(z_ref, b_ref, c_ref):
        def logf(z):
            zz = z + b_ref[...]
            return jnp.minimum(zz, 0.0) - jnp.log(1.0 + jnp.exp(-jnp.abs(zz)))

        _scan_rows(z_ref, c_ref, t, False, logf)

    return pl.pallas_call(
        body,
        name="forget_fwd",
        out_shape=jax.ShapeDtypeStruct((rows, t), F32),
        compiler_params=pltpu.CompilerParams(vmem_limit_bytes=VMEM_LIMIT),
    )(z_row, b_col)


def _rows_to_colb(c_row3, tq):
    heads, _, t = c_row3.shape

    def body(r_ref, o_ref):
        o_ref[...] = jnp.broadcast_to(_row_to_col(r_ref[...]), (tq, LANES))

    return pl.pallas_call(
        body,
        name="rows_to_colb",
        grid=(heads, t // tq),
        in_specs=[pl.BlockSpec((None, 1, tq), lambda h, i: (h, 0, i))],
        out_specs=pl.BlockSpec((None, tq, LANES), lambda h, i: (h, i, 0)),
        out_shape=jax.ShapeDtypeStruct((heads, t, LANES), F32),
        compiler_params=_params(("parallel", "parallel")),
    )(c_row3)


def _forget_bwd(z_row, b_col, dc_row):
    rows, t = z_row.shape

    def body(z_ref, b_ref, dc_ref, dz_ref, db_ref, tmp_ref):
        _scan_rows(dc_ref, tmp_ref, t, True)
        zz = z_ref[...] + b_ref[...]
        dz = tmp_ref[...] * (1.0 / (1.0 + jnp.exp(zz)))
        dz_ref[...] = dz.astype(dz_ref.dtype)
        db_ref[...] = jnp.sum(dz, axis=1, keepdims=True)

    return pl.pallas_call(
        body,
        name="forget_bwd",
        out_shape=[jax.ShapeDtypeStruct((rows, t), BF16), jax.ShapeDtypeStruct((rows, 1), F32)],
        scratch_shapes=[pltpu.VMEM((rows, t), F32)],
        compiler_params=pltpu.CompilerParams(vmem_limit_bytes=VMEM_LIMIT),
    )(z_row, b_col, dc_row)


def _fox_fwd(proj, off, gq, gk, c_row3, c_colb, heads, tq, rider=None):
    t = proj.shape[0]
    hd = FOX_HEAD_DIM
    tq = _tile(t, tq)
    nq = t // tq
    blk0 = off // hd
    scale = 1.0 / math.sqrt(hd)
    host = _Host(rider)

    def body(*refs):
        q_ref, k_ref, v_ref, gq_ref, gk_ref, crow_ref, ccol_ref = refs[:7]
        r_ins = refs[7:7 + host.n_in]
        o_ref, lse_ref = refs[7 + host.n_in:9 + host.n_in]
        r_outs = refs[9 + host.n_in:9 + host.n_in + host.n_out]
        khat_ref, v_t_ref = refs[9 + host.n_in + host.n_out:11 + host.n_in + host.n_out]
        sems = refs[11 + host.n_in + host.n_out:]
        h, qi = pl.program_id(0), pl.program_id(1)

        def compute():
            eye = (lax.broadcasted_iota(jnp.int32, (hd, hd), 0) == lax.broadcasted_iota(jnp.int32, (hd, hd), 1)).astype(BF16)

            @pl.when(qi == 0)
            def _():
                kn, _ = _head_rms(k_ref[...].astype(F32))
                khat_ref[...] = (kn * gk_ref[...]).astype(BF16)
                v_t_ref[...] = _dot(eye, v_ref[...], NT).astype(BF16)

            qn, _ = _head_rms(q_ref[...].astype(F32))
            qhat = (qn * (gq_ref[...] * scale)).astype(BF16)
            crow = crow_ref[:, pl.ds(pl.multiple_of(qi * tq, tq), tq)]
            above = lax.broadcasted_iota(jnp.int32, (tq, tq), 1) >= lax.broadcasted_iota(jnp.int32, (tq, tq), 0)

            def tile(j, keys, carry, diagonal):
                m, l, acc_t = carry
                ks = pl.multiple_of(j * keys, keys)
                s_t = _dot(khat_ref[pl.ds(ks, keys), :], qhat, NT) - ccol_ref[pl.ds(ks, keys), 0:1]
                if diagonal:
                    s_t = jnp.where(above, s_t, NEG)
                m_new = jnp.maximum(m, jnp.max(s_t, axis=0, keepdims=True) + crow)
                alpha = jnp.exp(m - m_new)
                p_t = jnp.exp(s_t + (crow - m_new))
                l = alpha * l + jnp.sum(p_t, axis=0, keepdims=True)
                acc_t = alpha * acc_t + _dot(v_t_ref[:, pl.ds(ks, keys)], p_t.astype(BF16), NN)
                return m_new, l, acc_t

            init = (jnp.full((1, tq), NEG, F32), jnp.zeros((1, tq), F32), jnp.zeros((hd, tq), F32))
            pairs = qi // 2 if 2 * tq <= MAX_KEYS else 0
            carry = lax.fori_loop(0, pairs, lambda j, c: tile(j, 2 * tq, c, False), init)
            carry = lax.fori_loop(2 * pairs, qi, lambda j, c: tile(j, tq, c, False), carry)
            m, l, acc_t = tile(qi, tq, carry, True)
            o_ref[...] = _dot((acc_t / l).astype(BF16), eye, TN).astype(o_ref.dtype)
            lse_ref[...] = m + jnp.log(l)

        first = jnp.logical_and(h == 0, qi == 0)
        last = jnp.logical_and(h == heads - 1, qi == nq - 1)
        host.run(first, last, r_ins, r_outs, sems, compute)

    res = pl.pallas_call(
        body,
        name="fox_fwd",
        grid=(heads, nq),
        in_specs=[
            pl.BlockSpec((tq, hd), lambda h, i: (i, blk0 + 3 * h)),
            pl.BlockSpec((t, hd), lambda h, i: (0, blk0 + 3 * h + 1)),
            pl.BlockSpec((t, hd), lambda h, i: (0, blk0 + 3 * h + 2)),
            pl.BlockSpec((1, hd), lambda h, i: (0, 0)),
            pl.BlockSpec((1, hd), lambda h, i: (0, 0)),
            pl.BlockSpec((None, 1, t), lambda h, i: (h, 0, 0)),
            pl.BlockSpec((None, t, LANES), lambda h, i: (h, 0, 0)),
        ] + host.in_specs,
        out_specs=[pl.BlockSpec((tq, hd), lambda h, i: (i, h)), pl.BlockSpec((None, 1, tq), lambda h, i: (h, 0, i))] + host.out_specs,
        out_shape=[jax.ShapeDtypeStruct((t, heads * hd), BF16), jax.ShapeDtypeStruct((heads, 1, t), F32)] + host.out_shapes,
        scratch_shapes=[pltpu.VMEM((t, hd), BF16), pltpu.VMEM((hd, t), BF16)] + host.scratch,
        compiler_params=_params(("arbitrary", "arbitrary")),
    )(proj, proj, proj, gq.reshape(1, hd), gk.reshape(1, hd), c_row3, c_colb, *host.ins)
    return res


def _fox_bwd(proj, off, o, do, gq, gk, c_row3, c_colb, lse, heads, tq, dproj, rider=None):
    t = proj.shape[0]
    hd = FOX_HEAD_DIM
    tq = _tile(t, tq)
    nb = t // tq
    blk0 = off // hd
    scale = 1.0 / math.sqrt(hd)
    host = _Host(rider)
    n_fixed_in = 11

    def body(*refs):
        q_ref, k_ref, v_ref, o_ref, do_ref, gq_ref, gk_ref, crow_ref, ccol_ref, lse_ref = refs[:10]
        pos = n_fixed_in
        r_ins = refs[pos:pos + host.n_in]; pos += host.n_in
        dp_ref, dc_ref, ggq_ref, ggk_ref = refs[pos:pos + 4]; pos += 4
        r_outs = refs[pos:pos + host.n_out]; pos += host.n_out
        qhat_ref, khat_ref, khat_t_ref, dq_t_ref, dk_ref, dcq_ref, dck_ref, delta_ref = refs[pos:pos + 8]; pos += 8
        sems = refs[pos:]
        h = pl.program_id(0)

        def compute():
            qn, rq = _head_rms(q_ref[...].astype(F32))
            qhat_ref[...] = (qn * (gq_ref[...] * scale)).astype(BF16)
            kn, rk = _head_rms(k_ref[...].astype(F32))
            khat_ref[...] = (kn * gk_ref[...]).astype(BF16)
            eye = (lax.broadcasted_iota(jnp.int32, (hd, hd), 0) == lax.broadcasted_iota(jnp.int32, (hd, hd), 1)).astype(BF16)
            khat_t_ref[...] = _dot(eye, khat_ref[...], NT).astype(BF16)
            delta = jnp.sum(do_ref[...].astype(F32) * o_ref[...].astype(F32), axis=-1, keepdims=True)
            for b in range(nb):
                sl = slice(b * tq, (b + 1) * tq)
                delta_ref[:, sl] = _col_to_row(delta[sl, :])
            dq_t_ref[...] = jnp.zeros_like(dq_t_ref)
            dcq_ref[...] = jnp.zeros_like(dcq_ref)
            above = lax.broadcasted_iota(jnp.int32, (tq, tq), 1) >= lax.broadcasted_iota(jnp.int32, (tq, tq), 0)

            def kv_block(j, _):
                ks = pl.multiple_of(j * tq, tq)
                kh = khat_ref[pl.ds(ks, tq), :]
                kh_t = khat_t_ref[:, pl.ds(ks, tq)]
                vv = v_ref[pl.ds(ks, tq), :]
                ccol = ccol_ref[pl.ds(ks, tq), 0:1]

                def q_block(i, n, carry, diagonal):
                    dk, dv, dck = carry
                    qs = pl.multiple_of(i * tq, tq)
                    qh = qhat_ref[pl.ds(qs, n), :]
                    dob = do_ref[pl.ds(qs, n), :]
                    s_t = _dot(kh, qh, NT) + ((crow_ref[:, pl.ds(qs, n)] - lse_ref[:, pl.ds(qs, n)]) - ccol)
                    p_t = jnp.exp(s_t)
                    if diagonal:
                        p_t = jnp.where(above, p_t, 0.0)
                    ds_t = p_t * (_dot(vv, dob, NT) - delta_ref[:, pl.ds(qs, n)])
                    dsb = ds_t.astype(BF16)
                    dv = dv + _dot(p_t.astype(BF16), dob, NN)
                    dk = dk + _dot(dsb, qh, NN)
                    dq_t_ref[:, pl.ds(qs, n)] += _dot(kh_t, dsb, NN)
                    dcq_ref[:, pl.ds(qs, n)] += jnp.sum(ds_t, axis=0, keepdims=True)
                    dck = dck + jnp.sum(ds_t, axis=-1, keepdims=True)
                    return dk, dv, dck

                zero = jnp.zeros((tq, hd), F32)
                carry = q_block(j, tq, (zero, zero, jnp.zeros((tq, 1), F32)), True)
                pairs = (nb - 1 - j) // 2 if 2 * tq <= MAX_KEYS else 0
                carry = lax.fori_loop(0, pairs, lambda p, c: q_block(j + 1 + 2 * p, 2 * tq, c, False), carry)
                dk, dv, dck = lax.fori_loop(j + 1 + 2 * pairs, nb, lambda i, c: q_block(i, tq, c, False), carry)
                dk_ref[pl.ds(ks, tq), :] = dk
                dp_ref[pl.ds(ks, tq), 2 * hd:3 * hd] = dv.astype(dp_ref.dtype)
                dck_ref[pl.ds(ks, tq), :] = dck
                return 0

            lax.fori_loop(0, nb, kv_block, 0)

            dq, ggq = _head_rms_bwd(dq_t_ref[...].T * scale, qn, rq, gq_ref[...])
            dk, ggk = _head_rms_bwd(dk_ref[...], kn, rk, gk_ref[...])
            dp_ref[:, 0:hd] = dq.astype(dp_ref.dtype)
            dp_ref[:, hd:2 * hd] = dk.astype(dp_ref.dtype)
            for b in range(nb):
                sl = slice(b * tq, (b + 1) * tq)
                dc_ref[:, sl] = dcq_ref[:, sl] - _col_to_row(dck_ref[sl, :])

            @pl.when(h == 0)
            def _():
                ggq_ref[...] = jnp.zeros_like(ggq_ref)
                ggk_ref[...] = jnp.zeros_like(ggk_ref)

            ggq_ref[...] += ggq
            ggk_ref[...] += ggk

        host.run(h == 0, h == heads - 1, r_ins, r_outs, sems, compute)

    head_in = lambda part: pl.BlockSpec((t, hd), lambda h: (0, blk0 + 3 * h + part))
    vec = pl.BlockSpec((1, hd), lambda h: (0, 0))
    colb = pl.BlockSpec((None, t, LANES), lambda h: (h, 0, 0))
    res = pl.pallas_call(
        body,
        name="fox_bwd",
        grid=(heads,),
        in_specs=[
            head_in(0), head_in(1), head_in(2),
            pl.BlockSpec((t, hd), lambda h: (0, h)),
            pl.BlockSpec((t, hd), lambda h: (0, h)),
            vec, vec,
            pl.BlockSpec((None, 1, t), lambda h: (h, 0, 0)),
            colb,
            pl.BlockSpec((None, 1, t), lambda h: (h, 0, 0)),
            ANY,
        ] + host.in_specs,
        out_specs=[
            pl.BlockSpec((t, 3 * hd), lambda h: (0, blk0 // 3 + h)),
            pl.BlockSpec((None, 1, t), lambda h: (h, 0, 0)),
            vec, vec,
        ] + host.out_specs,
        out_shape=[
            jax.ShapeDtypeStruct(dproj.shape, dproj.dtype),
            jax.ShapeDtypeStruct((heads, 1, t), F32),
            jax.ShapeDtypeStruct((1, hd), F32),
            jax.ShapeDtypeStruct((1, hd), F32),
        ] + host.out_shapes,
        input_output_aliases={10: 0},
        scratch_shapes=[
            pltpu.VMEM((t, hd), BF16), pltpu.VMEM((t, hd), BF16), pltpu.VMEM((hd, t), BF16),
            pltpu.VMEM((hd, t), F32), pltpu.VMEM((t, hd), F32),
            pltpu.VMEM((1, t), F32), pltpu.VMEM((t, 1), F32), pltpu.VMEM((1, t), F32),
        ] + host.scratch,
        compiler_params=_params(("arbitrary",)),
    )(proj, proj, proj, o, do, gq.reshape(1, hd), gk.reshape(1, hd), c_row3, c_colb, lse, dproj, *host.ins)
    return res


def _mem_fwd(proj, off, kv, gq, gk, tq):
    t = proj.shape[0]
    m, width = kv.shape[0], kv.shape[1] // 2
    hd = width // MEM_HEADS
    tq = _tile(t, tq)
    blk0 = off // hd
    scale = 1.0 / math.sqrt(hd)

    def body(q_ref, k_ref, v_ref, gq_ref, gk_ref, o_ref):
        qn, _ = _head_rms(q_ref[...].astype(F32))
        kn, _ = _head_rms(k_ref[...])
        s = _dot((qn * gq_ref[...]).astype(BF16), (kn * gk_ref[...]).astype(BF16), NT) * scale
        p = jnp.exp(s - jnp.max(s, axis=-1, keepdims=True))
        p = p / jnp.sum(p, axis=-1, keepdims=True)
        o_ref[...] = _dot(p.astype(BF16), v_ref[...].astype(BF16), NN).astype(o_ref.dtype)

    vec = pl.BlockSpec((1, hd), lambda h, i: (0, 0))
    return pl.pallas_call(
        body,
        name="mem_fwd",
        grid=(MEM_HEADS, t // tq),
        in_specs=[
            pl.BlockSpec((tq, hd), lambda h, i: (i, blk0 + h)),
            pl.BlockSpec((m, hd), lambda h, i: (0, h)),
            pl.BlockSpec((m, hd), lambda h, i: (0, MEM_HEADS + h)),
            vec, vec,
        ],
        out_specs=pl.BlockSpec((tq, hd), lambda h, i: (i, h)),
        out_shape=jax.ShapeDtypeStruct((t, width), BF16),
        compiler_params=_params(("parallel", "parallel")),
    )(proj, kv, kv, gq.reshape(1, hd), gk.reshape(1, hd))


def _mem_bwd(proj, off, kv, do, gq, gk, tq, dproj, rider=None):
    t = proj.shape[0]
    m, width = kv.shape[0], kv.shape[1] // 2
    hd = width // MEM_HEADS
    tq = _tile(t, tq)
    nq = t // tq
    blk0 = off // hd
    scale = 1.0 / math.sqrt(hd)
    host = _Host(rider)

    def body(*refs):
        q_ref, k_ref, v_ref, do_ref, gq_ref, gk_ref = refs[:6]
        pos = 7
        r_ins = refs[pos:pos + host.n_in]; pos += host.n_in
        dq_ref, dk_ref, dv_ref, ggq_ref, ggk_ref = refs[pos:pos + 5]; pos += 5
        r_outs = refs[pos:pos + host.n_out]; pos += host.n_out
        dkh_ref, dvh_ref = refs[pos:pos + 2]; pos += 2
        sems = refs[pos:]
        h, i = pl.program_id(0), pl.program_id(1)

        def compute():
            qn, rq = _head_rms(q_ref[...].astype(F32))
            kn, rk = _head_rms(k_ref[...])
            qhat = (qn * gq_ref[...]).astype(BF16)
            khat = (kn * gk_ref[...]).astype(BF16)
            vb = v_ref[...].astype(BF16)
            dob = do_ref[...]
            s = _dot(qhat, khat, NT) * scale
            p = jnp.exp(s - jnp.max(s, axis=-1, keepdims=True))
            p = p / jnp.sum(p, axis=-1, keepdims=True)
            dp = _dot(dob, vb, NT)
            ds = p * (dp - jnp.sum(dp * p, axis=-1, keepdims=True))
            dsb = ds.astype(BF16)
            dq, ggq = _head_rms_bwd(_dot(dsb, khat, NN) * scale, qn, rq, gq_ref[...])
            dq_ref[...] = dq.astype(dq_ref.dtype)

            @pl.when(i == 0)
            def _():
                dkh_ref[...] = jnp.zeros_like(dkh_ref)
                dvh_ref[...] = jnp.zeros_like(dvh_ref)

            @pl.when(jnp.logical_and(h == 0, i == 0))
            def _():
                ggq_ref[...] = jnp.zeros_like(ggq_ref)
                ggk_ref[...] = jnp.zeros_like(ggk_ref)

            dkh_ref[...] += _dot(dsb, qhat, TN)
            dvh_ref[...] += _dot(p.astype(BF16), dob, TN)
            ggq_ref[...] += ggq

            @pl.when(i == nq - 1)
            def _():
                dk, ggk = _head_rms_bwd(dkh_ref[...] * scale, kn, rk, gk_ref[...])
                dk_ref[...] = dk.astype(dk_ref.dtype)
                dv_ref[...] = dvh_ref[...].astype(dv_ref.dtype)
                ggk_ref[...] += ggk

        first = jnp.logical_and(h == 0, i == 0)
        last = jnp.logical_and(h == MEM_HEADS - 1, i == nq - 1)
        host.run(first, last, r_ins, r_outs, sems, compute)

    vec = pl.BlockSpec((1, hd), lambda h, i: (0, 0))
    kblk = pl.BlockSpec((m, hd), lambda h, i: (0, h))
    res = pl.pallas_call(
        body,
        name="mem_bwd",
        grid=(MEM_HEADS, nq),
        in_specs=[
            pl.BlockSpec((tq, hd), lambda h, i: (i, blk0 + h)), kblk,
            pl.BlockSpec((m, hd), lambda h, i: (0, MEM_HEADS + h)),
            pl.BlockSpec((tq, hd), lambda h, i: (i, h)), vec, vec, ANY,
        ] + host.in_specs,
        out_specs=[pl.BlockSpec((tq, hd), lambda h, i: (i, blk0 + h)), kblk, kblk, vec, vec] + host.out_specs,
        out_shape=[
            jax.ShapeDtypeStruct(dproj.shape, dproj.dtype),
            jax.ShapeDtypeStruct((m, width), BF16),
            jax.ShapeDtypeStruct((m, width), BF16),
            jax.ShapeDtypeStruct((1, hd), F32),
            jax.ShapeDtypeStruct((1, hd), F32),
        ] + host.out_shapes,
        input_output_aliases={6: 0},
        scratch_shapes=[pltpu.VMEM((m, hd), F32), pltpu.VMEM((m, hd), F32)] + host.scratch,
        compiler_params=_params(("arbitrary", "arbitrary")),
    )(proj, kv, kv, do, gq.reshape(1, hd), gk.reshape(1, hd), dproj, *host.ins)
    dproj, dk, dv, ggq, ggk = res[:5]
    return (dproj, jnp.concatenate([dk, dv], axis=1), ggq.reshape(hd), ggk.reshape(hd), *res[5:])


def _sigmoid(z):
    return 1.0 / (1.0 + jnp.exp(-z))


def _merge_fwd(proj, ys, ws, tm, tc):
    t, cw = ys[0].shape
    d = ws[0].shape[1]
    tm = _tile(t, tm)

    def body(g_ref, ya_ref, yb_ref, yc_ref, wa_ref, wb_ref, wc_ref, oa_ref, ob_ref, oc_ref, out_ref):
        acc = jnp.zeros((tm, tc), F32)
        for s, (y_ref, w_ref, o_ref) in enumerate(((ya_ref, wa_ref, oa_ref), (yb_ref, wb_ref, ob_ref), (yc_ref, wc_ref, oc_ref))):
            o = _dot(y_ref[...], w_ref[...], NN)
            o_ref[...] = o.astype(o_ref.dtype)
            acc = acc + _sigmoid(g_ref[:, s * tc:(s + 1) * tc].astype(F32)) * o
        out_ref[...] = acc.astype(out_ref.dtype)

    blk = pl.BlockSpec((tm, tc), lambda i, j: (i, j))
    y_spec = pl.BlockSpec((tm, cw), lambda i, j: (i, 0))
    w_spec = pl.BlockSpec((cw, tc), lambda i, j: (0, j))
    return pl.pallas_call(
        body,
        name="merge_fwd",
        grid=(t // tm, d // tc),
        in_specs=[pl.BlockSpec((tm, 3 * tc), lambda i, j: (i, j))] + [y_spec] * 3 + [w_spec] * 3,
        out_specs=[blk] * 4,
        out_shape=[jax.ShapeDtypeStruct((t, d), BF16)] * 4,
        compiler_params=_params(("parallel", "parallel")),
    )(proj, *ys, *ws)


def _merge_bwd(proj, o3, dx1, w_out, tm, tc, rider=None):
    t, d = o3[0].shape
    k = dx1.shape[1]
    tm = _tile(t, tm)
    ni, nj = t // tm, d // tc
    host = _Host(rider)

    def body(*refs):
        dx_ref, w_ref, g_ref, oa_ref, ob_ref, oc_ref = refs[:6]
        r_ins = refs[6:6 + host.n_in]
        dg_ref, da_ref, db_ref, dc_ref = refs[6 + host.n_in:10 + host.n_in]
        r_outs = refs[10 + host.n_in:10 + host.n_in + host.n_out]
        sems = refs[10 + host.n_in + host.n_out:]
        i, j = pl.program_id(0), pl.program_id(1)

        def compute():
            dmf = _dot(dx_ref[...], w_ref[...], NT)
            for s, (o_ref, do_ref) in enumerate(((oa_ref, da_ref), (ob_ref, db_ref), (oc_ref, dc_ref))):
                g = _sigmoid(g_ref[:, s * tc:(s + 1) * tc].astype(F32))
                do_ref[...] = (dmf * g).astype(do_ref.dtype)
                dg_ref[:, s * tc:(s + 1) * tc] = (dmf * o_ref[...].astype(F32) * g * (1.0 - g)).astype(dg_ref.dtype)

        host.run(jnp.logical_and(i == 0, j == 0), jnp.logical_and(i == ni - 1, j == nj - 1), r_ins, r_outs, sems, compute)

    blk = pl.BlockSpec((tm, tc), lambda i, j: (i, j))
    wide = pl.BlockSpec((tm, 3 * tc), lambda i, j: (i, j))
    return pl.pallas_call(
        body,
        name="merge_bwd",
        grid=(ni, nj),
        in_specs=[pl.BlockSpec((tm, k), lambda i, j: (i, 0)), pl.BlockSpec((tc, k), lambda i, j: (j, 0)), wide, blk, blk, blk] + host.in_specs,
        out_specs=[wide, blk, blk, blk] + host.out_specs,
        out_shape=[jax.ShapeDtypeStruct(proj.shape, BF16)] + [jax.ShapeDtypeStruct((t, d), BF16)] * 3 + host.out_shapes,
        scratch_shapes=host.scratch,
        compiler_params=_params(("arbitrary", "arbitrary") if rider else ("parallel", "parallel")),
    )(dx1, w_out, proj, *o3, *host.ins)


def _w_in_chunks(d, tc):
    cw = d // 2
    heads = cw // FOX_HEAD_DIM
    conv0, fox0, f0, mq0, gate0 = 0, 3 * cw, 6 * cw, 6 * cw + heads, 7 * cw + heads
    chunks = [(gate0 + s * d + j * tc, gate0 + s * d + (j + 1) * tc) for j in range(d // tc) for s in range(N_BRANCHES)]
    chunks += [(conv0 + s * cw + j * LANES, conv0 + s * cw + (j + 1) * LANES) for j in range(cw // LANES) for s in range(3)]
    chunks += [(fox0 + s * cw + j * FOX_HEAD_DIM, fox0 + s * cw + (j + 1) * FOX_HEAD_DIM) for j in range(heads) for s in range(3)]
    chunks.append((mq0, mq0 + cw))
    return chunks, (f0, f0 + heads)


ROW_TILE = 16
GROUP = 128
GROUP_BACK = 112
SCRATCH_ROWS = 2 * GROUP + 32


def _padded_rows(r):
    return -(-r // GROUP_BACK) * GROUP_BACK


def _rows_from(scr_ref, use, q8, fine, g):
    x = scr_ref[pl.ds(pl.multiple_of(q8 * 8, 8), g + 8), :]
    for s in range(8):
        @pl.when(fine == s)
        def _(s=s):
            use((x if s == 0 else pltpu.roll(x, g + 8 - s, axis=0))[0:g])


def _assemble(name, tbl, grid, step, in_specs, out_spec, out_shape, operands, g, w1, cols_of):
    has_f = len(in_specs) == 3
    k = out_shape.shape[-1]
    c = cols_of

    def body(*refs):
        t_ref, s1_ref, s2_ref = refs[:3]
        f_ref = refs[3] if has_f else None
        out_ref = refs[3 + has_f]
        scr1, scr2, scrf = refs[4 + has_f:]
        t = step()

        def put(y):
            out_ref[...] = y.astype(out_ref.dtype)

        @pl.when(t == 0)
        def _():
            scr1[...] = jnp.zeros_like(scr1)
            scr2[...] = jnp.zeros_like(scr2)
            scrf[...] = jnp.zeros_like(scrf)

        rows = lax.broadcasted_iota(jnp.int32, (g, k), 0)
        n1, a2 = t_ref[c["n1"], t], t_ref[c["a2"], t]
        scr1[0:w1, :] = (s1_ref[0] if len(s1_ref.shape) == 3 else s1_ref[...]).astype(F32)
        _rows_from(scr1, put, t_ref[c["q1"], t], t_ref[c["s1"], t], g)

        @pl.when(a2 < g)
        def _():
            scr2[g:g + s2_ref.shape[0], :] = s2_ref[...].astype(F32)
            _rows_from(scr2, lambda y: put(jnp.where(rows < n1, out_ref[...].astype(F32), y)),
                       t_ref[c["q2"], t], t_ref[c["s2"], t], g)

        if has_f:
            fa, fb = t_ref[c["fa"], t], t_ref[c["fb"], t]

            @pl.when(fb > fa)
            def _():
                scrf[g:g + f_ref.shape[0], :] = f_ref[...].astype(F32)
                inside = jnp.logical_and(rows >= fa, rows < fb)
                _rows_from(scrf, lambda y: put(jnp.where(inside, y, out_ref[...].astype(F32))),
                           t_ref[c["qf"], t], t_ref[c["sf"], t], g)

            valid = t_ref[c["valid"], t]

            @pl.when(valid < g)
            def _():
                out_ref[...] = jnp.where(rows < valid, out_ref[...].astype(F32), 0.0).astype(out_ref.dtype)

    return pl.pallas_call(
        body,
        name=name,
        grid_spec=pltpu.PrefetchScalarGridSpec(
            num_scalar_prefetch=1, grid=grid, in_specs=in_specs, out_specs=out_spec,
            scratch_shapes=[pltpu.VMEM((SCRATCH_ROWS, k), F32)] * 3),
        out_shape=out_shape,
        compiler_params=_params(("arbitrary",) * len(grid)),
    )(jnp.asarray(tbl), *operands)


def _pack_w_in(w8, d, tc):
    blocks, rp, k = w8.shape
    chunks, (f_lo, f_hi) = _w_in_chunks(d, tc)
    r = max(hi for _, hi in chunks) // blocks
    g, w1 = GROUP, GROUP + ROW_TILE
    table = []
    for lo, hi in chunks:
        for g0 in range(lo, hi, g):
            b1, r1 = divmod(g0, r)
            n1 = min(g, r - r1)
            st1 = min(r1 // ROW_TILE * ROW_TILE, rp - w1)
            o1, o2 = r1 - st1, g - n1
            b2 = b1 + 1 if n1 < g else 0
            table.append((b1, st1, o1 // 8, o1 % 8, n1, n1, b2, o2 // 8, o2 % 8))
    names = ("b1", "st1", "q1", "s1", "n1", "a2", "b2", "q2", "s2")
    cols_of = {n: i for i, n in enumerate(names)}
    tbl = np.array(table, np.int32).T
    c = cols_of
    w_all = _assemble(
        "pack_w_in", tbl, (len(table),), lambda: pl.program_id(0),
        [pl.BlockSpec((pl.Element(1), pl.Element(w1), pl.Element(k)), lambda i, t: (t[c["b1"], i], pl.multiple_of(t[c["st1"], i], ROW_TILE), 0)),
         pl.BlockSpec((None, g, k), lambda i, t: (t[c["b2"], i], 0, 0))],
        pl.BlockSpec((g, k), lambda i, t: (i, 0)),
        jax.ShapeDtypeStruct((len(table) * g, k), w8.dtype), [w8, w8], g, w1, cols_of)
    fb, fr = divmod(f_lo, r)
    return w_all, jnp.pad(w8[fb, fr:fr + f_hi - f_lo], ((0, F_ROWS - (f_hi - f_lo)), (0, 0)))


def _unpack_g_in(g_all, g_f, d, tc, blocks):
    n_all, k = g_all.shape
    chunks, (f_lo, f_hi) = _w_in_chunks(d, tc)
    r = max(hi for _, hi in chunks) // blocks
    rp = _padded_rows(r)
    g, w1 = GROUP_BACK, GROUP_BACK + ROW_TILE
    pos, spans = 0, [(f_lo, f_hi, None)]
    for lo, hi in chunks:
        spans.append((lo, hi, pos))
        pos += hi - lo
    spans.sort()
    table = []
    for b in range(blocks):
        for l0 in range(0, rp, g):
            valid = max(0, min(g, r - l0))
            g0, segs, fa, fb, of = b * r + l0, [], 0, 0, 0
            for lo, hi, p in spans:
                a, e = max(lo, g0), min(hi, g0 + valid)
                if a < e and p is None:
                    fa, fb, of = a - g0, e - g0, g + (a - lo) - (a - g0)
                elif a < e:
                    segs.append((a - g0, p + a - lo, e - a))
            assert len(segs) <= 2 and (not segs or segs[0][0] == 0 or len(segs) == 1)
            first = segs[0] if segs and segs[0][0] == 0 else (0, 0, 0)
            second = segs[-1] if segs and segs[-1][0] > 0 else (g, 0, 0)
            st1 = min(first[1] // ROW_TILE * ROW_TILE, n_all - w1)
            o1, o2 = first[1] - st1, g - second[0]
            assert second[1] % GROUP == 0
            table.append((st1, o1 // 8, o1 % 8, first[2], second[0], second[1] // GROUP, o2 // 8, o2 % 8,
                          fa, fb, of // 8, of % 8, valid))
    names = ("st1", "q1", "s1", "n1", "a2", "j2", "q2", "s2", "fa", "fb", "qf", "sf", "valid")
    cols_of = {n: i for i, n in enumerate(names)}
    tbl = np.array(table, np.int32).T
    c, per = cols_of, rp // g
    return _assemble(
        "unpack_g_in", tbl, (blocks, per), lambda: pl.program_id(0) * per + pl.program_id(1),
        [pl.BlockSpec((pl.Element(w1), pl.Element(k)), lambda b, u, t: (pl.multiple_of(t[c["st1"], b * per + u], ROW_TILE), 0)),
         pl.BlockSpec((GROUP, k), lambda b, u, t: (t[c["j2"], b * per + u], 0)),
         pl.BlockSpec((F_ROWS, k), lambda b, u, t: (0, 0))],
        pl.BlockSpec((None, g, k), lambda b, u, t: (b, u, 0)),
        jax.ShapeDtypeStruct((blocks, rp, k), g_all.dtype), [g_all, g_all, g_f], g, w1, cols_of)


def _unblock(w8):
    return w8.transpose(1, 0, 2).reshape(w8.shape[1], -1)


def _tile2(r, cols, tr, tcols):
    if r % 8 == 0:
        return _tile(r, tr), cols
    return r, _tile(cols, tcols)


def _pair_sum(name, g8, got, c, halves=1):
    _, r, cols = g8.shape
    rows = r // halves
    tr, tcols = _tile2(rows, cols, 256, 256)
    per = rows // tr
    out = []
    for part in range(halves):
        def body(c_ref, g_ref, s_ref, o_ref):
            o_ref[...] = (g_ref[...].astype(F32) + s_ref[...].astype(F32)).astype(o_ref.dtype)

        own = pl.BlockSpec((None, tr, tcols), lambda q, i, j, c_ref, part=part: (2 * q + c_ref[0], part * per + i, j))
        src = pl.BlockSpec((None, tr, tcols), lambda q, i, j, c_ref, part=part: (q, part * per + i, j))
        dst = pl.BlockSpec((None, tr, tcols), lambda q, i, j, c_ref: (q, i, j))
        out.append(pl.pallas_call(
            body,
            name=name if halves == 1 else f"{name}_{part}",
            grid_spec=pltpu.PrefetchScalarGridSpec(num_scalar_prefetch=1, grid=(N_CHIPS, per, cols // tcols),
                                                   in_specs=[own, src], out_specs=dst),
            out_shape=jax.ShapeDtypeStruct((N_CHIPS, rows, cols), BF16),
            compiler_params=_params(("parallel",) * 3),
        )(c, g8, got))
    return out[0] if halves == 1 else out


def _local_step(x, mem, target, w, small, comm=None):
    t, d = x.shape
    cw = d // 2
    heads = cw // FOX_HEAD_DIM
    tc = min(512, d)
    tq = min(512, t)
    off_conv, off_fox, off_mq = 3 * d, 3 * d + 3 * cw, 3 * d + 6 * cw
    w, small = dict(w), dict(small)
    big = dict(tm=1024, tn=512, tk=2048)
    wide_k = dict(tm=512, tn=1024, tk=4096)
    tall = dict(tm=2048, tn=512, tk=2048)

    if comm:
        first = _gather_rider([comm["shards"]["w_in"], comm["conv_w"]], True)
        h, w["w_in"], cw8 = _rms_fwd("rms1_fwd", x, small["norm1_g"], rider=first)
        small["conv_w"] = _unblock(cw8)
        w_all, w_f = _pack_w_in(w["w_in"], d, tc)
        early = ("w_conv_out", "w_fox_out", "w_mem_out", "w_out", "w_mem_kv", "w_down")
        proj, *got = _matmul("proj", "nt", h, w_all, outs=[BF16], rider=_gather_rider([comm["shards"][n] for n in early], True), **tall)
        for n, val in zip(early, got):
            w[n] = _unblock(val) if n in COLUMN_SPLIT else val.reshape(-1, val.shape[-1])
    else:
        h = _rms_fwd("rms1_fwd", x, small["norm1_g"])
        w_all, w_f = _pack_w_in(w["w_in"], d, tc)
        proj = _matmul("proj", "nt", h, w_all, outs=[BF16], **tall)
    z_row = _matmul("proj_f", "nt", w_f, h, outs=[F32], tm=F_ROWS, tn=512, tk=2048)

    y_conv = _conv_fwd(proj, off_conv, small["conv_w"], LANES)

    b_col = jnp.pad(small["b_f"], (0, F_ROWS - heads)).reshape(F_ROWS, 1)
    c_row3 = _forget_fwd(z_row, b_col)[:heads].reshape(heads, 1, t)
    c_colb = _rows_to_colb(c_row3, tq)
    if comm:
        y_fox, lse, got = _fox_fwd(proj, off_fox, small["fox_q_g"], small["fox_k_g"], c_row3, c_colb, heads, 2 * tq,
                                   rider=_gather_rider([comm["shards"]["w_up"]], False))
        w["w_up"] = _unblock(got)
    else:
        y_fox, lse = _fox_fwd(proj, off_fox, small["fox_q_g"], small["fox_k_g"], c_row3, c_colb, heads, 2 * tq)

    nm = _rms_fwd("mem_rms_fwd", mem, small["mem_norm_g"])
    kv = _matmul("mem_kv", "nn", nm, w["w_mem_kv"], outs=[F32], tm=256, tn=512, tk=2048)
    y_mem = _mem_fwd(proj, off_mq, kv, small["mem_q_g"], small["mem_k_g"], tq)

    ys = (y_conv, y_fox, y_mem)
    w_outs = (w["w_conv_out"], w["w_fox_out"], w["w_mem_out"])
    *o3, merged = _merge_fwd(proj, ys, w_outs, 1024, tc)
    def out_epilogue(acc, xr, g2):
        x1r = acc + xr
        r = lax.rsqrt(jnp.mean(x1r * x1r, axis=-1, keepdims=True) + EPS)
        return x1r, x1r * r * g2

    x1, h2 = _matmul("out_proj", "nn", merged, w["w_out"], outs=[F32, BF16], extras=[x, small["norm2_g"].reshape(1, d)],
                     epilogue=out_epilogue, tm=512, tn=d, tk=2048)

    def up_epilogue(acc):
        return acc, jnp.square(jnp.maximum(acc, 0.0))

    up, act = _matmul("mlp_up", "nn", h2, w["w_up"], outs=[BF16, BF16], epilogue=up_epilogue, **big)

    def loss_epilogue(acc, x1r, tr):
        dy = (acc + x1r - tr) * (1.0 / d)
        return dy, dy

    dy, dyb = _matmul("mlp_down", "nn", act, w["w_down"], outs=[F32, BF16], extras=[x1, target],
                      epilogue=loss_epilogue, tm=1024, tn=512, tk=4096)

    def dup_epilogue(acc, upr):
        return (acc * 2.0 * jnp.maximum(upr.astype(F32), 0.0),)

    def by_owner(g):
        return g.reshape(N_DEV, -1, g.shape[-1])

    g, parts = {}, {}
    g["w_down"] = _matmul("d_w_down", "tn", act, dyb, outs=[BF16], **wide_k)
    if comm:
        dup = _matmul("d_act", "nt", dyb, w["w_down"], outs=[BF16], extras=[up], epilogue=dup_epilogue, **tall)
        g["w_up"], got = _matmul("d_w_up", "tn", h2, dup, outs=[BF16], out_blocks=True,
                                 rider=_pair_rider([by_owner(g["w_down"])]), **wide_k)
        pair, pair_later = _pair_sum("pair_w_down", by_owner(g["w_down"]), got, comm["c"], halves=2)
        dh2, down_top, got = _matmul("d_h2", "nt", dup, w["w_up"], outs=[F32],
                                     rider=_join_riders(_chip_rider([pair]), _pair_rider([g["w_up"]])), **tall)
        pair_up = _pair_sum("pair_w_up", g["w_up"], got, comm["c"])
    else:
        dup = _matmul("d_act", "nt", dyb, w["w_down"], outs=[BF16], extras=[up], epilogue=dup_epilogue, **tall)
        g["w_up"] = _matmul("d_w_up", "tn", h2, dup, outs=[BF16], out_blocks=True, **wide_k)
        dh2 = _matmul("d_h2", "nt", dup, w["w_up"], outs=[F32], **tall)
    dx1, dx1b, g_norm2, dy_sq = _rms_bwd("rms2_bwd", dh2, x1, small["norm2_g"], res=dy)
    loss = dy_sq * (0.5 * d)

    g["w_out"] = _matmul("d_w_out", "tn", merged, dx1b, outs=[BF16], **wide_k)
    if comm:
        dproj, *do3, down_bottom = _merge_bwd(proj, o3, dx1b, w["w_out"], 1024, tc, rider=_chip_rider([pair_later]))
        parts["w_down"] = [down_top, down_bottom]
    else:
        dproj, *do3 = _merge_bwd(proj, o3, dx1b, w["w_out"], 1024, tc)
    names = ("w_conv_out", "w_fox_out", "w_mem_out")
    dys = []
    for s in range(3):
        g[names[s]] = _matmul(f"d_w_branch{s}", "tn", ys[s], do3[s], outs=[BF16], out_blocks=True, **wide_k)
        dys.append(_matmul(f"d_branch{s}", "nt", do3[s], w_outs[s], outs=[BF16], **tall))

    dproj, dkv, g_mq, g_mk = _mem_bwd(proj, off_mq, kv, dys[2], small["mem_q_g"], small["mem_k_g"], tq, dproj)
    g["w_mem_kv"] = _matmul("d_w_mem_kv", "tn", nm, dkv, outs=[BF16], **wide_k)
    dnm = _matmul("d_mem_norm", "nt", dkv, w["w_mem_kv"], outs=[F32], tm=256, tn=512, tk=2048)
    _, _, g_mem_norm, _ = _rms_bwd("mem_rms_bwd", dnm, mem, small["mem_norm_g"])

    mid = ("w_out", "w_conv_out", "w_fox_out", "w_mem_out", "w_mem_kv")
    if comm:
        mid8 = [g[n] if n in names else by_owner(g[n]) for n in mid]
        dproj, g_conv_w, *got = _conv_bwd(proj, off_conv, small["conv_w"], dys[0], LANES, dproj, rider=_pair_rider(mid8))
        pairs_mid = [_pair_sum("pair_" + n, g8, s4, comm["c"]) for n, g8, s4 in zip(mid, mid8, got)]
        dproj, dc, g_fq, g_fk, parts["w_up"] = _fox_bwd(proj, off_fox, y_fox, dys[1], small["fox_q_g"], small["fox_k_g"], c_row3,
                                                        c_colb, lse, heads, tq, dproj, rider=_chip_rider([pair_up]))
    else:
        dproj, g_conv_w = _conv_bwd(proj, off_conv, small["conv_w"], dys[0], LANES, dproj)
        dproj, dc, g_fq, g_fk = _fox_bwd(proj, off_fox, y_fox, dys[1], small["fox_q_g"], small["fox_k_g"], c_row3, c_colb,
                                         lse, heads, tq, dproj)
    dc_row = jnp.pad(dc.reshape(heads, t), ((0, F_ROWS - heads), (0, 0)))
    dz_row, db = _forget_bwd(z_row, b_col, dc_row)

    if comm:
        g_all, *got = _matmul("d_w_in", "tn", dproj, h, outs=[BF16], j_outer=True, rider=_chip_rider(pairs_mid), **wide_k)
        parts.update(zip(mid, got))
    else:
        g_all = _matmul("d_w_in", "tn", dproj, h, outs=[BF16], j_outer=True, **wide_k)
    g_wf = _matmul("d_w_f", "nn", dz_row, h, outs=[BF16], tm=F_ROWS, tn=512, tk=4096)
    g["w_in"] = _unpack_g_in(g_all, g_wf, d, tc, w["w_in"].shape[0])
    dh = _matmul("d_h_f", "tn", dz_row, w_f, outs=[F32], tm=1024, tn=512, tk=F_ROWS)
    add_prev = lambda acc, prev: (acc + prev,)
    if comm:
        g_in8 = g["w_in"]
        got = _run_rider("pair_exchange_w_in", _pair_rider([g_in8]))[0]
        pair = _pair_sum("pair_w_in", g_in8, got, comm["c"])
        dh, parts["w_in"] = _matmul("d_h", "nn", dproj, w_all, outs=[F32], extras=[dh], epilogue=add_prev,
                                    rider=_chip_rider([pair]), tm=1024, tn=512, tk=3328)
    else:
        dh = _matmul("d_h", "nn", dproj, w_all, outs=[F32], extras=[dh], epilogue=add_prev, tm=1024, tn=512, tk=3328)
    grad_x, _, g_norm1, _ = _rms_bwd("rms1_bwd", dh, x, small["norm1_g"], res=dx1)

    gs = dict(norm1_g=g_norm1, b_f=db[:heads, 0], conv_w=g_conv_w, fox_q_g=g_fq.reshape(-1), fox_k_g=g_fk.reshape(-1),
              mem_norm_g=g_mem_norm, mem_q_g=g_mq, mem_k_g=g_mk, norm2_g=g_norm2)
    return loss, grad_x, (parts if comm else g), gs


def _adamw_math(w, g, m, v):
    m = ADAM_B1 * m + (1.0 - ADAM_B1) * g
    v = ADAM_B2 * v + (1.0 - ADAM_B2) * jnp.square(g)
    m_hat = m / (1.0 - ADAM_B1 ** ADAM_STEP)
    v_hat = v / (1.0 - ADAM_B2 ** ADAM_STEP)
    delta = -ADAM_LR * (m_hat / (jnp.sqrt(v_hat) + ADAM_EPS) + ADAM_WD * w)
    return delta, m, v


def _adamw(name, parts, w, m, v):
    r, c = w.shape
    pieces = list(parts) if isinstance(parts, (list, tuple)) else [parts]
    n_parts = pieces[0].shape[0]
    rp = sum(p.shape[1] for p in pieces)
    if rp == r:
        tr, tc = _tile2(pieces[0].shape[1], c, 128, 256)
    else:
        tr, tc = _tile(rp, 256), _tile(c, 1024)
    per = pieces[0].shape[1] // tr
    n_p = len(pieces)

    def body(*refs):
        p_refs = refs[:n_p]
        w_ref, m_ref, v_ref, g_ref, d_ref, nm_ref, nv_ref = refs[n_p:]
        i = pl.program_id(0)
        g = None
        for a, p_ref in enumerate(p_refs):
            s_a = p_ref[0].astype(F32)
            for s in range(1, n_parts):
                s_a = s_a + p_ref[s].astype(F32)
            g = s_a if g is None else jnp.where(i >= a * per, s_a, g)
        delta, nm, nv = _adamw_math(w_ref[...], g, m_ref[...], v_ref[...])
        g_ref[...] = g
        d_ref[...] = delta
        nm_ref[...] = nm
        nv_ref[...] = nv

    blk = pl.BlockSpec((tr, tc), lambda i, j: (i, j))
    p_specs = [pl.BlockSpec((n_parts, tr, tc), lambda i, j, a=a: (0, jnp.clip(i - a * per, 0, per - 1), j)) for a in range(n_p)]
    return pl.pallas_call(
        body,
        name=name,
        grid=(rp // tr, c // tc),
        in_specs=p_specs + [blk, blk, blk],
        out_specs=[blk] * 4,
        out_shape=[jax.ShapeDtypeStruct((r, c), F32)] * 4,
        compiler_params=_params(("parallel", "parallel")),
    )(*pieces, w, m, v)


def _sum_parts(name, parts):
    n_parts, r, c = parts.shape

    def body(p_ref, o_ref):
        acc = p_ref[0]
        for s in range(1, n_parts):
            acc = acc + p_ref[s]
        o_ref[...] = acc

    return pl.pallas_call(body, name=name, out_shape=jax.ShapeDtypeStruct((r, c), F32))(parts)


BIG = ("w_in", "w_mem_kv", "w_conv_out", "w_fox_out", "w_mem_out", "w_out", "w_up", "w_down")
COLUMN_SPLIT = ("w_in", "w_conv_out", "w_fox_out", "w_mem_out", "w_up")
SMALL = ("norm1_g", "b_f", "conv_w", "fox_q_g", "fox_k_g", "mem_norm_g", "mem_q_g", "mem_k_g", "norm2_g")
WEIGHTS = ("norm1_g", "w_in", "b_f", "conv_w", "fox_q_g", "fox_k_g", "mem_norm_g", "w_mem_kv", "mem_q_g", "mem_k_g",
           "w_conv_out", "w_fox_out", "w_mem_out", "w_out", "norm2_g", "w_up", "w_down")


def _pack(vectors):
    rows = []
    for vec in vectors:
        n = vec.shape[0]
        rows.append(jnp.pad(vec, (0, -n % LANES)).reshape(-1, LANES))
    out = jnp.concatenate(rows, axis=0)
    return jnp.pad(out, ((0, -out.shape[0] % 8), (0, 0)))


def _unpack(packed, sizes):
    out, row = [], 0
    for n in sizes:
        nr = -(-n // LANES)
        out.append(packed[row:row + nr].reshape(-1)[:n])
        row += nr
    return out


def kernel(x, mem, norm1_g, w_in, b_f, conv_w, fox_q_g, fox_k_g, mem_norm_g, w_mem_kv, mem_q_g, mem_k_g, w_conv_out, w_fox_out, w_mem_out, w_out, norm2_g, w_up, w_down, loss_target, m_norm1_g, m_w_in, m_b_f, m_conv_w, m_fox_q_g, m_fox_k_g, m_mem_norm_g, m_w_mem_kv, m_mem_q_g, m_mem_k_g, m_w_conv_out, m_w_fox_out, m_w_mem_out, m_w_out, m_norm2_g, m_w_up, m_w_down, v_norm1_g, v_w_in, v_b_f, v_conv_w, v_fox_q_g, v_fox_k_g, v_mem_norm_g, v_w_mem_kv, v_mem_q_g, v_mem_k_g, v_w_conv_out, v_w_fox_out, v_w_mem_out, v_w_out, v_norm2_g, v_w_up, v_w_down):
    args = dict(locals())
    wts = {n: args[n] for n in WEIGHTS}
    ms = {n: args["m_" + n] for n in WEIGHTS}
    vs = {n: args["v_" + n] for n in WEIGHTS}
    x_pos, y_pos, c_pos = _position()
    me = _index(x_pos, y_pos, c_pos)

    shards = {n: wts[n].astype(BF16) for n in BIG if n != "w_in"}
    rows_in = w_in.shape[1]
    shards["w_in"] = jnp.pad(w_in.T.astype(BF16), ((0, _padded_rows(rows_in) - rows_in), (0, 0)))
    small = {n: wts[n] for n in SMALL if n != "conv_w"}
    comm = {"shards": shards, "conv_w": conv_w, "c": c_pos.astype(jnp.int32).reshape(1)}

    loss, grad_x, parts, gs = _local_step(x[0], mem[0], loss_target[0], {}, small, comm)

    out_g, out_d, out_m, out_v = {}, {}, {}, {}
    for n in BIG:
        if n == "w_in":
            res = _adamw("adamw_" + n, parts[n], wts[n].T, ms[n].T, vs[n].T)
            out_g[n], out_d[n], out_m[n], out_v[n] = (r.T for r in res)
        else:
            out_g[n], out_d[n], out_m[n], out_v[n] = _adamw("adamw_" + n, parts[n], wts[n], ms[n], vs[n])

    small_sizes = [int(math.prod(gs[n].shape)) for n in SMALL]
    packed = _pack([gs[n].reshape(-1) for n in SMALL])
    gsum = _sum_parts("sum_small", _run_rider("exchange_small", _broadcast_rider([packed]))[0])
    gsmall = dict(zip(SMALL, _unpack(gsum, small_sizes)))
    cols = conv_w.shape[1]
    gsmall["conv_w"] = lax.dynamic_slice(gsmall["conv_w"].reshape(CONV_TAPS, -1), (0, me * cols), (CONV_TAPS, cols)).reshape(-1)
    pg, pw, pm, pv = (_pack([src[n].reshape(-1) for n in SMALL]) for src in (gsmall, wts, ms, vs))
    _, sd, sm, sv = _adamw("adamw_small", pg[None], pw, pm, pv)
    local_sizes = [int(math.prod(wts[n].shape)) for n in SMALL]
    for dst, src in ((out_d, sd), (out_m, sm), (out_v, sv)):
        for n, val in zip(SMALL, _unpack(src, local_sizes)):
            dst[n] = val.reshape(wts[n].shape)
    for n in SMALL:
        out_g[n] = gsmall[n].reshape(wts[n].shape)

    loss = lax.psum(loss, MESH_AXES)
    return (loss, grad_x[None], *[out_g[n] for n in WEIGHTS], *[out_d[n] for n in WEIGHTS],
            *[out_m[n] for n in WEIGHTS], *[out_v[n] for n in WEIGHTS])
```

```python
import math

import numpy as np
import jax
import jax.numpy as jnp
from jax import lax
from jax.experimental import pallas as pl
from jax.experimental.pallas import tpu as pltpu

F32 = jnp.float32
BF16 = jnp.bfloat16

EPS = 1e-6
N_DEV = 8
N_CHIPS = 4
FOX_HEAD_DIM = 128
MEM_HEADS = 4
CONV_TAPS = 3
N_BRANCHES = 3
F_ROWS = 16

ADAM_LR = 0.001
ADAM_B1 = 0.9
ADAM_B2 = 0.999
ADAM_EPS = 1e-08
ADAM_WD = 0.01
ADAM_STEP = 10

V7X_VMEM_BYTES = 64 * 1024 * 1024
VMEM_LIMIT = V7X_VMEM_BYTES * 3 // 4
LANES = 128
NEG = -1e30
MAX_KEYS = 1024

MESH_AXES = ("x", "y", "c")
MESH = pl.DeviceIdType.MESH
ANY = pl.BlockSpec(memory_space=pl.ANY)

NN = (((1,), (0,)), ((), ()))
NT = (((1,), (1,)), ((), ()))
TN = (((0,), (0,)), ((), ()))


def _params(sem):
    return pltpu.CompilerParams(dimension_semantics=sem, vmem_limit_bytes=VMEM_LIMIT)


def _dot(a, b, dn):
    return lax.dot_general(a, b, dn, preferred_element_type=F32)


def _tile(n, t):
    if n <= t:
        return n
    for step in (LANES, 16):
        for cand in range(t - t % step, 0, -step):
            if n % cand == 0:
                return cand
    raise ValueError((n, t))


class _Rider:
    def __init__(self, ins, out_shapes, sem_shapes, start, finish, middle=None):
        self.ins, self.out_shapes, self.sem_shapes = list(ins), list(out_shapes), list(sem_shapes)
        self.start, self.finish, self.middle = start, finish, middle


def _position():
    return lax.axis_index("x"), lax.axis_index("y"), lax.axis_index("c")


def _index(px, py, pc):
    return 4 * px + 2 * py + pc


def _dma_sems(n, per):
    return [pltpu.SemaphoreType.DMA((n, per)), pltpu.SemaphoreType.DMA((n, per)), pltpu.SemaphoreType.DMA((n,))]


def _gather_rider(shards, pass_on):
    n = len(shards)

    def copies(ins, outs, sems):
        send_sems, recv_sems, local_sems = sems
        x, y, c = _position()
        me, sibling = (x, y, c), (x, y, 1 - c)
        chips = [(1 - x, y), (x, 1 - y), (1 - x, 1 - y)]

        def copy(a, k, block, to, src=None, k_send=None):
            rows = outs[a].at[_index(*block)]
            return pltpu.make_async_remote_copy(
                src_ref=rows if src is None else src, dst_ref=rows,
                send_sem=send_sems.at[a, k if k_send is None else k_send], recv_sem=recv_sems.at[a, k],
                device_id=to, device_id_type=MESH)

        mine = [pltpu.make_async_copy(ins[a], outs[a].at[_index(*me)], local_sems.at[a]) for a in range(n)]
        first = []
        for a in range(n):
            first.append(copy(a, 0, me, sibling, src=ins[a]))
            first += [copy(a, 1 + j, me, (*chips[j], c), src=ins[a]) for j in range(2 if pass_on else 3)]
        return copy, mine, first, me, sibling, chips, c

    def start(ins, outs, sems):
        _, mine, first, *_ = copies(ins, outs, sems)
        for cp in mine + first:
            cp.start()

    def by_kind(c, fn):
        if pass_on:
            pl.when(c == 1)(lambda: fn(0, 1))
            pl.when(c == 0)(lambda: fn(1, 0))
        else:
            fn(0, 1)

    def onward(copy, a, j_on, j_to, chips, c, sibling):
        third = [copy(a, 3, (*chips[j_on], c), (*chips[j_to], c), k_send=7)] if pass_on else []
        return third + [copy(a, 4 + j_on, (*chips[j_on], c), sibling)], [copy(a, 4 + j_to, (*chips[j_to], c), sibling)]

    def middle(ins, outs, sems):
        copy, _, _, me, sibling, chips, c = copies(ins, outs, sems)

        def fn(j_on, j_to):
            for a in range(n):
                for j, after in zip((j_on, j_to), onward(copy, a, j_on, j_to, chips, c, sibling)):
                    copy(a, 1 + j, (*chips[j], c), me).wait_recv()
                    for cp in after:
                        cp.start()

        by_kind(c, fn)

    def finish(ins, outs, sems):
        copy, mine, first, me, sibling, chips, c = copies(ins, outs, sems)

        def fn(j_on, j_to):
            passed = [cp for a in range(n) for after in onward(copy, a, j_on, j_to, chips, c, sibling) for cp in after]
            for a in range(n):
                copy(a, 3, (*chips[2], c), me).wait_recv()
                passed.append(copy(a, 6, (*chips[2], c), sibling))
                passed[-1].start()
            for a in range(n):
                copy(a, 0, sibling, me).wait_recv()
                for j, chip in enumerate(chips):
                    copy(a, 4 + j, (*chip, 1 - c), me).wait_recv()
            for cp in first + passed:
                cp.wait_send()
            for cp in mine:
                cp.wait()

        by_kind(c, fn)

    out_shapes = [jax.ShapeDtypeStruct((N_DEV,) + s.shape, s.dtype) for s in shards]
    return _Rider(shards, out_shapes, _dma_sems(n, 8), start, finish, middle)


def _pair_rider(grads):
    n = len(grads)

    def copies(ins, outs, sems):
        send_sems, recv_sems, _ = sems
        x, y, c = _position()
        return [pltpu.make_async_remote_copy(
            src_ref=ins[a].at[2 * q + (1 - c)], dst_ref=outs[a].at[q],
            send_sem=send_sems.at[a, q], recv_sem=recv_sems.at[a, q], device_id=(x, y, 1 - c), device_id_type=MESH)
            for a in range(n) for q in range(N_CHIPS)]

    def start(ins, outs, sems):
        for cp in copies(ins, outs, sems):
            cp.start()

    def finish(ins, outs, sems):
        cps = copies(ins, outs, sems)
        for cp in cps:
            cp.wait_recv()
        for cp in cps:
            cp.wait_send()

    out_shapes = [jax.ShapeDtypeStruct((N_CHIPS,) + g.shape[1:], g.dtype) for g in grads]
    return _Rider(grads, out_shapes, _dma_sems(n, N_CHIPS), start, finish)


def _chip_rider(parts):
    n = len(parts)

    def copies(ins, outs, sems):
        send_sems, recv_sems, local_sems = sems
        x, y, c = _position()
        q_me = 2 * x + y
        chips = [(1 - x, y), (x, 1 - y), (1 - x, 1 - y)]
        mine = [pltpu.make_async_copy(ins[a].at[q_me], outs[a].at[q_me], local_sems.at[a]) for a in range(n)]
        sends, arrivals = [], []
        for a in range(n):
            for j, (tx, ty) in enumerate(chips):
                q_t = 2 * tx + ty
                sends.append(pltpu.make_async_remote_copy(
                    src_ref=ins[a].at[q_t], dst_ref=outs[a].at[q_me],
                    send_sem=send_sems.at[a, j], recv_sem=recv_sems.at[a, j], device_id=(tx, ty, c), device_id_type=MESH))
                arrivals.append(pltpu.make_async_remote_copy(
                    src_ref=ins[a].at[q_t], dst_ref=outs[a].at[q_t],
                    send_sem=send_sems.at[a, j], recv_sem=recv_sems.at[a, j], device_id=(tx, ty, c), device_id_type=MESH))
        return mine, sends, arrivals

    def start(ins, outs, sems):
        mine, sends, _ = copies(ins, outs, sems)
        for cp in mine + sends:
            cp.start()

    def finish(ins, outs, sems):
        mine, sends, arrivals = copies(ins, outs, sems)
        for cp in arrivals:
            cp.wait_recv()
        for cp in sends:
            cp.wait_send()
        for cp in mine:
            cp.wait()

    out_shapes = [jax.ShapeDtypeStruct(p.shape, p.dtype) for p in parts]
    return _Rider(parts, out_shapes, _dma_sems(n, 3), start, finish)


def _broadcast_rider(values):
    n = len(values)

    def copies(ins, outs, sems):
        send_sems, recv_sems, local_sems = sems
        x, y, c = _position()
        me = _index(x, y, c)

        def peer(k):
            return (1 - x if k & 4 else x, 1 - y if k & 2 else y, 1 - c if k & 1 else c)

        mine = [pltpu.make_async_copy(ins[a], outs[a].at[me], local_sems.at[a]) for a in range(n)]
        sends, arrivals = [], []
        for a in range(n):
            for k in range(1, N_DEV):
                common = dict(send_sem=send_sems.at[a, k - 1], recv_sem=recv_sems.at[a, k - 1], device_id=peer(k), device_id_type=MESH)
                sends.append(pltpu.make_async_remote_copy(src_ref=ins[a], dst_ref=outs[a].at[me], **common))
                arrivals.append(pltpu.make_async_remote_copy(src_ref=ins[a], dst_ref=outs[a].at[_index(*peer(k))], **common))
        return mine, sends, arrivals

    def start(ins, outs, sems):
        mine, sends, _ = copies(ins, outs, sems)
        for cp in mine + sends:
            cp.start()

    def finish(ins, outs, sems):
        mine, sends, arrivals = copies(ins, outs, sems)
        for cp in arrivals:
            cp.wait_recv()
        for cp in sends:
            cp.wait_send()
        for cp in mine:
            cp.wait()

    out_shapes = [jax.ShapeDtypeStruct((N_DEV,) + v.shape, v.dtype) for v in values]
    return _Rider(values, out_shapes, _dma_sems(n, 7), start, finish)


def _join_riders(*riders):
    def each(fn_name, ins, outs, sems):
        i = o = s = 0
        for r in riders:
            n_i, n_o, n_s = len(r.ins), len(r.out_shapes), len(r.sem_shapes)
            if getattr(r, fn_name) is not None:
                getattr(r, fn_name)(ins[i:i + n_i], outs[o:o + n_o], sems[s:s + n_s])
            i, o, s = i + n_i, o + n_o, s + n_s

    middle = (lambda ins, outs, sems: each("middle", ins, outs, sems)) if any(r.middle for r in riders) else None
    return _Rider([a for r in riders for a in r.ins], [a for r in riders for a in r.out_shapes],
                  [a for r in riders for a in r.sem_shapes],
                  lambda ins, outs, sems: each("start", ins, outs, sems),
                  lambda ins, outs, sems: each("finish", ins, outs, sems), middle)


def _run_rider(name, rider):
    n_in, n_out = len(rider.ins), len(rider.out_shapes)

    def body(*refs):
        ins, outs, sems = refs[:n_in], refs[n_in:n_in + n_out], refs[n_in + n_out:]
        rider.start(ins, outs, sems)
        if rider.middle is not None:
            rider.middle(ins, outs, sems)
        rider.finish(ins, outs, sems)

    return pl.pallas_call(
        body, name=name, in_specs=[ANY] * n_in, out_specs=[ANY] * n_out, out_shape=rider.out_shapes,
        scratch_shapes=rider.sem_shapes)(*rider.ins)


class _Host:
    def __init__(self, rider):
        self.rider = rider
        self.n_in = len(rider.ins) if rider else 0
        self.n_out = len(rider.out_shapes) if rider else 0
        self.n_sem = len(rider.sem_shapes) if rider else 0
        self.ins = rider.ins if rider else []
        self.in_specs = [ANY] * self.n_in
        self.out_specs = [ANY] * self.n_out
        self.out_shapes = rider.out_shapes if rider else []
        self.scratch = rider.sem_shapes if rider else []

    def run(self, first, last, ins, outs, sems, compute, midway=None):
        if self.rider is None:
            compute()
            return

        @pl.when(first)
        def _():
            self.rider.start(ins, outs, sems)

        compute()
        if self.rider.middle is not None and midway is not None:
            pl.when(midway)(lambda: self.rider.middle(ins, outs, sems))

        @pl.when(last)
        def _():
            if self.rider.middle is not None and midway is None:
                self.rider.middle(ins, outs, sems)
            self.rider.finish(ins, outs, sems)


def _matmul(name, kind, a, b, *, tm, tn, tk, outs, epilogue=None, extras=(), out_blocks=False, rider=None, j_outer=False):
    if kind == "nn":
        (m, kdim), n = a.shape, b.shape[1]
    elif kind == "nt":
        (m, kdim), n = a.shape, b.shape[0]
    else:
        (kdim, m), n = a.shape, b.shape[1]
    if out_blocks:
        tn = min(tn, n // N_DEV)
    tm, tn, tk = _tile(m, tm), _tile(n, tn), _tile(kdim, tk)
    ni, nj, nk = m // tm, n // tn, kdim // tk

    def spec(shape, fn):
        return pl.BlockSpec(shape, (lambda g0, g1, k: fn(g1, g0, k)) if j_outer else fn)

    a_spec = spec((tk, tm), lambda i, j, k: (k, i)) if kind == "tn" else spec((tm, tk), lambda i, j, k: (i, k))
    b_spec = spec((tn, tk), lambda i, j, k: (j, k)) if kind == "nt" else spec((tk, tn), lambda i, j, k: (k, j))
    dn = {"nn": NN, "nt": NT, "tn": TN}[kind]

    tile_spec = spec((tm, tn), lambda i, j, k: (i, j))
    row_spec = spec((1, tn), lambda i, j, k: (0, j))
    if out_blocks:
        width = n // N_DEV
        r_out = width // tn
        out_shape = [jax.ShapeDtypeStruct((N_DEV, m, width), dt) for dt in outs]
        out_specs = [spec((None, tm, tn), lambda i, j, k: (j // r_out, i, j % r_out)) for _ in outs]
    else:
        out_shape = [jax.ShapeDtypeStruct((m, n), dt) for dt in outs]
        out_specs = [tile_spec for _ in outs]
    n_ex, n_out = len(extras), len(outs)
    host = _Host(rider)
    n_acc = 1 if nk > 1 else 0

    def body(*refs):
        a_ref, b_ref = refs[0], refs[1]
        pos = 2
        ex_refs = refs[pos:pos + n_ex]; pos += n_ex
        r_ins = refs[pos:pos + host.n_in]; pos += host.n_in
        out_refs = refs[pos:pos + n_out]; pos += n_out
        r_outs = refs[pos:pos + host.n_out]; pos += host.n_out
        acc_ref = refs[pos] if n_acc else None
        sems = refs[pos + n_acc:]
        i, j, k = pl.program_id(1 if j_outer else 0), pl.program_id(0 if j_outer else 1), pl.program_id(2)

        def finish_tile(acc):
            vals = (acc,) if epilogue is None else epilogue(acc, *[e[...] for e in ex_refs])
            for o_ref, v in zip(out_refs, vals):
                o_ref[...] = v.astype(o_ref.dtype)

        def compute():
            part = _dot(a_ref[...], b_ref[...], dn)
            if nk == 1:
                finish_tile(part)
                return

            @pl.when(k == 0)
            def _():
                acc_ref[...] = part

            @pl.when(jnp.logical_and(k > 0, k < nk - 1))
            def _():
                acc_ref[...] += part

            @pl.when(k == nk - 1)
            def _():
                finish_tile(acc_ref[...] + part)

        first = jnp.logical_and(jnp.logical_and(i == 0, j == 0), k == 0)
        last = jnp.logical_and(jnp.logical_and(i == ni - 1, j == nj - 1), k == nk - 1)
        step = (pl.program_id(0) * (ni if j_outer else nj) + pl.program_id(1)) * nk + k
        host.run(first, last, r_ins, r_outs, sems, compute, midway=step == (ni * nj * nk * 3) // 5)

    sem = ("arbitrary",) * 3 if rider else ("parallel", "parallel", "arbitrary")
    res = pl.pallas_call(
        body,
        name=name,
        grid=(nj, ni, nk) if j_outer else (ni, nj, nk),
        in_specs=[a_spec, b_spec] + [row_spec if e.shape[0] == 1 else tile_spec for e in extras] + host.in_specs,
        out_specs=out_specs + host.out_specs,
        out_shape=out_shape + host.out_shapes,
        scratch_shapes=([pltpu.VMEM((tm, tn), F32)] if n_acc else []) + host.scratch,
        compiler_params=_params(sem),
    )(a, b, *extras, *host.ins)
    return res[0] if len(res) == 1 else res


def _rms_fwd(name, x, g, tm=512, rider=None):
    t, d = x.shape
    tm = _tile(t, tm)
    n = t // tm
    host = _Host(rider)

    def body(*refs):
        x_ref, g_ref = refs[:2]
        r_ins = refs[2:2 + host.n_in]
        h_ref = refs[2 + host.n_in]
        r_outs = refs[3 + host.n_in:3 + host.n_in + host.n_out]
        sems = refs[3 + host.n_in + host.n_out:]
        i = pl.program_id(0)

        def compute():
            xf = x_ref[...]
            r = lax.rsqrt(jnp.mean(xf * xf, axis=-1, keepdims=True) + EPS)
            h_ref[...] = (xf * r * g_ref[...]).astype(h_ref.dtype)

        host.run(i == 0, i == n - 1, r_ins, r_outs, sems, compute, midway=i == (n * 3) // 5)

    res = pl.pallas_call(
        body,
        name=name,
        grid=(n,),
        in_specs=[pl.BlockSpec((tm, d), lambda i: (i, 0)), pl.BlockSpec((1, d), lambda i: (0, 0))] + host.in_specs,
        out_specs=[pl.BlockSpec((tm, d), lambda i: (i, 0))] + host.out_specs,
        out_shape=[jax.ShapeDtypeStruct((t, d), BF16)] + host.out_shapes,
        scratch_shapes=host.scratch,
        compiler_params=_params(("arbitrary",) if rider else ("parallel",)),
    )(x, g.reshape(1, d), *host.ins)
    return res[0] if len(res) == 1 else res


def _rms_bwd(name, dh, x, g, res=None, tm=256):
    t, d = x.shape
    tm = _tile(t, tm)
    has_res = res is not None

    def body(*refs):
        if has_res:
            dh_ref, x_ref, g_ref, res_ref, dx_ref, dxb_ref, gg_ref, ss_ref = refs
        else:
            dh_ref, x_ref, g_ref, dx_ref, dxb_ref, gg_ref, ss_ref = refs
        i = pl.program_id(0)
        xf = x_ref[...]
        r = lax.rsqrt(jnp.mean(xf * xf, axis=-1, keepdims=True) + EPS)
        xh = xf * r
        dhf = dh_ref[...].astype(F32)
        dxh = dhf * g_ref[...]
        dx = r * (dxh - xh * jnp.mean(dxh * xh, axis=-1, keepdims=True))

        @pl.when(i == 0)
        def _():
            gg_ref[...] = jnp.zeros_like(gg_ref)
            ss_ref[...] = jnp.zeros_like(ss_ref)

        if has_res:
            resf = res_ref[...]
            dx = dx + resf
            ss_ref[...] += jnp.sum(jnp.sum(resf * resf, axis=0, keepdims=True), axis=1, keepdims=True)
        dx_ref[...] = dx
        dxb_ref[...] = dx.astype(BF16)
        gg_ref[...] += jnp.sum(dhf * xh, axis=0, keepdims=True)

    row = pl.BlockSpec((tm, d), lambda i: (i, 0))
    vec = pl.BlockSpec((1, d), lambda i: (0, 0))
    one = pl.BlockSpec((1, 1), lambda i: (0, 0))
    ins = [dh, x, g.reshape(1, d)] + ([res] if has_res else [])
    dx, dxb, gg, ss = pl.pallas_call(
        body,
        name=name,
        grid=(t // tm,),
        in_specs=[row, row, vec] + ([row] if has_res else []),
        out_specs=[row, row, vec, one],
        out_shape=[jax.ShapeDtypeStruct((t, d), F32), jax.ShapeDtypeStruct((t, d), BF16), jax.ShapeDtypeStruct((1, d), F32),
                   jax.ShapeDtypeStruct((1, 1), F32)],
        compiler_params=_params(("arbitrary",)),
    )(*ins)
    return dx, dxb, gg.reshape(d), ss[0, 0]


def _head_rms(xf):
    r = lax.rsqrt(jnp.mean(xf * xf, axis=-1, keepdims=True) + EPS)
    return xf * r, r


def _head_rms_bwd(dy, xn, r, g):
    dxh = dy * g
    dx = r * (dxh - xn * jnp.mean(dxh * xn, axis=-1, keepdims=True))
    return dx, jnp.sum(dy * xn, axis=0, keepdims=True)


def _col_to_row(col):
    n = col.shape[0]
    eye = lax.broadcasted_iota(jnp.int32, (n, n), 0) == lax.broadcasted_iota(jnp.int32, (n, n), 1)
    return jnp.sum(jnp.where(eye, col, 0.0), axis=0, keepdims=True)


def _row_to_col(row):
    n = row.shape[1]
    eye = lax.broadcasted_iota(jnp.int32, (n, n), 0) == lax.broadcasted_iota(jnp.int32, (n, n), 1)
    return jnp.sum(jnp.where(eye, row, 0.0), axis=1, keepdims=True)


def _dproj_args(dproj, n_in):
    if dproj is None:
        return [], [], {}
    return [dproj], [ANY], {n_in: 0}


def _shift_down(u, s, rows):
    return jnp.where(rows >= s, pltpu.roll(u, s, axis=0), 0.0)


def _shift_up(u, s, rows, t):
    return jnp.where(rows < t - s, pltpu.roll(u, t - s, axis=0), 0.0)


def _conv_fwd(proj, off, conv_w, cb):
    t = proj.shape[0]
    c = conv_w.shape[1]
    blk0 = off // (3 * cb)

    def body(p_ref, w_ref, y_ref):
        rows = lax.broadcasted_iota(jnp.int32, (t, cb), 0)
        bg = p_ref[:, 0:cb].astype(F32)
        u = p_ref[:, cb:2 * cb].astype(F32) * p_ref[:, 2 * cb:3 * cb].astype(F32)
        w = w_ref[...]
        conv = w[2:3] * u + w[1:2] * _shift_down(u, 1, rows) + w[0:1] * _shift_down(u, 2, rows)
        y_ref[...] = (bg * conv).astype(y_ref.dtype)

    return pl.pallas_call(
        body,
        name="conv_fwd",
        grid=(c // cb,),
        in_specs=[pl.BlockSpec((t, 3 * cb), lambda j: (0, blk0 + j)), pl.BlockSpec((CONV_TAPS, cb), lambda j: (0, j))],
        out_specs=pl.BlockSpec((t, cb), lambda j: (0, j)),
        out_shape=jax.ShapeDtypeStruct((t, c), BF16),
        compiler_params=_params(("parallel",)),
    )(proj, conv_w)


def _conv_bwd(proj, off, conv_w, dy, cb, dproj, rider=None):
    t = proj.shape[0]
    c = conv_w.shape[1]
    blk0 = off // (3 * cb)
    nj = c // cb
    host = _Host(rider)

    def body(*refs):
        p_ref, w_ref, dy_ref = refs[:3]
        r_ins = refs[4:4 + host.n_in]
        dp_ref, gw_ref = refs[4 + host.n_in:6 + host.n_in]
        r_outs = refs[6 + host.n_in:6 + host.n_in + host.n_out]
        sems = refs[6 + host.n_in + host.n_out:]
        j = pl.program_id(0)

        def compute():
            rows = lax.broadcasted_iota(jnp.int32, (t, cb), 0)
            bg = p_ref[:, 0:cb].astype(F32)
            cg = p_ref[:, cb:2 * cb].astype(F32)
            v = p_ref[:, 2 * cb:3 * cb].astype(F32)
            u = cg * v
            w = w_ref[...]
            u1 = _shift_down(u, 1, rows)
            u2 = _shift_down(u, 2, rows)
            conv = w[2:3] * u + w[1:2] * u1 + w[0:1] * u2
            dyf = dy_ref[...].astype(F32)
            dconv = dyf * bg
            du = w[2:3] * dconv + w[1:2] * _shift_up(dconv, 1, rows, t) + w[0:1] * _shift_up(dconv, 2, rows, t)
            dp_ref[:, 0:cb] = (dyf * conv).astype(dp_ref.dtype)
            dp_ref[:, cb:2 * cb] = (du * v).astype(dp_ref.dtype)
            dp_ref[:, 2 * cb:3 * cb] = (du * cg).astype(dp_ref.dtype)
            gw_ref[0:1, :] = jnp.sum(dconv * u2, axis=0, keepdims=True)
            gw_ref[1:2, :] = jnp.sum(dconv * u1, axis=0, keepdims=True)
            gw_ref[2:3, :] = jnp.sum(dconv * u, axis=0, keepdims=True)

        host.run(j == 0, j == nj - 1, r_ins, r_outs, sems, compute)

    res = pl.pallas_call(
        body,
        name="conv_bwd",
        grid=(nj,),
        in_specs=[
            pl.BlockSpec((t, 3 * cb), lambda j: (0, blk0 + j)),
            pl.BlockSpec((CONV_TAPS, cb), lambda j: (0, j)),
            pl.BlockSpec((t, cb), lambda j: (0, j)),
            ANY,
        ] + host.in_specs,
        out_specs=[pl.BlockSpec((t, 3 * cb), lambda j: (0, blk0 + j)), pl.BlockSpec((CONV_TAPS, cb), lambda j: (0, j))] + host.out_specs,
        out_shape=[jax.ShapeDtypeStruct(dproj.shape, dproj.dtype), jax.ShapeDtypeStruct((CONV_TAPS, c), F32)] + host.out_shapes,
        input_output_aliases={3: 0},
        scratch_shapes=host.scratch,
        compiler_params=_params(("arbitrary",)),
    )(proj, conv_w, dy, dproj, *host.ins)
    return res


def _lane_scan(x, reverse):
    lane = lax.broadcasted_iota(jnp.int32, x.shape, 1)
    s = 1
    while s < LANES:
        if reverse:
            x = x + jnp.where(lane < LANES - s, pltpu.roll(x, LANES - s, axis=1), 0.0)
        else:
            x = x + jnp.where(lane >= s, pltpu.roll(x, s, axis=1), 0.0)
        s *= 2
    return x


def _scan_rows(src_ref, dst_ref, t, reverse, fn=None):
    groups = list(range(t // LANES))
    if reverse:
        groups = groups[::-1]
    carry = None
    for gi in groups:
        sl = slice(gi * LANES, (gi + 1) * LANES)
        blk = src_ref[:, sl]
        if fn is not None:
            blk = fn(blk)
        blk = _lane_scan(blk, reverse)
        if carry is not None:
            blk = blk + carry
        dst_ref[:, sl] = blk
        carry = blk[:, 0:1] if reverse else blk[:, LANES - 1:LANES]


def _forget_fwd(z_row, b_col):
    rows, t = z_row.shape

    def body(z_ref, b_ref, c_ref):
        def logf(z):
            zz = z + b_ref[...]
            return jnp.minimum(zz, 0.0) - jnp.log(1.0 + jnp.exp(-jnp.abs(zz)))

        _scan_rows(z_ref, c_ref, t, False, logf)

    return pl.pallas_call(
        body,
        name="forget_fwd",
        out_shape=jax.ShapeDtypeStruct((rows, t), F32),
        compiler_params=pltpu.CompilerParams(vmem_limit_bytes=VMEM_LIMIT),
    )(z_row, b_col)


def _rows_to_colb(c_row3, tq):
    heads, _, t = c_row3.shape

    def body(r_ref, o_ref):
        o_ref[...] = jnp.broadcast_to(_row_to_col(r_ref[...]), (tq, LANES))

    return pl.pallas_call(
        body,
        name="rows_to_colb",
        grid=(heads, t // tq),
        in_specs=[pl.BlockSpec((None, 1, tq), lambda h, i: (h, 0, i))],
        out_specs=pl.BlockSpec((None, tq, LANES), lambda h, i: (h, i, 0)),
        out_shape=jax.ShapeDtypeStruct((heads, t, LANES), F32),
        compiler_params=_params(("parallel", "parallel")),
    )(c_row3)


def _forget_bwd(z_row, b_col, dc_row):
    rows, t = z_row.shape

    def body(z_ref, b_ref, dc_ref, dz_ref, db_ref, tmp_ref):
        _scan_rows(dc_ref, tmp_ref, t, True)
        zz = z_ref[...] + b_ref[...]
        dz = tmp_ref[...] * (1.0 / (1.0 + jnp.exp(zz)))
        dz_ref[...] = dz.astype(dz_ref.dtype)
        db_ref[...] = jnp.sum(dz, axis=1, keepdims=True)

    return pl.pallas_call(
        body,
        name="forget_bwd",
        out_shape=[jax.ShapeDtypeStruct((rows, t), BF16), jax.ShapeDtypeStruct((rows, 1), F32)],
        scratch_shapes=[pltpu.VMEM((rows, t), F32)],
        compiler_params=pltpu.CompilerParams(vmem_limit_bytes=VMEM_LIMIT),
    )(z_row, b_col, dc_row)


def _fox_fwd(proj, off, gq, gk, c_row3, c_colb, heads, tq, rider=None):
    t = proj.shape[0]
    hd = FOX_HEAD_DIM
    tq = _tile(t, tq)
    nq = t // tq
    blk0 = off // hd
    scale = 1.0 / math.sqrt(hd)
    host = _Host(rider)

    def body(*refs):
        q_ref, k_ref, v_ref, gq_ref, gk_ref, crow_ref, ccol_ref = refs[:7]
        r_ins = refs[7:7 + host.n_in]
        o_ref, lse_ref = refs[7 + host.n_in:9 + host.n_in]
        r_outs = refs[9 + host.n_in:9 + host.n_in + host.n_out]
        khat_ref, v_t_ref = refs[9 + host.n_in + host.n_out:11 + host.n_in + host.n_out]
        sems = refs[11 + host.n_in + host.n_out:]
        h, qi = pl.program_id(0), pl.program_id(1)

        def compute():
            eye = (lax.broadcasted_iota(jnp.int32, (hd, hd), 0) == lax.broadcasted_iota(jnp.int32, (hd, hd), 1)).astype(BF16)

            @pl.when(qi == 0)
            def _():
                kn, _ = _head_rms(k_ref[...].astype(F32))
                khat_ref[...] = (kn * gk_ref[...]).astype(BF16)
                v_t_ref[...] = _dot(eye, v_ref[...], NT).astype(BF16)

            qn, _ = _head_rms(q_ref[...].astype(F32))
            qhat = (qn * (gq_ref[...] * scale)).astype(BF16)
            crow = crow_ref[:, pl.ds(pl.multiple_of(qi * tq, tq), tq)]
            above = lax.broadcasted_iota(jnp.int32, (tq, tq), 1) >= lax.broadcasted_iota(jnp.int32, (tq, tq), 0)

            def tile(j, keys, carry, diagonal):
                m, l, acc_t = carry
                ks = pl.multiple_of(j * keys, keys)
                s_t = _dot(khat_ref[pl.ds(ks, keys), :], qhat, NT) - ccol_ref[pl.ds(ks, keys), 0:1]
                if diagonal:
                    s_t = jnp.where(above, s_t, NEG)
                m_new = jnp.maximum(m, jnp.max(s_t, axis=0, keepdims=True) + crow)
                alpha = jnp.exp(m - m_new)
                p_t = jnp.exp(s_t + (crow - m_new))
                l = alpha * l + jnp.sum(p_t, axis=0, keepdims=True)
                acc_t = alpha * acc_t + _dot(v_t_ref[:, pl.ds(ks, keys)], p_t.astype(BF16), NN)
                return m_new, l, acc_t

            init = (jnp.full((1, tq), NEG, F32), jnp.zeros((1, tq), F32), jnp.zeros((hd, tq), F32))
            pairs = qi // 2 if 2 * tq <= MAX_KEYS else 0
            carry = lax.fori_loop(0, pairs, lambda j, c: tile(j, 2 * tq, c, False), init)
            carry = lax.fori_loop(2 * pairs, qi, lambda j, c: tile(j, tq, c, False), carry)
            m, l, acc_t = tile(qi, tq, carry, True)
            o_ref[...] = _dot((acc_t / l).astype(BF16), eye, TN).astype(o_ref.dtype)
            lse_ref[...] = m + jnp.log(l)

        first = jnp.logical_and(h == 0, qi == 0)
        last = jnp.logical_and(h == heads - 1, qi == nq - 1)
        host.run(first, last, r_ins, r_outs, sems, compute)

    res = pl.pallas_call(
        body,
        name="fox_fwd",
        grid=(heads, nq),
        in_specs=[
            pl.BlockSpec((tq, hd), lambda h, i: (i, blk0 + 3 * h)),
            pl.BlockSpec((t, hd), lambda h, i: (0, blk0 + 3 * h + 1)),
            pl.BlockSpec((t, hd), lambda h, i: (0, blk0 + 3 * h + 2)),
            pl.BlockSpec((1, hd), lambda h, i: (0, 0)),
            pl.BlockSpec((1, hd), lambda h, i: (0, 0)),
            pl.BlockSpec((None, 1, t), lambda h, i: (h, 0, 0)),
            pl.BlockSpec((None, t, LANES), lambda h, i: (h, 0, 0)),
        ] + host.in_specs,
        out_specs=[pl.BlockSpec((tq, hd), lambda h, i: (i, h)), pl.BlockSpec((None, 1, tq), lambda h, i: (h, 0, i))] + host.out_specs,
        out_shape=[jax.ShapeDtypeStruct((t, heads * hd), BF16), jax.ShapeDtypeStruct((heads, 1, t), F32)] + host.out_shapes,
        scratch_shapes=[pltpu.VMEM((t, hd), BF16), pltpu.VMEM((hd, t), BF16)] + host.scratch,
        compiler_params=_params(("arbitrary", "arbitrary")),
    )(proj, proj, proj, gq.reshape(1, hd), gk.reshape(1, hd), c_row3, c_colb, *host.ins)
    return res


def _fox_bwd(proj, off, o, do, gq, gk, c_row3, c_colb, lse, heads, tq, dproj, rider=None):
    t = proj.shape[0]
    hd = FOX_HEAD_DIM
    tq = _tile(t, tq)
    nb = t // tq
    blk0 = off // hd
    scale = 1.0 / math.sqrt(hd)
    host = _Host(rider)
    n_fixed_in = 11

    def body(*refs):
        q_ref, k_ref, v_ref, o_ref, do_ref, gq_ref, gk_ref, crow_ref, ccol_ref, lse_ref = refs[:10]
        pos = n_fixed_in
        r_ins = refs[pos:pos + host.n_in]; pos += host.n_in
        dp_ref, dc_ref, ggq_ref, ggk_ref = refs[pos:pos + 4]; pos += 4
        r_outs = refs[pos:pos + host.n_out]; pos += host.n_out
        qhat_ref, khat_ref, khat_t_ref, dq_t_ref, dk_ref, dcq_ref, dck_ref, delta_ref = refs[pos:pos + 8]; pos += 8
        sems = refs[pos:]
        h = pl.program_id(0)

        def compute():
            qn, rq = _head_rms(q_ref[...].astype(F32))
            qhat_ref[...] = (qn * (gq_ref[...] * scale)).astype(BF16)
            kn, rk = _head_rms(k_ref[...].astype(F32))
            khat_ref[...] = (kn * gk_ref[...]).astype(BF16)
            eye = (lax.broadcasted_iota(jnp.int32, (hd, hd), 0) == lax.broadcasted_iota(jnp.int32, (hd, hd), 1)).astype(BF16)
            khat_t_ref[...] = _dot(eye, khat_ref[...], NT).astype(BF16)
            delta = jnp.sum(do_ref[...].astype(F32) * o_ref[...].astype(F32), axis=-1, keepdims=True)
            for b in range(nb):
                sl = slice(b * tq, (b + 1) * tq)
                delta_ref[:, sl] = _col_to_row(delta[sl, :])
            dq_t_ref[...] = jnp.zeros_like(dq_t_ref)
            dcq_ref[...] = jnp.zeros_like(dcq_ref)
            above = lax.broadcasted_iota(jnp.int32, (tq, tq), 1) >= lax.broadcasted_iota(jnp.int32, (tq, tq), 0)

            def kv_block(j, _):
                ks = pl.multiple_of(j * tq, tq)
                kh = khat_ref[pl.ds(ks, tq), :]
                kh_t = khat_t_ref[:, pl.ds(ks, tq)]
                vv = v_ref[pl.ds(ks, tq), :]
                ccol = ccol_ref[pl.ds(ks, tq), 0:1]

                def q_block(i, n, carry, diagonal):
                    dk, dv, dck = carry
                    qs = pl.multiple_of(i * tq, tq)
                    qh = qhat_ref[pl.ds(qs, n), :]
                    dob = do_ref[pl.ds(qs, n), :]
                    s_t = _dot(kh, qh, NT) + ((crow_ref[:, pl.ds(qs, n)] - lse_ref[:, pl.ds(qs, n)]) - ccol)
                    p_t = jnp.exp(s_t)
                    if diagonal:
                        p_t = jnp.where(above, p_t, 0.0)
                    ds_t = p_t * (_dot(vv, dob, NT) - delta_ref[:, pl.ds(qs, n)])
                    dsb = ds_t.astype(BF16)
                    dv = dv + _dot(p_t.astype(BF16), dob, NN)
                    dk = dk + _dot(dsb, qh, NN)
                    dq_t_ref[:, pl.ds(qs, n)] += _dot(kh_t, dsb, NN)
                    dcq_ref[:, pl.ds(qs, n)] += jnp.sum(ds_t, axis=0, keepdims=True)
                    dck = dck + jnp.sum(ds_t, axis=-1, keepdims=True)
                    return dk, dv, dck

                zero = jnp.zeros((tq, hd), F32)
                carry = q_block(j, tq, (zero, zero, jnp.zeros((tq, 1), F32)), True)
                pairs = (nb - 1 - j) // 2 if 2 * tq <= MAX_KEYS else 0
                carry = lax.fori_loop(0, pairs, lambda p, c: q_block(j + 1 + 2 * p, 2 * tq, c, False), carry)
                dk, dv, dck = lax.fori_loop(j + 1 + 2 * pairs, nb, lambda i, c: q_block(i, tq, c, False), carry)
                dk_ref[pl.ds(ks, tq), :] = dk
                dp_ref[pl.ds(ks, tq), 2 * hd:3 * hd] = dv.astype(dp_ref.dtype)
                dck_ref[pl.ds(ks, tq), :] = dck
                return 0

            lax.fori_loop(0, nb, kv_block, 0)

            dq, ggq = _head_rms_bwd(dq_t_ref[...].T * scale, qn, rq, gq_ref[...])
            dk, ggk = _head_rms_bwd(dk_ref[...], kn, rk, gk_ref[...])
            dp_ref[:, 0:hd] = dq.astype(dp_ref.dtype)
            dp_ref[:, hd:2 * hd] = dk.astype(dp_ref.dtype)
            for b in range(nb):
                sl = slice(b * tq, (b + 1) * tq)
                dc_ref[:, sl] = dcq_ref[:, sl] - _col_to_row(dck_ref[sl, :])

            @pl.when(h == 0)
            def _():
                ggq_ref[...] = jnp.zeros_like(ggq_ref)
                ggk_ref[...] = jnp.zeros_like(ggk_ref)

            ggq_ref[...] += ggq
            ggk_ref[...] += ggk

        host.run(h == 0, h == heads - 1, r_ins, r_outs, sems, compute)

    head_in = lambda part: pl.BlockSpec((t, hd), lambda h: (0, blk0 + 3 * h + part))
    vec = pl.BlockSpec((1, hd), lambda h: (0, 0))
    colb = pl.BlockSpec((None, t, LANES), lambda h: (h, 0, 0))
    res = pl.pallas_call(
        body,
        name="fox_bwd",
        grid=(heads,),
        in_specs=[
            head_in(0), head_in(1), head_in(2),
            pl.BlockSpec((t, hd), lambda h: (0, h)),
            pl.BlockSpec((t, hd), lambda h: (0, h)),
            vec, vec,
            pl.BlockSpec((None, 1, t), lambda h: (h, 0, 0)),
            colb,
            pl.BlockSpec((None, 1, t), lambda h: (h, 0, 0)),
            ANY,
        ] + host.in_specs,
        out_specs=[
            pl.BlockSpec((t, 3 * hd), lambda h: (0, blk0 // 3 + h)),
            pl.BlockSpec((None, 1, t), lambda h: (h, 0, 0)),
            vec, vec,
        ] + host.out_specs,
        out_shape=[
            jax.ShapeDtypeStruct(dproj.shape, dproj.dtype),
            jax.ShapeDtypeStruct((heads, 1, t), F32),
            jax.ShapeDtypeStruct((1, hd), F32),
            jax.ShapeDtypeStruct((1, hd), F32),
        ] + host.out_shapes,
        input_output_aliases={10: 0},
        scratch_shapes=[
            pltpu.VMEM((t, hd), BF16), pltpu.VMEM((t, hd), BF16), pltpu.VMEM((hd, t), BF16),
            pltpu.VMEM((hd, t), F32), pltpu.VMEM((t, hd), F32),
            pltpu.VMEM((1, t), F32), pltpu.VMEM((t, 1), F32), pltpu.VMEM((1, t), F32),
        ] + host.scratch,
        compiler_params=_params(("arbitrary",)),
    )(proj, proj, proj, o, do, gq.reshape(1, hd), gk.reshape(1, hd), c_row3, c_colb, lse, dproj, *host.ins)
    return res


def _mem_fwd(proj, off, kv, gq, gk, tq):
    t = proj.shape[0]
    m, width = kv.shape[0], kv.shape[1] // 2
    hd = width // MEM_HEADS
    tq = _tile(t, tq)
    blk0 = off // hd
    scale = 1.0 / math.sqrt(hd)

    def body(q_ref, k_ref, v_ref, gq_ref, gk_ref, o_ref):
        qn, _ = _head_rms(q_ref[...].astype(F32))
        kn, _ = _head_rms(k_ref[...])
        s = _dot((qn * gq_ref[...]).astype(BF16), (kn * gk_ref[...]).astype(BF16), NT) * scale
        p = jnp.exp(s - jnp.max(s, axis=-1, keepdims=True))
        p = p / jnp.sum(p, axis=-1, keepdims=True)
        o_ref[...] = _dot(p.astype(BF16), v_ref[...].astype(BF16), NN).astype(o_ref.dtype)

    vec = pl.BlockSpec((1, hd), lambda h, i: (0, 0))
    return pl.pallas_call(
        body,
        name="mem_fwd",
        grid=(MEM_HEADS, t // tq),
        in_specs=[
            pl.BlockSpec((tq, hd), lambda h, i: (i, blk0 + h)),
            pl.BlockSpec((m, hd), lambda h, i: (0, h)),
            pl.BlockSpec((m, hd), lambda h, i: (0, MEM_HEADS + h)),
            vec, vec,
        ],
        out_specs=pl.BlockSpec((tq, hd), lambda h, i: (i, h)),
        out_shape=jax.ShapeDtypeStruct((t, width), BF16),
        compiler_params=_params(("parallel", "parallel")),
    )(proj, kv, kv, gq.reshape(1, hd), gk.reshape(1, hd))


def _mem_bwd(proj, off, kv, do, gq, gk, tq, dproj, rider=None):
    t = proj.shape[0]
    m, width = kv.shape[0], kv.shape[1] // 2
    hd = width // MEM_HEADS
    tq = _tile(t, tq)
    nq = t // tq
    blk0 = off // hd
    scale = 1.0 / math.sqrt(hd)
    host = _Host(rider)

    def body(*refs):
        q_ref, k_ref, v_ref, do_ref, gq_ref, gk_ref = refs[:6]
        pos = 7
        r_ins = refs[pos:pos + host.n_in]; pos += host.n_in
        dq_ref, dk_ref, dv_ref, ggq_ref, ggk_ref = refs[pos:pos + 5]; pos += 5
        r_outs = refs[pos:pos + host.n_out]; pos += host.n_out
        dkh_ref, dvh_ref = refs[pos:pos + 2]; pos += 2
        sems = refs[pos:]
        h, i = pl.program_id(0), pl.program_id(1)

        def compute():
            qn, rq = _head_rms(q_ref[...].astype(F32))
            kn, rk = _head_rms(k_ref[...])
            qhat = (qn * gq_ref[...]).astype(BF16)
            khat = (kn * gk_ref[...]).astype(BF16)
            vb = v_ref[...].astype(BF16)
            dob = do_ref[...]
            s = _dot(qhat, khat, NT) * scale
            p = jnp.exp(s - jnp.max(s, axis=-1, keepdims=True))
            p = p / jnp.sum(p, axis=-1, keepdims=True)
            dp = _dot(dob, vb, NT)
            ds = p * (dp - jnp.sum(dp * p, axis=-1, keepdims=True))
            dsb = ds.astype(BF16)
            dq, ggq = _head_rms_bwd(_dot(dsb, khat, NN) * scale, qn, rq, gq_ref[...])
            dq_ref[...] = dq.astype(dq_ref.dtype)

            @pl.when(i == 0)
            def _():
                dkh_ref[...] = jnp.zeros_like(dkh_ref)
                dvh_ref[...] = jnp.zeros_like(dvh_ref)

            @pl.when(jnp.logical_and(h == 0, i == 0))
            def _():
                ggq_ref[...] = jnp.zeros_like(ggq_ref)
                ggk_ref[...] = jnp.zeros_like(ggk_ref)

            dkh_ref[...] += _dot(dsb, qhat, TN)
            dvh_ref[...] += _dot(p.astype(BF16), dob, TN)
            ggq_ref[...] += ggq

            @pl.when(i == nq - 1)
            def _():
                dk, ggk = _head_rms_bwd(dkh_ref[...] * scale, kn, rk, gk_ref[...])
                dk_ref[...] = dk.astype(dk_ref.dtype)
                dv_ref[...] = dvh_ref[...].astype(dv_ref.dtype)
                ggk_ref[...] += ggk

        first = jnp.logical_and(h == 0, i == 0)
        last = jnp.logical_and(h == MEM_HEADS - 1, i == nq - 1)
        host.run(first, last, r_ins, r_outs, sems, compute)

    vec = pl.BlockSpec((1, hd), lambda h, i: (0, 0))
    kblk = pl.BlockSpec((m, hd), lambda h, i: (0, h))
    res = pl.pallas_call(
        body,
        name="mem_bwd",
        grid=(MEM_HEADS, nq),
        in_specs=[
            pl.BlockSpec((tq, hd), lambda h, i: (i, blk0 + h)), kblk,
            pl.BlockSpec((m, hd), lambda h, i: (0, MEM_HEADS + h)),
            pl.BlockSpec((tq, hd), lambda h, i: (i, h)), vec, vec, ANY,
        ] + host.in_specs,
        out_specs=[pl.BlockSpec((tq, hd), lambda h, i: (i, blk0 + h)), kblk, kblk, vec, vec] + host.out_specs,
        out_shape=[
            jax.ShapeDtypeStruct(dproj.shape, dproj.dtype),
            jax.ShapeDtypeStruct((m, width), BF16),
            jax.ShapeDtypeStruct((m, width), BF16),
            jax.ShapeDtypeStruct((1, hd), F32),
            jax.ShapeDtypeStruct((1, hd), F32),
        ] + host.out_shapes,
        input_output_aliases={6: 0},
        scratch_shapes=[pltpu.VMEM((m, hd), F32), pltpu.VMEM((m, hd), F32)] + host.scratch,
        compiler_params=_params(("arbitrary", "arbitrary")),
    )(proj, kv, kv, do, gq.reshape(1, hd), gk.reshape(1, hd), dproj, *host.ins)
    dproj, dk, dv, ggq, ggk = res[:5]
    return (dproj, jnp.concatenate([dk, dv], axis=1), ggq.reshape(hd), ggk.reshape(hd), *res[5:])


def _sigmoid(z):
    return 1.0 / (1.0 + jnp.exp(-z))


def _merge_fwd(proj, ys, ws, tm, tc):
    t, cw = ys[0].shape
    d = ws[0].shape[1]
    tm = _tile(t, tm)

    def body(g_ref, ya_ref, yb_ref, yc_ref, wa_ref, wb_ref, wc_ref, oa_ref, ob_ref, oc_ref, out_ref):
        acc = jnp.zeros((tm, tc), F32)
        for s, (y_ref, w_ref, o_ref) in enumerate(((ya_ref, wa_ref, oa_ref), (yb_ref, wb_ref, ob_ref), (yc_ref, wc_ref, oc_ref))):
            o = _dot(y_ref[...], w_ref[...], NN)
            o_ref[...] = o.astype(o_ref.dtype)
            acc = acc + _sigmoid(g_ref[:, s * tc:(s + 1) * tc].astype(F32)) * o
        out_ref[...] = acc.astype(out_ref.dtype)

    blk = pl.BlockSpec((tm, tc), lambda i, j: (i, j))
    y_spec = pl.BlockSpec((tm, cw), lambda i, j: (i, 0))
    w_spec = pl.BlockSpec((cw, tc), lambda i, j: (0, j))
    return pl.pallas_call(
        body,
        name="merge_fwd",
        grid=(t // tm, d // tc),
        in_specs=[pl.BlockSpec((tm, 3 * tc), lambda i, j: (i, j))] + [y_spec] * 3 + [w_spec] * 3,
        out_specs=[blk] * 4,
        out_shape=[jax.ShapeDtypeStruct((t, d), BF16)] * 4,
        compiler_params=_params(("parallel", "parallel")),
    )(proj, *ys, *ws)


def _merge_bwd(proj, o3, dx1, w_out, ws, tm, tc):
    t, d = o3[0].shape
    k = dx1.shape[1]
    cw = ws[0].shape[0]
    tm = _tile(t, tm)
    ni, nj = t // tm, d // tc

    def body(dx_ref, w_ref, g_ref, oa_ref, ob_ref, oc_ref, wa_ref, wb_ref, wc_ref,
             dg_ref, da_ref, db_ref, dc_ref, ya_ref, yb_ref, yc_ref, acc_ref):
        j = pl.program_id(1)
        dmf = _dot(dx_ref[...], w_ref[...], NT)
        branches = ((oa_ref, da_ref, wa_ref, ya_ref), (ob_ref, db_ref, wb_ref, yb_ref), (oc_ref, dc_ref, wc_ref, yc_ref))
        for s, (o_ref, do_ref, ws_ref, dy_ref) in enumerate(branches):
            g = _sigmoid(g_ref[:, s * tc:(s + 1) * tc].astype(F32))
            do = (dmf * g).astype(do_ref.dtype)
            do_ref[...] = do
            dg_ref[:, s * tc:(s + 1) * tc] = (dmf * o_ref[...].astype(F32) * g * (1.0 - g)).astype(dg_ref.dtype)
            part = _dot(do, ws_ref[...], NT)

            @pl.when(j == 0)
            def _():
                acc_ref[s] = part

            @pl.when(j > 0)
            def _():
                acc_ref[s] += part

            @pl.when(j == nj - 1)
            def _():
                dy_ref[...] = acc_ref[s].astype(dy_ref.dtype)

    blk = pl.BlockSpec((tm, tc), lambda i, j: (i, j))
    wide = pl.BlockSpec((tm, 3 * tc), lambda i, j: (i, j))
    w_spec = pl.BlockSpec((cw, tc), lambda i, j: (0, j))
    y_spec = pl.BlockSpec((tm, cw), lambda i, j: (i, 0))
    return pl.pallas_call(
        body,
        name="merge_bwd",
        grid=(ni, nj),
        in_specs=[pl.BlockSpec((tm, k), lambda i, j: (i, 0)), pl.BlockSpec((tc, k), lambda i, j: (j, 0)), wide, blk, blk, blk] + [w_spec] * 3,
        out_specs=[wide, blk, blk, blk] + [y_spec] * 3,
        out_shape=[jax.ShapeDtypeStruct(proj.shape, BF16)] + [jax.ShapeDtypeStruct((t, d), BF16)] * 3 + [jax.ShapeDtypeStruct((t, cw), BF16)] * 3,
        scratch_shapes=[pltpu.VMEM((3, tm, cw), F32)],
        compiler_params=_params(("parallel", "arbitrary")),
    )(dx1, w_out, proj, *o3, *ws)


def _w_in_chunks(d, tc):
    cw = d // 2
    heads = cw // FOX_HEAD_DIM
    conv0, fox0, f0, mq0, gate0 = 0, 3 * cw, 6 * cw, 6 * cw + heads, 7 * cw + heads
    chunks = [(gate0 + s * d + j * tc, gate0 + s * d + (j + 1) * tc) for j in range(d // tc) for s in range(N_BRANCHES)]
    chunks += [(conv0 + s * cw + j * LANES, conv0 + s * cw + (j + 1) * LANES) for j in range(cw // LANES) for s in range(3)]
    chunks += [(fox0 + s * cw + j * FOX_HEAD_DIM, fox0 + s * cw + (j + 1) * FOX_HEAD_DIM) for j in range(heads) for s in range(3)]
    chunks.append((mq0, mq0 + cw))
    return chunks, (f0, f0 + heads)


ROW_TILE = 16
GROUP = 128
GROUP_BACK = 112
SCRATCH_ROWS = 2 * GROUP + 32


def _padded_rows(r):
    return -(-r // GROUP_BACK) * GROUP_BACK


def _rows_from(scr_ref, use, q8, fine, g):
    x = scr_ref[pl.ds(pl.multiple_of(q8 * 8, 8), g + 8), :]
    for s in range(8):
        @pl.when(fine == s)
        def _(s=s):
            use((x if s == 0 else pltpu.roll(x, g + 8 - s, axis=0))[0:g])


def _assemble(name, tbl, grid, step, in_specs, out_spec, out_shape, operands, g, w1, cols_of):
    has_f = len(in_specs) == 3
    k = out_shape.shape[-1]
    c = cols_of

    def body(*refs):
        t_ref, s1_ref, s2_ref = refs[:3]
        f_ref = refs[3] if has_f else None
        out_ref = refs[3 + has_f]
        scr1, scr2, scrf = refs[4 + has_f:]
        t = step()

        def put(y):
            out_ref[...] = y.astype(out_ref.dtype)

        @pl.when(t == 0)
        def _():
            scr1[...] = jnp.zeros_like(scr1)
            scr2[...] = jnp.zeros_like(scr2)
            scrf[...] = jnp.zeros_like(scrf)

        rows = lax.broadcasted_iota(jnp.int32, (g, k), 0)
        n1, a2 = t_ref[c["n1"], t], t_ref[c["a2"], t]
        scr1[0:w1, :] = (s1_ref[0] if len(s1_ref.shape) == 3 else s1_ref[...]).astype(F32)
        _rows_from(scr1, put, t_ref[c["q1"], t], t_ref[c["s1"], t], g)

        @pl.when(a2 < g)
        def _():
            scr2[g:g + s2_ref.shape[0], :] = s2_ref[...].astype(F32)
            _rows_from(scr2, lambda y: put(jnp.where(rows < n1, out_ref[...].astype(F32), y)),
                       t_ref[c["q2"], t], t_ref[c["s2"], t], g)

        if has_f:
            fa, fb = t_ref[c["fa"], t], t_ref[c["fb"], t]

            @pl.when(fb > fa)
            def _():
                scrf[g:g + f_ref.shape[0], :] = f_ref[...].astype(F32)
                inside = jnp.logical_and(rows >= fa, rows < fb)
                _rows_from(scrf, lambda y: put(jnp.where(inside, y, out_ref[...].astype(F32))),
                           t_ref[c["qf"], t], t_ref[c["sf"], t], g)

            valid = t_ref[c["valid"], t]

            @pl.when(valid < g)
            def _():
                out_ref[...] = jnp.where(rows < valid, out_ref[...].astype(F32), 0.0).astype(out_ref.dtype)

    return pl.pallas_call(
        body,
        name=name,
        grid_spec=pltpu.PrefetchScalarGridSpec(
            num_scalar_prefetch=1, grid=grid, in_specs=in_specs, out_specs=out_spec,
            scratch_shapes=[pltpu.VMEM((SCRATCH_ROWS, k), F32)] * 3),
        out_shape=out_shape,
        compiler_params=_params(("arbitrary",) * len(grid)),
    )(jnp.asarray(tbl), *operands)


def _pack_w_in(w8, d, tc):
    blocks, rp, k = w8.shape
    chunks, (f_lo, f_hi) = _w_in_chunks(d, tc)
    r = max(hi for _, hi in chunks) // blocks
    g, w1 = GROUP, GROUP + ROW_TILE
    table = []
    for lo, hi in chunks:
        for g0 in range(lo, hi, g):
            b1, r1 = divmod(g0, r)
            n1 = min(g, r - r1)
            st1 = min(r1 // ROW_TILE * ROW_TILE, rp - w1)
            o1, o2 = r1 - st1, g - n1
            b2 = b1 + 1 if n1 < g else 0
            table.append((b1, st1, o1 // 8, o1 % 8, n1, n1, b2, o2 // 8, o2 % 8))
    names = ("b1", "st1", "q1", "s1", "n1", "a2", "b2", "q2", "s2")
    cols_of = {n: i for i, n in enumerate(names)}
    tbl = np.array(table, np.int32).T
    c = cols_of
    w_all = _assemble(
        "pack_w_in", tbl, (len(table),), lambda: pl.program_id(0),
        [pl.BlockSpec((pl.Element(1), pl.Element(w1), pl.Element(k)), lambda i, t: (t[c["b1"], i], pl.multiple_of(t[c["st1"], i], ROW_TILE), 0)),
         pl.BlockSpec((None, g, k), lambda i, t: (t[c["b2"], i], 0, 0))],
        pl.BlockSpec((g, k), lambda i, t: (i, 0)),
        jax.ShapeDtypeStruct((len(table) * g, k), w8.dtype), [w8, w8], g, w1, cols_of)
    fb, fr = divmod(f_lo, r)
    return w_all, jnp.pad(w8[fb, fr:fr + f_hi - f_lo], ((0, F_ROWS - (f_hi - f_lo)), (0, 0)))


def _unpack_g_in(g_all, g_f, d, tc, blocks):
    n_all, k = g_all.shape
    chunks, (f_lo, f_hi) = _w_in_chunks(d, tc)
    r = max(hi for _, hi in chunks) // blocks
    rp = _padded_rows(r)
    g, w1 = GROUP_BACK, GROUP_BACK + ROW_TILE
    pos, spans = 0, [(f_lo, f_hi, None)]
    for lo, hi in chunks:
        spans.append((lo, hi, pos))
        pos += hi - lo
    spans.sort()
    table = []
    for b in range(blocks):
        for l0 in range(0, rp, g):
            valid = max(0, min(g, r - l0))
            g0, segs, fa, fb, of = b * r + l0, [], 0, 0, 0
            for lo, hi, p in spans:
                a, e = max(lo, g0), min(hi, g0 + valid)
                if a < e and p is None:
                    fa, fb, of = a - g0, e - g0, g + (a - lo) - (a - g0)
                elif a < e:
                    segs.append((a - g0, p + a - lo, e - a))
            assert len(segs) <= 2 and (not segs or segs[0][0] == 0 or len(segs) == 1)
            first = segs[0] if segs and segs[0][0] == 0 else (0, 0, 0)
            second = segs[-1] if segs and segs[-1][0] > 0 else (g, 0, 0)
            st1 = min(first[1] // ROW_TILE * ROW_TILE, n_all - w1)
            o1, o2 = first[1] - st1, g - second[0]
            assert second[1] % GROUP == 0
            table.append((st1, o1 // 8, o1 % 8, first[2], second[0], second[1] // GROUP, o2 // 8, o2 % 8,
                          fa, fb, of // 8, of % 8, valid))
    names = ("st1", "q1", "s1", "n1", "a2", "j2", "q2", "s2", "fa", "fb", "qf", "sf", "valid")
    cols_of = {n: i for i, n in enumerate(names)}
    tbl = np.array(table, np.int32).T
    c, per = cols_of, rp // g
    return _assemble(
        "unpack_g_in", tbl, (blocks, per), lambda: pl.program_id(0) * per + pl.program_id(1),
        [pl.BlockSpec((pl.Element(w1), pl.Element(k)), lambda b, u, t: (pl.multiple_of(t[c["st1"], b * per + u], ROW_TILE), 0)),
         pl.BlockSpec((GROUP, k), lambda b, u, t: (t[c["j2"], b * per + u], 0)),
         pl.BlockSpec((F_ROWS, k), lambda b, u, t: (0, 0))],
        pl.BlockSpec((None, g, k), lambda b, u, t: (b, u, 0)),
        jax.ShapeDtypeStruct((blocks, rp, k), g_all.dtype), [g_all, g_all, g_f], g, w1, cols_of)


def _unblock(w8):
    return w8.transpose(1, 0, 2).reshape(w8.shape[1], -1)


def _tile2(r, cols, tr, tcols):
    if r % 8 == 0:
        return _tile(r, tr), cols
    return r, _tile(cols, tcols)


def _pair_sum(name, g8, got, c, halves=1):
    _, r, cols = g8.shape
    rows = r // halves
    tr, tcols = _tile2(rows, cols, 256, 256)
    per = rows // tr
    out = []
    for part in range(halves):
        def body(c_ref, g_ref, s_ref, o_ref):
            o_ref[...] = (g_ref[...].astype(F32) + s_ref[...].astype(F32)).astype(o_ref.dtype)

        own = pl.BlockSpec((None, tr, tcols), lambda q, i, j, c_ref, part=part: (2 * q + c_ref[0], part * per + i, j))
        src = pl.BlockSpec((None, tr, tcols), lambda q, i, j, c_ref, part=part: (q, part * per + i, j))
        dst = pl.BlockSpec((None, tr, tcols), lambda q, i, j, c_ref: (q, i, j))
        out.append(pl.pallas_call(
            body,
            name=name if halves == 1 else f"{name}_{part}",
            grid_spec=pltpu.PrefetchScalarGridSpec(num_scalar_prefetch=1, grid=(N_CHIPS, per, cols // tcols),
                                                   in_specs=[own, src], out_specs=dst),
            out_shape=jax.ShapeDtypeStruct((N_CHIPS, rows, cols), BF16),
            compiler_params=_params(("parallel",) * 3),
        )(c, g8, got))
    return out[0] if halves == 1 else out


def _local_step(x, mem, target, w, small, comm=None):
    t, d = x.shape
    cw = d // 2
    heads = cw // FOX_HEAD_DIM
    tc = min(512, d)
    tq = min(512, t)
    off_conv, off_fox, off_mq = 3 * d, 3 * d + 3 * cw, 3 * d + 6 * cw
    w, small = dict(w), dict(small)
    big = dict(tm=1024, tn=512, tk=2048)
    wide_k = dict(tm=512, tn=1024, tk=4096)
    tall = dict(tm=2048, tn=512, tk=2048)

    if comm:
        first = _gather_rider([comm["shards"]["w_in"], comm["conv_w"]], True)
        h, w["w_in"], cw8 = _rms_fwd("rms1_fwd", x, small["norm1_g"], rider=first)
        small["conv_w"] = _unblock(cw8)
        w_all, w_f = _pack_w_in(w["w_in"], d, tc)
        early = ("w_conv_out", "w_fox_out", "w_mem_out", "w_out", "w_mem_kv", "w_down")
        proj, *got = _matmul("proj", "nt", h, w_all, outs=[BF16], rider=_gather_rider([comm["shards"][n] for n in early], True), **tall)
        for n, val in zip(early, got):
            w[n] = _unblock(val) if n in COLUMN_SPLIT else val.reshape(-1, val.shape[-1])
    else:
        h = _rms_fwd("rms1_fwd", x, small["norm1_g"])
        w_all, w_f = _pack_w_in(w["w_in"], d, tc)
        proj = _matmul("proj", "nt", h, w_all, outs=[BF16], **tall)
    z_row = _matmul("proj_f", "nt", w_f, h, outs=[F32], tm=F_ROWS, tn=512, tk=2048)

    y_conv = _conv_fwd(proj, off_conv, small["conv_w"], LANES)

    b_col = jnp.pad(small["b_f"], (0, F_ROWS - heads)).reshape(F_ROWS, 1)
    c_row3 = _forget_fwd(z_row, b_col)[:heads].reshape(heads, 1, t)
    c_colb = _rows_to_colb(c_row3, tq)
    if comm:
        y_fox, lse, got = _fox_fwd(proj, off_fox, small["fox_q_g"], small["fox_k_g"], c_row3, c_colb, heads, 2 * tq,
                                   rider=_gather_rider([comm["shards"]["w_up"]], False))
        w["w_up"] = _unblock(got)
    else:
        y_fox, lse = _fox_fwd(proj, off_fox, small["fox_q_g"], small["fox_k_g"], c_row3, c_colb, heads, 2 * tq)

    nm = _rms_fwd("mem_rms_fwd", mem, small["mem_norm_g"])
    kv = _matmul("mem_kv", "nn", nm, w["w_mem_kv"], outs=[F32], tm=256, tn=512, tk=2048)
    y_mem = _mem_fwd(proj, off_mq, kv, small["mem_q_g"], small["mem_k_g"], tq)

    ys = (y_conv, y_fox, y_mem)
    w_outs = (w["w_conv_out"], w["w_fox_out"], w["w_mem_out"])
    *o3, merged = _merge_fwd(proj, ys, w_outs, 1024, tc)
    def out_epilogue(acc, xr, g2):
        x1r = acc + xr
        r = lax.rsqrt(jnp.mean(x1r * x1r, axis=-1, keepdims=True) + EPS)
        return x1r, x1r * r * g2

    x1, h2 = _matmul("out_proj", "nn", merged, w["w_out"], outs=[F32, BF16], extras=[x, small["norm2_g"].reshape(1, d)],
                     epilogue=out_epilogue, tm=512, tn=d, tk=2048)

    def up_epilogue(acc):
        return acc, jnp.square(jnp.maximum(acc, 0.0))

    up, act = _matmul("mlp_up", "nn", h2, w["w_up"], outs=[BF16, BF16], epilogue=up_epilogue, **big)

    def loss_epilogue(acc, x1r, tr):
        dy = (acc + x1r - tr) * (1.0 / d)
        return dy, dy

    dy, dyb = _matmul("mlp_down", "nn", act, w["w_down"], outs=[F32, BF16], extras=[x1, target],
                      epilogue=loss_epilogue, tm=1024, tn=512, tk=4096)

    def dup_epilogue(acc, upr):
        return (acc * 2.0 * jnp.maximum(upr.astype(F32), 0.0),)

    def by_owner(g):
        return g.reshape(N_DEV, -1, g.shape[-1])

    g, parts = {}, {}
    g["w_down"] = _matmul("d_w_down", "tn", act, dyb, outs=[BF16], **wide_k)
    if comm:
        dup = _matmul("d_act", "nt", dyb, w["w_down"], outs=[BF16], extras=[up], epilogue=dup_epilogue, **tall)
        g["w_up"], got = _matmul("d_w_up", "tn", h2, dup, outs=[BF16], out_blocks=True,
                                 rider=_pair_rider([by_owner(g["w_down"])]), **wide_k)
        pair = _pair_sum("pair_w_down", by_owner(g["w_down"]), got, comm["c"])
        dh2, parts["w_down"], got = _matmul("d_h2", "nt", dup, w["w_up"], outs=[F32],
                                            rider=_join_riders(_chip_rider([pair]), _pair_rider([g["w_up"]])), **tall)
        pair_up = _pair_sum("pair_w_up", g["w_up"], got, comm["c"])
    else:
        dup = _matmul("d_act", "nt", dyb, w["w_down"], outs=[BF16], extras=[up], epilogue=dup_epilogue, **tall)
        g["w_up"] = _matmul("d_w_up", "tn", h2, dup, outs=[BF16], out_blocks=True, **wide_k)
        dh2 = _matmul("d_h2", "nt", dup, w["w_up"], outs=[F32], **tall)
    dx1, dx1b, g_norm2, dy_sq = _rms_bwd("rms2_bwd", dh2, x1, small["norm2_g"], res=dy)
    loss = dy_sq * (0.5 * d)

    g["w_out"] = _matmul("d_w_out", "tn", merged, dx1b, outs=[BF16], **wide_k)
    dproj, *rest = _merge_bwd(proj, o3, dx1b, w["w_out"], w_outs, 512, tc)
    do3, dys = rest[:3], rest[3:]
    names = ("w_conv_out", "w_fox_out", "w_mem_out")
    for s in range(3):
        g[names[s]] = _matmul(f"d_w_branch{s}", "tn", ys[s], do3[s], outs=[BF16], out_blocks=True, **wide_k)

    dproj, dkv, g_mq, g_mk = _mem_bwd(proj, off_mq, kv, dys[2], small["mem_q_g"], small["mem_k_g"], tq, dproj)
    g["w_mem_kv"] = _matmul("d_w_mem_kv", "tn", nm, dkv, outs=[BF16], **wide_k)
    dnm = _matmul("d_mem_norm", "nt", dkv, w["w_mem_kv"], outs=[F32], tm=256, tn=512, tk=2048)
    _, _, g_mem_norm, _ = _rms_bwd("mem_rms_bwd", dnm, mem, small["mem_norm_g"])

    mid = ("w_out", "w_conv_out", "w_fox_out", "w_mem_out", "w_mem_kv")
    if comm:
        mid8 = [g[n] if n in names else by_owner(g[n]) for n in mid]
        dproj, g_conv_w, *got = _conv_bwd(proj, off_conv, small["conv_w"], dys[0], LANES, dproj, rider=_pair_rider(mid8))
        pairs_mid = [_pair_sum("pair_" + n, g8, s4, comm["c"]) for n, g8, s4 in zip(mid, mid8, got)]
        dproj, dc, g_fq, g_fk, parts["w_up"] = _fox_bwd(proj, off_fox, y_fox, dys[1], small["fox_q_g"], small["fox_k_g"], c_row3,
                                                        c_colb, lse, heads, tq, dproj, rider=_chip_rider([pair_up]))
    else:
        dproj, g_conv_w = _conv_bwd(proj, off_conv, small["conv_w"], dys[0], LANES, dproj)
        dproj, dc, g_fq, g_fk = _fox_bwd(proj, off_fox, y_fox, dys[1], small["fox_q_g"], small["fox_k_g"], c_row3, c_colb,
                                         lse, heads, tq, dproj)
    dc_row = jnp.pad(dc.reshape(heads, t), ((0, F_ROWS - heads), (0, 0)))
    dz_row, db = _forget_bwd(z_row, b_col, dc_row)

    if comm:
        g_all, *got = _matmul("d_w_in", "tn", dproj, h, outs=[BF16], j_outer=True, rider=_chip_rider(pairs_mid), **wide_k)
        parts.update(zip(mid, got))
    else:
        g_all = _matmul("d_w_in", "tn", dproj, h, outs=[BF16], j_outer=True, **wide_k)
    g_wf = _matmul("d_w_f", "nn", dz_row, h, outs=[BF16], tm=F_ROWS, tn=512, tk=4096)
    g["w_in"] = _unpack_g_in(g_all, g_wf, d, tc, w["w_in"].shape[0])
    dh = _matmul("d_h_f", "tn", dz_row, w_f, outs=[F32], tm=1024, tn=512, tk=F_ROWS)
    add_prev = lambda acc, prev: (acc + prev,)
    if comm:
        g_in8 = g["w_in"]
        got = _run_rider("pair_exchange_w_in", _pair_rider([g_in8]))[0]
        pair = _pair_sum("pair_w_in", g_in8, got, comm["c"])
        dh, parts["w_in"] = _matmul("d_h", "nn", dproj, w_all, outs=[F32], extras=[dh], epilogue=add_prev,
                                    rider=_chip_rider([pair]), tm=1024, tn=512, tk=3328)
    else:
        dh = _matmul("d_h", "nn", dproj, w_all, outs=[F32], extras=[dh], epilogue=add_prev, tm=1024, tn=512, tk=3328)
    grad_x, _, g_norm1, _ = _rms_bwd("rms1_bwd", dh, x, small["norm1_g"], res=dx1)

    gs = dict(norm1_g=g_norm1, b_f=db[:heads, 0], conv_w=g_conv_w, fox_q_g=g_fq.reshape(-1), fox_k_g=g_fk.reshape(-1),
              mem_norm_g=g_mem_norm, mem_q_g=g_mq, mem_k_g=g_mk, norm2_g=g_norm2)
    return loss, grad_x, (parts if comm else g), gs


def _adamw_math(w, g, m, v):
    m = ADAM_B1 * m + (1.0 - ADAM_B1) * g
    v = ADAM_B2 * v + (1.0 - ADAM_B2) * jnp.square(g)
    m_hat = m / (1.0 - ADAM_B1 ** ADAM_STEP)
    v_hat = v / (1.0 - ADAM_B2 ** ADAM_STEP)
    delta = -ADAM_LR * (m_hat / (jnp.sqrt(v_hat) + ADAM_EPS) + ADAM_WD * w)
    return delta, m, v


def _adamw(name, parts, w, m, v):
    r, c = w.shape
    pieces = list(parts) if isinstance(parts, (list, tuple)) else [parts]
    n_parts = pieces[0].shape[0]
    rp = sum(p.shape[1] for p in pieces)
    if rp == r:
        tr, tc = _tile2(pieces[0].shape[1], c, 128, 256)
    else:
        tr, tc = _tile(rp, 256), _tile(c, 1024)
    per = pieces[0].shape[1] // tr
    n_p = len(pieces)

    def body(*refs):
        p_refs = refs[:n_p]
        w_ref, m_ref, v_ref, g_ref, d_ref, nm_ref, nv_ref = refs[n_p:]
        i = pl.program_id(0)
        g = None
        for a, p_ref in enumerate(p_refs):
            s_a = p_ref[0].astype(F32)
            for s in range(1, n_parts):
                s_a = s_a + p_ref[s].astype(F32)
            g = s_a if g is None else jnp.where(i >= a * per, s_a, g)
        delta, nm, nv = _adamw_math(w_ref[...], g, m_ref[...], v_ref[...])
        g_ref[...] = g
        d_ref[...] = delta
        nm_ref[...] = nm
        nv_ref[...] = nv

    blk = pl.BlockSpec((tr, tc), lambda i, j: (i, j))
    p_specs = [pl.BlockSpec((n_parts, tr, tc), lambda i, j, a=a: (0, jnp.clip(i - a * per, 0, per - 1), j)) for a in range(n_p)]
    return pl.pallas_call(
        body,
        name=name,
        grid=(rp // tr, c // tc),
        in_specs=p_specs + [blk, blk, blk],
        out_specs=[blk] * 4,
        out_shape=[jax.ShapeDtypeStruct((r, c), F32)] * 4,
        compiler_params=_params(("parallel", "parallel")),
    )(*pieces, w, m, v)


def _sum_parts(name, parts):
    n_parts, r, c = parts.shape

    def body(p_ref, o_ref):
        acc = p_ref[0]
        for s in range(1, n_parts):
            acc = acc + p_ref[s]
        o_ref[...] = acc

    return pl.pallas_call(body, name=name, out_shape=jax.ShapeDtypeStruct((r, c), F32))(parts)


BIG = ("w_in", "w_mem_kv", "w_conv_out", "w_fox_out", "w_mem_out", "w_out", "w_up", "w_down")
COLUMN_SPLIT = ("w_in", "w_conv_out", "w_fox_out", "w_mem_out", "w_up")
SMALL = ("norm1_g", "b_f", "conv_w", "fox_q_g", "fox_k_g", "mem_norm_g", "mem_q_g", "mem_k_g", "norm2_g")
WEIGHTS = ("norm1_g", "w_in", "b_f", "conv_w", "fox_q_g", "fox_k_g", "mem_norm_g", "w_mem_kv", "mem_q_g", "mem_k_g",
           "w_conv_out", "w_fox_out", "w_mem_out", "w_out", "norm2_g", "w_up", "w_down")


def _pack(vectors):
    rows = []
    for vec in vectors:
        n = vec.shape[0]
        rows.append(jnp.pad(vec, (0, -n % LANES)).reshape(-1, LANES))
    out = jnp.concatenate(rows, axis=0)
    return jnp.pad(out, ((0, -out.shape[0] % 8), (0, 0)))


def _unpack(packed, sizes):
    out, row = [], 0
    for n in sizes:
        nr = -(-n // LANES)
        out.append(packed[row:row + nr].reshape(-1)[:n])
        row += nr
    return out


def kernel(x, mem, norm1_g, w_in, b_f, conv_w, fox_q_g, fox_k_g, mem_norm_g, w_mem_kv, mem_q_g, mem_k_g, w_conv_out, w_fox_out, w_mem_out, w_out, norm2_g, w_up, w_down, loss_target, m_norm1_g, m_w_in, m_b_f, m_conv_w, m_fox_q_g, m_fox_k_g, m_mem_norm_g, m_w_mem_kv, m_mem_q_g, m_mem_k_g, m_w_conv_out, m_w_fox_out, m_w_mem_out, m_w_out, m_norm2_g, m_w_up, m_w_down, v_norm1_g, v_w_in, v_b_f, v_conv_w, v_fox_q_g, v_fox_k_g, v_mem_norm_g, v_w_mem_kv, v_mem_q_g, v_mem_k_g, v_w_conv_out, v_w_fox_out, v_w_mem_out, v_w_out, v_norm2_g, v_w_up, v_w_down):
    args = dict(locals())
    wts = {n: args[n] for n in WEIGHTS}
    ms = {n: args["m_" + n] for n in WEIGHTS}
    vs = {n: args["v_" + n] for n in WEIGHTS}
    x_pos, y_pos, c_pos = _position()
    me = _index(x_pos, y_pos, c_pos)

    shards = {n: wts[n].astype(BF16) for n in BIG if n != "w_in"}
    rows_in = w_in.shape[1]
    shards["w_in"] = jnp.pad(w_in.T.astype(BF16), ((0, _padded_rows(rows_in) - rows_in), (0, 0)))
    small = {n: wts[n] for n in SMALL if n != "conv_w"}
    comm = {"shards": shards, "conv_w": conv_w, "c": c_pos.astype(jnp.int32).reshape(1)}

    loss, grad_x, parts, gs = _local_step(x[0], mem[0], loss_target[0], {}, small, comm)

    out_g, out_d, out_m, out_v = {}, {}, {}, {}
    for n in BIG:
        if n == "w_in":
            res = _adamw("adamw_" + n, parts[n], wts[n].T, ms[n].T, vs[n].T)
            out_g[n], out_d[n], out_m[n], out_v[n] = (r.T for r in res)
        else:
            out_g[n], out_d[n], out_m[n], out_v[n] = _adamw("adamw_" + n, parts[n], wts[n], ms[n], vs[n])

    small_sizes = [int(math.prod(gs[n].shape)) for n in SMALL]
    packed = _pack([gs[n].reshape(-1) for n in SMALL])
    gsum = _sum_parts("sum_small", _run_rider("exchange_small", _broadcast_rider([packed]))[0])
    gsmall = dict(zip(SMALL, _unpack(gsum, small_sizes)))
    cols = conv_w.shape[1]
    gsmall["conv_w"] = lax.dynamic_slice(gsmall["conv_w"].reshape(CONV_TAPS, -1), (0, me * cols), (CONV_TAPS, cols)).reshape(-1)
    pg, pw, pm, pv = (_pack([src[n].reshape(-1) for n in SMALL]) for src in (gsmall, wts, ms, vs))
    _, sd, sm, sv = _adamw("adamw_small", pg[None], pw, pm, pv)
    local_sizes = [int(math.prod(wts[n].shape)) for n in SMALL]
    for dst, src in ((out_d, sd), (out_m, sm), (out_v, sv)):
        for n, val in zip(SMALL, _unpack(src, local_sizes)):
            dst[n] = val.reshape(wts[n].shape)
    for n in SMALL:
        out_g[n] = gsmall[n].reshape(wts[n].shape)

    loss = lax.psum(loss, MESH_AXES)
    return (loss, grad_x[None], *[out_g[n] for n in WEIGHTS], *[out_d[n] for n in WEIGHTS],
            *[out_m[n] for n in WEIGHTS], *[out_v[n] for n in WEIGHTS])
```

```python
import math

import numpy as np
import jax
import jax.numpy as jnp
from jax import lax
from jax.experimental import pallas as pl
from jax.experimental.pallas import tpu as pltpu

F32 = jnp.float32
BF16 = jnp.bfloat16

EPS = 1e-6
N_DEV = 8
N_CHIPS = 4
FOX_HEAD_DIM = 128
MEM_HEADS = 4
CONV_TAPS = 3
N_BRANCHES = 3
F_ROWS = 16

ADAM_LR = 0.001
ADAM_B1 = 0.9
ADAM_B2 = 0.999
ADAM_EPS = 1e-08
ADAM_WD = 0.01
ADAM_STEP = 10

V7X_VMEM_BYTES = 64 * 1024 * 1024
VMEM_LIMIT = V7X_VMEM_BYTES * 3 // 4
LANES = 128
NEG = -1e30
MAX_KEYS = 1024

MESH_AXES = ("x", "y", "c")
MESH = pl.DeviceIdType.MESH
ANY = pl.BlockSpec(memory_space=pl.ANY)

NN = (((1,), (0,)), ((), ()))
NT = (((1,), (1,)), ((), ()))
TN = (((0,), (0,)), ((), ()))


def _params(sem):
    return pltpu.CompilerParams(dimension_semantics=sem, vmem_limit_bytes=VMEM_LIMIT)


def _dot(a, b, dn):
    return lax.dot_general(a, b, dn, preferred_element_type=F32)


def _tile(n, t):
    if n <= t:
        return n
    for step in (LANES, 16):
        for cand in range(t - t % step, 0, -step):
            if n % cand == 0:
                return cand
    raise ValueError((n, t))


class _Rider:
    def __init__(self, ins, out_shapes, sem_shapes, start, finish, middle=None):
        self.ins, self.out_shapes, self.sem_shapes = list(ins), list(out_shapes), list(sem_shapes)
        self.start, self.finish, self.middle = start, finish, middle


def _position():
    return lax.axis_index("x"), lax.axis_index("y"), lax.axis_index("c")


def _index(px, py, pc):
    return 4 * px + 2 * py + pc


def _dma_sems(n, per):
    return [pltpu.SemaphoreType.DMA((n, per)), pltpu.SemaphoreType.DMA((n, per)), pltpu.SemaphoreType.DMA((n,))]


def _gather_rider(shards, pass_on):
    n = len(shards)

    def copies(ins, outs, sems):
        send_sems, recv_sems, local_sems = sems
        x, y, c = _position()
        me, sibling = (x, y, c), (x, y, 1 - c)
        chips = [(1 - x, y), (x, 1 - y), (1 - x, 1 - y)]

        def copy(a, k, block, to, src=None, k_send=None):
            rows = outs[a].at[_index(*block)]
            return pltpu.make_async_remote_copy(
                src_ref=rows if src is None else src, dst_ref=rows,
                send_sem=send_sems.at[a, k if k_send is None else k_send], recv_sem=recv_sems.at[a, k],
                device_id=to, device_id_type=MESH)

        mine = [pltpu.make_async_copy(ins[a], outs[a].at[_index(*me)], local_sems.at[a]) for a in range(n)]
        first = []
        for a in range(n):
            first.append(copy(a, 0, me, sibling, src=ins[a]))
            first += [copy(a, 1 + j, me, (*chips[j], c), src=ins[a]) for j in range(2 if pass_on else 3)]
        return copy, mine, first, me, sibling, chips, c

    def start(ins, outs, sems):
        _, mine, first, *_ = copies(ins, outs, sems)
        for cp in mine + first:
            cp.start()

    def by_kind(c, fn):
        if pass_on:
            pl.when(c == 1)(lambda: fn(0, 1))
            pl.when(c == 0)(lambda: fn(1, 0))
        else:
            fn(0, 1)

    def onward(copy, a, j_on, j_to, chips, c, sibling):
        third = [copy(a, 3, (*chips[j_on], c), (*chips[j_to], c), k_send=7)] if pass_on else []
        return third + [copy(a, 4 + j_on, (*chips[j_on], c), sibling)], [copy(a, 4 + j_to, (*chips[j_to], c), sibling)]

    def middle(ins, outs, sems):
        copy, _, _, me, sibling, chips, c = copies(ins, outs, sems)

        def fn(j_on, j_to):
            for a in range(n):
                for j, after in zip((j_on, j_to), onward(copy, a, j_on, j_to, chips, c, sibling)):
                    copy(a, 1 + j, (*chips[j], c), me).wait_recv()
                    for cp in after:
                        cp.start()

        by_kind(c, fn)

    def finish(ins, outs, sems):
        copy, mine, first, me, sibling, chips, c = copies(ins, outs, sems)

        def fn(j_on, j_to):
            passed = [cp for a in range(n) for after in onward(copy, a, j_on, j_to, chips, c, sibling) for cp in after]
            for a in range(n):
                copy(a, 3, (*chips[2], c), me).wait_recv()
                passed.append(copy(a, 6, (*chips[2], c), sibling))
                passed[-1].start()
            for a in range(n):
                copy(a, 0, sibling, me).wait_recv()
                for j, chip in enumerate(chips):
                    copy(a, 4 + j, (*chip, 1 - c), me).wait_recv()
            for cp in first + passed:
                cp.wait_send()
            for cp in mine:
                cp.wait()

        by_kind(c, fn)

    out_shapes = [jax.ShapeDtypeStruct((N_DEV,) + s.shape, s.dtype) for s in shards]
    return _Rider(shards, out_shapes, _dma_sems(n, 8), start, finish, middle)


def _pair_rider(grads):
    n = len(grads)

    def copies(ins, outs, sems):
        send_sems, recv_sems, _ = sems
        x, y, c = _position()
        return [pltpu.make_async_remote_copy(
            src_ref=ins[a].at[2 * q + (1 - c)], dst_ref=outs[a].at[q],
            send_sem=send_sems.at[a, q], recv_sem=recv_sems.at[a, q], device_id=(x, y, 1 - c), device_id_type=MESH)
            for a in range(n) for q in range(N_CHIPS)]

    def start(ins, outs, sems):
        for cp in copies(ins, outs, sems):
            cp.start()

    def finish(ins, outs, sems):
        cps = copies(ins, outs, sems)
        for cp in cps:
            cp.wait_recv()
        for cp in cps:
            cp.wait_send()

    out_shapes = [jax.ShapeDtypeStruct((N_CHIPS,) + g.shape[1:], g.dtype) for g in grads]
    return _Rider(grads, out_shapes, _dma_sems(n, N_CHIPS), start, finish)


def _chip_rider(parts):
    n = len(parts)

    def copies(ins, outs, sems):
        send_sems, recv_sems, local_sems = sems
        x, y, c = _position()
        q_me = 2 * x + y
        chips = [(1 - x, y), (x, 1 - y), (1 - x, 1 - y)]
        mine = [pltpu.make_async_copy(ins[a].at[q_me], outs[a].at[q_me], local_sems.at[a]) for a in range(n)]
        sends, arrivals = [], []
        for a in range(n):
            for j, (tx, ty) in enumerate(chips):
                q_t = 2 * tx + ty
                sends.append(pltpu.make_async_remote_copy(
                    src_ref=ins[a].at[q_t], dst_ref=outs[a].at[q_me],
                    send_sem=send_sems.at[a, j], recv_sem=recv_sems.at[a, j], device_id=(tx, ty, c), device_id_type=MESH))
                arrivals.append(pltpu.make_async_remote_copy(
                    src_ref=ins[a].at[q_t], dst_ref=outs[a].at[q_t],
                    send_sem=send_sems.at[a, j], recv_sem=recv_sems.at[a, j], device_id=(tx, ty, c), device_id_type=MESH))
        return mine, sends, arrivals

    def start(ins, outs, sems):
        mine, sends, _ = copies(ins, outs, sems)
        for cp in mine + sends:
            cp.start()

    def finish(ins, outs, sems):
        mine, sends, arrivals = copies(ins, outs, sems)
        for cp in arrivals:
            cp.wait_recv()
        for cp in sends:
            cp.wait_send()
        for cp in mine:
            cp.wait()

    out_shapes = [jax.ShapeDtypeStruct(p.shape, p.dtype) for p in parts]
    return _Rider(parts, out_shapes, _dma_sems(n, 3), start, finish)


def _broadcast_rider(values):
    n = len(values)

    def copies(ins, outs, sems):
        send_sems, recv_sems, local_sems = sems
        x, y, c = _position()
        me = _index(x, y, c)

        def peer(k):
            return (1 - x if k & 4 else x, 1 - y if k & 2 else y, 1 - c if k & 1 else c)

        mine = [pltpu.make_async_copy(ins[a], outs[a].at[me], local_sems.at[a]) for a in range(n)]
        sends, arrivals = [], []
        for a in range(n):
            for k in range(1, N_DEV):
                common = dict(send_sem=send_sems.at[a, k - 1], recv_sem=recv_sems.at[a, k - 1], device_id=peer(k), device_id_type=MESH)
                sends.append(pltpu.make_async_remote_copy(src_ref=ins[a], dst_ref=outs[a].at[me], **common))
                arrivals.append(pltpu.make_async_remote_copy(src_ref=ins[a], dst_ref=outs[a].at[_index(*peer(k))], **common))
        return mine, sends, arrivals

    def start(ins, outs, sems):
        mine, sends, _ = copies(ins, outs, sems)
        for cp in mine + sends:
            cp.start()

    def finish(ins, outs, sems):
        mine, sends, arrivals = copies(ins, outs, sems)
        for cp in arrivals:
            cp.wait_recv()
        for cp in sends:
            cp.wait_send()
        for cp in mine:
            cp.wait()

    out_shapes = [jax.ShapeDtypeStruct((N_DEV,) + v.shape, v.dtype) for v in values]
    return _Rider(values, out_shapes, _dma_sems(n, 7), start, finish)


def _join_riders(*riders):
    def each(fn_name, ins, outs, sems):
        i = o = s = 0
        for r in riders:
            n_i, n_o, n_s = len(r.ins), len(r.out_shapes), len(r.sem_shapes)
            if getattr(r, fn_name) is not None:
                getattr(r, fn_name)(ins[i:i + n_i], outs[o:o + n_o], sems[s:s + n_s])
            i, o, s = i + n_i, o + n_o, s + n_s

    middle = (lambda ins, outs, sems: each("middle", ins, outs, sems)) if any(r.middle for r in riders) else None
    return _Rider([a for r in riders for a in r.ins], [a for r in riders for a in r.out_shapes],
                  [a for r in riders for a in r.sem_shapes],
                  lambda ins, outs, sems: each("start", ins, outs, sems),
                  lambda ins, outs, sems: each("finish", ins, outs, sems), middle)


def _run_rider(name, rider):
    n_in, n_out = len(rider.ins), len(rider.out_shapes)

    def body(*refs):
        ins, outs, sems = refs[:n_in], refs[n_in:n_in + n_out], refs[n_in + n_out:]
        rider.start(ins, outs, sems)
        if rider.middle is not None:
            rider.middle(ins, outs, sems)
        rider.finish(ins, outs, sems)

    return pl.pallas_call(
        body, name=name, in_specs=[ANY] * n_in, out_specs=[ANY] * n_out, out_shape=rider.out_shapes,
        scratch_shapes=rider.sem_shapes)(*rider.ins)


class _Host:
    def __init__(self, rider):
        self.rider = rider
        self.n_in = len(rider.ins) if rider else 0
        self.n_out = len(rider.out_shapes) if rider else 0
        self.n_sem = len(rider.sem_shapes) if rider else 0
        self.ins = rider.ins if rider else []
        self.in_specs = [ANY] * self.n_in
        self.out_specs = [ANY] * self.n_out
        self.out_shapes = rider.out_shapes if rider else []
        self.scratch = rider.sem_shapes if rider else []

    def run(self, first, last, ins, outs, sems, compute, midway=None):
        if self.rider is None:
            compute()
            return

        @pl.when(first)
        def _():
            self.rider.start(ins, outs, sems)

        compute()
        if self.rider.middle is not None and midway is not None:
            pl.when(midway)(lambda: self.rider.middle(ins, outs, sems))

        @pl.when(last)
        def _():
            if self.rider.middle is not None and midway is None:
                self.rider.middle(ins, outs, sems)
            self.rider.finish(ins, outs, sems)


def _matmul(name, kind, a, b, *, tm, tn, tk, outs, epilogue=None, extras=(), out_blocks=False, rider=None, j_outer=False):
    if kind == "nn":
        (m, kdim), n = a.shape, b.shape[1]
    elif kind == "nt":
        (m, kdim), n = a.shape, b.shape[0]
    else:
        (kdim, m), n = a.shape, b.shape[1]
    if out_blocks:
        tn = min(tn, n // N_DEV)
    tm, tn, tk = _tile(m, tm), _tile(n, tn), _tile(kdim, tk)
    ni, nj, nk = m // tm, n // tn, kdim // tk

    def spec(shape, fn):
        return pl.BlockSpec(shape, (lambda g0, g1, k: fn(g1, g0, k)) if j_outer else fn)

    a_spec = spec((tk, tm), lambda i, j, k: (k, i)) if kind == "tn" else spec((tm, tk), lambda i, j, k: (i, k))
    b_spec = spec((tn, tk), lambda i, j, k: (j, k)) if kind == "nt" else spec((tk, tn), lambda i, j, k: (k, j))
    dn = {"nn": NN, "nt": NT, "tn": TN}[kind]

    tile_spec = spec((tm, tn), lambda i, j, k: (i, j))
    row_spec = spec((1, tn), lambda i, j, k: (0, j))
    if out_blocks:
        width = n // N_DEV
        r_out = width // tn
        out_shape = [jax.ShapeDtypeStruct((N_DEV, m, width), dt) for dt in outs]
        out_specs = [spec((None, tm, tn), lambda i, j, k: (j // r_out, i, j % r_out)) for _ in outs]
    else:
        out_shape = [jax.ShapeDtypeStruct((m, n), dt) for dt in outs]
        out_specs = [tile_spec for _ in outs]
    n_ex, n_out = len(extras), len(outs)
    host = _Host(rider)
    n_acc = 1 if nk > 1 else 0

    def body(*refs):
        a_ref, b_ref = refs[0], refs[1]
        pos = 2
        ex_refs = refs[pos:pos + n_ex]; pos += n_ex
        r_ins = refs[pos:pos + host.n_in]; pos += host.n_in
        out_refs = refs[pos:pos + n_out]; pos += n_out
        r_outs = refs[pos:pos + host.n_out]; pos += host.n_out
        acc_ref = refs[pos] if n_acc else None
        sems = refs[pos + n_acc:]
        i, j, k = pl.program_id(1 if j_outer else 0), pl.program_id(0 if j_outer else 1), pl.program_id(2)

        def finish_tile(acc):
            vals = (acc,) if epilogue is None else epilogue(acc, *[e[...] for e in ex_refs])
            for o_ref, v in zip(out_refs, vals):
                o_ref[...] = v.astype(o_ref.dtype)

        def compute():
            part = _dot(a_ref[...], b_ref[...], dn)
            if nk == 1:
                finish_tile(part)
                return

            @pl.when(k == 0)
            def _():
                acc_ref[...] = part

            @pl.when(jnp.logical_and(k > 0, k < nk - 1))
            def _():
                acc_ref[...] += part

            @pl.when(k == nk - 1)
            def _():
                finish_tile(acc_ref[...] + part)

        first = jnp.logical_and(jnp.logical_and(i == 0, j == 0), k == 0)
        last = jnp.logical_and(jnp.logical_and(i == ni - 1, j == nj - 1), k == nk - 1)
        step = (pl.program_id(0) * (ni if j_outer else nj) + pl.program_id(1)) * nk + k
        host.run(first, last, r_ins, r_outs, sems, compute, midway=step == (ni * nj * nk * 3) // 5)

    sem = ("arbitrary",) * 3 if rider else ("parallel", "parallel", "arbitrary")
    res = pl.pallas_call(
        body,
        name=name,
        grid=(nj, ni, nk) if j_outer else (ni, nj, nk),
        in_specs=[a_spec, b_spec] + [row_spec if e.shape[0] == 1 else tile_spec for e in extras] + host.in_specs,
        out_specs=out_specs + host.out_specs,
        out_shape=out_shape + host.out_shapes,
        scratch_shapes=([pltpu.VMEM((tm, tn), F32)] if n_acc else []) + host.scratch,
        compiler_params=_params(sem),
    )(a, b, *extras, *host.ins)
    return res[0] if len(res) == 1 else res


def _rms_fwd(name, x, g, tm=512, rider=None):
    t, d = x.shape
    tm = _tile(t, tm)
    n = t // tm
    host = _Host(rider)

    def body(*refs):
        x_ref, g_ref = refs[:2]
        r_ins = refs[2:2 + host.n_in]
        h_ref = refs[2 + host.n_in]
        r_outs = refs[3 + host.n_in:3 + host.n_in + host.n_out]
        sems = refs[3 + host.n_in + host.n_out:]
        i = pl.program_id(0)

        def compute():
            xf = x_ref[...]
            r = lax.rsqrt(jnp.mean(xf * xf, axis=-1, keepdims=True) + EPS)
            h_ref[...] = (xf * r * g_ref[...]).astype(h_ref.dtype)

        host.run(i == 0, i == n - 1, r_ins, r_outs, sems, compute, midway=i == (n * 3) // 5)

    res = pl.pallas_call(
        body,
        name=name,
        grid=(n,),
        in_specs=[pl.BlockSpec((tm, d), lambda i: (i, 0)), pl.BlockSpec((1, d), lambda i: (0, 0))] + host.in_specs,
        out_specs=[pl.BlockSpec((tm, d), lambda i: (i, 0))] + host.out_specs,
        out_shape=[jax.ShapeDtypeStruct((t, d), BF16)] + host.out_shapes,
        scratch_shapes=host.scratch,
        compiler_params=_params(("arbitrary",) if rider else ("parallel",)),
    )(x, g.reshape(1, d), *host.ins)
    return res[0] if len(res) == 1 else res


def _rms_bwd(name, dh, x, g, res=None, tm=256):
    t, d = x.shape
    tm = _tile(t, tm)
    has_res = res is not None

    def body(*refs):
        if has_res:
            dh_ref, x_ref, g_ref, res_ref, dx_ref, dxb_ref, gg_ref, ss_ref = refs
        else:
            dh_ref, x_ref, g_ref, dx_ref, dxb_ref, gg_ref, ss_ref = refs
        i = pl.program_id(0)
        xf = x_ref[...]
        r = lax.rsqrt(jnp.mean(xf * xf, axis=-1, keepdims=True) + EPS)
        xh = xf * r
        dhf = dh_ref[...].astype(F32)
        dxh = dhf * g_ref[...]
        dx = r * (dxh - xh * jnp.mean(dxh * xh, axis=-1, keepdims=True))

        @pl.when(i == 0)
        def _():
            gg_ref[...] = jnp.zeros_like(gg_ref)
            ss_ref[...] = jnp.zeros_like(ss_ref)

        if has_res:
            resf = res_ref[...]
            dx = dx + resf
            ss_ref[...] += jnp.sum(jnp.sum(resf * resf, axis=0, keepdims=True), axis=1, keepdims=True)
        dx_ref[...] = dx
        dxb_ref[...] = dx.astype(BF16)
        gg_ref[...] += jnp.sum(dhf * xh, axis=0, keepdims=True)

    row = pl.BlockSpec((tm, d), lambda i: (i, 0))
    vec = pl.BlockSpec((1, d), lambda i: (0, 0))
    one = pl.BlockSpec((1, 1), lambda i: (0, 0))
    ins = [dh, x, g.reshape(1, d)] + ([res] if has_res else [])
    dx, dxb, gg, ss = pl.pallas_call(
        body,
        name=name,
        grid=(t // tm,),
        in_specs=[row, row, vec] + ([row] if has_res else []),
        out_specs=[row, row, vec, one],
        out_shape=[jax.ShapeDtypeStruct((t, d), F32), jax.ShapeDtypeStruct((t, d), BF16), jax.ShapeDtypeStruct((1, d), F32),
                   jax.ShapeDtypeStruct((1, 1), F32)],
        compiler_params=_params(("arbitrary",)),
    )(*ins)
    return dx, dxb, gg.reshape(d), ss[0, 0]


def _head_rms(xf):
    r = lax.rsqrt(jnp.mean(xf * xf, axis=-1, keepdims=True) + EPS)
    return xf * r, r


def _head_rms_bwd(dy, xn, r, g):
    dxh = dy * g
    dx = r * (dxh - xn * jnp.mean(dxh * xn, axis=-1, keepdims=True))
    return dx, jnp.sum(dy * xn, axis=0, keepdims=True)


def _col_to_row(col):
    n = col.shape[0]
    eye = lax.broadcasted_iota(jnp.int32, (n, n), 0) == lax.broadcasted_iota(jnp.int32, (n, n), 1)
    return jnp.sum(jnp.where(eye, col, 0.0), axis=0, keepdims=True)


def _row_to_col(row):
    n = row.shape[1]
    eye = lax.broadcasted_iota(jnp.int32, (n, n), 0) == lax.broadcasted_iota(jnp.int32, (n, n), 1)
    return jnp.sum(jnp.where(eye, row, 0.0), axis=1, keepdims=True)


def _dproj_args(dproj, n_in):
    if dproj is None:
        return [], [], {}
    return [dproj], [ANY], {n_in: 0}


def _shift_down(u, s, rows):
    return jnp.where(rows >= s, pltpu.roll(u, s, axis=0), 0.0)


def _shift_up(u, s, rows, t):
    return jnp.where(rows < t - s, pltpu.roll(u, t - s, axis=0), 0.0)


def _conv_fwd(proj, off, conv_w, cb):
    t = proj.shape[0]
    c = conv_w.shape[1]
    blk0 = off // (3 * cb)

    def body(p_ref, w_ref, y_ref):
        rows = lax.broadcasted_iota(jnp.int32, (t, cb), 0)
        bg = p_ref[:, 0:cb].astype(F32)
        u = p_ref[:, cb:2 * cb].astype(F32) * p_ref[:, 2 * cb:3 * cb].astype(F32)
        w = w_ref[...]
        conv = w[2:3] * u + w[1:2] * _shift_down(u, 1, rows) + w[0:1] * _shift_down(u, 2, rows)
        y_ref[...] = (bg * conv).astype(y_ref.dtype)

    return pl.pallas_call(
        body,
        name="conv_fwd",
        grid=(c // cb,),
        in_specs=[pl.BlockSpec((t, 3 * cb), lambda j: (0, blk0 + j)), pl.BlockSpec((CONV_TAPS, cb), lambda j: (0, j))],
        out_specs=pl.BlockSpec((t, cb), lambda j: (0, j)),
        out_shape=jax.ShapeDtypeStruct((t, c), BF16),
        compiler_params=_params(("parallel",)),
    )(proj, conv_w)


def _conv_bwd(proj, off, conv_w, dy, cb, dproj, rider=None):
    t = proj.shape[0]
    c = conv_w.shape[1]
    blk0 = off // (3 * cb)
    nj = c // cb
    host = _Host(rider)

    def body(*refs):
        p_ref, w_ref, dy_ref = refs[:3]
        r_ins = refs[4:4 + host.n_in]
        dp_ref, gw_ref = refs[4 + host.n_in:6 + host.n_in]
        r_outs = refs[6 + host.n_in:6 + host.n_in + host.n_out]
        sems = refs[6 + host.n_in + host.n_out:]
        j = pl.program_id(0)

        def compute():
            rows = lax.broadcasted_iota(jnp.int32, (t, cb), 0)
            bg = p_ref[:, 0:cb].astype(F32)
            cg = p_ref[:, cb:2 * cb].astype(F32)
            v = p_ref[:, 2 * cb:3 * cb].astype(F32)
            u = cg * v
            w = w_ref[...]
            u1 = _shift_down(u, 1, rows)
            u2 = _shift_down(u, 2, rows)
            conv = w[2:3] * u + w[1:2] * u1 + w[0:1] * u2
            dyf = dy_ref[...].astype(F32)
            dconv = dyf * bg
            du = w[2:3] * dconv + w[1:2] * _shift_up(dconv, 1, rows, t) + w[0:1] * _shift_up(dconv, 2, rows, t)
            dp_ref[:, 0:cb] = (dyf * conv).astype(dp_ref.dtype)
            dp_ref[:, cb:2 * cb] = (du * v).astype(dp_ref.dtype)
            dp_ref[:, 2 * cb:3 * cb] = (du * cg).astype(dp_ref.dtype)
            gw_ref[0:1, :] = jnp.sum(dconv * u2, axis=0, keepdims=True)
            gw_ref[1:2, :] = jnp.sum(dconv * u1, axis=0, keepdims=True)
            gw_ref[2:3, :] = jnp.sum(dconv * u, axis=0, keepdims=True)

        host.run(j == 0, j == nj - 1, r_ins, r_outs, sems, compute)

    res = pl.pallas_call(
        body,
        name="conv_bwd",
        grid=(nj,),
        in_specs=[
            pl.BlockSpec((t, 3 * cb), lambda j: (0, blk0 + j)),
            pl.BlockSpec((CONV_TAPS, cb), lambda j: (0, j)),
            pl.BlockSpec((t, cb), lambda j: (0, j)),
            ANY,
        ] + host.in_specs,
        out_specs=[pl.BlockSpec((t, 3 * cb), lambda j: (0, blk0 + j)), pl.BlockSpec((CONV_TAPS, cb), lambda j: (0, j))] + host.out_specs,
        out_shape=[jax.ShapeDtypeStruct(dproj.shape, dproj.dtype), jax.ShapeDtypeStruct((CONV_TAPS, c), F32)] + host.out_shapes,
        input_output_aliases={3: 0},
        scratch_shapes=host.scratch,
        compiler_params=_params(("arbitrary",)),
    )(proj, conv_w, dy, dproj, *host.ins)
    return res


def _lane_scan(x, reverse):
    lane = lax.broadcasted_iota(jnp.int32, x.shape, 1)
    s = 1
    while s < LANES:
        if reverse:
            x = x + jnp.where(lane < LANES - s, pltpu.roll(x, LANES - s, axis=1), 0.0)
        else:
            x = x + jnp.where(lane >= s, pltpu.roll(x, s, axis=1), 0.0)
        s *= 2
    return x


def _scan_rows(src_ref, dst_ref, t, reverse, fn=None):
    groups = list(range(t // LANES))
    if reverse:
        groups = groups[::-1]
    carry = None
    for gi in groups:
        sl = slice(gi * LANES, (gi + 1) * LANES)
        blk = src_ref[:, sl]
        if fn is not None:
            blk = fn(blk)
        blk = _lane_scan(blk, reverse)
        if carry is not None:
            blk = blk + carry
        dst_ref[:, sl] = blk
        carry = blk[:, 0:1] if reverse else blk[:, LANES - 1:LANES]


def _forget_fwd(z_row, b_col):
    rows, t = z_row.shape

    def body(z_ref, b_ref, c_ref):
        def logf(z):
            zz = z + b_ref[...]
            return jnp.minimum(zz, 0.0) - jnp.log(1.0 + jnp.exp(-jnp.abs(zz)))

        _scan_rows(z_ref, c_ref, t, False, logf)

    return pl.pallas_call(
        body,
        name="forget_fwd",
        out_shape=jax.ShapeDtypeStruct((rows, t), F32),
        compiler_params=pltpu.CompilerParams(vmem_limit_bytes=VMEM_LIMIT),
    )(z_row, b_col)


def _rows_to_colb(c_row3, tq):
    heads, _, t = c_row3.shape

    def body(r_ref, o_ref):
        o_ref[...] = jnp.broadcast_to(_row_to_col(r_ref[...]), (tq, LANES))

    return pl.pallas_call(
        body,
        name="rows_to_colb",
        grid=(heads, t // tq),
        in_specs=[pl.BlockSpec((None, 1, tq), lambda h, i: (h, 0, i))],
        out_specs=pl.BlockSpec((None, tq, LANES), lambda h, i: (h, i, 0)),
        out_shape=jax.ShapeDtypeStruct((heads, t, LANES), F32),
        compiler_params=_params(("parallel", "parallel")),
    )(c_row3)


def _forget_bwd(z_row, b_col, dc_row):
    rows, t = z_row.shape

    def body(z_ref, b_ref, dc_ref, dz_ref, db_ref, tmp_ref):
        _scan_rows(dc_ref, tmp_ref, t, True)
        zz = z_ref[...] + b_ref[...]
        dz = tmp_ref[...] * (1.0 / (1.0 + jnp.exp(zz)))
        dz_ref[...] = dz.astype(dz_ref.dtype)
        db_ref[...] = jnp.sum(dz, axis=1, keepdims=True)

    return pl.pallas_call(
        body,
        name="forget_bwd",
        out_shape=[jax.ShapeDtypeStruct((rows, t), BF16), jax.ShapeDtypeStruct((rows, 1), F32)],
        scratch_shapes=[pltpu.VMEM((rows, t), F32)],
        compiler_params=pltpu.CompilerParams(vmem_limit_bytes=VMEM_LIMIT),
    )(z_row, b_col, dc_row)


def _fox_fwd(proj, off, gq, gk, c_row3, c_colb, heads, tq, rider=None):
    t = proj.shape[0]
    hd = FOX_HEAD_DIM
    tq = _tile(t, tq)
    nq = t // tq
    blk0 = off // hd
    scale = 1.0 / math.sqrt(hd)
    host = _Host(rider)

    def body(*refs):
        q_ref, k_ref, v_ref, gq_ref, gk_ref, crow_ref, ccol_ref = refs[:7]
        r_ins = refs[7:7 + host.n_in]
        o_ref, lse_ref = refs[7 + host.n_in:9 + host.n_in]
        r_outs = refs[9 + host.n_in:9 + host.n_in + host.n_out]
        khat_ref, v_t_ref = refs[9 + host.n_in + host.n_out:11 + host.n_in + host.n_out]
        sems = refs[11 + host.n_in + host.n_out:]
        h, qi = pl.program_id(0), pl.program_id(1)

        def compute():
            eye = (lax.broadcasted_iota(jnp.int32, (hd, hd), 0) == lax.broadcasted_iota(jnp.int32, (hd, hd), 1)).astype(BF16)

            @pl.when(qi == 0)
            def _():
                kn, _ = _head_rms(k_ref[...].astype(F32))
                khat_ref[...] = (kn * gk_ref[...]).astype(BF16)
                v_t_ref[...] = _dot(eye, v_ref[...], NT).astype(BF16)

            qn, _ = _head_rms(q_ref[...].astype(F32))
            qhat = (qn * (gq_ref[...] * scale)).astype(BF16)
            crow = crow_ref[:, pl.ds(pl.multiple_of(qi * tq, tq), tq)]
            above = lax.broadcasted_iota(jnp.int32, (tq, tq), 1) >= lax.broadcasted_iota(jnp.int32, (tq, tq), 0)

            def tile(j, keys, carry, diagonal):
                m, l, acc_t = carry
                ks = pl.multiple_of(j * keys, keys)
                s_t = _dot(khat_ref[pl.ds(ks, keys), :], qhat, NT) - ccol_ref[pl.ds(ks, keys), 0:1]
                if diagonal:
                    s_t = jnp.where(above, s_t, NEG)
                m_new = jnp.maximum(m, jnp.max(s_t, axis=0, keepdims=True) + crow)
                alpha = jnp.exp(m - m_new)
                p_t = jnp.exp(s_t + (crow - m_new))
                l = alpha * l + jnp.sum(p_t, axis=0, keepdims=True)
                acc_t = alpha * acc_t + _dot(v_t_ref[:, pl.ds(ks, keys)], p_t.astype(BF16), NN)
                return m_new, l, acc_t

            init = (jnp.full((1, tq), NEG, F32), jnp.zeros((1, tq), F32), jnp.zeros((hd, tq), F32))
            pairs = qi // 2 if 2 * tq <= MAX_KEYS else 0
            carry = lax.fori_loop(0, pairs, lambda j, c: tile(j, 2 * tq, c, False), init)
            carry = lax.fori_loop(2 * pairs, qi, lambda j, c: tile(j, tq, c, False), carry)
            m, l, acc_t = tile(qi, tq, carry, True)
            o_ref[...] = _dot((acc_t / l).astype(BF16), eye, TN).astype(o_ref.dtype)
            lse_ref[...] = m + jnp.log(l)

        first = jnp.logical_and(h == 0, qi == 0)
        last = jnp.logical_and(h == heads - 1, qi == nq - 1)
        host.run(first, last, r_ins, r_outs, sems, compute, midway=h * nq + qi == (heads * nq * 3) // 5)

    res = pl.pallas_call(
        body,
        name="fox_fwd",
        grid=(heads, nq),
        in_specs=[
            pl.BlockSpec((tq, hd), lambda h, i: (i, blk0 + 3 * h)),
            pl.BlockSpec((t, hd), lambda h, i: (0, blk0 + 3 * h + 1)),
            pl.BlockSpec((t, hd), lambda h, i: (0, blk0 + 3 * h + 2)),
            pl.BlockSpec((1, hd), lambda h, i: (0, 0)),
            pl.BlockSpec((1, hd), lambda h, i: (0, 0)),
            pl.BlockSpec((None, 1, t), lambda h, i: (h, 0, 0)),
            pl.BlockSpec((None, t, LANES), lambda h, i: (h, 0, 0)),
        ] + host.in_specs,
        out_specs=[pl.BlockSpec((tq, hd), lambda h, i: (i, h)), pl.BlockSpec((None, 1, tq), lambda h, i: (h, 0, i))] + host.out_specs,
        out_shape=[jax.ShapeDtypeStruct((t, heads * hd), BF16), jax.ShapeDtypeStruct((heads, 1, t), F32)] + host.out_shapes,
        scratch_shapes=[pltpu.VMEM((t, hd), BF16), pltpu.VMEM((hd, t), BF16)] + host.scratch,
        compiler_params=_params(("arbitrary", "arbitrary")),
    )(proj, proj, proj, gq.reshape(1, hd), gk.reshape(1, hd), c_row3, c_colb, *host.ins)
    return res


def _fox_bwd(proj, off, o, do, gq, gk, c_row3, c_colb, lse, heads, tq, dproj, rider=None):
    t = proj.shape[0]
    hd = FOX_HEAD_DIM
    tq = _tile(t, tq)
    nb = t // tq
    blk0 = off // hd
    scale = 1.0 / math.sqrt(hd)
    host = _Host(rider)
    n_fixed_in = 11

    def body(*refs):
        q_ref, k_ref, v_ref, o_ref, do_ref, gq_ref, gk_ref, crow_ref, ccol_ref, lse_ref = refs[:10]
        pos = n_fixed_in
        r_ins = refs[pos:pos + host.n_in]; pos += host.n_in
        dp_ref, dc_ref, ggq_ref, ggk_ref = refs[pos:pos + 4]; pos += 4
        r_outs = refs[pos:pos + host.n_out]; pos += host.n_out
        qhat_ref, khat_ref, khat_t_ref, dq_t_ref, dk_ref, dcq_ref, dck_ref, delta_ref = refs[pos:pos + 8]; pos += 8
        sems = refs[pos:]
        h = pl.program_id(0)

        def compute():
            qn, rq = _head_rms(q_ref[...].astype(F32))
            qhat_ref[...] = (qn * (gq_ref[...] * scale)).astype(BF16)
            kn, rk = _head_rms(k_ref[...].astype(F32))
            khat_ref[...] = (kn * gk_ref[...]).astype(BF16)
            eye = (lax.broadcasted_iota(jnp.int32, (hd, hd), 0) == lax.broadcasted_iota(jnp.int32, (hd, hd), 1)).astype(BF16)
            khat_t_ref[...] = _dot(eye, khat_ref[...], NT).astype(BF16)
            delta = jnp.sum(do_ref[...].astype(F32) * o_ref[...].astype(F32), axis=-1, keepdims=True)
            for b in range(nb):
                sl = slice(b * tq, (b + 1) * tq)
                delta_ref[:, sl] = _col_to_row(delta[sl, :])
            dq_t_ref[...] = jnp.zeros_like(dq_t_ref)
            dcq_ref[...] = jnp.zeros_like(dcq_ref)
            above = lax.broadcasted_iota(jnp.int32, (tq, tq), 1) >= lax.broadcasted_iota(jnp.int32, (tq, tq), 0)

            def kv_block(j, _):
                ks = pl.multiple_of(j * tq, tq)
                kh = khat_ref[pl.ds(ks, tq), :]
                kh_t = khat_t_ref[:, pl.ds(ks, tq)]
                vv = v_ref[pl.ds(ks, tq), :]
                ccol = ccol_ref[pl.ds(ks, tq), 0:1]

                def q_block(i, n, carry, diagonal):
                    dk, dv, dck = carry
                    qs = pl.multiple_of(i * tq, tq)
                    qh = qhat_ref[pl.ds(qs, n), :]
                    dob = do_ref[pl.ds(qs, n), :]
                    s_t = _dot(kh, qh, NT) + ((crow_ref[:, pl.ds(qs, n)] - lse_ref[:, pl.ds(qs, n)]) - ccol)
                    p_t = jnp.exp(s_t)
                    if diagonal:
                        p_t = jnp.where(above, p_t, 0.0)
                    ds_t = p_t * (_dot(vv, dob, NT) - delta_ref[:, pl.ds(qs, n)])
                    dsb = ds_t.astype(BF16)
                    dv = dv + _dot(p_t.astype(BF16), dob, NN)
                    dk = dk + _dot(dsb, qh, NN)
                    dq_t_ref[:, pl.ds(qs, n)] += _dot(kh_t, dsb, NN)
                    dcq_ref[:, pl.ds(qs, n)] += jnp.sum(ds_t, axis=0, keepdims=True)
                    dck = dck + jnp.sum(ds_t, axis=-1, keepdims=True)
                    return dk, dv, dck

                zero = jnp.zeros((tq, hd), F32)
                carry = q_block(j, tq, (zero, zero, jnp.zeros((tq, 1), F32)), True)
                pairs = (nb - 1 - j) // 2 if 2 * tq <= MAX_KEYS else 0
                carry = lax.fori_loop(0, pairs, lambda p, c: q_block(j + 1 + 2 * p, 2 * tq, c, False), carry)
                dk, dv, dck = lax.fori_loop(j + 1 + 2 * pairs, nb, lambda i, c: q_block(i, tq, c, False), carry)
                dk_ref[pl.ds(ks, tq), :] = dk
                dp_ref[pl.ds(ks, tq), 2 * hd:3 * hd] = dv.astype(dp_ref.dtype)
                dck_ref[pl.ds(ks, tq), :] = dck
                return 0

            lax.fori_loop(0, nb, kv_block, 0)

            dq, ggq = _head_rms_bwd(dq_t_ref[...].T * scale, qn, rq, gq_ref[...])
            dk, ggk = _head_rms_bwd(dk_ref[...], kn, rk, gk_ref[...])
            dp_ref[:, 0:hd] = dq.astype(dp_ref.dtype)
            dp_ref[:, hd:2 * hd] = dk.astype(dp_ref.dtype)
            for b in range(nb):
                sl = slice(b * tq, (b + 1) * tq)
                dc_ref[:, sl] = dcq_ref[:, sl] - _col_to_row(dck_ref[sl, :])

            @pl.when(h == 0)
            def _():
                ggq_ref[...] = jnp.zeros_like(ggq_ref)
                ggk_ref[...] = jnp.zeros_like(ggk_ref)

            ggq_ref[...] += ggq
            ggk_ref[...] += ggk

        host.run(h == 0, h == heads - 1, r_ins, r_outs, sems, compute)

    head_in = lambda part: pl.BlockSpec((t, hd), lambda h: (0, blk0 + 3 * h + part))
    vec = pl.BlockSpec((1, hd), lambda h: (0, 0))
    colb = pl.BlockSpec((None, t, LANES), lambda h: (h, 0, 0))
    res = pl.pallas_call(
        body,
        name="fox_bwd",
        grid=(heads,),
        in_specs=[
            head_in(0), head_in(1), head_in(2),
            pl.BlockSpec((t, hd), lambda h: (0, h)),
            pl.BlockSpec((t, hd), lambda h: (0, h)),
            vec, vec,
            pl.BlockSpec((None, 1, t), lambda h: (h, 0, 0)),
            colb,
            pl.BlockSpec((None, 1, t), lambda h: (h, 0, 0)),
            ANY,
        ] + host.in_specs,
        out_specs=[
            pl.BlockSpec((t, 3 * hd), lambda h: (0, blk0 // 3 + h)),
            pl.BlockSpec((None, 1, t), lambda h: (h, 0, 0)),
            vec, vec,
        ] + host.out_specs,
        out_shape=[
            jax.ShapeDtypeStruct(dproj.shape, dproj.dtype),
            jax.ShapeDtypeStruct((heads, 1, t), F32),
            jax.ShapeDtypeStruct((1, hd), F32),
            jax.ShapeDtypeStruct((1, hd), F32),
        ] + host.out_shapes,
        input_output_aliases={10: 0},
        scratch_shapes=[
            pltpu.VMEM((t, hd), BF16), pltpu.VMEM((t, hd), BF16), pltpu.VMEM((hd, t), BF16),
            pltpu.VMEM((hd, t), F32), pltpu.VMEM((t, hd), F32),
            pltpu.VMEM((1, t), F32), pltpu.VMEM((t, 1), F32), pltpu.VMEM((1, t), F32),
        ] + host.scratch,
        compiler_params=_params(("arbitrary",)),
    )(proj, proj, proj, o, do, gq.reshape(1, hd), gk.reshape(1, hd), c_row3, c_colb, lse, dproj, *host.ins)
    return res


def _mem_fwd(proj, off, kv, gq, gk, tq):
    t = proj.shape[0]
    m, width = kv.shape[0], kv.shape[1] // 2
    hd = width // MEM_HEADS
    tq = _tile(t, tq)
    blk0 = off // hd
    scale = 1.0 / math.sqrt(hd)

    def body(q_ref, k_ref, v_ref, gq_ref, gk_ref, o_ref):
        qn, _ = _head_rms(q_ref[...].astype(F32))
        kn, _ = _head_rms(k_ref[...])
        s = _dot((qn * gq_ref[...]).astype(BF16), (kn * gk_ref[...]).astype(BF16), NT) * scale
        p = jnp.exp(s - jnp.max(s, axis=-1, keepdims=True))
        p = p / jnp.sum(p, axis=-1, keepdims=True)
        o_ref[...] = _dot(p.astype(BF16), v_ref[...].astype(BF16), NN).astype(o_ref.dtype)

    vec = pl.BlockSpec((1, hd), lambda h, i: (0, 0))
    return pl.pallas_call(
        body,
        name="mem_fwd",
        grid=(MEM_HEADS, t // tq),
        in_specs=[
            pl.BlockSpec((tq, hd), lambda h, i: (i, blk0 + h)),
            pl.BlockSpec((m, hd), lambda h, i: (0, h)),
            pl.BlockSpec((m, hd), lambda h, i: (0, MEM_HEADS + h)),
            vec, vec,
        ],
        out_specs=pl.BlockSpec((tq, hd), lambda h, i: (i, h)),
        out_shape=jax.ShapeDtypeStruct((t, width), BF16),
        compiler_params=_params(("parallel", "parallel")),
    )(proj, kv, kv, gq.reshape(1, hd), gk.reshape(1, hd))


def _mem_bwd(proj, off, kv, do, gq, gk, tq, dproj, rider=None):
    t = proj.shape[0]
    m, width = kv.shape[0], kv.shape[1] // 2
    hd = width // MEM_HEADS
    tq = _tile(t, tq)
    nq = t // tq
    blk0 = off // hd
    scale = 1.0 / math.sqrt(hd)
    host = _Host(rider)

    def body(*refs):
        q_ref, k_ref, v_ref, do_ref, gq_ref, gk_ref = refs[:6]
        pos = 7
        r_ins = refs[pos:pos + host.n_in]; pos += host.n_in
        dq_ref, dk_ref, dv_ref, ggq_ref, ggk_ref = refs[pos:pos + 5]; pos += 5
        r_outs = refs[pos:pos + host.n_out]; pos += host.n_out
        dkh_ref, dvh_ref = refs[pos:pos + 2]; pos += 2
        sems = refs[pos:]
        h, i = pl.program_id(0), pl.program_id(1)

        def compute():
            qn, rq = _head_rms(q_ref[...].astype(F32))
            kn, rk = _head_rms(k_ref[...])
            qhat = (qn * gq_ref[...]).astype(BF16)
            khat = (kn * gk_ref[...]).astype(BF16)
            vb = v_ref[...].astype(BF16)
            dob = do_ref[...]
            s = _dot(qhat, khat, NT) * scale
            p = jnp.exp(s - jnp.max(s, axis=-1, keepdims=True))
            p = p / jnp.sum(p, axis=-1, keepdims=True)
            dp = _dot(dob, vb, NT)
            ds = p * (dp - jnp.sum(dp * p, axis=-1, keepdims=True))
            dsb = ds.astype(BF16)
            dq, ggq = _head_rms_bwd(_dot(dsb, khat, NN) * scale, qn, rq, gq_ref[...])
            dq_ref[...] = dq.astype(dq_ref.dtype)

            @pl.when(i == 0)
            def _():
                dkh_ref[...] = jnp.zeros_like(dkh_ref)
                dvh_ref[...] = jnp.zeros_like(dvh_ref)

            @pl.when(jnp.logical_and(h == 0, i == 0))
            def _():
                ggq_ref[...] = jnp.zeros_like(ggq_ref)
                ggk_ref[...] = jnp.zeros_like(ggk_ref)

            dkh_ref[...] += _dot(dsb, qhat, TN)
            dvh_ref[...] += _dot(p.astype(BF16), dob, TN)
            ggq_ref[...] += ggq

            @pl.when(i == nq - 1)
            def _():
                dk, ggk = _head_rms_bwd(dkh_ref[...] * scale, kn, rk, gk_ref[...])
                dk_ref[...] = dk.astype(dk_ref.dtype)
                dv_ref[...] = dvh_ref[...].astype(dv_ref.dtype)
                ggk_ref[...] += ggk

        first = jnp.logical_and(h == 0, i == 0)
        last = jnp.logical_and(h == MEM_HEADS - 1, i == nq - 1)
        host.run(first, last, r_ins, r_outs, sems, compute)

    vec = pl.BlockSpec((1, hd), lambda h, i: (0, 0))
    kblk = pl.BlockSpec((m, hd), lambda h, i: (0, h))
    res = pl.pallas_call(
        body,
        name="mem_bwd",
        grid=(MEM_HEADS, nq),
        in_specs=[
            pl.BlockSpec((tq, hd), lambda h, i: (i, blk0 + h)), kblk,
            pl.BlockSpec((m, hd), lambda h, i: (0, MEM_HEADS + h)),
            pl.BlockSpec((tq, hd), lambda h, i: (i, h)), vec, vec, ANY,
        ] + host.in_specs,
        out_specs=[pl.BlockSpec((tq, hd), lambda h, i: (i, blk0 + h)), kblk, kblk, vec, vec] + host.out_specs,
        out_shape=[
            jax.ShapeDtypeStruct(dproj.shape, dproj.dtype),
            jax.ShapeDtypeStruct((m, width), BF16),
            jax.ShapeDtypeStruct((m, width), BF16),
            jax.ShapeDtypeStruct((1, hd), F32),
            jax.ShapeDtypeStruct((1, hd), F32),
        ] + host.out_shapes,
        input_output_aliases={6: 0},
        scratch_shapes=[pltpu.VMEM((m, hd), F32), pltpu.VMEM((m, hd), F32)] + host.scratch,
        compiler_params=_params(("arbitrary", "arbitrary")),
    )(proj, kv, kv, do, gq.reshape(1, hd), gk.reshape(1, hd), dproj, *host.ins)
    dproj, dk, dv, ggq, ggk = res[:5]
    return (dproj, jnp.concatenate([dk, dv], axis=1), ggq.reshape(hd), ggk.reshape(hd), *res[5:])


def _sigmoid(z):
    return 1.0 / (1.0 + jnp.exp(-z))


def _merge_fwd(proj, ys, ws, tm, tc):
    t, cw = ys[0].shape
    d = ws[0].shape[1]
    tm = _tile(t, tm)

    def body(g_ref, ya_ref, yb_ref, yc_ref, wa_ref, wb_ref, wc_ref, oa_ref, ob_ref, oc_ref, out_ref):
        acc = jnp.zeros((tm, tc), F32)
        for s, (y_ref, w_ref, o_ref) in enumerate(((ya_ref, wa_ref, oa_ref), (yb_ref, wb_ref, ob_ref), (yc_ref, wc_ref, oc_ref))):
            o = _dot(y_ref[...], w_ref[...], NN)
            o_ref[...] = o.astype(o_ref.dtype)
            acc = acc + _sigmoid(g_ref[:, s * tc:(s + 1) * tc].astype(F32)) * o
        out_ref[...] = acc.astype(out_ref.dtype)

    blk = pl.BlockSpec((tm, tc), lambda i, j: (i, j))
    y_spec = pl.BlockSpec((tm, cw), lambda i, j: (i, 0))
    w_spec = pl.BlockSpec((cw, tc), lambda i, j: (0, j))
    return pl.pallas_call(
        body,
        name="merge_fwd",
        grid=(t // tm, d // tc),
        in_specs=[pl.BlockSpec((tm, 3 * tc), lambda i, j: (i, j))] + [y_spec] * 3 + [w_spec] * 3,
        out_specs=[blk] * 4,
        out_shape=[jax.ShapeDtypeStruct((t, d), BF16)] * 4,
        compiler_params=_params(("parallel", "parallel")),
    )(proj, *ys, *ws)


def _merge_bwd(proj, o3, dx1, w_out, ws, tm, tc):
    t, d = o3[0].shape
    k = dx1.shape[1]
    cw = ws[0].shape[0]
    tm = _tile(t, tm)
    ni, nj = t // tm, d // tc

    def body(dx_ref, w_ref, g_ref, oa_ref, ob_ref, oc_ref, wa_ref, wb_ref, wc_ref,
             dg_ref, da_ref, db_ref, dc_ref, ya_ref, yb_ref, yc_ref, acc_ref):
        j = pl.program_id(1)
        dmf = _dot(dx_ref[...], w_ref[...], NT)
        branches = ((oa_ref, da_ref, wa_ref, ya_ref), (ob_ref, db_ref, wb_ref, yb_ref), (oc_ref, dc_ref, wc_ref, yc_ref))
        for s, (o_ref, do_ref, ws_ref, dy_ref) in enumerate(branches):
            g = _sigmoid(g_ref[:, s * tc:(s + 1) * tc].astype(F32))
            do = (dmf * g).astype(do_ref.dtype)
            do_ref[...] = do
            dg_ref[:, s * tc:(s + 1) * tc] = (dmf * o_ref[...].astype(F32) * g * (1.0 - g)).astype(dg_ref.dtype)
            part = _dot(do, ws_ref[...], NT)

            @pl.when(j == 0)
            def _():
                acc_ref[s] = part

            @pl.when(j > 0)
            def _():
                acc_ref[s] += part

            @pl.when(j == nj - 1)
            def _():
                dy_ref[...] = acc_ref[s].astype(dy_ref.dtype)

    blk = pl.BlockSpec((tm, tc), lambda i, j: (i, j))
    wide = pl.BlockSpec((tm, 3 * tc), lambda i, j: (i, j))
    w_spec = pl.BlockSpec((cw, tc), lambda i, j: (0, j))
    y_spec = pl.BlockSpec((tm, cw), lambda i, j: (i, 0))
    return pl.pallas_call(
        body,
        name="merge_bwd",
        grid=(ni, nj),
        in_specs=[pl.BlockSpec((tm, k), lambda i, j: (i, 0)), pl.BlockSpec((tc, k), lambda i, j: (j, 0)), wide, blk, blk, blk] + [w_spec] * 3,
        out_specs=[wide, blk, blk, blk] + [y_spec] * 3,
        out_shape=[jax.ShapeDtypeStruct(proj.shape, BF16)] + [jax.ShapeDtypeStruct((t, d), BF16)] * 3 + [jax.ShapeDtypeStruct((t, cw), BF16)] * 3,
        scratch_shapes=[pltpu.VMEM((3, tm, cw), F32)],
        compiler_params=_params(("parallel", "arbitrary")),
    )(dx1, w_out, proj, *o3, *ws)


def _w_in_chunks(d, tc):
    cw = d // 2
    heads = cw // FOX_HEAD_DIM
    conv0, fox0, f0, mq0, gate0 = 0, 3 * cw, 6 * cw, 6 * cw + heads, 7 * cw + heads
    chunks = [(gate0 + s * d + j * tc, gate0 + s * d + (j + 1) * tc) for j in range(d // tc) for s in range(N_BRANCHES)]
    chunks += [(conv0 + s * cw + j * LANES, conv0 + s * cw + (j + 1) * LANES) for j in range(cw // LANES) for s in range(3)]
    chunks += [(fox0 + s * cw + j * FOX_HEAD_DIM, fox0 + s * cw + (j + 1) * FOX_HEAD_DIM) for j in range(heads) for s in range(3)]
    chunks.append((mq0, mq0 + cw))
    return chunks, (f0, f0 + heads)


ROW_TILE = 16
GROUP = 128
GROUP_BACK = 112
SCRATCH_ROWS = 2 * GROUP + 32


def _padded_rows(r):
    return -(-r // GROUP_BACK) * GROUP_BACK


def _rows_from(scr_ref, use, q8, fine, g):
    x = scr_ref[pl.ds(pl.multiple_of(q8 * 8, 8), g + 8), :]
    for s in range(8):
        @pl.when(fine == s)
        def _(s=s):
            use((x if s == 0 else pltpu.roll(x, g + 8 - s, axis=0))[0:g])


def _assemble(name, tbl, grid, step, in_specs, out_spec, out_shape, operands, g, w1, cols_of):
    has_f = len(in_specs) == 3
    k = out_shape.shape[-1]
    c = cols_of

    def body(*refs):
        t_ref, s1_ref, s2_ref = refs[:3]
        f_ref = refs[3] if has_f else None
        out_ref = refs[3 + has_f]
        scr1, scr2, scrf = refs[4 + has_f:]
        t = step()

        def put(y):
            out_ref[...] = y.astype(out_ref.dtype)

        @pl.when(t == 0)
        def _():
            scr1[...] = jnp.zeros_like(scr1)
            scr2[...] = jnp.zeros_like(scr2)
            scrf[...] = jnp.zeros_like(scrf)

        rows = lax.broadcasted_iota(jnp.int32, (g, k), 0)
        n1, a2 = t_ref[c["n1"], t], t_ref[c["a2"], t]
        scr1[0:w1, :] = (s1_ref[0] if len(s1_ref.shape) == 3 else s1_ref[...]).astype(F32)
        _rows_from(scr1, put, t_ref[c["q1"], t], t_ref[c["s1"], t], g)

        @pl.when(a2 < g)
        def _():
            scr2[g:g + s2_ref.shape[0], :] = s2_ref[...].astype(F32)
            _rows_from(scr2, lambda y: put(jnp.where(rows < n1, out_ref[...].astype(F32), y)),
                       t_ref[c["q2"], t], t_ref[c["s2"], t], g)

        if has_f:
            fa, fb = t_ref[c["fa"], t], t_ref[c["fb"], t]

            @pl.when(fb > fa)
            def _():
                scrf[g:g + f_ref.shape[0], :] = f_ref[...].astype(F32)
                inside = jnp.logical_and(rows >= fa, rows < fb)
                _rows_from(scrf, lambda y: put(jnp.where(inside, y, out_ref[...].astype(F32))),
                           t_ref[c["qf"], t], t_ref[c["sf"], t], g)

            valid = t_ref[c["valid"], t]

            @pl.when(valid < g)
            def _():
                out_ref[...] = jnp.where(rows < valid, out_ref[...].astype(F32), 0.0).astype(out_ref.dtype)

    return pl.pallas_call(
        body,
        name=name,
        grid_spec=pltpu.PrefetchScalarGridSpec(
            num_scalar_prefetch=1, grid=grid, in_specs=in_specs, out_specs=out_spec,
            scratch_shapes=[pltpu.VMEM((SCRATCH_ROWS, k), F32)] * 3),
        out_shape=out_shape,
        compiler_params=_params(("arbitrary",) * len(grid)),
    )(jnp.asarray(tbl), *operands)


def _pack_w_in(w8, d, tc):
    blocks, rp, k = w8.shape
    chunks, (f_lo, f_hi) = _w_in_chunks(d, tc)
    r = max(hi for _, hi in chunks) // blocks
    g, w1 = GROUP, GROUP + ROW_TILE
    table = []
    for lo, hi in chunks:
        for g0 in range(lo, hi, g):
            b1, r1 = divmod(g0, r)
            n1 = min(g, r - r1)
            st1 = min(r1 // ROW_TILE * ROW_TILE, rp - w1)
            o1, o2 = r1 - st1, g - n1
            b2 = b1 + 1 if n1 < g else 0
            table.append((b1, st1, o1 // 8, o1 % 8, n1, n1, b2, o2 // 8, o2 % 8))
    names = ("b1", "st1", "q1", "s1", "n1", "a2", "b2", "q2", "s2")
    cols_of = {n: i for i, n in enumerate(names)}
    tbl = np.array(table, np.int32).T
    c = cols_of
    w_all = _assemble(
        "pack_w_in", tbl, (len(table),), lambda: pl.program_id(0),
        [pl.BlockSpec((pl.Element(1), pl.Element(w1), pl.Element(k)), lambda i, t: (t[c["b1"], i], pl.multiple_of(t[c["st1"], i], ROW_TILE), 0)),
         pl.BlockSpec((None, g, k), lambda i, t: (t[c["b2"], i], 0, 0))],
        pl.BlockSpec((g, k), lambda i, t: (i, 0)),
        jax.ShapeDtypeStruct((len(table) * g, k), w8.dtype), [w8, w8], g, w1, cols_of)
    fb, fr = divmod(f_lo, r)
    return w_all, jnp.pad(w8[fb, fr:fr + f_hi - f_lo], ((0, F_ROWS - (f_hi - f_lo)), (0, 0)))


def _unpack_g_in(g_all, g_f, d, tc, blocks):
    n_all, k = g_all.shape
    chunks, (f_lo, f_hi) = _w_in_chunks(d, tc)
    r = max(hi for _, hi in chunks) // blocks
    rp = _padded_rows(r)
    g, w1 = GROUP_BACK, GROUP_BACK + ROW_TILE
    pos, spans = 0, [(f_lo, f_hi, None)]
    for lo, hi in chunks:
        spans.append((lo, hi, pos))
        pos += hi - lo
    spans.sort()
    table = []
    for b in range(blocks):
        for l0 in range(0, rp, g):
            valid = max(0, min(g, r - l0))
            g0, segs, fa, fb, of = b * r + l0, [], 0, 0, 0
            for lo, hi, p in spans:
                a, e = max(lo, g0), min(hi, g0 + valid)
                if a < e and p is None:
                    fa, fb, of = a - g0, e - g0, g + (a - lo) - (a - g0)
                elif a < e:
                    segs.append((a - g0, p + a - lo, e - a))
            assert len(segs) <= 2 and (not segs or segs[0][0] == 0 or len(segs) == 1)
            first = segs[0] if segs and segs[0][0] == 0 else (0, 0, 0)
            second = segs[-1] if segs and segs[-1][0] > 0 else (g, 0, 0)
            st1 = min(first[1] // ROW_TILE * ROW_TILE, n_all - w1)
            o1, o2 = first[1] - st1, g - second[0]
            assert second[1] % GROUP == 0
            table.append((st1, o1 // 8, o1 % 8, first[2], second[0], second[1] // GROUP, o2 // 8, o2 % 8,
                          fa, fb, of // 8, of % 8, valid))
    names = ("st1", "q1", "s1", "n1", "a2", "j2", "q2", "s2", "fa", "fb", "qf", "sf", "valid")
    cols_of = {n: i for i, n in enumerate(names)}
    tbl = np.array(table, np.int32).T
    c, per = cols_of, rp // g
    return _assemble(
        "unpack_g_in", tbl, (blocks, per), lambda: pl.program_id(0) * per + pl.program_id(1),
        [pl.BlockSpec((pl.Element(w1), pl.Element(k)), lambda b, u, t: (pl.multiple_of(t[c["st1"], b * per + u], ROW_TILE), 0)),
         pl.BlockSpec((GROUP, k), lambda b, u, t: (t[c["j2"], b * per + u], 0)),
         pl.BlockSpec((F_ROWS, k), lambda b, u, t: (0, 0))],
        pl.BlockSpec((None, g, k), lambda b, u, t: (b, u, 0)),
        jax.ShapeDtypeStruct((blocks, rp, k), g_all.dtype), [g_all, g_all, g_f], g, w1, cols_of)


def _unblock(w8):
    return w8.transpose(1, 0, 2).reshape(w8.shape[1], -1)


def _tile2(r, cols, tr, tcols):
    if r % 8 == 0:
        return _tile(r, tr), cols
    return r, _tile(cols, tcols)


def _pair_sum(name, g8, got, c, halves=1):
    _, r, cols = g8.shape
    rows = r // halves
    tr, tcols = _tile2(rows, cols, 256, 256)
    per = rows // tr
    out = []
    for part in range(halves):
        def body(c_ref, g_ref, s_ref, o_ref):
            o_ref[...] = (g_ref[...].astype(F32) + s_ref[...].astype(F32)).astype(o_ref.dtype)

        own = pl.BlockSpec((None, tr, tcols), lambda q, i, j, c_ref, part=part: (2 * q + c_ref[0], part * per + i, j))
        src = pl.BlockSpec((None, tr, tcols), lambda q, i, j, c_ref, part=part: (q, part * per + i, j))
        dst = pl.BlockSpec((None, tr, tcols), lambda q, i, j, c_ref: (q, i, j))
        out.append(pl.pallas_call(
            body,
            name=name if halves == 1 else f"{name}_{part}",
            grid_spec=pltpu.PrefetchScalarGridSpec(num_scalar_prefetch=1, grid=(N_CHIPS, per, cols // tcols),
                                                   in_specs=[own, src], out_specs=dst),
            out_shape=jax.ShapeDtypeStruct((N_CHIPS, rows, cols), BF16),
            compiler_params=_params(("parallel",) * 3),
        )(c, g8, got))
    return out[0] if halves == 1 else out


def _local_step(x, mem, target, w, small, comm=None):
    t, d = x.shape
    cw = d // 2
    heads = cw // FOX_HEAD_DIM
    tc = min(512, d)
    tq = min(512, t)
    off_conv, off_fox, off_mq = 3 * d, 3 * d + 3 * cw, 3 * d + 6 * cw
    w, small = dict(w), dict(small)
    big = dict(tm=1024, tn=512, tk=2048)
    wide_k = dict(tm=512, tn=1024, tk=4096)
    tall = dict(tm=2048, tn=512, tk=2048)

    if comm:
        first = _gather_rider([comm["shards"]["w_in"], comm["conv_w"]], True)
        h, w["w_in"], cw8 = _rms_fwd("rms1_fwd", x, small["norm1_g"], rider=first)
        small["conv_w"] = _unblock(cw8)
        w_all, w_f = _pack_w_in(w["w_in"], d, tc)
        early = ("w_out", "w_mem_kv", "w_down")
        proj, *got = _matmul("proj", "nt", h, w_all, outs=[BF16], rider=_gather_rider([comm["shards"][n] for n in early], True), **tall)
        for n, val in zip(early, got):
            w[n] = _unblock(val) if n in COLUMN_SPLIT else val.reshape(-1, val.shape[-1])
    else:
        h = _rms_fwd("rms1_fwd", x, small["norm1_g"])
        w_all, w_f = _pack_w_in(w["w_in"], d, tc)
        proj = _matmul("proj", "nt", h, w_all, outs=[BF16], **tall)
    z_row = _matmul("proj_f", "nt", w_f, h, outs=[F32], tm=F_ROWS, tn=512, tk=2048)

    y_conv = _conv_fwd(proj, off_conv, small["conv_w"], LANES)

    b_col = jnp.pad(small["b_f"], (0, F_ROWS - heads)).reshape(F_ROWS, 1)
    c_row3 = _forget_fwd(z_row, b_col)[:heads].reshape(heads, 1, t)
    c_colb = _rows_to_colb(c_row3, tq)
    if comm:
        later = ("w_up", "w_conv_out", "w_fox_out", "w_mem_out")
        y_fox, lse, *got = _fox_fwd(proj, off_fox, small["fox_q_g"], small["fox_k_g"], c_row3, c_colb, heads, 2 * tq,
                                    rider=_gather_rider([comm["shards"][n] for n in later], True))
        for n, val in zip(later, got):
            w[n] = _unblock(val)
    else:
        y_fox, lse = _fox_fwd(proj, off_fox, small["fox_q_g"], small["fox_k_g"], c_row3, c_colb, heads, 2 * tq)

    nm = _rms_fwd("mem_rms_fwd", mem, small["mem_norm_g"])
    kv = _matmul("mem_kv", "nn", nm, w["w_mem_kv"], outs=[F32], tm=256, tn=512, tk=2048)
    y_mem = _mem_fwd(proj, off_mq, kv, small["mem_q_g"], small["mem_k_g"], tq)

    ys = (y_conv, y_fox, y_mem)
    w_outs = (w["w_conv_out"], w["w_fox_out"], w["w_mem_out"])
    *o3, merged = _merge_fwd(proj, ys, w_outs, 1024, tc)
    def out_epilogue(acc, xr, g2):
        x1r = acc + xr
        r = lax.rsqrt(jnp.mean(x1r * x1r, axis=-1, keepdims=True) + EPS)
        return x1r, x1r * r * g2

    x1, h2 = _matmul("out_proj", "nn", merged, w["w_out"], outs=[F32, BF16], extras=[x, small["norm2_g"].reshape(1, d)],
                     epilogue=out_epilogue, tm=512, tn=d, tk=2048)

    def up_epilogue(acc):
        return acc, jnp.square(jnp.maximum(acc, 0.0))

    up, act = _matmul("mlp_up", "nn", h2, w["w_up"], outs=[BF16, BF16], epilogue=up_epilogue, **big)

    def loss_epilogue(acc, x1r, tr):
        dy = (acc + x1r - tr) * (1.0 / d)
        return dy, dy

    dy, dyb = _matmul("mlp_down", "nn", act, w["w_down"], outs=[F32, BF16], extras=[x1, target],
                      epilogue=loss_epilogue, tm=1024, tn=512, tk=4096)

    def dup_epilogue(acc, upr):
        return (acc * 2.0 * jnp.maximum(upr.astype(F32), 0.0),)

    def by_owner(g):
        return g.reshape(N_DEV, -1, g.shape[-1])

    g, parts = {}, {}
    g["w_down"] = _matmul("d_w_down", "tn", act, dyb, outs=[BF16], **wide_k)
    if comm:
        dup = _matmul("d_act", "nt", dyb, w["w_down"], outs=[BF16], extras=[up], epilogue=dup_epilogue, **tall)
        g["w_up"], got = _matmul("d_w_up", "tn", h2, dup, outs=[BF16], out_blocks=True,
                                 rider=_pair_rider([by_owner(g["w_down"])]), **wide_k)
        pair = _pair_sum("pair_w_down", by_owner(g["w_down"]), got, comm["c"])
        dh2, parts["w_down"], got = _matmul("d_h2", "nt", dup, w["w_up"], outs=[F32],
                                            rider=_join_riders(_chip_rider([pair]), _pair_rider([g["w_up"]])), **tall)
        pair_up = _pair_sum("pair_w_up", g["w_up"], got, comm["c"])
    else:
        dup = _matmul("d_act", "nt", dyb, w["w_down"], outs=[BF16], extras=[up], epilogue=dup_epilogue, **tall)
        g["w_up"] = _matmul("d_w_up", "tn", h2, dup, outs=[BF16], out_blocks=True, **wide_k)
        dh2 = _matmul("d_h2", "nt", dup, w["w_up"], outs=[F32], **tall)
    dx1, dx1b, g_norm2, dy_sq = _rms_bwd("rms2_bwd", dh2, x1, small["norm2_g"], res=dy)
    loss = dy_sq * (0.5 * d)

    g["w_out"] = _matmul("d_w_out", "tn", merged, dx1b, outs=[BF16], **wide_k)
    dproj, *rest = _merge_bwd(proj, o3, dx1b, w["w_out"], w_outs, 512, tc)
    do3, dys = rest[:3], rest[3:]
    names = ("w_conv_out", "w_fox_out", "w_mem_out")
    for s in range(3):
        g[names[s]] = _matmul(f"d_w_branch{s}", "tn", ys[s], do3[s], outs=[BF16], out_blocks=True, **wide_k)

    dproj, dkv, g_mq, g_mk = _mem_bwd(proj, off_mq, kv, dys[2], small["mem_q_g"], small["mem_k_g"], tq, dproj)
    g["w_mem_kv"] = _matmul("d_w_mem_kv", "tn", nm, dkv, outs=[BF16], **wide_k)
    dnm = _matmul("d_mem_norm", "nt", dkv, w["w_mem_kv"], outs=[F32], tm=256, tn=512, tk=2048)
    _, _, g_mem_norm, _ = _rms_bwd("mem_rms_bwd", dnm, mem, small["mem_norm_g"])

    mid = ("w_out", "w_conv_out", "w_fox_out", "w_mem_out", "w_mem_kv")
    if comm:
        mid8 = [g[n] if n in names else by_owner(g[n]) for n in mid]
        dproj, g_conv_w, *got = _conv_bwd(proj, off_conv, small["conv_w"], dys[0], LANES, dproj, rider=_pair_rider(mid8))
        pairs_mid = [_pair_sum("pair_" + n, g8, s4, comm["c"]) for n, g8, s4 in zip(mid, mid8, got)]
        dproj, dc, g_fq, g_fk, parts["w_up"] = _fox_bwd(proj, off_fox, y_fox, dys[1], small["fox_q_g"], small["fox_k_g"], c_row3,
                                                        c_colb, lse, heads, tq, dproj, rider=_chip_rider([pair_up]))
    else:
        dproj, g_conv_w = _conv_bwd(proj, off_conv, small["conv_w"], dys[0], LANES, dproj)
        dproj, dc, g_fq, g_fk = _fox_bwd(proj, off_fox, y_fox, dys[1], small["fox_q_g"], small["fox_k_g"], c_row3, c_colb,
                                         lse, heads, tq, dproj)
    dc_row = jnp.pad(dc.reshape(heads, t), ((0, F_ROWS - heads), (0, 0)))
    dz_row, db = _forget_bwd(z_row, b_col, dc_row)

    if comm:
        g_all, *got = _matmul("d_w_in", "tn", dproj, h, outs=[BF16], j_outer=True, rider=_chip_rider(pairs_mid), **wide_k)
        parts.update(zip(mid, got))
    else:
        g_all = _matmul("d_w_in", "tn", dproj, h, outs=[BF16], j_outer=True, **wide_k)
    g_wf = _matmul("d_w_f", "nn", dz_row, h, outs=[BF16], tm=F_ROWS, tn=512, tk=4096)
    g["w_in"] = _unpack_g_in(g_all, g_wf, d, tc, w["w_in"].shape[0])
    dh = _matmul("d_h_f", "tn", dz_row, w_f, outs=[F32], tm=1024, tn=512, tk=F_ROWS)
    add_prev = lambda acc, prev: (acc + prev,)
    if comm:
        g_in8 = g["w_in"]
        got = _run_rider("pair_exchange_w_in", _pair_rider([g_in8]))[0]
        pair = _pair_sum("pair_w_in", g_in8, got, comm["c"])
        dh, parts["w_in"] = _matmul("d_h", "nn", dproj, w_all, outs=[F32], extras=[dh], epilogue=add_prev,
                                    rider=_chip_rider([pair]), tm=1024, tn=512, tk=3328)
    else:
        dh = _matmul("d_h", "nn", dproj, w_all, outs=[F32], extras=[dh], epilogue=add_prev, tm=1024, tn=512, tk=3328)
    grad_x, _, g_norm1, _ = _rms_bwd("rms1_bwd", dh, x, small["norm1_g"], res=dx1)

    gs = dict(norm1_g=g_norm1, b_f=db[:heads, 0], conv_w=g_conv_w, fox_q_g=g_fq.reshape(-1), fox_k_g=g_fk.reshape(-1),
              mem_norm_g=g_mem_norm, mem_q_g=g_mq, mem_k_g=g_mk, norm2_g=g_norm2)
    return loss, grad_x, (parts if comm else g), gs


def _adamw_math(w, g, m, v):
    m = ADAM_B1 * m + (1.0 - ADAM_B1) * g
    v = ADAM_B2 * v + (1.0 - ADAM_B2) * jnp.square(g)
    m_hat = m / (1.0 - ADAM_B1 ** ADAM_STEP)
    v_hat = v / (1.0 - ADAM_B2 ** ADAM_STEP)
    delta = -ADAM_LR * (m_hat / (jnp.sqrt(v_hat) + ADAM_EPS) + ADAM_WD * w)
    return delta, m, v


def _adamw(name, parts, w, m, v):
    r, c = w.shape
    pieces = list(parts) if isinstance(parts, (list, tuple)) else [parts]
    n_parts = pieces[0].shape[0]
    rp = sum(p.shape[1] for p in pieces)
    if rp == r:
        tr, tc = _tile2(pieces[0].shape[1], c, 128, 256)
    else:
        tr, tc = _tile(rp, 256), _tile(c, 1024)
    per = pieces[0].shape[1] // tr
    n_p = len(pieces)

    def body(*refs):
        p_refs = refs[:n_p]
        w_ref, m_ref, v_ref, g_ref, d_ref, nm_ref, nv_ref = refs[n_p:]
        i = pl.program_id(0)
        g = None
        for a, p_ref in enumerate(p_refs):
            s_a = p_ref[0].astype(F32)
            for s in range(1, n_parts):
                s_a = s_a + p_ref[s].astype(F32)
            g = s_a if g is None else jnp.where(i >= a * per, s_a, g)
        delta, nm, nv = _adamw_math(w_ref[...], g, m_ref[...], v_ref[...])
        g_ref[...] = g
        d_ref[...] = delta
        nm_ref[...] = nm
        nv_ref[...] = nv

    blk = pl.BlockSpec((tr, tc), lambda i, j: (i, j))
    p_specs = [pl.BlockSpec((n_parts, tr, tc), lambda i, j, a=a: (0, jnp.clip(i - a * per, 0, per - 1), j)) for a in range(n_p)]
    return pl.pallas_call(
        body,
        name=name,
        grid=(rp // tr, c // tc),
        in_specs=p_specs + [blk, blk, blk],
        out_specs=[blk] * 4,
        out_shape=[jax.ShapeDtypeStruct((r, c), F32)] * 4,
        compiler_params=_params(("parallel", "parallel")),
    )(*pieces, w, m, v)


def _sum_parts(name, parts):
    n_parts, r, c = parts.shape

    def body(p_ref, o_ref):
        acc = p_ref[0]
        for s in range(1, n_parts):
            acc = acc + p_ref[s]
        o_ref[...] = acc

    return pl.pallas_call(body, name=name, out_shape=jax.ShapeDtypeStruct((r, c), F32))(parts)


BIG = ("w_in", "w_mem_kv", "w_conv_out", "w_fox_out", "w_mem_out", "w_out", "w_up", "w_down")
COLUMN_SPLIT = ("w_in", "w_conv_out", "w_fox_out", "w_mem_out", "w_up")
SMALL = ("norm1_g", "b_f", "conv_w", "fox_q_g", "fox_k_g", "mem_norm_g", "mem_q_g", "mem_k_g", "norm2_g")
WEIGHTS = ("norm1_g", "w_in", "b_f", "conv_w", "fox_q_g", "fox_k_g", "mem_norm_g", "w_mem_kv", "mem_q_g", "mem_k_g",
           "w_conv_out", "w_fox_out", "w_mem_out", "w_out", "norm2_g", "w_up", "w_down")


def _pack(vectors):
    rows = []
    for vec in vectors:
        n = vec.shape[0]
        rows.append(jnp.pad(vec, (0, -n % LANES)).reshape(-1, LANES))
    out = jnp.concatenate(rows, axis=0)
    return jnp.pad(out, ((0, -out.shape[0] % 8), (0, 0)))


def _unpack(packed, sizes):
    out, row = [], 0
    for n in sizes:
        nr = -(-n // LANES)
        out.append(packed[row:row + nr].reshape(-1)[:n])
        row += nr
    return out


def kernel(x, mem, norm1_g, w_in, b_f, conv_w, fox_q_g, fox_k_g, mem_norm_g, w_mem_kv, mem_q_g, mem_k_g, w_conv_out, w_fox_out, w_mem_out, w_out, norm2_g, w_up, w_down, loss_target, m_norm1_g, m_w_in, m_b_f, m_conv_w, m_fox_q_g, m_fox_k_g, m_mem_norm_g, m_w_mem_kv, m_mem_q_g, m_mem_k_g, m_w_conv_out, m_w_fox_out, m_w_mem_out, m_w_out, m_norm2_g, m_w_up, m_w_down, v_norm1_g, v_w_in, v_b_f, v_conv_w, v_fox_q_g, v_fox_k_g, v_mem_norm_g, v_w_mem_kv, v_mem_q_g, v_mem_k_g, v_w_conv_out, v_w_fox_out, v_w_mem_out, v_w_out, v_norm2_g, v_w_up, v_w_down):
    args = dict(locals())
    wts = {n: args[n] for n in WEIGHTS}
    ms = {n: args["m_" + n] for n in WEIGHTS}
    vs = {n: args["v_" + n] for n in WEIGHTS}
    x_pos, y_pos, c_pos = _position()
    me = _index(x_pos, y_pos, c_pos)

    shards = {n: wts[n].astype(BF16) for n in BIG if n != "w_in"}
    rows_in = w_in.shape[1]
    shards["w_in"] = jnp.pad(w_in.T.astype(BF16), ((0, _padded_rows(rows_in) - rows_in), (0, 0)))
    small = {n: wts[n] for n in SMALL if n != "conv_w"}
    comm = {"shards": shards, "conv_w": conv_w, "c": c_pos.astype(jnp.int32).reshape(1)}

    loss, grad_x, parts, gs = _local_step(x[0], mem[0], loss_target[0], {}, small, comm)

    out_g, out_d, out_m, out_v = {}, {}, {}, {}
    for n in BIG:
        if n == "w_in":
            res = _adamw("adamw_" + n, parts[n], wts[n].T, ms[n].T, vs[n].T)
            out_g[n], out_d[n], out_m[n], out_v[n] = (r.T for r in res)
        else:
            out_g[n], out_d[n], out_m[n], out_v[n] = _adamw("adamw_" + n, parts[n], wts[n], ms[n], vs[n])

    small_sizes = [int(math.prod(gs[n].shape)) for n in SMALL]
    packed = _pack([gs[n].reshape(-1) for n in SMALL])
    gsum = _sum_parts("sum_small", _run_rider("exchange_small", _broadcast_rider([packed]))[0])
    gsmall = dict(zip(SMALL, _unpack(gsum, small_sizes)))
    cols = conv_w.shape[1]
    gsmall["conv_w"] = lax.dynamic_slice(gsmall["conv_w"].reshape(CONV_TAPS, -1), (0, me * cols), (CONV_TAPS, cols)).reshape(-1)
    pg, pw, pm, pv = (_pack([src[n].reshape(-1) for n in SMALL]) for src in (gsmall, wts, ms, vs))
    _, sd, sm, sv = _adamw("adamw_small", pg[None], pw, pm, pv)
    local_sizes = [int(math.prod(wts[n].shape)) for n in SMALL]
    for dst, src in ((out_d, sd), (out_m, sm), (out_v, sv)):
        for n, val in zip(SMALL, _unpack(src, local_sizes)):
            dst[n] = val.reshape(wts[n].shape)
    for n in SMALL:
        out_g[n] = gsmall[n].reshape(wts[n].shape)

    loss = lax.psum(loss, MESH_AXES)
    return (loss, grad_x[None], *[out_g[n] for n in WEIGHTS], *[out_d[n] for n in WEIGHTS],
            *[out_m[n] for n in WEIGHTS], *[out_v[n] for n in WEIGHTS])
```

```python
import math

import numpy as np
import jax
import jax.numpy as jnp
from jax import lax
from jax.experimental import pallas as pl
from jax.experimental.pallas import tpu as pltpu

F32 = jnp.float32
BF16 = jnp.bfloat16

EPS = 1e-6
N_DEV = 8
N_CHIPS = 4
FOX_HEAD_DIM = 128
MEM_HEADS = 4
CONV_TAPS = 3
N_BRANCHES = 3
F_ROWS = 16

ADAM_LR = 0.001
ADAM_B1 = 0.9
ADAM_B2 = 0.999
ADAM_EPS = 1e-08
ADAM_WD = 0.01
ADAM_STEP = 10

V7X_VMEM_BYTES = 64 * 1024 * 1024
VMEM_LIMIT = V7X_VMEM_BYTES * 3 // 4
LANES = 128
NEG = -1e30
MAX_KEYS = 1024

MESH = pl.DeviceIdType.MESH
ANY = pl.BlockSpec(memory_space=pl.ANY)

NN = (((1,), (0,)), ((), ()))
NT = (((1,), (1,)), ((), ()))
TN = (((0,), (0,)), ((), ()))


def _params(sem):
    return pltpu.CompilerParams(dimension_semantics=sem, vmem_limit_bytes=VMEM_LIMIT)


def _dot(a, b, dn):
    return lax.dot_general(a, b, dn, preferred_element_type=F32)


def _tile(n, t):
    if n <= t:
        return n
    for step in (LANES, 16):
        for cand in range(t - t % step, 0, -step):
            if n % cand == 0:
                return cand
    raise ValueError((n, t))


class _Rider:
    def __init__(self, ins, out_shapes, sem_shapes, start, finish, middle=None):
        self.ins, self.out_shapes, self.sem_shapes = list(ins), list(out_shapes), list(sem_shapes)
        self.start, self.finish, self.middle = start, finish, middle


def _position():
    return lax.axis_index("x"), lax.axis_index("y"), lax.axis_index("c")


def _index(px, py, pc):
    return 4 * px + 2 * py + pc


def _dma_sems(n, per):
    return [pltpu.SemaphoreType.DMA((n, per)), pltpu.SemaphoreType.DMA((n, per)), pltpu.SemaphoreType.DMA((n,))]


def _gather_rider(shards, pass_on):
    n = len(shards)

    def copies(ins, outs, sems):
        send_sems, recv_sems, local_sems = sems
        x, y, c = _position()
        me, sibling = (x, y, c), (x, y, 1 - c)
        chips = [(1 - x, y), (x, 1 - y), (1 - x, 1 - y)]

        def copy(a, k, block, to, src=None, k_send=None):
            rows = outs[a].at[_index(*block)]
            return pltpu.make_async_remote_copy(
                src_ref=rows if src is None else src, dst_ref=rows,
                send_sem=send_sems.at[a, k if k_send is None else k_send], recv_sem=recv_sems.at[a, k],
                device_id=to, device_id_type=MESH)

        mine = [pltpu.make_async_copy(ins[a], outs[a].at[_index(*me)], local_sems.at[a]) for a in range(n)]
        first = []
        for a in range(n):
            first.append(copy(a, 0, me, sibling, src=ins[a]))
            first += [copy(a, 1 + j, me, (*chips[j], c), src=ins[a]) for j in range(2 if pass_on else 3)]
        return copy, mine, first, me, sibling, chips, c

    def start(ins, outs, sems):
        _, mine, first, *_ = copies(ins, outs, sems)
        for cp in mine + first:
            cp.start()

    def by_kind(c, fn):
        if pass_on:
            pl.when(c == 1)(lambda: fn(0, 1))
            pl.when(c == 0)(lambda: fn(1, 0))
        else:
            fn(0, 1)

    def onward(copy, a, j_on, j_to, chips, c, sibling):
        third = [copy(a, 3, (*chips[j_on], c), (*chips[j_to], c), k_send=7)] if pass_on else []
        return third + [copy(a, 4 + j_on, (*chips[j_on], c), sibling)], [copy(a, 4 + j_to, (*chips[j_to], c), sibling)]

    def middle(ins, outs, sems):
        copy, _, _, me, sibling, chips, c = copies(ins, outs, sems)

        def fn(j_on, j_to):
            for a in range(n):
                for j, after in zip((j_on, j_to), onward(copy, a, j_on, j_to, chips, c, sibling)):
                    copy(a, 1 + j, (*chips[j], c), me).wait_recv()
                    for cp in after:
                        cp.start()

        by_kind(c, fn)

    def finish(ins, outs, sems):
        copy, mine, first, me, sibling, chips, c = copies(ins, outs, sems)

        def fn(j_on, j_to):
            passed = [cp for a in range(n) for after in onward(copy, a, j_on, j_to, chips, c, sibling) for cp in after]
            for a in range(n):
                copy(a, 3, (*chips[2], c), me).wait_recv()
                passed.append(copy(a, 6, (*chips[2], c), sibling))
                passed[-1].start()
            for a in range(n):
                copy(a, 0, sibling, me).wait_recv()
                for j, chip in enumerate(chips):
                    copy(a, 4 + j, (*chip, 1 - c), me).wait_recv()
            for cp in first + passed:
                cp.wait_send()
            for cp in mine:
                cp.wait()

        by_kind(c, fn)

    out_shapes = [jax.ShapeDtypeStruct((N_DEV,) + s.shape, s.dtype) for s in shards]
    return _Rider(shards, out_shapes, _dma_sems(n, 8), start, finish, middle)


def _pair_rider(grads):
    n = len(grads)

    def copies(ins, outs, sems):
        send_sems, recv_sems, _ = sems
        x, y, c = _position()
        return [pltpu.make_async_remote_copy(
            src_ref=ins[a].at[2 * q + (1 - c)], dst_ref=outs[a].at[q],
            send_sem=send_sems.at[a, q], recv_sem=recv_sems.at[a, q], device_id=(x, y, 1 - c), device_id_type=MESH)
            for a in range(n) for q in range(N_CHIPS)]

    def start(ins, outs, sems):
        for cp in copies(ins, outs, sems):
            cp.start()

    def finish(ins, outs, sems):
        cps = copies(ins, outs, sems)
        for cp in cps:
            cp.wait_recv()
        for cp in cps:
            cp.wait_send()

    out_shapes = [jax.ShapeDtypeStruct((N_CHIPS,) + g.shape[1:], g.dtype) for g in grads]
    return _Rider(grads, out_shapes, _dma_sems(n, N_CHIPS), start, finish)


def _chip_rider(parts):
    n = len(parts)

    def copies(ins, outs, sems):
        send_sems, recv_sems, local_sems = sems
        x, y, c = _position()
        q_me = 2 * x + y
        chips = [(1 - x, y), (x, 1 - y), (1 - x, 1 - y)]
        mine = [pltpu.make_async_copy(ins[a].at[q_me], outs[a].at[q_me], local_sems.at[a]) for a in range(n)]
        sends, arrivals = [], []
        for a in range(n):
            for j, (tx, ty) in enumerate(chips):
                q_t = 2 * tx + ty
                sends.append(pltpu.make_async_remote_copy(
                    src_ref=ins[a].at[q_t], dst_ref=outs[a].at[q_me],
                    send_sem=send_sems.at[a, j], recv_sem=recv_sems.at[a, j], device_id=(tx, ty, c), device_id_type=MESH))
                arrivals.append(pltpu.make_async_remote_copy(
                    src_ref=ins[a].at[q_t], dst_ref=outs[a].at[q_t],
                    send_sem=send_sems.at[a, j], recv_sem=recv_sems.at[a, j], device_id=(tx, ty, c), device_id_type=MESH))
        return mine, sends, arrivals

    def start(ins, outs, sems):
        mine, sends, _ = copies(ins, outs, sems)
        for cp in mine + sends:
            cp.start()

    def finish(ins, outs, sems):
        mine, sends, arrivals = copies(ins, outs, sems)
        for cp in arrivals:
            cp.wait_recv()
        for cp in sends:
            cp.wait_send()
        for cp in mine:
            cp.wait()

    out_shapes = [jax.ShapeDtypeStruct(p.shape, p.dtype) for p in parts]
    return _Rider(parts, out_shapes, _dma_sems(n, 3), start, finish)


def _broadcast_rider(values):
    n = len(values)

    def copies(ins, outs, sems):
        send_sems, recv_sems, local_sems = sems
        x, y, c = _position()
        me = _index(x, y, c)

        def peer(k):
            return (1 - x if k & 4 else x, 1 - y if k & 2 else y, 1 - c if k & 1 else c)

        mine = [pltpu.make_async_copy(ins[a], outs[a].at[me], local_sems.at[a]) for a in range(n)]
        sends, arrivals = [], []
        for a in range(n):
            for k in range(1, N_DEV):
                common = dict(send_sem=send_sems.at[a, k - 1], recv_sem=recv_sems.at[a, k - 1], device_id=peer(k), device_id_type=MESH)
                sends.append(pltpu.make_async_remote_copy(src_ref=ins[a], dst_ref=outs[a].at[me], **common))
                arrivals.append(pltpu.make_async_remote_copy(src_ref=ins[a], dst_ref=outs[a].at[_index(*peer(k))], **common))
        return mine, sends, arrivals

    def start(ins, outs, sems):
        mine, sends, _ = copies(ins, outs, sems)
        for cp in mine + sends:
            cp.start()

    def finish(ins, outs, sems):
        mine, sends, arrivals = copies(ins, outs, sems)
        for cp in arrivals:
            cp.wait_recv()
        for cp in sends:
            cp.wait_send()
        for cp in mine:
            cp.wait()

    out_shapes = [jax.ShapeDtypeStruct((N_DEV,) + v.shape, v.dtype) for v in values]
    return _Rider(values, out_shapes, _dma_sems(n, 7), start, finish)


def _join_riders(*riders):
    def each(fn_name, ins, outs, sems):
        i = o = s = 0
        for r in riders:
            n_i, n_o, n_s = len(r.ins), len(r.out_shapes), len(r.sem_shapes)
            if getattr(r, fn_name) is not None:
                getattr(r, fn_name)(ins[i:i + n_i], outs[o:o + n_o], sems[s:s + n_s])
            i, o, s = i + n_i, o + n_o, s + n_s

    middle = (lambda ins, outs, sems: each("middle", ins, outs, sems)) if any(r.middle for r in riders) else None
    return _Rider([a for r in riders for a in r.ins], [a for r in riders for a in r.out_shapes],
                  [a for r in riders for a in r.sem_shapes],
                  lambda ins, outs, sems: each("start", ins, outs, sems),
                  lambda ins, outs, sems: each("finish", ins, outs, sems), middle)


def _run_rider(name, rider):
    n_in, n_out = len(rider.ins), len(rider.out_shapes)

    def body(*refs):
        ins, outs, sems = refs[:n_in], refs[n_in:n_in + n_out], refs[n_in + n_out:]
        rider.start(ins, outs, sems)
        if rider.middle is not None:
            rider.middle(ins, outs, sems)
        rider.finish(ins, outs, sems)

    return pl.pallas_call(
        body, name=name, in_specs=[ANY] * n_in, out_specs=[ANY] * n_out, out_shape=rider.out_shapes,
        scratch_shapes=rider.sem_shapes)(*rider.ins)


class _Host:
    def __init__(self, rider):
        self.rider = rider
        self.n_in = len(rider.ins) if rider else 0
        self.n_out = len(rider.out_shapes) if rider else 0
        self.n_sem = len(rider.sem_shapes) if rider else 0
        self.ins = rider.ins if rider else []
        self.in_specs = [ANY] * self.n_in
        self.out_specs = [ANY] * self.n_out
        self.out_shapes = rider.out_shapes if rider else []
        self.scratch = rider.sem_shapes if rider else []

    def run(self, first, last, ins, outs, sems, compute, midway=None):
        if self.rider is None:
            compute()
            return

        @pl.when(first)
        def _():
            self.rider.start(ins, outs, sems)

        compute()
        if self.rider.middle is not None and midway is not None:
            pl.when(midway)(lambda: self.rider.middle(ins, outs, sems))

        @pl.when(last)
        def _():
            if self.rider.middle is not None and midway is None:
                self.rider.middle(ins, outs, sems)
            self.rider.finish(ins, outs, sems)


def _matmul(name, kind, a, b, *, tm, tn, tk, outs, epilogue=None, extras=(), out_blocks=False, rider=None, j_outer=False):
    if kind == "nn":
        (m, kdim), n = a.shape, b.shape[1]
    elif kind == "nt":
        (m, kdim), n = a.shape, b.shape[0]
    else:
        (kdim, m), n = a.shape, b.shape[1]
    if out_blocks:
        tn = min(tn, n // N_DEV)
    tm, tn, tk = _tile(m, tm), _tile(n, tn), _tile(kdim, tk)
    ni, nj, nk = m // tm, n // tn, kdim // tk

    def spec(shape, fn):
        return pl.BlockSpec(shape, (lambda g0, g1, k: fn(g1, g0, k)) if j_outer else fn)

    a_spec = spec((tk, tm), lambda i, j, k: (k, i)) if kind == "tn" else spec((tm, tk), lambda i, j, k: (i, k))
    b_spec = spec((tn, tk), lambda i, j, k: (j, k)) if kind == "nt" else spec((tk, tn), lambda i, j, k: (k, j))
    dn = {"nn": NN, "nt": NT, "tn": TN}[kind]

    tile_spec = spec((tm, tn), lambda i, j, k: (i, j))
    row_spec = spec((1, tn), lambda i, j, k: (0, j))
    if out_blocks:
        width = n // N_DEV
        r_out = width // tn
        out_shape = [jax.ShapeDtypeStruct((N_DEV, m, width), dt) for dt in outs]
        out_specs = [spec((None, tm, tn), lambda i, j, k: (j // r_out, i, j % r_out)) for _ in outs]
    else:
        out_shape = [jax.ShapeDtypeStruct((m, n), dt) for dt in outs]
        out_specs = [tile_spec for _ in outs]
    n_ex, n_out = len(extras), len(outs)
    host = _Host(rider)
    n_acc = 1 if nk > 1 else 0

    def body(*refs):
        a_ref, b_ref = refs[0], refs[1]
        pos = 2
        ex_refs = refs[pos:pos + n_ex]; pos += n_ex
        r_ins = refs[pos:pos + host.n_in]; pos += host.n_in
        out_refs = refs[pos:pos + n_out]; pos += n_out
        r_outs = refs[pos:pos + host.n_out]; pos += host.n_out
        acc_ref = refs[pos] if n_acc else None
        sems = refs[pos + n_acc:]
        i, j, k = pl.program_id(1 if j_outer else 0), pl.program_id(0 if j_outer else 1), pl.program_id(2)

        def finish_tile(acc):
            vals = (acc,) if epilogue is None else epilogue(acc, *[e[...] for e in ex_refs])
            for o_ref, v in zip(out_refs, vals):
                o_ref[...] = v.astype(o_ref.dtype)

        def compute():
            part = _dot(a_ref[...], b_ref[...], dn)
            if nk == 1:
                finish_tile(part)
                return

            @pl.when(k == 0)
            def _():
                acc_ref[...] = part

            @pl.when(jnp.logical_and(k > 0, k < nk - 1))
            def _():
                acc_ref[...] += part

            @pl.when(k == nk - 1)
            def _():
                finish_tile(acc_ref[...] + part)

        first = jnp.logical_and(jnp.logical_and(i == 0, j == 0), k == 0)
        last = jnp.logical_and(jnp.logical_and(i == ni - 1, j == nj - 1), k == nk - 1)
        step = (pl.program_id(0) * (ni if j_outer else nj) + pl.program_id(1)) * nk + k
        host.run(first, last, r_ins, r_outs, sems, compute, midway=step == (ni * nj * nk * 3) // 5)

    sem = ("arbitrary",) * 3 if rider else ("parallel", "parallel", "arbitrary")
    res = pl.pallas_call(
        body,
        name=name,
        grid=(nj, ni, nk) if j_outer else (ni, nj, nk),
        in_specs=[a_spec, b_spec] + [row_spec if e.shape[0] == 1 else tile_spec for e in extras] + host.in_specs,
        out_specs=out_specs + host.out_specs,
        out_shape=out_shape + host.out_shapes,
        scratch_shapes=([pltpu.VMEM((tm, tn), F32)] if n_acc else []) + host.scratch,
        compiler_params=_params(sem),
    )(a, b, *extras, *host.ins)
    return res[0] if len(res) == 1 else res


def _rms_fwd(name, x, g, tm=512, rider=None):
    t, d = x.shape
    tm = _tile(t, tm)
    n = t // tm
    host = _Host(rider)

    def body(*refs):
        x_ref, g_ref = refs[:2]
        r_ins = refs[2:2 + host.n_in]
        h_ref = refs[2 + host.n_in]
        r_outs = refs[3 + host.n_in:3 + host.n_in + host.n_out]
        sems = refs[3 + host.n_in + host.n_out:]
        i = pl.program_id(0)

        def compute():
            xf = x_ref[...]
            r = lax.rsqrt(jnp.mean(xf * xf, axis=-1, keepdims=True) + EPS)
            h_ref[...] = (xf * r * g_ref[...]).astype(h_ref.dtype)

        host.run(i == 0, i == n - 1, r_ins, r_outs, sems, compute, midway=i == (n * 3) // 5)

    res = pl.pallas_call(
        body,
        name=name,
        grid=(n,),
        in_specs=[pl.BlockSpec((tm, d), lambda i: (i, 0)), pl.BlockSpec((1, d), lambda i: (0, 0))] + host.in_specs,
        out_specs=[pl.BlockSpec((tm, d), lambda i: (i, 0))] + host.out_specs,
        out_shape=[jax.ShapeDtypeStruct((t, d), BF16)] + host.out_shapes,
        scratch_shapes=host.scratch,
        compiler_params=_params(("arbitrary",) if rider else ("parallel",)),
    )(x, g.reshape(1, d), *host.ins)
    return res[0] if len(res) == 1 else res


def _rms_bwd(name, dh, x, g, res=None, tm=256):
    t, d = x.shape
    tm = _tile(t, tm)
    has_res = res is not None

    def body(*refs):
        if has_res:
            dh_ref, x_ref, g_ref, res_ref, dx_ref, dxb_ref, gg_ref, ss_ref = refs
        else:
            dh_ref, x_ref, g_ref, dx_ref, dxb_ref, gg_ref, ss_ref = refs
        i = pl.program_id(0)
        xf = x_ref[...]
        r = lax.rsqrt(jnp.mean(xf * xf, axis=-1, keepdims=True) + EPS)
        xh = xf * r
        dhf = dh_ref[...].astype(F32)
        dxh = dhf * g_ref[...]
        dx = r * (dxh - xh * jnp.mean(dxh * xh, axis=-1, keepdims=True))

        @pl.when(i == 0)
        def _():
            gg_ref[...] = jnp.zeros_like(gg_ref)
            ss_ref[...] = jnp.zeros_like(ss_ref)

        if has_res:
            resf = res_ref[...]
            dx = dx + resf
            ss_ref[...] += jnp.sum(jnp.sum(resf * resf, axis=0, keepdims=True), axis=1, keepdims=True)
        dx_ref[...] = dx
        dxb_ref[...] = dx.astype(BF16)
        gg_ref[...] += jnp.sum(dhf * xh, axis=0, keepdims=True)

    row = pl.BlockSpec((tm, d), lambda i: (i, 0))
    vec = pl.BlockSpec((1, d), lambda i: (0, 0))
    one = pl.BlockSpec((1, 1), lambda i: (0, 0))
    ins = [dh, x, g.reshape(1, d)] + ([res] if has_res else [])
    dx, dxb, gg, ss = pl.pallas_call(
        body,
        name=name,
        grid=(t // tm,),
        in_specs=[row, row, vec] + ([row] if has_res else []),
        out_specs=[row, row, vec, one],
        out_shape=[jax.ShapeDtypeStruct((t, d), F32), jax.ShapeDtypeStruct((t, d), BF16), jax.ShapeDtypeStruct((1, d), F32),
                   jax.ShapeDtypeStruct((1, 1), F32)],
        compiler_params=_params(("arbitrary",)),
    )(*ins)
    return dx, dxb, gg.reshape(d), ss[0, 0]


def _head_rms(xf):
    r = lax.rsqrt(jnp.mean(xf * xf, axis=-1, keepdims=True) + EPS)
    return xf * r, r


def _head_rms_bwd(dy, xn, r, g):
    dxh = dy * g
    dx = r * (dxh - xn * jnp.mean(dxh * xn, axis=-1, keepdims=True))
    return dx, jnp.sum(dy * xn, axis=0, keepdims=True)


def _col_to_row(col):
    n = col.shape[0]
    eye = lax.broadcasted_iota(jnp.int32, (n, n), 0) == lax.broadcasted_iota(jnp.int32, (n, n), 1)
    return jnp.sum(jnp.where(eye, col, 0.0), axis=0, keepdims=True)


def _row_to_col(row):
    n = row.shape[1]
    eye = lax.broadcasted_iota(jnp.int32, (n, n), 0) == lax.broadcasted_iota(jnp.int32, (n, n), 1)
    return jnp.sum(jnp.where(eye, row, 0.0), axis=1, keepdims=True)


def _dproj_args(dproj, n_in):
    if dproj is None:
        return [], [], {}
    return [dproj], [ANY], {n_in: 0}


def _shift_down(u, s, rows):
    return jnp.where(rows >= s, pltpu.roll(u, s, axis=0), 0.0)


def _shift_up(u, s, rows, t):
    return jnp.where(rows < t - s, pltpu.roll(u, t - s, axis=0), 0.0)


def _conv_fwd(proj, off, conv_w, cb):
    t = proj.shape[0]
    c = conv_w.shape[1]
    blk0 = off // (3 * cb)

    def body(p_ref, w_ref, y_ref):
        rows = lax.broadcasted_iota(jnp.int32, (t, cb), 0)
        bg = p_ref[:, 0:cb].astype(F32)
        u = p_ref[:, cb:2 * cb].astype(F32) * p_ref[:, 2 * cb:3 * cb].astype(F32)
        w = w_ref[...]
        conv = w[2:3] * u + w[1:2] * _shift_down(u, 1, rows) + w[0:1] * _shift_down(u, 2, rows)
        y_ref[...] = (bg * conv).astype(y_ref.dtype)

    return pl.pallas_call(
        body,
        name="conv_fwd",
        grid=(c // cb,),
        in_specs=[pl.BlockSpec((t, 3 * cb), lambda j: (0, blk0 + j)), pl.BlockSpec((CONV_TAPS, cb), lambda j: (0, j))],
        out_specs=pl.BlockSpec((t, cb), lambda j: (0, j)),
        out_shape=jax.ShapeDtypeStruct((t, c), BF16),
        compiler_params=_params(("parallel",)),
    )(proj, conv_w)


def _conv_bwd(proj, off, conv_w, dy, cb, dproj, rider=None):
    t = proj.shape[0]
    c = conv_w.shape[1]
    blk0 = off // (3 * cb)
    nj = c // cb
    host = _Host(rider)

    def body(*refs):
        p_ref, w_ref, dy_ref = refs[:3]
        r_ins = refs[4:4 + host.n_in]
        dp_ref, gw_ref = refs[4 + host.n_in:6 + host.n_in]
        r_outs = refs[6 + host.n_in:6 + host.n_in + host.n_out]
        sems = refs[6 + host.n_in + host.n_out:]
        j = pl.program_id(0)

        def compute():
            rows = lax.broadcasted_iota(jnp.int32, (t, cb), 0)
            bg = p_ref[:, 0:cb].astype(F32)
            cg = p_ref[:, cb:2 * cb].astype(F32)
            v = p_ref[:, 2 * cb:3 * cb].astype(F32)
            u = cg * v
            w = w_ref[...]
            u1 = _shift_down(u, 1, rows)
            u2 = _shift_down(u, 2, rows)
            conv = w[2:3] * u + w[1:2] * u1 + w[0:1] * u2
            dyf = dy_ref[...].astype(F32)
            dconv = dyf * bg
            du = w[2:3] * dconv + w[1:2] * _shift_up(dconv, 1, rows, t) + w[0:1] * _shift_up(dconv, 2, rows, t)
            dp_ref[:, 0:cb] = (dyf * conv).astype(dp_ref.dtype)
            dp_ref[:, cb:2 * cb] = (du * v).astype(dp_ref.dtype)
            dp_ref[:, 2 * cb:3 * cb] = (du * cg).astype(dp_ref.dtype)
            gw_ref[0:1, :] = jnp.sum(dconv * u2, axis=0, keepdims=True)
            gw_ref[1:2, :] = jnp.sum(dconv * u1, axis=0, keepdims=True)
            gw_ref[2:3, :] = jnp.sum(dconv * u, axis=0, keepdims=True)

        host.run(j == 0, j == nj - 1, r_ins, r_outs, sems, compute)

    res = pl.pallas_call(
        body,
        name="conv_bwd",
        grid=(nj,),
        in_specs=[
            pl.BlockSpec((t, 3 * cb), lambda j: (0, blk0 + j)),
            pl.BlockSpec((CONV_TAPS, cb), lambda j: (0, j)),
            pl.BlockSpec((t, cb), lambda j: (0, j)),
            ANY,
        ] + host.in_specs,
        out_specs=[pl.BlockSpec((t, 3 * cb), lambda j: (0, blk0 + j)), pl.BlockSpec((CONV_TAPS, cb), lambda j: (0, j))] + host.out_specs,
        out_shape=[jax.ShapeDtypeStruct(dproj.shape, dproj.dtype), jax.ShapeDtypeStruct((CONV_TAPS, c), F32)] + host.out_shapes,
        input_output_aliases={3: 0},
        scratch_shapes=host.scratch,
        compiler_params=_params(("arbitrary",)),
    )(proj, conv_w, dy, dproj, *host.ins)
    return res


def _lane_scan(x, reverse):
    lane = lax.broadcasted_iota(jnp.int32, x.shape, 1)
    s = 1
    while s < LANES:
        if reverse:
            x = x + jnp.where(lane < LANES - s, pltpu.roll(x, LANES - s, axis=1), 0.0)
        else:
            x = x + jnp.where(lane >= s, pltpu.roll(x, s, axis=1), 0.0)
        s *= 2
    return x


def _scan_rows(src_ref, dst_ref, t, reverse, fn=None):
    groups = list(range(t // LANES))
    if reverse:
        groups = groups[::-1]
    carry = None
    for gi in groups:
        sl = slice(gi * LANES, (gi + 1) * LANES)
        blk = src_ref[:, sl]
        if fn is not None:
            blk = fn(blk)
        blk = _lane_scan(blk, reverse)
        if carry is not None:
            blk = blk + carry
        dst_ref[:, sl] = blk
        carry = blk[:, 0:1] if reverse else blk[:, LANES - 1:LANES]


def _forget_fwd(z_row, b_col):
    rows, t = z_row.shape

    def body(z_ref, b_ref, c_ref):
        def logf(z):
            zz = z + b_ref[...]
            return jnp.minimum(zz, 0.0) - jnp.log(1.0 + jnp.exp(-jnp.abs(zz)))

        _scan_rows(z_ref, c_ref, t, False, logf)

    return pl.pallas_call(
        body,
        name="forget_fwd",
        out_shape=jax.ShapeDtypeStruct((rows, t), F32),
        compiler_params=pltpu.CompilerParams(vmem_limit_bytes=VMEM_LIMIT),
    )(z_row, b_col)


def _rows_to_colb(c_row3, tq):
    heads, _, t = c_row3.shape

    def body(r_ref, o_ref):
        o_ref[...] = jnp.broadcast_to(_row_to_col(r_ref[...]), (tq, LANES))

    return pl.pallas_call(
        body,
        name="rows_to_colb",
        grid=(heads, t // tq),
        in_specs=[pl.BlockSpec((None, 1, tq), lambda h, i: (h, 0, i))],
        out_specs=pl.BlockSpec((None, tq, LANES), lambda h, i: (h, i, 0)),
        out_shape=jax.ShapeDtypeStruct((heads, t, LANES), F32),
        compiler_params=_params(("parallel", "parallel")),
    )(c_row3)


def _forget_bwd(z_row, b_col, dc_row):
    rows, t = z_row.shape

    def body(z_ref, b_ref, dc_ref, dz_ref, db_ref, tmp_ref):
        _scan_rows(dc_ref, tmp_ref, t, True)
        zz = z_ref[...] + b_ref[...]
        dz = tmp_ref[...] * (1.0 / (1.0 + jnp.exp(zz)))
        dz_ref[...] = dz.astype(dz_ref.dtype)
        db_ref[...] = jnp.sum(dz, axis=1, keepdims=True)

    return pl.pallas_call(
        body,
        name="forget_bwd",
        out_shape=[jax.ShapeDtypeStruct((rows, t), BF16), jax.ShapeDtypeStruct((rows, 1), F32)],
        scratch_shapes=[pltpu.VMEM((rows, t), F32)],
        compiler_params=pltpu.CompilerParams(vmem_limit_bytes=VMEM_LIMIT),
    )(z_row, b_col, dc_row)


def _fox_fwd(proj, off, gq, gk, c_row3, c_colb, heads, tq, rider=None):
    t = proj.shape[0]
    hd = FOX_HEAD_DIM
    tq = _tile(t, tq)
    nq = t // tq
    blk0 = off // hd
    scale = 1.0 / math.sqrt(hd)
    host = _Host(rider)

    def body(*refs):
        q_ref, k_ref, v_ref, gq_ref, gk_ref, crow_ref, ccol_ref = refs[:7]
        r_ins = refs[7:7 + host.n_in]
        o_ref, lse_ref = refs[7 + host.n_in:9 + host.n_in]
        r_outs = refs[9 + host.n_in:9 + host.n_in + host.n_out]
        khat_ref, v_t_ref = refs[9 + host.n_in + host.n_out:11 + host.n_in + host.n_out]
        sems = refs[11 + host.n_in + host.n_out:]
        h, qi = pl.program_id(0), pl.program_id(1)

        def compute():
            eye = (lax.broadcasted_iota(jnp.int32, (hd, hd), 0) == lax.broadcasted_iota(jnp.int32, (hd, hd), 1)).astype(BF16)

            @pl.when(qi == 0)
            def _():
                kn, _ = _head_rms(k_ref[...].astype(F32))
                khat_ref[...] = (kn * gk_ref[...]).astype(BF16)
                v_t_ref[...] = _dot(eye, v_ref[...], NT).astype(BF16)

            qn, _ = _head_rms(q_ref[...].astype(F32))
            qhat = (qn * (gq_ref[...] * scale)).astype(BF16)
            crow = crow_ref[:, pl.ds(pl.multiple_of(qi * tq, tq), tq)]
            above = lax.broadcasted_iota(jnp.int32, (tq, tq), 1) >= lax.broadcasted_iota(jnp.int32, (tq, tq), 0)

            def tile(j, keys, carry, diagonal):
                m, l, acc_t = carry
                ks = pl.multiple_of(j * keys, keys)
                s_t = _dot(khat_ref[pl.ds(ks, keys), :], qhat, NT) - ccol_ref[pl.ds(ks, keys), 0:1]
                if diagonal:
                    s_t = jnp.where(above, s_t, NEG)
                m_new = jnp.maximum(m, jnp.max(s_t, axis=0, keepdims=True) + crow)
                alpha = jnp.exp(m - m_new)
                p_t = jnp.exp(s_t + (crow - m_new))
                l = alpha * l + jnp.sum(p_t, axis=0, keepdims=True)
                acc_t = alpha * acc_t + _dot(v_t_ref[:, pl.ds(ks, keys)], p_t.astype(BF16), NN)
                return m_new, l, acc_t

            init = (jnp.full((1, tq), NEG, F32), jnp.zeros((1, tq), F32), jnp.zeros((hd, tq), F32))
            pairs = qi // 2 if 2 * tq <= MAX_KEYS else 0
            carry = lax.fori_loop(0, pairs, lambda j, c: tile(j, 2 * tq, c, False), init)
            carry = lax.fori_loop(2 * pairs, qi, lambda j, c: tile(j, tq, c, False), carry)
            m, l, acc_t = tile(qi, tq, carry, True)
            o_ref[...] = _dot((acc_t / l).astype(BF16), eye, TN).astype(o_ref.dtype)
            lse_ref[...] = m + jnp.log(l)

        first = jnp.logical_and(h == 0, qi == 0)
        last = jnp.logical_and(h == heads - 1, qi == nq - 1)
        host.run(first, last, r_ins, r_outs, sems, compute, midway=h * nq + qi == (heads * nq * 3) // 5)

    res = pl.pallas_call(
        body,
        name="fox_fwd",
        grid=(heads, nq),
        in_specs=[
            pl.BlockSpec((tq, hd), lambda h, i: (i, blk0 + 3 * h)),
            pl.BlockSpec((t, hd), lambda h, i: (0, blk0 + 3 * h + 1)),
            pl.BlockSpec((t, hd), lambda h, i: (0, blk0 + 3 * h + 2)),
            pl.BlockSpec((1, hd), lambda h, i: (0, 0)),
            pl.BlockSpec((1, hd), lambda h, i: (0, 0)),
            pl.BlockSpec((None, 1, t), lambda h, i: (h, 0, 0)),
            pl.BlockSpec((None, t, LANES), lambda h, i: (h, 0, 0)),
        ] + host.in_specs,
        out_specs=[pl.BlockSpec((tq, hd), lambda h, i: (i, h)), pl.BlockSpec((None, 1, tq), lambda h, i: (h, 0, i))] + host.out_specs,
        out_shape=[jax.ShapeDtypeStruct((t, heads * hd), BF16), jax.ShapeDtypeStruct((heads, 1, t), F32)] + host.out_shapes,
        scratch_shapes=[pltpu.VMEM((t, hd), BF16), pltpu.VMEM((hd, t), BF16)] + host.scratch,
        compiler_params=_params(("arbitrary", "arbitrary")),
    )(proj, proj, proj, gq.reshape(1, hd), gk.reshape(1, hd), c_row3, c_colb, *host.ins)
    return res


def _fox_bwd(proj, off, o, do, gq, gk, c_row3, c_colb, lse, heads, tq, dproj, rider=None):
    t = proj.shape[0]
    hd = FOX_HEAD_DIM
    tq = _tile(t, tq)
    nb = t // tq
    blk0 = off // hd
    scale = 1.0 / math.sqrt(hd)
    host = _Host(rider)
    n_fixed_in = 11

    def body(*refs):
        q_ref, k_ref, v_ref, o_ref, do_ref, gq_ref, gk_ref, crow_ref, ccol_ref, lse_ref = refs[:10]
        pos = n_fixed_in
        r_ins = refs[pos:pos + host.n_in]; pos += host.n_in
        dp_ref, dc_ref, ggq_ref, ggk_ref = refs[pos:pos + 4]; pos += 4
        r_outs = refs[pos:pos + host.n_out]; pos += host.n_out
        qhat_ref, khat_ref, khat_t_ref, dq_t_ref, dk_ref, dcq_ref, dck_ref, delta_ref = refs[pos:pos + 8]; pos += 8
        sems = refs[pos:]
        h = pl.program_id(0)

        def compute():
            qn, rq = _head_rms(q_ref[...].astype(F32))
            qhat_ref[...] = (qn * (gq_ref[...] * scale)).astype(BF16)
            kn, rk = _head_rms(k_ref[...].astype(F32))
            khat_ref[...] = (kn * gk_ref[...]).astype(BF16)
            eye = (lax.broadcasted_iota(jnp.int32, (hd, hd), 0) == lax.broadcasted_iota(jnp.int32, (hd, hd), 1)).astype(BF16)
            khat_t_ref[...] = _dot(eye, khat_ref[...], NT).astype(BF16)
            delta = jnp.sum(do_ref[...].astype(F32) * o_ref[...].astype(F32), axis=-1, keepdims=True)
            for b in range(nb):
                sl = slice(b * tq, (b + 1) * tq)
                delta_ref[:, sl] = _col_to_row(delta[sl, :])
            dq_t_ref[...] = jnp.zeros_like(dq_t_ref)
            dcq_ref[...] = jnp.zeros_like(dcq_ref)
            above = lax.broadcasted_iota(jnp.int32, (tq, tq), 1) >= lax.broadcasted_iota(jnp.int32, (tq, tq), 0)

            def kv_block(j, _):
                ks = pl.multiple_of(j * tq, tq)
                kh = khat_ref[pl.ds(ks, tq), :]
                kh_t = khat_t_ref[:, pl.ds(ks, tq)]
                vv = v_ref[pl.ds(ks, tq), :]
                ccol = ccol_ref[pl.ds(ks, tq), 0:1]

                def q_block(i, n, carry, diagonal):
                    dk, dv, dck = carry
                    qs = pl.multiple_of(i * tq, tq)
                    qh = qhat_ref[pl.ds(qs, n), :]
                    dob = do_ref[pl.ds(qs, n), :]
                    s_t = _dot(kh, qh, NT) + ((crow_ref[:, pl.ds(qs, n)] - lse_ref[:, pl.ds(qs, n)]) - ccol)
                    p_t = jnp.exp(s_t)
                    if diagonal:
                        p_t = jnp.where(above, p_t, 0.0)
                    ds_t = p_t * (_dot(vv, dob, NT) - delta_ref[:, pl.ds(qs, n)])
                    dsb = ds_t.astype(BF16)
                    dv = dv + _dot(p_t.astype(BF16), dob, NN)
                    dk = dk + _dot(dsb, qh, NN)
                    dq_t_ref[:, pl.ds(qs, n)] += _dot(kh_t, dsb, NN)
                    dcq_ref[:, pl.ds(qs, n)] += jnp.sum(ds_t, axis=0, keepdims=True)
                    dck = dck + jnp.sum(ds_t, axis=-1, keepdims=True)
                    return dk, dv, dck

                zero = jnp.zeros((tq, hd), F32)
                carry = q_block(j, tq, (zero, zero, jnp.zeros((tq, 1), F32)), True)
                pairs = (nb - 1 - j) // 2 if 2 * tq <= MAX_KEYS else 0
                carry = lax.fori_loop(0, pairs, lambda p, c: q_block(j + 1 + 2 * p, 2 * tq, c, False), carry)
                dk, dv, dck = lax.fori_loop(j + 1 + 2 * pairs, nb, lambda i, c: q_block(i, tq, c, False), carry)
                dk_ref[pl.ds(ks, tq), :] = dk
                dp_ref[pl.ds(ks, tq), 2 * hd:3 * hd] = dv.astype(dp_ref.dtype)
                dck_ref[pl.ds(ks, tq), :] = dck
                return 0

            lax.fori_loop(0, nb, kv_block, 0)

            dq, ggq = _head_rms_bwd(dq_t_ref[...].T * scale, qn, rq, gq_ref[...])
            dk, ggk = _head_rms_bwd(dk_ref[...], kn, rk, gk_ref[...])
            dp_ref[:, 0:hd] = dq.astype(dp_ref.dtype)
            dp_ref[:, hd:2 * hd] = dk.astype(dp_ref.dtype)
            for b in range(nb):
                sl = slice(b * tq, (b + 1) * tq)
                dc_ref[:, sl] = dcq_ref[:, sl] - _col_to_row(dck_ref[sl, :])

            @pl.when(h == 0)
            def _():
                ggq_ref[...] = jnp.zeros_like(ggq_ref)
                ggk_ref[...] = jnp.zeros_like(ggk_ref)

            ggq_ref[...] += ggq
            ggk_ref[...] += ggk

        host.run(h == 0, h == heads - 1, r_ins, r_outs, sems, compute)

    head_in = lambda part: pl.BlockSpec((t, hd), lambda h: (0, blk0 + 3 * h + part))
    vec = pl.BlockSpec((1, hd), lambda h: (0, 0))
    colb = pl.BlockSpec((None, t, LANES), lambda h: (h, 0, 0))
    res = pl.pallas_call(
        body,
        name="fox_bwd",
        grid=(heads,),
        in_specs=[
            head_in(0), head_in(1), head_in(2),
            pl.BlockSpec((t, hd), lambda h: (0, h)),
            pl.BlockSpec((t, hd), lambda h: (0, h)),
            vec, vec,
            pl.BlockSpec((None, 1, t), lambda h: (h, 0, 0)),
            colb,
            pl.BlockSpec((None, 1, t), lambda h: (h, 0, 0)),
            ANY,
        ] + host.in_specs,
        out_specs=[
            pl.BlockSpec((t, 3 * hd), lambda h: (0, blk0 // 3 + h)),
            pl.BlockSpec((None, 1, t), lambda h: (h, 0, 0)),
            vec, vec,
        ] + host.out_specs,
        out_shape=[
            jax.ShapeDtypeStruct(dproj.shape, dproj.dtype),
            jax.ShapeDtypeStruct((heads, 1, t), F32),
            jax.ShapeDtypeStruct((1, hd), F32),
            jax.ShapeDtypeStruct((1, hd), F32),
        ] + host.out_shapes,
        input_output_aliases={10: 0},
        scratch_shapes=[
            pltpu.VMEM((t, hd), BF16), pltpu.VMEM((t, hd), BF16), pltpu.VMEM((hd, t), BF16),
            pltpu.VMEM((hd, t), F32), pltpu.VMEM((t, hd), F32),
            pltpu.VMEM((1, t), F32), pltpu.VMEM((t, 1), F32), pltpu.VMEM((1, t), F32),
        ] + host.scratch,
        compiler_params=_params(("arbitrary",)),
    )(proj, proj, proj, o, do, gq.reshape(1, hd), gk.reshape(1, hd), c_row3, c_colb, lse, dproj, *host.ins)
    return res


def _mem_fwd(proj, off, kv, gq, gk, tq):
    t = proj.shape[0]
    m, width = kv.shape[0], kv.shape[1] // 2
    hd = width // MEM_HEADS
    tq = _tile(t, tq)
    blk0 = off // hd
    scale = 1.0 / math.sqrt(hd)

    def body(q_ref, k_ref, v_ref, gq_ref, gk_ref, o_ref):
        qn, _ = _head_rms(q_ref[...].astype(F32))
        kn, _ = _head_rms(k_ref[...])
        s = _dot((qn * gq_ref[...]).astype(BF16), (kn * gk_ref[...]).astype(BF16), NT) * scale
        p = jnp.exp(s - jnp.max(s, axis=-1, keepdims=True))
        p = p / jnp.sum(p, axis=-1, keepdims=True)
        o_ref[...] = _dot(p.astype(BF16), v_ref[...].astype(BF16), NN).astype(o_ref.dtype)

    vec = pl.BlockSpec((1, hd), lambda h, i: (0, 0))
    return pl.pallas_call(
        body,
        name="mem_fwd",
        grid=(MEM_HEADS, t // tq),
        in_specs=[
            pl.BlockSpec((tq, hd), lambda h, i: (i, blk0 + h)),
            pl.BlockSpec((m, hd), lambda h, i: (0, h)),
            pl.BlockSpec((m, hd), lambda h, i: (0, MEM_HEADS + h)),
            vec, vec,
        ],
        out_specs=pl.BlockSpec((tq, hd), lambda h, i: (i, h)),
        out_shape=jax.ShapeDtypeStruct((t, width), BF16),
        compiler_params=_params(("parallel", "parallel")),
    )(proj, kv, kv, gq.reshape(1, hd), gk.reshape(1, hd))


def _mem_bwd(proj, off, kv, do, gq, gk, tq, dproj, rider=None):
    t = proj.shape[0]
    m, width = kv.shape[0], kv.shape[1] // 2
    hd = width // MEM_HEADS
    tq = _tile(t, tq)
    nq = t // tq
    blk0 = off // hd
    scale = 1.0 / math.sqrt(hd)
    host = _Host(rider)

    def body(*refs):
        q_ref, k_ref, v_ref, do_ref, gq_ref, gk_ref = refs[:6]
        pos = 7
        r_ins = refs[pos:pos + host.n_in]; pos += host.n_in
        dq_ref, dk_ref, dv_ref, ggq_ref, ggk_ref = refs[pos:pos + 5]; pos += 5
        r_outs = refs[pos:pos + host.n_out]; pos += host.n_out
        dkh_ref, dvh_ref = refs[pos:pos + 2]; pos += 2
        sems = refs[pos:]
        h, i = pl.program_id(0), pl.program_id(1)

        def compute():
            qn, rq = _head_rms(q_ref[...].astype(F32))
            kn, rk = _head_rms(k_ref[...])
            qhat = (qn * gq_ref[...]).astype(BF16)
            khat = (kn * gk_ref[...]).astype(BF16)
            vb = v_ref[...].astype(BF16)
            dob = do_ref[...]
            s = _dot(qhat, khat, NT) * scale
            p = jnp.exp(s - jnp.max(s, axis=-1, keepdims=True))
            p = p / jnp.sum(p, axis=-1, keepdims=True)
            dp = _dot(dob, vb, NT)
            ds = p * (dp - jnp.sum(dp * p, axis=-1, keepdims=True))
            dsb = ds.astype(BF16)
            dq, ggq = _head_rms_bwd(_dot(dsb, khat, NN) * scale, qn, rq, gq_ref[...])
            dq_ref[...] = dq.astype(dq_ref.dtype)

            @pl.when(i == 0)
            def _():
                dkh_ref[...] = jnp.zeros_like(dkh_ref)
                dvh_ref[...] = jnp.zeros_like(dvh_ref)

            @pl.when(jnp.logical_and(h == 0, i == 0))
            def _():
                ggq_ref[...] = jnp.zeros_like(ggq_ref)
                ggk_ref[...] = jnp.zeros_like(ggk_ref)

            dkh_ref[...] += _dot(dsb, qhat, TN)
            dvh_ref[...] += _dot(p.astype(BF16), dob, TN)
            ggq_ref[...] += ggq

            @pl.when(i == nq - 1)
            def _():
                dk, ggk = _head_rms_bwd(dkh_ref[...] * scale, kn, rk, gk_ref[...])
                dk_ref[...] = dk.astype(dk_ref.dtype)
                dv_ref[...] = dvh_ref[...].astype(dv_ref.dtype)
                ggk_ref[...] += ggk

        first = jnp.logical_and(h == 0, i == 0)
        last = jnp.logical_and(h == MEM_HEADS - 1, i == nq - 1)
        host.run(first, last, r_ins, r_outs, sems, compute)

    vec = pl.BlockSpec((1, hd), lambda h, i: (0, 0))
    kblk = pl.BlockSpec((m, hd), lambda h, i: (0, h))
    res = pl.pallas_call(
        body,
        name="mem_bwd",
        grid=(MEM_HEADS, nq),
        in_specs=[
            pl.BlockSpec((tq, hd), lambda h, i: (i, blk0 + h)), kblk,
            pl.BlockSpec((m, hd), lambda h, i: (0, MEM_HEADS + h)),
            pl.BlockSpec((tq, hd), lambda h, i: (i, h)), vec, vec, ANY,
        ] + host.in_specs,
        out_specs=[pl.BlockSpec((tq, hd), lambda h, i: (i, blk0 + h)), kblk, kblk, vec, vec] + host.out_specs,
        out_shape=[
            jax.ShapeDtypeStruct(dproj.shape, dproj.dtype),
            jax.ShapeDtypeStruct((m, width), BF16),
            jax.ShapeDtypeStruct((m, width), BF16),
            jax.ShapeDtypeStruct((1, hd), F32),
            jax.ShapeDtypeStruct((1, hd), F32),
        ] + host.out_shapes,
        input_output_aliases={6: 0},
        scratch_shapes=[pltpu.VMEM((m, hd), F32), pltpu.VMEM((m, hd), F32)] + host.scratch,
        compiler_params=_params(("arbitrary", "arbitrary")),
    )(proj, kv, kv, do, gq.reshape(1, hd), gk.reshape(1, hd), dproj, *host.ins)
    dproj, dk, dv, ggq, ggk = res[:5]
    return (dproj, jnp.concatenate([dk, dv], axis=1), ggq.reshape(hd), ggk.reshape(hd), *res[5:])


def _sigmoid(z):
    return 1.0 / (1.0 + jnp.exp(-z))


def _merge_fwd(proj, ys, ws, tm, tc):
    t, cw = ys[0].shape
    d = ws[0].shape[1]
    tm = _tile(t, tm)

    def body(g_ref, ya_ref, yb_ref, yc_ref, wa_ref, wb_ref, wc_ref, oa_ref, ob_ref, oc_ref, out_ref):
        acc = jnp.zeros((tm, tc), F32)
        for s, (y_ref, w_ref, o_ref) in enumerate(((ya_ref, wa_ref, oa_ref), (yb_ref, wb_ref, ob_ref), (yc_ref, wc_ref, oc_ref))):
            o = _dot(y_ref[...], w_ref[...], NN)
            o_ref[...] = o.astype(o_ref.dtype)
            acc = acc + _sigmoid(g_ref[:, s * tc:(s + 1) * tc].astype(F32)) * o
        out_ref[...] = acc.astype(out_ref.dtype)

    blk = pl.BlockSpec((tm, tc), lambda i, j: (i, j))
    y_spec = pl.BlockSpec((tm, cw), lambda i, j: (i, 0))
    w_spec = pl.BlockSpec((cw, tc), lambda i, j: (0, j))
    return pl.pallas_call(
        body,
        name="merge_fwd",
        grid=(t // tm, d // tc),
        in_specs=[pl.BlockSpec((tm, 3 * tc), lambda i, j: (i, j))] + [y_spec] * 3 + [w_spec] * 3,
        out_specs=[blk] * 4,
        out_shape=[jax.ShapeDtypeStruct((t, d), BF16)] * 4,
        compiler_params=_params(("parallel", "parallel")),
    )(proj, *ys, *ws)


def _merge_bwd(proj, o3, dx1, w_out, ws, tm, tc):
    t, d = o3[0].shape
    k = dx1.shape[1]
    cw = ws[0].shape[0]
    tm = _tile(t, tm)
    ni, nj = t // tm, d // tc

    def body(dx_ref, w_ref, g_ref, oa_ref, ob_ref, oc_ref, wa_ref, wb_ref, wc_ref,
             dg_ref, da_ref, db_ref, dc_ref, ya_ref, yb_ref, yc_ref, acc_ref):
        j = pl.program_id(1)
        dmf = _dot(dx_ref[...], w_ref[...], NT)
        branches = ((oa_ref, da_ref, wa_ref, ya_ref), (ob_ref, db_ref, wb_ref, yb_ref), (oc_ref, dc_ref, wc_ref, yc_ref))
        for s, (o_ref, do_ref, ws_ref, dy_ref) in enumerate(branches):
            g = _sigmoid(g_ref[:, s * tc:(s + 1) * tc].astype(F32))
            do = (dmf * g).astype(do_ref.dtype)
            do_ref[...] = do
            dg_ref[:, s * tc:(s + 1) * tc] = (dmf * o_ref[...].astype(F32) * g * (1.0 - g)).astype(dg_ref.dtype)
            part = _dot(do, ws_ref[...], NT)

            @pl.when(j == 0)
            def _():
                acc_ref[s] = part

            @pl.when(j > 0)
            def _():
                acc_ref[s] += part

            @pl.when(j == nj - 1)
            def _():
                dy_ref[...] = acc_ref[s].astype(dy_ref.dtype)

    blk = pl.BlockSpec((tm, tc), lambda i, j: (i, j))
    wide = pl.BlockSpec((tm, 3 * tc), lambda i, j: (i, j))
    w_spec = pl.BlockSpec((cw, tc), lambda i, j: (0, j))
    y_spec = pl.BlockSpec((tm, cw), lambda i, j: (i, 0))
    return pl.pallas_call(
        body,
        name="merge_bwd",
        grid=(ni, nj),
        in_specs=[pl.BlockSpec((tm, k), lambda i, j: (i, 0)), pl.BlockSpec((tc, k), lambda i, j: (j, 0)), wide, blk, blk, blk] + [w_spec] * 3,
        out_specs=[wide, blk, blk, blk] + [y_spec] * 3,
        out_shape=[jax.ShapeDtypeStruct(proj.shape, BF16)] + [jax.ShapeDtypeStruct((t, d), BF16)] * 3 + [jax.ShapeDtypeStruct((t, cw), BF16)] * 3,
        scratch_shapes=[pltpu.VMEM((3, tm, cw), F32)],
        compiler_params=_params(("parallel", "arbitrary")),
    )(dx1, w_out, proj, *o3, *ws)


def _w_in_chunks(d, tc):
    cw = d // 2
    heads = cw // FOX_HEAD_DIM
    conv0, fox0, f0, mq0, gate0 = 0, 3 * cw, 6 * cw, 6 * cw + heads, 7 * cw + heads
    chunks = [(gate0 + s * d + j * tc, gate0 + s * d + (j + 1) * tc) for j in range(d // tc) for s in range(N_BRANCHES)]
    chunks += [(conv0 + s * cw + j * LANES, conv0 + s * cw + (j + 1) * LANES) for j in range(cw // LANES) for s in range(3)]
    chunks += [(fox0 + s * cw + j * FOX_HEAD_DIM, fox0 + s * cw + (j + 1) * FOX_HEAD_DIM) for j in range(heads) for s in range(3)]
    chunks.append((mq0, mq0 + cw))
    return chunks, (f0, f0 + heads)


ROW_TILE = 16
GROUP = 128
GROUP_BACK = 112
SCRATCH_ROWS = 2 * GROUP + 32


def _padded_rows(r):
    return -(-r // GROUP_BACK) * GROUP_BACK


def _rows_from(scr_ref, use, q8, fine, g):
    x = scr_ref[pl.ds(pl.multiple_of(q8 * 8, 8), g + 8), :]
    for s in range(8):
        @pl.when(fine == s)
        def _(s=s):
            use((x if s == 0 else pltpu.roll(x, g + 8 - s, axis=0))[0:g])


def _assemble(name, tbl, grid, step, in_specs, out_spec, out_shape, operands, g, w1, cols_of):
    has_f = len(in_specs) == 3
    k = out_shape.shape[-1]
    c = cols_of

    def body(*refs):
        t_ref, s1_ref, s2_ref = refs[:3]
        f_ref = refs[3] if has_f else None
        out_ref = refs[3 + has_f]
        scr1, scr2, scrf = refs[4 + has_f:]
        t = step()

        def put(y):
            out_ref[...] = y.astype(out_ref.dtype)

        @pl.when(t == 0)
        def _():
            scr1[...] = jnp.zeros_like(scr1)
            scr2[...] = jnp.zeros_like(scr2)
            scrf[...] = jnp.zeros_like(scrf)

        rows = lax.broadcasted_iota(jnp.int32, (g, k), 0)
        n1, a2 = t_ref[c["n1"], t], t_ref[c["a2"], t]
        scr1[0:w1, :] = (s1_ref[0] if len(s1_ref.shape) == 3 else s1_ref[...]).astype(F32)
        _rows_from(scr1, put, t_ref[c["q1"], t], t_ref[c["s1"], t], g)

        @pl.when(a2 < g)
        def _():
            scr2[g:g + s2_ref.shape[0], :] = s2_ref[...].astype(F32)
            _rows_from(scr2, lambda y: put(jnp.where(rows < n1, out_ref[...].astype(F32), y)),
                       t_ref[c["q2"], t], t_ref[c["s2"], t], g)

        if has_f:
            fa, fb = t_ref[c["fa"], t], t_ref[c["fb"], t]

            @pl.when(fb > fa)
            def _():
                scrf[g:g + f_ref.shape[0], :] = f_ref[...].astype(F32)
                inside = jnp.logical_and(rows >= fa, rows < fb)
                _rows_from(scrf, lambda y: put(jnp.where(inside, y, out_ref[...].astype(F32))),
                           t_ref[c["qf"], t], t_ref[c["sf"], t], g)

            valid = t_ref[c["valid"], t]

            @pl.when(valid < g)
            def _():
                out_ref[...] = jnp.where(rows < valid, out_ref[...].astype(F32), 0.0).astype(out_ref.dtype)

    return pl.pallas_call(
        body,
        name=name,
        grid_spec=pltpu.PrefetchScalarGridSpec(
            num_scalar_prefetch=1, grid=grid, in_specs=in_specs, out_specs=out_spec,
            scratch_shapes=[pltpu.VMEM((SCRATCH_ROWS, k), F32)] * 3),
        out_shape=out_shape,
        compiler_params=_params(("arbitrary",) * len(grid)),
    )(jnp.asarray(tbl), *operands)


def _pack_w_in(w8, d, tc):
    blocks, rp, k = w8.shape
    chunks, (f_lo, f_hi) = _w_in_chunks(d, tc)
    r = max(hi for _, hi in chunks) // blocks
    g, w1 = GROUP, GROUP + ROW_TILE
    table = []
    for lo, hi in chunks:
        for g0 in range(lo, hi, g):
            b1, r1 = divmod(g0, r)
            n1 = min(g, r - r1)
            st1 = min(r1 // ROW_TILE * ROW_TILE, rp - w1)
            o1, o2 = r1 - st1, g - n1
            b2 = b1 + 1 if n1 < g else 0
            table.append((b1, st1, o1 // 8, o1 % 8, n1, n1, b2, o2 // 8, o2 % 8))
    names = ("b1", "st1", "q1", "s1", "n1", "a2", "b2", "q2", "s2")
    cols_of = {n: i for i, n in enumerate(names)}
    tbl = np.array(table, np.int32).T
    c = cols_of
    w_all = _assemble(
        "pack_w_in", tbl, (len(table),), lambda: pl.program_id(0),
        [pl.BlockSpec((pl.Element(1), pl.Element(w1), pl.Element(k)), lambda i, t: (t[c["b1"], i], pl.multiple_of(t[c["st1"], i], ROW_TILE), 0)),
         pl.BlockSpec((None, g, k), lambda i, t: (t[c["b2"], i], 0, 0))],
        pl.BlockSpec((g, k), lambda i, t: (i, 0)),
        jax.ShapeDtypeStruct((len(table) * g, k), w8.dtype), [w8, w8], g, w1, cols_of)
    fb, fr = divmod(f_lo, r)
    return w_all, jnp.pad(w8[fb, fr:fr + f_hi - f_lo], ((0, F_ROWS - (f_hi - f_lo)), (0, 0)))


def _unpack_g_in(g_all, g_f, d, tc, blocks):
    n_all, k = g_all.shape
    chunks, (f_lo, f_hi) = _w_in_chunks(d, tc)
    r = max(hi for _, hi in chunks) // blocks
    rp = _padded_rows(r)
    g, w1 = GROUP_BACK, GROUP_BACK + ROW_TILE
    pos, spans = 0, [(f_lo, f_hi, None)]
    for lo, hi in chunks:
        spans.append((lo, hi, pos))
        pos += hi - lo
    spans.sort()
    table = []
    for b in range(blocks):
        for l0 in range(0, rp, g):
            valid = max(0, min(g, r - l0))
            g0, segs, fa, fb, of = b * r + l0, [], 0, 0, 0
            for lo, hi, p in spans:
                a, e = max(lo, g0), min(hi, g0 + valid)
                if a < e and p is None:
                    fa, fb, of = a - g0, e - g0, g + (a - lo) - (a - g0)
                elif a < e:
                    segs.append((a - g0, p + a - lo, e - a))
            assert len(segs) <= 2 and (not segs or segs[0][0] == 0 or len(segs) == 1)
            first = segs[0] if segs and segs[0][0] == 0 else (0, 0, 0)
            second = segs[-1] if segs and segs[-1][0] > 0 else (g, 0, 0)
            st1 = min(first[1] // ROW_TILE * ROW_TILE, n_all - w1)
            o1, o2 = first[1] - st1, g - second[0]
            assert second[1] % GROUP == 0
            table.append((st1, o1 // 8, o1 % 8, first[2], second[0], second[1] // GROUP, o2 // 8, o2 % 8,
                          fa, fb, of // 8, of % 8, valid))
    names = ("st1", "q1", "s1", "n1", "a2", "j2", "q2", "s2", "fa", "fb", "qf", "sf", "valid")
    cols_of = {n: i for i, n in enumerate(names)}
    tbl = np.array(table, np.int32).T
    c, per = cols_of, rp // g
    return _assemble(
        "unpack_g_in", tbl, (blocks, per), lambda: pl.program_id(0) * per + pl.program_id(1),
        [pl.BlockSpec((pl.Element(w1), pl.Element(k)), lambda b, u, t: (pl.multiple_of(t[c["st1"], b * per + u], ROW_TILE), 0)),
         pl.BlockSpec((GROUP, k), lambda b, u, t: (t[c["j2"], b * per + u], 0)),
         pl.BlockSpec((F_ROWS, k), lambda b, u, t: (0, 0))],
        pl.BlockSpec((None, g, k), lambda b, u, t: (b, u, 0)),
        jax.ShapeDtypeStruct((blocks, rp, k), g_all.dtype), [g_all, g_all, g_f], g, w1, cols_of)


def _unblock(w8):
    return w8.transpose(1, 0, 2).reshape(w8.shape[1], -1)


def _tile2(r, cols, tr, tcols):
    if r % 8 == 0:
        return _tile(r, tr), cols
    return r, _tile(cols, tcols)


def _pair_sum(name, g8, got, c):
    _, r, cols = g8.shape
    tr, tcols = _tile2(r, cols, 256, 256)

    def body(c_ref, g_ref, s_ref, o_ref):
        o_ref[...] = (g_ref[...].astype(F32) + s_ref[...].astype(F32)).astype(o_ref.dtype)

    own = pl.BlockSpec((None, tr, tcols), lambda q, i, j, c_ref: (2 * q + c_ref[0], i, j))
    blk = pl.BlockSpec((None, tr, tcols), lambda q, i, j, c_ref: (q, i, j))
    return pl.pallas_call(
        body,
        name=name,
        grid_spec=pltpu.PrefetchScalarGridSpec(num_scalar_prefetch=1, grid=(N_CHIPS, r // tr, cols // tcols),
                                               in_specs=[own, blk], out_specs=blk),
        out_shape=jax.ShapeDtypeStruct((N_CHIPS, r, cols), BF16),
        compiler_params=_params(("parallel",) * 3),
    )(c, g8, got)


def _local_step(x, mem, target, w, small, comm=None):
    t, d = x.shape
    cw = d // 2
    heads = cw // FOX_HEAD_DIM
    tc = min(512, d)
    tq = min(512, t)
    off_conv, off_fox, off_mq = 3 * d, 3 * d + 3 * cw, 3 * d + 6 * cw
    w, small = dict(w), dict(small)
    big = dict(tm=1024, tn=512, tk=2048)
    wide_k = dict(tm=512, tn=1024, tk=4096)
    tall = dict(tm=2048, tn=512, tk=2048)

    if comm:
        first = _gather_rider([comm["shards"]["w_in"], comm["conv_w"]], True)
        h, w["w_in"], cw8 = _rms_fwd("rms1_fwd", x, small["norm1_g"], rider=first)
        small["conv_w"] = _unblock(cw8)
        w_all, w_f = _pack_w_in(w["w_in"], d, tc)
        early = ("w_out", "w_mem_kv", "w_down")
        proj, *got = _matmul("proj", "nt", h, w_all, outs=[BF16], rider=_gather_rider([comm["shards"][n] for n in early], True), **tall)
        for n, val in zip(early, got):
            w[n] = _unblock(val) if n in COLUMN_SPLIT else val.reshape(-1, val.shape[-1])
    else:
        h = _rms_fwd("rms1_fwd", x, small["norm1_g"])
        w_all, w_f = _pack_w_in(w["w_in"], d, tc)
        proj = _matmul("proj", "nt", h, w_all, outs=[BF16], **tall)
    z_row = _matmul("proj_f", "nt", w_f, h, outs=[F32], tm=F_ROWS, tn=512, tk=2048)

    y_conv = _conv_fwd(proj, off_conv, small["conv_w"], LANES)

    b_col = jnp.pad(small["b_f"], (0, F_ROWS - heads)).reshape(F_ROWS, 1)
    c_row3 = _forget_fwd(z_row, b_col)[:heads].reshape(heads, 1, t)
    c_colb = _rows_to_colb(c_row3, tq)
    if comm:
        later = ("w_up", "w_conv_out", "w_fox_out", "w_mem_out")
        y_fox, lse, *got = _fox_fwd(proj, off_fox, small["fox_q_g"], small["fox_k_g"], c_row3, c_colb, heads, 2 * tq,
                                    rider=_gather_rider([comm["shards"][n] for n in later], True))
        for n, val in zip(later, got):
            w[n] = _unblock(val)
    else:
        y_fox, lse = _fox_fwd(proj, off_fox, small["fox_q_g"], small["fox_k_g"], c_row3, c_colb, heads, 2 * tq)

    nm = _rms_fwd("mem_rms_fwd", mem, small["mem_norm_g"])
    kv = _matmul("mem_kv", "nn", nm, w["w_mem_kv"], outs=[F32], tm=256, tn=512, tk=2048)
    y_mem = _mem_fwd(proj, off_mq, kv, small["mem_q_g"], small["mem_k_g"], tq)

    ys = (y_conv, y_fox, y_mem)
    w_outs = (w["w_conv_out"], w["w_fox_out"], w["w_mem_out"])
    *o3, merged = _merge_fwd(proj, ys, w_outs, 1024, tc)
    def out_epilogue(acc, xr, g2):
        x1r = acc + xr
        r = lax.rsqrt(jnp.mean(x1r * x1r, axis=-1, keepdims=True) + EPS)
        return x1r, x1r * r * g2

    x1, h2 = _matmul("out_proj", "nn", merged, w["w_out"], outs=[F32, BF16], extras=[x, small["norm2_g"].reshape(1, d)],
                     epilogue=out_epilogue, tm=512, tn=d, tk=2048)

    def up_epilogue(acc):
        return acc, jnp.square(jnp.maximum(acc, 0.0))

    up, act = _matmul("mlp_up", "nn", h2, w["w_up"], outs=[BF16, BF16], epilogue=up_epilogue, **big)

    def loss_epilogue(acc, x1r, tr):
        dy = (acc + x1r - tr) * (1.0 / d)
        return dy, dy

    dy, dyb = _matmul("mlp_down", "nn", act, w["w_down"], outs=[F32, BF16], extras=[x1, target],
                      epilogue=loss_epilogue, tm=1024, tn=512, tk=4096)

    def dup_epilogue(acc, upr):
        return (acc * 2.0 * jnp.maximum(upr.astype(F32), 0.0),)

    def by_owner(g):
        return g.reshape(N_DEV, -1, g.shape[-1])

    g, parts = {}, {}
    g["w_down"] = _matmul("d_w_down", "tn", act, dyb, outs=[BF16], **wide_k)
    if comm:
        dup = _matmul("d_act", "nt", dyb, w["w_down"], outs=[BF16], extras=[up], epilogue=dup_epilogue, **tall)
        g["w_up"], got = _matmul("d_w_up", "tn", h2, dup, outs=[BF16], out_blocks=True,
                                 rider=_pair_rider([by_owner(g["w_down"])]), **wide_k)
        pair = _pair_sum("pair_w_down", by_owner(g["w_down"]), got, comm["c"])
        dh2, parts["w_down"], got = _matmul("d_h2", "nt", dup, w["w_up"], outs=[F32],
                                            rider=_join_riders(_chip_rider([pair]), _pair_rider([g["w_up"]])), **tall)
        pair_up = _pair_sum("pair_w_up", g["w_up"], got, comm["c"])
    else:
        dup = _matmul("d_act", "nt", dyb, w["w_down"], outs=[BF16], extras=[up], epilogue=dup_epilogue, **tall)
        g["w_up"] = _matmul("d_w_up", "tn", h2, dup, outs=[BF16], out_blocks=True, **wide_k)
        dh2 = _matmul("d_h2", "nt", dup, w["w_up"], outs=[F32], **tall)
    dx1, dx1b, g_norm2, dy_sq = _rms_bwd("rms2_bwd", dh2, x1, small["norm2_g"], res=dy)
    loss = dy_sq * (0.5 * d)

    g["w_out"] = _matmul("d_w_out", "tn", merged, dx1b, outs=[BF16], **wide_k)
    dproj, *rest = _merge_bwd(proj, o3, dx1b, w["w_out"], w_outs, 512, tc)
    do3, dys = rest[:3], rest[3:]
    names = ("w_conv_out", "w_fox_out", "w_mem_out")
    for s in range(3):
        g[names[s]] = _matmul(f"d_w_branch{s}", "tn", ys[s], do3[s], outs=[BF16], out_blocks=True, **wide_k)

    dproj, dkv, g_mq, g_mk = _mem_bwd(proj, off_mq, kv, dys[2], small["mem_q_g"], small["mem_k_g"], tq, dproj)
    g["w_mem_kv"] = _matmul("d_w_mem_kv", "tn", nm, dkv, outs=[BF16], **wide_k)
    dnm = _matmul("d_mem_norm", "nt", dkv, w["w_mem_kv"], outs=[F32], tm=256, tn=512, tk=2048)
    _, _, g_mem_norm, _ = _rms_bwd("mem_rms_bwd", dnm, mem, small["mem_norm_g"])

    mid = ("w_out", "w_conv_out", "w_fox_out", "w_mem_out", "w_mem_kv")
    if comm:
        mid8 = [g[n] if n in names else by_owner(g[n]) for n in mid]
        dproj, g_conv_w, *got = _conv_bwd(proj, off_conv, small["conv_w"], dys[0], LANES, dproj, rider=_pair_rider(mid8))
        pairs_mid = [_pair_sum("pair_" + n, g8, s4, comm["c"]) for n, g8, s4 in zip(mid, mid8, got)]
        dproj, dc, g_fq, g_fk, parts["w_up"] = _fox_bwd(proj, off_fox, y_fox, dys[1], small["fox_q_g"], small["fox_k_g"], c_row3,
                                                        c_colb, lse, heads, tq, dproj, rider=_chip_rider([pair_up]))
    else:
        dproj, g_conv_w = _conv_bwd(proj, off_conv, small["conv_w"], dys[0], LANES, dproj)
        dproj, dc, g_fq, g_fk = _fox_bwd(proj, off_fox, y_fox, dys[1], small["fox_q_g"], small["fox_k_g"], c_row3, c_colb,
                                         lse, heads, tq, dproj)
    dc_row = jnp.pad(dc.reshape(heads, t), ((0, F_ROWS - heads), (0, 0)))
    dz_row, db = _forget_bwd(z_row, b_col, dc_row)

    if comm:
        g_all, *got = _matmul("d_w_in", "tn", dproj, h, outs=[BF16], j_outer=True, rider=_chip_rider(pairs_mid), **wide_k)
        parts.update(zip(mid, got))
    else:
        g_all = _matmul("d_w_in", "tn", dproj, h, outs=[BF16], j_outer=True, **wide_k)
    g_wf = _matmul("d_w_f", "nn", dz_row, h, outs=[BF16], tm=F_ROWS, tn=512, tk=4096)
    g["w_in"] = _unpack_g_in(g_all, g_wf, d, tc, w["w_in"].shape[0])
    dh = _matmul("d_h_f", "tn", dz_row, w_f, outs=[F32], tm=1024, tn=512, tk=F_ROWS)
    add_prev = lambda acc, prev: (acc + prev,)
    if comm:
        g_in8 = g["w_in"]
        got = _run_rider("pair_exchange_w_in", _pair_rider([g_in8]))[0]
        pair = _pair_sum("pair_w_in", g_in8, got, comm["c"])
        dh, parts["w_in"] = _matmul("d_h", "nn", dproj, w_all, outs=[F32], extras=[dh], epilogue=add_prev,
                                    rider=_chip_rider([pair]), tm=1024, tn=512, tk=3328)
    else:
        dh = _matmul("d_h", "nn", dproj, w_all, outs=[F32], extras=[dh], epilogue=add_prev, tm=1024, tn=512, tk=3328)
    grad_x, _, g_norm1, _ = _rms_bwd("rms1_bwd", dh, x, small["norm1_g"], res=dx1)

    gs = dict(norm1_g=g_norm1, b_f=db[:heads, 0], conv_w=g_conv_w, fox_q_g=g_fq.reshape(-1), fox_k_g=g_fk.reshape(-1),
              mem_norm_g=g_mem_norm, mem_q_g=g_mq, mem_k_g=g_mk, norm2_g=g_norm2)
    return loss, grad_x, (parts if comm else g), gs


def _adamw_math(w, g, m, v):
    m = ADAM_B1 * m + (1.0 - ADAM_B1) * g
    v = ADAM_B2 * v + (1.0 - ADAM_B2) * jnp.square(g)
    m_hat = m / (1.0 - ADAM_B1 ** ADAM_STEP)
    v_hat = v / (1.0 - ADAM_B2 ** ADAM_STEP)
    delta = -ADAM_LR * (m_hat / (jnp.sqrt(v_hat) + ADAM_EPS) + ADAM_WD * w)
    return delta, m, v


def _adamw(name, parts, w, m, v):
    r, c = w.shape
    n_parts, rp = parts.shape[:2]
    if rp == r:
        tr, tc = _tile2(r, c, 128, 256)
    else:
        tr, tc = _tile(rp, 256), _tile(c, 1024)

    def body(p_ref, w_ref, m_ref, v_ref, g_ref, d_ref, nm_ref, nv_ref):
        g = p_ref[0].astype(F32)
        for s in range(1, n_parts):
            g = g + p_ref[s].astype(F32)
        delta, nm, nv = _adamw_math(w_ref[...], g, m_ref[...], v_ref[...])
        g_ref[...] = g
        d_ref[...] = delta
        nm_ref[...] = nm
        nv_ref[...] = nv

    blk = pl.BlockSpec((tr, tc), lambda i, j: (i, j))
    return pl.pallas_call(
        body,
        name=name,
        grid=(rp // tr, c // tc),
        in_specs=[pl.BlockSpec((n_parts, tr, tc), lambda i, j: (0, i, j)), blk, blk, blk],
        out_specs=[blk] * 4,
        out_shape=[jax.ShapeDtypeStruct((r, c), F32)] * 4,
        compiler_params=_params(("parallel", "parallel")),
    )(parts, w, m, v)


def _sum_parts(name, parts):
    n_parts, r, c = parts.shape

    def body(p_ref, o_ref):
        acc = p_ref[0]
        for s in range(1, n_parts):
            acc = acc + p_ref[s]
        o_ref[...] = acc

    return pl.pallas_call(body, name=name, out_shape=jax.ShapeDtypeStruct((r, c), F32))(parts)


BIG = ("w_in", "w_mem_kv", "w_conv_out", "w_fox_out", "w_mem_out", "w_out", "w_up", "w_down")
COLUMN_SPLIT = ("w_in", "w_conv_out", "w_fox_out", "w_mem_out", "w_up")
SMALL = ("norm1_g", "b_f", "conv_w", "fox_q_g", "fox_k_g", "mem_norm_g", "mem_q_g", "mem_k_g", "norm2_g")
WEIGHTS = ("norm1_g", "w_in", "b_f", "conv_w", "fox_q_g", "fox_k_g", "mem_norm_g", "w_mem_kv", "mem_q_g", "mem_k_g",
           "w_conv_out", "w_fox_out", "w_mem_out", "w_out", "norm2_g", "w_up", "w_down")


PACK_UNIT = 8 * LANES


def _pack(vectors):
    return jnp.concatenate([jnp.pad(vec, (0, -vec.shape[0] % PACK_UNIT)).reshape(-1, LANES) for vec in vectors], axis=0)


def _unpack(packed, sizes):
    out, row = [], 0
    for n in sizes:
        nr = -(-n // PACK_UNIT) * 8
        out.append(packed[row:row + nr].reshape(-1)[:n])
        row += nr
    return out


def kernel(x, mem, norm1_g, w_in, b_f, conv_w, fox_q_g, fox_k_g, mem_norm_g, w_mem_kv, mem_q_g, mem_k_g, w_conv_out, w_fox_out, w_mem_out, w_out, norm2_g, w_up, w_down, loss_target, m_norm1_g, m_w_in, m_b_f, m_conv_w, m_fox_q_g, m_fox_k_g, m_mem_norm_g, m_w_mem_kv, m_mem_q_g, m_mem_k_g, m_w_conv_out, m_w_fox_out, m_w_mem_out, m_w_out, m_norm2_g, m_w_up, m_w_down, v_norm1_g, v_w_in, v_b_f, v_conv_w, v_fox_q_g, v_fox_k_g, v_mem_norm_g, v_w_mem_kv, v_mem_q_g, v_mem_k_g, v_w_conv_out, v_w_fox_out, v_w_mem_out, v_w_out, v_norm2_g, v_w_up, v_w_down):
    args = dict(locals())
    wts = {n: args[n] for n in WEIGHTS}
    ms = {n: args["m_" + n] for n in WEIGHTS}
    vs = {n: args["v_" + n] for n in WEIGHTS}
    x_pos, y_pos, c_pos = _position()
    me = _index(x_pos, y_pos, c_pos)

    shards = {n: wts[n].astype(BF16) for n in BIG if n != "w_in"}
    rows_in = w_in.shape[1]
    shards["w_in"] = jnp.pad(w_in.T.astype(BF16), ((0, _padded_rows(rows_in) - rows_in), (0, 0)))
    small = {n: wts[n] for n in SMALL if n != "conv_w"}
    comm = {"shards": shards, "conv_w": conv_w, "c": c_pos.astype(jnp.int32).reshape(1)}

    loss, grad_x, parts, gs = _local_step(x[0], mem[0], loss_target[0], {}, small, comm)

    out_g, out_d, out_m, out_v = {}, {}, {}, {}
    for n in BIG:
        if n == "w_in":
            res = _adamw("adamw_" + n, parts[n], wts[n].T, ms[n].T, vs[n].T)
            out_g[n], out_d[n], out_m[n], out_v[n] = (r.T for r in res)
        else:
            out_g[n], out_d[n], out_m[n], out_v[n] = _adamw("adamw_" + n, parts[n], wts[n], ms[n], vs[n])

    small_sizes = [int(math.prod(gs[n].shape)) for n in SMALL] + [1]
    packed = _pack([gs[n].reshape(-1) for n in SMALL] + [loss.reshape(1)])
    gsum = _sum_parts("sum_small", _run_rider("exchange_small", _broadcast_rider([packed]))[0])
    *small_sums, loss_sum = _unpack(gsum, small_sizes)
    gsmall = dict(zip(SMALL, small_sums))
    cols = conv_w.shape[1]
    gsmall["conv_w"] = lax.dynamic_slice(gsmall["conv_w"].reshape(CONV_TAPS, -1), (0, me * cols), (CONV_TAPS, cols)).reshape(-1)
    pg, pw, pm, pv = (_pack([src[n].reshape(-1) for n in SMALL]) for src in (gsmall, wts, ms, vs))
    _, sd, sm, sv = _adamw("adamw_small", pg[None], pw, pm, pv)
    local_sizes = [int(math.prod(wts[n].shape)) for n in SMALL]
    for dst, src in ((out_d, sd), (out_m, sm), (out_v, sv)):
        for n, val in zip(SMALL, _unpack(src, local_sizes)):
            dst[n] = val.reshape(wts[n].shape)
    for n in SMALL:
        out_g[n] = gsmall[n].reshape(wts[n].shape)

    loss = loss_sum[0]
    return (loss, grad_x[None], *[out_g[n] for n in WEIGHTS], *[out_d[n] for n in WEIGHTS],
            *[out_m[n] for n in WEIGHTS], *[out_v[n] for n in WEIGHTS])
```

```python
import math

import numpy as np
import jax
import jax.numpy as jnp
from jax import lax
from jax.experimental import pallas as pl
from jax.experimental.pallas import tpu as pltpu

F32 = jnp.float32
BF16 = jnp.bfloat16

EPS = 1e-6
N_DEV = 8
N_CHIPS = 4
FOX_HEAD_DIM = 128
MEM_HEADS = 4
CONV_TAPS = 3
N_BRANCHES = 3
F_ROWS = 16

ADAM_LR = 0.001
ADAM_B1 = 0.9
ADAM_B2 = 0.999
ADAM_EPS = 1e-08
ADAM_WD = 0.01
ADAM_STEP = 10

V7X_VMEM_BYTES = 64 * 1024 * 1024
VMEM_LIMIT = V7X_VMEM_BYTES * 3 // 4
LANES = 128
NEG = -1e30
MAX_KEYS = 1024

MESH = pl.DeviceIdType.MESH
ANY = pl.BlockSpec(memory_space=pl.ANY)

NN = (((1,), (0,)), ((), ()))
NT = (((1,), (1,)), ((), ()))
TN = (((0,), (0,)), ((), ()))


def _params(sem):
    return pltpu.CompilerParams(dimension_semantics=sem, vmem_limit_bytes=VMEM_LIMIT)


def _dot(a, b, dn):
    return lax.dot_general(a, b, dn, preferred_element_type=F32)


def _tile(n, t):
    if n <= t:
        return n
    for step in (LANES, 16):
        for cand in range(t - t % step, 0, -step):
            if n % cand == 0:
                return cand
    raise ValueError((n, t))


class _Rider:
    def __init__(self, ins, out_shapes, sem_shapes, start, finish, middle=None):
        self.ins, self.out_shapes, self.sem_shapes = list(ins), list(out_shapes), list(sem_shapes)
        self.start, self.finish, self.middle = start, finish, middle


def _position():
    return lax.axis_index("x"), lax.axis_index("y"), lax.axis_index("c")


def _index(px, py, pc):
    return 4 * px + 2 * py + pc


def _dma_sems(n, per):
    return [pltpu.SemaphoreType.DMA((n, per)), pltpu.SemaphoreType.DMA((n, per)), pltpu.SemaphoreType.DMA((n,))]


def _gather_rider(shards, pass_on):
    n = len(shards)

    def copies(ins, outs, sems):
        send_sems, recv_sems, local_sems = sems
        x, y, c = _position()
        me, sibling = (x, y, c), (x, y, 1 - c)
        chips = [(1 - x, y), (x, 1 - y), (1 - x, 1 - y)]

        def copy(a, k, block, to, src=None, k_send=None):
            rows = outs[a].at[_index(*block)]
            return pltpu.make_async_remote_copy(
                src_ref=rows if src is None else src, dst_ref=rows,
                send_sem=send_sems.at[a, k if k_send is None else k_send], recv_sem=recv_sems.at[a, k],
                device_id=to, device_id_type=MESH)

        mine = [pltpu.make_async_copy(ins[a], outs[a].at[_index(*me)], local_sems.at[a]) for a in range(n)]
        first = []
        for a in range(n):
            first.append(copy(a, 0, me, sibling, src=ins[a]))
            first += [copy(a, 1 + j, me, (*chips[j], c), src=ins[a]) for j in range(2 if pass_on else 3)]
        return copy, mine, first, me, sibling, chips, c

    def start(ins, outs, sems):
        _, mine, first, *_ = copies(ins, outs, sems)
        for cp in mine + first:
            cp.start()

    def by_kind(c, fn):
        if pass_on:
            pl.when(c == 1)(lambda: fn(0, 1))
            pl.when(c == 0)(lambda: fn(1, 0))
        else:
            fn(0, 1)

    def onward(copy, a, j_on, j_to, chips, c, sibling):
        third = [copy(a, 3, (*chips[j_on], c), (*chips[j_to], c), k_send=7)] if pass_on else []
        return third + [copy(a, 4 + j_on, (*chips[j_on], c), sibling)], [copy(a, 4 + j_to, (*chips[j_to], c), sibling)]

    def middle(ins, outs, sems):
        copy, _, _, me, sibling, chips, c = copies(ins, outs, sems)

        def fn(j_on, j_to):
            for a in range(n):
                for j, after in zip((j_on, j_to), onward(copy, a, j_on, j_to, chips, c, sibling)):
                    copy(a, 1 + j, (*chips[j], c), me).wait_recv()
                    for cp in after:
                        cp.start()

        by_kind(c, fn)

    def finish(ins, outs, sems):
        copy, mine, first, me, sibling, chips, c = copies(ins, outs, sems)

        def fn(j_on, j_to):
            passed = [cp for a in range(n) for after in onward(copy, a, j_on, j_to, chips, c, sibling) for cp in after]
            for a in range(n):
                copy(a, 3, (*chips[2], c), me).wait_recv()
                passed.append(copy(a, 6, (*chips[2], c), sibling))
                passed[-1].start()
            for a in range(n):
                copy(a, 0, sibling, me).wait_recv()
                for j, chip in enumerate(chips):
                    copy(a, 4 + j, (*chip, 1 - c), me).wait_recv()
            for cp in first + passed:
                cp.wait_send()
            for cp in mine:
                cp.wait()

        by_kind(c, fn)

    out_shapes = [jax.ShapeDtypeStruct((N_DEV,) + s.shape, s.dtype) for s in shards]
    return _Rider(shards, out_shapes, _dma_sems(n, 8), start, finish, middle)


def _pair_rider(grads):
    n = len(grads)

    def copies(ins, outs, sems):
        send_sems, recv_sems, _ = sems
        x, y, c = _position()
        return [pltpu.make_async_remote_copy(
            src_ref=ins[a].at[2 * q + (1 - c)], dst_ref=outs[a].at[q],
            send_sem=send_sems.at[a, q], recv_sem=recv_sems.at[a, q], device_id=(x, y, 1 - c), device_id_type=MESH)
            for a in range(n) for q in range(N_CHIPS)]

    def start(ins, outs, sems):
        for cp in copies(ins, outs, sems):
            cp.start()

    def finish(ins, outs, sems):
        cps = copies(ins, outs, sems)
        for cp in cps:
            cp.wait_recv()
        for cp in cps:
            cp.wait_send()

    out_shapes = [jax.ShapeDtypeStruct((N_CHIPS,) + g.shape[1:], g.dtype) for g in grads]
    return _Rider(grads, out_shapes, _dma_sems(n, N_CHIPS), start, finish)


def _chip_rider(parts):
    n = len(parts)

    def copies(ins, outs, sems):
        send_sems, recv_sems, local_sems = sems
        x, y, c = _position()
        q_me = 2 * x + y
        chips = [(1 - x, y), (x, 1 - y), (1 - x, 1 - y)]
        mine = [pltpu.make_async_copy(ins[a].at[q_me], outs[a].at[q_me], local_sems.at[a]) for a in range(n)]
        sends, arrivals = [], []
        for a in range(n):
            for j, (tx, ty) in enumerate(chips):
                q_t = 2 * tx + ty
                sends.append(pltpu.make_async_remote_copy(
                    src_ref=ins[a].at[q_t], dst_ref=outs[a].at[q_me],
                    send_sem=send_sems.at[a, j], recv_sem=recv_sems.at[a, j], device_id=(tx, ty, c), device_id_type=MESH))
                arrivals.append(pltpu.make_async_remote_copy(
                    src_ref=ins[a].at[q_t], dst_ref=outs[a].at[q_t],
                    send_sem=send_sems.at[a, j], recv_sem=recv_sems.at[a, j], device_id=(tx, ty, c), device_id_type=MESH))
        return mine, sends, arrivals

    def start(ins, outs, sems):
        mine, sends, _ = copies(ins, outs, sems)
        for cp in mine + sends:
            cp.start()

    def finish(ins, outs, sems):
        mine, sends, arrivals = copies(ins, outs, sems)
        for cp in arrivals:
            cp.wait_recv()
        for cp in sends:
            cp.wait_send()
        for cp in mine:
            cp.wait()

    out_shapes = [jax.ShapeDtypeStruct(p.shape, p.dtype) for p in parts]
    return _Rider(parts, out_shapes, _dma_sems(n, 3), start, finish)


def _broadcast_rider(values):
    n = len(values)

    def copies(ins, outs, sems):
        send_sems, recv_sems, local_sems = sems
        x, y, c = _position()
        me = _index(x, y, c)

        def peer(k):
            return (1 - x if k & 4 else x, 1 - y if k & 2 else y, 1 - c if k & 1 else c)

        mine = [pltpu.make_async_copy(ins[a], outs[a].at[me], local_sems.at[a]) for a in range(n)]
        sends, arrivals = [], []
        for a in range(n):
            for k in range(1, N_DEV):
                common = dict(send_sem=send_sems.at[a, k - 1], recv_sem=recv_sems.at[a, k - 1], device_id=peer(k), device_id_type=MESH)
                sends.append(pltpu.make_async_remote_copy(src_ref=ins[a], dst_ref=outs[a].at[me], **common))
                arrivals.append(pltpu.make_async_remote_copy(src_ref=ins[a], dst_ref=outs[a].at[_index(*peer(k))], **common))
        return mine, sends, arrivals

    def start(ins, outs, sems):
        mine, sends, _ = copies(ins, outs, sems)
        for cp in mine + sends:
            cp.start()

    def finish(ins, outs, sems):
        mine, sends, arrivals = copies(ins, outs, sems)
        for cp in arrivals:
            cp.wait_recv()
        for cp in sends:
            cp.wait_send()
        for cp in mine:
            cp.wait()

    out_shapes = [jax.ShapeDtypeStruct((N_DEV,) + v.shape, v.dtype) for v in values]
    return _Rider(values, out_shapes, _dma_sems(n, 7), start, finish)


def _join_riders(*riders):
    def each(fn_name, ins, outs, sems):
        i = o = s = 0
        for r in riders:
            n_i, n_o, n_s = len(r.ins), len(r.out_shapes), len(r.sem_shapes)
            if getattr(r, fn_name) is not None:
                getattr(r, fn_name)(ins[i:i + n_i], outs[o:o + n_o], sems[s:s + n_s])
            i, o, s = i + n_i, o + n_o, s + n_s

    middle = (lambda ins, outs, sems: each("middle", ins, outs, sems)) if any(r.middle for r in riders) else None
    return _Rider([a for r in riders for a in r.ins], [a for r in riders for a in r.out_shapes],
                  [a for r in riders for a in r.sem_shapes],
                  lambda ins, outs, sems: each("start", ins, outs, sems),
                  lambda ins, outs, sems: each("finish", ins, outs, sems), middle)


def _run_rider(name, rider):
    n_in, n_out = len(rider.ins), len(rider.out_shapes)

    def body(*refs):
        ins, outs, sems = refs[:n_in], refs[n_in:n_in + n_out], refs[n_in + n_out:]
        rider.start(ins, outs, sems)
        if rider.middle is not None:
            rider.middle(ins, outs, sems)
        rider.finish(ins, outs, sems)

    return pl.pallas_call(
        body, name=name, in_specs=[ANY] * n_in, out_specs=[ANY] * n_out, out_shape=rider.out_shapes,
        scratch_shapes=rider.sem_shapes)(*rider.ins)


class _Host:
    def __init__(self, rider):
        self.rider = rider
        self.n_in = len(rider.ins) if rider else 0
        self.n_out = len(rider.out_shapes) if rider else 0
        self.n_sem = len(rider.sem_shapes) if rider else 0
        self.ins = rider.ins if rider else []
        self.in_specs = [ANY] * self.n_in
        self.out_specs = [ANY] * self.n_out
        self.out_shapes = rider.out_shapes if rider else []
        self.scratch = rider.sem_shapes if rider else []

    def run(self, first, last, ins, outs, sems, compute, midway=None):
        if self.rider is None:
            compute()
            return

        @pl.when(first)
        def _():
            self.rider.start(ins, outs, sems)

        compute()
        if self.rider.middle is not None and midway is not None:
            pl.when(midway)(lambda: self.rider.middle(ins, outs, sems))

        @pl.when(last)
        def _():
            if self.rider.middle is not None and midway is None:
                self.rider.middle(ins, outs, sems)
            self.rider.finish(ins, outs, sems)


def _matmul(name, kind, a, b, *, tm, tn, tk, outs, epilogue=None, extras=(), out_blocks=False, rider=None, j_outer=False):
    if kind == "nn":
        (m, kdim), n = a.shape, b.shape[1]
    elif kind == "nt":
        (m, kdim), n = a.shape, b.shape[0]
    else:
        (kdim, m), n = a.shape, b.shape[1]
    if out_blocks:
        tn = min(tn, n // N_DEV)
    tm, tn, tk = _tile(m, tm), _tile(n, tn), _tile(kdim, tk)
    ni, nj, nk = m // tm, n // tn, kdim // tk

    def spec(shape, fn):
        return pl.BlockSpec(shape, (lambda g0, g1, k: fn(g1, g0, k)) if j_outer else fn)

    a_spec = spec((tk, tm), lambda i, j, k: (k, i)) if kind == "tn" else spec((tm, tk), lambda i, j, k: (i, k))
    b_spec = spec((tn, tk), lambda i, j, k: (j, k)) if kind == "nt" else spec((tk, tn), lambda i, j, k: (k, j))
    dn = {"nn": NN, "nt": NT, "tn": TN}[kind]

    tile_spec = spec((tm, tn), lambda i, j, k: (i, j))
    row_spec = spec((1, tn), lambda i, j, k: (0, j))
    if out_blocks:
        width = n // N_DEV
        r_out = width // tn
        out_shape = [jax.ShapeDtypeStruct((N_DEV, m, width), dt) for dt in outs]
        out_specs = [spec((None, tm, tn), lambda i, j, k: (j // r_out, i, j % r_out)) for _ in outs]
    else:
        out_shape = [jax.ShapeDtypeStruct((m, n), dt) for dt in outs]
        out_specs = [tile_spec for _ in outs]
    n_ex, n_out = len(extras), len(outs)
    host = _Host(rider)
    n_acc = 1 if nk > 1 else 0

    def body(*refs):
        a_ref, b_ref = refs[0], refs[1]
        pos = 2
        ex_refs = refs[pos:pos + n_ex]; pos += n_ex
        r_ins = refs[pos:pos + host.n_in]; pos += host.n_in
        out_refs = refs[pos:pos + n_out]; pos += n_out
        r_outs = refs[pos:pos + host.n_out]; pos += host.n_out
        acc_ref = refs[pos] if n_acc else None
        sems = refs[pos + n_acc:]
        i, j, k = pl.program_id(1 if j_outer else 0), pl.program_id(0 if j_outer else 1), pl.program_id(2)

        def finish_tile(acc):
            vals = (acc,) if epilogue is None else epilogue(acc, *[e[...] for e in ex_refs])
            for o_ref, v in zip(out_refs, vals):
                o_ref[...] = v.astype(o_ref.dtype)

        def compute():
            part = _dot(a_ref[...], b_ref[...], dn)
            if nk == 1:
                finish_tile(part)
                return

            @pl.when(k == 0)
            def _():
                acc_ref[...] = part

            @pl.when(jnp.logical_and(k > 0, k < nk - 1))
            def _():
                acc_ref[...] += part

            @pl.when(k == nk - 1)
            def _():
                finish_tile(acc_ref[...] + part)

        first = jnp.logical_and(jnp.logical_and(i == 0, j == 0), k == 0)
        last = jnp.logical_and(jnp.logical_and(i == ni - 1, j == nj - 1), k == nk - 1)
        step = (pl.program_id(0) * (ni if j_outer else nj) + pl.program_id(1)) * nk + k
        host.run(first, last, r_ins, r_outs, sems, compute, midway=step == (ni * nj * nk * 3) // 5)

    sem = ("arbitrary",) * 3 if rider else ("parallel", "parallel", "arbitrary")
    res = pl.pallas_call(
        body,
        name=name,
        grid=(nj, ni, nk) if j_outer else (ni, nj, nk),
        in_specs=[a_spec, b_spec] + [row_spec if e.shape[0] == 1 else tile_spec for e in extras] + host.in_specs,
        out_specs=out_specs + host.out_specs,
        out_shape=out_shape + host.out_shapes,
        scratch_shapes=([pltpu.VMEM((tm, tn), F32)] if n_acc else []) + host.scratch,
        compiler_params=_params(sem),
    )(a, b, *extras, *host.ins)
    return res[0] if len(res) == 1 else res


def _rms_fwd(name, x, g, tm=512, rider=None):
    t, d = x.shape
    tm = _tile(t, tm)
    n = t // tm
    host = _Host(rider)

    def body(*refs):
        x_ref, g_ref = refs[:2]
        r_ins = refs[2:2 + host.n_in]
        h_ref = refs[2 + host.n_in]
        r_outs = refs[3 + host.n_in:3 + host.n_in + host.n_out]
        sems = refs[3 + host.n_in + host.n_out:]
        i = pl.program_id(0)

        def compute():
            xf = x_ref[...]
            r = lax.rsqrt(jnp.mean(xf * xf, axis=-1, keepdims=True) + EPS)
            h_ref[...] = (xf * r * g_ref[...]).astype(h_ref.dtype)

        host.run(i == 0, i == n - 1, r_ins, r_outs, sems, compute, midway=i == (n * 3) // 5)

    res = pl.pallas_call(
        body,
        name=name,
        grid=(n,),
        in_specs=[pl.BlockSpec((tm, d), lambda i: (i, 0)), pl.BlockSpec((1, d), lambda i: (0, 0))] + host.in_specs,
        out_specs=[pl.BlockSpec((tm, d), lambda i: (i, 0))] + host.out_specs,
        out_shape=[jax.ShapeDtypeStruct((t, d), BF16)] + host.out_shapes,
        scratch_shapes=host.scratch,
        compiler_params=_params(("arbitrary",) if rider else ("parallel",)),
    )(x, g.reshape(1, d), *host.ins)
    return res[0] if len(res) == 1 else res


def _rms_bwd(name, dh, x, g, res=None, tm=256):
    t, d = x.shape
    tm = _tile(t, tm)
    has_res = res is not None

    def body(*refs):
        if has_res:
            dh_ref, x_ref, g_ref, res_ref, dx_ref, dxb_ref, gg_ref, ss_ref = refs
        else:
            dh_ref, x_ref, g_ref, dx_ref, dxb_ref, gg_ref, ss_ref = refs
        i = pl.program_id(0)
        xf = x_ref[...]
        r = lax.rsqrt(jnp.mean(xf * xf, axis=-1, keepdims=True) + EPS)
        xh = xf * r
        dhf = dh_ref[...].astype(F32)
        dxh = dhf * g_ref[...]
        dx = r * (dxh - xh * jnp.mean(dxh * xh, axis=-1, keepdims=True))

        @pl.when(i == 0)
        def _():
            gg_ref[...] = jnp.zeros_like(gg_ref)
            ss_ref[...] = jnp.zeros_like(ss_ref)

        if has_res:
            resf = res_ref[...]
            dx = dx + resf
            ss_ref[...] += jnp.sum(jnp.sum(resf * resf, axis=0, keepdims=True), axis=1, keepdims=True)
        dx_ref[...] = dx
        dxb_ref[...] = dx.astype(BF16)
        gg_ref[...] += jnp.sum(dhf * xh, axis=0, keepdims=True)

    row = pl.BlockSpec((tm, d), lambda i: (i, 0))
    vec = pl.BlockSpec((1, d), lambda i: (0, 0))
    one = pl.BlockSpec((1, 1), lambda i: (0, 0))
    ins = [dh, x, g.reshape(1, d)] + ([res] if has_res else [])
    dx, dxb, gg, ss = pl.pallas_call(
        body,
        name=name,
        grid=(t // tm,),
        in_specs=[row, row, vec] + ([row] if has_res else []),
        out_specs=[row, row, vec, one],
        out_shape=[jax.ShapeDtypeStruct((t, d), F32), jax.ShapeDtypeStruct((t, d), BF16), jax.ShapeDtypeStruct((1, d), F32),
                   jax.ShapeDtypeStruct((1, 1), F32)],
        compiler_params=_params(("arbitrary",)),
    )(*ins)
    return dx, dxb, gg.reshape(d), ss[0, 0]


def _head_rms(xf):
    r = lax.rsqrt(jnp.mean(xf * xf, axis=-1, keepdims=True) + EPS)
    return xf * r, r


def _head_rms_bwd(dy, xn, r, g):
    dxh = dy * g
    dx = r * (dxh - xn * jnp.mean(dxh * xn, axis=-1, keepdims=True))
    return dx, jnp.sum(dy * xn, axis=0, keepdims=True)


def _col_to_row(col):
    n = col.shape[0]
    eye = lax.broadcasted_iota(jnp.int32, (n, n), 0) == lax.broadcasted_iota(jnp.int32, (n, n), 1)
    return jnp.sum(jnp.where(eye, col, 0.0), axis=0, keepdims=True)


def _row_to_col(row):
    n = row.shape[1]
    eye = lax.broadcasted_iota(jnp.int32, (n, n), 0) == lax.broadcasted_iota(jnp.int32, (n, n), 1)
    return jnp.sum(jnp.where(eye, row, 0.0), axis=1, keepdims=True)


def _dproj_args(dproj, n_in):
    if dproj is None:
        return [], [], {}
    return [dproj], [ANY], {n_in: 0}


def _shift_down(u, s, rows):
    return jnp.where(rows >= s, pltpu.roll(u, s, axis=0), 0.0)


def _shift_up(u, s, rows, t):
    return jnp.where(rows < t - s, pltpu.roll(u, t - s, axis=0), 0.0)


def _conv_fwd(proj, off, conv_w, cb):
    t = proj.shape[0]
    c = conv_w.shape[1]
    blk0 = off // (3 * cb)

    def body(p_ref, w_ref, y_ref):
        rows = lax.broadcasted_iota(jnp.int32, (t, cb), 0)
        bg = p_ref[:, 0:cb].astype(F32)
        u = p_ref[:, cb:2 * cb].astype(F32) * p_ref[:, 2 * cb:3 * cb].astype(F32)
        w = w_ref[...]
        conv = w[2:3] * u + w[1:2] * _shift_down(u, 1, rows) + w[0:1] * _shift_down(u, 2, rows)
        y_ref[...] = (bg * conv).astype(y_ref.dtype)

    return pl.pallas_call(
        body,
        name="conv_fwd",
        grid=(c // cb,),
        in_specs=[pl.BlockSpec((t, 3 * cb), lambda j: (0, blk0 + j)), pl.BlockSpec((CONV_TAPS, cb), lambda j: (0, j))],
        out_specs=pl.BlockSpec((t, cb), lambda j: (0, j)),
        out_shape=jax.ShapeDtypeStruct((t, c), BF16),
        compiler_params=_params(("parallel",)),
    )(proj, conv_w)


def _conv_bwd(proj, off, conv_w, dy, cb, dproj, rider=None):
    t = proj.shape[0]
    c = conv_w.shape[1]
    blk0 = off // (3 * cb)
    nj = c // cb
    host = _Host(rider)

    def body(*refs):
        p_ref, w_ref, dy_ref = refs[:3]
        r_ins = refs[4:4 + host.n_in]
        dp_ref, gw_ref = refs[4 + host.n_in:6 + host.n_in]
        r_outs = refs[6 + host.n_in:6 + host.n_in + host.n_out]
        sems = refs[6 + host.n_in + host.n_out:]
        j = pl.program_id(0)

        def compute():
            rows = lax.broadcasted_iota(jnp.int32, (t, cb), 0)
            bg = p_ref[:, 0:cb].astype(F32)
            cg = p_ref[:, cb:2 * cb].astype(F32)
            v = p_ref[:, 2 * cb:3 * cb].astype(F32)
            u = cg * v
            w = w_ref[...]
            u1 = _shift_down(u, 1, rows)
            u2 = _shift_down(u, 2, rows)
            conv = w[2:3] * u + w[1:2] * u1 + w[0:1] * u2
            dyf = dy_ref[...].astype(F32)
            dconv = dyf * bg
            du = w[2:3] * dconv + w[1:2] * _shift_up(dconv, 1, rows, t) + w[0:1] * _shift_up(dconv, 2, rows, t)
            dp_ref[:, 0:cb] = (dyf * conv).astype(dp_ref.dtype)
            dp_ref[:, cb:2 * cb] = (du * v).astype(dp_ref.dtype)
            dp_ref[:, 2 * cb:3 * cb] = (du * cg).astype(dp_ref.dtype)
            gw_ref[0:1, :] = jnp.sum(dconv * u2, axis=0, keepdims=True)
            gw_ref[1:2, :] = jnp.sum(dconv * u1, axis=0, keepdims=True)
            gw_ref[2:3, :] = jnp.sum(dconv * u, axis=0, keepdims=True)

        host.run(j == 0, j == nj - 1, r_ins, r_outs, sems, compute)

    res = pl.pallas_call(
        body,
        name="conv_bwd",
        grid=(nj,),
        in_specs=[
            pl.BlockSpec((t, 3 * cb), lambda j: (0, blk0 + j)),
            pl.BlockSpec((CONV_TAPS, cb), lambda j: (0, j)),
            pl.BlockSpec((t, cb), lambda j: (0, j)),
            ANY,
        ] + host.in_specs,
        out_specs=[pl.BlockSpec((t, 3 * cb), lambda j: (0, blk0 + j)), pl.BlockSpec((CONV_TAPS, cb), lambda j: (0, j))] + host.out_specs,
        out_shape=[jax.ShapeDtypeStruct(dproj.shape, dproj.dtype), jax.ShapeDtypeStruct((CONV_TAPS, c), F32)] + host.out_shapes,
        input_output_aliases={3: 0},
        scratch_shapes=host.scratch,
        compiler_params=_params(("arbitrary",)),
    )(proj, conv_w, dy, dproj, *host.ins)
    return res


def _lane_scan(x, reverse):
    lane = lax.broadcasted_iota(jnp.int32, x.shape, 1)
    s = 1
    while s < LANES:
        if reverse:
            x = x + jnp.where(lane < LANES - s, pltpu.roll(x, LANES - s, axis=1), 0.0)
        else:
            x = x + jnp.where(lane >= s, pltpu.roll(x, s, axis=1), 0.0)
        s *= 2
    return x


def _scan_rows(src_ref, dst_ref, t, reverse, fn=None):
    groups = list(range(t // LANES))
    if reverse:
        groups = groups[::-1]
    carry = None
    for gi in groups:
        sl = slice(gi * LANES, (gi + 1) * LANES)
        blk = src_ref[:, sl]
        if fn is not None:
            blk = fn(blk)
        blk = _lane_scan(blk, reverse)
        if carry is not None:
            blk = blk + carry
        dst_ref[:, sl] = blk
        carry = blk[:, 0:1] if reverse else blk[:, LANES - 1:LANES]


def _forget_fwd(z_row, b_col):
    rows, t = z_row.shape

    def body(z_ref, b_ref, c_ref):
        def logf(z):
            zz = z + b_ref[...]
            return jnp.minimum(zz, 0.0) - jnp.log(1.0 + jnp.exp(-jnp.abs(zz)))

        _scan_rows(z_ref, c_ref, t, False, logf)

    return pl.pallas_call(
        body,
        name="forget_fwd",
        out_shape=jax.ShapeDtypeStruct((rows, t), F32),
        compiler_params=pltpu.CompilerParams(vmem_limit_bytes=VMEM_LIMIT),
    )(z_row, b_col)


def _rows_to_colb(c_row3, tq):
    heads, _, t = c_row3.shape

    def body(r_ref, o_ref):
        o_ref[...] = jnp.broadcast_to(_row_to_col(r_ref[...]), (tq, LANES))

    return pl.pallas_call(
        body,
        name="rows_to_colb",
        grid=(heads, t // tq),
        in_specs=[pl.BlockSpec((None, 1, tq), lambda h, i: (h, 0, i))],
        out_specs=pl.BlockSpec((None, tq, LANES), lambda h, i: (h, i, 0)),
        out_shape=jax.ShapeDtypeStruct((heads, t, LANES), F32),
        compiler_params=_params(("parallel", "parallel")),
    )(c_row3)


def _forget_bwd(z_row, b_col, dcq_row, dck_row):
    rows, t = z_row.shape

    def body(z_ref, b_ref, dcq_ref, dck_ref, dz_ref, db_ref, tmp_ref):
        tmp_ref[...] = dcq_ref[...] - dck_ref[...]
        _scan_rows(tmp_ref, tmp_ref, t, True)
        zz = z_ref[...] + b_ref[...]
        dz = tmp_ref[...] * (1.0 / (1.0 + jnp.exp(zz)))
        dz_ref[...] = dz.astype(dz_ref.dtype)
        db_ref[...] = jnp.sum(dz, axis=1, keepdims=True)

    return pl.pallas_call(
        body,
        name="forget_bwd",
        out_shape=[jax.ShapeDtypeStruct((rows, t), BF16), jax.ShapeDtypeStruct((rows, 1), F32)],
        scratch_shapes=[pltpu.VMEM((rows, t), F32)],
        compiler_params=pltpu.CompilerParams(vmem_limit_bytes=VMEM_LIMIT),
    )(z_row, b_col, dcq_row, dck_row)


def _fox_fwd(proj, off, gq, gk, c_row3, c_colb, heads, tq, rider=None):
    t = proj.shape[0]
    hd = FOX_HEAD_DIM
    tq = _tile(t, tq)
    nq = t // tq
    blk0 = off // hd
    scale = 1.0 / math.sqrt(hd)
    host = _Host(rider)

    def body(*refs):
        q_ref, k_ref, v_ref, gq_ref, gk_ref, crow_ref, ccol_ref = refs[:7]
        r_ins = refs[7:7 + host.n_in]
        o_ref, lse_ref = refs[7 + host.n_in:9 + host.n_in]
        r_outs = refs[9 + host.n_in:9 + host.n_in + host.n_out]
        khat_ref, v_t_ref = refs[9 + host.n_in + host.n_out:11 + host.n_in + host.n_out]
        sems = refs[11 + host.n_in + host.n_out:]
        h, qi = pl.program_id(0), pl.program_id(1)

        def compute():
            eye = (lax.broadcasted_iota(jnp.int32, (hd, hd), 0) == lax.broadcasted_iota(jnp.int32, (hd, hd), 1)).astype(BF16)

            @pl.when(qi == 0)
            def _():
                kn, _ = _head_rms(k_ref[...].astype(F32))
                khat_ref[...] = (kn * gk_ref[...]).astype(BF16)
                v_t_ref[...] = _dot(eye, v_ref[...], NT).astype(BF16)

            qn, _ = _head_rms(q_ref[...].astype(F32))
            qhat = (qn * (gq_ref[...] * scale)).astype(BF16)
            crow = crow_ref[:, pl.ds(pl.multiple_of(qi * tq, tq), tq)]
            above = lax.broadcasted_iota(jnp.int32, (tq, tq), 1) >= lax.broadcasted_iota(jnp.int32, (tq, tq), 0)

            def tile(j, keys, carry, diagonal):
                m, l, acc_t = carry
                ks = pl.multiple_of(j * keys, keys)
                s_t = _dot(khat_ref[pl.ds(ks, keys), :], qhat, NT) - ccol_ref[pl.ds(ks, keys), 0:1]
                if diagonal:
                    s_t = jnp.where(above, s_t, NEG)
                m_new = jnp.maximum(m, jnp.max(s_t, axis=0, keepdims=True) + crow)
                alpha = jnp.exp(m - m_new)
                p_t = jnp.exp(s_t + (crow - m_new))
                l = alpha * l + jnp.sum(p_t, axis=0, keepdims=True)
                acc_t = alpha * acc_t + _dot(v_t_ref[:, pl.ds(ks, keys)], p_t.astype(BF16), NN)
                return m_new, l, acc_t

            init = (jnp.full((1, tq), NEG, F32), jnp.zeros((1, tq), F32), jnp.zeros((hd, tq), F32))
            pairs = qi // 2 if 2 * tq <= MAX_KEYS else 0
            carry = lax.fori_loop(0, pairs, lambda j, c: tile(j, 2 * tq, c, False), init)
            carry = lax.fori_loop(2 * pairs, qi, lambda j, c: tile(j, tq, c, False), carry)
            m, l, acc_t = tile(qi, tq, carry, True)
            o_ref[...] = _dot((acc_t / l).astype(BF16), eye, TN).astype(o_ref.dtype)
            lse_ref[...] = m + jnp.log(l)

        first = jnp.logical_and(h == 0, qi == 0)
        last = jnp.logical_and(h == heads - 1, qi == nq - 1)
        host.run(first, last, r_ins, r_outs, sems, compute, midway=h * nq + qi == (heads * nq * 3) // 5)

    res = pl.pallas_call(
        body,
        name="fox_fwd",
        grid=(heads, nq),
        in_specs=[
            pl.BlockSpec((tq, hd), lambda h, i: (i, blk0 + 3 * h)),
            pl.BlockSpec((t, hd), lambda h, i: (0, blk0 + 3 * h + 1)),
            pl.BlockSpec((t, hd), lambda h, i: (0, blk0 + 3 * h + 2)),
            pl.BlockSpec((1, hd), lambda h, i: (0, 0)),
            pl.BlockSpec((1, hd), lambda h, i: (0, 0)),
            pl.BlockSpec((None, 1, t), lambda h, i: (h, 0, 0)),
            pl.BlockSpec((None, t, LANES), lambda h, i: (h, 0, 0)),
        ] + host.in_specs,
        out_specs=[pl.BlockSpec((tq, hd), lambda h, i: (i, h)), pl.BlockSpec((None, 1, tq), lambda h, i: (h, 0, i))] + host.out_specs,
        out_shape=[jax.ShapeDtypeStruct((t, heads * hd), BF16), jax.ShapeDtypeStruct((heads, 1, t), F32)] + host.out_shapes,
        scratch_shapes=[pltpu.VMEM((t, hd), BF16), pltpu.VMEM((hd, t), BF16)] + host.scratch,
        compiler_params=_params(("arbitrary", "arbitrary")),
    )(proj, proj, proj, gq.reshape(1, hd), gk.reshape(1, hd), c_row3, c_colb, *host.ins)
    return res


def _fox_bwd(proj, off, o, do, gq, gk, c_row3, c_colb, lse, heads, tq, dproj, rider=None):
    t = proj.shape[0]
    hd = FOX_HEAD_DIM
    tq = _tile(t, tq)
    nb = t // tq
    blk0 = off // hd
    scale = 1.0 / math.sqrt(hd)
    host = _Host(rider)
    n_fixed_in = 11

    def body(*refs):
        q_ref, k_ref, v_ref, o_ref, do_ref, gq_ref, gk_ref, crow_ref, ccol_ref, lse_ref = refs[:10]
        pos = n_fixed_in
        r_ins = refs[pos:pos + host.n_in]; pos += host.n_in
        dp_ref, dc_ref, dck_out_ref, ggq_ref, ggk_ref = refs[pos:pos + 5]; pos += 5
        r_outs = refs[pos:pos + host.n_out]; pos += host.n_out
        qhat_ref, khat_ref, khat_t_ref, dq_t_ref, dk_ref, dcq_ref, delta_ref = refs[pos:pos + 7]; pos += 7
        dck_ref = dck_out_ref
        sems = refs[pos:]
        h = pl.program_id(0)

        def compute():
            qn, rq = _head_rms(q_ref[...].astype(F32))
            qhat_ref[...] = (qn * (gq_ref[...] * scale)).astype(BF16)
            kn, rk = _head_rms(k_ref[...].astype(F32))
            khat_ref[...] = (kn * gk_ref[...]).astype(BF16)
            eye = (lax.broadcasted_iota(jnp.int32, (hd, hd), 0) == lax.broadcasted_iota(jnp.int32, (hd, hd), 1)).astype(BF16)
            khat_t_ref[...] = _dot(eye, khat_ref[...], NT).astype(BF16)
            delta = jnp.sum(do_ref[...].astype(F32) * o_ref[...].astype(F32), axis=-1, keepdims=True)
            for b in range(nb):
                sl = slice(b * tq, (b + 1) * tq)
                delta_ref[:, sl] = _col_to_row(delta[sl, :])
            dq_t_ref[...] = jnp.zeros_like(dq_t_ref)
            dcq_ref[...] = jnp.zeros_like(dcq_ref)
            above = lax.broadcasted_iota(jnp.int32, (tq, tq), 1) >= lax.broadcasted_iota(jnp.int32, (tq, tq), 0)

            def kv_block(j, _):
                ks = pl.multiple_of(j * tq, tq)
                kh = khat_ref[pl.ds(ks, tq), :]
                kh_t = khat_t_ref[:, pl.ds(ks, tq)]
                vv = v_ref[pl.ds(ks, tq), :]
                ccol = ccol_ref[pl.ds(ks, tq), 0:1]

                def q_block(i, n, carry, diagonal):
                    dk, dv, dck = carry
                    qs = pl.multiple_of(i * tq, tq)
                    qh = qhat_ref[pl.ds(qs, n), :]
                    dob = do_ref[pl.ds(qs, n), :]
                    s_t = _dot(kh, qh, NT) + ((crow_ref[:, pl.ds(qs, n)] - lse_ref[:, pl.ds(qs, n)]) - ccol)
                    p_t = jnp.exp(s_t)
                    if diagonal:
                        p_t = jnp.where(above, p_t, 0.0)
                    ds_t = p_t * (_dot(vv, dob, NT) - delta_ref[:, pl.ds(qs, n)])
                    dsb = ds_t.astype(BF16)
                    dv = dv + _dot(p_t.astype(BF16), dob, NN)
                    dk = dk + _dot(dsb, qh, NN)
                    dq_t_ref[:, pl.ds(qs, n)] += _dot(kh_t, dsb, NN)
                    dcq_ref[:, pl.ds(qs, n)] += jnp.sum(ds_t, axis=0, keepdims=True)
                    dck = dck + jnp.sum(ds_t, axis=-1, keepdims=True)
                    return dk, dv, dck

                zero = jnp.zeros((tq, hd), F32)
                carry = q_block(j, tq, (zero, zero, jnp.zeros((tq, 1), F32)), True)
                pairs = (nb - 1 - j) // 2 if 2 * tq <= MAX_KEYS else 0
                carry = lax.fori_loop(0, pairs, lambda p, c: q_block(j + 1 + 2 * p, 2 * tq, c, False), carry)
                dk, dv, dck = lax.fori_loop(j + 1 + 2 * pairs, nb, lambda i, c: q_block(i, tq, c, False), carry)
                dk_ref[pl.ds(ks, tq), :] = dk
                dp_ref[pl.ds(ks, tq), 2 * hd:3 * hd] = dv.astype(dp_ref.dtype)
                dck_ref[pl.ds(ks, tq), :] = dck
                return 0

            lax.fori_loop(0, nb, kv_block, 0)

            dq, ggq = _head_rms_bwd(dq_t_ref[...].T * scale, qn, rq, gq_ref[...])
            dk, ggk = _head_rms_bwd(dk_ref[...], kn, rk, gk_ref[...])
            dp_ref[:, 0:hd] = dq.astype(dp_ref.dtype)
            dp_ref[:, hd:2 * hd] = dk.astype(dp_ref.dtype)
            dc_ref[...] = dcq_ref[...]

            @pl.when(h == 0)
            def _():
                ggq_ref[...] = jnp.zeros_like(ggq_ref)
                ggk_ref[...] = jnp.zeros_like(ggk_ref)

            ggq_ref[...] += ggq
            ggk_ref[...] += ggk

        host.run(h == 0, h == heads - 1, r_ins, r_outs, sems, compute)

    head_in = lambda part: pl.BlockSpec((t, hd), lambda h: (0, blk0 + 3 * h + part))
    vec = pl.BlockSpec((1, hd), lambda h: (0, 0))
    colb = pl.BlockSpec((None, t, LANES), lambda h: (h, 0, 0))
    res = pl.pallas_call(
        body,
        name="fox_bwd",
        grid=(heads,),
        in_specs=[
            head_in(0), head_in(1), head_in(2),
            pl.BlockSpec((t, hd), lambda h: (0, h)),
            pl.BlockSpec((t, hd), lambda h: (0, h)),
            vec, vec,
            pl.BlockSpec((None, 1, t), lambda h: (h, 0, 0)),
            colb,
            pl.BlockSpec((None, 1, t), lambda h: (h, 0, 0)),
            ANY,
        ] + host.in_specs,
        out_specs=[
            pl.BlockSpec((t, 3 * hd), lambda h: (0, blk0 // 3 + h)),
            pl.BlockSpec((None, 1, t), lambda h: (h, 0, 0)),
            pl.BlockSpec((None, t, 1), lambda h: (h, 0, 0)),
            vec, vec,
        ] + host.out_specs,
        out_shape=[
            jax.ShapeDtypeStruct(dproj.shape, dproj.dtype),
            jax.ShapeDtypeStruct((heads, 1, t), F32),
            jax.ShapeDtypeStruct((heads, t, 1), F32),
            jax.ShapeDtypeStruct((1, hd), F32),
            jax.ShapeDtypeStruct((1, hd), F32),
        ] + host.out_shapes,
        input_output_aliases={10: 0},
        scratch_shapes=[
            pltpu.VMEM((t, hd), BF16), pltpu.VMEM((t, hd), BF16), pltpu.VMEM((hd, t), BF16),
            pltpu.VMEM((hd, t), F32), pltpu.VMEM((t, hd), F32),
            pltpu.VMEM((1, t), F32), pltpu.VMEM((1, t), F32),
        ] + host.scratch,
        compiler_params=_params(("arbitrary",)),
    )(proj, proj, proj, o, do, gq.reshape(1, hd), gk.reshape(1, hd), c_row3, c_colb, lse, dproj, *host.ins)
    return res


def _mem_fwd(proj, off, kv, gq, gk, tq):
    t = proj.shape[0]
    m, width = kv.shape[0], kv.shape[1] // 2
    hd = width // MEM_HEADS
    tq = _tile(t, tq)
    blk0 = off // hd
    scale = 1.0 / math.sqrt(hd)

    def body(q_ref, k_ref, v_ref, gq_ref, gk_ref, o_ref):
        qn, _ = _head_rms(q_ref[...].astype(F32))
        kn, _ = _head_rms(k_ref[...])
        s = _dot((qn * gq_ref[...]).astype(BF16), (kn * gk_ref[...]).astype(BF16), NT) * scale
        p = jnp.exp(s - jnp.max(s, axis=-1, keepdims=True))
        p = p / jnp.sum(p, axis=-1, keepdims=True)
        o_ref[...] = _dot(p.astype(BF16), v_ref[...].astype(BF16), NN).astype(o_ref.dtype)

    vec = pl.BlockSpec((1, hd), lambda h, i: (0, 0))
    return pl.pallas_call(
        body,
        name="mem_fwd",
        grid=(MEM_HEADS, t // tq),
        in_specs=[
            pl.BlockSpec((tq, hd), lambda h, i: (i, blk0 + h)),
            pl.BlockSpec((m, hd), lambda h, i: (0, h)),
            pl.BlockSpec((m, hd), lambda h, i: (0, MEM_HEADS + h)),
            vec, vec,
        ],
        out_specs=pl.BlockSpec((tq, hd), lambda h, i: (i, h)),
        out_shape=jax.ShapeDtypeStruct((t, width), BF16),
        compiler_params=_params(("parallel", "parallel")),
    )(proj, kv, kv, gq.reshape(1, hd), gk.reshape(1, hd))


def _mem_bwd(proj, off, kv, do, gq, gk, tq, dproj, rider=None):
    t = proj.shape[0]
    m, width = kv.shape[0], kv.shape[1] // 2
    hd = width // MEM_HEADS
    tq = _tile(t, tq)
    nq = t // tq
    blk0 = off // hd
    scale = 1.0 / math.sqrt(hd)
    host = _Host(rider)

    def body(*refs):
        q_ref, k_ref, v_ref, do_ref, gq_ref, gk_ref = refs[:6]
        pos = 7
        r_ins = refs[pos:pos + host.n_in]; pos += host.n_in
        dq_ref, dk_ref, dv_ref, ggq_ref, ggk_ref = refs[pos:pos + 5]; pos += 5
        r_outs = refs[pos:pos + host.n_out]; pos += host.n_out
        dkh_ref, dvh_ref = refs[pos:pos + 2]; pos += 2
        sems = refs[pos:]
        h, i = pl.program_id(0), pl.program_id(1)

        def compute():
            qn, rq = _head_rms(q_ref[...].astype(F32))
            kn, rk = _head_rms(k_ref[...])
            qhat = (qn * gq_ref[...]).astype(BF16)
            khat = (kn * gk_ref[...]).astype(BF16)
            vb = v_ref[...].astype(BF16)
            dob = do_ref[...]
            s = _dot(qhat, khat, NT) * scale
            p = jnp.exp(s - jnp.max(s, axis=-1, keepdims=True))
            p = p / jnp.sum(p, axis=-1, keepdims=True)
            dp = _dot(dob, vb, NT)
            ds = p * (dp - jnp.sum(dp * p, axis=-1, keepdims=True))
            dsb = ds.astype(BF16)
            dq, ggq = _head_rms_bwd(_dot(dsb, khat, NN) * scale, qn, rq, gq_ref[...])
            dq_ref[...] = dq.astype(dq_ref.dtype)

            @pl.when(i == 0)
            def _():
                dkh_ref[...] = jnp.zeros_like(dkh_ref)
                dvh_ref[...] = jnp.zeros_like(dvh_ref)

            @pl.when(jnp.logical_and(h == 0, i == 0))
            def _():
                ggq_ref[...] = jnp.zeros_like(ggq_ref)
                ggk_ref[...] = jnp.zeros_like(ggk_ref)

            dkh_ref[...] += _dot(dsb, qhat, TN)
            dvh_ref[...] += _dot(p.astype(BF16), dob, TN)
            ggq_ref[...] += ggq

            @pl.when(i == nq - 1)
            def _():
                dk, ggk = _head_rms_bwd(dkh_ref[...] * scale, kn, rk, gk_ref[...])
                dk_ref[...] = dk.astype(dk_ref.dtype)
                dv_ref[...] = dvh_ref[...].astype(dv_ref.dtype)
                ggk_ref[...] += ggk

        first = jnp.logical_and(h == 0, i == 0)
        last = jnp.logical_and(h == MEM_HEADS - 1, i == nq - 1)
        host.run(first, last, r_ins, r_outs, sems, compute)

    vec = pl.BlockSpec((1, hd), lambda h, i: (0, 0))
    kblk = pl.BlockSpec((m, hd), lambda h, i: (0, h))
    res = pl.pallas_call(
        body,
        name="mem_bwd",
        grid=(MEM_HEADS, nq),
        in_specs=[
            pl.BlockSpec((tq, hd), lambda h, i: (i, blk0 + h)), kblk,
            pl.BlockSpec((m, hd), lambda h, i: (0, MEM_HEADS + h)),
            pl.BlockSpec((tq, hd), lambda h, i: (i, h)), vec, vec, ANY,
        ] + host.in_specs,
        out_specs=[pl.BlockSpec((tq, hd), lambda h, i: (i, blk0 + h)), kblk, kblk, vec, vec] + host.out_specs,
        out_shape=[
            jax.ShapeDtypeStruct(dproj.shape, dproj.dtype),
            jax.ShapeDtypeStruct((m, width), BF16),
            jax.ShapeDtypeStruct((m, width), BF16),
            jax.ShapeDtypeStruct((1, hd), F32),
            jax.ShapeDtypeStruct((1, hd), F32),
        ] + host.out_shapes,
        input_output_aliases={6: 0},
        scratch_shapes=[pltpu.VMEM((m, hd), F32), pltpu.VMEM((m, hd), F32)] + host.scratch,
        compiler_params=_params(("arbitrary", "arbitrary")),
    )(proj, kv, kv, do, gq.reshape(1, hd), gk.reshape(1, hd), dproj, *host.ins)
    dproj, dk, dv, ggq, ggk = res[:5]
    return (dproj, jnp.concatenate([dk, dv], axis=1), ggq.reshape(hd), ggk.reshape(hd), *res[5:])


def _sigmoid(z):
    return 1.0 / (1.0 + jnp.exp(-z))


def _merge_fwd(proj, ys, ws, tm, tc):
    t, cw = ys[0].shape
    d = ws[0].shape[1]
    tm = _tile(t, tm)

    def body(g_ref, ya_ref, yb_ref, yc_ref, wa_ref, wb_ref, wc_ref, oa_ref, ob_ref, oc_ref, out_ref):
        acc = jnp.zeros((tm, tc), F32)
        for s, (y_ref, w_ref, o_ref) in enumerate(((ya_ref, wa_ref, oa_ref), (yb_ref, wb_ref, ob_ref), (yc_ref, wc_ref, oc_ref))):
            o = _dot(y_ref[...], w_ref[...], NN)
            o_ref[...] = o.astype(o_ref.dtype)
            acc = acc + _sigmoid(g_ref[:, s * tc:(s + 1) * tc].astype(F32)) * o
        out_ref[...] = acc.astype(out_ref.dtype)

    blk = pl.BlockSpec((tm, tc), lambda i, j: (i, j))
    y_spec = pl.BlockSpec((tm, cw), lambda i, j: (i, 0))
    w_spec = pl.BlockSpec((cw, tc), lambda i, j: (0, j))
    return pl.pallas_call(
        body,
        name="merge_fwd",
        grid=(t // tm, d // tc),
        in_specs=[pl.BlockSpec((tm, 3 * tc), lambda i, j: (i, j))] + [y_spec] * 3 + [w_spec] * 3,
        out_specs=[blk] * 4,
        out_shape=[jax.ShapeDtypeStruct((t, d), BF16)] * 4,
        compiler_params=_params(("parallel", "parallel")),
    )(proj, *ys, *ws)


def _merge_bwd(proj, o3, dx1, w_out, ws, tm, tc):
    t, d = o3[0].shape
    k = dx1.shape[1]
    cw = ws[0].shape[0]
    tm = _tile(t, tm)
    ni, nj = t // tm, d // tc

    def body(dx_ref, w_ref, g_ref, oa_ref, ob_ref, oc_ref, wa_ref, wb_ref, wc_ref,
             dg_ref, da_ref, db_ref, dc_ref, ya_ref, yb_ref, yc_ref, acc_ref):
        j = pl.program_id(1)
        dmf = _dot(dx_ref[...], w_ref[...], NT)
        branches = ((oa_ref, da_ref, wa_ref, ya_ref), (ob_ref, db_ref, wb_ref, yb_ref), (oc_ref, dc_ref, wc_ref, yc_ref))
        for s, (o_ref, do_ref, ws_ref, dy_ref) in enumerate(branches):
            g = _sigmoid(g_ref[:, s * tc:(s + 1) * tc].astype(F32))
            do = (dmf * g).astype(do_ref.dtype)
            do_ref[...] = do
            dg_ref[:, s * tc:(s + 1) * tc] = (dmf * o_ref[...].astype(F32) * g * (1.0 - g)).astype(dg_ref.dtype)
            part = _dot(do, ws_ref[...], NT)

            @pl.when(j == 0)
            def _():
                acc_ref[s] = part

            @pl.when(j > 0)
            def _():
                acc_ref[s] += part

            @pl.when(j == nj - 1)
            def _():
                dy_ref[...] = acc_ref[s].astype(dy_ref.dtype)

    blk = pl.BlockSpec((tm, tc), lambda i, j: (i, j))
    wide = pl.BlockSpec((tm, 3 * tc), lambda i, j: (i, j))
    w_spec = pl.BlockSpec((cw, tc), lambda i, j: (0, j))
    y_spec = pl.BlockSpec((tm, cw), lambda i, j: (i, 0))
    return pl.pallas_call(
        body,
        name="merge_bwd",
        grid=(ni, nj),
        in_specs=[pl.BlockSpec((tm, k), lambda i, j: (i, 0)), pl.BlockSpec((tc, k), lambda i, j: (j, 0)), wide, blk, blk, blk] + [w_spec] * 3,
        out_specs=[wide, blk, blk, blk] + [y_spec] * 3,
        out_shape=[jax.ShapeDtypeStruct(proj.shape, BF16)] + [jax.ShapeDtypeStruct((t, d), BF16)] * 3 + [jax.ShapeDtypeStruct((t, cw), BF16)] * 3,
        scratch_shapes=[pltpu.VMEM((3, tm, cw), F32)],
        compiler_params=_params(("parallel", "arbitrary")),
    )(dx1, w_out, proj, *o3, *ws)


def _w_in_chunks(d, tc):
    cw = d // 2
    heads = cw // FOX_HEAD_DIM
    conv0, fox0, f0, mq0, gate0 = 0, 3 * cw, 6 * cw, 6 * cw + heads, 7 * cw + heads
    chunks = [(gate0 + s * d + j * tc, gate0 + s * d + (j + 1) * tc) for j in range(d // tc) for s in range(N_BRANCHES)]
    chunks += [(conv0 + s * cw + j * LANES, conv0 + s * cw + (j + 1) * LANES) for j in range(cw // LANES) for s in range(3)]
    chunks += [(fox0 + s * cw + j * FOX_HEAD_DIM, fox0 + s * cw + (j + 1) * FOX_HEAD_DIM) for j in range(heads) for s in range(3)]
    chunks.append((mq0, mq0 + cw))
    return chunks, (f0, f0 + heads)


ROW_TILE = 16
GROUP = 128
GROUP_BACK = 112
SCRATCH_ROWS = 2 * GROUP + 32


def _padded_rows(r):
    return -(-r // GROUP_BACK) * GROUP_BACK


def _rows_from(scr_ref, use, q8, fine, g):
    x = scr_ref[pl.ds(pl.multiple_of(q8 * 8, 8), g + 8), :]
    for s in range(8):
        @pl.when(fine == s)
        def _(s=s):
            use((x if s == 0 else pltpu.roll(x, g + 8 - s, axis=0))[0:g])


def _assemble(name, tbl, grid, step, in_specs, out_spec, out_shape, operands, g, w1, cols_of):
    has_f = len(in_specs) == 3
    k = out_shape.shape[-1]
    c = cols_of

    def body(*refs):
        t_ref, s1_ref, s2_ref = refs[:3]
        f_ref = refs[3] if has_f else None
        out_ref = refs[3 + has_f]
        scr1, scr2, scrf = refs[4 + has_f:]
        t = step()

        def put(y):
            out_ref[...] = y.astype(out_ref.dtype)

        @pl.when(t == 0)
        def _():
            scr1[...] = jnp.zeros_like(scr1)
            scr2[...] = jnp.zeros_like(scr2)
            scrf[...] = jnp.zeros_like(scrf)

        rows = lax.broadcasted_iota(jnp.int32, (g, k), 0)
        n1, a2 = t_ref[c["n1"], t], t_ref[c["a2"], t]
        scr1[0:w1, :] = (s1_ref[0] if len(s1_ref.shape) == 3 else s1_ref[...]).astype(F32)
        _rows_from(scr1, put, t_ref[c["q1"], t], t_ref[c["s1"], t], g)

        @pl.when(a2 < g)
        def _():
            scr2[g:g + s2_ref.shape[0], :] = s2_ref[...].astype(F32)
            _rows_from(scr2, lambda y: put(jnp.where(rows < n1, out_ref[...].astype(F32), y)),
                       t_ref[c["q2"], t], t_ref[c["s2"], t], g)

        if has_f:
            fa, fb = t_ref[c["fa"], t], t_ref[c["fb"], t]

            @pl.when(fb > fa)
            def _():
                scrf[g:g + f_ref.shape[0], :] = f_ref[...].astype(F32)
                inside = jnp.logical_and(rows >= fa, rows < fb)
                _rows_from(scrf, lambda y: put(jnp.where(inside, y, out_ref[...].astype(F32))),
                           t_ref[c["qf"], t], t_ref[c["sf"], t], g)

            valid = t_ref[c["valid"], t]

            @pl.when(valid < g)
            def _():
                out_ref[...] = jnp.where(rows < valid, out_ref[...].astype(F32), 0.0).astype(out_ref.dtype)

    return pl.pallas_call(
        body,
        name=name,
        grid_spec=pltpu.PrefetchScalarGridSpec(
            num_scalar_prefetch=1, grid=grid, in_specs=in_specs, out_specs=out_spec,
            scratch_shapes=[pltpu.VMEM((SCRATCH_ROWS, k), F32)] * 3),
        out_shape=out_shape,
        compiler_params=_params(("arbitrary",) * len(grid)),
    )(jnp.asarray(tbl), *operands)


def _pack_w_in(w8, d, tc):
    blocks, rp, k = w8.shape
    chunks, (f_lo, f_hi) = _w_in_chunks(d, tc)
    r = max(hi for _, hi in chunks) // blocks
    g, w1 = GROUP, GROUP + ROW_TILE
    table = []
    for lo, hi in chunks:
        for g0 in range(lo, hi, g):
            b1, r1 = divmod(g0, r)
            n1 = min(g, r - r1)
            st1 = min(r1 // ROW_TILE * ROW_TILE, rp - w1)
            o1, o2 = r1 - st1, g - n1
            b2 = b1 + 1 if n1 < g else 0
            table.append((b1, st1, o1 // 8, o1 % 8, n1, n1, b2, o2 // 8, o2 % 8))
    names = ("b1", "st1", "q1", "s1", "n1", "a2", "b2", "q2", "s2")
    cols_of = {n: i for i, n in enumerate(names)}
    tbl = np.array(table, np.int32).T
    c = cols_of
    w_all = _assemble(
        "pack_w_in", tbl, (len(table),), lambda: pl.program_id(0),
        [pl.BlockSpec((pl.Element(1), pl.Element(w1), pl.Element(k)), lambda i, t: (t[c["b1"], i], pl.multiple_of(t[c["st1"], i], ROW_TILE), 0)),
         pl.BlockSpec((None, g, k), lambda i, t: (t[c["b2"], i], 0, 0))],
        pl.BlockSpec((g, k), lambda i, t: (i, 0)),
        jax.ShapeDtypeStruct((len(table) * g, k), w8.dtype), [w8, w8], g, w1, cols_of)
    fb, fr = divmod(f_lo, r)
    return w_all, jnp.pad(w8[fb, fr:fr + f_hi - f_lo], ((0, F_ROWS - (f_hi - f_lo)), (0, 0)))


def _unpack_g_in(g_all, g_f, d, tc, blocks):
    n_all, k = g_all.shape
    chunks, (f_lo, f_hi) = _w_in_chunks(d, tc)
    r = max(hi for _, hi in chunks) // blocks
    rp = _padded_rows(r)
    g, w1 = GROUP_BACK, GROUP_BACK + ROW_TILE
    pos, spans = 0, [(f_lo, f_hi, None)]
    for lo, hi in chunks:
        spans.append((lo, hi, pos))
        pos += hi - lo
    spans.sort()
    table = []
    for b in range(blocks):
        for l0 in range(0, rp, g):
            valid = max(0, min(g, r - l0))
            g0, segs, fa, fb, of = b * r + l0, [], 0, 0, 0
            for lo, hi, p in spans:
                a, e = max(lo, g0), min(hi, g0 + valid)
                if a < e and p is None:
                    fa, fb, of = a - g0, e - g0, g + (a - lo) - (a - g0)
                elif a < e:
                    segs.append((a - g0, p + a - lo, e - a))
            assert len(segs) <= 2 and (not segs or segs[0][0] == 0 or len(segs) == 1)
            first = segs[0] if segs and segs[0][0] == 0 else (0, 0, 0)
            second = segs[-1] if segs and segs[-1][0] > 0 else (g, 0, 0)
            st1 = min(first[1] // ROW_TILE * ROW_TILE, n_all - w1)
            o1, o2 = first[1] - st1, g - second[0]
            assert second[1] % GROUP == 0
            table.append((st1, o1 // 8, o1 % 8, first[2], second[0], second[1] // GROUP, o2 // 8, o2 % 8,
                          fa, fb, of // 8, of % 8, valid))
    names = ("st1", "q1", "s1", "n1", "a2", "j2", "q2", "s2", "fa", "fb", "qf", "sf", "valid")
    cols_of = {n: i for i, n in enumerate(names)}
    tbl = np.array(table, np.int32).T
    c, per = cols_of, rp // g
    return _assemble(
        "unpack_g_in", tbl, (blocks, per), lambda: pl.program_id(0) * per + pl.program_id(1),
        [pl.BlockSpec((pl.Element(w1), pl.Element(k)), lambda b, u, t: (pl.multiple_of(t[c["st1"], b * per + u], ROW_TILE), 0)),
         pl.BlockSpec((GROUP, k), lambda b, u, t: (t[c["j2"], b * per + u], 0)),
         pl.BlockSpec((F_ROWS, k), lambda b, u, t: (0, 0))],
        pl.BlockSpec((None, g, k), lambda b, u, t: (b, u, 0)),
        jax.ShapeDtypeStruct((blocks, rp, k), g_all.dtype), [g_all, g_all, g_f], g, w1, cols_of)


def _unblock(w8):
    return w8.transpose(1, 0, 2).reshape(w8.shape[1], -1)


def _tile2(r, cols, tr, tcols):
    if r % 8 == 0:
        return _tile(r, tr), cols
    return r, _tile(cols, tcols)


def _pair_sum(name, g8, got, c):
    _, r, cols = g8.shape
    tr, tcols = _tile2(r, cols, 256, 256)

    def body(c_ref, g_ref, s_ref, o_ref):
        o_ref[...] = (g_ref[...].astype(F32) + s_ref[...].astype(F32)).astype(o_ref.dtype)

    own = pl.BlockSpec((None, tr, tcols), lambda q, i, j, c_ref: (2 * q + c_ref[0], i, j))
    blk = pl.BlockSpec((None, tr, tcols), lambda q, i, j, c_ref: (q, i, j))
    return pl.pallas_call(
        body,
        name=name,
        grid_spec=pltpu.PrefetchScalarGridSpec(num_scalar_prefetch=1, grid=(N_CHIPS, r // tr, cols // tcols),
                                               in_specs=[own, blk], out_specs=blk),
        out_shape=jax.ShapeDtypeStruct((N_CHIPS, r, cols), BF16),
        compiler_params=_params(("parallel",) * 3),
    )(c, g8, got)


def _local_step(x, mem, target, w, small, comm=None):
    t, d = x.shape
    cw = d // 2
    heads = cw // FOX_HEAD_DIM
    tc = min(512, d)
    tq = min(512, t)
    off_conv, off_fox, off_mq = 3 * d, 3 * d + 3 * cw, 3 * d + 6 * cw
    w, small = dict(w), dict(small)
    big = dict(tm=1024, tn=512, tk=2048)
    wide_k = dict(tm=512, tn=1024, tk=4096)
    tall = dict(tm=2048, tn=512, tk=2048)

    if comm:
        first = _gather_rider([comm["shards"]["w_in"], comm["conv_w"]], True)
        h, w["w_in"], cw8 = _rms_fwd("rms1_fwd", x, small["norm1_g"], rider=first)
        small["conv_w"] = _unblock(cw8)
        w_all, w_f = _pack_w_in(w["w_in"], d, tc)
        early = ("w_out", "w_mem_kv", "w_down")
        proj, *got = _matmul("proj", "nt", h, w_all, outs=[BF16], rider=_gather_rider([comm["shards"][n] for n in early], True), **tall)
        for n, val in zip(early, got):
            w[n] = _unblock(val) if n in COLUMN_SPLIT else val.reshape(-1, val.shape[-1])
    else:
        h = _rms_fwd("rms1_fwd", x, small["norm1_g"])
        w_all, w_f = _pack_w_in(w["w_in"], d, tc)
        proj = _matmul("proj", "nt", h, w_all, outs=[BF16], **tall)
    z_row = _matmul("proj_f", "nt", w_f, h, outs=[F32], tm=F_ROWS, tn=512, tk=2048)

    y_conv = _conv_fwd(proj, off_conv, small["conv_w"], LANES)

    b_col = jnp.pad(small["b_f"], (0, F_ROWS - heads)).reshape(F_ROWS, 1)
    c_row3 = _forget_fwd(z_row, b_col)[:heads].reshape(heads, 1, t)
    c_colb = _rows_to_colb(c_row3, tq)
    if comm:
        later = ("w_up", "w_conv_out", "w_fox_out", "w_mem_out")
        y_fox, lse, *got = _fox_fwd(proj, off_fox, small["fox_q_g"], small["fox_k_g"], c_row3, c_colb, heads, 2 * tq,
                                    rider=_gather_rider([comm["shards"][n] for n in later], True))
        for n, val in zip(later, got):
            w[n] = _unblock(val)
    else:
        y_fox, lse = _fox_fwd(proj, off_fox, small["fox_q_g"], small["fox_k_g"], c_row3, c_colb, heads, 2 * tq)

    nm = _rms_fwd("mem_rms_fwd", mem, small["mem_norm_g"])
    kv = _matmul("mem_kv", "nn", nm, w["w_mem_kv"], outs=[F32], tm=256, tn=512, tk=2048)
    y_mem = _mem_fwd(proj, off_mq, kv, small["mem_q_g"], small["mem_k_g"], tq)

    ys = (y_conv, y_fox, y_mem)
    w_outs = (w["w_conv_out"], w["w_fox_out"], w["w_mem_out"])
    *o3, merged = _merge_fwd(proj, ys, w_outs, 1024, tc)
    def out_epilogue(acc, xr, g2):
        x1r = acc + xr
        r = lax.rsqrt(jnp.mean(x1r * x1r, axis=-1, keepdims=True) + EPS)
        return x1r, x1r * r * g2

    x1, h2 = _matmul("out_proj", "nn", merged, w["w_out"], outs=[F32, BF16], extras=[x, small["norm2_g"].reshape(1, d)],
                     epilogue=out_epilogue, tm=512, tn=d, tk=2048)

    def up_epilogue(acc):
        return acc, jnp.square(jnp.maximum(acc, 0.0))

    up, act = _matmul("mlp_up", "nn", h2, w["w_up"], outs=[BF16, BF16], epilogue=up_epilogue, **big)

    def loss_epilogue(acc, x1r, tr):
        dy = (acc + x1r - tr) * (1.0 / d)
        return dy, dy

    dy, dyb = _matmul("mlp_down", "nn", act, w["w_down"], outs=[F32, BF16], extras=[x1, target],
                      epilogue=loss_epilogue, tm=1024, tn=512, tk=4096)

    def dup_epilogue(acc, upr):
        return (acc * 2.0 * jnp.maximum(upr.astype(F32), 0.0),)

    def by_owner(g):
        return g.reshape(N_DEV, -1, g.shape[-1])

    g, parts = {}, {}
    g["w_down"] = _matmul("d_w_down", "tn", act, dyb, outs=[BF16], **wide_k)
    if comm:
        dup = _matmul("d_act", "nt", dyb, w["w_down"], outs=[BF16], extras=[up], epilogue=dup_epilogue, **tall)
        g["w_up"], got = _matmul("d_w_up", "tn", h2, dup, outs=[BF16], out_blocks=True,
                                 rider=_pair_rider([by_owner(g["w_down"])]), **wide_k)
        pair = _pair_sum("pair_w_down", by_owner(g["w_down"]), got, comm["c"])
        dh2, parts["w_down"], got = _matmul("d_h2", "nt", dup, w["w_up"], outs=[F32],
                                            rider=_join_riders(_chip_rider([pair]), _pair_rider([g["w_up"]])), **tall)
        pair_up = _pair_sum("pair_w_up", g["w_up"], got, comm["c"])
    else:
        dup = _matmul("d_act", "nt", dyb, w["w_down"], outs=[BF16], extras=[up], epilogue=dup_epilogue, **tall)
        g["w_up"] = _matmul("d_w_up", "tn", h2, dup, outs=[BF16], out_blocks=True, **wide_k)
        dh2 = _matmul("d_h2", "nt", dup, w["w_up"], outs=[F32], **tall)
    dx1, dx1b, g_norm2, dy_sq = _rms_bwd("rms2_bwd", dh2, x1, small["norm2_g"], res=dy)
    loss = dy_sq * (0.5 * d)

    g["w_out"] = _matmul("d_w_out", "tn", merged, dx1b, outs=[BF16], **wide_k)
    dproj, *rest = _merge_bwd(proj, o3, dx1b, w["w_out"], w_outs, 512, tc)
    do3, dys = rest[:3], rest[3:]
    names = ("w_conv_out", "w_fox_out", "w_mem_out")
    for s in range(3):
        g[names[s]] = _matmul(f"d_w_branch{s}", "tn", ys[s], do3[s], outs=[BF16], out_blocks=True, **wide_k)

    dproj, dkv, g_mq, g_mk = _mem_bwd(proj, off_mq, kv, dys[2], small["mem_q_g"], small["mem_k_g"], tq, dproj)
    g["w_mem_kv"] = _matmul("d_w_mem_kv", "tn", nm, dkv, outs=[BF16], **wide_k)
    dnm = _matmul("d_mem_norm", "nt", dkv, w["w_mem_kv"], outs=[F32], tm=256, tn=512, tk=2048)
    _, _, g_mem_norm, _ = _rms_bwd("mem_rms_bwd", dnm, mem, small["mem_norm_g"])

    mid = ("w_out", "w_conv_out", "w_fox_out", "w_mem_out", "w_mem_kv")
    if comm:
        mid8 = [g[n] if n in names else by_owner(g[n]) for n in mid]
        dproj, g_conv_w, *got = _conv_bwd(proj, off_conv, small["conv_w"], dys[0], LANES, dproj, rider=_pair_rider(mid8))
        pairs_mid = [_pair_sum("pair_" + n, g8, s4, comm["c"]) for n, g8, s4 in zip(mid, mid8, got)]
        dproj, dc, dck, g_fq, g_fk, parts["w_up"] = _fox_bwd(proj, off_fox, y_fox, dys[1], small["fox_q_g"], small["fox_k_g"], c_row3,
                                                        c_colb, lse, heads, tq, dproj, rider=_chip_rider([pair_up]))
    else:
        dproj, g_conv_w = _conv_bwd(proj, off_conv, small["conv_w"], dys[0], LANES, dproj)
        dproj, dc, dck, g_fq, g_fk = _fox_bwd(proj, off_fox, y_fox, dys[1], small["fox_q_g"], small["fox_k_g"], c_row3, c_colb,
                                              lse, heads, tq, dproj)
    rows_of = lambda a: jnp.pad(a.reshape(heads, t), ((0, F_ROWS - heads), (0, 0)))
    dz_row, db = _forget_bwd(z_row, b_col, rows_of(dc), rows_of(dck))

    if comm:
        g_all, *got = _matmul("d_w_in", "tn", dproj, h, outs=[BF16], j_outer=True, rider=_chip_rider(pairs_mid), **wide_k)
        parts.update(zip(mid, got))
    else:
        g_all = _matmul("d_w_in", "tn", dproj, h, outs=[BF16], j_outer=True, **wide_k)
    g_wf = _matmul("d_w_f", "nn", dz_row, h, outs=[BF16], tm=F_ROWS, tn=512, tk=4096)
    g["w_in"] = _unpack_g_in(g_all, g_wf, d, tc, w["w_in"].shape[0])
    dh = _matmul("d_h_f", "tn", dz_row, w_f, outs=[F32], tm=1024, tn=512, tk=F_ROWS)
    add_prev = lambda acc, prev: (acc + prev,)
    if comm:
        g_in8 = g["w_in"]
        got = _run_rider("pair_exchange_w_in", _pair_rider([g_in8]))[0]
        pair = _pair_sum("pair_w_in", g_in8, got, comm["c"])
        dh, parts["w_in"] = _matmul("d_h", "nn", dproj, w_all, outs=[F32], extras=[dh], epilogue=add_prev,
                                    rider=_chip_rider([pair]), tm=1024, tn=512, tk=3328)
    else:
        dh = _matmul("d_h", "nn", dproj, w_all, outs=[F32], extras=[dh], epilogue=add_prev, tm=1024, tn=512, tk=3328)
    grad_x, _, g_norm1, _ = _rms_bwd("rms1_bwd", dh, x, small["norm1_g"], res=dx1)

    gs = dict(norm1_g=g_norm1, b_f=db[:heads, 0], conv_w=g_conv_w, fox_q_g=g_fq.reshape(-1), fox_k_g=g_fk.reshape(-1),
              mem_norm_g=g_mem_norm, mem_q_g=g_mq, mem_k_g=g_mk, norm2_g=g_norm2)
    return loss, grad_x, (parts if comm else g), gs


def _adamw_math(w, g, m, v):
    m = ADAM_B1 * m + (1.0 - ADAM_B1) * g
    v = ADAM_B2 * v + (1.0 - ADAM_B2) * jnp.square(g)
    m_hat = m / (1.0 - ADAM_B1 ** ADAM_STEP)
    v_hat = v / (1.0 - ADAM_B2 ** ADAM_STEP)
    delta = -ADAM_LR * (m_hat / (jnp.sqrt(v_hat) + ADAM_EPS) + ADAM_WD * w)
    return delta, m, v


def _adamw(name, parts, w, m, v):
    r, c = w.shape
    n_parts, rp = parts.shape[:2]
    if rp == r:
        tr, tc = _tile2(r, c, 128, 256)
    else:
        tr, tc = _tile(rp, 256), _tile(c, 1024)

    def body(p_ref, w_ref, m_ref, v_ref, g_ref, d_ref, nm_ref, nv_ref):
        g = p_ref[0].astype(F32)
        for s in range(1, n_parts):
            g = g + p_ref[s].astype(F32)
        delta, nm, nv = _adamw_math(w_ref[...], g, m_ref[...], v_ref[...])
        g_ref[...] = g
        d_ref[...] = delta
        nm_ref[...] = nm
        nv_ref[...] = nv

    blk = pl.BlockSpec((tr, tc), lambda i, j: (i, j))
    return pl.pallas_call(
        body,
        name=name,
        grid=(rp // tr, c // tc),
        in_specs=[pl.BlockSpec((n_parts, tr, tc), lambda i, j: (0, i, j)), blk, blk, blk],
        out_specs=[blk] * 4,
        out_shape=[jax.ShapeDtypeStruct((r, c), F32)] * 4,
        compiler_params=_params(("parallel", "parallel")),
    )(parts, w, m, v)


def _sum_parts(name, parts):
    n_parts, r, c = parts.shape

    def body(p_ref, o_ref):
        acc = p_ref[0]
        for s in range(1, n_parts):
            acc = acc + p_ref[s]
        o_ref[...] = acc

    return pl.pallas_call(body, name=name, out_shape=jax.ShapeDtypeStruct((r, c), F32))(parts)


BIG = ("w_in", "w_mem_kv", "w_conv_out", "w_fox_out", "w_mem_out", "w_out", "w_up", "w_down")
COLUMN_SPLIT = ("w_in", "w_conv_out", "w_fox_out", "w_mem_out", "w_up")
SMALL = ("norm1_g", "b_f", "conv_w", "fox_q_g", "fox_k_g", "mem_norm_g", "mem_q_g", "mem_k_g", "norm2_g")
WEIGHTS = ("norm1_g", "w_in", "b_f", "conv_w", "fox_q_g", "fox_k_g", "mem_norm_g", "w_mem_kv", "mem_q_g", "mem_k_g",
           "w_conv_out", "w_fox_out", "w_mem_out", "w_out", "norm2_g", "w_up", "w_down")


PACK_UNIT = 8 * LANES


def _pack(vectors):
    return jnp.concatenate([jnp.pad(vec, (0, -vec.shape[0] % PACK_UNIT)).reshape(-1, LANES) for vec in vectors], axis=0)


def _unpack(packed, sizes):
    out, row = [], 0
    for n in sizes:
        nr = -(-n // PACK_UNIT) * 8
        out.append(packed[row:row + nr].reshape(-1)[:n])
        row += nr
    return out


def kernel(x, mem, norm1_g, w_in, b_f, conv_w, fox_q_g, fox_k_g, mem_norm_g, w_mem_kv, mem_q_g, mem_k_g, w_conv_out, w_fox_out, w_mem_out, w_out, norm2_g, w_up, w_down, loss_target, m_norm1_g, m_w_in, m_b_f, m_conv_w, m_fox_q_g, m_fox_k_g, m_mem_norm_g, m_w_mem_kv, m_mem_q_g, m_mem_k_g, m_w_conv_out, m_w_fox_out, m_w_mem_out, m_w_out, m_norm2_g, m_w_up, m_w_down, v_norm1_g, v_w_in, v_b_f, v_conv_w, v_fox_q_g, v_fox_k_g, v_mem_norm_g, v_w_mem_kv, v_mem_q_g, v_mem_k_g, v_w_conv_out, v_w_fox_out, v_w_mem_out, v_w_out, v_norm2_g, v_w_up, v_w_down):
    args = dict(locals())
    wts = {n: args[n] for n in WEIGHTS}
    ms = {n: args["m_" + n] for n in WEIGHTS}
    vs = {n: args["v_" + n] for n in WEIGHTS}
    x_pos, y_pos, c_pos = _position()
    me = _index(x_pos, y_pos, c_pos)

    shards = {n: wts[n].astype(BF16) for n in BIG if n != "w_in"}
    rows_in = w_in.shape[1]
    shards["w_in"] = jnp.pad(w_in.T.astype(BF16), ((0, _padded_rows(rows_in) - rows_in), (0, 0)))
    small = {n: wts[n] for n in SMALL if n != "conv_w"}
    comm = {"shards": shards, "conv_w": conv_w, "c": c_pos.astype(jnp.int32).reshape(1)}

    loss, grad_x, parts, gs = _local_step(x[0], mem[0], loss_target[0], {}, small, comm)

    out_g, out_d, out_m, out_v = {}, {}, {}, {}
    for n in BIG:
        if n == "w_in":
            res = _adamw("adamw_" + n, parts[n], wts[n].T, ms[n].T, vs[n].T)
            out_g[n], out_d[n], out_m[n], out_v[n] = (r.T for r in res)
        else:
            out_g[n], out_d[n], out_m[n], out_v[n] = _adamw("adamw_" + n, parts[n], wts[n], ms[n], vs[n])

    small_sizes = [int(math.prod(gs[n].shape)) for n in SMALL] + [1]
    packed = _pack([gs[n].reshape(-1) for n in SMALL] + [loss.reshape(1)])
    gsum = _sum_parts("sum_small", _run_rider("exchange_small", _broadcast_rider([packed]))[0])
    *small_sums, loss_sum = _unpack(gsum, small_sizes)
    gsmall = dict(zip(SMALL, small_sums))
    cols = conv_w.shape[1]
    gsmall["conv_w"] = lax.dynamic_slice(gsmall["conv_w"].reshape(CONV_TAPS, -1), (0, me * cols), (CONV_TAPS, cols)).reshape(-1)
    pg, pw, pm, pv = (_pack([src[n].reshape(-1) for n in SMALL]) for src in (gsmall, wts, ms, vs))
    _, sd, sm, sv = _adamw("adamw_small", pg[None], pw, pm, pv)
    local_sizes = [int(math.prod(wts[n].shape)) for n in SMALL]
    for dst, src in ((out_d, sd), (out_m, sm), (out_v, sv)):
        for n, val in zip(SMALL, _unpack(src, local_sizes)):
            dst[n] = val.reshape(wts[n].shape)
    for n in SMALL:
        out_g[n] = gsmall[n].reshape(wts[n].shape)

    loss = loss_sum[0]
    return (loss, grad_x[None], *[out_g[n] for n in WEIGHTS], *[out_d[n] for n in WEIGHTS],
            *[out_m[n] for n in WEIGHTS], *[out_v[n] for n in WEIGHTS])
```
